```python
import jax, jax.numpy as jnp
from jax import lax
import numpy as np

D_MODEL = 1024
BATCH = 8
SEQ = 4096
DEPTH = 4

N_META = 16
BLOCK = 128
WINDOW = 128
ROPE_THETA = 10000.0
EPS = 1e-6
NEG = -1e30
SWA_HEADS = 8
SWA_KV_HEADS = 2
SWA_HEAD_DIM = 64
SWA_GROUP = SWA_HEADS // SWA_KV_HEADS
MLA_HEADS = 8
MLA_Q_RANK = 256
MLA_KV_RANK = 128
MLA_NOPE_DIM = 64
MLA_ROPE_DIM = 32
MLA_V_DIM = 64
MLA_QK_DIM = MLA_NOPE_DIM + MLA_ROPE_DIM
SWA_Q_W = SWA_HEADS * SWA_HEAD_DIM
SWA_KV_W = SWA_KV_HEADS * SWA_HEAD_DIM
MLA_OUT_W = MLA_HEADS * MLA_V_DIM
MIX_W = SWA_Q_W + MLA_OUT_W
IN_W = SWA_Q_W + 2 * SWA_KV_W + MLA_Q_RANK + MLA_KV_RANK + MLA_ROPE_DIM
D_FF = -(-8 * D_MODEL // (3 * 256)) * 256

kernel_name = "hymba_swa_sink_mla_hybrid"


def rmsnorm(x, g):
    xf = x.astype(jnp.float32)
    y = xf * lax.rsqrt(jnp.mean(xf * xf, axis=-1, keepdims=True) + EPS)
    return (y * g.astype(jnp.float32)).astype(x.dtype)


def rope(x, pos):
    d = x.shape[-1]
    inv = ROPE_THETA ** (-jnp.arange(0, d, 2, dtype=jnp.float32) / d)
    ang = pos[:, None] * inv[None, :]
    cos = jnp.cos(ang)[:, None, :]
    sin = jnp.sin(ang)[:, None, :]
    xf = x.astype(jnp.float32)
    x1, x2 = xf[..., : d // 2], xf[..., d // 2:]
    return jnp.concatenate([x1 * cos - x2 * sin, x2 * cos + x1 * sin], -1).astype(x.dtype)


def swa_sink_attention(q, k, v, sinks, key_valid):
    B, T, _, D = q.shape
    nb = T // BLOCK
    qb = q.reshape(B, nb, BLOCK, SWA_KV_HEADS, SWA_GROUP, D)
    kb = k.reshape(B, nb, BLOCK, SWA_KV_HEADS, D)
    vb = v.reshape(B, nb, BLOCK, SWA_KV_HEADS, D)
    prev = lambda a: jnp.concatenate([jnp.zeros_like(a[:, :1]), a[:, :-1]], axis=1)
    kw = jnp.concatenate([prev(kb), kb], axis=2)
    vw = jnp.concatenate([prev(vb), vb], axis=2)
    s = jnp.einsum("bnqhgd,bnkhd->bhgnqk", qb, kw,
                   preferred_element_type=jnp.float32) * (D ** -0.5)
    qpos = jnp.arange(T).reshape(nb, BLOCK)
    kpos = jnp.concatenate([qpos - BLOCK, qpos], axis=1)
    kv_ok = key_valid.reshape(nb, BLOCK)
    kv_ok = jnp.concatenate(
        [jnp.concatenate([jnp.zeros((1, BLOCK), bool), kv_ok[:-1]], 0), kv_ok], 1)
    diff = qpos[:, :, None] - kpos[:, None, :]
    mask = (diff >= 0) & (diff < WINDOW) & kv_ok[:, None, :]
    s = jnp.where(mask, s, NEG)
    sink = jnp.broadcast_to(
        sinks.astype(jnp.float32).reshape(SWA_KV_HEADS, SWA_GROUP)[None, :, :, None, None, None],
        s.shape[:-1] + (1,))
    p = jax.nn.softmax(jnp.concatenate([s, sink], axis=-1), axis=-1)[..., :-1]
    o = jnp.einsum("bhgnqk,bnkhd->bnqhgd", p.astype(v.dtype), vw)
    return o.reshape(B, T, SWA_HEADS * D)


def causal_block_attention(q, k, v, key_valid):
    B, T, H, dqk = q.shape
    nb = T // BLOCK
    scale = dqk ** -0.5
    qb = jnp.moveaxis(q.reshape(B, nb, BLOCK, H, dqk), 1, 0)
    kpos = jnp.arange(T)

    def one_block(args):
        qblk, i = args
        s = jnp.einsum("bqhd,bkhd->bhqk", qblk, k,
                       preferred_element_type=jnp.float32) * scale
        qpos = i * BLOCK + jnp.arange(BLOCK)
        mask = (kpos[None, :] <= qpos[:, None]) & key_valid[None, :]
        p = jax.nn.softmax(jnp.where(mask, s, NEG), axis=-1)
        return jnp.einsum("bhqk,bkhd->bqhd", p.astype(v.dtype), v)

    o = lax.map(one_block, (qb, jnp.arange(nb)))
    return jnp.moveaxis(o, 0, 1).reshape(B, T, H * v.shape[-1])


def _fwd_setup_inputs(seed: int = 0) -> dict:
    key = jax.random.key(seed)
    ks = jax.random.split(key, 20)
    f32 = jnp.float32
    nrm = lambda k, shape, scale: jax.random.normal(k, shape, f32) * scale
    gain = lambda k, shape: 1.0 + 0.02 * jax.random.normal(k, shape, f32)
    return {
        "x": nrm(ks[0], (BATCH, SEQ, D_MODEL), 1.0),
        "meta_tokens": nrm(ks[1], (N_META, D_MODEL), 1.0),
        "attn_norm": gain(ks[2], (DEPTH, D_MODEL)),
        "w_in": nrm(ks[3], (DEPTH, D_MODEL, IN_W), D_MODEL ** -0.5),
        "q_norm": gain(ks[4], (DEPTH, MLA_Q_RANK)),
        "w_q_up": nrm(ks[5], (DEPTH, MLA_Q_RANK, MLA_HEADS * MLA_QK_DIM), MLA_Q_RANK ** -0.5),
        "kv_norm": gain(ks[6], (DEPTH, MLA_KV_RANK)),
        "w_kv_up": nrm(ks[7], (DEPTH, MLA_KV_RANK, MLA_HEADS * (MLA_NOPE_DIM + MLA_V_DIM)),
                        MLA_KV_RANK ** -0.5),
        "sinks": nrm(ks[8], (DEPTH, SWA_HEADS), 1.0),
        "out_norm_swa": gain(ks[9], (DEPTH, SWA_Q_W)),
        "out_norm_mla": gain(ks[10], (DEPTH, MLA_OUT_W)),
        "w_o": nrm(ks[11], (DEPTH, MIX_W, D_MODEL), MIX_W ** -0.5),
        "ffn_norm": gain(ks[12], (DEPTH, D_MODEL)),
        "w_gate": nrm(ks[13], (DEPTH, D_MODEL, D_FF), D_MODEL ** -0.5),
        "w_up": nrm(ks[14], (DEPTH, D_MODEL, D_FF), D_MODEL ** -0.5),
        "w_down": nrm(ks[15], (DEPTH, D_FF, D_MODEL), D_FF ** -0.5),
        "final_norm": gain(ks[16], (D_MODEL,)),
    }


def _fwd_reference(x, meta_tokens, attn_norm, w_in, q_norm, w_q_up, kv_norm, w_kv_up, sinks,
              out_norm_swa, out_norm_mla, w_o, ffn_norm, w_gate, w_up, w_down, final_norm):
    B, S, D = x.shape
    front = (-N_META) % BLOCK
    back = (-S) % BLOCK
    T = front + N_META + S + back
    h = jnp.concatenate([
        jnp.zeros((B, front, D), x.dtype),
        jnp.broadcast_to(meta_tokens.astype(x.dtype)[None], (B, N_META, D)),
        x,
        jnp.zeros((B, back, D), x.dtype)], axis=1)
    idx = jnp.arange(T)
    key_valid = (idx >= front) & (idx < front + N_META + S)
    pos = (idx - front).astype(jnp.float32)

    o1 = SWA_Q_W
    o2 = o1 + SWA_KV_W
    o3 = o2 + SWA_KV_W
    o4 = o3 + MLA_Q_RANK
    o5 = o4 + MLA_KV_RANK
    for l in range(DEPTH):
        u = rmsnorm(h, attn_norm[l])
        proj = u @ w_in[l]
        q_a = rope(proj[..., :o1].reshape(B, T, SWA_HEADS, SWA_HEAD_DIM), pos)
        k_a = rope(proj[..., o1:o2].reshape(B, T, SWA_KV_HEADS, SWA_HEAD_DIM), pos)
        v_a = proj[..., o2:o3].reshape(B, T, SWA_KV_HEADS, SWA_HEAD_DIM)
        out_a = swa_sink_attention(q_a, k_a, v_a, sinks[l], key_valid)
        q_b = (rmsnorm(proj[..., o3:o4], q_norm[l]) @ w_q_up[l]).reshape(
            B, T, MLA_HEADS, MLA_QK_DIM)
        kv_b = (rmsnorm(proj[..., o4:o5], kv_norm[l]) @ w_kv_up[l]).reshape(
            B, T, MLA_HEADS, MLA_NOPE_DIM + MLA_V_DIM)
        k_rope = rope(proj[..., o5:][:, :, None, :], pos)
        q_full = jnp.concatenate(
            [q_b[..., :MLA_NOPE_DIM], rope(q_b[..., MLA_NOPE_DIM:], pos)], axis=-1)
        k_full = jnp.concatenate(
            [kv_b[..., :MLA_NOPE_DIM],
             jnp.broadcast_to(k_rope, (B, T, MLA_HEADS, MLA_ROPE_DIM))], axis=-1)
        v_b = kv_b[..., MLA_NOPE_DIM:]
        out_b = causal_block_attention(q_full, k_full, v_b, key_valid)
        mix = jnp.concatenate([rmsnorm(out_a, out_norm_swa[l]),
                               rmsnorm(out_b, out_norm_mla[l])], axis=-1)
        h = h + mix @ w_o[l]
        u = rmsnorm(h, ffn_norm[l])
        h = h + (jax.nn.silu(u @ w_gate[l]) * (u @ w_up[l])) @ w_down[l]

    h = rmsnorm(h, final_norm)
    start = front + N_META
    return h[:, start:start + S]


import jax as _jax
import jax.numpy as _jnp

TWIN_FORMAT = 'train_step'
FWD_PARAMS = ['x', 'meta_tokens', 'attn_norm', 'w_in', 'q_norm', 'w_q_up', 'kv_norm', 'w_kv_up', 'sinks', 'out_norm_swa', 'out_norm_mla', 'w_o', 'ffn_norm', 'w_gate', 'w_up', 'w_down', 'final_norm']
TWIN_WEIGHTS = ['meta_tokens', 'attn_norm', 'w_in', 'q_norm', 'w_q_up', 'kv_norm', 'w_kv_up', 'sinks', 'out_norm_swa', 'out_norm_mla', 'w_o', 'ffn_norm', 'w_gate', 'w_up', 'w_down', 'final_norm']
TWIN_DIFF_INPUT = 'x'
TWIN_INPUTS = ['x', 'meta_tokens', 'attn_norm', 'w_in', 'q_norm', 'w_q_up', 'kv_norm', 'w_kv_up', 'sinks', 'out_norm_swa', 'out_norm_mla', 'w_o', 'ffn_norm', 'w_gate', 'w_up', 'w_down', 'final_norm', 'loss_target', 'm_meta_tokens', 'm_attn_norm', 'm_w_in', 'm_q_norm', 'm_w_q_up', 'm_kv_norm', 'm_w_kv_up', 'm_sinks', 'm_out_norm_swa', 'm_out_norm_mla', 'm_w_o', 'm_ffn_norm', 'm_w_gate', 'm_w_up', 'm_w_down', 'm_final_norm', 'v_meta_tokens', 'v_attn_norm', 'v_w_in', 'v_q_norm', 'v_w_q_up', 'v_kv_norm', 'v_w_kv_up', 'v_sinks', 'v_out_norm_swa', 'v_out_norm_mla', 'v_w_o', 'v_ffn_norm', 'v_w_gate', 'v_w_up', 'v_w_down', 'v_final_norm']
TWIN_OUTPUTS = ['loss', 'grad_x', 'grad_meta_tokens', 'grad_attn_norm', 'grad_w_in', 'grad_q_norm', 'grad_w_q_up', 'grad_kv_norm', 'grad_w_kv_up', 'grad_sinks', 'grad_out_norm_swa', 'grad_out_norm_mla', 'grad_w_o', 'grad_ffn_norm', 'grad_w_gate', 'grad_w_up', 'grad_w_down', 'grad_final_norm', 'delta_meta_tokens', 'delta_attn_norm', 'delta_w_in', 'delta_q_norm', 'delta_w_q_up', 'delta_kv_norm', 'delta_w_kv_up', 'delta_sinks', 'delta_out_norm_swa', 'delta_out_norm_mla', 'delta_w_o', 'delta_ffn_norm', 'delta_w_gate', 'delta_w_up', 'delta_w_down', 'delta_final_norm', 'new_m_meta_tokens', 'new_m_attn_norm', 'new_m_w_in', 'new_m_q_norm', 'new_m_w_q_up', 'new_m_kv_norm', 'new_m_w_kv_up', 'new_m_sinks', 'new_m_out_norm_swa', 'new_m_out_norm_mla', 'new_m_w_o', 'new_m_ffn_norm', 'new_m_w_gate', 'new_m_w_up', 'new_m_w_down', 'new_m_final_norm', 'new_v_meta_tokens', 'new_v_attn_norm', 'new_v_w_in', 'new_v_q_norm', 'new_v_w_q_up', 'new_v_kv_norm', 'new_v_w_kv_up', 'new_v_sinks', 'new_v_out_norm_swa', 'new_v_out_norm_mla', 'new_v_w_o', 'new_v_ffn_norm', 'new_v_w_gate', 'new_v_w_up', 'new_v_w_down', 'new_v_final_norm']
TWIN_LEAF_KINDS = {'loss': 'loss', 'grad_x': 'grad_x', 'grad_meta_tokens': 'grad_w', 'grad_attn_norm': 'grad_w', 'grad_w_in': 'grad_w', 'grad_q_norm': 'grad_w', 'grad_w_q_up': 'grad_w', 'grad_kv_norm': 'grad_w', 'grad_w_kv_up': 'grad_w', 'grad_sinks': 'grad_w', 'grad_out_norm_swa': 'grad_w', 'grad_out_norm_mla': 'grad_w', 'grad_w_o': 'grad_w', 'grad_ffn_norm': 'grad_w', 'grad_w_gate': 'grad_w', 'grad_w_up': 'grad_w', 'grad_w_down': 'grad_w', 'grad_final_norm': 'grad_w', 'delta_meta_tokens': 'delta_w', 'delta_attn_norm': 'delta_w', 'delta_w_in': 'delta_w', 'delta_q_norm': 'delta_w', 'delta_w_q_up': 'delta_w', 'delta_kv_norm': 'delta_w', 'delta_w_kv_up': 'delta_w', 'delta_sinks': 'delta_w', 'delta_out_norm_swa': 'delta_w', 'delta_out_norm_mla': 'delta_w', 'delta_w_o': 'delta_w', 'delta_ffn_norm': 'delta_w', 'delta_w_gate': 'delta_w', 'delta_w_up': 'delta_w', 'delta_w_down': 'delta_w', 'delta_final_norm': 'delta_w', 'new_m_meta_tokens': 'new_m', 'new_m_attn_norm': 'new_m', 'new_m_w_in': 'new_m', 'new_m_q_norm': 'new_m', 'new_m_w_q_up': 'new_m', 'new_m_kv_norm': 'new_m', 'new_m_w_kv_up': 'new_m', 'new_m_sinks': 'new_m', 'new_m_out_norm_swa': 'new_m', 'new_m_out_norm_mla': 'new_m', 'new_m_w_o': 'new_m', 'new_m_ffn_norm': 'new_m', 'new_m_w_gate': 'new_m', 'new_m_w_up': 'new_m', 'new_m_w_down': 'new_m', 'new_m_final_norm': 'new_m', 'new_v_meta_tokens': 'new_v', 'new_v_attn_norm': 'new_v', 'new_v_w_in': 'new_v', 'new_v_q_norm': 'new_v', 'new_v_w_q_up': 'new_v', 'new_v_kv_norm': 'new_v', 'new_v_w_kv_up': 'new_v', 'new_v_sinks': 'new_v', 'new_v_out_norm_swa': 'new_v', 'new_v_out_norm_mla': 'new_v', 'new_v_w_o': 'new_v', 'new_v_ffn_norm': 'new_v', 'new_v_w_gate': 'new_v', 'new_v_w_up': 'new_v', 'new_v_w_down': 'new_v', 'new_v_final_norm': 'new_v'}


def _forward(args):
    return _fwd_reference(*[args[k] for k in FWD_PARAMS])


def _output_shape():
    out = _jax.eval_shape(lambda: _forward(_fwd_setup_inputs(0)))
    return out.shape, out.dtype

N_MICROBATCH = 1
ADAM_LR = 0.001
ADAM_B1 = 0.9
ADAM_B2 = 0.999
ADAM_EPS = 1e-08
ADAM_WD = 0.01
ADAM_STEP = 10
PER_EXAMPLE_BATCH_AXIS = {'x': 0, 'loss_target': 0}
SHARED_INPUTS = []
_WEIGHT_DTYPES = {'meta_tokens': _jnp.float32, 'attn_norm': _jnp.float32, 'w_in': _jnp.float32, 'q_norm': _jnp.float32, 'w_q_up': _jnp.float32, 'kv_norm': _jnp.float32, 'w_kv_up': _jnp.float32, 'sinks': _jnp.float32, 'out_norm_swa': _jnp.float32, 'out_norm_mla': _jnp.float32, 'w_o': _jnp.float32, 'ffn_norm': _jnp.float32, 'w_gate': _jnp.float32, 'w_up': _jnp.float32, 'w_down': _jnp.float32, 'final_norm': _jnp.float32}
MOMENT_SCALE = {'meta_tokens': 3.831638e-02, 'attn_norm': 1.687014e-01, 'w_in': 1.552999e-01, 'q_norm': 9.556580e-02, 'w_q_up': 5.098123e-02, 'kv_norm': 3.412830e-01, 'w_kv_up': 1.297236e-01, 'sinks': 4.369703e-02, 'out_norm_swa': 1.425530e-01, 'out_norm_mla': 1.716204e-01, 'w_o': 1.519608e-01, 'ffn_norm': 7.699937e-02, 'w_gate': 3.356081e-02, 'w_up': 3.348955e-02, 'w_down': 5.563437e-02, 'final_norm': 3.125626e+01}


def _to_microbatches(a, axis):
    t = _jnp.moveaxis(a, axis, 0)
    t = t.reshape((N_MICROBATCH, t.shape[0] // N_MICROBATCH) + t.shape[1:])
    return _jnp.moveaxis(t, 1, axis + 1)


def setup_inputs(seed: int = 0) -> dict:
    inp = _fwd_setup_inputs(seed)
    key = _jax.random.fold_in(_jax.random.key(seed), 7919)
    shape, _ = _output_shape()
    out = dict(inp)
    out["loss_target"] = _jax.random.normal(_jax.random.fold_in(key, 0), shape, _jnp.float32)
    for i, name in enumerate(TWIN_WEIGHTS):
        w = inp[name].astype(_jnp.float32)
        if MOMENT_SCALE is None:
            s = _jnp.sqrt(_jnp.mean(_jnp.square(w)) + 1e-30)
        else:
            s = MOMENT_SCALE[name]
        km, kv = _jax.random.split(_jax.random.fold_in(key, i + 1))
        out[name] = w
        out["m_" + name] = s * _jax.random.normal(km, w.shape, _jnp.float32)
        out["v_" + name] = (s * s) * _jax.random.uniform(kv, w.shape, _jnp.float32, 0.5, 1.5)
    if N_MICROBATCH > 1:
        for name, axis in PER_EXAMPLE_BATCH_AXIS.items():
            out[name] = _to_microbatches(out[name], axis)
    return {'x': out['x'], 'meta_tokens': out['meta_tokens'], 'attn_norm': out['attn_norm'], 'w_in': out['w_in'], 'q_norm': out['q_norm'], 'w_q_up': out['w_q_up'], 'kv_norm': out['kv_norm'], 'w_kv_up': out['w_kv_up'], 'sinks': out['sinks'], 'out_norm_swa': out['out_norm_swa'], 'out_norm_mla': out['out_norm_mla'], 'w_o': out['w_o'], 'ffn_norm': out['ffn_norm'], 'w_gate': out['w_gate'], 'w_up': out['w_up'], 'w_down': out['w_down'], 'final_norm': out['final_norm'], 'loss_target': out['loss_target'], 'm_meta_tokens': out['m_meta_tokens'], 'm_attn_norm': out['m_attn_norm'], 'm_w_in': out['m_w_in'], 'm_q_norm': out['m_q_norm'], 'm_w_q_up': out['m_w_q_up'], 'm_kv_norm': out['m_kv_norm'], 'm_w_kv_up': out['m_w_kv_up'], 'm_sinks': out['m_sinks'], 'm_out_norm_swa': out['m_out_norm_swa'], 'm_out_norm_mla': out['m_out_norm_mla'], 'm_w_o': out['m_w_o'], 'm_ffn_norm': out['m_ffn_norm'], 'm_w_gate': out['m_w_gate'], 'm_w_up': out['m_w_up'], 'm_w_down': out['m_w_down'], 'm_final_norm': out['m_final_norm'], 'v_meta_tokens': out['v_meta_tokens'], 'v_attn_norm': out['v_attn_norm'], 'v_w_in': out['v_w_in'], 'v_q_norm': out['v_q_norm'], 'v_w_q_up': out['v_w_q_up'], 'v_kv_norm': out['v_kv_norm'], 'v_w_kv_up': out['v_w_kv_up'], 'v_sinks': out['v_sinks'], 'v_out_norm_swa': out['v_out_norm_swa'], 'v_out_norm_mla': out['v_out_norm_mla'], 'v_w_o': out['v_w_o'], 'v_ffn_norm': out['v_ffn_norm'], 'v_w_gate': out['v_w_gate'], 'v_w_up': out['v_w_up'], 'v_w_down': out['v_w_down'], 'v_final_norm': out['v_final_norm']}


def _loss(weights, diff, rest, loss_target):
    with _jax.named_scope("forward"):
        args = {**rest, TWIN_DIFF_INPUT: diff, **{k: w.astype(_WEIGHT_DTYPES[k]) for k, w in weights.items()}}
        y = _forward(args)
    with _jax.named_scope("loss_head"):
        err = _jnp.square(y.astype(_jnp.float32) - loss_target)
        return 0.5 * _jnp.sum(_jnp.mean(err, axis=-1)) if err.ndim else 0.5 * err


def _adamw(w, g, m, v):
    m = ADAM_B1 * m + (1.0 - ADAM_B1) * g
    v = ADAM_B2 * v + (1.0 - ADAM_B2) * _jnp.square(g)
    m_hat = m / (1.0 - ADAM_B1 ** ADAM_STEP)
    v_hat = v / (1.0 - ADAM_B2 ** ADAM_STEP)
    delta = -ADAM_LR * (m_hat / (_jnp.sqrt(v_hat) + ADAM_EPS) + ADAM_WD * w)
    return delta, m, v


def reference(x, meta_tokens, attn_norm, w_in, q_norm, w_q_up, kv_norm, w_kv_up, sinks, out_norm_swa, out_norm_mla, w_o, ffn_norm, w_gate, w_up, w_down, final_norm, loss_target, m_meta_tokens, m_attn_norm, m_w_in, m_q_norm, m_w_q_up, m_kv_norm, m_w_kv_up, m_sinks, m_out_norm_swa, m_out_norm_mla, m_w_o, m_ffn_norm, m_w_gate, m_w_up, m_w_down, m_final_norm, v_meta_tokens, v_attn_norm, v_w_in, v_q_norm, v_w_q_up, v_kv_norm, v_w_kv_up, v_sinks, v_out_norm_swa, v_out_norm_mla, v_w_o, v_ffn_norm, v_w_gate, v_w_up, v_w_down, v_final_norm):
    given = dict(x=x, meta_tokens=meta_tokens, attn_norm=attn_norm, w_in=w_in, q_norm=q_norm, w_q_up=w_q_up, kv_norm=kv_norm, w_kv_up=w_kv_up, sinks=sinks, out_norm_swa=out_norm_swa, out_norm_mla=out_norm_mla, w_o=w_o, ffn_norm=ffn_norm, w_gate=w_gate, w_up=w_up, w_down=w_down, final_norm=final_norm, loss_target=loss_target, m_meta_tokens=m_meta_tokens, m_attn_norm=m_attn_norm, m_w_in=m_w_in, m_q_norm=m_q_norm, m_w_q_up=m_w_q_up, m_kv_norm=m_kv_norm, m_w_kv_up=m_w_kv_up, m_sinks=m_sinks, m_out_norm_swa=m_out_norm_swa, m_out_norm_mla=m_out_norm_mla, m_w_o=m_w_o, m_ffn_norm=m_ffn_norm, m_w_gate=m_w_gate, m_w_up=m_w_up, m_w_down=m_w_down, m_final_norm=m_final_norm, v_meta_tokens=v_meta_tokens, v_attn_norm=v_attn_norm, v_w_in=v_w_in, v_q_norm=v_q_norm, v_w_q_up=v_w_q_up, v_kv_norm=v_kv_norm, v_w_kv_up=v_w_kv_up, v_sinks=v_sinks, v_out_norm_swa=v_out_norm_swa, v_out_norm_mla=v_out_norm_mla, v_w_o=v_w_o, v_ffn_norm=v_ffn_norm, v_w_gate=v_w_gate, v_w_up=v_w_up, v_w_down=v_w_down, v_final_norm=v_final_norm)
    weights = {n: given[n] for n in TWIN_WEIGHTS}
    shared = {n: given[n] for n in SHARED_INPUTS}
    per_example = {n: given[n] for n in ['x']}
    grad_fn = _jax.value_and_grad(_loss, argnums=(0, 1))

    def one_microbatch(ex, loss_target):
        ex = dict(ex)
        diff = ex.pop(TWIN_DIFF_INPUT)
        return grad_fn(weights, diff, {**shared, **ex}, loss_target)

    if N_MICROBATCH == 1:
        loss, (grad_w, grad_x) = one_microbatch(per_example, given["loss_target"])
    else:
        def body(carry, xs):
            loss_sum, grad_sum = carry
            l_k, (gw_k, gx_k) = one_microbatch(xs[0], xs[1])
            with _jax.named_scope("update"):
                return (loss_sum + l_k, _jax.tree.map(_jnp.add, grad_sum, gw_k)), gx_k

        init = (_jnp.zeros((), _jnp.float32), _jax.tree.map(_jnp.zeros_like, weights))
        (loss, grad_w), grad_x = _jax.lax.scan(body, init, (per_example, given["loss_target"]))
    with _jax.named_scope("update"):
        delta_w, new_m, new_v = {}, {}, {}
        for n in TWIN_WEIGHTS:
            delta_w[n], new_m[n], new_v[n] = _adamw(weights[n], grad_w[n], given["m_" + n], given["v_" + n])
    return (loss, grad_x, *[grad_w[n] for n in TWIN_WEIGHTS], *[delta_w[n] for n in TWIN_WEIGHTS],
            *[new_m[n] for n in TWIN_WEIGHTS], *[new_v[n] for n in TWIN_WEIGHTS])
```

```python
import functools

import jax
import jax.numpy as jnp
from jax import lax
from jax.experimental import pallas as pl
from jax.experimental.pallas import tpu as pltpu

F32 = jnp.float32
BF16 = jnp.bfloat16

D_MODEL = 1024
N_META = 16
BLOCK = 128
FRONT = (-N_META) % BLOCK
ROPE_THETA = 10000.0
EPS = 1e-6
NEG = -1e30
SWA_HEADS = 8
SWA_KV_HEADS = 2
SWA_GROUP = SWA_HEADS // SWA_KV_HEADS
SWA_HEAD_DIM = 64
MLA_HEADS = 8
MLA_Q_RANK = 256
MLA_KV_RANK = 128
MLA_NOPE_DIM = 64
MLA_ROPE_DIM = 32
MLA_V_DIM = 64
MLA_QK_DIM = MLA_NOPE_DIM + MLA_ROPE_DIM
SWA_Q_W = SWA_HEADS * SWA_HEAD_DIM
SWA_KV_W = SWA_KV_HEADS * SWA_HEAD_DIM
MLA_OUT_W = MLA_HEADS * MLA_V_DIM
SCALE_A = SWA_HEAD_DIM ** -0.5
SCALE_B = MLA_QK_DIM ** -0.5
ADAM_LR = 0.001
ADAM_B1 = 0.9
ADAM_B2 = 0.999
ADAM_EPS = 1e-08
ADAM_WD = 0.01
ADAM_STEP = 10

LANE = 128
N_DEV = 8
HP = 8 * LANE
PO_QA, PO_KA, PO_VA = 0, HP, HP + 2 * LANE
PO_CQ = PO_VA + 2 * LANE
PO_CKV = PO_CQ + MLA_Q_RANK
PO_KR = PO_CKV + MLA_KV_RANK
PW_IN = PO_KR + LANE
N_TAB = 7
VMEM_LIMIT = 56 * 2 ** 20

NT = (((1,), (1,)), ((), ()))
TN = (((0,), (0,)), ((), ()))


def _tile(t):
    return 384 if t % 384 == 0 else 128


def _params(*sem):
    return pltpu.CompilerParams(dimension_semantics=sem, vmem_limit_bytes=VMEM_LIMIT)


def _row(tm, n):
    return pl.BlockSpec((tm, n), lambda i: (i, 0))


def _const(shape):
    return pl.BlockSpec(shape, lambda i: (0,) * len(shape))


def _dot(a, b):
    return jnp.dot(a, b, preferred_element_type=F32)


def _dot_nt(a, b):
    return lax.dot_general(a, b, NT, preferred_element_type=F32)


def _dot_tn(a, b):
    return lax.dot_general(a, b, TN, preferred_element_type=F32)


def _rope(x, c, s1, s2, shift):
    return x * c + pltpu.roll(x, LANE - shift, 1) * s1 + pltpu.roll(x, shift, 1) * s2


def _rope_t(dy, c, s1, s2, shift):
    return dy * c + pltpu.roll(dy * s1, shift, 1) + pltpu.roll(dy * s2, LANE - shift, 1)


def _rms_r(x, n):
    return lax.rsqrt(jnp.sum(x * x, axis=-1, keepdims=True) * (1.0 / n) + EPS)


def _rms_bwd(x, g, dy, n):
    r = _rms_r(x, n)
    xh = x * r
    dxh = dy * g
    dx = r * (dxh - xh * (jnp.sum(dxh * xh, axis=-1, keepdims=True) * (1.0 / n)))
    return dx, jnp.sum(dy * xh, axis=0, keepdims=True)


def _acc(ref, val, first):
    @pl.when(first)
    def _():
        ref[...] = val

    @pl.when(jnp.logical_not(first))
    def _():
        ref[...] += val


def _tabs(tab_ref):
    return [tab_ref[:, LANE * i:LANE * (i + 1)] for i in range(N_TAB)]


def _pre_fwd(h, g1, win, gq, wqu, gkv, wkv, tabs):
    t = h.shape[0]
    tm = _tile(t)

    def body(h_ref, g1_ref, win_ref, gq_ref, wqu_ref, gkv_ref, wkv_ref, tab_ref,
             u_ref, qa_ref, ka_ref, va_ref, cq_ref, ckv_ref, qn_ref, kvn_ref, qb_ref, kf_ref, vb_ref):
        ca, sa1, sa2, cb, sb1, sb2, ck = _tabs(tab_ref)
        hv = h_ref[...]
        u = (hv * _rms_r(hv, D_MODEL) * g1_ref[...]).astype(BF16)
        u_ref[...] = u
        p = _dot(u, win_ref[...])
        for c in range(SWA_HEADS):
            sl = slice(LANE * c, LANE * (c + 1))
            qa_ref[:, sl] = _rope(p[:, PO_QA + LANE * c:PO_QA + LANE * (c + 1)], ca, sa1, sa2, 32).astype(BF16)
        for c in range(SWA_KV_HEADS):
            sl = slice(LANE * c, LANE * (c + 1))
            ka_ref[:, sl] = _rope(p[:, PO_KA + LANE * c:PO_KA + LANE * (c + 1)], ca, sa1, sa2, 32).astype(BF16)
        va_ref[...] = p[:, PO_VA:PO_CQ].astype(BF16)
        cq = p[:, PO_CQ:PO_CKV]
        ckv = p[:, PO_CKV:PO_KR]
        cq_ref[...] = cq
        ckv_ref[...] = ckv
        qn = (cq * _rms_r(cq, MLA_Q_RANK) * gq_ref[...]).astype(BF16)
        qn_ref[...] = qn
        qb = _dot(qn, wqu_ref[...])
        kvn = (ckv * _rms_r(ckv, MLA_KV_RANK) * gkv_ref[...]).astype(BF16)
        kvn_ref[...] = kvn
        kv = _dot(kvn, wkv_ref[...])
        kr = _rope(p[:, PO_KR:PW_IN], ck, sb1, sb2, 16)
        for c in range(MLA_HEADS):
            sl = slice(LANE * c, LANE * (c + 1))
            qb_ref[:, sl] = _rope(qb[:, sl], cb, sb1, sb2, 16).astype(BF16)
            kf_ref[:, sl] = (kv[:, sl] + kr).astype(BF16)
        vb_ref[...] = kv[:, HP:].astype(BF16)

    widths = [(D_MODEL, BF16), (HP, BF16), (2 * LANE, BF16), (2 * LANE, BF16), (MLA_Q_RANK, F32),
              (MLA_KV_RANK, F32), (MLA_Q_RANK, BF16), (MLA_KV_RANK, BF16), (HP, BF16), (HP, BF16), (HP, BF16)]
    return pl.pallas_call(
        body, name="pre_fwd", grid=(t // tm,),
        in_specs=[_row(tm, D_MODEL), _const(g1.shape), _const(win.shape), _const(gq.shape), _const(wqu.shape),
                  _const(gkv.shape), _const(wkv.shape), _row(tm, N_TAB * LANE)],
        out_specs=[_row(tm, w) for w, _ in widths],
        out_shape=[jax.ShapeDtypeStruct((t, w), d) for w, d in widths],
        compiler_params=_params("parallel"),
    )(h, g1, win, gq, wqu, gkv, wkv, tabs)


def _swa_probs(qh, k2, sink, mask):
    s = jnp.where(mask, _dot_nt(qh, k2) * SCALE_A, NEG)
    m = jnp.maximum(jnp.max(s, axis=1, keepdims=True), sink)
    e = jnp.exp(s - m)
    es = jnp.exp(sink - m)
    inv = 1.0 / (jnp.sum(e, axis=1, keepdims=True) + es)
    return e * inv, es * inv


def _swa_mask(n):
    row = lax.broadcasted_iota(jnp.int32, (BLOCK, 2 * BLOCK), 0)
    col = lax.broadcasted_iota(jnp.int32, (BLOCK, 2 * BLOCK), 1)
    return (col > row) & (col <= row + BLOCK) & (col + (n - 1) * BLOCK >= FRONT)


def _swa_specs():
    prev = lambda n: (jnp.maximum(n - 1, 0), 0)
    cur = lambda n: (n, 0)
    kv = (BLOCK, 2 * LANE)
    return [pl.BlockSpec(memory_space=pltpu.SMEM), pl.BlockSpec((BLOCK, HP), cur),
            pl.BlockSpec(kv, prev), pl.BlockSpec(kv, cur), pl.BlockSpec(kv, prev), pl.BlockSpec(kv, cur)]


def _swa_fwd(sinks, q, k, v):
    t = q.shape[0]

    def body(sink_ref, q_ref, kp_ref, kc_ref, vp_ref, vc_ref, o_ref):
        mask = _swa_mask(pl.program_id(0))
        for j in range(SWA_KV_HEADS):
            sl = slice(LANE * j, LANE * (j + 1))
            k2 = jnp.concatenate([kp_ref[:, sl], kc_ref[:, sl]], axis=0)
            v2 = jnp.concatenate([vp_ref[:, sl], vc_ref[:, sl]], axis=0)
            for g in range(SWA_GROUP):
                hd = SWA_GROUP * j + g
                hs = slice(LANE * hd, LANE * (hd + 1))
                p, _ = _swa_probs(q_ref[:, hs], k2, sink_ref[0, hd], mask)
                o_ref[:, hs] = _dot(p.astype(BF16), v2)

    return pl.pallas_call(
        body, name="swa_fwd", grid=(t // BLOCK,),
        in_specs=_swa_specs(),
        out_specs=pl.BlockSpec((BLOCK, HP), lambda n: (n, 0)),
        out_shape=jax.ShapeDtypeStruct((t, HP), F32),
        compiler_params=_params("parallel"),
    )(sinks, q, k, k, v, v)


def _causal_mask(q0, k0, tq, tk, transposed):
    if transposed:
        key = k0 + lax.broadcasted_iota(jnp.int32, (tk, tq), 0)
        qry = q0 + lax.broadcasted_iota(jnp.int32, (tk, tq), 1)
    else:
        qry = q0 + lax.broadcasted_iota(jnp.int32, (tq, tk), 0)
        key = k0 + lax.broadcasted_iota(jnp.int32, (tq, tk), 1)
    return (key <= qry) & (key >= FRONT)


def _mla_fwd(q, k, v):
    t = q.shape[0]
    tq = _tile(t)
    nq = t // tq

    def body(q_ref, k_ref, v_ref, o_ref, lse_ref):
        i = pl.program_id(1)
        qv = q_ref[...]

        def step(j, carry):
            m, l, acc = carry
            off = pl.multiple_of(j * tq, tq)
            kj = k_ref[pl.ds(off, tq), :]
            vj = v_ref[pl.ds(off, tq), :]
            s = jnp.where(_causal_mask(i * tq, j * tq, tq, tq, False), _dot_nt(qv, kj) * SCALE_B, NEG)
            mn = jnp.maximum(m, jnp.max(s, axis=1, keepdims=True))
            a = jnp.exp(m - mn)
            p = jnp.exp(s - mn)
            return mn, a * l + jnp.sum(p, axis=1, keepdims=True), a * acc + _dot(p.astype(BF16), vj)

        init = (jnp.full((tq, 1), NEG, F32), jnp.zeros((tq, 1), F32), jnp.zeros((tq, LANE), F32))
        m, l, acc = lax.fori_loop(0, i + 1, step, init)
        o_ref[...] = acc / l
        lse_ref[...] = jnp.broadcast_to(m + jnp.log(l), (tq, LANE))

    blk = pl.BlockSpec((tq, LANE), lambda h, i: (i, h))
    full = pl.BlockSpec((t, LANE), lambda h, i: (0, h))
    return pl.pallas_call(
        body, name="mla_fwd", grid=(MLA_HEADS, nq),
        in_specs=[blk, full, full], out_specs=[blk, blk],
        out_shape=[jax.ShapeDtypeStruct((t, HP), F32)] * 2,
        compiler_params=_params("parallel", "parallel"),
    )(q, k, v)


def _mix_fwd(h, oa, ob, ga, gb, wo, g2):
    t = h.shape[0]
    tm = _tile(t)

    def body(h_ref, oa_ref, ob_ref, ga_ref, gb_ref, wo_ref, g2_ref, h2_ref, mix_ref, u2_ref):
        oa_v = oa_ref[...]
        ob_v = ob_ref[...]
        na = (oa_v * _rms_r(oa_v, SWA_Q_W) * ga_ref[...]).astype(BF16)
        nb = (ob_v * _rms_r(ob_v, MLA_OUT_W) * gb_ref[...]).astype(BF16)
        mix_ref[:, :HP] = na
        mix_ref[:, HP:] = nb
        h2 = h_ref[...] + _dot(na, wo_ref[:HP, :]) + _dot(nb, wo_ref[HP:, :])
        h2_ref[...] = h2
        u2_ref[...] = (h2 * _rms_r(h2, D_MODEL) * g2_ref[...]).astype(BF16)

    return pl.pallas_call(
        body, name="mix_fwd", grid=(t // tm,),
        in_specs=[_row(tm, D_MODEL), _row(tm, HP), _row(tm, HP), _const(ga.shape), _const(gb.shape),
                  _const(wo.shape), _const(g2.shape)],
        out_specs=[_row(tm, D_MODEL), _row(tm, 2 * HP), _row(tm, D_MODEL)],
        out_shape=[jax.ShapeDtypeStruct((t, D_MODEL), F32), jax.ShapeDtypeStruct((t, 2 * HP), BF16),
                   jax.ShapeDtypeStruct((t, D_MODEL), BF16)],
        compiler_params=_params("parallel"),
    )(h, oa, ob, ga, gb, wo, g2)


def _ffn_fwd(h2, u2, wg, wu, wd):
    t = h2.shape[0]
    tm = _tile(t)
    dff = wg.shape[1]

    def body(h2_ref, u2_ref, wg_ref, wu_ref, wd_ref, h3_ref, g_ref, up_ref):
        u2v = u2_ref[...]
        g = _dot(u2v, wg_ref[...])
        up = _dot(u2v, wu_ref[...])
        g_ref[...] = g.astype(BF16)
        up_ref[...] = up.astype(BF16)
        a = (g * jax.nn.sigmoid(g) * up).astype(BF16)
        h3_ref[...] = h2_ref[...] + _dot(a, wd_ref[...])

    return pl.pallas_call(
        body, name="ffn_fwd", grid=(t // tm,),
        in_specs=[_row(tm, D_MODEL), _row(tm, D_MODEL), _const(wg.shape), _const(wu.shape), _const(wd.shape)],
        out_specs=[_row(tm, D_MODEL), _row(tm, dff), _row(tm, dff)],
        out_shape=[jax.ShapeDtypeStruct((t, D_MODEL), F32), jax.ShapeDtypeStruct((t, dff), BF16),
                   jax.ShapeDtypeStruct((t, dff), BF16)],
        compiler_params=_params("parallel"),
    )(h2, u2, wg, wu, wd)


def _loss_bwd(h, gf, target):
    t = h.shape[0]
    tm = _tile(t)
    first_row = FRONT + N_META

    def body(h_ref, gf_ref, t_ref, dh_ref, dgf_ref, loss_ref):
        i = pl.program_id(0)
        hv = h_ref[...]
        y = hv * _rms_r(hv, D_MODEL) * gf_ref[...]
        row = i * tm + lax.broadcasted_iota(jnp.int32, (tm, 1), 0)
        err = jnp.where(row >= first_row, y - t_ref[...], 0.0)
        dx, dg = _rms_bwd(hv, gf_ref[...], err * (1.0 / D_MODEL), D_MODEL)
        dh_ref[...] = dx
        _acc(dgf_ref, dg, i == 0)
        part = 0.5 * jnp.sum(jnp.sum(err * err, axis=1, keepdims=True) * (1.0 / D_MODEL), axis=0, keepdims=True)
        _acc(loss_ref, jnp.broadcast_to(part, (1, LANE)), i == 0)

    return pl.pallas_call(
        body, name="loss_bwd", grid=(t // tm,),
        in_specs=[_row(tm, D_MODEL), _const(gf.shape), _row(tm, D_MODEL)],
        out_specs=[_row(tm, D_MODEL), _const((1, D_MODEL)), _const((1, LANE))],
        out_shape=[jax.ShapeDtypeStruct((t, D_MODEL), F32), jax.ShapeDtypeStruct((1, D_MODEL), F32),
                   jax.ShapeDtypeStruct((1, LANE), F32)],
        compiler_params=_params("arbitrary"),
    )(h, gf, target)


def _tn_matmul(a, b, name):
    t, k = a.shape
    n = b.shape[1]
    tt = _tile(t)
    tn = next(c for c in (n, 2048, 1024, 512, 256, 128) if n % c == 0 and k * c * 4 <= 6 * 2 ** 20)

    def body(a_ref, b_ref, o_ref):
        part = _dot_tn(a_ref[...].astype(BF16), b_ref[...].astype(BF16))
        _acc(o_ref, part, pl.program_id(1) == 0)

    return pl.pallas_call(
        body, name=name, grid=(n // tn, t // tt),
        in_specs=[pl.BlockSpec((tt, k), lambda j, i: (i, 0)), pl.BlockSpec((tt, tn), lambda j, i: (i, j))],
        out_specs=pl.BlockSpec((k, tn), lambda j, i: (0, j)),
        out_shape=jax.ShapeDtypeStruct((k, n), F32),
        compiler_params=_params("parallel", "arbitrary"),
    )(a, b)


def _ffn_bwd_a(dh3, g, up, wd):
    t = dh3.shape[0]
    tm = _tile(t)
    dff = wd.shape[0]

    def body(dh3_ref, g_ref, up_ref, wd_ref, a_ref, dgu_ref):
        da = _dot_nt(dh3_ref[...].astype(BF16), wd_ref[...])
        gv = g_ref[...].astype(F32)
        upv = up_ref[...].astype(F32)
        sg = jax.nn.sigmoid(gv)
        silu = gv * sg
        a_ref[...] = (silu * upv).astype(BF16)
        dgu_ref[:, :dff] = (da * upv * (sg * (1.0 + gv * (1.0 - sg)))).astype(BF16)
        dgu_ref[:, dff:] = (da * silu).astype(BF16)

    return pl.pallas_call(
        body, name="ffn_bwd_a", grid=(t // tm,),
        in_specs=[_row(tm, D_MODEL), _row(tm, dff), _row(tm, dff), _const(wd.shape)],
        out_specs=[_row(tm, dff), _row(tm, 2 * dff)],
        out_shape=[jax.ShapeDtypeStruct((t, dff), BF16), jax.ShapeDtypeStruct((t, 2 * dff), BF16)],
        compiler_params=_params("parallel"),
    )(dh3, g, up, wd)


def _ffn_bwd_b(dh3, dgu, h2, g2, wg, wu):
    t = dh3.shape[0]
    tm = _tile(t)
    dff = wg.shape[1]

    def body(dh3_ref, dgu_ref, h2_ref, g2_ref, wg_ref, wu_ref, dh2_ref, dg2_ref):
        du2 = _dot_nt(dgu_ref[:, :dff], wg_ref[...]) + _dot_nt(dgu_ref[:, dff:], wu_ref[...])
        dx, dg = _rms_bwd(h2_ref[...], g2_ref[...], du2, D_MODEL)
        dh2_ref[...] = dh3_ref[...] + dx
        _acc(dg2_ref, dg, pl.program_id(0) == 0)

    return pl.pallas_call(
        body, name="ffn_bwd_b", grid=(t // tm,),
        in_specs=[_row(tm, D_MODEL), _row(tm, 2 * dff), _row(tm, D_MODEL), _const(g2.shape), _const(wg.shape),
                  _const(wu.shape)],
        out_specs=[_row(tm, D_MODEL), _const((1, D_MODEL))],
        out_shape=[jax.ShapeDtypeStruct((t, D_MODEL), F32), jax.ShapeDtypeStruct((1, D_MODEL), F32)],
        compiler_params=_params("arbitrary"),
    )(dh3, dgu, h2, g2, wg, wu)


def _mix_bwd(dh2, oa, ob, ga, gb, wo):
    t = dh2.shape[0]
    tm = _tile(t)

    def body(dh2_ref, oa_ref, ob_ref, ga_ref, gb_ref, wo_ref, doa_ref, dob_ref, dga_ref, dgb_ref):
        first = pl.program_id(0) == 0
        d = dh2_ref[...].astype(BF16)
        dxa, dga = _rms_bwd(oa_ref[...], ga_ref[...], _dot_nt(d, wo_ref[:HP, :]), SWA_Q_W)
        dxb, dgb = _rms_bwd(ob_ref[...], gb_ref[...], _dot_nt(d, wo_ref[HP:, :]), MLA_OUT_W)
        doa_ref[...] = dxa.astype(BF16)
        dob_ref[...] = dxb.astype(BF16)
        _acc(dga_ref, dga, first)
        _acc(dgb_ref, dgb, first)

    return pl.pallas_call(
        body, name="mix_bwd", grid=(t // tm,),
        in_specs=[_row(tm, D_MODEL), _row(tm, HP), _row(tm, HP), _const(ga.shape), _const(gb.shape), _const(wo.shape)],
        out_specs=[_row(tm, HP), _row(tm, HP), _const((1, HP)), _const((1, HP))],
        out_shape=[jax.ShapeDtypeStruct((t, HP), BF16), jax.ShapeDtypeStruct((t, HP), BF16),
                   jax.ShapeDtypeStruct((1, HP), F32), jax.ShapeDtypeStruct((1, HP), F32)],
        compiler_params=_params("arbitrary"),
    )(dh2, oa, ob, ga, gb, wo)


def _swa_bwd(sinks, q, k, v, o, do):
    t = q.shape[0]

    def body(sink_ref, q_ref, kp_ref, kc_ref, vp_ref, vc_ref, o_ref, do_ref,
             dq_ref, dkc_ref, dkp_ref, dvc_ref, dvp_ref, dsink_ref):
        n = pl.program_id(0)
        mask = _swa_mask(n)
        for j in range(SWA_KV_HEADS):
            sl = slice(LANE * j, LANE * (j + 1))
            k2 = jnp.concatenate([kp_ref[:, sl], kc_ref[:, sl]], axis=0)
            v2 = jnp.concatenate([vp_ref[:, sl], vc_ref[:, sl]], axis=0)
            dk2 = jnp.zeros((2 * BLOCK, LANE), F32)
            dv2 = jnp.zeros((2 * BLOCK, LANE), F32)
            for g in range(SWA_GROUP):
                hd = SWA_GROUP * j + g
                hs = slice(LANE * hd, LANE * (hd + 1))
                qh = q_ref[:, hs]
                doh = do_ref[:, hs]
                p, psink = _swa_probs(qh, k2, sink_ref[0, hd], mask)
                delta = jnp.sum(o_ref[:, hs] * doh.astype(F32), axis=1, keepdims=True)
                ds = p * (_dot_nt(doh, v2) - delta) * SCALE_A
                dq_ref[:, hs] = _dot(ds.astype(BF16), k2)
                dk2 += _dot(ds.T.astype(BF16), qh)
                dv2 += _dot(p.T.astype(BF16), doh)
                dsink = jnp.broadcast_to(jnp.sum(-psink * delta, axis=0, keepdims=True), (1, LANE))
                _acc(dsink_ref.at[hd:hd + 1, :], dsink, n == 0)
            dkp_ref[:, sl] = dk2[:BLOCK]
            dkc_ref[:, sl] = dk2[BLOCK:]
            dvp_ref[:, sl] = dv2[:BLOCK]
            dvc_ref[:, sl] = dv2[BLOCK:]

    cur = lambda n: (n, 0)
    kv = pl.BlockSpec((BLOCK, 2 * LANE), cur)
    hp = pl.BlockSpec((BLOCK, HP), cur)
    kvs = jax.ShapeDtypeStruct((t, 2 * LANE), F32)
    return pl.pallas_call(
        body, name="swa_bwd", grid=(t // BLOCK,),
        in_specs=_swa_specs() + [hp, hp],
        out_specs=[hp, kv, kv, kv, kv, _const((SWA_HEADS, LANE))],
        out_shape=[jax.ShapeDtypeStruct((t, HP), F32), kvs, kvs, kvs, kvs,
                   jax.ShapeDtypeStruct((SWA_HEADS, LANE), F32)],
        compiler_params=_params("arbitrary"),
    )(sinks, q, k, k, v, v, o, do)


def _mla_bwd_dq(q, k, v, o, do, lse):
    t = q.shape[0]
    tq = _tile(t)
    nq = t // tq

    def body(q_ref, k_ref, v_ref, o_ref, do_ref, lse_ref, dq_ref, dl_ref):
        i = pl.program_id(1)
        qv = q_ref[...]
        dov = do_ref[...]
        delta = jnp.sum(o_ref[...] * dov.astype(F32), axis=1, keepdims=True)
        lse_v = lse_ref[:, :1]

        def step(j, dq):
            off = pl.multiple_of(j * tq, tq)
            kj = k_ref[pl.ds(off, tq), :]
            vj = v_ref[pl.ds(off, tq), :]
            s = jnp.where(_causal_mask(i * tq, j * tq, tq, tq, False), _dot_nt(qv, kj) * SCALE_B, NEG)
            ds = jnp.exp(s - lse_v) * (_dot_nt(dov, vj) - delta) * SCALE_B
            return dq + _dot(ds.astype(BF16), kj)

        dq_ref[...] = lax.fori_loop(0, i + 1, step, jnp.zeros((tq, LANE), F32))
        dl_ref[...] = jnp.broadcast_to(delta, (tq, LANE))

    blk = pl.BlockSpec((tq, LANE), lambda h, i: (i, h))
    full = pl.BlockSpec((t, LANE), lambda h, i: (0, h))
    return pl.pallas_call(
        body, name="mla_bwd_dq", grid=(MLA_HEADS, nq),
        in_specs=[blk, full, full, blk, blk, blk], out_specs=[blk, blk],
        out_shape=[jax.ShapeDtypeStruct((t, HP), F32)] * 2,
        compiler_params=_params("parallel", "parallel"),
    )(q, k, v, o, do, lse)


def _mla_bwd_dkv(q, k, v, do, lse_t, dl_t):
    t = q.shape[0]
    tq = _tile(t)
    nq = t // tq

    def body(k_ref, v_ref, q_ref, do_ref, lse_ref, dl_ref, dk_ref, dv_ref):
        hd = pl.program_id(0)
        j = pl.program_id(1)
        kj = k_ref[...]
        vj = v_ref[...]

        def step(i, carry):
            dk, dv = carry
            off = pl.multiple_of(i * tq, tq)
            qi = q_ref[pl.ds(off, tq), :]
            doi = do_ref[pl.ds(off, tq), :]
            st = jnp.where(_causal_mask(i * tq, j * tq, tq, tq, True), _dot_nt(kj, qi) * SCALE_B, NEG)
            pt = jnp.exp(st - lse_ref[hd * nq + i])
            dst = pt * (_dot_nt(vj, doi) - dl_ref[hd * nq + i]) * SCALE_B
            return dk + _dot(dst.astype(BF16), qi), dv + _dot(pt.astype(BF16), doi)

        zero = jnp.zeros((tq, LANE), F32)
        dk, dv = lax.fori_loop(j, nq, step, (zero, zero))
        dk_ref[...] = dk
        dv_ref[...] = dv

    blk = pl.BlockSpec((tq, LANE), lambda h, j: (j, h))
    full = pl.BlockSpec((t, LANE), lambda h, j: (0, h))
    rows = pl.BlockSpec((MLA_HEADS * nq, 1, tq), lambda h, j: (0, 0, 0))
    return pl.pallas_call(
        body, name="mla_bwd_dkv", grid=(MLA_HEADS, nq),
        in_specs=[blk, blk, full, full, rows, rows], out_specs=[blk, blk],
        out_shape=[jax.ShapeDtypeStruct((t, HP), F32)] * 2,
        compiler_params=_params("parallel", "parallel"),
    )(k, v, q, do, lse_t, dl_t)


def _pre_bwd(dh2, h, cq, ckv, dqa, dka, dka_next, dva, dva_next, dqb, dkf, dvb, g1, win, gq, wqu, gkv, wkv, tabs):
    t = h.shape[0]
    tm = _tile(t)

    def body(dh2_ref, h_ref, cq_ref, ckv_ref, dqa_ref, dka_ref, dkan_ref, dva_ref, dvan_ref, dqb_ref, dkf_ref, dvb_ref,
             g1_ref, win_ref, gq_ref, wqu_ref, gkv_ref, wkv_ref, tab_ref,
             dh_ref, dp_ref, dqbo_ref, dkvo_ref, dg1_ref, dgq_ref, dgkv_ref):
        first = pl.program_id(0) == 0
        ca, sa1, sa2, cb, sb1, sb2, ck = _tabs(tab_ref)
        dkr = jnp.zeros((tm, LANE), F32)
        for c in range(MLA_HEADS):
            sl = slice(LANE * c, LANE * (c + 1))
            dqbo_ref[:, sl] = _rope_t(dqb_ref[:, sl], cb, sb1, sb2, 16).astype(BF16)
            dkr += dkf_ref[:, sl]
        dkvo_ref[:, :HP] = dkf_ref[...].astype(BF16)
        dkvo_ref[:, HP:] = dvb_ref[...].astype(BF16)
        dcq, dgq = _rms_bwd(cq_ref[...], gq_ref[...], _dot_nt(dqbo_ref[...], wqu_ref[...]), MLA_Q_RANK)
        dckv, dgkv = _rms_bwd(ckv_ref[...], gkv_ref[...], _dot_nt(dkvo_ref[...], wkv_ref[...]), MLA_KV_RANK)
        for c in range(SWA_HEADS):
            sl = slice(LANE * c, LANE * (c + 1))
            dp_ref[:, PO_QA + LANE * c:PO_QA + LANE * (c + 1)] = _rope_t(dqa_ref[:, sl], ca, sa1, sa2, 32).astype(BF16)
        for c in range(SWA_KV_HEADS):
            sl = slice(LANE * c, LANE * (c + 1))
            dk = dka_ref[:, sl] + dkan_ref[:, sl]
            dp_ref[:, PO_KA + LANE * c:PO_KA + LANE * (c + 1)] = _rope_t(dk, ca, sa1, sa2, 32).astype(BF16)
        dp_ref[:, PO_VA:PO_CQ] = (dva_ref[...] + dvan_ref[...]).astype(BF16)
        dp_ref[:, PO_CQ:PO_CKV] = dcq.astype(BF16)
        dp_ref[:, PO_CKV:PO_KR] = dckv.astype(BF16)
        dp_ref[:, PO_KR:PW_IN] = _rope_t(dkr, ck, sb1, sb2, 16).astype(BF16)
        dx, dg1 = _rms_bwd(h_ref[...], g1_ref[...], _dot_nt(dp_ref[...], win_ref[...]), D_MODEL)
        dh_ref[...] = dh2_ref[...] + dx
        _acc(dg1_ref, dg1, first)
        _acc(dgq_ref, dgq, first)
        _acc(dgkv_ref, dgkv, first)

    kv = _row(tm, 2 * LANE)
    return pl.pallas_call(
        body, name="pre_bwd", grid=(t // tm,),
        in_specs=[_row(tm, D_MODEL), _row(tm, D_MODEL), _row(tm, MLA_Q_RANK), _row(tm, MLA_KV_RANK), _row(tm, HP),
                  kv, kv, kv, kv, _row(tm, HP), _row(tm, HP), _row(tm, HP),
                  _const(g1.shape), _const(win.shape), _const(gq.shape), _const(wqu.shape), _const(gkv.shape),
                  _const(wkv.shape), _row(tm, N_TAB * LANE)],
        out_specs=[_row(tm, D_MODEL), _row(tm, PW_IN), _row(tm, HP), _row(tm, 2 * HP),
                   _const((1, D_MODEL)), _const((1, MLA_Q_RANK)), _const((1, MLA_KV_RANK))],
        out_shape=[jax.ShapeDtypeStruct((t, D_MODEL), F32), jax.ShapeDtypeStruct((t, PW_IN), BF16),
                   jax.ShapeDtypeStruct((t, HP), BF16), jax.ShapeDtypeStruct((t, 2 * HP), BF16),
                   jax.ShapeDtypeStruct((1, D_MODEL), F32), jax.ShapeDtypeStruct((1, MLA_Q_RANK), F32),
                   jax.ShapeDtypeStruct((1, MLA_KV_RANK), F32)],
        compiler_params=_params("arbitrary"),
    )(dh2, h, cq, ckv, dqa, dka, dka_next, dva, dva_next, dqb, dkf, dvb, g1, win, gq, wqu, gkv, wkv, tabs)


def _rope_tables(t):
    pos = (jnp.arange(t, dtype=jnp.int32) - FRONT).astype(F32)[:, None]
    lane = jnp.arange(LANE)[None, :]

    def table(dim, start):
        half = dim // 2
        inv = ROPE_THETA ** (-jnp.arange(0, dim, 2, dtype=F32) / dim)
        ang = pos * inv[None, :]
        cos = jnp.concatenate([jnp.cos(ang)] * 2, axis=1)
        sin = jnp.concatenate([jnp.sin(ang)] * 2, axis=1)
        pad = lambda a: jnp.pad(a, ((0, 0), (start, LANE - start - dim)))
        first = (lane >= start) & (lane < start + half)
        second = (lane >= start + half) & (lane < start + dim)
        return pad(cos), jnp.where(first, -pad(sin), 0.0), jnp.where(second, pad(sin), 0.0)

    ca, sa1, sa2 = table(SWA_HEAD_DIM, 0)
    ck, sb1, sb2 = table(MLA_ROPE_DIM, MLA_NOPE_DIM)
    cb = jnp.where(lane < MLA_NOPE_DIM, 1.0, ck)
    return jnp.concatenate([ca, sa1, sa2, cb, sb1, sb2, ck], axis=1)


def _pad_heads(w, heads, dim, axis):
    shp = w.shape
    w = w.reshape(shp[:axis] + (heads, dim) + shp[axis + 1:])
    pad = [(0, 0)] * w.ndim
    pad[axis + 1] = (0, LANE - dim)
    return jnp.pad(w, pad).reshape(shp[:axis] + (heads * LANE,) + shp[axis + 1:])


def _unpad_heads(w, heads, dim, axis):
    shp = w.shape
    w = w.reshape(shp[:axis] + (heads, LANE) + shp[axis + 1:])
    w = lax.slice_in_dim(w, 0, dim, axis=axis + 1)
    return w.reshape(shp[:axis] + (heads * dim,) + shp[axis + 1:])


def _pad_layer(w_in, w_q_up, w_kv_up, w_o, out_norm_swa, out_norm_mla):
    o1 = SWA_Q_W
    o2 = o1 + SWA_KV_W
    o3 = o2 + SWA_KV_W
    o4 = o3 + MLA_Q_RANK
    o5 = o4 + MLA_KV_RANK
    kr = jnp.pad(w_in[:, o5:], ((0, 0), (MLA_NOPE_DIM, LANE - MLA_QK_DIM)))
    win = jnp.concatenate([
        _pad_heads(w_in[:, :o1], SWA_HEADS, SWA_HEAD_DIM, 1),
        _pad_heads(w_in[:, o1:o2], SWA_KV_HEADS, SWA_HEAD_DIM, 1),
        _pad_heads(w_in[:, o2:o3], SWA_KV_HEADS, SWA_HEAD_DIM, 1),
        w_in[:, o3:o5], kr], axis=1)
    wqu = _pad_heads(w_q_up, MLA_HEADS, MLA_QK_DIM, 1)
    kv = w_kv_up.reshape(MLA_KV_RANK, MLA_HEADS, MLA_NOPE_DIM + MLA_V_DIM)
    wkv = jnp.concatenate([
        _pad_heads(kv[:, :, :MLA_NOPE_DIM].reshape(MLA_KV_RANK, -1), MLA_HEADS, MLA_NOPE_DIM, 1),
        _pad_heads(kv[:, :, MLA_NOPE_DIM:].reshape(MLA_KV_RANK, -1), MLA_HEADS, MLA_V_DIM, 1)], axis=1)
    wo = jnp.concatenate([_pad_heads(w_o[:SWA_Q_W], SWA_HEADS, SWA_HEAD_DIM, 0),
                          _pad_heads(w_o[SWA_Q_W:], MLA_HEADS, MLA_V_DIM, 0)], axis=0)
    ga = _pad_heads(out_norm_swa[None, :], SWA_HEADS, SWA_HEAD_DIM, 1)
    gb = _pad_heads(out_norm_mla[None, :], MLA_HEADS, MLA_V_DIM, 1)
    return win, wqu, wkv, wo, ga, gb


def _unpad_layer(dwin, dwqu, dwkv, dwo, dga, dgb):
    d_w_in = jnp.concatenate([
        _unpad_heads(dwin[:, PO_QA:PO_KA], SWA_HEADS, SWA_HEAD_DIM, 1),
        _unpad_heads(dwin[:, PO_KA:PO_VA], SWA_KV_HEADS, SWA_HEAD_DIM, 1),
        _unpad_heads(dwin[:, PO_VA:PO_CQ], SWA_KV_HEADS, SWA_HEAD_DIM, 1),
        dwin[:, PO_CQ:PO_KR], dwin[:, PO_KR + MLA_NOPE_DIM:PO_KR + MLA_QK_DIM]], axis=1)
    d_w_q_up = _unpad_heads(dwqu, MLA_HEADS, MLA_QK_DIM, 1)
    dk = _unpad_heads(dwkv[:, :HP], MLA_HEADS, MLA_NOPE_DIM, 1).reshape(MLA_KV_RANK, MLA_HEADS, MLA_NOPE_DIM)
    dv = _unpad_heads(dwkv[:, HP:], MLA_HEADS, MLA_V_DIM, 1).reshape(MLA_KV_RANK, MLA_HEADS, MLA_V_DIM)
    d_w_kv_up = jnp.concatenate([dk, dv], axis=2).reshape(MLA_KV_RANK, -1)
    d_w_o = jnp.concatenate([_unpad_heads(dwo[:HP], SWA_HEADS, SWA_HEAD_DIM, 0),
                             _unpad_heads(dwo[HP:], MLA_HEADS, MLA_V_DIM, 0)], axis=0)
    d_ga = _unpad_heads(dga, SWA_HEADS, SWA_HEAD_DIM, 1)[0]
    d_gb = _unpad_heads(dgb, MLA_HEADS, MLA_V_DIM, 1)[0]
    return d_w_in, d_w_q_up, d_w_kv_up, d_w_o, d_ga, d_gb


def _shift_up(a):
    return jnp.concatenate([a[BLOCK:], jnp.zeros((BLOCK, a.shape[1]), a.dtype)], axis=0)


def _local_step(x, target, meta, attn_norm, w_in, q_norm, w_q_up, kv_norm, w_kv_up, sinks, out_norm_swa,
                out_norm_mla, w_o, ffn_norm, w_gate, w_up, w_down, final_norm):
    s = x.shape[0]
    depth = w_in.shape[0]
    t = FRONT + N_META + s
    assert t % BLOCK == 0
    tq = _tile(t)
    nq = t // tq
    tabs = _rope_tables(t)
    h = jnp.concatenate([jnp.zeros((FRONT, D_MODEL), F32), meta, x], axis=0)
    tgt = jnp.concatenate([jnp.zeros((FRONT + N_META, D_MODEL), F32), target], axis=0)
    row = lambda v: v[None, :]

    saved = []
    for l in range(depth):
        win, wqu, wkv, wo, ga, gb = _pad_layer(w_in[l], w_q_up[l], w_kv_up[l], w_o[l], out_norm_swa[l], out_norm_mla[l])
        g1, gq, gkv, g2 = row(attn_norm[l]), row(q_norm[l]), row(kv_norm[l]), row(ffn_norm[l])
        sk = row(sinks[l])
        u, qa, ka, va, cq, ckv, qn, kvn, qb, kf, vb = _pre_fwd(h, g1, win, gq, wqu, gkv, wkv, tabs)
        oa = _swa_fwd(sk, qa, ka, va)
        ob, lse = _mla_fwd(qb, kf, vb)
        h2, mix, u2 = _mix_fwd(h, oa, ob, ga, gb, wo, g2)
        h3, gt, up = _ffn_fwd(h2, u2, w_gate[l], w_up[l], w_down[l])
        saved.append((h, u, qa, ka, va, cq, ckv, qn, kvn, qb, kf, vb, oa, ob, lse, h2, mix, u2, gt, up,
                      win, wqu, wkv, wo, ga, gb, g1, gq, gkv, g2, sk))
        h = h3

    dh, d_final, loss = _loss_bwd(h, row(final_norm), tgt)

    grads = []
    to_rows = lambda a: a[:, ::LANE].T.reshape(MLA_HEADS * nq, 1, tq)
    for l in reversed(range(depth)):
        (h0, u, qa, ka, va, cq, ckv, qn, kvn, qb, kf, vb, oa, ob, lse, h2, mix, u2, gt, up,
         win, wqu, wkv, wo, ga, gb, g1, gq, gkv, g2, sk) = saved[l]
        act, dgu = _ffn_bwd_a(dh, gt, up, w_down[l])
        d_w_down = _tn_matmul(act, dh, "dw_down")
        d_w_gu = _tn_matmul(u2, dgu, "dw_gate_up")
        dh2, d_g2 = _ffn_bwd_b(dh, dgu, h2, g2, w_gate[l], w_up[l])
        d_wo = _tn_matmul(mix, dh2, "dw_o")
        doa, dob, d_ga, d_gb = _mix_bwd(dh2, oa, ob, ga, gb, wo)
        dqa, dkc, dkp, dvc, dvp, dsink = _swa_bwd(sk, qa, ka, va, oa, doa)
        dqb, dl = _mla_bwd_dq(qb, kf, vb, ob, dob, lse)
        dkf, dvb = _mla_bwd_dkv(qb, kf, vb, dob, to_rows(lse), to_rows(dl))
        dh, dp, dqbo, dkvo, d_g1, d_gq, d_gkv = _pre_bwd(
            dh2, h0, cq, ckv, dqa, dkc, _shift_up(dkp), dvc, _shift_up(dvp), dqb, dkf, dvb,
            g1, win, gq, wqu, gkv, wkv, tabs)
        d_win = _tn_matmul(u, dp, "dw_in")
        d_wqu = _tn_matmul(qn, dqbo, "dw_q_up")
        d_wkv = _tn_matmul(kvn, dkvo, "dw_kv_up")
        d_w_in, d_w_q_up, d_w_kv_up, d_w_o, d_sw, d_ml = _unpad_layer(d_win, d_wqu, d_wkv, d_wo, d_ga, d_gb)
        dff = w_gate.shape[2]
        grads.append(dict(attn_norm=d_g1[0], w_in=d_w_in, q_norm=d_gq[0], w_q_up=d_w_q_up, kv_norm=d_gkv[0],
                          w_kv_up=d_w_kv_up, sinks=dsink[:, 0], out_norm_swa=d_sw, out_norm_mla=d_ml, w_o=d_w_o,
                          ffn_norm=d_g2[0], w_gate=d_w_gu[:, :dff], w_up=d_w_gu[:, dff:], w_down=d_w_down))
    grads = grads[::-1]
    stacked = {k: jnp.stack([g[k] for g in grads]) for k in grads[0]}
    stacked["final_norm"] = d_final[0]
    return loss[0, 0], dh[FRONT + N_META:], dh[FRONT:FRONT + N_META], stacked


MESH = pl.DeviceIdType.MESH
ANY = pl.BlockSpec(memory_space=pl.ANY)
SEMS = [pltpu.SemaphoreType.DMA((N_DEV - 1,)), pltpu.SemaphoreType.DMA((N_DEV - 1,)), pltpu.SemaphoreType.DMA(())]


def _place():
    return lax.axis_index("x"), lax.axis_index("y"), lax.axis_index("c")


def _index(x, y, c):
    return 4 * x + 2 * y + c


def _all_gather(shard, name):
    def body(x_ref, out_ref, send_sems, recv_sems, local_sem):
        x, y, c = _place()
        me, sibling = (x, y, c), (x, y, 1 - c)
        chips = [(1 - x, y), (x, 1 - y), (1 - x, 1 - y)]

        def slot(px, py, pc):
            return out_ref.at[_index(px, py, pc)]

        def copy(k, block, to, src=None):
            return pltpu.make_async_remote_copy(
                src_ref=slot(*block) if src is None else src, dst_ref=slot(*block),
                send_sem=send_sems.at[k], recv_sem=recv_sems.at[k], device_id=to, device_id_type=MESH)

        mine = pltpu.make_async_copy(x_ref, slot(*me), local_sem)
        mine.start()
        first = [copy(0, me, sibling, src=x_ref)]
        first += [copy(1 + j, me, (*chip, c), src=x_ref) for j, chip in enumerate(chips)]
        for cp in first:
            cp.start()
        passed = [copy(4 + j, (*chip, c), sibling) for j, chip in enumerate(chips)]
        for j, chip in enumerate(chips):
            copy(1 + j, (*chip, c), me).wait_recv()
            passed[j].start()
        copy(0, sibling, me).wait_recv()
        for j, chip in enumerate(chips):
            copy(4 + j, (*chip, 1 - c), me).wait_recv()
        for cp in first + passed:
            cp.wait_send()
        mine.wait()

    return pl.pallas_call(
        body, name=name, in_specs=[ANY], out_specs=ANY, scratch_shapes=SEMS,
        out_shape=jax.ShapeDtypeStruct((N_DEV,) + shard.shape, shard.dtype),
    )(shard)


def _exchange(slabs, name):
    def body(in_ref, out_ref, send_sems, recv_sems, local_sem):
        x, y, c = _place()
        me = _index(x, y, c)
        mine = pltpu.make_async_copy(in_ref.at[me], out_ref.at[me], local_sem)
        mine.start()
        copies = []
        for k in range(1, N_DEV):
            peer = (1 - x if k & 4 else x, 1 - y if k & 2 else y, 1 - c if k & 1 else c)
            copies.append(pltpu.make_async_remote_copy(
                src_ref=in_ref.at[_index(*peer)], dst_ref=out_ref.at[me],
                send_sem=send_sems.at[k - 1], recv_sem=recv_sems.at[k - 1], device_id=peer, device_id_type=MESH))
        for cp in copies:
            cp.start()
        for cp in copies:
            cp.wait_recv()
        for cp in copies:
            cp.wait_send()
        mine.wait()

    return pl.pallas_call(
        body, name=name, in_specs=[ANY], out_specs=ANY, scratch_shapes=SEMS,
        out_shape=jax.ShapeDtypeStruct(slabs.shape, slabs.dtype),
    )(slabs)


def _adamw(w, g, m, v):
    m = ADAM_B1 * m + (1.0 - ADAM_B1) * g
    v = ADAM_B2 * v + (1.0 - ADAM_B2) * (g * g)
    m_hat = m / (1.0 - ADAM_B1 ** ADAM_STEP)
    v_hat = v / (1.0 - ADAM_B2 ** ADAM_STEP)
    return -ADAM_LR * (m_hat / (jnp.sqrt(v_hat) + ADAM_EPS) + ADAM_WD * w), m, v


def _sum_slots(ref):
    g = ref[0].astype(F32)
    for s in range(1, N_DEV):
        g = g + ref[s].astype(F32)
    return g


def _reduce_adamw(parts, w, m, v, name, tile):
    r, c = w.shape

    def body(p_ref, w_ref, m_ref, v_ref, g_ref, d_ref, nm_ref, nv_ref):
        g = _sum_slots(p_ref)
        g_ref[...] = g
        d_ref[...], nm_ref[...], nv_ref[...] = _adamw(w_ref[...], g, m_ref[...], v_ref[...])

    blk = _row(tile, c)
    return pl.pallas_call(
        body, name=name, grid=(r // tile,),
        in_specs=[pl.BlockSpec((N_DEV, tile, c), lambda i: (0, i, 0)), blk, blk, blk], out_specs=[blk] * 4,
        out_shape=[jax.ShapeDtypeStruct((r, c), F32)] * 4,
        compiler_params=_params("parallel"),
    )(parts, w, m, v)


def _sum_parts(parts, name):
    _, r, c = parts.shape

    def body(p_ref, g_ref):
        g_ref[...] = _sum_slots(p_ref)

    return pl.pallas_call(body, name=name, out_shape=jax.ShapeDtypeStruct((r, c), F32))(parts)


def _adamw_call(w, g, m, v, name):
    def body(w_ref, g_ref, m_ref, v_ref, d_ref, nm_ref, nv_ref):
        d_ref[...], nm_ref[...], nv_ref[...] = _adamw(w_ref[...], g_ref[...], m_ref[...], v_ref[...])

    return pl.pallas_call(body, name=name, out_shape=[jax.ShapeDtypeStruct(w.shape, F32)] * 3)(w, g, m, v)


BIG = (("w_in", 2), ("w_q_up", 2), ("w_kv_up", 2), ("w_o", 1), ("w_gate", 2), ("w_up", 2), ("w_down", 1))
SMALL = ("attn_norm", "ffn_norm", "final_norm", "out_norm_swa", "out_norm_mla", "q_norm", "kv_norm", "sinks")
PACK_W = 1024
SMALL_ROWS = 16
ADAM_TILE = 112


def _pack(arrs, dtype):
    flat = jnp.concatenate([a.astype(dtype).reshape(-1) for a in arrs])
    return flat.reshape(-1, PACK_W)


def _unpack(packed, like):
    flat = packed.reshape(-1)
    out, off = [], 0
    for a in like:
        out.append(flat[off:off + a.size].reshape(a.shape))
        off += a.size
    return out


def _gather_to_full(gathered, shards):
    flat = gathered.reshape(N_DEV, -1)
    out, off = [], 0
    for (_, axis), a in zip(BIG, shards):
        seg = flat[:, off:off + a.size].reshape((N_DEV,) + a.shape)
        off += a.size
        seg = jnp.moveaxis(seg, 0, axis)
        shp = list(a.shape)
        shp[axis] *= N_DEV
        out.append(seg.reshape(shp))
    return out


def _full_to_slabs(fulls, dtype):
    segs = []
    for (_, axis), a in zip(BIG, fulls):
        shp = list(a.shape)
        shp[axis:axis + 1] = [N_DEV, shp[axis] // N_DEV]
        segs.append(jnp.moveaxis(a.astype(dtype).reshape(shp), axis, 0).reshape(N_DEV, -1))
    return jnp.concatenate(segs, axis=1).reshape(N_DEV, -1, PACK_W)


def kernel(x, meta_tokens, attn_norm, w_in, q_norm, w_q_up, kv_norm, w_kv_up, sinks, out_norm_swa, out_norm_mla, w_o, ffn_norm, w_gate, w_up, w_down, final_norm, loss_target, m_meta_tokens, m_attn_norm, m_w_in, m_q_norm, m_w_q_up, m_kv_norm, m_w_kv_up, m_sinks, m_out_norm_swa, m_out_norm_mla, m_w_o, m_ffn_norm, m_w_gate, m_w_up, m_w_down, m_final_norm, v_meta_tokens, v_attn_norm, v_w_in, v_q_norm, v_w_q_up, v_kv_norm, v_w_kv_up, v_sinks, v_out_norm_swa, v_out_norm_mla, v_w_o, v_ffn_norm, v_w_gate, v_w_up, v_w_down, v_final_norm):
    w = dict(meta_tokens=meta_tokens, attn_norm=attn_norm, w_in=w_in, q_norm=q_norm, w_q_up=w_q_up, kv_norm=kv_norm,
             w_kv_up=w_kv_up, sinks=sinks, out_norm_swa=out_norm_swa, out_norm_mla=out_norm_mla, w_o=w_o,
             ffn_norm=ffn_norm, w_gate=w_gate, w_up=w_up, w_down=w_down, final_norm=final_norm)
    m = dict(meta_tokens=m_meta_tokens, attn_norm=m_attn_norm, w_in=m_w_in, q_norm=m_q_norm, w_q_up=m_w_q_up,
             kv_norm=m_kv_norm, w_kv_up=m_w_kv_up, sinks=m_sinks, out_norm_swa=m_out_norm_swa,
             out_norm_mla=m_out_norm_mla, w_o=m_w_o, ffn_norm=m_ffn_norm, w_gate=m_w_gate, w_up=m_w_up,
             w_down=m_w_down, final_norm=m_final_norm)
    v = dict(meta_tokens=v_meta_tokens, attn_norm=v_attn_norm, w_in=v_w_in, q_norm=v_q_norm, w_q_up=v_w_q_up,
             kv_norm=v_kv_norm, w_kv_up=v_w_kv_up, sinks=v_sinks, out_norm_swa=v_out_norm_swa,
             out_norm_mla=v_out_norm_mla, w_o=v_w_o, ffn_norm=v_ffn_norm, w_gate=v_w_gate, w_up=v_w_up,
             w_down=v_w_down, final_norm=v_final_norm)
    names = list(w)
    big = [n for n, _ in BIG]
    me = _index(*_place())

    shards = [w[n] for n in big]
    full = dict(zip(big, _gather_to_full(_all_gather(_pack(shards, BF16), "gather_weights"), shards)))
    meta = jnp.moveaxis(_all_gather(meta_tokens, "gather_meta"), 0, 1).reshape(N_META, D_MODEL)

    loss, grad_x, d_meta, grads = _local_step(
        x[0], loss_target[0], meta, attn_norm, full["w_in"], q_norm, full["w_q_up"], kv_norm, full["w_kv_up"], sinks,
        out_norm_swa, out_norm_mla, full["w_o"], ffn_norm, full["w_gate"], full["w_up"], full["w_down"], final_norm)

    parts = _exchange(_full_to_slabs([grads[n] for n in big], BF16), "exchange_grads")
    packed = _reduce_adamw(parts, _pack(shards, F32), _pack([m[n] for n in big], F32), _pack([v[n] for n in big], F32),
                           "reduce_adamw", ADAM_TILE)
    g_big, d_big, m_big, v_big = [dict(zip(big, _unpack(p, shards))) for p in packed]

    small = [grads[n] for n in SMALL] + [loss.reshape(1)]
    pad = SMALL_ROWS * PACK_W - sum(a.size for a in small)
    part = jnp.concatenate([_pack(small + [jnp.zeros((pad,), F32)], F32), d_meta], axis=0)
    total = _sum_parts(_all_gather(part, "gather_small"), "sum_small")
    small_w = [w[n] for n in SMALL]
    packs = [_pack([d[n] for n in SMALL] + [jnp.zeros((pad + 1,), F32)], F32) for d in (w, m, v)]
    upd = _adamw_call(packs[0], total[:SMALL_ROWS], packs[1], packs[2], "adamw_small")
    g_small, d_small, m_small, v_small = [dict(zip(SMALL, _unpack(p, small_w))) for p in (total[:SMALL_ROWS],) + tuple(upd)]
    loss_total = total[:SMALL_ROWS].reshape(-1)[SMALL_ROWS * PACK_W - pad - 1]
    g_meta = lax.dynamic_slice_in_dim(total[SMALL_ROWS:], me * LANE, LANE, axis=1)
    d_mt, m_mt, v_mt = _adamw_call(meta_tokens, g_meta, m_meta_tokens, v_meta_tokens, "adamw_meta")

    outs = []
    for got in ({**g_big, **g_small, "meta_tokens": g_meta}, {**d_big, **d_small, "meta_tokens": d_mt},
                {**m_big, **m_small, "meta_tokens": m_mt}, {**v_big, **v_small, "meta_tokens": v_mt}):
        outs += [got[n] for n in names]
    return (loss_total, grad_x[None], *outs)
```

```python
import functools

import jax
import jax.numpy as jnp
from jax import lax
from jax.experimental import pallas as pl
from jax.experimental.pallas import tpu as pltpu

F32 = jnp.float32
BF16 = jnp.bfloat16

D_MODEL = 1024
N_META = 16
BLOCK = 128
FRONT = (-N_META) % BLOCK
ROPE_THETA = 10000.0
EPS = 1e-6
NEG = -1e30
SWA_HEADS = 8
SWA_KV_HEADS = 2
SWA_GROUP = SWA_HEADS // SWA_KV_HEADS
SWA_HEAD_DIM = 64
MLA_HEADS = 8
MLA_Q_RANK = 256
MLA_KV_RANK = 128
MLA_NOPE_DIM = 64
MLA_ROPE_DIM = 32
MLA_V_DIM = 64
MLA_QK_DIM = MLA_NOPE_DIM + MLA_ROPE_DIM
SWA_Q_W = SWA_HEADS * SWA_HEAD_DIM
SWA_KV_W = SWA_KV_HEADS * SWA_HEAD_DIM
MLA_OUT_W = MLA_HEADS * MLA_V_DIM
SCALE_A = SWA_HEAD_DIM ** -0.5
SCALE_B = MLA_QK_DIM ** -0.5
ADAM_LR = 0.001
ADAM_B1 = 0.9
ADAM_B2 = 0.999
ADAM_EPS = 1e-08
ADAM_WD = 0.01
ADAM_STEP = 10

LANE = 128
N_DEV = 8
HP = 8 * LANE
PO_QA, PO_KA, PO_VA = 0, HP, HP + 2 * LANE
PO_CQ = PO_VA + 2 * LANE
PO_CKV = PO_CQ + MLA_Q_RANK
PO_KR = PO_CKV + MLA_KV_RANK
PW_IN = PO_KR + LANE
N_TAB = 7
VMEM_LIMIT = 56 * 2 ** 20
TN_VMEM_BUDGET = 36 * 2 ** 20

NT = (((1,), (1,)), ((), ()))
TN = (((0,), (0,)), ((), ()))


def _tile(t):
    return 384 if t % 384 == 0 else 128


def _params(*sem):
    return pltpu.CompilerParams(dimension_semantics=sem, vmem_limit_bytes=VMEM_LIMIT)


def _row(tm, n):
    return pl.BlockSpec((tm, n), lambda i: (i, 0))


def _const(shape):
    return pl.BlockSpec(shape, lambda i: (0,) * len(shape))


def _dot(a, b):
    return jnp.dot(a, b, preferred_element_type=F32)


def _dot_nt(a, b):
    return lax.dot_general(a, b, NT, preferred_element_type=F32)


def _dot_tn(a, b):
    return lax.dot_general(a, b, TN, preferred_element_type=F32)


def _rope(x, c, s1, s2, shift):
    return x * c + pltpu.roll(x, LANE - shift, 1) * s1 + pltpu.roll(x, shift, 1) * s2


def _rope_t(dy, c, s1, s2, shift):
    return dy * c + pltpu.roll(dy * s1, shift, 1) + pltpu.roll(dy * s2, LANE - shift, 1)


def _rms_r(x, n):
    return lax.rsqrt(jnp.sum(x * x, axis=-1, keepdims=True) * (1.0 / n) + EPS)


def _rms_bwd(x, g, dy, n):
    r = _rms_r(x, n)
    xh = x * r
    dxh = dy * g
    dx = r * (dxh - xh * (jnp.sum(dxh * xh, axis=-1, keepdims=True) * (1.0 / n)))
    return dx, jnp.sum(dy * xh, axis=0, keepdims=True)


def _acc(ref, val, first):
    @pl.when(first)
    def _():
        ref[...] = val

    @pl.when(jnp.logical_not(first))
    def _():
        ref[...] += val


def _tabs(tab_ref):
    return [tab_ref[:, LANE * i:LANE * (i + 1)] for i in range(N_TAB)]


def _pre_fwd(h, g1, win, gq, wqu, gkv, wkv, tabs):
    t = h.shape[0]
    tm = _tile(t)

    def body(h_ref, g1_ref, win_ref, gq_ref, wqu_ref, gkv_ref, wkv_ref, tab_ref,
             u_ref, qa_ref, ka_ref, va_ref, cq_ref, ckv_ref, qn_ref, kvn_ref, qb_ref, kf_ref, vb_ref):
        ca, sa1, sa2, cb, sb1, sb2, ck = _tabs(tab_ref)
        hv = h_ref[...]
        u = (hv * _rms_r(hv, D_MODEL) * g1_ref[...]).astype(BF16)
        u_ref[...] = u
        p = _dot(u, win_ref[...])
        for c in range(SWA_HEADS):
            sl = slice(LANE * c, LANE * (c + 1))
            qa_ref[:, sl] = _rope(p[:, PO_QA + LANE * c:PO_QA + LANE * (c + 1)], ca, sa1, sa2, 32).astype(BF16)
        for c in range(SWA_KV_HEADS):
            sl = slice(LANE * c, LANE * (c + 1))
            ka_ref[:, sl] = _rope(p[:, PO_KA + LANE * c:PO_KA + LANE * (c + 1)], ca, sa1, sa2, 32).astype(BF16)
        va_ref[...] = p[:, PO_VA:PO_CQ].astype(BF16)
        cq = p[:, PO_CQ:PO_CKV]
        ckv = p[:, PO_CKV:PO_KR]
        cq_ref[...] = cq
        ckv_ref[...] = ckv
        qn = (cq * _rms_r(cq, MLA_Q_RANK) * gq_ref[...]).astype(BF16)
        qn_ref[...] = qn
        qb = _dot(qn, wqu_ref[...])
        kvn = (ckv * _rms_r(ckv, MLA_KV_RANK) * gkv_ref[...]).astype(BF16)
        kvn_ref[...] = kvn
        kv = _dot(kvn, wkv_ref[...])
        kr = _rope(p[:, PO_KR:PW_IN], ck, sb1, sb2, 16)
        for c in range(MLA_HEADS):
            sl = slice(LANE * c, LANE * (c + 1))
            qb_ref[:, sl] = _rope(qb[:, sl], cb, sb1, sb2, 16).astype(BF16)
            kf_ref[:, sl] = (kv[:, sl] + kr).astype(BF16)
        vb_ref[...] = kv[:, HP:].astype(BF16)

    widths = [(D_MODEL, BF16), (HP, BF16), (2 * LANE, BF16), (2 * LANE, BF16), (MLA_Q_RANK, F32),
              (MLA_KV_RANK, F32), (MLA_Q_RANK, BF16), (MLA_KV_RANK, BF16), (HP, BF16), (HP, BF16), (HP, BF16)]
    return pl.pallas_call(
        body, name="pre_fwd", grid=(t // tm,),
        in_specs=[_row(tm, D_MODEL), _const(g1.shape), _const(win.shape), _const(gq.shape), _const(wqu.shape),
                  _const(gkv.shape), _const(wkv.shape), _row(tm, N_TAB * LANE)],
        out_specs=[_row(tm, w) for w, _ in widths],
        out_shape=[jax.ShapeDtypeStruct((t, w), d) for w, d in widths],
        compiler_params=_params("parallel"),
    )(h, g1, win, gq, wqu, gkv, wkv, tabs)


def _swa_probs(qh, k2, sink, mask):
    s = jnp.where(mask, _dot_nt(qh, k2) * SCALE_A, NEG)
    m = jnp.maximum(jnp.max(s, axis=1, keepdims=True), sink)
    e = jnp.exp(s - m)
    es = jnp.exp(sink - m)
    inv = 1.0 / (jnp.sum(e, axis=1, keepdims=True) + es)
    return e * inv, es * inv


def _swa_mask(n):
    row = lax.broadcasted_iota(jnp.int32, (SWA_GROUP * BLOCK, 2 * BLOCK), 0) & (BLOCK - 1)
    col = lax.broadcasted_iota(jnp.int32, (SWA_GROUP * BLOCK, 2 * BLOCK), 1)
    return (col > row) & (col <= row + BLOCK) & (col + (n - 1) * BLOCK >= FRONT)


def _swa_group(ref, j):
    return jnp.concatenate([ref[:, LANE * (SWA_GROUP * j + g):LANE * (SWA_GROUP * j + g + 1)]
                            for g in range(SWA_GROUP)], axis=0)


def _swa_sinks(sink_ref, j):
    return jnp.concatenate([jnp.full((BLOCK, 1), sink_ref[0, SWA_GROUP * j + g], F32) for g in range(SWA_GROUP)], axis=0)


def _swa_specs():
    prev = lambda n: (jnp.maximum(n - 1, 0), 0)
    cur = lambda n: (n, 0)
    kv = (BLOCK, 2 * LANE)
    return [pl.BlockSpec(memory_space=pltpu.SMEM), pl.BlockSpec((BLOCK, HP), cur),
            pl.BlockSpec(kv, prev), pl.BlockSpec(kv, cur), pl.BlockSpec(kv, prev), pl.BlockSpec(kv, cur)]


def _swa_fwd(sinks, q, k, v):
    t = q.shape[0]

    def body(sink_ref, q_ref, kp_ref, kc_ref, vp_ref, vc_ref, o_ref):
        mask = _swa_mask(pl.program_id(0))
        for j in range(SWA_KV_HEADS):
            sl = slice(LANE * j, LANE * (j + 1))
            k2 = jnp.concatenate([kp_ref[:, sl], kc_ref[:, sl]], axis=0)
            v2 = jnp.concatenate([vp_ref[:, sl], vc_ref[:, sl]], axis=0)
            p, _ = _swa_probs(_swa_group(q_ref, j), k2, _swa_sinks(sink_ref, j), mask)
            o4 = _dot(p.astype(BF16), v2)
            for g in range(SWA_GROUP):
                hd = SWA_GROUP * j + g
                o_ref[:, LANE * hd:LANE * (hd + 1)] = o4[BLOCK * g:BLOCK * (g + 1)]

    return pl.pallas_call(
        body, name="swa_fwd", grid=(t // BLOCK,),
        in_specs=_swa_specs(),
        out_specs=pl.BlockSpec((BLOCK, HP), lambda n: (n, 0)),
        out_shape=jax.ShapeDtypeStruct((t, HP), F32),
        compiler_params=_params("parallel"),
    )(sinks, q, k, k, v, v)


def _causal_mask(q0, k0, tq, tk, transposed):
    if transposed:
        key = k0 + lax.broadcasted_iota(jnp.int32, (tk, tq), 0)
        qry = q0 + lax.broadcasted_iota(jnp.int32, (tk, tq), 1)
    else:
        qry = q0 + lax.broadcasted_iota(jnp.int32, (tq, tk), 0)
        key = k0 + lax.broadcasted_iota(jnp.int32, (tq, tk), 1)
    return (key <= qry) & (key >= FRONT)


def _mla_fwd(q, k, v):
    t = q.shape[0]
    tq = _tile(t)
    nq = t // tq

    def body(q_ref, k_ref, v_ref, o_ref, lse_ref):
        i = pl.program_id(1)
        qv = q_ref[...]

        def step(j, carry, masked):
            m, l, acc = carry
            off = pl.multiple_of(j * tq, tq)
            kj = k_ref[pl.ds(off, tq), :]
            vj = v_ref[pl.ds(off, tq), :]
            s = _dot_nt(qv, kj) * SCALE_B
            if masked:
                s = jnp.where(_causal_mask(i * tq, j * tq, tq, tq, False), s, NEG)
            mn = jnp.maximum(m, jnp.max(s, axis=1, keepdims=True))
            a = jnp.exp(m - mn)
            p = jnp.exp(s - mn)
            return mn, a * l + jnp.sum(p, axis=1, keepdims=True), a * acc + _dot(p.astype(BF16), vj)

        init = (jnp.full((tq, 1), NEG, F32), jnp.zeros((tq, 1), F32), jnp.zeros((tq, LANE), F32))
        carry = lax.fori_loop(0, jnp.minimum(i, 1) + 1, lambda it, c: step(it * i, c, True), init)
        m, l, acc = lax.fori_loop(1, i, lambda j, c: step(j, c, False), carry)
        o_ref[...] = acc / l
        lse_ref[...] = jnp.broadcast_to(m + jnp.log(l), (tq, LANE))

    blk = pl.BlockSpec((tq, LANE), lambda h, i: (i, h))
    full = pl.BlockSpec((t, LANE), lambda h, i: (0, h))
    return pl.pallas_call(
        body, name="mla_fwd", grid=(MLA_HEADS, nq),
        in_specs=[blk, full, full], out_specs=[blk, blk],
        out_shape=[jax.ShapeDtypeStruct((t, HP), F32)] * 2,
        compiler_params=_params("parallel", "parallel"),
    )(q, k, v)


def _mix_fwd(h, oa, ob, ga, gb, wo, g2):
    t = h.shape[0]
    tm = _tile(t)

    def body(h_ref, oa_ref, ob_ref, ga_ref, gb_ref, wo_ref, g2_ref, h2_ref, mix_ref, u2_ref):
        oa_v = oa_ref[...]
        ob_v = ob_ref[...]
        na = (oa_v * _rms_r(oa_v, SWA_Q_W) * ga_ref[...]).astype(BF16)
        nb = (ob_v * _rms_r(ob_v, MLA_OUT_W) * gb_ref[...]).astype(BF16)
        mix_ref[:, :HP] = na
        mix_ref[:, HP:] = nb
        h2 = h_ref[...] + _dot(na, wo_ref[:HP, :]) + _dot(nb, wo_ref[HP:, :])
        h2_ref[...] = h2
        u2_ref[...] = (h2 * _rms_r(h2, D_MODEL) * g2_ref[...]).astype(BF16)

    return pl.pallas_call(
        body, name="mix_fwd", grid=(t // tm,),
        in_specs=[_row(tm, D_MODEL), _row(tm, HP), _row(tm, HP), _const(ga.shape), _const(gb.shape),
                  _const(wo.shape), _const(g2.shape)],
        out_specs=[_row(tm, D_MODEL), _row(tm, 2 * HP), _row(tm, D_MODEL)],
        out_shape=[jax.ShapeDtypeStruct((t, D_MODEL), F32), jax.ShapeDtypeStruct((t, 2 * HP), BF16),
                   jax.ShapeDtypeStruct((t, D_MODEL), BF16)],
        compiler_params=_params("parallel"),
    )(h, oa, ob, ga, gb, wo, g2)


def _ffn_fwd(h2, u2, wg, wu, wd):
    t = h2.shape[0]
    tm = _tile(t)
    dff = wg.shape[1]

    def body(h2_ref, u2_ref, wg_ref, wu_ref, wd_ref, h3_ref, g_ref, up_ref):
        u2v = u2_ref[...]
        g = _dot(u2v, wg_ref[...])
        up = _dot(u2v, wu_ref[...])
        g_ref[...] = g.astype(BF16)
        up_ref[...] = up.astype(BF16)
        a = (g * jax.nn.sigmoid(g) * up).astype(BF16)
        h3_ref[...] = h2_ref[...] + _dot(a, wd_ref[...])

    return pl.pallas_call(
        body, name="ffn_fwd", grid=(t // tm,),
        in_specs=[_row(tm, D_MODEL), _row(tm, D_MODEL), _const(wg.shape), _const(wu.shape), _const(wd.shape)],
        out_specs=[_row(tm, D_MODEL), _row(tm, dff), _row(tm, dff)],
        out_shape=[jax.ShapeDtypeStruct((t, D_MODEL), F32), jax.ShapeDtypeStruct((t, dff), BF16),
                   jax.ShapeDtypeStruct((t, dff), BF16)],
        compiler_params=_params("parallel"),
    )(h2, u2, wg, wu, wd)


def _loss_bwd(h, gf, target):
    t = h.shape[0]
    tm = _tile(t)
    first_row = FRONT + N_META

    def body(h_ref, gf_ref, t_ref, dh_ref, dgf_ref, loss_ref):
        i = pl.program_id(0)
        hv = h_ref[...]
        y = hv * _rms_r(hv, D_MODEL) * gf_ref[...]
        row = i * tm + lax.broadcasted_iota(jnp.int32, (tm, 1), 0)
        err = jnp.where(row >= first_row, y - t_ref[...], 0.0)
        dx, dg = _rms_bwd(hv, gf_ref[...], err * (1.0 / D_MODEL), D_MODEL)
        dh_ref[...] = dx
        _acc(dgf_ref, dg, i == 0)
        part = 0.5 * jnp.sum(jnp.sum(err * err, axis=1, keepdims=True) * (1.0 / D_MODEL), axis=0, keepdims=True)
        _acc(loss_ref, jnp.broadcast_to(part, (1, LANE)), i == 0)

    return pl.pallas_call(
        body, name="loss_bwd", grid=(t // tm,),
        in_specs=[_row(tm, D_MODEL), _const(gf.shape), _row(tm, D_MODEL)],
        out_specs=[_row(tm, D_MODEL), _const((1, D_MODEL)), _const((1, LANE))],
        out_shape=[jax.ShapeDtypeStruct((t, D_MODEL), F32), jax.ShapeDtypeStruct((1, D_MODEL), F32),
                   jax.ShapeDtypeStruct((1, LANE), F32)],
        compiler_params=_params("arbitrary"),
    )(h, gf, target)


def _tn_matmul(a, b, name):
    t, k = a.shape
    n = b.shape[1]
    tk = next(c for c in (k, 1024, 512, 256, 128) if k % c == 0 and c <= 1024)
    fits = lambda c: 2 * (t * (tk + c) * 2 + tk * c * 4) <= TN_VMEM_BUDGET
    tn = next(c for c in (n, 1024, 512, 256, 128) if n % c == 0 and fits(c))

    def body(a_ref, b_ref, o_ref):
        o_ref[...] = _dot_tn(a_ref[...], b_ref[...])

    return pl.pallas_call(
        body, name=name, grid=(k // tk, n // tn),
        in_specs=[pl.BlockSpec((t, tk), lambda i, j: (0, i)), pl.BlockSpec((t, tn), lambda i, j: (0, j))],
        out_specs=pl.BlockSpec((tk, tn), lambda i, j: (i, j)),
        out_shape=jax.ShapeDtypeStruct((k, n), F32),
        compiler_params=_params("parallel", "parallel"),
    )(a, b)


def _ffn_bwd_a(dh3, g, up, wd):
    t = dh3.shape[0]
    tm = _tile(t)
    dff = wd.shape[0]

    def body(dh3_ref, g_ref, up_ref, wd_ref, a_ref, dgu_ref, dh3b_ref):
        dh3b = dh3_ref[...].astype(BF16)
        dh3b_ref[...] = dh3b
        da = _dot_nt(dh3b, wd_ref[...])
        gv = g_ref[...].astype(F32)
        upv = up_ref[...].astype(F32)
        sg = jax.nn.sigmoid(gv)
        silu = gv * sg
        a_ref[...] = (silu * upv).astype(BF16)
        dgu_ref[:, :dff] = (da * upv * (sg * (1.0 + gv * (1.0 - sg)))).astype(BF16)
        dgu_ref[:, dff:] = (da * silu).astype(BF16)

    return pl.pallas_call(
        body, name="ffn_bwd_a", grid=(t // tm,),
        in_specs=[_row(tm, D_MODEL), _row(tm, dff), _row(tm, dff), _const(wd.shape)],
        out_specs=[_row(tm, dff), _row(tm, 2 * dff), _row(tm, D_MODEL)],
        out_shape=[jax.ShapeDtypeStruct((t, dff), BF16), jax.ShapeDtypeStruct((t, 2 * dff), BF16),
                   jax.ShapeDtypeStruct((t, D_MODEL), BF16)],
        compiler_params=_params("parallel"),
    )(dh3, g, up, wd)


def _ffn_bwd_b(dh3, dgu, h2, g2, wg, wu):
    t = dh3.shape[0]
    tm = _tile(t)
    dff = wg.shape[1]

    def body(dh3_ref, dgu_ref, h2_ref, g2_ref, wg_ref, wu_ref, dh2_ref, dh2b_ref, dg2_ref):
        du2 = _dot_nt(dgu_ref[:, :dff], wg_ref[...]) + _dot_nt(dgu_ref[:, dff:], wu_ref[...])
        dx, dg = _rms_bwd(h2_ref[...], g2_ref[...], du2, D_MODEL)
        dh2 = dh3_ref[...] + dx
        dh2_ref[...] = dh2
        dh2b_ref[...] = dh2.astype(BF16)
        _acc(dg2_ref, dg, pl.program_id(0) == 0)

    return pl.pallas_call(
        body, name="ffn_bwd_b", grid=(t // tm,),
        in_specs=[_row(tm, D_MODEL), _row(tm, 2 * dff), _row(tm, D_MODEL), _const(g2.shape), _const(wg.shape),
                  _const(wu.shape)],
        out_specs=[_row(tm, D_MODEL), _row(tm, D_MODEL), _const((1, D_MODEL))],
        out_shape=[jax.ShapeDtypeStruct((t, D_MODEL), F32), jax.ShapeDtypeStruct((t, D_MODEL), BF16),
                   jax.ShapeDtypeStruct((1, D_MODEL), F32)],
        compiler_params=_params("arbitrary"),
    )(dh3, dgu, h2, g2, wg, wu)


def _mix_bwd(dh2, oa, ob, ga, gb, wo):
    t = dh2.shape[0]
    tm = _tile(t)

    def body(dh2_ref, oa_ref, ob_ref, ga_ref, gb_ref, wo_ref, doa_ref, dob_ref, dga_ref, dgb_ref):
        first = pl.program_id(0) == 0
        d = dh2_ref[...].astype(BF16)
        dxa, dga = _rms_bwd(oa_ref[...], ga_ref[...], _dot_nt(d, wo_ref[:HP, :]), SWA_Q_W)
        dxb, dgb = _rms_bwd(ob_ref[...], gb_ref[...], _dot_nt(d, wo_ref[HP:, :]), MLA_OUT_W)
        doa_ref[...] = dxa.astype(BF16)
        dob_ref[...] = dxb.astype(BF16)
        _acc(dga_ref, dga, first)
        _acc(dgb_ref, dgb, first)

    return pl.pallas_call(
        body, name="mix_bwd", grid=(t // tm,),
        in_specs=[_row(tm, D_MODEL), _row(tm, HP), _row(tm, HP), _const(ga.shape), _const(gb.shape), _const(wo.shape)],
        out_specs=[_row(tm, HP), _row(tm, HP), _const((1, HP)), _const((1, HP))],
        out_shape=[jax.ShapeDtypeStruct((t, HP), BF16), jax.ShapeDtypeStruct((t, HP), BF16),
                   jax.ShapeDtypeStruct((1, HP), F32), jax.ShapeDtypeStruct((1, HP), F32)],
        compiler_params=_params("arbitrary"),
    )(dh2, oa, ob, ga, gb, wo)


def _swa_bwd(sinks, q, k, v, o, do):
    t = q.shape[0]

    def body(sink_ref, q_ref, kp_ref, kc_ref, vp_ref, vc_ref, o_ref, do_ref,
             dq_ref, dkc_ref, dkp_ref, dvc_ref, dvp_ref, dsink_ref):
        n = pl.program_id(0)
        mask = _swa_mask(n)
        for j in range(SWA_KV_HEADS):
            sl = slice(LANE * j, LANE * (j + 1))
            k2 = jnp.concatenate([kp_ref[:, sl], kc_ref[:, sl]], axis=0)
            v2 = jnp.concatenate([vp_ref[:, sl], vc_ref[:, sl]], axis=0)
            q4 = _swa_group(q_ref, j)
            do4 = _swa_group(do_ref, j)
            p, psink = _swa_probs(q4, k2, _swa_sinks(sink_ref, j), mask)
            delta = jnp.sum(_swa_group(o_ref, j) * do4.astype(F32), axis=1, keepdims=True)
            ds = p * (_dot_nt(do4, v2) - delta) * SCALE_A
            dq4 = _dot(ds.astype(BF16), k2)
            dk2 = _dot(ds.T.astype(BF16), q4)
            dv2 = _dot(p.T.astype(BF16), do4)
            dsk = -psink * delta
            for g in range(SWA_GROUP):
                hd = SWA_GROUP * j + g
                rows = slice(BLOCK * g, BLOCK * (g + 1))
                dq_ref[:, LANE * hd:LANE * (hd + 1)] = dq4[rows]
                dsink = jnp.broadcast_to(jnp.sum(dsk[rows], axis=0, keepdims=True), (1, LANE))
                _acc(dsink_ref.at[hd:hd + 1, :], dsink, n == 0)
            dkp_ref[:, sl] = dk2[:BLOCK]
            dkc_ref[:, sl] = dk2[BLOCK:]
            dvp_ref[:, sl] = dv2[:BLOCK]
            dvc_ref[:, sl] = dv2[BLOCK:]

    cur = lambda n: (n, 0)
    kv = pl.BlockSpec((BLOCK, 2 * LANE), cur)
    hp = pl.BlockSpec((BLOCK, HP), cur)
    kvs = jax.ShapeDtypeStruct((t, 2 * LANE), F32)
    return pl.pallas_call(
        body, name="swa_bwd", grid=(t // BLOCK,),
        in_specs=_swa_specs() + [hp, hp],
        out_specs=[hp, kv, kv, kv, kv, _const((SWA_HEADS, LANE))],
        out_shape=[jax.ShapeDtypeStruct((t, HP), F32), kvs, kvs, kvs, kvs,
                   jax.ShapeDtypeStruct((SWA_HEADS, LANE), F32)],
        compiler_params=_params("arbitrary"),
    )(sinks, q, k, k, v, v, o, do)


def _mla_bwd_dq(q, k, v, o, do, lse):
    t = q.shape[0]
    tq = _tile(t)
    nq = t // tq

    def body(q_ref, k_ref, v_ref, o_ref, do_ref, lse_ref, dq_ref, dl_ref):
        i = pl.program_id(1)
        qv = q_ref[...]
        dov = do_ref[...]
        delta = jnp.sum(o_ref[...] * dov.astype(F32), axis=1, keepdims=True)
        lse_v = lse_ref[:, :1]

        def step(j, dq, masked):
            off = pl.multiple_of(j * tq, tq)
            kj = k_ref[pl.ds(off, tq), :]
            vj = v_ref[pl.ds(off, tq), :]
            s = _dot_nt(qv, kj) * SCALE_B
            if masked:
                s = jnp.where(_causal_mask(i * tq, j * tq, tq, tq, False), s, NEG)
            ds = jnp.exp(s - lse_v) * (_dot_nt(dov, vj) - delta) * SCALE_B
            return dq + _dot(ds.astype(BF16), kj)

        dq = lax.fori_loop(0, jnp.minimum(i, 1) + 1, lambda it, c: step(it * i, c, True), jnp.zeros((tq, LANE), F32))
        dq_ref[...] = lax.fori_loop(1, i, lambda j, c: step(j, c, False), dq)
        dl_ref[...] = jnp.broadcast_to(delta, (tq, LANE))

    blk = pl.BlockSpec((tq, LANE), lambda h, i: (i, h))
    full = pl.BlockSpec((t, LANE), lambda h, i: (0, h))
    return pl.pallas_call(
        body, name="mla_bwd_dq", grid=(MLA_HEADS, nq),
        in_specs=[blk, full, full, blk, blk, blk], out_specs=[blk, blk],
        out_shape=[jax.ShapeDtypeStruct((t, HP), F32)] * 2,
        compiler_params=_params("parallel", "parallel"),
    )(q, k, v, o, do, lse)


def _mla_bwd_dkv(q, k, v, do, lse_t, dl_t):
    t = q.shape[0]
    tq = _tile(t)
    nq = t // tq

    def body(k_ref, v_ref, q_ref, do_ref, lse_ref, dl_ref, dk_ref, dv_ref):
        hd = pl.program_id(0)
        j = pl.program_id(1)
        kj = k_ref[...]
        vj = v_ref[...]

        def step(i, carry, masked):
            dk, dv = carry
            off = pl.multiple_of(i * tq, tq)
            qi = q_ref[pl.ds(off, tq), :]
            doi = do_ref[pl.ds(off, tq), :]
            st = _dot_nt(kj, qi) * SCALE_B
            if masked:
                st = jnp.where(_causal_mask(i * tq, j * tq, tq, tq, True), st, NEG)
            pt = jnp.exp(st - lse_ref[hd * nq + i])
            dst = pt * (_dot_nt(vj, doi) - dl_ref[hd * nq + i]) * SCALE_B
            return dk + _dot(dst.astype(BF16), qi), dv + _dot(pt.astype(BF16), doi)

        zero = jnp.zeros((tq, LANE), F32)
        split = jnp.where(j == 0, nq, j + 1)
        carry = lax.fori_loop(j, split, lambda i, c: step(i, c, True), (zero, zero))
        dk, dv = lax.fori_loop(split, nq, lambda i, c: step(i, c, False), carry)
        dk_ref[...] = dk
        dv_ref[...] = dv

    blk = pl.BlockSpec((tq, LANE), lambda h, j: (j, h))
    full = pl.BlockSpec((t, LANE), lambda h, j: (0, h))
    rows = pl.BlockSpec((MLA_HEADS * nq, 1, tq), lambda h, j: (0, 0, 0))
    return pl.pallas_call(
        body, name="mla_bwd_dkv", grid=(MLA_HEADS, nq),
        in_specs=[blk, blk, full, full, rows, rows], out_specs=[blk, blk],
        out_shape=[jax.ShapeDtypeStruct((t, HP), F32)] * 2,
        compiler_params=_params("parallel", "parallel"),
    )(k, v, q, do, lse_t, dl_t)


def _pre_bwd(dh2, h, cq, ckv, dqa, dka, dka_next, dva, dva_next, dqb, dkf, dvb, g1, win, gq, wqu, gkv, wkv, tabs):
    t = h.shape[0]
    tm = _tile(t)

    def body(dh2_ref, h_ref, cq_ref, ckv_ref, dqa_ref, dka_ref, dkan_ref, dva_ref, dvan_ref, dqb_ref, dkf_ref, dvb_ref,
             g1_ref, win_ref, gq_ref, wqu_ref, gkv_ref, wkv_ref, tab_ref,
             dh_ref, dp_ref, dqbo_ref, dkvo_ref, dg1_ref, dgq_ref, dgkv_ref):
        first = pl.program_id(0) == 0
        ca, sa1, sa2, cb, sb1, sb2, ck = _tabs(tab_ref)
        dkr = jnp.zeros((tm, LANE), F32)
        for c in range(MLA_HEADS):
            sl = slice(LANE * c, LANE * (c + 1))
            dqbo_ref[:, sl] = _rope_t(dqb_ref[:, sl], cb, sb1, sb2, 16).astype(BF16)
            dkr += dkf_ref[:, sl]
        dkvo_ref[:, :HP] = dkf_ref[...].astype(BF16)
        dkvo_ref[:, HP:] = dvb_ref[...].astype(BF16)
        dcq, dgq = _rms_bwd(cq_ref[...], gq_ref[...], _dot_nt(dqbo_ref[...], wqu_ref[...]), MLA_Q_RANK)
        dckv, dgkv = _rms_bwd(ckv_ref[...], gkv_ref[...], _dot_nt(dkvo_ref[...], wkv_ref[...]), MLA_KV_RANK)
        for c in range(SWA_HEADS):
            sl = slice(LANE * c, LANE * (c + 1))
            dp_ref[:, PO_QA + LANE * c:PO_QA + LANE * (c + 1)] = _rope_t(dqa_ref[:, sl], ca, sa1, sa2, 32).astype(BF16)
        for c in range(SWA_KV_HEADS):
            sl = slice(LANE * c, LANE * (c + 1))
            dk = dka_ref[:, sl] + dkan_ref[:, sl]
            dp_ref[:, PO_KA + LANE * c:PO_KA + LANE * (c + 1)] = _rope_t(dk, ca, sa1, sa2, 32).astype(BF16)
        dp_ref[:, PO_VA:PO_CQ] = (dva_ref[...] + dvan_ref[...]).astype(BF16)
        dp_ref[:, PO_CQ:PO_CKV] = dcq.astype(BF16)
        dp_ref[:, PO_CKV:PO_KR] = dckv.astype(BF16)
        dp_ref[:, PO_KR:PW_IN] = _rope_t(dkr, ck, sb1, sb2, 16).astype(BF16)
        dx, dg1 = _rms_bwd(h_ref[...], g1_ref[...], _dot_nt(dp_ref[...], win_ref[...]), D_MODEL)
        dh_ref[...] = dh2_ref[...] + dx
        _acc(dg1_ref, dg1, first)
        _acc(dgq_ref, dgq, first)
        _acc(dgkv_ref, dgkv, first)

    kv = _row(tm, 2 * LANE)
    return pl.pallas_call(
        body, name="pre_bwd", grid=(t // tm,),
        in_specs=[_row(tm, D_MODEL), _row(tm, D_MODEL), _row(tm, MLA_Q_RANK), _row(tm, MLA_KV_RANK), _row(tm, HP),
                  kv, kv, kv, kv, _row(tm, HP), _row(tm, HP), _row(tm, HP),
                  _const(g1.shape), _const(win.shape), _const(gq.shape), _const(wqu.shape), _const(gkv.shape),
                  _const(wkv.shape), _row(tm, N_TAB * LANE)],
        out_specs=[_row(tm, D_MODEL), _row(tm, PW_IN), _row(tm, HP), _row(tm, 2 * HP),
                   _const((1, D_MODEL)), _const((1, MLA_Q_RANK)), _const((1, MLA_KV_RANK))],
        out_shape=[jax.ShapeDtypeStruct((t, D_MODEL), F32), jax.ShapeDtypeStruct((t, PW_IN), BF16),
                   jax.ShapeDtypeStruct((t, HP), BF16), jax.ShapeDtypeStruct((t, 2 * HP), BF16),
                   jax.ShapeDtypeStruct((1, D_MODEL), F32), jax.ShapeDtypeStruct((1, MLA_Q_RANK), F32),
                   jax.ShapeDtypeStruct((1, MLA_KV_RANK), F32)],
        compiler_params=_params("arbitrary"),
    )(dh2, h, cq, ckv, dqa, dka, dka_next, dva, dva_next, dqb, dkf, dvb, g1, win, gq, wqu, gkv, wkv, tabs)


def _rope_tables(t):
    pos = (jnp.arange(t, dtype=jnp.int32) - FRONT).astype(F32)[:, None]
    lane = jnp.arange(LANE)[None, :]

    def table(dim, start):
        half = dim // 2
        inv = ROPE_THETA ** (-jnp.arange(0, dim, 2, dtype=F32) / dim)
        ang = pos * inv[None, :]
        cos = jnp.concatenate([jnp.cos(ang)] * 2, axis=1)
        sin = jnp.concatenate([jnp.sin(ang)] * 2, axis=1)
        pad = lambda a: jnp.pad(a, ((0, 0), (start, LANE - start - dim)))
        first = (lane >= start) & (lane < start + half)
        second = (lane >= start + half) & (lane < start + dim)
        return pad(cos), jnp.where(first, -pad(sin), 0.0), jnp.where(second, pad(sin), 0.0)

    ca, sa1, sa2 = table(SWA_HEAD_DIM, 0)
    ck, sb1, sb2 = table(MLA_ROPE_DIM, MLA_NOPE_DIM)
    cb = jnp.where(lane < MLA_NOPE_DIM, 1.0, ck)
    return jnp.concatenate([ca, sa1, sa2, cb, sb1, sb2, ck], axis=1)


def _pad_heads(w, heads, dim, axis):
    shp = w.shape
    w = w.reshape(shp[:axis] + (heads, dim) + shp[axis + 1:])
    pad = [(0, 0)] * w.ndim
    pad[axis + 1] = (0, LANE - dim)
    return jnp.pad(w, pad).reshape(shp[:axis] + (heads * LANE,) + shp[axis + 1:])


def _unpad_heads(w, heads, dim, axis):
    shp = w.shape
    w = w.reshape(shp[:axis] + (heads, LANE) + shp[axis + 1:])
    w = lax.slice_in_dim(w, 0, dim, axis=axis + 1)
    return w.reshape(shp[:axis] + (heads * dim,) + shp[axis + 1:])


def _pad_layer(w_in, w_q_up, w_kv_up, w_o, out_norm_swa, out_norm_mla):
    o1 = SWA_Q_W
    o2 = o1 + SWA_KV_W
    o3 = o2 + SWA_KV_W
    o4 = o3 + MLA_Q_RANK
    o5 = o4 + MLA_KV_RANK
    kr = jnp.pad(w_in[:, o5:], ((0, 0), (MLA_NOPE_DIM, LANE - MLA_QK_DIM)))
    win = jnp.concatenate([
        _pad_heads(w_in[:, :o1], SWA_HEADS, SWA_HEAD_DIM, 1),
        _pad_heads(w_in[:, o1:o2], SWA_KV_HEADS, SWA_HEAD_DIM, 1),
        _pad_heads(w_in[:, o2:o3], SWA_KV_HEADS, SWA_HEAD_DIM, 1),
        w_in[:, o3:o5], kr], axis=1)
    wqu = _pad_heads(w_q_up, MLA_HEADS, MLA_QK_DIM, 1)
    kv = w_kv_up.reshape(MLA_KV_RANK, MLA_HEADS, MLA_NOPE_DIM + MLA_V_DIM)
    wkv = jnp.concatenate([
        _pad_heads(kv[:, :, :MLA_NOPE_DIM].reshape(MLA_KV_RANK, -1), MLA_HEADS, MLA_NOPE_DIM, 1),
        _pad_heads(kv[:, :, MLA_NOPE_DIM:].reshape(MLA_KV_RANK, -1), MLA_HEADS, MLA_V_DIM, 1)], axis=1)
    wo = jnp.concatenate([_pad_heads(w_o[:SWA_Q_W], SWA_HEADS, SWA_HEAD_DIM, 0),
                          _pad_heads(w_o[SWA_Q_W:], MLA_HEADS, MLA_V_DIM, 0)], axis=0)
    ga = _pad_heads(out_norm_swa[None, :], SWA_HEADS, SWA_HEAD_DIM, 1)
    gb = _pad_heads(out_norm_mla[None, :], MLA_HEADS, MLA_V_DIM, 1)
    return win, wqu, wkv, wo, ga, gb


def _unpad_layer(dwin, dwqu, dwkv, dwo, dga, dgb):
    d_w_in = jnp.concatenate([
        _unpad_heads(dwin[:, PO_QA:PO_KA], SWA_HEADS, SWA_HEAD_DIM, 1),
        _unpad_heads(dwin[:, PO_KA:PO_VA], SWA_KV_HEADS, SWA_HEAD_DIM, 1),
        _unpad_heads(dwin[:, PO_VA:PO_CQ], SWA_KV_HEADS, SWA_HEAD_DIM, 1),
        dwin[:, PO_CQ:PO_KR], dwin[:, PO_KR + MLA_NOPE_DIM:PO_KR + MLA_QK_DIM]], axis=1)
    d_w_q_up = _unpad_heads(dwqu, MLA_HEADS, MLA_QK_DIM, 1)
    dk = _unpad_heads(dwkv[:, :HP], MLA_HEADS, MLA_NOPE_DIM, 1).reshape(MLA_KV_RANK, MLA_HEADS, MLA_NOPE_DIM)
    dv = _unpad_heads(dwkv[:, HP:], MLA_HEADS, MLA_V_DIM, 1).reshape(MLA_KV_RANK, MLA_HEADS, MLA_V_DIM)
    d_w_kv_up = jnp.concatenate([dk, dv], axis=2).reshape(MLA_KV_RANK, -1)
    d_w_o = jnp.concatenate([_unpad_heads(dwo[:HP], SWA_HEADS, SWA_HEAD_DIM, 0),
                             _unpad_heads(dwo[HP:], MLA_HEADS, MLA_V_DIM, 0)], axis=0)
    d_ga = _unpad_heads(dga, SWA_HEADS, SWA_HEAD_DIM, 1)[0]
    d_gb = _unpad_heads(dgb, MLA_HEADS, MLA_V_DIM, 1)[0]
    return d_w_in, d_w_q_up, d_w_kv_up, d_w_o, d_ga, d_gb


def _shift_up(a):
    return jnp.concatenate([a[BLOCK:], jnp.zeros((BLOCK, a.shape[1]), a.dtype)], axis=0)


def _local_step(x, target, meta, attn_norm, w_in, q_norm, w_q_up, kv_norm, w_kv_up, sinks, out_norm_swa,
                out_norm_mla, w_o, ffn_norm, w_gate, w_up, w_down, final_norm):
    s = x.shape[0]
    depth = w_in.shape[0]
    t = FRONT + N_META + s
    assert t % BLOCK == 0
    tq = _tile(t)
    nq = t // tq
    tabs = _rope_tables(t)
    h = jnp.concatenate([jnp.zeros((FRONT, D_MODEL), F32), meta, x], axis=0)
    tgt = jnp.concatenate([jnp.zeros((FRONT + N_META, D_MODEL), F32), target], axis=0)
    row = lambda v: v[None, :]

    saved = []
    for l in range(depth):
        win, wqu, wkv, wo, ga, gb = _pad_layer(w_in[l], w_q_up[l], w_kv_up[l], w_o[l], out_norm_swa[l], out_norm_mla[l])
        g1, gq, gkv, g2 = row(attn_norm[l]), row(q_norm[l]), row(kv_norm[l]), row(ffn_norm[l])
        sk = row(sinks[l])
        u, qa, ka, va, cq, ckv, qn, kvn, qb, kf, vb = _pre_fwd(h, g1, win, gq, wqu, gkv, wkv, tabs)
        oa = _swa_fwd(sk, qa, ka, va)
        ob, lse = _mla_fwd(qb, kf, vb)
        h2, mix, u2 = _mix_fwd(h, oa, ob, ga, gb, wo, g2)
        h3, gt, up = _ffn_fwd(h2, u2, w_gate[l], w_up[l], w_down[l])
        saved.append((h, u, qa, ka, va, cq, ckv, qn, kvn, qb, kf, vb, oa, ob, lse, h2, mix, u2, gt, up,
                      win, wqu, wkv, wo, ga, gb, g1, gq, gkv, g2, sk))
        h = h3

    dh, d_final, loss = _loss_bwd(h, row(final_norm), tgt)

    grads = []
    to_rows = lambda a: a[:, ::LANE].T.reshape(MLA_HEADS * nq, 1, tq)
    for l in reversed(range(depth)):
        (h0, u, qa, ka, va, cq, ckv, qn, kvn, qb, kf, vb, oa, ob, lse, h2, mix, u2, gt, up,
         win, wqu, wkv, wo, ga, gb, g1, gq, gkv, g2, sk) = saved[l]
        act, dgu, dhb = _ffn_bwd_a(dh, gt, up, w_down[l])
        d_w_down = _tn_matmul(act, dhb, "dw_down")
        d_w_gu = _tn_matmul(u2, dgu, "dw_gate_up")
        dh2, dh2b, d_g2 = _ffn_bwd_b(dh, dgu, h2, g2, w_gate[l], w_up[l])
        d_wo = _tn_matmul(mix, dh2b, "dw_o")
        doa, dob, d_ga, d_gb = _mix_bwd(dh2b, oa, ob, ga, gb, wo)
        dqa, dkc, dkp, dvc, dvp, dsink = _swa_bwd(sk, qa, ka, va, oa, doa)
        dqb, dl = _mla_bwd_dq(qb, kf, vb, ob, dob, lse)
        dkf, dvb = _mla_bwd_dkv(qb, kf, vb, dob, to_rows(lse), to_rows(dl))
        dh, dp, dqbo, dkvo, d_g1, d_gq, d_gkv = _pre_bwd(
            dh2, h0, cq, ckv, dqa, dkc, _shift_up(dkp), dvc, _shift_up(dvp), dqb, dkf, dvb,
            g1, win, gq, wqu, gkv, wkv, tabs)
        d_win = _tn_matmul(u, dp, "dw_in")
        d_wqu = _tn_matmul(qn, dqbo, "dw_q_up")
        d_wkv = _tn_matmul(kvn, dkvo, "dw_kv_up")
        d_w_in, d_w_q_up, d_w_kv_up, d_w_o, d_sw, d_ml = _unpad_layer(d_win, d_wqu, d_wkv, d_wo, d_ga, d_gb)
        dff = w_gate.shape[2]
        grads.append(dict(attn_norm=d_g1[0], w_in=d_w_in, q_norm=d_gq[0], w_q_up=d_w_q_up, kv_norm=d_gkv[0],
                          w_kv_up=d_w_kv_up, sinks=dsink[:, 0], out_norm_swa=d_sw, out_norm_mla=d_ml, w_o=d_w_o,
                          ffn_norm=d_g2[0], w_gate=d_w_gu[:, :dff], w_up=d_w_gu[:, dff:], w_down=d_w_down))
    grads = grads[::-1]
    stacked = {k: jnp.stack([g[k] for g in grads]) for k in grads[0]}
    stacked["final_norm"] = d_final[0]
    return loss[0, 0], dh[FRONT + N_META:], dh[FRONT:FRONT + N_META], stacked


MESH = pl.DeviceIdType.MESH
ANY = pl.BlockSpec(memory_space=pl.ANY)


def _place():
    return lax.axis_index("x"), lax.axis_index("y"), lax.axis_index("c")


def _index(x, y, c):
    return 4 * x + 2 * y + c


def _comm_sems(n):
    return [pltpu.SemaphoreType.DMA((n, N_DEV - 1)), pltpu.SemaphoreType.DMA((n, N_DEV - 1)),
            pltpu.SemaphoreType.DMA((n,))]


def _all_gather(shards, name):
    n = len(shards)

    def body(*refs):
        x_refs, out_refs, (send_sems, recv_sems, local_sems) = refs[:n], refs[n:2 * n], refs[2 * n:]
        x, y, c = _place()
        me, sibling = (x, y, c), (x, y, 1 - c)
        chips = [(1 - x, y), (x, 1 - y), (1 - x, 1 - y)]

        def copy(i, k, block, to, from_input=False):
            slot = out_refs[i].at[_index(*block)]
            return pltpu.make_async_remote_copy(
                src_ref=x_refs[i] if from_input else slot, dst_ref=slot,
                send_sem=send_sems.at[i, k], recv_sem=recv_sems.at[i, k], device_id=to, device_id_type=MESH)

        mine = [pltpu.make_async_copy(x_refs[i], out_refs[i].at[_index(*me)], local_sems.at[i]) for i in range(n)]
        for cp in mine:
            cp.start()
        first = [copy(i, 1 + j, me, (*chip, c), True) for j, chip in enumerate(chips) for i in range(n)]
        first += [copy(i, 0, me, sibling, True) for i in range(n)]
        for cp in first:
            cp.start()
        passed = []
        for j, chip in enumerate(chips):
            for i in range(n):
                copy(i, 1 + j, (*chip, c), me).wait_recv()
                passed.append(copy(i, 4 + j, (*chip, c), sibling))
                passed[-1].start()
        for i in range(n):
            copy(i, 0, sibling, me).wait_recv()
            for j, chip in enumerate(chips):
                copy(i, 4 + j, (*chip, 1 - c), me).wait_recv()
        for cp in first + passed:
            cp.wait_send()
        for cp in mine:
            cp.wait()

    return pl.pallas_call(
        body, name=name, in_specs=[ANY] * n, out_specs=[ANY] * n, scratch_shapes=_comm_sems(n),
        out_shape=[jax.ShapeDtypeStruct((N_DEV,) + a.shape, a.dtype) for a in shards],
    )(*shards)


def _exchange(slabs, name):
    n = len(slabs)

    def body(*refs):
        in_refs, out_refs, (send_sems, recv_sems, local_sems) = refs[:n], refs[n:2 * n], refs[2 * n:]
        x, y, c = _place()
        me = _index(x, y, c)
        mine = [pltpu.make_async_copy(in_refs[i].at[me], out_refs[i].at[me], local_sems.at[i]) for i in range(n)]
        for cp in mine:
            cp.start()
        copies = []
        for k in range(1, N_DEV):
            peer = (1 - x if k & 4 else x, 1 - y if k & 2 else y, 1 - c if k & 1 else c)
            copies += [pltpu.make_async_remote_copy(
                src_ref=in_refs[i].at[_index(*peer)], dst_ref=out_refs[i].at[me],
                send_sem=send_sems.at[i, k - 1], recv_sem=recv_sems.at[i, k - 1], device_id=peer, device_id_type=MESH)
                for i in range(n)]
        for cp in copies:
            cp.start()
        for cp in copies:
            cp.wait_recv()
        for cp in copies:
            cp.wait_send()
        for cp in mine:
            cp.wait()

    return pl.pallas_call(
        body, name=name, in_specs=[ANY] * n, out_specs=[ANY] * n, scratch_shapes=_comm_sems(n),
        out_shape=[jax.ShapeDtypeStruct(a.shape, a.dtype) for a in slabs],
    )(*slabs)


def _adamw(w, g, m, v):
    m = ADAM_B1 * m + (1.0 - ADAM_B1) * g
    v = ADAM_B2 * v + (1.0 - ADAM_B2) * (g * g)
    m_hat = m / (1.0 - ADAM_B1 ** ADAM_STEP)
    v_hat = v / (1.0 - ADAM_B2 ** ADAM_STEP)
    return -ADAM_LR * (m_hat / (jnp.sqrt(v_hat) + ADAM_EPS) + ADAM_WD * w), m, v


def _sum_slots(ref):
    g = ref[0].astype(F32)
    for s in range(1, N_DEV):
        g = g + ref[s].astype(F32)
    return g


def _reduce_adamw(parts, w, m, v, name):
    l, r, c = w.shape
    tile = next(t for t in (256, 128, r) if r % t == 0)

    def body(p_ref, w_ref, m_ref, v_ref, g_ref, d_ref, nm_ref, nv_ref):
        g = _sum_slots(p_ref)
        g_ref[...] = g
        d_ref[...], nm_ref[...], nv_ref[...] = _adamw(w_ref[...], g, m_ref[...], v_ref[...])

    blk = pl.BlockSpec((None, tile, c), lambda i, j: (i, j, 0))
    return pl.pallas_call(
        body, name=name, grid=(l, r // tile),
        in_specs=[pl.BlockSpec((N_DEV, None, tile, c), lambda i, j: (0, i, j, 0)), blk, blk, blk], out_specs=[blk] * 4,
        out_shape=[jax.ShapeDtypeStruct((l, r, c), F32)] * 4,
        compiler_params=_params("parallel", "parallel"),
    )(parts, w, m, v)


def _sum_parts(parts, name):
    _, r, c = parts.shape

    def body(p_ref, g_ref):
        g_ref[...] = _sum_slots(p_ref)

    return pl.pallas_call(body, name=name, out_shape=jax.ShapeDtypeStruct((r, c), F32))(parts)


def _adamw_call(w, g, m, v, name):
    def body(w_ref, g_ref, m_ref, v_ref, d_ref, nm_ref, nv_ref):
        d_ref[...], nm_ref[...], nv_ref[...] = _adamw(w_ref[...], g_ref[...], m_ref[...], v_ref[...])

    return pl.pallas_call(body, name=name, out_shape=[jax.ShapeDtypeStruct(w.shape, F32)] * 3)(w, g, m, v)


BIG = (("w_in", 2), ("w_q_up", 2), ("w_kv_up", 2), ("w_o", 1), ("w_gate", 2), ("w_up", 2), ("w_down", 1))
SMALL = ("attn_norm", "ffn_norm", "final_norm", "out_norm_swa", "out_norm_mla", "q_norm", "kv_norm", "sinks")
PACK_W = 1024
SMALL_ROWS = 16


def _pack(arrs, dtype):
    flat = jnp.concatenate([a.astype(dtype).reshape(-1) for a in arrs])
    return flat.reshape(-1, PACK_W)


def _unpack(packed, like):
    flat = packed.reshape(-1)
    out, off = [], 0
    for a in like:
        out.append(flat[off:off + a.size].reshape(a.shape))
        off += a.size
    return out


def _gather_to_full(gathered, axis):
    shp = list(gathered.shape[1:])
    shp[axis] *= N_DEV
    return jnp.moveaxis(gathered, 0, axis).reshape(shp)


def _full_to_slabs(full, axis, dtype):
    shp = list(full.shape)
    shp[axis:axis + 1] = [N_DEV, shp[axis] // N_DEV]
    return jnp.moveaxis(full.reshape(shp), axis, 0).astype(dtype)


def kernel(x, meta_tokens, attn_norm, w_in, q_norm, w_q_up, kv_norm, w_kv_up, sinks, out_norm_swa, out_norm_mla, w_o, ffn_norm, w_gate, w_up, w_down, final_norm, loss_target, m_meta_tokens, m_attn_norm, m_w_in, m_q_norm, m_w_q_up, m_kv_norm, m_w_kv_up, m_sinks, m_out_norm_swa, m_out_norm_mla, m_w_o, m_ffn_norm, m_w_gate, m_w_up, m_w_down, m_final_norm, v_meta_tokens, v_attn_norm, v_w_in, v_q_norm, v_w_q_up, v_kv_norm, v_w_kv_up, v_sinks, v_out_norm_swa, v_out_norm_mla, v_w_o, v_ffn_norm, v_w_gate, v_w_up, v_w_down, v_final_norm):
    w = dict(meta_tokens=meta_tokens, attn_norm=attn_norm, w_in=w_in, q_norm=q_norm, w_q_up=w_q_up, kv_norm=kv_norm,
             w_kv_up=w_kv_up, sinks=sinks, out_norm_swa=out_norm_swa, out_norm_mla=out_norm_mla, w_o=w_o,
             ffn_norm=ffn_norm, w_gate=w_gate, w_up=w_up, w_down=w_down, final_norm=final_norm)
    m = dict(meta_tokens=m_meta_tokens, attn_norm=m_attn_norm, w_in=m_w_in, q_norm=m_q_norm, w_q_up=m_w_q_up,
             kv_norm=m_kv_norm, w_kv_up=m_w_kv_up, sinks=m_sinks, out_norm_swa=m_out_norm_swa,
             out_norm_mla=m_out_norm_mla, w_o=m_w_o, ffn_norm=m_ffn_norm, w_gate=m_w_gate, w_up=m_w_up,
             w_down=m_w_down, final_norm=m_final_norm)
    v = dict(meta_tokens=v_meta_tokens, attn_norm=v_attn_norm, w_in=v_w_in, q_norm=v_q_norm, w_q_up=v_w_q_up,
             kv_norm=v_kv_norm, w_kv_up=v_w_kv_up, sinks=v_sinks, out_norm_swa=v_out_norm_swa,
             out_norm_mla=v_out_norm_mla, w_o=v_w_o, ffn_norm=v_ffn_norm, w_gate=v_w_gate, w_up=v_w_up,
             w_down=v_w_down, final_norm=v_final_norm)
    names = list(w)
    big = [n for n, _ in BIG]
    me = _index(*_place())

    gathered = _all_gather([w[n].astype(BF16) for n in big], "gather_weights")
    full = {n: _gather_to_full(g, axis) for (n, axis), g in zip(BIG, gathered)}
    meta = jnp.moveaxis(_all_gather([meta_tokens], "gather_meta")[0], 0, 1).reshape(N_META, D_MODEL)

    loss, grad_x, d_meta, grads = _local_step(
        x[0], loss_target[0], meta, attn_norm, full["w_in"], q_norm, full["w_q_up"], kv_norm, full["w_kv_up"], sinks,
        out_norm_swa, out_norm_mla, full["w_o"], ffn_norm, full["w_gate"], full["w_up"], full["w_down"], final_norm)

    parts = _exchange([_full_to_slabs(grads[n], axis, BF16) for n, axis in BIG], "exchange_grads")
    g_big, d_big, m_big, v_big = {}, {}, {}, {}
    for n, p in zip(big, parts):
        g_big[n], d_big[n], m_big[n], v_big[n] = _reduce_adamw(p, w[n], m[n], v[n], "reduce_adamw_" + n)

    small = [grads[n] for n in SMALL] + [loss.reshape(1)]
    pad = SMALL_ROWS * PACK_W - sum(a.size for a in small)
    part = jnp.concatenate([_pack(small + [jnp.zeros((pad,), F32)], F32), d_meta], axis=0)
    total = _sum_parts(_all_gather([part], "gather_small")[0], "sum_small")
    small_w = [w[n] for n in SMALL]
    packs = [_pack([d[n] for n in SMALL] + [jnp.zeros((pad + 1,), F32)], F32) for d in (w, m, v)]
    upd = _adamw_call(packs[0], total[:SMALL_ROWS], packs[1], packs[2], "adamw_small")
    g_small, d_small, m_small, v_small = [dict(zip(SMALL, _unpack(p, small_w))) for p in (total[:SMALL_ROWS],) + tuple(upd)]
    loss_total = total[:SMALL_ROWS].reshape(-1)[SMALL_ROWS * PACK_W - pad - 1]
    g_meta = lax.dynamic_slice_in_dim(total[SMALL_ROWS:], me * LANE, LANE, axis=1)
    d_mt, m_mt, v_mt = _adamw_call(meta_tokens, g_meta, m_meta_tokens, v_meta_tokens, "adamw_meta")

    outs = []
    for got in ({**g_big, **g_small, "meta_tokens": g_meta}, {**d_big, **d_small, "meta_tokens": d_mt},
                {**m_big, **m_small, "meta_tokens": m_mt}, {**v_big, **v_small, "meta_tokens": v_mt}):
        outs += [got[n] for n in names]
    return (loss_total, grad_x[None], *outs)
```

```python
import functools

import jax
import jax.numpy as jnp
from jax import lax
from jax.experimental import pallas as pl
from jax.experimental.pallas import tpu as pltpu

F32 = jnp.float32
BF16 = jnp.bfloat16

D_MODEL = 1024
N_META = 16
BLOCK = 128
FRONT = (-N_META) % BLOCK
ROPE_THETA = 10000.0
EPS = 1e-6
NEG = -1e30
SWA_HEADS = 8
SWA_KV_HEADS = 2
SWA_GROUP = SWA_HEADS // SWA_KV_HEADS
SWA_HEAD_DIM = 64
MLA_HEADS = 8
MLA_Q_RANK = 256
MLA_KV_RANK = 128
MLA_NOPE_DIM = 64
MLA_ROPE_DIM = 32
MLA_V_DIM = 64
MLA_QK_DIM = MLA_NOPE_DIM + MLA_ROPE_DIM
SWA_Q_W = SWA_HEADS * SWA_HEAD_DIM
SWA_KV_W = SWA_KV_HEADS * SWA_HEAD_DIM
MLA_OUT_W = MLA_HEADS * MLA_V_DIM
SCALE_A = SWA_HEAD_DIM ** -0.5
SCALE_B = MLA_QK_DIM ** -0.5
ADAM_LR = 0.001
ADAM_B1 = 0.9
ADAM_B2 = 0.999
ADAM_EPS = 1e-08
ADAM_WD = 0.01
ADAM_STEP = 10

LANE = 128
N_DEV = 8
HP = 8 * LANE
PO_QA, PO_KA, PO_VA = 0, HP, HP + 2 * LANE
PO_CQ = PO_VA + 2 * LANE
PO_CKV = PO_CQ + MLA_Q_RANK
PO_KR = PO_CKV + MLA_KV_RANK
PW_IN = PO_KR + LANE
N_TAB = 7
VMEM_LIMIT = 56 * 2 ** 20
TN_VMEM_BUDGET = 36 * 2 ** 20

NT = (((1,), (1,)), ((), ()))
TN = (((0,), (0,)), ((), ()))


def _tile(t):
    return 384 if t % 384 == 0 else 128


def _params(*sem):
    return pltpu.CompilerParams(dimension_semantics=sem, vmem_limit_bytes=VMEM_LIMIT)


def _row(tm, n):
    return pl.BlockSpec((tm, n), lambda i: (i, 0))


def _const(shape):
    return pl.BlockSpec(shape, lambda i: (0,) * len(shape))


def _dot(a, b):
    return jnp.dot(a, b, preferred_element_type=F32)


def _dot_nt(a, b):
    return lax.dot_general(a, b, NT, preferred_element_type=F32)


def _dot_tn(a, b):
    return lax.dot_general(a, b, TN, preferred_element_type=F32)


def _rope(x, c, s1, s2, shift):
    return x * c + pltpu.roll(x, LANE - shift, 1) * s1 + pltpu.roll(x, shift, 1) * s2


def _rope_t(dy, c, s1, s2, shift):
    return dy * c + pltpu.roll(dy * s1, shift, 1) + pltpu.roll(dy * s2, LANE - shift, 1)


def _rms_r(x, n):
    return lax.rsqrt(jnp.sum(x * x, axis=-1, keepdims=True) * (1.0 / n) + EPS)


def _rms_bwd(x, g, dy, n):
    r = _rms_r(x, n)
    xh = x * r
    dxh = dy * g
    dx = r * (dxh - xh * (jnp.sum(dxh * xh, axis=-1, keepdims=True) * (1.0 / n)))
    return dx, jnp.sum(dy * xh, axis=0, keepdims=True)


def _acc(ref, val, first):
    @pl.when(first)
    def _():
        ref[...] = val

    @pl.when(jnp.logical_not(first))
    def _():
        ref[...] += val


def _tabs(tab_ref):
    return [tab_ref[:, LANE * i:LANE * (i + 1)] for i in range(N_TAB)]


def _pre_fwd(h, g1, win, gq, wqu, gkv, wkv, tabs):
    t = h.shape[0]
    tm = _tile(t)

    def body(h_ref, g1_ref, win_ref, gq_ref, wqu_ref, gkv_ref, wkv_ref, tab_ref,
             u_ref, qa_ref, ka_ref, va_ref, cq_ref, ckv_ref, qn_ref, kvn_ref, qb_ref, kf_ref, vb_ref):
        ca, sa1, sa2, cb, sb1, sb2, ck = _tabs(tab_ref)
        hv = h_ref[...]
        u = (hv * _rms_r(hv, D_MODEL) * g1_ref[...]).astype(BF16)
        u_ref[...] = u
        p = _dot(u, win_ref[...])
        for c in range(SWA_HEADS):
            sl = slice(LANE * c, LANE * (c + 1))
            qa_ref[:, sl] = _rope(p[:, PO_QA + LANE * c:PO_QA + LANE * (c + 1)], ca, sa1, sa2, 32).astype(BF16)
        for c in range(SWA_KV_HEADS):
            sl = slice(LANE * c, LANE * (c + 1))
            ka_ref[:, sl] = _rope(p[:, PO_KA + LANE * c:PO_KA + LANE * (c + 1)], ca, sa1, sa2, 32).astype(BF16)
        va_ref[...] = p[:, PO_VA:PO_CQ].astype(BF16)
        cq = p[:, PO_CQ:PO_CKV]
        ckv = p[:, PO_CKV:PO_KR]
        cq_ref[...] = cq
        ckv_ref[...] = ckv
        qn = (cq * _rms_r(cq, MLA_Q_RANK) * gq_ref[...]).astype(BF16)
        qn_ref[...] = qn
        qb = _dot(qn, wqu_ref[...])
        kvn = (ckv * _rms_r(ckv, MLA_KV_RANK) * gkv_ref[...]).astype(BF16)
        kvn_ref[...] = kvn
        kv = _dot(kvn, wkv_ref[...])
        kr = _rope(p[:, PO_KR:PW_IN], ck, sb1, sb2, 16)
        for c in range(MLA_HEADS):
            sl = slice(LANE * c, LANE * (c + 1))
            qb_ref[:, sl] = _rope(qb[:, sl], cb, sb1, sb2, 16).astype(BF16)
            kf_ref[:, sl] = (kv[:, sl] + kr).astype(BF16)
        vb_ref[...] = kv[:, HP:].astype(BF16)

    widths = [(D_MODEL, BF16), (HP, BF16), (2 * LANE, BF16), (2 * LANE, BF16), (MLA_Q_RANK, F32),
              (MLA_KV_RANK, F32), (MLA_Q_RANK, BF16), (MLA_KV_RANK, BF16), (HP, BF16), (HP, BF16), (HP, BF16)]
    return pl.pallas_call(
        body, name="pre_fwd", grid=(t // tm,),
        in_specs=[_row(tm, D_MODEL), _const(g1.shape), _const(win.shape), _const(gq.shape), _const(wqu.shape),
                  _const(gkv.shape), _const(wkv.shape), _row(tm, N_TAB * LANE)],
        out_specs=[_row(tm, w) for w, _ in widths],
        out_shape=[jax.ShapeDtypeStruct((t, w), d) for w, d in widths],
        compiler_params=_params("parallel"),
    )(h, g1, win, gq, wqu, gkv, wkv, tabs)


def _swa_probs(qh, k2, sink, mask):
    s = jnp.where(mask, _dot_nt(qh, k2) * SCALE_A, NEG)
    m = jnp.maximum(jnp.max(s, axis=1, keepdims=True), sink)
    e = jnp.exp(s - m)
    es = jnp.exp(sink - m)
    inv = 1.0 / (jnp.sum(e, axis=1, keepdims=True) + es)
    return e * inv, es * inv


def _swa_mask(n):
    row = lax.broadcasted_iota(jnp.int32, (SWA_GROUP * BLOCK, 2 * BLOCK), 0) & (BLOCK - 1)
    col = lax.broadcasted_iota(jnp.int32, (SWA_GROUP * BLOCK, 2 * BLOCK), 1)
    return (col > row) & (col <= row + BLOCK) & (col + (n - 1) * BLOCK >= FRONT)


def _swa_group(ref, j):
    return jnp.concatenate([ref[:, LANE * (SWA_GROUP * j + g):LANE * (SWA_GROUP * j + g + 1)]
                            for g in range(SWA_GROUP)], axis=0)


def _swa_sinks(sink_ref, j):
    return jnp.concatenate([jnp.full((BLOCK, 1), sink_ref[0, SWA_GROUP * j + g], F32) for g in range(SWA_GROUP)], axis=0)


def _swa_specs():
    prev = lambda n: (jnp.maximum(n - 1, 0), 0)
    cur = lambda n: (n, 0)
    kv = (BLOCK, 2 * LANE)
    return [pl.BlockSpec(memory_space=pltpu.SMEM), pl.BlockSpec((BLOCK, HP), cur),
            pl.BlockSpec(kv, prev), pl.BlockSpec(kv, cur), pl.BlockSpec(kv, prev), pl.BlockSpec(kv, cur)]


def _swa_fwd(sinks, q, k, v):
    t = q.shape[0]

    def body(sink_ref, q_ref, kp_ref, kc_ref, vp_ref, vc_ref, o_ref):
        mask = _swa_mask(pl.program_id(0))
        for j in range(SWA_KV_HEADS):
            sl = slice(LANE * j, LANE * (j + 1))
            k2 = jnp.concatenate([kp_ref[:, sl], kc_ref[:, sl]], axis=0)
            v2 = jnp.concatenate([vp_ref[:, sl], vc_ref[:, sl]], axis=0)
            p, _ = _swa_probs(_swa_group(q_ref, j), k2, _swa_sinks(sink_ref, j), mask)
            o4 = _dot(p.astype(BF16), v2)
            for g in range(SWA_GROUP):
                hd = SWA_GROUP * j + g
                o_ref[:, LANE * hd:LANE * (hd + 1)] = o4[BLOCK * g:BLOCK * (g + 1)]

    return pl.pallas_call(
        body, name="swa_fwd", grid=(t // BLOCK,),
        in_specs=_swa_specs(),
        out_specs=pl.BlockSpec((BLOCK, HP), lambda n: (n, 0)),
        out_shape=jax.ShapeDtypeStruct((t, HP), F32),
        compiler_params=_params("parallel"),
    )(sinks, q, k, k, v, v)


def _causal_mask(q0, k0, tq, tk, transposed):
    if transposed:
        key = k0 + lax.broadcasted_iota(jnp.int32, (tk, tq), 0)
        qry = q0 + lax.broadcasted_iota(jnp.int32, (tk, tq), 1)
    else:
        qry = q0 + lax.broadcasted_iota(jnp.int32, (tq, tk), 0)
        key = k0 + lax.broadcasted_iota(jnp.int32, (tq, tk), 1)
    return (key <= qry) & (key >= FRONT)


def _mla_fwd(q, k, v, shards=()):
    t = q.shape[0]
    tq = _tile(t)
    nq = t // tq
    n = len(shards)
    steps = MLA_HEADS * nq

    def body(q_ref, k_ref, v_ref, *rest):
        x_refs, (o_ref, lse_ref), out_refs, sems = rest[:n], rest[n:n + 2], rest[n + 2:2 * n + 2], rest[2 * n + 2:]
        i = pl.program_id(1)
        step_id = pl.program_id(0) * nq + i
        if n:
            plan = _gather_plan(x_refs, out_refs, *sems)
            pl.when(step_id == 0)(plan.start)
            pl.when(step_id == steps // 2)(plan.forward)
        qv = q_ref[...]

        def step(j, carry, masked):
            m, l, acc = carry
            off = pl.multiple_of(j * tq, tq)
            kj = k_ref[pl.ds(off, tq), :]
            vj = v_ref[pl.ds(off, tq), :]
            s = _dot_nt(qv, kj) * SCALE_B
            if masked:
                s = jnp.where(_causal_mask(i * tq, j * tq, tq, tq, False), s, NEG)
            mn = jnp.maximum(m, jnp.max(s, axis=1, keepdims=True))
            a = jnp.exp(m - mn)
            p = jnp.exp(s - mn)
            return mn, a * l + jnp.sum(p, axis=1, keepdims=True), a * acc + _dot(p.astype(BF16), vj)

        init = (jnp.full((tq, 1), NEG, F32), jnp.zeros((tq, 1), F32), jnp.zeros((tq, LANE), F32))
        carry = lax.fori_loop(0, jnp.minimum(i, 1) + 1, lambda it, c: step(it * i, c, True), init)
        m, l, acc = lax.fori_loop(1, i, lambda j, c: step(j, c, False), carry)
        o_ref[...] = acc / l
        lse_ref[...] = jnp.broadcast_to(m + jnp.log(l), (tq, LANE))
        if n:
            pl.when(step_id == steps - 1)(plan.finish)

    blk = pl.BlockSpec((tq, LANE), lambda h, i: (i, h))
    full = pl.BlockSpec((t, LANE), lambda h, i: (0, h))
    out = pl.pallas_call(
        body, name="mla_fwd_gather" if n else "mla_fwd", grid=(MLA_HEADS, nq),
        in_specs=[blk, full, full] + [ANY] * n, out_specs=[blk, blk] + [ANY] * n,
        out_shape=[jax.ShapeDtypeStruct((t, HP), F32)] * 2
        + [jax.ShapeDtypeStruct((N_DEV,) + a.shape, a.dtype) for a in shards],
        scratch_shapes=_comm_sems(n) if n else [],
        compiler_params=_params("arbitrary", "arbitrary"),
    )(q, k, v, *shards)
    return out[0], out[1], out[2:]


def _mix_fwd(h, oa, ob, ga, gb, wo, g2):
    t = h.shape[0]
    tm = _tile(t)

    def body(h_ref, oa_ref, ob_ref, ga_ref, gb_ref, wo_ref, g2_ref, h2_ref, mix_ref, u2_ref):
        oa_v = oa_ref[...]
        ob_v = ob_ref[...]
        na = (oa_v * _rms_r(oa_v, SWA_Q_W) * ga_ref[...]).astype(BF16)
        nb = (ob_v * _rms_r(ob_v, MLA_OUT_W) * gb_ref[...]).astype(BF16)
        mix_ref[:, :HP] = na
        mix_ref[:, HP:] = nb
        h2 = h_ref[...] + _dot(na, wo_ref[:HP, :]) + _dot(nb, wo_ref[HP:, :])
        h2_ref[...] = h2
        u2_ref[...] = (h2 * _rms_r(h2, D_MODEL) * g2_ref[...]).astype(BF16)

    return pl.pallas_call(
        body, name="mix_fwd", grid=(t // tm,),
        in_specs=[_row(tm, D_MODEL), _row(tm, HP), _row(tm, HP), _const(ga.shape), _const(gb.shape),
                  _const(wo.shape), _const(g2.shape)],
        out_specs=[_row(tm, D_MODEL), _row(tm, 2 * HP), _row(tm, D_MODEL)],
        out_shape=[jax.ShapeDtypeStruct((t, D_MODEL), F32), jax.ShapeDtypeStruct((t, 2 * HP), BF16),
                   jax.ShapeDtypeStruct((t, D_MODEL), BF16)],
        compiler_params=_params("parallel"),
    )(h, oa, ob, ga, gb, wo, g2)


def _ffn_fwd(h2, u2, wg, wu, wd):
    t = h2.shape[0]
    tm = _tile(t)
    dff = wg.shape[1]

    def body(h2_ref, u2_ref, wg_ref, wu_ref, wd_ref, h3_ref, g_ref, up_ref):
        u2v = u2_ref[...]
        g = _dot(u2v, wg_ref[...])
        up = _dot(u2v, wu_ref[...])
        g_ref[...] = g.astype(BF16)
        up_ref[...] = up.astype(BF16)
        a = (g * jax.nn.sigmoid(g) * up).astype(BF16)
        h3_ref[...] = h2_ref[...] + _dot(a, wd_ref[...])

    return pl.pallas_call(
        body, name="ffn_fwd", grid=(t // tm,),
        in_specs=[_row(tm, D_MODEL), _row(tm, D_MODEL), _const(wg.shape), _const(wu.shape), _const(wd.shape)],
        out_specs=[_row(tm, D_MODEL), _row(tm, dff), _row(tm, dff)],
        out_shape=[jax.ShapeDtypeStruct((t, D_MODEL), F32), jax.ShapeDtypeStruct((t, dff), BF16),
                   jax.ShapeDtypeStruct((t, dff), BF16)],
        compiler_params=_params("parallel"),
    )(h2, u2, wg, wu, wd)


def _loss_bwd(h, gf, target):
    t = h.shape[0]
    tm = _tile(t)
    first_row = FRONT + N_META

    def body(h_ref, gf_ref, t_ref, dh_ref, dgf_ref, loss_ref):
        i = pl.program_id(0)
        hv = h_ref[...]
        y = hv * _rms_r(hv, D_MODEL) * gf_ref[...]
        row = i * tm + lax.broadcasted_iota(jnp.int32, (tm, 1), 0)
        err = jnp.where(row >= first_row, y - t_ref[...], 0.0)
        dx, dg = _rms_bwd(hv, gf_ref[...], err * (1.0 / D_MODEL), D_MODEL)
        dh_ref[...] = dx
        _acc(dgf_ref, dg, i == 0)
        part = 0.5 * jnp.sum(jnp.sum(err * err, axis=1, keepdims=True) * (1.0 / D_MODEL), axis=0, keepdims=True)
        _acc(loss_ref, jnp.broadcast_to(part, (1, LANE)), i == 0)

    return pl.pallas_call(
        body, name="loss_bwd", grid=(t // tm,),
        in_specs=[_row(tm, D_MODEL), _const(gf.shape), _row(tm, D_MODEL)],
        out_specs=[_row(tm, D_MODEL), _const((1, D_MODEL)), _const((1, LANE))],
        out_shape=[jax.ShapeDtypeStruct((t, D_MODEL), F32), jax.ShapeDtypeStruct((1, D_MODEL), F32),
                   jax.ShapeDtypeStruct((1, LANE), F32)],
        compiler_params=_params("arbitrary"),
    )(h, gf, target)


def _tn_matmul(a, b, name):
    t, k = a.shape
    n = b.shape[1]
    tk = next(c for c in (k, 1024, 512, 256, 128) if k % c == 0 and c <= 1024)
    fits = lambda c: 2 * (t * (tk + c) * 2 + tk * c * 4) <= TN_VMEM_BUDGET
    tn = next(c for c in (n, 1024, 512, 256, 128) if n % c == 0 and fits(c))

    def body(a_ref, b_ref, o_ref):
        o_ref[...] = _dot_tn(a_ref[...], b_ref[...])

    return pl.pallas_call(
        body, name=name, grid=(k // tk, n // tn),
        in_specs=[pl.BlockSpec((t, tk), lambda i, j: (0, i)), pl.BlockSpec((t, tn), lambda i, j: (0, j))],
        out_specs=pl.BlockSpec((tk, tn), lambda i, j: (i, j)),
        out_shape=jax.ShapeDtypeStruct((k, n), F32),
        compiler_params=_params("parallel", "parallel"),
    )(a, b)


def _ffn_bwd_a(dh3, g, up, wd):
    t = dh3.shape[0]
    tm = _tile(t)
    dff = wd.shape[0]

    def body(dh3_ref, g_ref, up_ref, wd_ref, a_ref, dgu_ref, dh3b_ref):
        dh3b = dh3_ref[...].astype(BF16)
        dh3b_ref[...] = dh3b
        da = _dot_nt(dh3b, wd_ref[...])
        gv = g_ref[...].astype(F32)
        upv = up_ref[...].astype(F32)
        sg = jax.nn.sigmoid(gv)
        silu = gv * sg
        a_ref[...] = (silu * upv).astype(BF16)
        dgu_ref[:, :dff] = (da * upv * (sg * (1.0 + gv * (1.0 - sg)))).astype(BF16)
        dgu_ref[:, dff:] = (da * silu).astype(BF16)

    return pl.pallas_call(
        body, name="ffn_bwd_a", grid=(t // tm,),
        in_specs=[_row(tm, D_MODEL), _row(tm, dff), _row(tm, dff), _const(wd.shape)],
        out_specs=[_row(tm, dff), _row(tm, 2 * dff), _row(tm, D_MODEL)],
        out_shape=[jax.ShapeDtypeStruct((t, dff), BF16), jax.ShapeDtypeStruct((t, 2 * dff), BF16),
                   jax.ShapeDtypeStruct((t, D_MODEL), BF16)],
        compiler_params=_params("parallel"),
    )(dh3, g, up, wd)


def _ffn_bwd_b(dh3, dgu, h2, g2, wg, wu):
    t = dh3.shape[0]
    tm = _tile(t)
    dff = wg.shape[1]

    def body(dh3_ref, dgu_ref, h2_ref, g2_ref, wg_ref, wu_ref, dh2_ref, dh2b_ref, dg2_ref):
        du2 = _dot_nt(dgu_ref[:, :dff], wg_ref[...]) + _dot_nt(dgu_ref[:, dff:], wu_ref[...])
        dx, dg = _rms_bwd(h2_ref[...], g2_ref[...], du2, D_MODEL)
        dh2 = dh3_ref[...] + dx
        dh2_ref[...] = dh2
        dh2b_ref[...] = dh2.astype(BF16)
        _acc(dg2_ref, dg, pl.program_id(0) == 0)

    return pl.pallas_call(
        body, name="ffn_bwd_b", grid=(t // tm,),
        in_specs=[_row(tm, D_MODEL), _row(tm, 2 * dff), _row(tm, D_MODEL), _const(g2.shape), _const(wg.shape),
                  _const(wu.shape)],
        out_specs=[_row(tm, D_MODEL), _row(tm, D_MODEL), _const((1, D_MODEL))],
        out_shape=[jax.ShapeDtypeStruct((t, D_MODEL), F32), jax.ShapeDtypeStruct((t, D_MODEL), BF16),
                   jax.ShapeDtypeStruct((1, D_MODEL), F32)],
        compiler_params=_params("arbitrary"),
    )(dh3, dgu, h2, g2, wg, wu)


def _mix_bwd(dh2, oa, ob, ga, gb, wo):
    t = dh2.shape[0]
    tm = _tile(t)

    def body(dh2_ref, oa_ref, ob_ref, ga_ref, gb_ref, wo_ref, doa_ref, dob_ref, dga_ref, dgb_ref):
        first = pl.program_id(0) == 0
        d = dh2_ref[...].astype(BF16)
        dxa, dga = _rms_bwd(oa_ref[...], ga_ref[...], _dot_nt(d, wo_ref[:HP, :]), SWA_Q_W)
        dxb, dgb = _rms_bwd(ob_ref[...], gb_ref[...], _dot_nt(d, wo_ref[HP:, :]), MLA_OUT_W)
        doa_ref[...] = dxa.astype(BF16)
        dob_ref[...] = dxb.astype(BF16)
        _acc(dga_ref, dga, first)
        _acc(dgb_ref, dgb, first)

    return pl.pallas_call(
        body, name="mix_bwd", grid=(t // tm,),
        in_specs=[_row(tm, D_MODEL), _row(tm, HP), _row(tm, HP), _const(ga.shape), _const(gb.shape), _const(wo.shape)],
        out_specs=[_row(tm, HP), _row(tm, HP), _const((1, HP)), _const((1, HP))],
        out_shape=[jax.ShapeDtypeStruct((t, HP), BF16), jax.ShapeDtypeStruct((t, HP), BF16),
                   jax.ShapeDtypeStruct((1, HP), F32), jax.ShapeDtypeStruct((1, HP), F32)],
        compiler_params=_params("arbitrary"),
    )(dh2, oa, ob, ga, gb, wo)


def _swa_bwd(sinks, q, k, v, o, do):
    t = q.shape[0]

    def body(sink_ref, q_ref, kp_ref, kc_ref, vp_ref, vc_ref, o_ref, do_ref,
             dq_ref, dkc_ref, dkp_ref, dvc_ref, dvp_ref, dsink_ref):
        n = pl.program_id(0)
        mask = _swa_mask(n)
        for j in range(SWA_KV_HEADS):
            sl = slice(LANE * j, LANE * (j + 1))
            k2 = jnp.concatenate([kp_ref[:, sl], kc_ref[:, sl]], axis=0)
            v2 = jnp.concatenate([vp_ref[:, sl], vc_ref[:, sl]], axis=0)
            q4 = _swa_group(q_ref, j)
            do4 = _swa_group(do_ref, j)
            p, psink = _swa_probs(q4, k2, _swa_sinks(sink_ref, j), mask)
            delta = jnp.sum(_swa_group(o_ref, j) * do4.astype(F32), axis=1, keepdims=True)
            ds = p * (_dot_nt(do4, v2) - delta) * SCALE_A
            dq4 = _dot(ds.astype(BF16), k2)
            dk2 = _dot(ds.T.astype(BF16), q4)
            dv2 = _dot(p.T.astype(BF16), do4)
            dsk = -psink * delta
            for g in range(SWA_GROUP):
                hd = SWA_GROUP * j + g
                rows = slice(BLOCK * g, BLOCK * (g + 1))
                dq_ref[:, LANE * hd:LANE * (hd + 1)] = dq4[rows]
                dsink = jnp.broadcast_to(jnp.sum(dsk[rows], axis=0, keepdims=True), (1, LANE))
                _acc(dsink_ref.at[hd:hd + 1, :], dsink, n == 0)
            dkp_ref[:, sl] = dk2[:BLOCK]
            dkc_ref[:, sl] = dk2[BLOCK:]
            dvp_ref[:, sl] = dv2[:BLOCK]
            dvc_ref[:, sl] = dv2[BLOCK:]

    cur = lambda n: (n, 0)
    kv = pl.BlockSpec((BLOCK, 2 * LANE), cur)
    hp = pl.BlockSpec((BLOCK, HP), cur)
    kvs = jax.ShapeDtypeStruct((t, 2 * LANE), F32)
    return pl.pallas_call(
        body, name="swa_bwd", grid=(t // BLOCK,),
        in_specs=_swa_specs() + [hp, hp],
        out_specs=[hp, kv, kv, kv, kv, _const((SWA_HEADS, LANE))],
        out_shape=[jax.ShapeDtypeStruct((t, HP), F32), kvs, kvs, kvs, kvs,
                   jax.ShapeDtypeStruct((SWA_HEADS, LANE), F32)],
        compiler_params=_params("arbitrary"),
    )(sinks, q, k, k, v, v, o, do)


def _mla_bwd_dq(q, k, v, o, do, lse):
    t = q.shape[0]
    tq = _tile(t)
    nq = t // tq

    def body(q_ref, k_ref, v_ref, o_ref, do_ref, lse_ref, dq_ref, dl_ref):
        i = pl.program_id(1)
        qv = q_ref[...]
        dov = do_ref[...]
        delta = jnp.sum(o_ref[...] * dov.astype(F32), axis=1, keepdims=True)
        lse_v = lse_ref[:, :1]

        def step(j, dq, masked):
            off = pl.multiple_of(j * tq, tq)
            kj = k_ref[pl.ds(off, tq), :]
            vj = v_ref[pl.ds(off, tq), :]
            s = _dot_nt(qv, kj) * SCALE_B
            if masked:
                s = jnp.where(_causal_mask(i * tq, j * tq, tq, tq, False), s, NEG)
            ds = jnp.exp(s - lse_v) * (_dot_nt(dov, vj) - delta) * SCALE_B
            return dq + _dot(ds.astype(BF16), kj)

        dq = lax.fori_loop(0, jnp.minimum(i, 1) + 1, lambda it, c: step(it * i, c, True), jnp.zeros((tq, LANE), F32))
        dq_ref[...] = lax.fori_loop(1, i, lambda j, c: step(j, c, False), dq)
        dl_ref[...] = jnp.broadcast_to(delta, (tq, LANE))

    blk = pl.BlockSpec((tq, LANE), lambda h, i: (i, h))
    full = pl.BlockSpec((t, LANE), lambda h, i: (0, h))
    return pl.pallas_call(
        body, name="mla_bwd_dq", grid=(MLA_HEADS, nq),
        in_specs=[blk, full, full, blk, blk, blk], out_specs=[blk, blk],
        out_shape=[jax.ShapeDtypeStruct((t, HP), F32)] * 2,
        compiler_params=_params("parallel", "parallel"),
    )(q, k, v, o, do, lse)


def _mla_bwd_dkv(q, k, v, do, lse_t, dl_t, slabs=()):
    t = q.shape[0]
    tq = _tile(t)
    nq = t // tq
    n = len(slabs)
    steps = MLA_HEADS * nq

    def body(k_ref, v_ref, q_ref, do_ref, lse_ref, dl_ref, *rest):
        in_refs, (dk_ref, dv_ref), out_refs, sems = rest[:n], rest[n:n + 2], rest[n + 2:2 * n + 2], rest[2 * n + 2:]
        hd = pl.program_id(0)
        j = pl.program_id(1)
        step_id = hd * nq + j
        if n:
            plan = _exchange_plan(in_refs, out_refs, *sems)
            pl.when(step_id == 0)(plan.start)
        kj = k_ref[...]
        vj = v_ref[...]

        def step(i, carry, masked):
            dk, dv = carry
            off = pl.multiple_of(i * tq, tq)
            qi = q_ref[pl.ds(off, tq), :]
            doi = do_ref[pl.ds(off, tq), :]
            st = _dot_nt(kj, qi) * SCALE_B
            if masked:
                st = jnp.where(_causal_mask(i * tq, j * tq, tq, tq, True), st, NEG)
            pt = jnp.exp(st - lse_ref[hd * nq + i])
            dst = pt * (_dot_nt(vj, doi) - dl_ref[hd * nq + i]) * SCALE_B
            return dk + _dot(dst.astype(BF16), qi), dv + _dot(pt.astype(BF16), doi)

        zero = jnp.zeros((tq, LANE), F32)
        split = jnp.where(j == 0, nq, j + 1)
        carry = lax.fori_loop(j, split, lambda i, c: step(i, c, True), (zero, zero))
        dk, dv = lax.fori_loop(split, nq, lambda i, c: step(i, c, False), carry)
        dk_ref[...] = dk
        dv_ref[...] = dv
        if n:
            pl.when(step_id == steps - 1)(plan.finish)

    blk = pl.BlockSpec((tq, LANE), lambda h, j: (j, h))
    full = pl.BlockSpec((t, LANE), lambda h, j: (0, h))
    rows = pl.BlockSpec((MLA_HEADS * nq, 1, tq), lambda h, j: (0, 0, 0))
    out = pl.pallas_call(
        body, name="mla_bwd_dkv_exchange" if n else "mla_bwd_dkv", grid=(MLA_HEADS, nq),
        in_specs=[blk, blk, full, full, rows, rows] + [ANY] * n, out_specs=[blk, blk] + [ANY] * n,
        out_shape=[jax.ShapeDtypeStruct((t, HP), F32)] * 2 + [jax.ShapeDtypeStruct(a.shape, a.dtype) for a in slabs],
        scratch_shapes=_comm_sems(n) if n else [],
        compiler_params=_params("arbitrary", "arbitrary"),
    )(k, v, q, do, lse_t, dl_t, *slabs)
    return out[0], out[1], out[2:]


def _pre_bwd(dh2, h, cq, ckv, dqa, dka, dka_next, dva, dva_next, dqb, dkf, dvb, g1, win, gq, wqu, gkv, wkv, tabs):
    t = h.shape[0]
    tm = _tile(t)

    def body(dh2_ref, h_ref, cq_ref, ckv_ref, dqa_ref, dka_ref, dkan_ref, dva_ref, dvan_ref, dqb_ref, dkf_ref, dvb_ref,
             g1_ref, win_ref, gq_ref, wqu_ref, gkv_ref, wkv_ref, tab_ref,
             dh_ref, dp_ref, dqbo_ref, dkvo_ref, dg1_ref, dgq_ref, dgkv_ref):
        first = pl.program_id(0) == 0
        ca, sa1, sa2, cb, sb1, sb2, ck = _tabs(tab_ref)
        dkr = jnp.zeros((tm, LANE), F32)
        for c in range(MLA_HEADS):
            sl = slice(LANE * c, LANE * (c + 1))
            dqbo_ref[:, sl] = _rope_t(dqb_ref[:, sl], cb, sb1, sb2, 16).astype(BF16)
            dkr += dkf_ref[:, sl]
        dkvo_ref[:, :HP] = dkf_ref[...].astype(BF16)
        dkvo_ref[:, HP:] = dvb_ref[...].astype(BF16)
        dcq, dgq = _rms_bwd(cq_ref[...], gq_ref[...], _dot_nt(dqbo_ref[...], wqu_ref[...]), MLA_Q_RANK)
        dckv, dgkv = _rms_bwd(ckv_ref[...], gkv_ref[...], _dot_nt(dkvo_ref[...], wkv_ref[...]), MLA_KV_RANK)
        for c in range(SWA_HEADS):
            sl = slice(LANE * c, LANE * (c + 1))
            dp_ref[:, PO_QA + LANE * c:PO_QA + LANE * (c + 1)] = _rope_t(dqa_ref[:, sl], ca, sa1, sa2, 32).astype(BF16)
        for c in range(SWA_KV_HEADS):
            sl = slice(LANE * c, LANE * (c + 1))
            dk = dka_ref[:, sl] + dkan_ref[:, sl]
            dp_ref[:, PO_KA + LANE * c:PO_KA + LANE * (c + 1)] = _rope_t(dk, ca, sa1, sa2, 32).astype(BF16)
        dp_ref[:, PO_VA:PO_CQ] = (dva_ref[...] + dvan_ref[...]).astype(BF16)
        dp_ref[:, PO_CQ:PO_CKV] = dcq.astype(BF16)
        dp_ref[:, PO_CKV:PO_KR] = dckv.astype(BF16)
        dp_ref[:, PO_KR:PW_IN] = _rope_t(dkr, ck, sb1, sb2, 16).astype(BF16)
        dx, dg1 = _rms_bwd(h_ref[...], g1_ref[...], _dot_nt(dp_ref[...], win_ref[...]), D_MODEL)
        dh_ref[...] = dh2_ref[...] + dx
        _acc(dg1_ref, dg1, first)
        _acc(dgq_ref, dgq, first)
        _acc(dgkv_ref, dgkv, first)

    kv = _row(tm, 2 * LANE)
    return pl.pallas_call(
        body, name="pre_bwd", grid=(t // tm,),
        in_specs=[_row(tm, D_MODEL), _row(tm, D_MODEL), _row(tm, MLA_Q_RANK), _row(tm, MLA_KV_RANK), _row(tm, HP),
                  kv, kv, kv, kv, _row(tm, HP), _row(tm, HP), _row(tm, HP),
                  _const(g1.shape), _const(win.shape), _const(gq.shape), _const(wqu.shape), _const(gkv.shape),
                  _const(wkv.shape), _row(tm, N_TAB * LANE)],
        out_specs=[_row(tm, D_MODEL), _row(tm, PW_IN), _row(tm, HP), _row(tm, 2 * HP),
                   _const((1, D_MODEL)), _const((1, MLA_Q_RANK)), _const((1, MLA_KV_RANK))],
        out_shape=[jax.ShapeDtypeStruct((t, D_MODEL), F32), jax.ShapeDtypeStruct((t, PW_IN), BF16),
                   jax.ShapeDtypeStruct((t, HP), BF16), jax.ShapeDtypeStruct((t, 2 * HP), BF16),
                   jax.ShapeDtypeStruct((1, D_MODEL), F32), jax.ShapeDtypeStruct((1, MLA_Q_RANK), F32),
                   jax.ShapeDtypeStruct((1, MLA_KV_RANK), F32)],
        compiler_params=_params("arbitrary"),
    )(dh2, h, cq, ckv, dqa, dka, dka_next, dva, dva_next, dqb, dkf, dvb, g1, win, gq, wqu, gkv, wkv, tabs)


def _rope_tables(t):
    pos = (jnp.arange(t, dtype=jnp.int32) - FRONT).astype(F32)[:, None]
    lane = jnp.arange(LANE)[None, :]

    def table(dim, start):
        half = dim // 2
        inv = ROPE_THETA ** (-jnp.arange(0, dim, 2, dtype=F32) / dim)
        ang = pos * inv[None, :]
        cos = jnp.concatenate([jnp.cos(ang)] * 2, axis=1)
        sin = jnp.concatenate([jnp.sin(ang)] * 2, axis=1)
        pad = lambda a: jnp.pad(a, ((0, 0), (start, LANE - start - dim)))
        first = (lane >= start) & (lane < start + half)
        second = (lane >= start + half) & (lane < start + dim)
        return pad(cos), jnp.where(first, -pad(sin), 0.0), jnp.where(second, pad(sin), 0.0)

    ca, sa1, sa2 = table(SWA_HEAD_DIM, 0)
    ck, sb1, sb2 = table(MLA_ROPE_DIM, MLA_NOPE_DIM)
    cb = jnp.where(lane < MLA_NOPE_DIM, 1.0, ck)
    return jnp.concatenate([ca, sa1, sa2, cb, sb1, sb2, ck], axis=1)


def _pad_heads(w, heads, dim, axis):
    shp = w.shape
    w = w.reshape(shp[:axis] + (heads, dim) + shp[axis + 1:])
    pad = [(0, 0)] * w.ndim
    pad[axis + 1] = (0, LANE - dim)
    return jnp.pad(w, pad).reshape(shp[:axis] + (heads * LANE,) + shp[axis + 1:])


def _unpad_heads(w, heads, dim, axis):
    shp = w.shape
    w = w.reshape(shp[:axis] + (heads, LANE) + shp[axis + 1:])
    w = lax.slice_in_dim(w, 0, dim, axis=axis + 1)
    return w.reshape(shp[:axis] + (heads * dim,) + shp[axis + 1:])


def _pad_layer(w_in, w_q_up, w_kv_up, w_o, out_norm_swa, out_norm_mla):
    o1 = SWA_Q_W
    o2 = o1 + SWA_KV_W
    o3 = o2 + SWA_KV_W
    o4 = o3 + MLA_Q_RANK
    o5 = o4 + MLA_KV_RANK
    kr = jnp.pad(w_in[:, o5:], ((0, 0), (MLA_NOPE_DIM, LANE - MLA_QK_DIM)))
    win = jnp.concatenate([
        _pad_heads(w_in[:, :o1], SWA_HEADS, SWA_HEAD_DIM, 1),
        _pad_heads(w_in[:, o1:o2], SWA_KV_HEADS, SWA_HEAD_DIM, 1),
        _pad_heads(w_in[:, o2:o3], SWA_KV_HEADS, SWA_HEAD_DIM, 1),
        w_in[:, o3:o5], kr], axis=1)
    wqu = _pad_heads(w_q_up, MLA_HEADS, MLA_QK_DIM, 1)
    kv = w_kv_up.reshape(MLA_KV_RANK, MLA_HEADS, MLA_NOPE_DIM + MLA_V_DIM)
    wkv = jnp.concatenate([
        _pad_heads(kv[:, :, :MLA_NOPE_DIM].reshape(MLA_KV_RANK, -1), MLA_HEADS, MLA_NOPE_DIM, 1),
        _pad_heads(kv[:, :, MLA_NOPE_DIM:].reshape(MLA_KV_RANK, -1), MLA_HEADS, MLA_V_DIM, 1)], axis=1)
    wo = jnp.concatenate([_pad_heads(w_o[:SWA_Q_W], SWA_HEADS, SWA_HEAD_DIM, 0),
                          _pad_heads(w_o[SWA_Q_W:], MLA_HEADS, MLA_V_DIM, 0)], axis=0)
    ga = _pad_heads(out_norm_swa[None, :], SWA_HEADS, SWA_HEAD_DIM, 1)
    gb = _pad_heads(out_norm_mla[None, :], MLA_HEADS, MLA_V_DIM, 1)
    return win, wqu, wkv, wo, ga, gb


def _unpad_layer(dwin, dwqu, dwkv, dwo, dga, dgb):
    d_w_in = jnp.concatenate([
        _unpad_heads(dwin[:, PO_QA:PO_KA], SWA_HEADS, SWA_HEAD_DIM, 1),
        _unpad_heads(dwin[:, PO_KA:PO_VA], SWA_KV_HEADS, SWA_HEAD_DIM, 1),
        _unpad_heads(dwin[:, PO_VA:PO_CQ], SWA_KV_HEADS, SWA_HEAD_DIM, 1),
        dwin[:, PO_CQ:PO_KR], dwin[:, PO_KR + MLA_NOPE_DIM:PO_KR + MLA_QK_DIM]], axis=1)
    d_w_q_up = _unpad_heads(dwqu, MLA_HEADS, MLA_QK_DIM, 1)
    dk = _unpad_heads(dwkv[:, :HP], MLA_HEADS, MLA_NOPE_DIM, 1).reshape(MLA_KV_RANK, MLA_HEADS, MLA_NOPE_DIM)
    dv = _unpad_heads(dwkv[:, HP:], MLA_HEADS, MLA_V_DIM, 1).reshape(MLA_KV_RANK, MLA_HEADS, MLA_V_DIM)
    d_w_kv_up = jnp.concatenate([dk, dv], axis=2).reshape(MLA_KV_RANK, -1)
    d_w_o = jnp.concatenate([_unpad_heads(dwo[:HP], SWA_HEADS, SWA_HEAD_DIM, 0),
                             _unpad_heads(dwo[HP:], MLA_HEADS, MLA_V_DIM, 0)], axis=0)
    d_ga = _unpad_heads(dga, SWA_HEADS, SWA_HEAD_DIM, 1)[0]
    d_gb = _unpad_heads(dgb, MLA_HEADS, MLA_V_DIM, 1)[0]
    return d_w_in, d_w_q_up, d_w_kv_up, d_w_o, d_ga, d_gb


def _shift_up(a):
    return jnp.concatenate([a[BLOCK:], jnp.zeros((BLOCK, a.shape[1]), a.dtype)], axis=0)


def _train_example(x, target, meta, vec, weights):
    s = x.shape[0]
    depth = vec["attn_norm"].shape[0]
    t = FRONT + N_META + s
    assert t % BLOCK == 0
    tq = _tile(t)
    nq = t // tq
    tabs = _rope_tables(t)
    h = jnp.concatenate([jnp.zeros((FRONT, D_MODEL), F32), meta, x], axis=0)
    tgt = jnp.concatenate([jnp.zeros((FRONT + N_META, D_MODEL), F32), target], axis=0)
    row = lambda v: v[None, :]

    saved = []
    for l in range(depth):
        win, wqu, wkv, wo, ga, gb = _pad_layer(*weights.attn(l), vec["out_norm_swa"][l], vec["out_norm_mla"][l])
        g1, gq, gkv, g2 = (row(vec[n][l]) for n in ("attn_norm", "q_norm", "kv_norm", "ffn_norm"))
        sk = row(vec["sinks"][l])
        u, qa, ka, va, cq, ckv, qn, kvn, qb, kf, vb = _pre_fwd(h, g1, win, gq, wqu, gkv, wkv, tabs)
        oa = _swa_fwd(sk, qa, ka, va)
        ob, lse = weights.mla_fwd(l, qb, kf, vb)
        h2, mix, u2 = _mix_fwd(h, oa, ob, ga, gb, wo, g2)
        wg, wu, wd = weights.ffn(l)
        h3, gt, up = _ffn_fwd(h2, u2, wg, wu, wd)
        saved.append((h, u, qa, ka, va, cq, ckv, qn, kvn, qb, kf, vb, oa, ob, lse, h2, mix, u2, gt, up,
                      win, wqu, wkv, wo, ga, gb, g1, gq, gkv, g2, sk, wg, wu, wd))
        h = h3

    dh, d_final, loss = _loss_bwd(h, row(vec["final_norm"]), tgt)

    grads = []
    to_rows = lambda a: a[:, ::LANE].T.reshape(MLA_HEADS * nq, 1, tq)
    for l in reversed(range(depth)):
        (h0, u, qa, ka, va, cq, ckv, qn, kvn, qb, kf, vb, oa, ob, lse, h2, mix, u2, gt, up,
         win, wqu, wkv, wo, ga, gb, g1, gq, gkv, g2, sk, wg, wu, wd) = saved[l]
        dff = wg.shape[1]
        act, dgu, dhb = _ffn_bwd_a(dh, gt, up, wd)
        d_w_gu = _tn_matmul(u2, dgu, "dw_gate_up")
        weights.ffn_grads(l, d_w_gu[:, :dff], d_w_gu[:, dff:], _tn_matmul(act, dhb, "dw_down"))
        dh2, dh2b, d_g2 = _ffn_bwd_b(dh, dgu, h2, g2, wg, wu)
        d_wo = _tn_matmul(mix, dh2b, "dw_o")
        doa, dob, d_ga, d_gb = _mix_bwd(dh2b, oa, ob, ga, gb, wo)
        dqa, dkc, dkp, dvc, dvp, dsink = _swa_bwd(sk, qa, ka, va, oa, doa)
        dqb, dl = _mla_bwd_dq(qb, kf, vb, ob, dob, lse)
        dkf, dvb = weights.mla_bwd_dkv(l, qb, kf, vb, dob, to_rows(lse), to_rows(dl))
        dh, dp, dqbo, dkvo, d_g1, d_gq, d_gkv = _pre_bwd(
            dh2, h0, cq, ckv, dqa, dkc, _shift_up(dkp), dvc, _shift_up(dvp), dqb, dkf, dvb,
            g1, win, gq, wqu, gkv, wkv, tabs)
        d_win = _tn_matmul(u, dp, "dw_in")
        d_wqu = _tn_matmul(qn, dqbo, "dw_q_up")
        d_wkv = _tn_matmul(kvn, dkvo, "dw_kv_up")
        d_w_in, d_w_q_up, d_w_kv_up, d_w_o, d_sw, d_ml = _unpad_layer(d_win, d_wqu, d_wkv, d_wo, d_ga, d_gb)
        weights.attn_grads(l, d_w_in, d_w_q_up, d_w_kv_up, d_w_o)
        grads.append(dict(attn_norm=d_g1[0], q_norm=d_gq[0], kv_norm=d_gkv[0], sinks=dsink[:, 0], out_norm_swa=d_sw,
                          out_norm_mla=d_ml, ffn_norm=d_g2[0]))
    grads = grads[::-1]
    stacked = {k: jnp.stack([g[k] for g in grads]) for k in grads[0]}
    stacked["final_norm"] = d_final[0]
    return loss[0, 0], dh[FRONT + N_META:], dh[FRONT:FRONT + N_META], stacked


MESH = pl.DeviceIdType.MESH
ANY = pl.BlockSpec(memory_space=pl.ANY)


def _place():
    return lax.axis_index("x"), lax.axis_index("y"), lax.axis_index("c")


def _index(x, y, c):
    return 4 * x + 2 * y + c


def _comm_sems(n):
    return [pltpu.SemaphoreType.DMA((n, N_DEV - 1)), pltpu.SemaphoreType.DMA((n, N_DEV - 1)),
            pltpu.SemaphoreType.DMA((n,))]


class _gather_plan:
    def __init__(self, x_refs, out_refs, send_sems, recv_sems, local_sems):
        self.x_refs, self.out_refs = x_refs, out_refs
        self.send_sems, self.recv_sems, self.local_sems = send_sems, recv_sems, local_sems
        self.n = len(x_refs)

    def _where(self):
        x, y, c = _place()
        return (x, y, c), (x, y, 1 - c), [(1 - x, y), (x, 1 - y), (1 - x, 1 - y)], c

    def _copy(self, i, k, block, to, from_input=False):
        slot = self.out_refs[i].at[_index(*block)]
        return pltpu.make_async_remote_copy(
            src_ref=self.x_refs[i] if from_input else slot, dst_ref=slot,
            send_sem=self.send_sems.at[i, k], recv_sem=self.recv_sems.at[i, k], device_id=to, device_id_type=MESH)

    def _mine(self, i, me):
        return pltpu.make_async_copy(self.x_refs[i], self.out_refs[i].at[_index(*me)], self.local_sems.at[i])

    def _first(self, me, sibling, chips, c):
        out = [self._copy(i, 1 + j, me, (*chip, c), True) for j, chip in enumerate(chips) for i in range(self.n)]
        return out + [self._copy(i, 0, me, sibling, True) for i in range(self.n)]

    def start(self):
        me, sibling, chips, c = self._where()
        for i in range(self.n):
            self._mine(i, me).start()
        for cp in self._first(me, sibling, chips, c):
            cp.start()

    def forward(self):
        me, sibling, chips, c = self._where()
        for j, chip in enumerate(chips):
            for i in range(self.n):
                self._copy(i, 1 + j, (*chip, c), me).wait_recv()
                self._copy(i, 4 + j, (*chip, c), sibling).start()

    def finish(self):
        me, sibling, chips, c = self._where()
        for i in range(self.n):
            self._copy(i, 0, sibling, me).wait_recv()
            for j, chip in enumerate(chips):
                self._copy(i, 4 + j, (*chip, 1 - c), me).wait_recv()
        for cp in self._first(me, sibling, chips, c):
            cp.wait_send()
        for j, chip in enumerate(chips):
            for i in range(self.n):
                self._copy(i, 4 + j, (*chip, c), sibling).wait_send()
        for i in range(self.n):
            self._mine(i, me).wait()


class _exchange_plan:
    def __init__(self, in_refs, out_refs, send_sems, recv_sems, local_sems):
        self.in_refs, self.out_refs = in_refs, out_refs
        self.send_sems, self.recv_sems, self.local_sems = send_sems, recv_sems, local_sems
        self.n = len(in_refs)

    def _copies(self):
        x, y, c = _place()
        me = _index(x, y, c)
        mine = [pltpu.make_async_copy(self.in_refs[i].at[me], self.out_refs[i].at[me], self.local_sems.at[i])
                for i in range(self.n)]
        remote = []
        for k in range(1, N_DEV):
            peer = (1 - x if k & 4 else x, 1 - y if k & 2 else y, 1 - c if k & 1 else c)
            remote += [pltpu.make_async_remote_copy(
                src_ref=self.in_refs[i].at[_index(*peer)], dst_ref=self.out_refs[i].at[me],
                send_sem=self.send_sems.at[i, k - 1], recv_sem=self.recv_sems.at[i, k - 1],
                device_id=peer, device_id_type=MESH) for i in range(self.n)]
        return mine, remote

    def start(self):
        mine, remote = self._copies()
        for cp in mine + remote:
            cp.start()

    def finish(self):
        mine, remote = self._copies()
        for cp in remote:
            cp.wait_recv()
        for cp in remote:
            cp.wait_send()
        for cp in mine:
            cp.wait()


def _all_gather(shards, name):
    n = len(shards)

    def body(*refs):
        plan = _gather_plan(refs[:n], refs[n:2 * n], *refs[2 * n:])
        plan.start()
        plan.forward()
        plan.finish()

    return pl.pallas_call(
        body, name=name, in_specs=[ANY] * n, out_specs=[ANY] * n, scratch_shapes=_comm_sems(n),
        out_shape=[jax.ShapeDtypeStruct((N_DEV,) + a.shape, a.dtype) for a in shards],
    )(*shards)


def _exchange(slabs, name):
    n = len(slabs)

    def body(*refs):
        plan = _exchange_plan(refs[:n], refs[n:2 * n], *refs[2 * n:])
        plan.start()
        plan.finish()

    return pl.pallas_call(
        body, name=name, in_specs=[ANY] * n, out_specs=[ANY] * n, scratch_shapes=_comm_sems(n),
        out_shape=[jax.ShapeDtypeStruct(a.shape, a.dtype) for a in slabs],
    )(*slabs)


def _adamw(w, g, m, v):
    m = ADAM_B1 * m + (1.0 - ADAM_B1) * g
    v = ADAM_B2 * v + (1.0 - ADAM_B2) * (g * g)
    m_hat = m / (1.0 - ADAM_B1 ** ADAM_STEP)
    v_hat = v / (1.0 - ADAM_B2 ** ADAM_STEP)
    return -ADAM_LR * (m_hat / (jnp.sqrt(v_hat) + ADAM_EPS) + ADAM_WD * w), m, v


def _sum_slots(ref):
    g = ref[0].astype(F32)
    for s in range(1, N_DEV):
        g = g + ref[s].astype(F32)
    return g


def _reduce_adamw(parts, w, m, v, layer, outs, name):
    l, r, c = w.shape
    tile = next(t for t in (256, 128, r) if r % t == 0)

    def body(p_ref, w_ref, m_ref, v_ref, g0, d0, m0, v0, g_ref, d_ref, nm_ref, nv_ref):
        g = _sum_slots(p_ref)
        g_ref[...] = g
        d_ref[...], nm_ref[...], nv_ref[...] = _adamw(w_ref[...], g, m_ref[...], v_ref[...])

    blk = pl.BlockSpec((None, tile, c), lambda j: (layer, j, 0))
    return pl.pallas_call(
        body, name=name, grid=(r // tile,),
        in_specs=[pl.BlockSpec((N_DEV, tile, c), lambda j: (0, j, 0)), blk, blk, blk] + [ANY] * 4, out_specs=[blk] * 4,
        out_shape=[jax.ShapeDtypeStruct((l, r, c), F32)] * 4,
        input_output_aliases={4: 0, 5: 1, 6: 2, 7: 3},
        compiler_params=_params("parallel"),
    )(parts, w, m, v, *outs)


def _sum_parts(parts, name):
    _, r, c = parts.shape

    def body(p_ref, g_ref):
        g_ref[...] = _sum_slots(p_ref)

    return pl.pallas_call(body, name=name, out_shape=jax.ShapeDtypeStruct((r, c), F32))(parts)


def _adamw_call(w, g, m, v, name):
    def body(w_ref, g_ref, m_ref, v_ref, d_ref, nm_ref, nv_ref):
        d_ref[...], nm_ref[...], nv_ref[...] = _adamw(w_ref[...], g_ref[...], m_ref[...], v_ref[...])

    return pl.pallas_call(body, name=name, out_shape=[jax.ShapeDtypeStruct(w.shape, F32)] * 3)(w, g, m, v)


ATTN = ("w_in", "w_q_up", "w_kv_up", "w_o")
FFN = ("w_gate", "w_up", "w_down")
SHARD_AXIS = dict(w_in=1, w_q_up=1, w_kv_up=1, w_o=0, w_gate=1, w_up=1, w_down=0)
SMALL = ("attn_norm", "ffn_norm", "final_norm", "out_norm_swa", "out_norm_mla", "q_norm", "kv_norm", "sinks")
PACK_W = 1024
SMALL_ROWS = 16


def _pack(arrs, dtype):
    flat = jnp.concatenate([a.astype(dtype).reshape(-1) for a in arrs])
    return flat.reshape(-1, PACK_W)


def _unpack(packed, like):
    flat = packed.reshape(-1)
    out, off = [], 0
    for a in like:
        out.append(flat[off:off + a.size].reshape(a.shape))
        off += a.size
    return out


def _gather_to_full(gathered, axis):
    shp = list(gathered.shape[1:])
    shp[axis] *= N_DEV
    return jnp.moveaxis(gathered, 0, axis).reshape(shp)


def _full_to_slabs(full, axis):
    shp = list(full.shape)
    shp[axis:axis + 1] = [N_DEV, shp[axis] // N_DEV]
    return jnp.moveaxis(full.reshape(shp), axis, 0).astype(BF16)


class _ShardedWeights:
    def __init__(self, shards, depth):
        self.shards, self.depth = shards, depth
        self.gathered, self.pending, self.parts = {}, {}, {}
        self._gather([(n, 0) for n in ATTN], lambda xs: _all_gather(xs, "gather_attn0"))

    def _gather(self, keys, run):
        self.gathered.update(zip(keys, run([self.shards[n][l] for n, l in keys])))

    def _full(self, names, l):
        return tuple(_gather_to_full(self.gathered[n, l], SHARD_AXIS[n]) for n in names)

    def attn(self, l):
        return self._full(ATTN, l)

    def ffn(self, l):
        return self._full(FFN, l)

    def mla_fwd(self, l, q, k, v):
        keys = [(n, l) for n in FFN] + ([(n, l + 1) for n in ATTN] if l + 1 < self.depth else [])
        out = []
        self._gather(keys, lambda xs: out.extend(_mla_fwd(q, k, v, xs)) or out[2])
        return out[0], out[1]

    def _add(self, names, l, grads):
        for n, g in zip(names, grads):
            self.pending[n, l] = _full_to_slabs(g, SHARD_AXIS[n])

    def ffn_grads(self, l, *grads):
        self._add(FFN, l, grads)

    def attn_grads(self, l, *grads):
        self._add(ATTN, l, grads)

    def _exchange(self, run):
        keys = list(self.pending)
        self.parts.update(zip(keys, run([self.pending.pop(k) for k in keys])))

    def mla_bwd_dkv(self, l, *args):
        out = []
        self._exchange(lambda xs: out.extend(_mla_bwd_dkv(*args, xs)) or out[2])
        return out[0], out[1]

    def flush(self):
        self._exchange(lambda xs: _exchange(xs, "exchange_attn0"))


def kernel(x, meta_tokens, attn_norm, w_in, q_norm, w_q_up, kv_norm, w_kv_up, sinks, out_norm_swa, out_norm_mla, w_o, ffn_norm, w_gate, w_up, w_down, final_norm, loss_target, m_meta_tokens, m_attn_norm, m_w_in, m_q_norm, m_w_q_up, m_kv_norm, m_w_kv_up, m_sinks, m_out_norm_swa, m_out_norm_mla, m_w_o, m_ffn_norm, m_w_gate, m_w_up, m_w_down, m_final_norm, v_meta_tokens, v_attn_norm, v_w_in, v_q_norm, v_w_q_up, v_kv_norm, v_w_kv_up, v_sinks, v_out_norm_swa, v_out_norm_mla, v_w_o, v_ffn_norm, v_w_gate, v_w_up, v_w_down, v_final_norm):
    w = dict(meta_tokens=meta_tokens, attn_norm=attn_norm, w_in=w_in, q_norm=q_norm, w_q_up=w_q_up, kv_norm=kv_norm,
             w_kv_up=w_kv_up, sinks=sinks, out_norm_swa=out_norm_swa, out_norm_mla=out_norm_mla, w_o=w_o,
             ffn_norm=ffn_norm, w_gate=w_gate, w_up=w_up, w_down=w_down, final_norm=final_norm)
    m = dict(meta_tokens=m_meta_tokens, attn_norm=m_attn_norm, w_in=m_w_in, q_norm=m_q_norm, w_q_up=m_w_q_up,
             kv_norm=m_kv_norm, w_kv_up=m_w_kv_up, sinks=m_sinks, out_norm_swa=m_out_norm_swa,
             out_norm_mla=m_out_norm_mla, w_o=m_w_o, ffn_norm=m_ffn_norm, w_gate=m_w_gate, w_up=m_w_up,
             w_down=m_w_down, final_norm=m_final_norm)
    v = dict(meta_tokens=v_meta_tokens, attn_norm=v_attn_norm, w_in=v_w_in, q_norm=v_q_norm, w_q_up=v_w_q_up,
             kv_norm=v_kv_norm, w_kv_up=v_w_kv_up, sinks=v_sinks, out_norm_swa=v_out_norm_swa,
             out_norm_mla=v_out_norm_mla, w_o=v_w_o, ffn_norm=v_ffn_norm, w_gate=v_w_gate, w_up=v_w_up,
             w_down=v_w_down, final_norm=v_final_norm)
    names = list(w)
    big = ATTN + FFN
    depth = w_in.shape[0]
    me = _index(*_place())

    weights = _ShardedWeights({n: w[n].astype(BF16) for n in big}, depth)
    meta = jnp.moveaxis(_all_gather([meta_tokens], "gather_meta")[0], 0, 1).reshape(N_META, D_MODEL)
    loss, grad_x, d_meta, grads = _train_example(x[0], loss_target[0], meta, {n: w[n] for n in SMALL}, weights)
    weights.flush()

    g_big, d_big, m_big, v_big = {}, {}, {}, {}
    for n in big:
        outs = [lax.empty(w[n].shape, F32) for _ in range(4)]
        for l in reversed(range(depth)):
            outs = _reduce_adamw(weights.parts[n, l], w[n], m[n], v[n], l, outs, "reduce_adamw_" + n)
        g_big[n], d_big[n], m_big[n], v_big[n] = outs

    small = [grads[n] for n in SMALL] + [loss.reshape(1)]
    pad = SMALL_ROWS * PACK_W - sum(a.size for a in small)
    part = jnp.concatenate([_pack(small + [jnp.zeros((pad,), F32)], F32), d_meta], axis=0)
    total = _sum_parts(_all_gather([part], "gather_small")[0], "sum_small")
    small_w = [w[n] for n in SMALL]
    packs = [_pack([d[n] for n in SMALL] + [jnp.zeros((pad + 1,), F32)], F32) for d in (w, m, v)]
    upd = _adamw_call(packs[0], total[:SMALL_ROWS], packs[1], packs[2], "adamw_small")
    g_small, d_small, m_small, v_small = [dict(zip(SMALL, _unpack(p, small_w))) for p in (total[:SMALL_ROWS],) + tuple(upd)]
    loss_total = total[:SMALL_ROWS].reshape(-1)[SMALL_ROWS * PACK_W - pad - 1]
    g_meta = lax.dynamic_slice_in_dim(total[SMALL_ROWS:], me * LANE, LANE, axis=1)
    d_mt, m_mt, v_mt = _adamw_call(meta_tokens, g_meta, m_meta_tokens, v_meta_tokens, "adamw_meta")

    outs = []
    for got in ({**g_big, **g_small, "meta_tokens": g_meta}, {**d_big, **d_small, "meta_tokens": d_mt},
                {**m_big, **m_small, "meta_tokens": m_mt}, {**v_big, **v_small, "meta_tokens": v_mt}):
        outs += [got[n] for n in names]
    return (loss_total, grad_x[None], *outs)
```

```python
import functools

import jax
import jax.numpy as jnp
from jax import lax
from jax.experimental import pallas as pl
from jax.experimental.pallas import tpu as pltpu

F32 = jnp.float32
BF16 = jnp.bfloat16

D_MODEL = 1024
N_META = 16
BLOCK = 128
FRONT = (-N_META) % BLOCK
ROPE_THETA = 10000.0
EPS = 1e-6
NEG = -1e30
SWA_HEADS = 8
SWA_KV_HEADS = 2
SWA_GROUP = SWA_HEADS // SWA_KV_HEADS
SWA_HEAD_DIM = 64
MLA_HEADS = 8
MLA_Q_RANK = 256
MLA_KV_RANK = 128
MLA_NOPE_DIM = 64
MLA_ROPE_DIM = 32
MLA_V_DIM = 64
MLA_QK_DIM = MLA_NOPE_DIM + MLA_ROPE_DIM
SWA_Q_W = SWA_HEADS * SWA_HEAD_DIM
SWA_KV_W = SWA_KV_HEADS * SWA_HEAD_DIM
MLA_OUT_W = MLA_HEADS * MLA_V_DIM
SCALE_A = SWA_HEAD_DIM ** -0.5
SCALE_B = MLA_QK_DIM ** -0.5
LOG2E = 1.4426950408889634
Q_SCALE = SCALE_B * LOG2E
ADAM_LR = 0.001
ADAM_B1 = 0.9
ADAM_B2 = 0.999
ADAM_EPS = 1e-08
ADAM_WD = 0.01
ADAM_STEP = 10

LANE = 128
N_DEV = 8
HP = 8 * LANE
PO_QA, PO_KA, PO_VA = 0, HP, HP + 2 * LANE
PO_CQ = PO_VA + 2 * LANE
PO_CKV = PO_CQ + MLA_Q_RANK
PO_KR = PO_CKV + MLA_KV_RANK
PW_IN = PO_KR + LANE
N_TAB = 7
VMEM_LIMIT = 56 * 2 ** 20
TN_VMEM_BUDGET = 36 * 2 ** 20
MLA_HB = 4

NT = (((1,), (1,)), ((), ()))
TN = (((0,), (0,)), ((), ()))


def _tile(t):
    return 384 if t % 384 == 0 else 128


def _params(*sem):
    return pltpu.CompilerParams(dimension_semantics=sem, vmem_limit_bytes=VMEM_LIMIT)


def _row(tm, n):
    return pl.BlockSpec((tm, n), lambda i: (i, 0))


def _const(shape):
    return pl.BlockSpec(shape, lambda i: (0,) * len(shape))


def _dot(a, b):
    return jnp.dot(a, b, preferred_element_type=F32)


def _dot_nt(a, b):
    return lax.dot_general(a, b, NT, preferred_element_type=F32)


def _dot_tn(a, b):
    return lax.dot_general(a, b, TN, preferred_element_type=F32)


def _rope(x, c, s1, s2, shift):
    return x * c + pltpu.roll(x, LANE - shift, 1) * s1 + pltpu.roll(x, shift, 1) * s2


def _rope_t(dy, c, s1, s2, shift):
    return dy * c + pltpu.roll(dy * s1, shift, 1) + pltpu.roll(dy * s2, LANE - shift, 1)


def _rms_r(x, n):
    return lax.rsqrt(jnp.sum(x * x, axis=-1, keepdims=True) * (1.0 / n) + EPS)


def _rms_bwd(x, g, dy, n):
    r = _rms_r(x, n)
    xh = x * r
    dxh = dy * g
    dx = r * (dxh - xh * (jnp.sum(dxh * xh, axis=-1, keepdims=True) * (1.0 / n)))
    return dx, jnp.sum(dy * xh, axis=0, keepdims=True)


def _acc(ref, val, first):
    @pl.when(first)
    def _():
        ref[...] = val

    @pl.when(jnp.logical_not(first))
    def _():
        ref[...] += val


def _tabs(tab_ref):
    return [tab_ref[:, LANE * i:LANE * (i + 1)] for i in range(N_TAB)]


def _pre_fwd(h, g1, win, gq, wqu, gkv, wkv, tabs):
    t = h.shape[0]
    tm = _tile(t)

    def body(h_ref, g1_ref, win_ref, gq_ref, wqu_ref, gkv_ref, wkv_ref, tab_ref,
             u_ref, qa_ref, ka_ref, va_ref, cq_ref, ckv_ref, qn_ref, kvn_ref, qb_ref, kf_ref, vb_ref):
        ca, sa1, sa2, cb, sb1, sb2, ck = _tabs(tab_ref)
        hv = h_ref[...]
        u = (hv * _rms_r(hv, D_MODEL) * g1_ref[...]).astype(BF16)
        u_ref[...] = u
        p = _dot(u, win_ref[...])
        for c in range(SWA_HEADS):
            sl = slice(LANE * c, LANE * (c + 1))
            qa_ref[:, sl] = _rope(p[:, PO_QA + LANE * c:PO_QA + LANE * (c + 1)], ca, sa1, sa2, 32).astype(BF16)
        for c in range(SWA_KV_HEADS):
            sl = slice(LANE * c, LANE * (c + 1))
            ka_ref[:, sl] = _rope(p[:, PO_KA + LANE * c:PO_KA + LANE * (c + 1)], ca, sa1, sa2, 32).astype(BF16)
        va_ref[...] = p[:, PO_VA:PO_CQ].astype(BF16)
        cq = p[:, PO_CQ:PO_CKV]
        ckv = p[:, PO_CKV:PO_KR]
        cq_ref[...] = cq
        ckv_ref[...] = ckv
        qn = (cq * _rms_r(cq, MLA_Q_RANK) * gq_ref[...]).astype(BF16)
        qn_ref[...] = qn
        qb = _dot(qn, wqu_ref[...])
        kvn = (ckv * _rms_r(ckv, MLA_KV_RANK) * gkv_ref[...]).astype(BF16)
        kvn_ref[...] = kvn
        kv = _dot(kvn, wkv_ref[...])
        kr = _rope(p[:, PO_KR:PW_IN], ck, sb1, sb2, 16)
        for c in range(MLA_HEADS):
            sl = slice(LANE * c, LANE * (c + 1))
            qb_ref[:, sl] = (_rope(qb[:, sl], cb, sb1, sb2, 16) * Q_SCALE).astype(BF16)
            kf_ref[:, sl] = (kv[:, sl] + kr).astype(BF16)
        vb_ref[...] = kv[:, HP:].astype(BF16)

    widths = [(D_MODEL, BF16), (HP, BF16), (2 * LANE, BF16), (2 * LANE, BF16), (MLA_Q_RANK, F32),
              (MLA_KV_RANK, F32), (MLA_Q_RANK, BF16), (MLA_KV_RANK, BF16), (HP, BF16), (HP, BF16), (HP, BF16)]
    return pl.pallas_call(
        body, name="pre_fwd", grid=(t // tm,),
        in_specs=[_row(tm, D_MODEL), _const(g1.shape), _const(win.shape), _const(gq.shape), _const(wqu.shape),
                  _const(gkv.shape), _const(wkv.shape), _row(tm, N_TAB * LANE)],
        out_specs=[_row(tm, w) for w, _ in widths],
        out_shape=[jax.ShapeDtypeStruct((t, w), d) for w, d in widths],
        compiler_params=_params("parallel"),
    )(h, g1, win, gq, wqu, gkv, wkv, tabs)


def _swa_probs(qh, k2, sink, mask):
    s = jnp.where(mask, _dot_nt(qh, k2) * SCALE_A, NEG)
    m = jnp.maximum(jnp.max(s, axis=1, keepdims=True), sink)
    e = jnp.exp(s - m)
    es = jnp.exp(sink - m)
    inv = 1.0 / (jnp.sum(e, axis=1, keepdims=True) + es)
    return e * inv, es * inv


def _swa_mask(n):
    row = lax.broadcasted_iota(jnp.int32, (SWA_GROUP * BLOCK, 2 * BLOCK), 0) & (BLOCK - 1)
    col = lax.broadcasted_iota(jnp.int32, (SWA_GROUP * BLOCK, 2 * BLOCK), 1)
    return (col > row) & (col <= row + BLOCK) & (col + (n - 1) * BLOCK >= FRONT)


def _swa_group(ref, j):
    return jnp.concatenate([ref[:, LANE * (SWA_GROUP * j + g):LANE * (SWA_GROUP * j + g + 1)]
                            for g in range(SWA_GROUP)], axis=0)


def _swa_sinks(sink_ref, j):
    return jnp.concatenate([jnp.full((BLOCK, 1), sink_ref[0, SWA_GROUP * j + g], F32) for g in range(SWA_GROUP)], axis=0)


def _swa_specs():
    prev = lambda n: (jnp.maximum(n - 1, 0), 0)
    cur = lambda n: (n, 0)
    kv = (BLOCK, 2 * LANE)
    return [pl.BlockSpec(memory_space=pltpu.SMEM), pl.BlockSpec((BLOCK, HP), cur),
            pl.BlockSpec(kv, prev), pl.BlockSpec(kv, cur), pl.BlockSpec(kv, prev), pl.BlockSpec(kv, cur)]


def _swa_fwd(sinks, q, k, v):
    t = q.shape[0]

    def body(sink_ref, q_ref, kp_ref, kc_ref, vp_ref, vc_ref, o_ref):
        mask = _swa_mask(pl.program_id(0))
        for j in range(SWA_KV_HEADS):
            sl = slice(LANE * j, LANE * (j + 1))
            k2 = jnp.concatenate([kp_ref[:, sl], kc_ref[:, sl]], axis=0)
            v2 = jnp.concatenate([vp_ref[:, sl], vc_ref[:, sl]], axis=0)
            p, _ = _swa_probs(_swa_group(q_ref, j), k2, _swa_sinks(sink_ref, j), mask)
            o4 = _dot(p.astype(BF16), v2)
            for g in range(SWA_GROUP):
                hd = SWA_GROUP * j + g
                o_ref[:, LANE * hd:LANE * (hd + 1)] = o4[BLOCK * g:BLOCK * (g + 1)]

    return pl.pallas_call(
        body, name="swa_fwd", grid=(t // BLOCK,),
        in_specs=_swa_specs(),
        out_specs=pl.BlockSpec((BLOCK, HP), lambda n: (n, 0)),
        out_shape=jax.ShapeDtypeStruct((t, HP), F32),
        compiler_params=_params("parallel"),
    )(sinks, q, k, k, v, v)


def _causal_mask(q0, k0, tq, tk, transposed):
    if transposed:
        key = k0 + lax.broadcasted_iota(jnp.int32, (tk, tq), 0)
        qry = q0 + lax.broadcasted_iota(jnp.int32, (tk, tq), 1)
    else:
        qry = q0 + lax.broadcasted_iota(jnp.int32, (tq, tk), 0)
        key = k0 + lax.broadcasted_iota(jnp.int32, (tq, tk), 1)
    return (key <= qry) & (key >= FRONT)


def _heads(ref, rows=slice(None)):
    return [ref[rows, LANE * a:LANE * (a + 1)] for a in range(MLA_HB)]


def _mla_fwd(q, k, v, shards=()):
    t = q.shape[0]
    tq = _tile(t)
    nq = t // tq
    n = len(shards)
    steps = (MLA_HEADS // MLA_HB) * nq

    def body(q_ref, k_ref, v_ref, *rest):
        x_refs, (o_ref, lse_ref), out_refs, sems = rest[:n], rest[n:n + 2], rest[n + 2:2 * n + 2], rest[2 * n + 2:]
        i = pl.program_id(1)
        step_id = pl.program_id(0) * nq + i
        if n:
            plan = _gather_plan(x_refs, out_refs, *sems)
            pl.when(step_id == 0)(plan.start)
            pl.when(step_id == steps // 2)(plan.forward)
        qs = _heads(q_ref)

        def step(j, carry, masked):
            rows = pl.ds(pl.multiple_of(j * tq, tq), tq)
            ks, vs = _heads(k_ref, rows), _heads(v_ref, rows)
            ss = [_dot_nt(qh, kh) for qh, kh in zip(qs, ks)]
            if masked:
                mask = _causal_mask(i * tq, j * tq, tq, tq, False)
                ss = [jnp.where(mask, s, NEG) for s in ss]
            mid = []
            for s, (m, l, _) in zip(ss, carry):
                mn = jnp.maximum(m, jnp.max(s, axis=1, keepdims=True))
                a = jnp.exp2(m - mn)
                p = jnp.exp2(s - mn)
                mid.append((mn, a * l + jnp.sum(p, axis=1, keepdims=True), a, p.astype(BF16)))
            return tuple((mn, l, a * acc + _dot(p, vh)) for (mn, l, a, p), (_, _, acc), vh in zip(mid, carry, vs))

        init = ((jnp.full((tq, 1), NEG, F32), jnp.zeros((tq, 1), F32), jnp.zeros((tq, LANE), F32)),) * MLA_HB
        carry = lax.fori_loop(0, jnp.minimum(i, 1) + 1, lambda it, c: step(it * i, c, True), init)
        carry = lax.fori_loop(1, i, lambda j, c: step(j, c, False), carry)
        for a, (m, l, acc) in enumerate(carry):
            o_ref[:, LANE * a:LANE * (a + 1)] = acc / l
            lse_ref[:, LANE * a:LANE * (a + 1)] = jnp.broadcast_to(m + jnp.log2(l), (tq, LANE))
        if n:
            pl.when(step_id == steps - 1)(plan.finish)

    blk = pl.BlockSpec((tq, MLA_HB * LANE), lambda h, i: (i, h))
    full = pl.BlockSpec((t, MLA_HB * LANE), lambda h, i: (0, h))
    out = pl.pallas_call(
        body, name="mla_fwd_gather" if n else "mla_fwd", grid=(MLA_HEADS // MLA_HB, nq),
        in_specs=[blk, full, full] + [ANY] * n, out_specs=[blk, blk] + [ANY] * n,
        out_shape=[jax.ShapeDtypeStruct((t, HP), F32)] * 2
        + [jax.ShapeDtypeStruct((N_DEV,) + a.shape, a.dtype) for a in shards],
        scratch_shapes=_comm_sems(n) if n else [],
        compiler_params=_params("arbitrary", "arbitrary"),
    )(q, k, v, *shards)
    return out[0], out[1], out[2:]


def _mix_fwd(h, oa, ob, ga, gb, wo, g2):
    t = h.shape[0]
    tm = _tile(t)

    def body(h_ref, oa_ref, ob_ref, ga_ref, gb_ref, wo_ref, g2_ref, h2_ref, mix_ref, u2_ref):
        oa_v = oa_ref[...]
        ob_v = ob_ref[...]
        na = (oa_v * _rms_r(oa_v, SWA_Q_W) * ga_ref[...]).astype(BF16)
        nb = (ob_v * _rms_r(ob_v, MLA_OUT_W) * gb_ref[...]).astype(BF16)
        mix_ref[:, :HP] = na
        mix_ref[:, HP:] = nb
        h2 = h_ref[...] + _dot(na, wo_ref[:HP, :]) + _dot(nb, wo_ref[HP:, :])
        h2_ref[...] = h2
        u2_ref[...] = (h2 * _rms_r(h2, D_MODEL) * g2_ref[...]).astype(BF16)

    return pl.pallas_call(
        body, name="mix_fwd", grid=(t // tm,),
        in_specs=[_row(tm, D_MODEL), _row(tm, HP), _row(tm, HP), _const(ga.shape), _const(gb.shape),
                  _const(wo.shape), _const(g2.shape)],
        out_specs=[_row(tm, D_MODEL), _row(tm, 2 * HP), _row(tm, D_MODEL)],
        out_shape=[jax.ShapeDtypeStruct((t, D_MODEL), F32), jax.ShapeDtypeStruct((t, 2 * HP), BF16),
                   jax.ShapeDtypeStruct((t, D_MODEL), BF16)],
        compiler_params=_params("parallel"),
    )(h, oa, ob, ga, gb, wo, g2)


def _ffn_fwd(h2, u2, wg, wu, wd):
    t = h2.shape[0]
    tm = _tile(t)
    dff = wg.shape[1]

    def body(h2_ref, u2_ref, wg_ref, wu_ref, wd_ref, h3_ref, g_ref, up_ref):
        u2v = u2_ref[...]
        g = _dot(u2v, wg_ref[...])
        up = _dot(u2v, wu_ref[...])
        g_ref[...] = g.astype(BF16)
        up_ref[...] = up.astype(BF16)
        a = (g * jax.nn.sigmoid(g) * up).astype(BF16)
        h3_ref[...] = h2_ref[...] + _dot(a, wd_ref[...])

    return pl.pallas_call(
        body, name="ffn_fwd", grid=(t // tm,),
        in_specs=[_row(tm, D_MODEL), _row(tm, D_MODEL), _const(wg.shape), _const(wu.shape), _const(wd.shape)],
        out_specs=[_row(tm, D_MODEL), _row(tm, dff), _row(tm, dff)],
        out_shape=[jax.ShapeDtypeStruct((t, D_MODEL), F32), jax.ShapeDtypeStruct((t, dff), BF16),
                   jax.ShapeDtypeStruct((t, dff), BF16)],
        compiler_params=_params("parallel"),
    )(h2, u2, wg, wu, wd)


def _loss_bwd(h, gf, target):
    t = h.shape[0]
    tm = _tile(t)
    first_row = FRONT + N_META

    def body(h_ref, gf_ref, t_ref, dh_ref, dgf_ref, loss_ref):
        i = pl.program_id(0)
        hv = h_ref[...]
        y = hv * _rms_r(hv, D_MODEL) * gf_ref[...]
        row = i * tm + lax.broadcasted_iota(jnp.int32, (tm, 1), 0)
        err = jnp.where(row >= first_row, y - t_ref[...], 0.0)
        dx, dg = _rms_bwd(hv, gf_ref[...], err * (1.0 / D_MODEL), D_MODEL)
        dh_ref[...] = dx
        _acc(dgf_ref, dg, i == 0)
        part = 0.5 * jnp.sum(jnp.sum(err * err, axis=1, keepdims=True) * (1.0 / D_MODEL), axis=0, keepdims=True)
        _acc(loss_ref, jnp.broadcast_to(part, (1, LANE)), i == 0)

    return pl.pallas_call(
        body, name="loss_bwd", grid=(t // tm,),
        in_specs=[_row(tm, D_MODEL), _const(gf.shape), _row(tm, D_MODEL)],
        out_specs=[_row(tm, D_MODEL), _const((1, D_MODEL)), _const((1, LANE))],
        out_shape=[jax.ShapeDtypeStruct((t, D_MODEL), F32), jax.ShapeDtypeStruct((1, D_MODEL), F32),
                   jax.ShapeDtypeStruct((1, LANE), F32)],
        compiler_params=_params("arbitrary"),
    )(h, gf, target)


def _tn_matmul(a, b, name):
    t, k = a.shape
    n = b.shape[1]
    tk = next(c for c in (k, 1024, 512, 256, 128) if k % c == 0 and c <= 1024)
    fits = lambda c: 2 * (t * (tk + c) * 2 + tk * c * 4) <= TN_VMEM_BUDGET
    tn = next(c for c in (n, 1024, 512, 256, 128) if n % c == 0 and fits(c))

    def body(a_ref, b_ref, o_ref):
        o_ref[...] = _dot_tn(a_ref[...], b_ref[...])

    return pl.pallas_call(
        body, name=name, grid=(k // tk, n // tn),
        in_specs=[pl.BlockSpec((t, tk), lambda i, j: (0, i)), pl.BlockSpec((t, tn), lambda i, j: (0, j))],
        out_specs=pl.BlockSpec((tk, tn), lambda i, j: (i, j)),
        out_shape=jax.ShapeDtypeStruct((k, n), F32),
        compiler_params=_params("parallel", "parallel"),
    )(a, b)


def _ffn_bwd_a(dh3, g, up, wd):
    t = dh3.shape[0]
    tm = _tile(t)
    dff = wd.shape[0]

    def body(dh3_ref, g_ref, up_ref, wd_ref, a_ref, dgu_ref, dh3b_ref):
        dh3b = dh3_ref[...].astype(BF16)
        dh3b_ref[...] = dh3b
        da = _dot_nt(dh3b, wd_ref[...])
        gv = g_ref[...].astype(F32)
        upv = up_ref[...].astype(F32)
        sg = jax.nn.sigmoid(gv)
        silu = gv * sg
        a_ref[...] = (silu * upv).astype(BF16)
        dgu_ref[:, :dff] = (da * upv * (sg * (1.0 + gv * (1.0 - sg)))).astype(BF16)
        dgu_ref[:, dff:] = (da * silu).astype(BF16)

    return pl.pallas_call(
        body, name="ffn_bwd_a", grid=(t // tm,),
        in_specs=[_row(tm, D_MODEL), _row(tm, dff), _row(tm, dff), _const(wd.shape)],
        out_specs=[_row(tm, dff), _row(tm, 2 * dff), _row(tm, D_MODEL)],
        out_shape=[jax.ShapeDtypeStruct((t, dff), BF16), jax.ShapeDtypeStruct((t, 2 * dff), BF16),
                   jax.ShapeDtypeStruct((t, D_MODEL), BF16)],
        compiler_params=_params("parallel"),
    )(dh3, g, up, wd)


def _ffn_bwd_b(dh3, dgu, h2, g2, wg, wu):
    t = dh3.shape[0]
    tm = _tile(t)
    dff = wg.shape[1]

    def body(dh3_ref, dgu_ref, h2_ref, g2_ref, wg_ref, wu_ref, dh2_ref, dh2b_ref, dg2_ref):
        du2 = _dot_nt(dgu_ref[:, :dff], wg_ref[...]) + _dot_nt(dgu_ref[:, dff:], wu_ref[...])
        dx, dg = _rms_bwd(h2_ref[...], g2_ref[...], du2, D_MODEL)
        dh2 = dh3_ref[...] + dx
        dh2_ref[...] = dh2
        dh2b_ref[...] = dh2.astype(BF16)
        _acc(dg2_ref, dg, pl.program_id(0) == 0)

    return pl.pallas_call(
        body, name="ffn_bwd_b", grid=(t // tm,),
        in_specs=[_row(tm, D_MODEL), _row(tm, 2 * dff), _row(tm, D_MODEL), _const(g2.shape), _const(wg.shape),
                  _const(wu.shape)],
        out_specs=[_row(tm, D_MODEL), _row(tm, D_MODEL), _const((1, D_MODEL))],
        out_shape=[jax.ShapeDtypeStruct((t, D_MODEL), F32), jax.ShapeDtypeStruct((t, D_MODEL), BF16),
                   jax.ShapeDtypeStruct((1, D_MODEL), F32)],
        compiler_params=_params("arbitrary"),
    )(dh3, dgu, h2, g2, wg, wu)


def _mix_bwd(dh2, oa, ob, ga, gb, wo):
    t = dh2.shape[0]
    tm = _tile(t)

    def body(dh2_ref, oa_ref, ob_ref, ga_ref, gb_ref, wo_ref, doa_ref, dob_ref, dga_ref, dgb_ref):
        first = pl.program_id(0) == 0
        d = dh2_ref[...].astype(BF16)
        dxa, dga = _rms_bwd(oa_ref[...], ga_ref[...], _dot_nt(d, wo_ref[:HP, :]), SWA_Q_W)
        dxb, dgb = _rms_bwd(ob_ref[...], gb_ref[...], _dot_nt(d, wo_ref[HP:, :]), MLA_OUT_W)
        doa_ref[...] = dxa.astype(BF16)
        dob_ref[...] = dxb.astype(BF16)
        _acc(dga_ref, dga, first)
        _acc(dgb_ref, dgb, first)

    return pl.pallas_call(
        body, name="mix_bwd", grid=(t // tm,),
        in_specs=[_row(tm, D_MODEL), _row(tm, HP), _row(tm, HP), _const(ga.shape), _const(gb.shape), _const(wo.shape)],
        out_specs=[_row(tm, HP), _row(tm, HP), _const((1, HP)), _const((1, HP))],
        out_shape=[jax.ShapeDtypeStruct((t, HP), BF16), jax.ShapeDtypeStruct((t, HP), BF16),
                   jax.ShapeDtypeStruct((1, HP), F32), jax.ShapeDtypeStruct((1, HP), F32)],
        compiler_params=_params("arbitrary"),
    )(dh2, oa, ob, ga, gb, wo)


def _swa_bwd(sinks, q, k, v, o, do):
    t = q.shape[0]

    def body(sink_ref, q_ref, kp_ref, kc_ref, vp_ref, vc_ref, o_ref, do_ref,
             dq_ref, dkc_ref, dkp_ref, dvc_ref, dvp_ref, dsink_ref):
        n = pl.program_id(0)
        mask = _swa_mask(n)
        for j in range(SWA_KV_HEADS):
            sl = slice(LANE * j, LANE * (j + 1))
            k2 = jnp.concatenate([kp_ref[:, sl], kc_ref[:, sl]], axis=0)
            v2 = jnp.concatenate([vp_ref[:, sl], vc_ref[:, sl]], axis=0)
            q4 = _swa_group(q_ref, j)
            do4 = _swa_group(do_ref, j)
            p, psink = _swa_probs(q4, k2, _swa_sinks(sink_ref, j), mask)
            delta = jnp.sum(_swa_group(o_ref, j) * do4.astype(F32), axis=1, keepdims=True)
            ds = p * (_dot_nt(do4, v2) - delta) * SCALE_A
            dq4 = _dot(ds.astype(BF16), k2)
            dk2 = _dot(ds.T.astype(BF16), q4)
            dv2 = _dot(p.T.astype(BF16), do4)
            dsk = -psink * delta
            for g in range(SWA_GROUP):
                hd = SWA_GROUP * j + g
                rows = slice(BLOCK * g, BLOCK * (g + 1))
                dq_ref[:, LANE * hd:LANE * (hd + 1)] = dq4[rows]
                dsink = jnp.broadcast_to(jnp.sum(dsk[rows], axis=0, keepdims=True), (1, LANE))
                _acc(dsink_ref.at[hd:hd + 1, :], dsink, n == 0)
            dkp_ref[:, sl] = dk2[:BLOCK]
            dkc_ref[:, sl] = dk2[BLOCK:]
            dvp_ref[:, sl] = dv2[:BLOCK]
            dvc_ref[:, sl] = dv2[BLOCK:]

    cur = lambda n: (n, 0)
    kv = pl.BlockSpec((BLOCK, 2 * LANE), cur)
    hp = pl.BlockSpec((BLOCK, HP), cur)
    kvs = jax.ShapeDtypeStruct((t, 2 * LANE), F32)
    return pl.pallas_call(
        body, name="swa_bwd", grid=(t // BLOCK,),
        in_specs=_swa_specs() + [hp, hp],
        out_specs=[hp, kv, kv, kv, kv, _const((SWA_HEADS, LANE))],
        out_shape=[jax.ShapeDtypeStruct((t, HP), F32), kvs, kvs, kvs, kvs,
                   jax.ShapeDtypeStruct((SWA_HEADS, LANE), F32)],
        compiler_params=_params("arbitrary"),
    )(sinks, q, k, k, v, v, o, do)


def _mla_bwd_dq(q, k, v, o, do, lse):
    t = q.shape[0]
    tq = _tile(t)
    nq = t // tq

    def body(q_ref, k_ref, v_ref, o_ref, do_ref, lse_ref, dq_ref, dl_ref):
        i = pl.program_id(1)
        qs, dos = _heads(q_ref), _heads(do_ref)
        deltas = [jnp.sum(oh * doh.astype(F32), axis=1, keepdims=True) for oh, doh in zip(_heads(o_ref), dos)]
        lses = [lh[:, :1] for lh in _heads(lse_ref)]

        def step(j, dqs, masked):
            rows = pl.ds(pl.multiple_of(j * tq, tq), tq)
            ks, vs = _heads(k_ref, rows), _heads(v_ref, rows)
            ss = [_dot_nt(qh, kh) for qh, kh in zip(qs, ks)]
            dps = [_dot_nt(doh, vh) for doh, vh in zip(dos, vs)]
            if masked:
                mask = _causal_mask(i * tq, j * tq, tq, tq, False)
                ss = [jnp.where(mask, s, NEG) for s in ss]
            dss = [(jnp.exp2(s - lh) * (dp - dl)).astype(BF16) for s, dp, lh, dl in zip(ss, dps, lses, deltas)]
            return tuple(dq + _dot(ds, kh) for dq, ds, kh in zip(dqs, dss, ks))

        zero = (jnp.zeros((tq, LANE), F32),) * MLA_HB
        dqs = lax.fori_loop(0, jnp.minimum(i, 1) + 1, lambda it, c: step(it * i, c, True), zero)
        dqs = lax.fori_loop(1, i, lambda j, c: step(j, c, False), dqs)
        for a, (dq, dl) in enumerate(zip(dqs, deltas)):
            dq_ref[:, LANE * a:LANE * (a + 1)] = dq * SCALE_B
            dl_ref[:, LANE * a:LANE * (a + 1)] = jnp.broadcast_to(dl, (tq, LANE))

    blk = pl.BlockSpec((tq, MLA_HB * LANE), lambda h, i: (i, h))
    full = pl.BlockSpec((t, MLA_HB * LANE), lambda h, i: (0, h))
    return pl.pallas_call(
        body, name="mla_bwd_dq", grid=(MLA_HEADS // MLA_HB, nq),
        in_specs=[blk, full, full, blk, blk, blk], out_specs=[blk, blk],
        out_shape=[jax.ShapeDtypeStruct((t, HP), F32)] * 2,
        compiler_params=_params("parallel", "parallel"),
    )(q, k, v, o, do, lse)


def _mla_bwd_dkv(q, k, v, do, lse_t, dl_t, slabs=()):
    t = q.shape[0]
    tq = _tile(t)
    nq = t // tq
    n = len(slabs)
    steps = (MLA_HEADS // MLA_HB) * nq

    def body(k_ref, v_ref, q_ref, do_ref, lse_ref, dl_ref, *rest):
        in_refs, (dk_ref, dv_ref), out_refs, sems = rest[:n], rest[n:n + 2], rest[n + 2:2 * n + 2], rest[2 * n + 2:]
        hb = pl.program_id(0)
        j = pl.program_id(1)
        step_id = hb * nq + j
        if n:
            plan = _exchange_plan(in_refs, out_refs, *sems)
            pl.when(step_id == 0)(plan.start)
        ks, vs = _heads(k_ref), _heads(v_ref)

        def step(i, carry, masked):
            rows = pl.ds(pl.multiple_of(i * tq, tq), tq)
            qs, dos = _heads(q_ref, rows), _heads(do_ref, rows)
            sts = [_dot_nt(kh, qh) for kh, qh in zip(ks, qs)]
            dpts = [_dot_nt(vh, doh) for vh, doh in zip(vs, dos)]
            if masked:
                mask = _causal_mask(i * tq, j * tq, tq, tq, True)
                sts = [jnp.where(mask, st, NEG) for st in sts]
            pts = [jnp.exp2(st - lse_ref[(hb * MLA_HB + a) * nq + i]) for a, st in enumerate(sts)]
            dsts = [(pt * (dpt - dl_ref[(hb * MLA_HB + a) * nq + i])).astype(BF16)
                    for a, (pt, dpt) in enumerate(zip(pts, dpts))]
            return tuple((dk + _dot(dst, qh), dv + _dot(pt.astype(BF16), doh))
                         for (dk, dv), dst, pt, qh, doh in zip(carry, dsts, pts, qs, dos))

        zero = ((jnp.zeros((tq, LANE), F32),) * 2,) * MLA_HB
        split = jnp.where(j == 0, nq, j + 1)
        carry = lax.fori_loop(j, split, lambda i, c: step(i, c, True), zero)
        carry = lax.fori_loop(split, nq, lambda i, c: step(i, c, False), carry)
        for a, (dk, dv) in enumerate(carry):
            dk_ref[:, LANE * a:LANE * (a + 1)] = dk * (1.0 / LOG2E)
            dv_ref[:, LANE * a:LANE * (a + 1)] = dv
        if n:
            pl.when(step_id == steps - 1)(plan.finish)

    blk = pl.BlockSpec((tq, MLA_HB * LANE), lambda h, j: (j, h))
    full = pl.BlockSpec((t, MLA_HB * LANE), lambda h, j: (0, h))
    rows = pl.BlockSpec((MLA_HEADS * nq, 1, tq), lambda h, j: (0, 0, 0))
    out = pl.pallas_call(
        body, name="mla_bwd_dkv_exchange" if n else "mla_bwd_dkv", grid=(MLA_HEADS // MLA_HB, nq),
        in_specs=[blk, blk, full, full, rows, rows] + [ANY] * n, out_specs=[blk, blk] + [ANY] * n,
        out_shape=[jax.ShapeDtypeStruct((t, HP), F32)] * 2 + [jax.ShapeDtypeStruct(a.shape, a.dtype) for a in slabs],
        scratch_shapes=_comm_sems(n) if n else [],
        compiler_params=_params("arbitrary", "arbitrary"),
    )(k, v, q, do, lse_t, dl_t, *slabs)
    return out[0], out[1], out[2:]


def _pre_bwd(dh2, h, cq, ckv, dqa, dka, dka_next, dva, dva_next, dqb, dkf, dvb, g1, win, gq, wqu, gkv, wkv, tabs):
    t = h.shape[0]
    tm = _tile(t)

    def body(dh2_ref, h_ref, cq_ref, ckv_ref, dqa_ref, dka_ref, dkan_ref, dva_ref, dvan_ref, dqb_ref, dkf_ref, dvb_ref,
             g1_ref, win_ref, gq_ref, wqu_ref, gkv_ref, wkv_ref, tab_ref,
             dh_ref, dp_ref, dqbo_ref, dkvo_ref, dg1_ref, dgq_ref, dgkv_ref):
        first = pl.program_id(0) == 0
        ca, sa1, sa2, cb, sb1, sb2, ck = _tabs(tab_ref)
        dkr = jnp.zeros((tm, LANE), F32)
        for c in range(MLA_HEADS):
            sl = slice(LANE * c, LANE * (c + 1))
            dqbo_ref[:, sl] = _rope_t(dqb_ref[:, sl], cb, sb1, sb2, 16).astype(BF16)
            dkr += dkf_ref[:, sl]
        dkvo_ref[:, :HP] = dkf_ref[...].astype(BF16)
        dkvo_ref[:, HP:] = dvb_ref[...].astype(BF16)
        dcq, dgq = _rms_bwd(cq_ref[...], gq_ref[...], _dot_nt(dqbo_ref[...], wqu_ref[...]), MLA_Q_RANK)
        dckv, dgkv = _rms_bwd(ckv_ref[...], gkv_ref[...], _dot_nt(dkvo_ref[...], wkv_ref[...]), MLA_KV_RANK)
        for c in range(SWA_HEADS):
            sl = slice(LANE * c, LANE * (c + 1))
            dp_ref[:, PO_QA + LANE * c:PO_QA + LANE * (c + 1)] = _rope_t(dqa_ref[:, sl], ca, sa1, sa2, 32).astype(BF16)
        for c in range(SWA_KV_HEADS):
            sl = slice(LANE * c, LANE * (c + 1))
            dk = dka_ref[:, sl] + dkan_ref[:, sl]
            dp_ref[:, PO_KA + LANE * c:PO_KA + LANE * (c + 1)] = _rope_t(dk, ca, sa1, sa2, 32).astype(BF16)
        dp_ref[:, PO_VA:PO_CQ] = (dva_ref[...] + dvan_ref[...]).astype(BF16)
        dp_ref[:, PO_CQ:PO_CKV] = dcq.astype(BF16)
        dp_ref[:, PO_CKV:PO_KR] = dckv.astype(BF16)
        dp_ref[:, PO_KR:PW_IN] = _rope_t(dkr, ck, sb1, sb2, 16).astype(BF16)
        dx, dg1 = _rms_bwd(h_ref[...], g1_ref[...], _dot_nt(dp_ref[...], win_ref[...]), D_MODEL)
        dh_ref[...] = dh2_ref[...] + dx
        _acc(dg1_ref, dg1, first)
        _acc(dgq_ref, dgq, first)
        _acc(dgkv_ref, dgkv, first)

    kv = _row(tm, 2 * LANE)
    return pl.pallas_call(
        body, name="pre_bwd", grid=(t // tm,),
        in_specs=[_row(tm, D_MODEL), _row(tm, D_MODEL), _row(tm, MLA_Q_RANK), _row(tm, MLA_KV_RANK), _row(tm, HP),
                  kv, kv, kv, kv, _row(tm, HP), _row(tm, HP), _row(tm, HP),
                  _const(g1.shape), _const(win.shape), _const(gq.shape), _const(wqu.shape), _const(gkv.shape),
                  _const(wkv.shape), _row(tm, N_TAB * LANE)],
        out_specs=[_row(tm, D_MODEL), _row(tm, PW_IN), _row(tm, HP), _row(tm, 2 * HP),
                   _const((1, D_MODEL)), _const((1, MLA_Q_RANK)), _const((1, MLA_KV_RANK))],
        out_shape=[jax.ShapeDtypeStruct((t, D_MODEL), F32), jax.ShapeDtypeStruct((t, PW_IN), BF16),
                   jax.ShapeDtypeStruct((t, HP), BF16), jax.ShapeDtypeStruct((t, 2 * HP), BF16),
                   jax.ShapeDtypeStruct((1, D_MODEL), F32), jax.ShapeDtypeStruct((1, MLA_Q_RANK), F32),
                   jax.ShapeDtypeStruct((1, MLA_KV_RANK), F32)],
        compiler_params=_params("arbitrary"),
    )(dh2, h, cq, ckv, dqa, dka, dka_next, dva, dva_next, dqb, dkf, dvb, g1, win, gq, wqu, gkv, wkv, tabs)


def _rope_tables(t):
    pos = (jnp.arange(t, dtype=jnp.int32) - FRONT).astype(F32)[:, None]
    lane = jnp.arange(LANE)[None, :]

    def table(dim, start):
        half = dim // 2
        inv = ROPE_THETA ** (-jnp.arange(0, dim, 2, dtype=F32) / dim)
        ang = pos * inv[None, :]
        cos = jnp.concatenate([jnp.cos(ang)] * 2, axis=1)
        sin = jnp.concatenate([jnp.sin(ang)] * 2, axis=1)
        pad = lambda a: jnp.pad(a, ((0, 0), (start, LANE - start - dim)))
        first = (lane >= start) & (lane < start + half)
        second = (lane >= start + half) & (lane < start + dim)
        return pad(cos), jnp.where(first, -pad(sin), 0.0), jnp.where(second, pad(sin), 0.0)

    ca, sa1, sa2 = table(SWA_HEAD_DIM, 0)
    ck, sb1, sb2 = table(MLA_ROPE_DIM, MLA_NOPE_DIM)
    cb = jnp.where(lane < MLA_NOPE_DIM, 1.0, ck)
    return jnp.concatenate([ca, sa1, sa2, cb, sb1, sb2, ck], axis=1)


def _pad_heads(w, heads, dim, axis):
    shp = w.shape
    w = w.reshape(shp[:axis] + (heads, dim) + shp[axis + 1:])
    pad = [(0, 0)] * w.ndim
    pad[axis + 1] = (0, LANE - dim)
    return jnp.pad(w, pad).reshape(shp[:axis] + (heads * LANE,) + shp[axis + 1:])


def _unpad_heads(w, heads, dim, axis):
    shp = w.shape
    w = w.reshape(shp[:axis] + (heads, LANE) + shp[axis + 1:])
    w = lax.slice_in_dim(w, 0, dim, axis=axis + 1)
    return w.reshape(shp[:axis] + (heads * dim,) + shp[axis + 1:])


def _pad_layer(w_in, w_q_up, w_kv_up, w_o, out_norm_swa, out_norm_mla):
    o1 = SWA_Q_W
    o2 = o1 + SWA_KV_W
    o3 = o2 + SWA_KV_W
    o4 = o3 + MLA_Q_RANK
    o5 = o4 + MLA_KV_RANK
    kr = jnp.pad(w_in[:, o5:], ((0, 0), (MLA_NOPE_DIM, LANE - MLA_QK_DIM)))
    win = jnp.concatenate([
        _pad_heads(w_in[:, :o1], SWA_HEADS, SWA_HEAD_DIM, 1),
        _pad_heads(w_in[:, o1:o2], SWA_KV_HEADS, SWA_HEAD_DIM, 1),
        _pad_heads(w_in[:, o2:o3], SWA_KV_HEADS, SWA_HEAD_DIM, 1),
        w_in[:, o3:o5], kr], axis=1)
    wqu = _pad_heads(w_q_up, MLA_HEADS, MLA_QK_DIM, 1)
    kv = w_kv_up.reshape(MLA_KV_RANK, MLA_HEADS, MLA_NOPE_DIM + MLA_V_DIM)
    wkv = jnp.concatenate([
        _pad_heads(kv[:, :, :MLA_NOPE_DIM].reshape(MLA_KV_RANK, -1), MLA_HEADS, MLA_NOPE_DIM, 1),
        _pad_heads(kv[:, :, MLA_NOPE_DIM:].reshape(MLA_KV_RANK, -1), MLA_HEADS, MLA_V_DIM, 1)], axis=1)
    wo = jnp.concatenate([_pad_heads(w_o[:SWA_Q_W], SWA_HEADS, SWA_HEAD_DIM, 0),
                          _pad_heads(w_o[SWA_Q_W:], MLA_HEADS, MLA_V_DIM, 0)], axis=0)
    ga = _pad_heads(out_norm_swa[None, :], SWA_HEADS, SWA_HEAD_DIM, 1)
    gb = _pad_heads(out_norm_mla[None, :], MLA_HEADS, MLA_V_DIM, 1)
    return win, wqu, wkv, wo, ga, gb


def _unpad_layer(dwin, dwqu, dwkv, dwo, dga, dgb):
    d_w_in = jnp.concatenate([
        _unpad_heads(dwin[:, PO_QA:PO_KA], SWA_HEADS, SWA_HEAD_DIM, 1),
        _unpad_heads(dwin[:, PO_KA:PO_VA], SWA_KV_HEADS, SWA_HEAD_DIM, 1),
        _unpad_heads(dwin[:, PO_VA:PO_CQ], SWA_KV_HEADS, SWA_HEAD_DIM, 1),
        dwin[:, PO_CQ:PO_KR], dwin[:, PO_KR + MLA_NOPE_DIM:PO_KR + MLA_QK_DIM]], axis=1)
    d_w_q_up = _unpad_heads(dwqu, MLA_HEADS, MLA_QK_DIM, 1)
    dk = _unpad_heads(dwkv[:, :HP], MLA_HEADS, MLA_NOPE_DIM, 1).reshape(MLA_KV_RANK, MLA_HEADS, MLA_NOPE_DIM)
    dv = _unpad_heads(dwkv[:, HP:], MLA_HEADS, MLA_V_DIM, 1).reshape(MLA_KV_RANK, MLA_HEADS, MLA_V_DIM)
    d_w_kv_up = jnp.concatenate([dk, dv], axis=2).reshape(MLA_KV_RANK, -1)
    d_w_o = jnp.concatenate([_unpad_heads(dwo[:HP], SWA_HEADS, SWA_HEAD_DIM, 0),
                             _unpad_heads(dwo[HP:], MLA_HEADS, MLA_V_DIM, 0)], axis=0)
    d_ga = _unpad_heads(dga, SWA_HEADS, SWA_HEAD_DIM, 1)[0]
    d_gb = _unpad_heads(dgb, MLA_HEADS, MLA_V_DIM, 1)[0]
    return d_w_in, d_w_q_up, d_w_kv_up, d_w_o, d_ga, d_gb


def _shift_up(a):
    return jnp.concatenate([a[BLOCK:], jnp.zeros((BLOCK, a.shape[1]), a.dtype)], axis=0)


def _train_example(x, target, meta, vec, weights):
    s = x.shape[0]
    depth = vec["attn_norm"].shape[0]
    t = FRONT + N_META + s
    assert t % BLOCK == 0
    tq = _tile(t)
    nq = t // tq
    tabs = _rope_tables(t)
    h = jnp.concatenate([jnp.zeros((FRONT, D_MODEL), F32), meta, x], axis=0)
    tgt = jnp.concatenate([jnp.zeros((FRONT + N_META, D_MODEL), F32), target], axis=0)
    row = lambda v: v[None, :]

    saved = []
    for l in range(depth):
        win, wqu, wkv, wo, ga, gb = _pad_layer(*weights.attn(l), vec["out_norm_swa"][l], vec["out_norm_mla"][l])
        g1, gq, gkv, g2 = (row(vec[n][l]) for n in ("attn_norm", "q_norm", "kv_norm", "ffn_norm"))
        sk = row(vec["sinks"][l])
        u, qa, ka, va, cq, ckv, qn, kvn, qb, kf, vb = _pre_fwd(h, g1, win, gq, wqu, gkv, wkv, tabs)
        oa = _swa_fwd(sk, qa, ka, va)
        ob, lse = weights.mla_fwd(l, qb, kf, vb)
        h2, mix, u2 = _mix_fwd(h, oa, ob, ga, gb, wo, g2)
        wg, wu, wd = weights.ffn(l)
        h3, gt, up = _ffn_fwd(h2, u2, wg, wu, wd)
        saved.append((h, u, qa, ka, va, cq, ckv, qn, kvn, qb, kf, vb, oa, ob, lse, h2, mix, u2, gt, up,
                      win, wqu, wkv, wo, ga, gb, g1, gq, gkv, g2, sk, wg, wu, wd))
        h = h3

    dh, d_final, loss = _loss_bwd(h, row(vec["final_norm"]), tgt)

    grads = []
    to_rows = lambda a: a[:, ::LANE].T.reshape(MLA_HEADS * nq, 1, tq)
    for l in reversed(range(depth)):
        (h0, u, qa, ka, va, cq, ckv, qn, kvn, qb, kf, vb, oa, ob, lse, h2, mix, u2, gt, up,
         win, wqu, wkv, wo, ga, gb, g1, gq, gkv, g2, sk, wg, wu, wd) = saved[l]
        dff = wg.shape[1]
        act, dgu, dhb = _ffn_bwd_a(dh, gt, up, wd)
        d_w_gu = _tn_matmul(u2, dgu, "dw_gate_up")
        weights.ffn_grads(l, d_w_gu[:, :dff], d_w_gu[:, dff:], _tn_matmul(act, dhb, "dw_down"))
        dh2, dh2b, d_g2 = _ffn_bwd_b(dh, dgu, h2, g2, wg, wu)
        d_wo = _tn_matmul(mix, dh2b, "dw_o")
        doa, dob, d_ga, d_gb = _mix_bwd(dh2b, oa, ob, ga, gb, wo)
        dqa, dkc, dkp, dvc, dvp, dsink = _swa_bwd(sk, qa, ka, va, oa, doa)
        dqb, dl = _mla_bwd_dq(qb, kf, vb, ob, dob, lse)
        dkf, dvb = weights.mla_bwd_dkv(l, qb, kf, vb, dob, to_rows(lse), to_rows(dl))
        dh, dp, dqbo, dkvo, d_g1, d_gq, d_gkv = _pre_bwd(
            dh2, h0, cq, ckv, dqa, dkc, _shift_up(dkp), dvc, _shift_up(dvp), dqb, dkf, dvb,
            g1, win, gq, wqu, gkv, wkv, tabs)
        d_win = _tn_matmul(u, dp, "dw_in")
        d_wqu = _tn_matmul(qn, dqbo, "dw_q_up")
        d_wkv = _tn_matmul(kvn, dkvo, "dw_kv_up")
        d_w_in, d_w_q_up, d_w_kv_up, d_w_o, d_sw, d_ml = _unpad_layer(d_win, d_wqu, d_wkv, d_wo, d_ga, d_gb)
        weights.attn_grads(l, d_w_in, d_w_q_up, d_w_kv_up, d_w_o)
        grads.append(dict(attn_norm=d_g1[0], q_norm=d_gq[0], kv_norm=d_gkv[0], sinks=dsink[:, 0], out_norm_swa=d_sw,
                          out_norm_mla=d_ml, ffn_norm=d_g2[0]))
    grads = grads[::-1]
    stacked = {k: jnp.stack([g[k] for g in grads]) for k in grads[0]}
    stacked["final_norm"] = d_final[0]
    return loss[0, 0], dh[FRONT + N_META:], dh[FRONT:FRONT + N_META], stacked


MESH = pl.DeviceIdType.MESH
ANY = pl.BlockSpec(memory_space=pl.ANY)


def _place():
    return lax.axis_index("x"), lax.axis_index("y"), lax.axis_index("c")


def _index(x, y, c):
    return 4 * x + 2 * y + c


def _comm_sems(n):
    return [pltpu.SemaphoreType.DMA((n, N_DEV - 1)), pltpu.SemaphoreType.DMA((n, N_DEV - 1)),
            pltpu.SemaphoreType.DMA((n,))]


class _gather_plan:
    def __init__(self, x_refs, out_refs, send_sems, recv_sems, local_sems):
        self.x_refs, self.out_refs = x_refs, out_refs
        self.send_sems, self.recv_sems, self.local_sems = send_sems, recv_sems, local_sems
        self.n = len(x_refs)

    def _where(self):
        x, y, c = _place()
        return (x, y, c), (x, y, 1 - c), [(1 - x, y), (x, 1 - y), (1 - x, 1 - y)], c

    def _copy(self, i, k, block, to, from_input=False):
        slot = self.out_refs[i].at[_index(*block)]
        return pltpu.make_async_remote_copy(
            src_ref=self.x_refs[i] if from_input else slot, dst_ref=slot,
            send_sem=self.send_sems.at[i, k], recv_sem=self.recv_sems.at[i, k], device_id=to, device_id_type=MESH)

    def _mine(self, i, me):
        return pltpu.make_async_copy(self.x_refs[i], self.out_refs[i].at[_index(*me)], self.local_sems.at[i])

    def _first(self, me, sibling, chips, c):
        out = [self._copy(i, 1 + j, me, (*chip, c), True) for j, chip in enumerate(chips) for i in range(self.n)]
        return out + [self._copy(i, 0, me, sibling, True) for i in range(self.n)]

    def start(self):
        me, sibling, chips, c = self._where()
        for i in range(self.n):
            self._mine(i, me).start()
        for cp in self._first(me, sibling, chips, c):
            cp.start()

    def forward(self):
        me, sibling, chips, c = self._where()
        for j, chip in enumerate(chips):
            for i in range(self.n):
                self._copy(i, 1 + j, (*chip, c), me).wait_recv()
                self._copy(i, 4 + j, (*chip, c), sibling).start()

    def finish(self):
        me, sibling, chips, c = self._where()
        for i in range(self.n):
            self._copy(i, 0, sibling, me).wait_recv()
            for j, chip in enumerate(chips):
                self._copy(i, 4 + j, (*chip, 1 - c), me).wait_recv()
        for cp in self._first(me, sibling, chips, c):
            cp.wait_send()
        for j, chip in enumerate(chips):
            for i in range(self.n):
                self._copy(i, 4 + j, (*chip, c), sibling).wait_send()
        for i in range(self.n):
            self._mine(i, me).wait()


class _exchange_plan:
    def __init__(self, in_refs, out_refs, send_sems, recv_sems, local_sems):
        self.in_refs, self.out_refs = in_refs, out_refs
        self.send_sems, self.recv_sems, self.local_sems = send_sems, recv_sems, local_sems
        self.n = len(in_refs)

    def _copies(self):
        x, y, c = _place()
        me = _index(x, y, c)
        mine = [pltpu.make_async_copy(self.in_refs[i].at[me], self.out_refs[i].at[me], self.local_sems.at[i])
                for i in range(self.n)]
        remote = []
        for k in range(1, N_DEV):
            peer = (1 - x if k & 4 else x, 1 - y if k & 2 else y, 1 - c if k & 1 else c)
            remote += [pltpu.make_async_remote_copy(
                src_ref=self.in_refs[i].at[_index(*peer)], dst_ref=self.out_refs[i].at[me],
                send_sem=self.send_sems.at[i, k - 1], recv_sem=self.recv_sems.at[i, k - 1],
                device_id=peer, device_id_type=MESH) for i in range(self.n)]
        return mine, remote

    def start(self):
        mine, remote = self._copies()
        for cp in mine + remote:
            cp.start()

    def finish(self):
        mine, remote = self._copies()
        for cp in remote:
            cp.wait_recv()
        for cp in remote:
            cp.wait_send()
        for cp in mine:
            cp.wait()


def _all_gather(shards, name):
    n = len(shards)

    def body(*refs):
        plan = _gather_plan(refs[:n], refs[n:2 * n], *refs[2 * n:])
        plan.start()
        plan.forward()
        plan.finish()

    return pl.pallas_call(
        body, name=name, in_specs=[ANY] * n, out_specs=[ANY] * n, scratch_shapes=_comm_sems(n),
        out_shape=[jax.ShapeDtypeStruct((N_DEV,) + a.shape, a.dtype) for a in shards],
    )(*shards)


def _exchange(slabs, name):
    n = len(slabs)

    def body(*refs):
        plan = _exchange_plan(refs[:n], refs[n:2 * n], *refs[2 * n:])
        plan.start()
        plan.finish()

    return pl.pallas_call(
        body, name=name, in_specs=[ANY] * n, out_specs=[ANY] * n, scratch_shapes=_comm_sems(n),
        out_shape=[jax.ShapeDtypeStruct(a.shape, a.dtype) for a in slabs],
    )(*slabs)


def _adamw(w, g, m, v):
    m = ADAM_B1 * m + (1.0 - ADAM_B1) * g
    v = ADAM_B2 * v + (1.0 - ADAM_B2) * (g * g)
    m_hat = m / (1.0 - ADAM_B1 ** ADAM_STEP)
    v_hat = v / (1.0 - ADAM_B2 ** ADAM_STEP)
    return -ADAM_LR * (m_hat / (jnp.sqrt(v_hat) + ADAM_EPS) + ADAM_WD * w), m, v


def _sum_slots(ref):
    g = ref[0].astype(F32)
    for s in range(1, N_DEV):
        g = g + ref[s].astype(F32)
    return g


def _reduce_adamw(parts, w, m, v, layer, outs, name):
    l, r, c = w.shape
    tile = next(t for t in (256, 128, r) if r % t == 0)

    def body(p_ref, w_ref, m_ref, v_ref, g0, d0, m0, v0, g_ref, d_ref, nm_ref, nv_ref):
        g = _sum_slots(p_ref)
        g_ref[...] = g
        d_ref[...], nm_ref[...], nv_ref[...] = _adamw(w_ref[...], g, m_ref[...], v_ref[...])

    blk = pl.BlockSpec((None, tile, c), lambda j: (layer, j, 0))
    return pl.pallas_call(
        body, name=name, grid=(r // tile,),
        in_specs=[pl.BlockSpec((N_DEV, tile, c), lambda j: (0, j, 0)), blk, blk, blk] + [ANY] * 4, out_specs=[blk] * 4,
        out_shape=[jax.ShapeDtypeStruct((l, r, c), F32)] * 4,
        input_output_aliases={4: 0, 5: 1, 6: 2, 7: 3},
        compiler_params=_params("parallel"),
    )(parts, w, m, v, *outs)


def _sum_parts(parts, name):
    _, r, c = parts.shape

    def body(p_ref, g_ref):
        g_ref[...] = _sum_slots(p_ref)

    return pl.pallas_call(body, name=name, out_shape=jax.ShapeDtypeStruct((r, c), F32))(parts)


def _adamw_call(w, g, m, v, name):
    def body(w_ref, g_ref, m_ref, v_ref, d_ref, nm_ref, nv_ref):
        d_ref[...], nm_ref[...], nv_ref[...] = _adamw(w_ref[...], g_ref[...], m_ref[...], v_ref[...])

    return pl.pallas_call(body, name=name, out_shape=[jax.ShapeDtypeStruct(w.shape, F32)] * 3)(w, g, m, v)


ATTN = ("w_in", "w_q_up", "w_kv_up", "w_o")
FFN = ("w_gate", "w_up", "w_down")
SHARD_AXIS = dict(w_in=1, w_q_up=1, w_kv_up=1, w_o=0, w_gate=1, w_up=1, w_down=0)
SMALL = ("attn_norm", "ffn_norm", "final_norm", "out_norm_swa", "out_norm_mla", "q_norm", "kv_norm", "sinks")
PACK_W = 1024
SMALL_ROWS = 16


def _pack(arrs, dtype):
    flat = jnp.concatenate([a.astype(dtype).reshape(-1) for a in arrs])
    return flat.reshape(-1, PACK_W)


def _unpack(packed, like):
    flat = packed.reshape(-1)
    out, off = [], 0
    for a in like:
        out.append(flat[off:off + a.size].reshape(a.shape))
        off += a.size
    return out


def _gather_to_full(gathered, axis):
    shp = list(gathered.shape[1:])
    shp[axis] *= N_DEV
    return jnp.moveaxis(gathered, 0, axis).reshape(shp)


def _full_to_slabs(full, axis):
    shp = list(full.shape)
    shp[axis:axis + 1] = [N_DEV, shp[axis] // N_DEV]
    return jnp.moveaxis(full.reshape(shp), axis, 0).astype(BF16)


class _ShardedWeights:
    def __init__(self, shards, depth):
        self.shards, self.depth = shards, depth
        self.gathered, self.pending, self.parts = {}, {}, {}
        self._gather([(n, 0) for n in ATTN], lambda xs: _all_gather(xs, "gather_attn0"))

    def _gather(self, keys, run):
        self.gathered.update(zip(keys, run([self.shards[n][l] for n, l in keys])))

    def _full(self, names, l):
        return tuple(_gather_to_full(self.gathered[n, l], SHARD_AXIS[n]) for n in names)

    def attn(self, l):
        return self._full(ATTN, l)

    def ffn(self, l):
        return self._full(FFN, l)

    def mla_fwd(self, l, q, k, v):
        keys = [(n, l) for n in FFN] + ([(n, l + 1) for n in ATTN] if l + 1 < self.depth else [])
        out = []
        self._gather(keys, lambda xs: out.extend(_mla_fwd(q, k, v, xs)) or out[2])
        return out[0], out[1]

    def _add(self, names, l, grads):
        for n, g in zip(names, grads):
            self.pending[n, l] = _full_to_slabs(g, SHARD_AXIS[n])

    def ffn_grads(self, l, *grads):
        self._add(FFN, l, grads)

    def attn_grads(self, l, *grads):
        self._add(ATTN, l, grads)

    def _exchange(self, run):
        keys = list(self.pending)
        self.parts.update(zip(keys, run([self.pending.pop(k) for k in keys])))

    def mla_bwd_dkv(self, l, *args):
        out = []
        self._exchange(lambda xs: out.extend(_mla_bwd_dkv(*args, xs)) or out[2])
        return out[0], out[1]

    def flush(self):
        self._exchange(lambda xs: _exchange(xs, "exchange_attn0"))


def kernel(x, meta_tokens, attn_norm, w_in, q_norm, w_q_up, kv_norm, w_kv_up, sinks, out_norm_swa, out_norm_mla, w_o, ffn_norm, w_gate, w_up, w_down, final_norm, loss_target, m_meta_tokens, m_attn_norm, m_w_in, m_q_norm, m_w_q_up, m_kv_norm, m_w_kv_up, m_sinks, m_out_norm_swa, m_out_norm_mla, m_w_o, m_ffn_norm, m_w_gate, m_w_up, m_w_down, m_final_norm, v_meta_tokens, v_attn_norm, v_w_in, v_q_norm, v_w_q_up, v_kv_norm, v_w_kv_up, v_sinks, v_out_norm_swa, v_out_norm_mla, v_w_o, v_ffn_norm, v_w_gate, v_w_up, v_w_down, v_final_norm):
    w = dict(meta_tokens=meta_tokens, attn_norm=attn_norm, w_in=w_in, q_norm=q_norm, w_q_up=w_q_up, kv_norm=kv_norm,
             w_kv_up=w_kv_up, sinks=sinks, out_norm_swa=out_norm_swa, out_norm_mla=out_norm_mla, w_o=w_o,
             ffn_norm=ffn_norm, w_gate=w_gate, w_up=w_up, w_down=w_down, final_norm=final_norm)
    m = dict(meta_tokens=m_meta_tokens, attn_norm=m_attn_norm, w_in=m_w_in, q_norm=m_q_norm, w_q_up=m_w_q_up,
             kv_norm=m_kv_norm, w_kv_up=m_w_kv_up, sinks=m_sinks, out_norm_swa=m_out_norm_swa,
             out_norm_mla=m_out_norm_mla, w_o=m_w_o, ffn_norm=m_ffn_norm, w_gate=m_w_gate, w_up=m_w_up,
             w_down=m_w_down, final_norm=m_final_norm)
    v = dict(meta_tokens=v_meta_tokens, attn_norm=v_attn_norm, w_in=v_w_in, q_norm=v_q_norm, w_q_up=v_w_q_up,
             kv_norm=v_kv_norm, w_kv_up=v_w_kv_up, sinks=v_sinks, out_norm_swa=v_out_norm_swa,
             out_norm_mla=v_out_norm_mla, w_o=v_w_o, ffn_norm=v_ffn_norm, w_gate=v_w_gate, w_up=v_w_up,
             w_down=v_w_down, final_norm=v_final_norm)
    names = list(w)
    big = ATTN + FFN
    depth = w_in.shape[0]
    me = _index(*_place())

    weights = _ShardedWeights({n: w[n].astype(BF16) for n in big}, depth)
    meta = jnp.moveaxis(_all_gather([meta_tokens], "gather_meta")[0], 0, 1).reshape(N_META, D_MODEL)
    loss, grad_x, d_meta, grads = _train_example(x[0], loss_target[0], meta, {n: w[n] for n in SMALL}, weights)
    weights.flush()

    g_big, d_big, m_big, v_big = {}, {}, {}, {}
    for n in big:
        outs = [lax.empty(w[n].shape, F32) for _ in range(4)]
        for l in reversed(range(depth)):
            outs = _reduce_adamw(weights.parts[n, l], w[n], m[n], v[n], l, outs, "reduce_adamw_" + n)
        g_big[n], d_big[n], m_big[n], v_big[n] = outs

    small = [grads[n] for n in SMALL] + [loss.reshape(1)]
    pad = SMALL_ROWS * PACK_W - sum(a.size for a in small)
    part = jnp.concatenate([_pack(small + [jnp.zeros((pad,), F32)], F32), d_meta], axis=0)
    total = _sum_parts(_all_gather([part], "gather_small")[0], "sum_small")
    small_w = [w[n] for n in SMALL]
    packs = [_pack([d[n] for n in SMALL] + [jnp.zeros((pad + 1,), F32)], F32) for d in (w, m, v)]
    upd = _adamw_call(packs[0], total[:SMALL_ROWS], packs[1], packs[2], "adamw_small")
    g_small, d_small, m_small, v_small = [dict(zip(SMALL, _unpack(p, small_w))) for p in (total[:SMALL_ROWS],) + tuple(upd)]
    loss_total = total[:SMALL_ROWS].reshape(-1)[SMALL_ROWS * PACK_W - pad - 1]
    g_meta = lax.dynamic_slice_in_dim(total[SMALL_ROWS:], me * LANE, LANE, axis=1)
    d_mt, m_mt, v_mt = _adamw_call(meta_tokens, g_meta, m_meta_tokens, v_meta_tokens, "adamw_meta")

    outs = []
    for got in ({**g_big, **g_small, "meta_tokens": g_meta}, {**d_big, **d_small, "meta_tokens": d_mt},
                {**m_big, **m_small, "meta_tokens": m_mt}, {**v_big, **v_small, "meta_tokens": v_mt}):
        outs += [got[n] for n in names]
    return (loss_total, grad_x[None], *outs)
```

```python
import jax
import jax.numpy as jnp
from jax import lax
from jax.experimental import pallas as pl
from jax.experimental.pallas import tpu as pltpu

F32 = jnp.float32
BF16 = jnp.bfloat16

D_MODEL = 1024
N_META = 16
BLOCK = 128
FRONT = (-N_META) % BLOCK
ROPE_THETA = 10000.0
EPS = 1e-6
NEG = -1e30
SWA_HEADS = 8
SWA_KV_HEADS = 2
SWA_GROUP = SWA_HEADS // SWA_KV_HEADS
SWA_HEAD_DIM = 64
MLA_HEADS = 8
MLA_Q_RANK = 256
MLA_KV_RANK = 128
MLA_NOPE_DIM = 64
MLA_ROPE_DIM = 32
MLA_V_DIM = 64
MLA_QK_DIM = MLA_NOPE_DIM + MLA_ROPE_DIM
SWA_Q_W = SWA_HEADS * SWA_HEAD_DIM
SWA_KV_W = SWA_KV_HEADS * SWA_HEAD_DIM
MLA_OUT_W = MLA_HEADS * MLA_V_DIM
SCALE_A = SWA_HEAD_DIM ** -0.5
SCALE_B = MLA_QK_DIM ** -0.5
LOG2E = 1.4426950408889634
Q_SCALE = SCALE_B * LOG2E
ADAM_LR = 0.001
ADAM_B1 = 0.9
ADAM_B2 = 0.999
ADAM_EPS = 1e-08
ADAM_WD = 0.01
ADAM_STEP = 10

LANE = 128
N_DEV = 8
HP = 8 * LANE
PO_QA, PO_KA, PO_VA = 0, HP, HP + 2 * LANE
PO_CQ = PO_VA + 2 * LANE
PO_CKV = PO_CQ + MLA_Q_RANK
PO_KR = PO_CKV + MLA_KV_RANK
PW_IN = PO_KR + LANE
N_TAB = 7
VMEM_LIMIT = 56 * 2 ** 20
TN_VMEM_BUDGET = 36 * 2 ** 20
MLA_HB = 4

NT = (((1,), (1,)), ((), ()))
TN = (((0,), (0,)), ((), ()))


def _tile(t):
    return 384 if t % 384 == 0 else 128


def _params(*sem):
    return pltpu.CompilerParams(dimension_semantics=sem, vmem_limit_bytes=VMEM_LIMIT)


def _row(tm, n):
    return pl.BlockSpec((tm, n), lambda i: (i, 0))


def _const(shape):
    return pl.BlockSpec(shape, lambda i: (0,) * len(shape))


def _dot(a, b):
    return jnp.dot(a, b, preferred_element_type=F32)


def _dot_nt(a, b):
    return lax.dot_general(a, b, NT, preferred_element_type=F32)


def _dot_tn(a, b):
    return lax.dot_general(a, b, TN, preferred_element_type=F32)


def _rope(x, c, s1, s2, shift):
    return x * c + pltpu.roll(x, LANE - shift, 1) * s1 + pltpu.roll(x, shift, 1) * s2


def _rope_t(dy, c, s1, s2, shift):
    return dy * c + pltpu.roll(dy * s1, shift, 1) + pltpu.roll(dy * s2, LANE - shift, 1)


def _rms_r(x, n):
    return lax.rsqrt(jnp.sum(x * x, axis=-1, keepdims=True) * (1.0 / n) + EPS)


def _rms_bwd(x, g, dy, n):
    r = _rms_r(x, n)
    xh = x * r
    dxh = dy * g
    dx = r * (dxh - xh * (jnp.sum(dxh * xh, axis=-1, keepdims=True) * (1.0 / n)))
    return dx, jnp.sum(dy * xh, axis=0, keepdims=True)


def _acc(ref, val, first):
    @pl.when(first)
    def _():
        ref[...] = val

    @pl.when(jnp.logical_not(first))
    def _():
        ref[...] += val


def _tabs(tab_ref):
    return [tab_ref[:, LANE * i:LANE * (i + 1)] for i in range(N_TAB)]


def _pre_fwd(h, g1, win, gq, wqu, gkv, wkv, tabs):
    t = h.shape[0]
    tm = _tile(t)

    def body(h_ref, g1_ref, win_ref, gq_ref, wqu_ref, gkv_ref, wkv_ref, tab_ref,
             u_ref, qa_ref, ka_ref, va_ref, cq_ref, ckv_ref, qn_ref, kvn_ref, qb_ref, kf_ref, vb_ref):
        ca, sa1, sa2, cb, sb1, sb2, ck = _tabs(tab_ref)
        hv = h_ref[...]
        u = (hv * _rms_r(hv, D_MODEL) * g1_ref[...]).astype(BF16)
        u_ref[...] = u
        p = _dot(u, win_ref[...])
        for c in range(SWA_HEADS):
            sl = slice(LANE * c, LANE * (c + 1))
            qa_ref[:, sl] = _rope(p[:, PO_QA + LANE * c:PO_QA + LANE * (c + 1)], ca, sa1, sa2, 32).astype(BF16)
        for c in range(SWA_KV_HEADS):
            sl = slice(LANE * c, LANE * (c + 1))
            ka_ref[:, sl] = _rope(p[:, PO_KA + LANE * c:PO_KA + LANE * (c + 1)], ca, sa1, sa2, 32).astype(BF16)
        va_ref[...] = p[:, PO_VA:PO_CQ].astype(BF16)
        cq = p[:, PO_CQ:PO_CKV]
        ckv = p[:, PO_CKV:PO_KR]
        cq_ref[...] = cq
        ckv_ref[...] = ckv
        qn = (cq * _rms_r(cq, MLA_Q_RANK) * gq_ref[...]).astype(BF16)
        qn_ref[...] = qn
        qb = _dot(qn, wqu_ref[...])
        kvn = (ckv * _rms_r(ckv, MLA_KV_RANK) * gkv_ref[...]).astype(BF16)
        kvn_ref[...] = kvn
        kv = _dot(kvn, wkv_ref[...])
        kr = _rope(p[:, PO_KR:PW_IN], ck, sb1, sb2, 16)
        for c in range(MLA_HEADS):
            sl = slice(LANE * c, LANE * (c + 1))
            qb_ref[:, sl] = (_rope(qb[:, sl], cb, sb1, sb2, 16) * Q_SCALE).astype(BF16)
            kf_ref[:, sl] = (kv[:, sl] + kr).astype(BF16)
        vb_ref[...] = kv[:, HP:].astype(BF16)

    widths = [(D_MODEL, BF16), (HP, BF16), (2 * LANE, BF16), (2 * LANE, BF16), (MLA_Q_RANK, F32),
              (MLA_KV_RANK, F32), (MLA_Q_RANK, BF16), (MLA_KV_RANK, BF16), (HP, BF16), (HP, BF16), (HP, BF16)]
    return pl.pallas_call(
        body, name="pre_fwd", grid=(t // tm,),
        in_specs=[_row(tm, D_MODEL), _const(g1.shape), _const(win.shape), _const(gq.shape), _const(wqu.shape),
                  _const(gkv.shape), _const(wkv.shape), _row(tm, N_TAB * LANE)],
        out_specs=[_row(tm, w) for w, _ in widths],
        out_shape=[jax.ShapeDtypeStruct((t, w), d) for w, d in widths],
        compiler_params=_params("parallel"),
    )(h, g1, win, gq, wqu, gkv, wkv, tabs)


def _swa_mask(nb):
    row = lax.broadcasted_iota(jnp.int32, (SWA_GROUP * BLOCK, 2 * BLOCK), 0) & (BLOCK - 1)
    col = lax.broadcasted_iota(jnp.int32, (SWA_GROUP * BLOCK, 2 * BLOCK), 1)
    return (col > row) & (col <= row + BLOCK) & (col + (nb - 1) * BLOCK >= FRONT)


def _swa_group(ref, rows, j):
    return jnp.concatenate([ref[rows, LANE * (SWA_GROUP * j + g):LANE * (SWA_GROUP * j + g + 1)]
                            for g in range(SWA_GROUP)], axis=0)


def _swa_sinks(sink_ref, j):
    return jnp.concatenate([jnp.full((BLOCK, 1), sink_ref[0, SWA_GROUP * j + g], F32) for g in range(SWA_GROUP)], axis=0)


def _swa_keys(prev_ref, cur_ref, rb, j):
    sl = slice(LANE * j, LANE * (j + 1))
    if rb == 0:
        return jnp.concatenate([prev_ref[:, sl], cur_ref[:BLOCK, sl]], axis=0)
    return cur_ref[BLOCK * (rb - 1):BLOCK * (rb + 1), sl]


def _swa_chains(t):
    return [(rb, j) for rb in range(_tile(t) // BLOCK) for j in range(SWA_KV_HEADS)]


def _swa_scores(sink_ref, q_ref, kp_ref, kc_ref, n, t):
    r = _tile(t) // BLOCK
    chains = _swa_chains(t)
    qs = [_swa_group(q_ref, slice(BLOCK * rb, BLOCK * (rb + 1)), j) for rb, j in chains]
    ks = [_swa_keys(kp_ref, kc_ref, rb, j) for rb, j in chains]
    ss = [_dot_nt(q4, k2) for q4, k2 in zip(qs, ks)]
    masks = [_swa_mask(n * r + rb) for rb in range(r)]
    out = []
    for (rb, j), s in zip(chains, ss):
        sink = _swa_sinks(sink_ref, j)
        s = jnp.where(masks[rb], s * SCALE_A, NEG)
        m = jnp.maximum(jnp.max(s, axis=1, keepdims=True), sink)
        e = jnp.exp(s - m)
        es = jnp.exp(sink - m)
        inv = 1.0 / (jnp.sum(e, axis=1, keepdims=True) + es)
        out.append((e * inv, es * inv))
    return qs, ks, out


def _swa_specs(t):
    ts = _tile(t)
    r = ts // BLOCK
    prev = lambda n: (jnp.maximum(n * r - 1, 0), 0)
    cur = lambda n: (n, 0)
    return [pl.BlockSpec(memory_space=pltpu.SMEM), pl.BlockSpec((ts, HP), cur),
            pl.BlockSpec((BLOCK, 2 * LANE), prev), pl.BlockSpec((ts, 2 * LANE), cur),
            pl.BlockSpec((BLOCK, 2 * LANE), prev), pl.BlockSpec((ts, 2 * LANE), cur)]


def _swa_fwd(sinks, q, k, v):
    t = q.shape[0]
    ts = _tile(t)

    def body(sink_ref, q_ref, kp_ref, kc_ref, vp_ref, vc_ref, o_ref):
        chains = _swa_chains(t)
        _, _, probs = _swa_scores(sink_ref, q_ref, kp_ref, kc_ref, pl.program_id(0), t)
        os_ = [_dot(p.astype(BF16), _swa_keys(vp_ref, vc_ref, rb, j)) for (rb, j), (p, _) in zip(chains, probs)]
        for (rb, j), o4 in zip(chains, os_):
            for g in range(SWA_GROUP):
                hd = SWA_GROUP * j + g
                o_ref[BLOCK * rb:BLOCK * (rb + 1), LANE * hd:LANE * (hd + 1)] = o4[BLOCK * g:BLOCK * (g + 1)]

    return pl.pallas_call(
        body, name="swa_fwd", grid=(t // ts,),
        in_specs=_swa_specs(t),
        out_specs=pl.BlockSpec((ts, HP), lambda n: (n, 0)),
        out_shape=jax.ShapeDtypeStruct((t, HP), F32),
        compiler_params=_params("parallel"),
    )(sinks, q, k, k, v, v)


def _causal_mask(q0, k0, tq, tk, transposed):
    if transposed:
        key = k0 + lax.broadcasted_iota(jnp.int32, (tk, tq), 0)
        qry = q0 + lax.broadcasted_iota(jnp.int32, (tk, tq), 1)
    else:
        qry = q0 + lax.broadcasted_iota(jnp.int32, (tq, tk), 0)
        key = k0 + lax.broadcasted_iota(jnp.int32, (tq, tk), 1)
    return (key <= qry) & (key >= FRONT)


def _heads(ref, rows=slice(None)):
    return [ref[rows, LANE * a:LANE * (a + 1)] for a in range(MLA_HB)]


def _as_row(col):
    return jnp.broadcast_to(col, (col.shape[0], LANE)).T[:1, :]


def _row_stats(t):
    tq = _tile(t)
    shape = (MLA_HEADS // MLA_HB, MLA_HB, t // tq, 1, tq)
    return jax.ShapeDtypeStruct(shape, F32), pl.BlockSpec((None, MLA_HB, None, 1, tq), lambda h, i: (h, 0, i, 0, 0))


def _mla_fwd(q, k, v, shards=()):
    t = q.shape[0]
    tq = _tile(t)
    nq = t // tq
    n = len(shards)
    steps = (MLA_HEADS // MLA_HB) * nq

    def body(q_ref, k_ref, v_ref, *rest):
        x_refs, (o_ref, lse_ref, lser_ref), out_refs = rest[:n], rest[n:n + 3], rest[n + 3:2 * n + 3]
        acc_sc, sems = rest[2 * n + 3], rest[2 * n + 4:]
        i = pl.program_id(1)
        step_id = pl.program_id(0) * nq + i
        if n:
            plan = _gather_plan(x_refs, out_refs, *sems)
            pl.when(step_id == 0)(plan.start)
            pl.when(step_id == steps // 2)(plan.forward)
        qs = _heads(q_ref)
        acc_sc[...] = jnp.zeros(acc_sc.shape, F32)

        def step(j, carry, masked):
            rows = pl.ds(pl.multiple_of(j * tq, tq), tq)
            ks, vs = _heads(k_ref, rows), _heads(v_ref, rows)
            ss = [_dot_nt(qh, kh) for qh, kh in zip(qs, ks)]
            if masked:
                mask = _causal_mask(i * tq, j * tq, tq, tq, False)
                ss = [jnp.where(mask, s, NEG) for s in ss]
            mid, out = [], []
            for s, (m, l) in zip(ss, carry):
                mn = jnp.maximum(m, jnp.max(s, axis=1, keepdims=True))
                al = jnp.exp2(m - mn)
                p = jnp.exp2(s - mn)
                out.append((mn, al * l + jnp.sum(p, axis=1, keepdims=True)))
                mid.append((al, p.astype(BF16)))
            for a, ((al, p), vh) in enumerate(zip(mid, vs)):
                acc_sc[a] = al * acc_sc[a] + _dot(p, vh)
            return tuple(out)

        init = ((jnp.full((tq, 1), NEG, F32), jnp.zeros((tq, 1), F32)),) * MLA_HB
        carry = lax.fori_loop(0, jnp.minimum(i, 1) + 1, lambda it, c: step(it * i, c, True), init)
        carry = lax.fori_loop(1, i, lambda j, c: step(j, c, False), carry)
        for a, (m, l) in enumerate(carry):
            o_ref[:, LANE * a:LANE * (a + 1)] = acc_sc[a] * (1.0 / l)
            lse = m + jnp.log2(l)
            lse_ref[:, LANE * a:LANE * (a + 1)] = jnp.broadcast_to(lse, (tq, LANE))
            lser_ref[a] = _as_row(lse)
        if n:
            pl.when(step_id == steps - 1)(plan.finish)

    blk = pl.BlockSpec((tq, MLA_HB * LANE), lambda h, i: (i, h))
    full = pl.BlockSpec((t, MLA_HB * LANE), lambda h, i: (0, h))
    rows_shape, rows_spec = _row_stats(t)
    out = pl.pallas_call(
        body, name="mla_fwd_gather" if n else "mla_fwd", grid=(MLA_HEADS // MLA_HB, nq),
        in_specs=[blk, full, full] + [ANY] * n, out_specs=[blk, blk, rows_spec] + [ANY] * n,
        out_shape=[jax.ShapeDtypeStruct((t, HP), F32)] * 2 + [rows_shape]
        + [jax.ShapeDtypeStruct((N_DEV,) + a.shape, a.dtype) for a in shards],
        scratch_shapes=[pltpu.VMEM((MLA_HB, tq, LANE), F32)] + (_comm_sems(n) if n else []),
        compiler_params=_params("arbitrary", "arbitrary"),
    )(q, k, v, *shards)
    return out[0], (out[1], out[2]), out[3:]


def _mix_fwd(h, oa, ob, ga, gb, wo, g2):
    t = h.shape[0]
    tm = _tile(t)

    def body(h_ref, oa_ref, ob_ref, ga_ref, gb_ref, wo_ref, g2_ref, h2_ref, mix_ref, u2_ref):
        oa_v = oa_ref[...]
        ob_v = ob_ref[...]
        na = (oa_v * _rms_r(oa_v, SWA_Q_W) * ga_ref[...]).astype(BF16)
        nb = (ob_v * _rms_r(ob_v, MLA_OUT_W) * gb_ref[...]).astype(BF16)
        mix_ref[:, :HP] = na
        mix_ref[:, HP:] = nb
        h2 = h_ref[...] + _dot(na, wo_ref[:HP, :]) + _dot(nb, wo_ref[HP:, :])
        h2_ref[...] = h2
        u2_ref[...] = (h2 * _rms_r(h2, D_MODEL) * g2_ref[...]).astype(BF16)

    return pl.pallas_call(
        body, name="mix_fwd", grid=(t // tm,),
        in_specs=[_row(tm, D_MODEL), _row(tm, HP), _row(tm, HP), _const(ga.shape), _const(gb.shape),
                  _const(wo.shape), _const(g2.shape)],
        out_specs=[_row(tm, D_MODEL), _row(tm, 2 * HP), _row(tm, D_MODEL)],
        out_shape=[jax.ShapeDtypeStruct((t, D_MODEL), F32), jax.ShapeDtypeStruct((t, 2 * HP), BF16),
                   jax.ShapeDtypeStruct((t, D_MODEL), BF16)],
        compiler_params=_params("parallel"),
    )(h, oa, ob, ga, gb, wo, g2)


def _ffn_fwd(h2, u2, wg, wu, wd):
    t = h2.shape[0]
    tm = _tile(t)
    dff = wg.shape[1]

    def body(h2_ref, u2_ref, wg_ref, wu_ref, wd_ref, h3_ref, g_ref, up_ref):
        u2v = u2_ref[...]
        g = _dot(u2v, wg_ref[...])
        up = _dot(u2v, wu_ref[...])
        g_ref[...] = g.astype(BF16)
        up_ref[...] = up.astype(BF16)
        a = (g * jax.nn.sigmoid(g) * up).astype(BF16)
        h3_ref[...] = h2_ref[...] + _dot(a, wd_ref[...])

    return pl.pallas_call(
        body, name="ffn_fwd", grid=(t // tm,),
        in_specs=[_row(tm, D_MODEL), _row(tm, D_MODEL), _const(wg.shape), _const(wu.shape), _const(wd.shape)],
        out_specs=[_row(tm, D_MODEL), _row(tm, dff), _row(tm, dff)],
        out_shape=[jax.ShapeDtypeStruct((t, D_MODEL), F32), jax.ShapeDtypeStruct((t, dff), BF16),
                   jax.ShapeDtypeStruct((t, dff), BF16)],
        compiler_params=_params("parallel"),
    )(h2, u2, wg, wu, wd)


def _loss_bwd(h, gf, target):
    t = h.shape[0]
    tm = _tile(t)
    first_row = FRONT + N_META

    def body(h_ref, gf_ref, t_ref, dh_ref, dgf_ref, loss_ref):
        i = pl.program_id(0)
        hv = h_ref[...]
        y = hv * _rms_r(hv, D_MODEL) * gf_ref[...]
        row = i * tm + lax.broadcasted_iota(jnp.int32, (tm, 1), 0)
        err = jnp.where(row >= first_row, y - t_ref[...], 0.0)
        dx, dg = _rms_bwd(hv, gf_ref[...], err * (1.0 / D_MODEL), D_MODEL)
        dh_ref[...] = dx
        _acc(dgf_ref, dg, i == 0)
        part = 0.5 * jnp.sum(jnp.sum(err * err, axis=1, keepdims=True) * (1.0 / D_MODEL), axis=0, keepdims=True)
        _acc(loss_ref, jnp.broadcast_to(part, (1, LANE)), i == 0)

    return pl.pallas_call(
        body, name="loss_bwd", grid=(t // tm,),
        in_specs=[_row(tm, D_MODEL), _const(gf.shape), _row(tm, D_MODEL)],
        out_specs=[_row(tm, D_MODEL), _const((1, D_MODEL)), _const((1, LANE))],
        out_shape=[jax.ShapeDtypeStruct((t, D_MODEL), F32), jax.ShapeDtypeStruct((1, D_MODEL), F32),
                   jax.ShapeDtypeStruct((1, LANE), F32)],
        compiler_params=_params("arbitrary"),
    )(h, gf, target)


def _tn_matmul(a, b, name, cols=None):
    t, k = a.shape
    first, n = cols or (0, b.shape[1])
    tk = next(c for c in (k, 1024, 512, 256, 128) if k % c == 0 and c <= 1024)
    fits = lambda c: 2 * (t * (tk + c) * 2 + tk * c * 2) <= TN_VMEM_BUDGET
    tn = next(c for c in (n, 1024, 512, 256, 128) if n % c == 0 and first % c == 0 and fits(c))

    def body(a_ref, b_ref, o_ref):
        o_ref[...] = _dot_tn(a_ref[...], b_ref[...]).astype(BF16)

    return pl.pallas_call(
        body, name=name, grid=(k // tk, n // tn),
        in_specs=[pl.BlockSpec((t, tk), lambda i, j: (0, i)), pl.BlockSpec((t, tn), lambda i, j: (0, j + first // tn))],
        out_specs=pl.BlockSpec((tk, tn), lambda i, j: (i, j)),
        out_shape=jax.ShapeDtypeStruct((k, n), BF16),
        compiler_params=_params("parallel", "parallel"),
    )(a, b)


def _ffn_bwd_a(dh3, g, up, wd):
    t = dh3.shape[0]
    tm = _tile(t)
    dff = wd.shape[0]

    def body(dh3_ref, g_ref, up_ref, wd_ref, a_ref, dgu_ref, dh3b_ref):
        dh3b = dh3_ref[...].astype(BF16)
        dh3b_ref[...] = dh3b
        da = _dot_nt(dh3b, wd_ref[...])
        gv = g_ref[...].astype(F32)
        upv = up_ref[...].astype(F32)
        sg = jax.nn.sigmoid(gv)
        silu = gv * sg
        a_ref[...] = (silu * upv).astype(BF16)
        dgu_ref[:, :dff] = (da * upv * (sg * (1.0 + gv * (1.0 - sg)))).astype(BF16)
        dgu_ref[:, dff:] = (da * silu).astype(BF16)

    return pl.pallas_call(
        body, name="ffn_bwd_a", grid=(t // tm,),
        in_specs=[_row(tm, D_MODEL), _row(tm, dff), _row(tm, dff), _const(wd.shape)],
        out_specs=[_row(tm, dff), _row(tm, 2 * dff), _row(tm, D_MODEL)],
        out_shape=[jax.ShapeDtypeStruct((t, dff), BF16), jax.ShapeDtypeStruct((t, 2 * dff), BF16),
                   jax.ShapeDtypeStruct((t, D_MODEL), BF16)],
        compiler_params=_params("parallel"),
    )(dh3, g, up, wd)


def _ffn_bwd_b(dh3, dgu, h2, g2, wg, wu):
    t = dh3.shape[0]
    tm = _tile(t)
    dff = wg.shape[1]

    def body(dh3_ref, dgu_ref, h2_ref, g2_ref, wg_ref, wu_ref, dh2_ref, dh2b_ref, dg2_ref):
        du2 = _dot_nt(dgu_ref[:, :dff], wg_ref[...]) + _dot_nt(dgu_ref[:, dff:], wu_ref[...])
        dx, dg = _rms_bwd(h2_ref[...], g2_ref[...], du2, D_MODEL)
        dh2 = dh3_ref[...] + dx
        dh2_ref[...] = dh2
        dh2b_ref[...] = dh2.astype(BF16)
        _acc(dg2_ref, dg, pl.program_id(0) == 0)

    return pl.pallas_call(
        body, name="ffn_bwd_b", grid=(t // tm,),
        in_specs=[_row(tm, D_MODEL), _row(tm, 2 * dff), _row(tm, D_MODEL), _const(g2.shape), _const(wg.shape),
                  _const(wu.shape)],
        out_specs=[_row(tm, D_MODEL), _row(tm, D_MODEL), _const((1, D_MODEL))],
        out_shape=[jax.ShapeDtypeStruct((t, D_MODEL), F32), jax.ShapeDtypeStruct((t, D_MODEL), BF16),
                   jax.ShapeDtypeStruct((1, D_MODEL), F32)],
        compiler_params=_params("arbitrary"),
    )(dh3, dgu, h2, g2, wg, wu)


def _mix_bwd(dh2, oa, ob, ga, gb, wo):
    t = dh2.shape[0]
    tm = _tile(t)

    def body(dh2_ref, oa_ref, ob_ref, ga_ref, gb_ref, wo_ref, doa_ref, dob_ref, dga_ref, dgb_ref):
        first = pl.program_id(0) == 0
        d = dh2_ref[...].astype(BF16)
        dxa, dga = _rms_bwd(oa_ref[...], ga_ref[...], _dot_nt(d, wo_ref[:HP, :]), SWA_Q_W)
        dxb, dgb = _rms_bwd(ob_ref[...], gb_ref[...], _dot_nt(d, wo_ref[HP:, :]), MLA_OUT_W)
        doa_ref[...] = dxa.astype(BF16)
        dob_ref[...] = dxb.astype(BF16)
        _acc(dga_ref, dga, first)
        _acc(dgb_ref, dgb, first)

    return pl.pallas_call(
        body, name="mix_bwd", grid=(t // tm,),
        in_specs=[_row(tm, D_MODEL), _row(tm, HP), _row(tm, HP), _const(ga.shape), _const(gb.shape), _const(wo.shape)],
        out_specs=[_row(tm, HP), _row(tm, HP), _const((1, HP)), _const((1, HP))],
        out_shape=[jax.ShapeDtypeStruct((t, HP), BF16), jax.ShapeDtypeStruct((t, HP), BF16),
                   jax.ShapeDtypeStruct((1, HP), F32), jax.ShapeDtypeStruct((1, HP), F32)],
        compiler_params=_params("arbitrary"),
    )(dh2, oa, ob, ga, gb, wo)


def _swa_bwd(sinks, q, k, v, o, do):
    t = q.shape[0]
    ts = _tile(t)

    def body(sink_ref, q_ref, kp_ref, kc_ref, vp_ref, vc_ref, o_ref, do_ref,
             dq_ref, dkc_ref, dkp_ref, dvc_ref, dvp_ref, dsink_ref):
        n = pl.program_id(0)
        chains = _swa_chains(t)
        qs, ks, probs = _swa_scores(sink_ref, q_ref, kp_ref, kc_ref, n, t)
        dos = [_swa_group(do_ref, slice(BLOCK * rb, BLOCK * (rb + 1)), j) for rb, j in chains]
        vs = [_swa_keys(vp_ref, vc_ref, rb, j) for rb, j in chains]
        dps = [_dot_nt(do4, v2) for do4, v2 in zip(dos, vs)]
        dss, dsks = [], []
        for (rb, j), (p, psink), do4, dp in zip(chains, probs, dos, dps):
            o4 = _swa_group(o_ref, slice(BLOCK * rb, BLOCK * (rb + 1)), j)
            delta = jnp.sum(o4 * do4.astype(F32), axis=1, keepdims=True)
            dss.append(p * (dp - delta) * SCALE_A)
            dsks.append(-psink * delta)
        dqs = [_dot(ds.astype(BF16), k2) for ds, k2 in zip(dss, ks)]
        dks = [_dot(ds.T.astype(BF16), q4) for ds, q4 in zip(dss, qs)]
        dvs = [_dot(p.T.astype(BF16), do4) for (p, _), do4 in zip(probs, dos)]
        dsink = [jnp.zeros((1, LANE), F32)] * SWA_HEADS
        ext = {}
        for (rb, j), dq4, dk2, dv2, dsk in zip(chains, dqs, dks, dvs, dsks):
            for g in range(SWA_GROUP):
                hd = SWA_GROUP * j + g
                rows = slice(BLOCK * g, BLOCK * (g + 1))
                dq_ref[BLOCK * rb:BLOCK * (rb + 1), LANE * hd:LANE * (hd + 1)] = dq4[rows]
                dsink[hd] = dsink[hd] + jnp.sum(dsk[rows], axis=0, keepdims=True)
            for half in range(2):
                key = (j, rb + half)
                part = (dk2[BLOCK * half:BLOCK * (half + 1)], dv2[BLOCK * half:BLOCK * (half + 1)])
                ext[key] = part if key not in ext else (ext[key][0] + part[0], ext[key][1] + part[1])
        for (j, blk), (dk, dv) in ext.items():
            sl = slice(LANE * j, LANE * (j + 1))
            if blk == 0:
                dkp_ref[:, sl] = dk
                dvp_ref[:, sl] = dv
            else:
                dkc_ref[BLOCK * (blk - 1):BLOCK * blk, sl] = dk
                dvc_ref[BLOCK * (blk - 1):BLOCK * blk, sl] = dv
        for hd in range(SWA_HEADS):
            _acc(dsink_ref.at[hd:hd + 1, :], jnp.broadcast_to(dsink[hd], (1, LANE)), n == 0)

    cur = lambda n: (n, 0)
    kv = pl.BlockSpec((ts, 2 * LANE), cur)
    kvp = pl.BlockSpec((BLOCK, 2 * LANE), cur)
    hp = pl.BlockSpec((ts, HP), cur)
    kvs = jax.ShapeDtypeStruct((t, 2 * LANE), F32)
    kvps = jax.ShapeDtypeStruct((t // ts * BLOCK, 2 * LANE), F32)
    return pl.pallas_call(
        body, name="swa_bwd", grid=(t // ts,),
        in_specs=_swa_specs(t) + [hp, hp],
        out_specs=[hp, kv, kvp, kv, kvp, _const((SWA_HEADS, LANE))],
        out_shape=[jax.ShapeDtypeStruct((t, HP), F32), kvs, kvps, kvs, kvps,
                   jax.ShapeDtypeStruct((SWA_HEADS, LANE), F32)],
        compiler_params=_params("arbitrary"),
    )(sinks, q, k, k, v, v, o, do)


def _mla_bwd_dq(q, k, v, o, do, lse):
    t = q.shape[0]
    tq = _tile(t)
    nq = t // tq

    def body(q_ref, k_ref, v_ref, o_ref, do_ref, lse_ref, dq_ref, dl_ref, dq_sc):
        i = pl.program_id(1)
        qs, dos = _heads(q_ref), _heads(do_ref)
        deltas = [jnp.sum(oh * doh.astype(F32), axis=1, keepdims=True) for oh, doh in zip(_heads(o_ref), dos)]
        lses = [lh[:, :1] for lh in _heads(lse_ref)]

        dq_sc[...] = jnp.zeros(dq_sc.shape, F32)

        def step(j, carry, masked):
            rows = pl.ds(pl.multiple_of(j * tq, tq), tq)
            ks, vs = _heads(k_ref, rows), _heads(v_ref, rows)
            ss = [_dot_nt(qh, kh) for qh, kh in zip(qs, ks)]
            dps = [_dot_nt(doh, vh) for doh, vh in zip(dos, vs)]
            if masked:
                mask = _causal_mask(i * tq, j * tq, tq, tq, False)
                ss = [jnp.where(mask, s, NEG) for s in ss]
            dss = [(jnp.exp2(s - lh) * (dp - dl)).astype(BF16) for s, dp, lh, dl in zip(ss, dps, lses, deltas)]
            for a, (ds, kh) in enumerate(zip(dss, ks)):
                dq_sc[a] += _dot(ds, kh)
            return carry

        lax.fori_loop(0, jnp.minimum(i, 1) + 1, lambda it, c: step(it * i, c, True), 0)
        lax.fori_loop(1, i, lambda j, c: step(j, c, False), 0)
        for a, dl in enumerate(deltas):
            dq_ref[:, LANE * a:LANE * (a + 1)] = dq_sc[a] * SCALE_B
            dl_ref[a] = _as_row(dl)

    blk = pl.BlockSpec((tq, MLA_HB * LANE), lambda h, i: (i, h))
    full = pl.BlockSpec((t, MLA_HB * LANE), lambda h, i: (0, h))
    rows_shape, rows_spec = _row_stats(t)
    return pl.pallas_call(
        body, name="mla_bwd_dq", grid=(MLA_HEADS // MLA_HB, nq),
        in_specs=[blk, full, full, blk, blk, blk], out_specs=[blk, rows_spec],
        out_shape=[jax.ShapeDtypeStruct((t, HP), F32), rows_shape],
        scratch_shapes=[pltpu.VMEM((MLA_HB, tq, LANE), F32)],
        compiler_params=_params("parallel", "parallel"),
    )(q, k, v, o, do, lse)


def _mla_bwd_dkv(q, k, v, do, lse_t, dl_t, slabs=()):
    t = q.shape[0]
    tq = _tile(t)
    nq = t // tq
    n = len(slabs)
    steps = (MLA_HEADS // MLA_HB) * nq

    def body(k_ref, v_ref, q_ref, do_ref, lse_ref, dl_ref, *rest):
        in_refs, (dk_ref, dv_ref), out_refs = rest[:n], rest[n:n + 2], rest[n + 2:2 * n + 2]
        (dk_sc, dv_sc), sems = rest[2 * n + 2:2 * n + 4], rest[2 * n + 4:]
        hb = pl.program_id(0)
        j = pl.program_id(1)
        step_id = hb * nq + j
        if n:
            plan = _exchange_plan(in_refs, out_refs, *sems)
            pl.when(step_id == 0)(plan.start)
        ks, vs = _heads(k_ref), _heads(v_ref)

        dk_sc[...] = jnp.zeros(dk_sc.shape, F32)
        dv_sc[...] = jnp.zeros(dv_sc.shape, F32)

        def step(i, carry, masked):
            rows = pl.ds(pl.multiple_of(i * tq, tq), tq)
            qs, dos = _heads(q_ref, rows), _heads(do_ref, rows)
            sts = [_dot_nt(kh, qh) for kh, qh in zip(ks, qs)]
            dpts = [_dot_nt(vh, doh) for vh, doh in zip(vs, dos)]
            if masked:
                mask = _causal_mask(i * tq, j * tq, tq, tq, True)
                sts = [jnp.where(mask, st, NEG) for st in sts]
            pts = [jnp.exp2(st - lse_ref[hb, a, i]) for a, st in enumerate(sts)]
            dsts = [(pt * (dpt - dl_ref[hb, a, i])).astype(BF16) for a, (pt, dpt) in enumerate(zip(pts, dpts))]
            for a, (dst, pt, qh, doh) in enumerate(zip(dsts, pts, qs, dos)):
                dk_sc[a] += _dot(dst, qh)
                dv_sc[a] += _dot(pt.astype(BF16), doh)
            return carry

        split = jnp.where(j == 0, nq, j + 1)
        lax.fori_loop(j, split, lambda i, c: step(i, c, True), 0)
        lax.fori_loop(split, nq, lambda i, c: step(i, c, False), 0)
        for a in range(MLA_HB):
            dk_ref[:, LANE * a:LANE * (a + 1)] = dk_sc[a] * (1.0 / LOG2E)
            dv_ref[:, LANE * a:LANE * (a + 1)] = dv_sc[a]
        if n:
            pl.when(step_id == steps - 1)(plan.finish)

    blk = pl.BlockSpec((tq, MLA_HB * LANE), lambda h, j: (j, h))
    full = pl.BlockSpec((t, MLA_HB * LANE), lambda h, j: (0, h))
    rows = pl.BlockSpec((MLA_HEADS // MLA_HB, MLA_HB, nq, 1, tq), lambda h, j: (0, 0, 0, 0, 0))
    out = pl.pallas_call(
        body, name="mla_bwd_dkv_exchange" if n else "mla_bwd_dkv", grid=(MLA_HEADS // MLA_HB, nq),
        in_specs=[blk, blk, full, full, rows, rows] + [ANY] * n, out_specs=[blk, blk] + [ANY] * n,
        out_shape=[jax.ShapeDtypeStruct((t, HP), F32)] * 2 + [jax.ShapeDtypeStruct(a.shape, a.dtype) for a in slabs],
        scratch_shapes=[pltpu.VMEM((MLA_HB, tq, LANE), F32)] * 2 + (_comm_sems(n) if n else []),
        compiler_params=_params("arbitrary", "arbitrary"),
    )(k, v, q, do, lse_t, dl_t, *slabs)
    return out[0], out[1], out[2:]


def _pre_bwd(dh2, h, cq, ckv, dqa, dka, dka_next, dva, dva_next, dqb, dkf, dvb, g1, win, gq, wqu, gkv, wkv, tabs):
    t = h.shape[0]
    tm = _tile(t)

    def body(dh2_ref, h_ref, cq_ref, ckv_ref, dqa_ref, dka_ref, dkan_ref, dva_ref, dvan_ref, dqb_ref, dkf_ref, dvb_ref,
             g1_ref, win_ref, gq_ref, wqu_ref, gkv_ref, wkv_ref, tab_ref,
             dh_ref, dp_ref, dqbo_ref, dkvo_ref, dg1_ref, dgq_ref, dgkv_ref):
        first = pl.program_id(0) == 0
        ca, sa1, sa2, cb, sb1, sb2, ck = _tabs(tab_ref)
        dkr = jnp.zeros((tm, LANE), F32)
        for c in range(MLA_HEADS):
            sl = slice(LANE * c, LANE * (c + 1))
            dqbo_ref[:, sl] = _rope_t(dqb_ref[:, sl], cb, sb1, sb2, 16).astype(BF16)
            dkr += dkf_ref[:, sl]
        dkvo_ref[:, :HP] = dkf_ref[...].astype(BF16)
        dkvo_ref[:, HP:] = dvb_ref[...].astype(BF16)
        dcq, dgq = _rms_bwd(cq_ref[...], gq_ref[...], _dot_nt(dqbo_ref[...], wqu_ref[...]), MLA_Q_RANK)
        dckv, dgkv = _rms_bwd(ckv_ref[...], gkv_ref[...], _dot_nt(dkvo_ref[...], wkv_ref[...]), MLA_KV_RANK)
        for c in range(SWA_HEADS):
            sl = slice(LANE * c, LANE * (c + 1))
            dp_ref[:, PO_QA + LANE * c:PO_QA + LANE * (c + 1)] = _rope_t(dqa_ref[:, sl], ca, sa1, sa2, 32).astype(BF16)
        last = slice(tm - BLOCK, tm)
        more = pl.program_id(0) < t // tm - 1
        for c in range(SWA_KV_HEADS):
            sl = slice(LANE * c, LANE * (c + 1))
            dk = dka_ref[:, sl]
            dk_last = dk[tm - BLOCK:] + jnp.where(more, dkan_ref[:, sl], 0.0)
            cols = slice(PO_KA + LANE * c, PO_KA + LANE * (c + 1))
            if tm > BLOCK:
                dp_ref[:tm - BLOCK, cols] = _rope_t(dk[:tm - BLOCK], ca[:tm - BLOCK], sa1[:tm - BLOCK], sa2[:tm - BLOCK],
                                                    32).astype(BF16)
            dp_ref[last, cols] = _rope_t(dk_last, ca[tm - BLOCK:], sa1[tm - BLOCK:], sa2[tm - BLOCK:], 32).astype(BF16)
        if tm > BLOCK:
            dp_ref[:tm - BLOCK, PO_VA:PO_CQ] = dva_ref[:tm - BLOCK, :].astype(BF16)
        dp_ref[last, PO_VA:PO_CQ] = (dva_ref[tm - BLOCK:, :] + jnp.where(more, dvan_ref[...], 0.0)).astype(BF16)
        dp_ref[:, PO_CQ:PO_CKV] = dcq.astype(BF16)
        dp_ref[:, PO_CKV:PO_KR] = dckv.astype(BF16)
        dp_ref[:, PO_KR:PW_IN] = _rope_t(dkr, ck, sb1, sb2, 16).astype(BF16)
        dx, dg1 = _rms_bwd(h_ref[...], g1_ref[...], _dot_nt(dp_ref[...], win_ref[...]), D_MODEL)
        dh_ref[...] = dh2_ref[...] + dx
        _acc(dg1_ref, dg1, first)
        _acc(dgq_ref, dgq, first)
        _acc(dgkv_ref, dgkv, first)

    kv = _row(tm, 2 * LANE)
    nxt = pl.BlockSpec((BLOCK, 2 * LANE), lambda i: (jnp.minimum(i + 1, t // tm - 1), 0))
    return pl.pallas_call(
        body, name="pre_bwd", grid=(t // tm,),
        in_specs=[_row(tm, D_MODEL), _row(tm, D_MODEL), _row(tm, MLA_Q_RANK), _row(tm, MLA_KV_RANK), _row(tm, HP),
                  kv, nxt, kv, nxt, _row(tm, HP), _row(tm, HP), _row(tm, HP),
                  _const(g1.shape), _const(win.shape), _const(gq.shape), _const(wqu.shape), _const(gkv.shape),
                  _const(wkv.shape), _row(tm, N_TAB * LANE)],
        out_specs=[_row(tm, D_MODEL), _row(tm, PW_IN), _row(tm, HP), _row(tm, 2 * HP),
                   _const((1, D_MODEL)), _const((1, MLA_Q_RANK)), _const((1, MLA_KV_RANK))],
        out_shape=[jax.ShapeDtypeStruct((t, D_MODEL), F32), jax.ShapeDtypeStruct((t, PW_IN), BF16),
                   jax.ShapeDtypeStruct((t, HP), BF16), jax.ShapeDtypeStruct((t, 2 * HP), BF16),
                   jax.ShapeDtypeStruct((1, D_MODEL), F32), jax.ShapeDtypeStruct((1, MLA_Q_RANK), F32),
                   jax.ShapeDtypeStruct((1, MLA_KV_RANK), F32)],
        compiler_params=_params("arbitrary"),
    )(dh2, h, cq, ckv, dqa, dka, dka_next, dva, dva_next, dqb, dkf, dvb, g1, win, gq, wqu, gkv, wkv, tabs)


def _rope_tables(t):
    pos = (jnp.arange(t, dtype=jnp.int32) - FRONT).astype(F32)[:, None]
    lane = jnp.arange(LANE)[None, :]

    def table(dim, start):
        half = dim // 2
        inv = ROPE_THETA ** (-jnp.arange(0, dim, 2, dtype=F32) / dim)
        ang = pos * inv[None, :]
        cos = jnp.concatenate([jnp.cos(ang)] * 2, axis=1)
        sin = jnp.concatenate([jnp.sin(ang)] * 2, axis=1)
        pad = lambda a: jnp.pad(a, ((0, 0), (start, LANE - start - dim)))
        first = (lane >= start) & (lane < start + half)
        second = (lane >= start + half) & (lane < start + dim)
        return pad(cos), jnp.where(first, -pad(sin), 0.0), jnp.where(second, pad(sin), 0.0)

    ca, sa1, sa2 = table(SWA_HEAD_DIM, 0)
    ck, sb1, sb2 = table(MLA_ROPE_DIM, MLA_NOPE_DIM)
    cb = jnp.where(lane < MLA_NOPE_DIM, 1.0, ck)
    return jnp.concatenate([ca, sa1, sa2, cb, sb1, sb2, ck], axis=1)


def _pad_heads(w, heads, dim, axis):
    shp = w.shape
    w = w.reshape(shp[:axis] + (heads, dim) + shp[axis + 1:])
    pad = [(0, 0)] * w.ndim
    pad[axis + 1] = (0, LANE - dim)
    return jnp.pad(w, pad).reshape(shp[:axis] + (heads * LANE,) + shp[axis + 1:])


def _unpad_heads(w, heads, dim, axis):
    shp = w.shape
    w = w.reshape(shp[:axis] + (heads, LANE) + shp[axis + 1:])
    w = lax.slice_in_dim(w, 0, dim, axis=axis + 1)
    return w.reshape(shp[:axis] + (heads * dim,) + shp[axis + 1:])


def _pad_layer(w_in, w_q_up, w_kv_up, w_o, out_norm_swa, out_norm_mla):
    o1 = SWA_Q_W
    o2 = o1 + SWA_KV_W
    o3 = o2 + SWA_KV_W
    o4 = o3 + MLA_Q_RANK
    o5 = o4 + MLA_KV_RANK
    kr = jnp.pad(w_in[:, o5:], ((0, 0), (MLA_NOPE_DIM, LANE - MLA_QK_DIM)))
    win = jnp.concatenate([
        _pad_heads(w_in[:, :o1], SWA_HEADS, SWA_HEAD_DIM, 1),
        _pad_heads(w_in[:, o1:o2], SWA_KV_HEADS, SWA_HEAD_DIM, 1),
        _pad_heads(w_in[:, o2:o3], SWA_KV_HEADS, SWA_HEAD_DIM, 1),
        w_in[:, o3:o5], kr], axis=1)
    wqu = _pad_heads(w_q_up, MLA_HEADS, MLA_QK_DIM, 1)
    kv = w_kv_up.reshape(MLA_KV_RANK, MLA_HEADS, MLA_NOPE_DIM + MLA_V_DIM)
    wkv = jnp.concatenate([
        _pad_heads(kv[:, :, :MLA_NOPE_DIM].reshape(MLA_KV_RANK, -1), MLA_HEADS, MLA_NOPE_DIM, 1),
        _pad_heads(kv[:, :, MLA_NOPE_DIM:].reshape(MLA_KV_RANK, -1), MLA_HEADS, MLA_V_DIM, 1)], axis=1)
    wo = jnp.concatenate([_pad_heads(w_o[:SWA_Q_W], SWA_HEADS, SWA_HEAD_DIM, 0),
                          _pad_heads(w_o[SWA_Q_W:], MLA_HEADS, MLA_V_DIM, 0)], axis=0)
    ga = _pad_heads(out_norm_swa[None, :], SWA_HEADS, SWA_HEAD_DIM, 1)
    gb = _pad_heads(out_norm_mla[None, :], MLA_HEADS, MLA_V_DIM, 1)
    return win, wqu, wkv, wo, ga, gb


def _unpad_layer(dwin, dwqu, dwkv, dwo, dga, dgb):
    d_w_in = jnp.concatenate([
        _unpad_heads(dwin[:, PO_QA:PO_KA], SWA_HEADS, SWA_HEAD_DIM, 1),
        _unpad_heads(dwin[:, PO_KA:PO_VA], SWA_KV_HEADS, SWA_HEAD_DIM, 1),
        _unpad_heads(dwin[:, PO_VA:PO_CQ], SWA_KV_HEADS, SWA_HEAD_DIM, 1),
        dwin[:, PO_CQ:PO_KR], dwin[:, PO_KR + MLA_NOPE_DIM:PO_KR + MLA_QK_DIM]], axis=1)
    d_w_q_up = _unpad_heads(dwqu, MLA_HEADS, MLA_QK_DIM, 1)
    dk = _unpad_heads(dwkv[:, :HP], MLA_HEADS, MLA_NOPE_DIM, 1).reshape(MLA_KV_RANK, MLA_HEADS, MLA_NOPE_DIM)
    dv = _unpad_heads(dwkv[:, HP:], MLA_HEADS, MLA_V_DIM, 1).reshape(MLA_KV_RANK, MLA_HEADS, MLA_V_DIM)
    d_w_kv_up = jnp.concatenate([dk, dv], axis=2).reshape(MLA_KV_RANK, -1)
    d_w_o = jnp.concatenate([_unpad_heads(dwo[:HP], SWA_HEADS, SWA_HEAD_DIM, 0),
                             _unpad_heads(dwo[HP:], MLA_HEADS, MLA_V_DIM, 0)], axis=0)
    d_ga = _unpad_heads(dga, SWA_HEADS, SWA_HEAD_DIM, 1)[0]
    d_gb = _unpad_heads(dgb, MLA_HEADS, MLA_V_DIM, 1)[0]
    return d_w_in, d_w_q_up, d_w_kv_up, d_w_o, d_ga, d_gb


def _train_example(x, target, meta, vec, weights):
    s = x.shape[0]
    depth = vec["attn_norm"].shape[0]
    t = FRONT + N_META + s
    assert t % BLOCK == 0
    tabs = _rope_tables(t)
    h = jnp.concatenate([jnp.zeros((FRONT, D_MODEL), F32), meta, x], axis=0)
    tgt = jnp.concatenate([jnp.zeros((FRONT + N_META, D_MODEL), F32), target], axis=0)
    row = lambda v: v[None, :]

    saved = []
    for l in range(depth):
        win, wqu, wkv, wo, ga, gb = _pad_layer(*weights.attn(l), vec["out_norm_swa"][l], vec["out_norm_mla"][l])
        g1, gq, gkv, g2 = (row(vec[n][l]) for n in ("attn_norm", "q_norm", "kv_norm", "ffn_norm"))
        sk = row(vec["sinks"][l])
        u, qa, ka, va, cq, ckv, qn, kvn, qb, kf, vb = _pre_fwd(h, g1, win, gq, wqu, gkv, wkv, tabs)
        oa = _swa_fwd(sk, qa, ka, va)
        ob, lse = weights.mla_fwd(l, qb, kf, vb)
        h2, mix, u2 = _mix_fwd(h, oa, ob, ga, gb, wo, g2)
        wg, wu, wd = weights.ffn(l)
        h3, gt, up = _ffn_fwd(h2, u2, wg, wu, wd)
        saved.append((h, u, qa, ka, va, cq, ckv, qn, kvn, qb, kf, vb, oa, ob, lse, h2, mix, u2, gt, up,
                      win, wqu, wkv, wo, ga, gb, g1, gq, gkv, g2, sk, wg, wu, wd))
        h = h3

    dh, d_final, loss = _loss_bwd(h, row(vec["final_norm"]), tgt)

    grads = []
    for l in reversed(range(depth)):
        (h0, u, qa, ka, va, cq, ckv, qn, kvn, qb, kf, vb, oa, ob, lse, h2, mix, u2, gt, up,
         win, wqu, wkv, wo, ga, gb, g1, gq, gkv, g2, sk, wg, wu, wd) = saved[l]
        dff = wg.shape[1]
        act, dgu, dhb = _ffn_bwd_a(dh, gt, up, wd)
        weights.ffn_grads(l, _tn_matmul(u2, dgu, "dw_gate", (0, dff)), _tn_matmul(u2, dgu, "dw_up", (dff, dff)),
                          _tn_matmul(act, dhb, "dw_down"))
        dh2, dh2b, d_g2 = _ffn_bwd_b(dh, dgu, h2, g2, wg, wu)
        d_wo = _tn_matmul(mix, dh2b, "dw_o")
        doa, dob, d_ga, d_gb = _mix_bwd(dh2b, oa, ob, ga, gb, wo)
        dqa, dkc, dkp, dvc, dvp, dsink = _swa_bwd(sk, qa, ka, va, oa, doa)
        dqb, dl = _mla_bwd_dq(qb, kf, vb, ob, dob, lse[0])
        dkf, dvb = weights.mla_bwd_dkv(l, qb, kf, vb, dob, lse[1], dl)
        dh, dp, dqbo, dkvo, d_g1, d_gq, d_gkv = _pre_bwd(
            dh2, h0, cq, ckv, dqa, dkc, dkp, dvc, dvp, dqb, dkf, dvb,
            g1, win, gq, wqu, gkv, wkv, tabs)
        d_win = _tn_matmul(u, dp, "dw_in")
        d_wqu = _tn_matmul(qn, dqbo, "dw_q_up")
        d_wkv = _tn_matmul(kvn, dkvo, "dw_kv_up")
        d_w_in, d_w_q_up, d_w_kv_up, d_w_o, d_sw, d_ml = _unpad_layer(d_win, d_wqu, d_wkv, d_wo, d_ga, d_gb)
        weights.attn_grads(l, d_w_in, d_w_q_up, d_w_kv_up, d_w_o)
        grads.append(dict(attn_norm=d_g1[0], q_norm=d_gq[0], kv_norm=d_gkv[0], sinks=dsink[:, 0], out_norm_swa=d_sw,
                          out_norm_mla=d_ml, ffn_norm=d_g2[0]))
    grads = grads[::-1]
    stacked = {k: jnp.stack([g[k] for g in grads]) for k in grads[0]}
    stacked["final_norm"] = d_final[0]
    return loss[0, 0], dh[FRONT + N_META:], dh[FRONT:FRONT + N_META], stacked


MESH = pl.DeviceIdType.MESH
ANY = pl.BlockSpec(memory_space=pl.ANY)


def _place():
    return lax.axis_index("x"), lax.axis_index("y"), lax.axis_index("c")


def _index(x, y, c):
    return 4 * x + 2 * y + c


def _comm_sems(n):
    return [pltpu.SemaphoreType.DMA((n, N_DEV - 1)), pltpu.SemaphoreType.DMA((n, N_DEV - 1)),
            pltpu.SemaphoreType.DMA((n,))]


class _gather_plan:
    def __init__(self, x_refs, out_refs, send_sems, recv_sems, local_sems):
        self.x_refs, self.out_refs = x_refs, out_refs
        self.send_sems, self.recv_sems, self.local_sems = send_sems, recv_sems, local_sems
        self.n = len(x_refs)

    def _where(self):
        x, y, c = _place()
        return (x, y, c), (x, y, 1 - c), [(1 - x, y), (x, 1 - y), (1 - x, 1 - y)], c

    def _copy(self, i, k, block, to, from_input=False):
        slot = self.out_refs[i].at[_index(*block)]
        return pltpu.make_async_remote_copy(
            src_ref=self.x_refs[i] if from_input else slot, dst_ref=slot,
            send_sem=self.send_sems.at[i, k], recv_sem=self.recv_sems.at[i, k], device_id=to, device_id_type=MESH)

    def _mine(self, i, me):
        return pltpu.make_async_copy(self.x_refs[i], self.out_refs[i].at[_index(*me)], self.local_sems.at[i])

    def _first(self, me, sibling, chips, c):
        out = [self._copy(i, 1 + j, me, (*chip, c), True) for j, chip in enumerate(chips) for i in range(self.n)]
        return out + [self._copy(i, 0, me, sibling, True) for i in range(self.n)]

    def start(self):
        me, sibling, chips, c = self._where()
        for i in range(self.n):
            self._mine(i, me).start()
        for cp in self._first(me, sibling, chips, c):
            cp.start()

    def forward(self):
        me, sibling, chips, c = self._where()
        for j, chip in enumerate(chips):
            for i in range(self.n):
                self._copy(i, 1 + j, (*chip, c), me).wait_recv()
                self._copy(i, 4 + j, (*chip, c), sibling).start()

    def finish(self):
        me, sibling, chips, c = self._where()
        for i in range(self.n):
            self._copy(i, 0, sibling, me).wait_recv()
            for j, chip in enumerate(chips):
                self._copy(i, 4 + j, (*chip, 1 - c), me).wait_recv()
        for cp in self._first(me, sibling, chips, c):
            cp.wait_send()
        for j, chip in enumerate(chips):
            for i in range(self.n):
                self._copy(i, 4 + j, (*chip, c), sibling).wait_send()
        for i in range(self.n):
            self._mine(i, me).wait()


class _exchange_plan:
    def __init__(self, in_refs, out_refs, send_sems, recv_sems, local_sems):
        self.in_refs, self.out_refs = in_refs, out_refs
        self.send_sems, self.recv_sems, self.local_sems = send_sems, recv_sems, local_sems
        self.n = len(in_refs)

    def _copies(self):
        x, y, c = _place()
        me = _index(x, y, c)
        mine = [pltpu.make_async_copy(self.in_refs[i].at[me], self.out_refs[i].at[me], self.local_sems.at[i])
                for i in range(self.n)]
        remote = []
        for k in range(1, N_DEV):
            peer = (1 - x if k & 4 else x, 1 - y if k & 2 else y, 1 - c if k & 1 else c)
            remote += [pltpu.make_async_remote_copy(
                src_ref=self.in_refs[i].at[_index(*peer)], dst_ref=self.out_refs[i].at[me],
                send_sem=self.send_sems.at[i, k - 1], recv_sem=self.recv_sems.at[i, k - 1],
                device_id=peer, device_id_type=MESH) for i in range(self.n)]
        return mine, remote

    def start(self):
        mine, remote = self._copies()
        for cp in mine + remote:
            cp.start()

    def finish(self):
        mine, remote = self._copies()
        for cp in remote:
            cp.wait_recv()
        for cp in remote:
            cp.wait_send()
        for cp in mine:
            cp.wait()


def _all_gather(shards, name):
    n = len(shards)

    def body(*refs):
        plan = _gather_plan(refs[:n], refs[n:2 * n], *refs[2 * n:])
        plan.start()
        plan.forward()
        plan.finish()

    return pl.pallas_call(
        body, name=name, in_specs=[ANY] * n, out_specs=[ANY] * n, scratch_shapes=_comm_sems(n),
        out_shape=[jax.ShapeDtypeStruct((N_DEV,) + a.shape, a.dtype) for a in shards],
    )(*shards)


def _exchange(slabs, name):
    n = len(slabs)

    def body(*refs):
        plan = _exchange_plan(refs[:n], refs[n:2 * n], *refs[2 * n:])
        plan.start()
        plan.finish()

    return pl.pallas_call(
        body, name=name, in_specs=[ANY] * n, out_specs=[ANY] * n, scratch_shapes=_comm_sems(n),
        out_shape=[jax.ShapeDtypeStruct(a.shape, a.dtype) for a in slabs],
    )(*slabs)


def _adamw(w, g, m, v):
    m = ADAM_B1 * m + (1.0 - ADAM_B1) * g
    v = ADAM_B2 * v + (1.0 - ADAM_B2) * (g * g)
    m_hat = m / (1.0 - ADAM_B1 ** ADAM_STEP)
    v_hat = v / (1.0 - ADAM_B2 ** ADAM_STEP)
    return -ADAM_LR * (m_hat / (jnp.sqrt(v_hat) + ADAM_EPS) + ADAM_WD * w), m, v


def _sum_slots(ref):
    g = ref[0].astype(F32)
    for s in range(1, N_DEV):
        g = g + ref[s].astype(F32)
    return g


def _reduce_adamw(parts, w, m, v, layer, outs, name):
    l, r, c = w.shape
    tile = next(t for t in (256, 128, r) if r % t == 0)

    def body(p_ref, w_ref, m_ref, v_ref, g0, d0, m0, v0, g_ref, d_ref, nm_ref, nv_ref):
        g = _sum_slots(p_ref)
        g_ref[...] = g
        d_ref[...], nm_ref[...], nv_ref[...] = _adamw(w_ref[...], g, m_ref[...], v_ref[...])

    blk = pl.BlockSpec((None, tile, c), lambda j: (layer, j, 0))
    return pl.pallas_call(
        body, name=name, grid=(r // tile,),
        in_specs=[pl.BlockSpec((N_DEV, tile, c), lambda j: (0, j, 0)), blk, blk, blk] + [ANY] * 4, out_specs=[blk] * 4,
        out_shape=[jax.ShapeDtypeStruct((l, r, c), F32)] * 4,
        input_output_aliases={4: 0, 5: 1, 6: 2, 7: 3},
        compiler_params=_params("parallel"),
    )(parts, w, m, v, *outs)


def _sum_parts(parts, name):
    _, r, c = parts.shape

    def body(p_ref, g_ref):
        g_ref[...] = _sum_slots(p_ref)

    return pl.pallas_call(body, name=name, out_shape=jax.ShapeDtypeStruct((r, c), F32))(parts)


def _adamw_call(w, g, m, v, name):
    def body(w_ref, g_ref, m_ref, v_ref, d_ref, nm_ref, nv_ref):
        d_ref[...], nm_ref[...], nv_ref[...] = _adamw(w_ref[...], g_ref[...], m_ref[...], v_ref[...])

    return pl.pallas_call(body, name=name, out_shape=[jax.ShapeDtypeStruct(w.shape, F32)] * 3)(w, g, m, v)


ATTN = ("w_in", "w_q_up", "w_kv_up", "w_o")
FFN = ("w_gate", "w_up", "w_down")
SHARD_AXIS = dict(w_in=1, w_q_up=1, w_kv_up=1, w_o=0, w_gate=1, w_up=1, w_down=0)
SMALL = ("attn_norm", "ffn_norm", "final_norm", "out_norm_swa", "out_norm_mla", "q_norm", "kv_norm", "sinks")
PACK_W = 1024
SMALL_ROWS = 16


def _pack(arrs, dtype):
    flat = jnp.concatenate([a.astype(dtype).reshape(-1) for a in arrs])
    return flat.reshape(-1, PACK_W)


def _unpack(packed, like):
    flat = packed.reshape(-1)
    out, off = [], 0
    for a in like:
        out.append(flat[off:off + a.size].reshape(a.shape))
        off += a.size
    return out


def _gather_to_full(gathered, axis):
    shp = list(gathered.shape[1:])
    shp[axis] *= N_DEV
    return jnp.moveaxis(gathered, 0, axis).reshape(shp)


def _full_to_slabs(full, axis):
    shp = list(full.shape)
    shp[axis:axis + 1] = [N_DEV, shp[axis] // N_DEV]
    return jnp.moveaxis(full.reshape(shp), axis, 0)


class _ShardedWeights:
    def __init__(self, shards, depth):
        self.shards, self.depth = shards, depth
        self.gathered, self.pending, self.parts = {}, {}, {}
        self._gather([(n, 0) for n in ATTN], lambda xs: _all_gather(xs, "gather_attn0"))

    def _gather(self, keys, run):
        self.gathered.update(zip(keys, run([self.shards[n][l] for n, l in keys])))

    def _full(self, names, l):
        return tuple(_gather_to_full(self.gathered[n, l], SHARD_AXIS[n]) for n in names)

    def attn(self, l):
        return self._full(ATTN, l)

    def ffn(self, l):
        return self._full(FFN, l)

    def mla_fwd(self, l, q, k, v):
        keys = [(n, l) for n in FFN] + ([(n, l + 1) for n in ATTN] if l + 1 < self.depth else [])
        out = []
        self._gather(keys, lambda xs: out.extend(_mla_fwd(q, k, v, xs)) or out[2])
        return out[0], out[1]

    def _add(self, names, l, grads):
        for n, g in zip(names, grads):
            self.pending[n, l] = _full_to_slabs(g, SHARD_AXIS[n])

    def ffn_grads(self, l, *grads):
        self._add(FFN, l, grads)

    def attn_grads(self, l, *grads):
        self._add(ATTN, l, grads)

    def _exchange(self, run):
        keys = list(self.pending)
        self.parts.update(zip(keys, run([self.pending.pop(k) for k in keys])))

    def mla_bwd_dkv(self, l, *args):
        out = []
        self._exchange(lambda xs: out.extend(_mla_bwd_dkv(*args, xs)) or out[2])
        return out[0], out[1]

    def flush(self):
        self._exchange(lambda xs: _exchange(xs, "exchange_attn0"))


def kernel(x, meta_tokens, attn_norm, w_in, q_norm, w_q_up, kv_norm, w_kv_up, sinks, out_norm_swa, out_norm_mla, w_o, ffn_norm, w_gate, w_up, w_down, final_norm, loss_target, m_meta_tokens, m_attn_norm, m_w_in, m_q_norm, m_w_q_up, m_kv_norm, m_w_kv_up, m_sinks, m_out_norm_swa, m_out_norm_mla, m_w_o, m_ffn_norm, m_w_gate, m_w_up, m_w_down, m_final_norm, v_meta_tokens, v_attn_norm, v_w_in, v_q_norm, v_w_q_up, v_kv_norm, v_w_kv_up, v_sinks, v_out_norm_swa, v_out_norm_mla, v_w_o, v_ffn_norm, v_w_gate, v_w_up, v_w_down, v_final_norm):
    w = dict(meta_tokens=meta_tokens, attn_norm=attn_norm, w_in=w_in, q_norm=q_norm, w_q_up=w_q_up, kv_norm=kv_norm,
             w_kv_up=w_kv_up, sinks=sinks, out_norm_swa=out_norm_swa, out_norm_mla=out_norm_mla, w_o=w_o,
             ffn_norm=ffn_norm, w_gate=w_gate, w_up=w_up, w_down=w_down, final_norm=final_norm)
    m = dict(meta_tokens=m_meta_tokens, attn_norm=m_attn_norm, w_in=m_w_in, q_norm=m_q_norm, w_q_up=m_w_q_up,
             kv_norm=m_kv_norm, w_kv_up=m_w_kv_up, sinks=m_sinks, out_norm_swa=m_out_norm_swa,
             out_norm_mla=m_out_norm_mla, w_o=m_w_o, ffn_norm=m_ffn_norm, w_gate=m_w_gate, w_up=m_w_up,
             w_down=m_w_down, final_norm=m_final_norm)
    v = dict(meta_tokens=v_meta_tokens, attn_norm=v_attn_norm, w_in=v_w_in, q_norm=v_q_norm, w_q_up=v_w_q_up,
             kv_norm=v_kv_norm, w_kv_up=v_w_kv_up, sinks=v_sinks, out_norm_swa=v_out_norm_swa,
             out_norm_mla=v_out_norm_mla, w_o=v_w_o, ffn_norm=v_ffn_norm, w_gate=v_w_gate, w_up=v_w_up,
             w_down=v_w_down, final_norm=v_final_norm)
    names = list(w)
    big = ATTN + FFN
    depth = w_in.shape[0]
    me = _index(*_place())

    weights = _ShardedWeights({n: w[n].astype(BF16) for n in big}, depth)
    meta = jnp.moveaxis(_all_gather([meta_tokens], "gather_meta")[0], 0, 1).reshape(N_META, D_MODEL)
    loss, grad_x, d_meta, grads = _train_example(x[0], loss_target[0], meta, {n: w[n] for n in SMALL}, weights)
    weights.flush()

    g_big, d_big, m_big, v_big = {}, {}, {}, {}
    for n in big:
        outs = [lax.empty(w[n].shape, F32) for _ in range(4)]
        for l in reversed(range(depth)):
            outs = _reduce_adamw(weights.parts[n, l], w[n], m[n], v[n], l, outs, "reduce_adamw_" + n)
        g_big[n], d_big[n], m_big[n], v_big[n] = outs

    small = [grads[n] for n in SMALL] + [loss.reshape(1)]
    pad = SMALL_ROWS * PACK_W - sum(a.size for a in small)
    part = jnp.concatenate([_pack(small + [jnp.zeros((pad,), F32)], F32), d_meta], axis=0)
    total = _sum_parts(_all_gather([part], "gather_small")[0], "sum_small")
    small_w = [w[n] for n in SMALL]
    packs = [_pack([d[n] for n in SMALL] + [jnp.zeros((pad + 1,), F32)], F32) for d in (w, m, v)]
    upd = _adamw_call(packs[0], total[:SMALL_ROWS], packs[1], packs[2], "adamw_small")
    g_small, d_small, m_small, v_small = [dict(zip(SMALL, _unpack(p, small_w))) for p in (total[:SMALL_ROWS],) + tuple(upd)]
    loss_total = total[:SMALL_ROWS].reshape(-1)[SMALL_ROWS * PACK_W - pad - 1]
    g_meta = lax.dynamic_slice_in_dim(total[SMALL_ROWS:], me * LANE, LANE, axis=1)
    d_mt, m_mt, v_mt = _adamw_call(meta_tokens, g_meta, m_meta_tokens, v_meta_tokens, "adamw_meta")

    outs = []
    for got in ({**g_big, **g_small, "meta_tokens": g_meta}, {**d_big, **d_small, "meta_tokens": d_mt},
                {**m_big, **m_small, "meta_tokens": m_mt}, {**v_big, **v_small, "meta_tokens": v_mt}):
        outs += [got[n] for n in names]
    return (loss_total, grad_x[None], *outs)
```

```python
import jax
import jax.numpy as jnp
from jax import lax
from jax.experimental import pallas as pl
from jax.experimental.pallas import tpu as pltpu

F32 = jnp.float32
BF16 = jnp.bfloat16

D_MODEL = 1024
N_META = 16
BLOCK = 128
FRONT = (-N_META) % BLOCK
ROPE_THETA = 10000.0
EPS = 1e-6
NEG = -1e30
SWA_HEADS = 8
SWA_KV_HEADS = 2
SWA_GROUP = SWA_HEADS // SWA_KV_HEADS
SWA_HEAD_DIM = 64
MLA_HEADS = 8
MLA_Q_RANK = 256
MLA_KV_RANK = 128
MLA_NOPE_DIM = 64
MLA_ROPE_DIM = 32
MLA_V_DIM = 64
MLA_QK_DIM = MLA_NOPE_DIM + MLA_ROPE_DIM
SWA_Q_W = SWA_HEADS * SWA_HEAD_DIM
SWA_KV_W = SWA_KV_HEADS * SWA_HEAD_DIM
MLA_OUT_W = MLA_HEADS * MLA_V_DIM
SCALE_A = SWA_HEAD_DIM ** -0.5
SCALE_B = MLA_QK_DIM ** -0.5
LOG2E = 1.4426950408889634
Q_SCALE = SCALE_B * LOG2E
ADAM_LR = 0.001
ADAM_B1 = 0.9
ADAM_B2 = 0.999
ADAM_EPS = 1e-08
ADAM_WD = 0.01
ADAM_STEP = 10

LANE = 128
N_DEV = 8
HP = 8 * LANE
PO_QA, PO_KA, PO_VA = 0, HP, HP + 2 * LANE
PO_CQ = PO_VA + 2 * LANE
PO_CKV = PO_CQ + MLA_Q_RANK
PO_KR = PO_CKV + MLA_KV_RANK
PW_IN = PO_KR + LANE
N_TAB = 7
VMEM_LIMIT = 56 * 2 ** 20
TN_VMEM_BUDGET = 36 * 2 ** 20
MLA_HB = 4
MLA_HB_BWD = 8
HALF = LANE // 2
assert SWA_HEAD_DIM == HALF and MLA_V_DIM == HALF

NT = (((1,), (1,)), ((), ()))
TN = (((0,), (0,)), ((), ()))


def _tile(t):
    return 384 if t % 384 == 0 else 128


def _params(*sem):
    return pltpu.CompilerParams(dimension_semantics=sem, vmem_limit_bytes=VMEM_LIMIT)


def _row(tm, n):
    return pl.BlockSpec((tm, n), lambda i: (i, 0))


def _const(shape):
    return pl.BlockSpec(shape, lambda i: (0,) * len(shape))


def _dot(a, b):
    return jnp.dot(a, b, preferred_element_type=F32)


def _dot_nt(a, b):
    return lax.dot_general(a, b, NT, preferred_element_type=F32)


def _dot_tn(a, b):
    return lax.dot_general(a, b, TN, preferred_element_type=F32)


def _rope(x, c, s1, s2, shift):
    return x * c + pltpu.roll(x, LANE - shift, 1) * s1 + pltpu.roll(x, shift, 1) * s2


def _rope_t(dy, c, s1, s2, shift):
    return dy * c + pltpu.roll(dy * s1, shift, 1) + pltpu.roll(dy * s2, LANE - shift, 1)


def _rms_r(x, n):
    return lax.rsqrt(jnp.sum(x * x, axis=-1, keepdims=True) * (1.0 / n) + EPS)


def _rms_bwd(x, g, dy, n):
    r = _rms_r(x, n)
    xh = x * r
    dxh = dy * g
    dx = r * (dxh - xh * (jnp.sum(dxh * xh, axis=-1, keepdims=True) * (1.0 / n)))
    return dx, jnp.sum(dy * xh, axis=0, keepdims=True)


def _acc(ref, val, first):
    @pl.when(first)
    def _():
        ref[...] = val

    @pl.when(jnp.logical_not(first))
    def _():
        ref[...] += val


def _pack_pair(even, odd):
    return even + pltpu.roll(odd, HALF, 1)


def _pair_half(slab, half):
    return slab if half == 0 else pltpu.roll(slab, HALF, 1)


def _unpack_pair(slab, half):
    x = _pair_half(slab, half)
    return jnp.where(lax.broadcasted_iota(jnp.int32, x.shape, 1) < HALF, x, 0.0)


def _tabs(tab_ref):
    return [tab_ref[:, LANE * i:LANE * (i + 1)] for i in range(N_TAB)]


def _pre_fwd(h, g1, win, gq, wqu, gkv, wkv, tabs):
    t = h.shape[0]
    tm = _tile(t)

    def body(h_ref, g1_ref, win_ref, gq_ref, wqu_ref, gkv_ref, wkv_ref, tab_ref,
             u_ref, qa_ref, ka_ref, va_ref, cq_ref, ckv_ref, qn_ref, kvn_ref, qb_ref, kf_ref, vb_ref):
        ca, sa1, sa2, cb, sb1, sb2, ck = _tabs(tab_ref)
        hv = h_ref[...]
        u = (hv * _rms_r(hv, D_MODEL) * g1_ref[...]).astype(BF16)
        u_ref[...] = u
        p = _dot(u, win_ref[...])
        for c in range(SWA_HEADS):
            sl = slice(LANE * c, LANE * (c + 1))
            qa_ref[:, sl] = _rope(p[:, PO_QA + LANE * c:PO_QA + LANE * (c + 1)], ca, sa1, sa2, 32).astype(BF16)
        for c in range(SWA_KV_HEADS):
            sl = slice(LANE * c, LANE * (c + 1))
            ka_ref[:, sl] = _rope(p[:, PO_KA + LANE * c:PO_KA + LANE * (c + 1)], ca, sa1, sa2, 32).astype(BF16)
        va_ref[...] = p[:, PO_VA:PO_CQ].astype(BF16)
        cq = p[:, PO_CQ:PO_CKV]
        ckv = p[:, PO_CKV:PO_KR]
        cq_ref[...] = cq
        ckv_ref[...] = ckv
        qn = (cq * _rms_r(cq, MLA_Q_RANK) * gq_ref[...]).astype(BF16)
        qn_ref[...] = qn
        qb = _dot(qn, wqu_ref[...])
        kvn = (ckv * _rms_r(ckv, MLA_KV_RANK) * gkv_ref[...]).astype(BF16)
        kvn_ref[...] = kvn
        kv = _dot(kvn, wkv_ref[...])
        kr = _rope(p[:, PO_KR:PW_IN], ck, sb1, sb2, 16)
        for c in range(MLA_HEADS):
            sl = slice(LANE * c, LANE * (c + 1))
            qb_ref[:, sl] = (_rope(qb[:, sl], cb, sb1, sb2, 16) * Q_SCALE).astype(BF16)
            kf_ref[:, sl] = (kv[:, sl] + kr).astype(BF16)
        vb_ref[...] = kv[:, HP:].astype(BF16)

    widths = [(D_MODEL, BF16), (HP, BF16), (2 * LANE, BF16), (2 * LANE, BF16), (MLA_Q_RANK, F32),
              (MLA_KV_RANK, F32), (MLA_Q_RANK, BF16), (MLA_KV_RANK, BF16), (HP, BF16), (HP, BF16), (HP, BF16)]
    return pl.pallas_call(
        body, name="pre_fwd", grid=(t // tm,),
        in_specs=[_row(tm, D_MODEL), _const(g1.shape), _const(win.shape), _const(gq.shape), _const(wqu.shape),
                  _const(gkv.shape), _const(wkv.shape), _row(tm, N_TAB * LANE)],
        out_specs=[_row(tm, w) for w, _ in widths],
        out_shape=[jax.ShapeDtypeStruct((t, w), d) for w, d in widths],
        compiler_params=_params("parallel"),
    )(h, g1, win, gq, wqu, gkv, wkv, tabs)


def _swa_mask(nb):
    row = lax.broadcasted_iota(jnp.int32, (SWA_GROUP * BLOCK, 2 * BLOCK), 0) & (BLOCK - 1)
    col = lax.broadcasted_iota(jnp.int32, (SWA_GROUP * BLOCK, 2 * BLOCK), 1)
    return (col > row) & (col <= row + BLOCK) & (col + (nb - 1) * BLOCK >= FRONT)


def _swa_group(ref, rows, j):
    return jnp.concatenate([ref[rows, LANE * (SWA_GROUP * j + g):LANE * (SWA_GROUP * j + g + 1)]
                            for g in range(SWA_GROUP)], axis=0)


def _swa_packed_group(ref, rows, j):
    heads = [SWA_GROUP * j + g for g in range(SWA_GROUP)]
    return jnp.concatenate([_pair_half(ref[rows, LANE * (hd // 2):LANE * (hd // 2 + 1)], hd % 2) for hd in heads], axis=0)


def _swa_sinks(sink_ref, j):
    return jnp.concatenate([jnp.full((BLOCK, 1), sink_ref[0, SWA_GROUP * j + g], F32) for g in range(SWA_GROUP)], axis=0)


def _swa_keys(prev_ref, cur_ref, rb, j):
    sl = slice(LANE * j, LANE * (j + 1))
    if rb == 0:
        return jnp.concatenate([prev_ref[:, sl], cur_ref[:BLOCK, sl]], axis=0)
    return cur_ref[BLOCK * (rb - 1):BLOCK * (rb + 1), sl]


def _swa_chains(t):
    return [(rb, j) for rb in range(_tile(t) // BLOCK) for j in range(SWA_KV_HEADS)]


def _swa_scores(sink_ref, q_ref, kp_ref, kc_ref, n, t):
    r = _tile(t) // BLOCK
    chains = _swa_chains(t)
    qs = [_swa_group(q_ref, slice(BLOCK * rb, BLOCK * (rb + 1)), j) for rb, j in chains]
    ks = [_swa_keys(kp_ref, kc_ref, rb, j) for rb, j in chains]
    ss = [_dot_nt(q4, k2) for q4, k2 in zip(qs, ks)]
    masks = [_swa_mask(n * r + rb) for rb in range(r)]
    out = []
    for (rb, j), s in zip(chains, ss):
        sink = _swa_sinks(sink_ref, j)
        s = jnp.where(masks[rb], s * SCALE_A, NEG)
        m = jnp.maximum(jnp.max(s, axis=1, keepdims=True), sink)
        e = jnp.exp(s - m)
        es = jnp.exp(sink - m)
        inv = 1.0 / (jnp.sum(e, axis=1, keepdims=True) + es)
        out.append((e * inv, es * inv))
    return qs, ks, out


def _swa_specs(t):
    ts = _tile(t)
    r = ts // BLOCK
    prev = lambda n: (jnp.maximum(n * r - 1, 0), 0)
    cur = lambda n: (n, 0)
    return [pl.BlockSpec(memory_space=pltpu.SMEM), pl.BlockSpec((ts, HP), cur),
            pl.BlockSpec((BLOCK, 2 * LANE), prev), pl.BlockSpec((ts, 2 * LANE), cur),
            pl.BlockSpec((BLOCK, 2 * LANE), prev), pl.BlockSpec((ts, 2 * LANE), cur)]


def _swa_fwd(sinks, q, k, v):
    t = q.shape[0]
    ts = _tile(t)

    def body(sink_ref, q_ref, kp_ref, kc_ref, vp_ref, vc_ref, o_ref):
        chains = _swa_chains(t)
        _, _, probs = _swa_scores(sink_ref, q_ref, kp_ref, kc_ref, pl.program_id(0), t)
        os_ = [_dot(p.astype(BF16), _swa_keys(vp_ref, vc_ref, rb, j)) for (rb, j), (p, _) in zip(chains, probs)]
        for (rb, j), o4 in zip(chains, os_):
            for g in range(0, SWA_GROUP, 2):
                pair = (SWA_GROUP * j + g) // 2
                o_ref[BLOCK * rb:BLOCK * (rb + 1), LANE * pair:LANE * (pair + 1)] = _pack_pair(
                    o4[BLOCK * g:BLOCK * (g + 1)], o4[BLOCK * (g + 1):BLOCK * (g + 2)])

    return pl.pallas_call(
        body, name="swa_fwd", grid=(t // ts,),
        in_specs=_swa_specs(t),
        out_specs=pl.BlockSpec((ts, SWA_Q_W), lambda n: (n, 0)),
        out_shape=jax.ShapeDtypeStruct((t, SWA_Q_W), F32),
        compiler_params=_params("parallel"),
    )(sinks, q, k, k, v, v)


def _causal_mask(q0, k0, tq, tk, transposed):
    if transposed:
        key = k0 + lax.broadcasted_iota(jnp.int32, (tk, tq), 0)
        qry = q0 + lax.broadcasted_iota(jnp.int32, (tk, tq), 1)
    else:
        qry = q0 + lax.broadcasted_iota(jnp.int32, (tq, tk), 0)
        key = k0 + lax.broadcasted_iota(jnp.int32, (tq, tk), 1)
    return (key <= qry) & (key >= FRONT)


def _heads(ref, hb, rows=slice(None)):
    return [ref[rows, LANE * a:LANE * (a + 1)] for a in range(hb)]


def _as_row(col):
    return jnp.broadcast_to(col, (col.shape[0], LANE)).T[:1, :]


def _row_stats(t, hb):
    tq = _tile(t)
    return (jax.ShapeDtypeStruct((MLA_HEADS, t // tq, 1, tq), F32),
            pl.BlockSpec((hb, None, 1, tq), lambda h, i: (h, i, 0, 0)))


def _mla_fwd(q, k, v, shards=()):
    t = q.shape[0]
    tq = _tile(t)
    nq = t // tq
    n = len(shards)
    steps = (MLA_HEADS // MLA_HB) * nq

    def body(q_ref, k_ref, v_ref, *rest):
        x_refs, (o_ref, lse_ref, lser_ref), out_refs = rest[:n], rest[n:n + 3], rest[n + 3:2 * n + 3]
        acc_sc, sems = rest[2 * n + 3], rest[2 * n + 4:]
        i = pl.program_id(1)
        step_id = pl.program_id(0) * nq + i
        if n:
            plan = _gather_plan(x_refs, out_refs, *sems)
            pl.when(step_id == 0)(plan.start)
            pl.when(step_id == steps // 2)(plan.forward)
        qs = _heads(q_ref, MLA_HB)
        acc_sc[...] = jnp.zeros(acc_sc.shape, F32)

        def step(j, carry, masked):
            rows = pl.ds(pl.multiple_of(j * tq, tq), tq)
            ks, vs = _heads(k_ref, MLA_HB, rows), _heads(v_ref, MLA_HB, rows)
            ss = [_dot_nt(qh, kh) for qh, kh in zip(qs, ks)]
            if masked:
                mask = _causal_mask(i * tq, j * tq, tq, tq, False)
                ss = [jnp.where(mask, s, NEG) for s in ss]
            mid, out = [], []
            for s, (m, l) in zip(ss, carry):
                mn = jnp.maximum(m, jnp.max(s, axis=1, keepdims=True))
                al = jnp.exp2(m - mn)
                p = jnp.exp2(s - mn)
                out.append((mn, al * l + jnp.sum(p, axis=1, keepdims=True)))
                mid.append((al, p.astype(BF16)))
            for a, ((al, p), vh) in enumerate(zip(mid, vs)):
                acc_sc[a] = al * acc_sc[a] + _dot(p, vh)
            return tuple(out)

        init = ((jnp.full((tq, 1), NEG, F32), jnp.zeros((tq, 1), F32)),) * MLA_HB
        carry = lax.fori_loop(0, jnp.minimum(i, 1) + 1, lambda it, c: step(it * i, c, True), init)
        carry = lax.fori_loop(1, i, lambda j, c: step(j, c, False), carry)
        outs = [acc_sc[a] * (1.0 / l) for a, (_, l) in enumerate(carry)]
        for a in range(0, MLA_HB, 2):
            o_ref[:, HALF * a:HALF * (a + 2)] = _pack_pair(outs[a], outs[a + 1])
        for a, (m, l) in enumerate(carry):
            lse = m + jnp.log2(l)
            lse_ref[:, LANE * a:LANE * (a + 1)] = jnp.broadcast_to(lse, (tq, LANE))
            lser_ref[a] = _as_row(lse)
        if n:
            pl.when(step_id == steps - 1)(plan.finish)

    blk = pl.BlockSpec((tq, MLA_HB * LANE), lambda h, i: (i, h))
    full = pl.BlockSpec((t, MLA_HB * LANE), lambda h, i: (0, h))
    rows_shape, rows_spec = _row_stats(t, MLA_HB)
    packed = pl.BlockSpec((tq, MLA_HB * HALF), lambda h, i: (i, h))
    out = pl.pallas_call(
        body, name="mla_fwd_gather" if n else "mla_fwd", grid=(MLA_HEADS // MLA_HB, nq),
        in_specs=[blk, full, full] + [ANY] * n, out_specs=[packed, blk, rows_spec] + [ANY] * n,
        out_shape=[jax.ShapeDtypeStruct((t, MLA_OUT_W), F32), jax.ShapeDtypeStruct((t, HP), F32), rows_shape]
        + [jax.ShapeDtypeStruct((N_DEV,) + a.shape, a.dtype) for a in shards],
        scratch_shapes=[pltpu.VMEM((MLA_HB, tq, LANE), F32)] + (_comm_sems(n) if n else []),
        compiler_params=_params("arbitrary", "arbitrary"),
    )(q, k, v, *shards)
    return out[0], (out[1], out[2]), out[3:]


def _mix_fwd(h, oa, ob, ga, gb, wo, g2):
    t = h.shape[0]
    tm = _tile(t)

    def body(h_ref, oa_ref, ob_ref, ga_ref, gb_ref, wo_ref, g2_ref, h2_ref, mix_ref, u2_ref):
        oa_v = oa_ref[...]
        ob_v = ob_ref[...]
        na = (oa_v * _rms_r(oa_v, SWA_Q_W) * ga_ref[...]).astype(BF16)
        nb = (ob_v * _rms_r(ob_v, MLA_OUT_W) * gb_ref[...]).astype(BF16)
        mix_ref[:, :SWA_Q_W] = na
        mix_ref[:, SWA_Q_W:] = nb
        h2 = h_ref[...] + _dot(na, wo_ref[:SWA_Q_W, :]) + _dot(nb, wo_ref[SWA_Q_W:, :])
        h2_ref[...] = h2
        u2_ref[...] = (h2 * _rms_r(h2, D_MODEL) * g2_ref[...]).astype(BF16)

    mix_w = SWA_Q_W + MLA_OUT_W
    return pl.pallas_call(
        body, name="mix_fwd", grid=(t // tm,),
        in_specs=[_row(tm, D_MODEL), _row(tm, SWA_Q_W), _row(tm, MLA_OUT_W), _const(ga.shape), _const(gb.shape),
                  _const(wo.shape), _const(g2.shape)],
        out_specs=[_row(tm, D_MODEL), _row(tm, mix_w), _row(tm, D_MODEL)],
        out_shape=[jax.ShapeDtypeStruct((t, D_MODEL), F32), jax.ShapeDtypeStruct((t, mix_w), BF16),
                   jax.ShapeDtypeStruct((t, D_MODEL), BF16)],
        compiler_params=_params("parallel"),
    )(h, oa, ob, ga, gb, wo, g2)


def _ffn_fwd(h2, u2, wg, wu, wd):
    t = h2.shape[0]
    tm = _tile(t)
    dff = wg.shape[1]

    def body(h2_ref, u2_ref, wg_ref, wu_ref, wd_ref, h3_ref, g_ref, up_ref):
        u2v = u2_ref[...]
        g = _dot(u2v, wg_ref[...])
        up = _dot(u2v, wu_ref[...])
        g_ref[...] = g.astype(BF16)
        up_ref[...] = up.astype(BF16)
        a = (g * jax.nn.sigmoid(g) * up).astype(BF16)
        h3_ref[...] = h2_ref[...] + _dot(a, wd_ref[...])

    return pl.pallas_call(
        body, name="ffn_fwd", grid=(t // tm,),
        in_specs=[_row(tm, D_MODEL), _row(tm, D_MODEL), _const(wg.shape), _const(wu.shape), _const(wd.shape)],
        out_specs=[_row(tm, D_MODEL), _row(tm, dff), _row(tm, dff)],
        out_shape=[jax.ShapeDtypeStruct((t, D_MODEL), F32), jax.ShapeDtypeStruct((t, dff), BF16),
                   jax.ShapeDtypeStruct((t, dff), BF16)],
        compiler_params=_params("parallel"),
    )(h2, u2, wg, wu, wd)


def _loss_bwd(h, gf, target):
    t = h.shape[0]
    tm = _tile(t)
    first_row = FRONT + N_META

    def body(h_ref, gf_ref, t_ref, dh_ref, dgf_ref, loss_ref):
        i = pl.program_id(0)
        hv = h_ref[...]
        y = hv * _rms_r(hv, D_MODEL) * gf_ref[...]
        row = i * tm + lax.broadcasted_iota(jnp.int32, (tm, 1), 0)
        err = jnp.where(row >= first_row, y - t_ref[...], 0.0)
        dx, dg = _rms_bwd(hv, gf_ref[...], err * (1.0 / D_MODEL), D_MODEL)
        dh_ref[...] = dx
        _acc(dgf_ref, dg, i == 0)
        part = 0.5 * jnp.sum(jnp.sum(err * err, axis=1, keepdims=True) * (1.0 / D_MODEL), axis=0, keepdims=True)
        _acc(loss_ref, jnp.broadcast_to(part, (1, LANE)), i == 0)

    return pl.pallas_call(
        body, name="loss_bwd", grid=(t // tm,),
        in_specs=[_row(tm, D_MODEL), _const(gf.shape), _row(tm, D_MODEL)],
        out_specs=[_row(tm, D_MODEL), _const((1, D_MODEL)), _const((1, LANE))],
        out_shape=[jax.ShapeDtypeStruct((t, D_MODEL), F32), jax.ShapeDtypeStruct((1, D_MODEL), F32),
                   jax.ShapeDtypeStruct((1, LANE), F32)],
        compiler_params=_params("arbitrary"),
    )(h, gf, target)


def _tn_matmul(a, b, name, cols=None):
    t, k = a.shape
    first, n = cols or (0, b.shape[1])
    tk = next(c for c in (k, 1024, 512, 256, 128) if k % c == 0 and c <= 1024)
    fits = lambda c: 2 * (t * (tk + c) * 2 + tk * c * 2) <= TN_VMEM_BUDGET
    tn = next(c for c in (n, 1024, 512, 256, 128) if n % c == 0 and first % c == 0 and fits(c))

    def body(a_ref, b_ref, o_ref):
        o_ref[...] = _dot_tn(a_ref[...], b_ref[...]).astype(BF16)

    return pl.pallas_call(
        body, name=name, grid=(k // tk, n // tn),
        in_specs=[pl.BlockSpec((t, tk), lambda i, j: (0, i)), pl.BlockSpec((t, tn), lambda i, j: (0, j + first // tn))],
        out_specs=pl.BlockSpec((tk, tn), lambda i, j: (i, j)),
        out_shape=jax.ShapeDtypeStruct((k, n), BF16),
        compiler_params=_params("parallel", "parallel"),
    )(a, b)


def _ffn_bwd_a(dh3, g, up, wd):
    t = dh3.shape[0]
    tm = _tile(t)
    dff = wd.shape[0]

    def body(dh3_ref, g_ref, up_ref, wd_ref, a_ref, dgu_ref, dh3b_ref):
        dh3b = dh3_ref[...].astype(BF16)
        dh3b_ref[...] = dh3b
        da = _dot_nt(dh3b, wd_ref[...])
        gv = g_ref[...].astype(F32)
        upv = up_ref[...].astype(F32)
        sg = jax.nn.sigmoid(gv)
        silu = gv * sg
        a_ref[...] = (silu * upv).astype(BF16)
        dgu_ref[:, :dff] = (da * upv * (sg * (1.0 + gv * (1.0 - sg)))).astype(BF16)
        dgu_ref[:, dff:] = (da * silu).astype(BF16)

    return pl.pallas_call(
        body, name="ffn_bwd_a", grid=(t // tm,),
        in_specs=[_row(tm, D_MODEL), _row(tm, dff), _row(tm, dff), _const(wd.shape)],
        out_specs=[_row(tm, dff), _row(tm, 2 * dff), _row(tm, D_MODEL)],
        out_shape=[jax.ShapeDtypeStruct((t, dff), BF16), jax.ShapeDtypeStruct((t, 2 * dff), BF16),
                   jax.ShapeDtypeStruct((t, D_MODEL), BF16)],
        compiler_params=_params("parallel"),
    )(dh3, g, up, wd)


def _ffn_bwd_b(dh3, dgu, h2, g2, wg, wu):
    t = dh3.shape[0]
    tm = _tile(t)
    dff = wg.shape[1]

    def body(dh3_ref, dgu_ref, h2_ref, g2_ref, wg_ref, wu_ref, dh2_ref, dh2b_ref, dg2_ref):
        du2 = _dot_nt(dgu_ref[:, :dff], wg_ref[...]) + _dot_nt(dgu_ref[:, dff:], wu_ref[...])
        dx, dg = _rms_bwd(h2_ref[...], g2_ref[...], du2, D_MODEL)
        dh2 = dh3_ref[...] + dx
        dh2_ref[...] = dh2
        dh2b_ref[...] = dh2.astype(BF16)
        _acc(dg2_ref, dg, pl.program_id(0) == 0)

    return pl.pallas_call(
        body, name="ffn_bwd_b", grid=(t // tm,),
        in_specs=[_row(tm, D_MODEL), _row(tm, 2 * dff), _row(tm, D_MODEL), _const(g2.shape), _const(wg.shape),
                  _const(wu.shape)],
        out_specs=[_row(tm, D_MODEL), _row(tm, D_MODEL), _const((1, D_MODEL))],
        out_shape=[jax.ShapeDtypeStruct((t, D_MODEL), F32), jax.ShapeDtypeStruct((t, D_MODEL), BF16),
                   jax.ShapeDtypeStruct((1, D_MODEL), F32)],
        compiler_params=_params("arbitrary"),
    )(dh3, dgu, h2, g2, wg, wu)


def _mix_bwd(dh2, oa, ob, ga, gb, wo):
    t = dh2.shape[0]
    tm = _tile(t)

    def body(dh2_ref, oa_ref, ob_ref, ga_ref, gb_ref, wo_ref, doa_ref, dob_ref, dga_ref, dgb_ref):
        first = pl.program_id(0) == 0
        d = dh2_ref[...]
        dxa, dga = _rms_bwd(oa_ref[...], ga_ref[...], _dot_nt(d, wo_ref[:SWA_Q_W, :]), SWA_Q_W)
        dxb, dgb = _rms_bwd(ob_ref[...], gb_ref[...], _dot_nt(d, wo_ref[SWA_Q_W:, :]), MLA_OUT_W)
        for ref, dx, heads in ((doa_ref, dxa, SWA_HEADS), (dob_ref, dxb, MLA_HEADS)):
            for hd in range(heads):
                slab = dx[:, LANE * (hd // 2):LANE * (hd // 2 + 1)]
                ref[:, LANE * hd:LANE * (hd + 1)] = _unpack_pair(slab, hd % 2).astype(BF16)
        _acc(dga_ref, dga, first)
        _acc(dgb_ref, dgb, first)

    return pl.pallas_call(
        body, name="mix_bwd", grid=(t // tm,),
        in_specs=[_row(tm, D_MODEL), _row(tm, SWA_Q_W), _row(tm, MLA_OUT_W), _const(ga.shape), _const(gb.shape),
                  _const(wo.shape)],
        out_specs=[_row(tm, HP), _row(tm, HP), _const((1, SWA_Q_W)), _const((1, MLA_OUT_W))],
        out_shape=[jax.ShapeDtypeStruct((t, HP), BF16), jax.ShapeDtypeStruct((t, HP), BF16),
                   jax.ShapeDtypeStruct((1, SWA_Q_W), F32), jax.ShapeDtypeStruct((1, MLA_OUT_W), F32)],
        compiler_params=_params("arbitrary"),
    )(dh2, oa, ob, ga, gb, wo)


def _swa_bwd(sinks, q, k, v, o, do):
    t = q.shape[0]
    ts = _tile(t)

    def body(sink_ref, q_ref, kp_ref, kc_ref, vp_ref, vc_ref, o_ref, do_ref,
             dq_ref, dkc_ref, dkp_ref, dvc_ref, dvp_ref, dsink_ref):
        n = pl.program_id(0)
        chains = _swa_chains(t)
        qs, ks, probs = _swa_scores(sink_ref, q_ref, kp_ref, kc_ref, n, t)
        dos = [_swa_group(do_ref, slice(BLOCK * rb, BLOCK * (rb + 1)), j) for rb, j in chains]
        vs = [_swa_keys(vp_ref, vc_ref, rb, j) for rb, j in chains]
        dps = [_dot_nt(do4, v2) for do4, v2 in zip(dos, vs)]
        dss, dsks = [], []
        for (rb, j), (p, psink), do4, dp in zip(chains, probs, dos, dps):
            o4 = _swa_packed_group(o_ref, slice(BLOCK * rb, BLOCK * (rb + 1)), j)
            delta = jnp.sum(o4 * do4.astype(F32), axis=1, keepdims=True)
            dss.append(p * (dp - delta) * SCALE_A)
            dsks.append(-psink * delta)
        dqs = [_dot(ds.astype(BF16), k2) for ds, k2 in zip(dss, ks)]
        dks = [_dot(ds.T.astype(BF16), q4) for ds, q4 in zip(dss, qs)]
        dvs = [_dot(p.T.astype(BF16), do4) for (p, _), do4 in zip(probs, dos)]
        dsink = [jnp.zeros((1, LANE), F32)] * SWA_HEADS
        ext = {}
        for (rb, j), dq4, dk2, dv2, dsk in zip(chains, dqs, dks, dvs, dsks):
            for g in range(SWA_GROUP):
                hd = SWA_GROUP * j + g
                rows = slice(BLOCK * g, BLOCK * (g + 1))
                dq_ref[BLOCK * rb:BLOCK * (rb + 1), LANE * hd:LANE * (hd + 1)] = dq4[rows]
                dsink[hd] = dsink[hd] + jnp.sum(dsk[rows], axis=0, keepdims=True)
            for half in range(2):
                key = (j, rb + half)
                part = (dk2[BLOCK * half:BLOCK * (half + 1)], dv2[BLOCK * half:BLOCK * (half + 1)])
                ext[key] = part if key not in ext else (ext[key][0] + part[0], ext[key][1] + part[1])
        for (j, blk), (dk, dv) in ext.items():
            sl = slice(LANE * j, LANE * (j + 1))
            if blk == 0:
                dkp_ref[:, sl] = dk
                dvp_ref[:, sl] = dv
            else:
                dkc_ref[BLOCK * (blk - 1):BLOCK * blk, sl] = dk
                dvc_ref[BLOCK * (blk - 1):BLOCK * blk, sl] = dv
        for hd in range(SWA_HEADS):
            _acc(dsink_ref.at[hd:hd + 1, :], jnp.broadcast_to(dsink[hd], (1, LANE)), n == 0)

    cur = lambda n: (n, 0)
    kv = pl.BlockSpec((ts, 2 * LANE), cur)
    kvp = pl.BlockSpec((BLOCK, 2 * LANE), cur)
    hp = pl.BlockSpec((ts, HP), cur)
    kvs = jax.ShapeDtypeStruct((t, 2 * LANE), F32)
    kvps = jax.ShapeDtypeStruct((t // ts * BLOCK, 2 * LANE), F32)
    return pl.pallas_call(
        body, name="swa_bwd", grid=(t // ts,),
        in_specs=_swa_specs(t) + [pl.BlockSpec((ts, SWA_Q_W), cur), hp],
        out_specs=[hp, kv, kvp, kv, kvp, _const((SWA_HEADS, LANE))],
        out_shape=[jax.ShapeDtypeStruct((t, HP), F32), kvs, kvps, kvs, kvps,
                   jax.ShapeDtypeStruct((SWA_HEADS, LANE), F32)],
        compiler_params=_params("arbitrary"),
    )(sinks, q, k, k, v, v, o, do)


def _mla_bwd_dq(q, k, v, o, do, lse):
    t = q.shape[0]
    tq = _tile(t)
    nq = t // tq

    def body(q_ref, k_ref, v_ref, o_ref, do_ref, lse_ref, dq_ref, dl_ref, dq_sc):
        i = pl.program_id(1)
        hb = MLA_HB_BWD
        qs, dos = _heads(q_ref, hb), _heads(do_ref, hb)
        os_ = [_pair_half(o_ref[:, LANE * (a // 2):LANE * (a // 2 + 1)], a % 2) for a in range(hb)]
        deltas = [jnp.sum(oh * doh.astype(F32), axis=1, keepdims=True) for oh, doh in zip(os_, dos)]
        lses = [lh[:, :1] for lh in _heads(lse_ref, hb)]

        dq_sc[...] = jnp.zeros(dq_sc.shape, F32)

        def step(j, carry, masked):
            rows = pl.ds(pl.multiple_of(j * tq, tq), tq)
            ks, vs = _heads(k_ref, hb, rows), _heads(v_ref, hb, rows)
            ss = [_dot_nt(qh, kh) for qh, kh in zip(qs, ks)]
            dps = [_dot_nt(doh, vh) for doh, vh in zip(dos, vs)]
            if masked:
                mask = _causal_mask(i * tq, j * tq, tq, tq, False)
                ss = [jnp.where(mask, s, NEG) for s in ss]
            dss = [(jnp.exp2(s - lh) * (dp - dl)).astype(BF16) for s, dp, lh, dl in zip(ss, dps, lses, deltas)]
            for a, (ds, kh) in enumerate(zip(dss, ks)):
                dq_sc[a] += _dot(ds, kh)
            return carry

        lax.fori_loop(0, jnp.minimum(i, 1) + 1, lambda it, c: step(it * i, c, True), 0)
        lax.fori_loop(1, i, lambda j, c: step(j, c, False), 0)
        for a, dl in enumerate(deltas):
            dq_ref[:, LANE * a:LANE * (a + 1)] = dq_sc[a] * SCALE_B
            dl_ref[a] = _as_row(dl)

    blk = pl.BlockSpec((tq, MLA_HB_BWD * LANE), lambda h, i: (i, h))
    full = pl.BlockSpec((t, MLA_HB_BWD * LANE), lambda h, i: (0, h))
    packed = pl.BlockSpec((tq, MLA_HB_BWD * HALF), lambda h, i: (i, h))
    rows_shape, rows_spec = _row_stats(t, MLA_HB_BWD)
    return pl.pallas_call(
        body, name="mla_bwd_dq", grid=(MLA_HEADS // MLA_HB_BWD, nq),
        in_specs=[blk, full, full, packed, blk, blk], out_specs=[blk, rows_spec],
        out_shape=[jax.ShapeDtypeStruct((t, HP), F32), rows_shape],
        scratch_shapes=[pltpu.VMEM((MLA_HB_BWD, tq, LANE), F32)],
        compiler_params=_params("parallel", "parallel"),
    )(q, k, v, o, do, lse)


def _mla_bwd_dkv(q, k, v, do, lse_t, dl_t, slabs=()):
    t = q.shape[0]
    tq = _tile(t)
    nq = t // tq
    n = len(slabs)
    hb = MLA_HB_BWD
    steps = (MLA_HEADS // hb) * nq

    def body(k_ref, v_ref, q_ref, do_ref, lse_ref, dl_ref, *rest):
        in_refs, (dk_ref, dv_ref), out_refs = rest[:n], rest[n:n + 2], rest[n + 2:2 * n + 2]
        (dk_sc, dv_sc), sems = rest[2 * n + 2:2 * n + 4], rest[2 * n + 4:]
        group = pl.program_id(0)
        j = pl.program_id(1)
        step_id = group * nq + j
        if n:
            plan = _exchange_plan(in_refs, out_refs, *sems)
            pl.when(step_id == 0)(plan.start)
        ks, vs = _heads(k_ref, hb), _heads(v_ref, hb)

        dk_sc[...] = jnp.zeros(dk_sc.shape, F32)
        dv_sc[...] = jnp.zeros(dv_sc.shape, F32)

        def step(i, carry, masked):
            rows = pl.ds(pl.multiple_of(i * tq, tq), tq)
            qs, dos = _heads(q_ref, hb, rows), _heads(do_ref, hb, rows)
            sts = [_dot_nt(kh, qh) for kh, qh in zip(ks, qs)]
            dpts = [_dot_nt(vh, doh) for vh, doh in zip(vs, dos)]
            if masked:
                mask = _causal_mask(i * tq, j * tq, tq, tq, True)
                sts = [jnp.where(mask, st, NEG) for st in sts]
            pts = [jnp.exp2(st - lse_ref[group * hb + a, i]) for a, st in enumerate(sts)]
            dsts = [(pt * (dpt - dl_ref[group * hb + a, i])).astype(BF16) for a, (pt, dpt) in enumerate(zip(pts, dpts))]
            for a, (dst, pt, qh, doh) in enumerate(zip(dsts, pts, qs, dos)):
                dk_sc[a] += _dot(dst, qh)
                dv_sc[a] += _dot(pt.astype(BF16), doh)
            return carry

        split = jnp.where(j == 0, nq, j + 1)
        lax.fori_loop(j, split, lambda i, c: step(i, c, True), 0)
        lax.fori_loop(split, nq, lambda i, c: step(i, c, False), 0)
        for a in range(hb):
            dk_ref[:, LANE * a:LANE * (a + 1)] = dk_sc[a] * (1.0 / LOG2E)
            dv_ref[:, LANE * a:LANE * (a + 1)] = dv_sc[a]
        if n:
            pl.when(step_id == steps - 1)(plan.finish)

    blk = pl.BlockSpec((tq, hb * LANE), lambda h, j: (j, h))
    full = pl.BlockSpec((t, hb * LANE), lambda h, j: (0, h))
    rows = pl.BlockSpec((MLA_HEADS, nq, 1, tq), lambda h, j: (0, 0, 0, 0))
    out = pl.pallas_call(
        body, name="mla_bwd_dkv_exchange" if n else "mla_bwd_dkv", grid=(MLA_HEADS // hb, nq),
        in_specs=[blk, blk, full, full, rows, rows] + [ANY] * n, out_specs=[blk, blk] + [ANY] * n,
        out_shape=[jax.ShapeDtypeStruct((t, HP), F32)] * 2 + [jax.ShapeDtypeStruct(a.shape, a.dtype) for a in slabs],
        scratch_shapes=[pltpu.VMEM((hb, tq, LANE), F32)] * 2 + (_comm_sems(n) if n else []),
        compiler_params=_params("arbitrary", "arbitrary"),
    )(k, v, q, do, lse_t, dl_t, *slabs)
    return out[0], out[1], out[2:]


def _pre_bwd(dh2, h, cq, ckv, dqa, dka, dka_next, dva, dva_next, dqb, dkf, dvb, g1, win, gq, wqu, gkv, wkv, tabs):
    t = h.shape[0]
    tm = _tile(t)

    def body(dh2_ref, h_ref, cq_ref, ckv_ref, dqa_ref, dka_ref, dkan_ref, dva_ref, dvan_ref, dqb_ref, dkf_ref, dvb_ref,
             g1_ref, win_ref, gq_ref, wqu_ref, gkv_ref, wkv_ref, tab_ref,
             dh_ref, dp_ref, dqbo_ref, dkvo_ref, dg1_ref, dgq_ref, dgkv_ref):
        first = pl.program_id(0) == 0
        ca, sa1, sa2, cb, sb1, sb2, ck = _tabs(tab_ref)
        dkr = jnp.zeros((tm, LANE), F32)
        for c in range(MLA_HEADS):
            sl = slice(LANE * c, LANE * (c + 1))
            dqbo_ref[:, sl] = _rope_t(dqb_ref[:, sl], cb, sb1, sb2, 16).astype(BF16)
            dkr += dkf_ref[:, sl]
        dkvo_ref[:, :HP] = dkf_ref[...].astype(BF16)
        dkvo_ref[:, HP:] = dvb_ref[...].astype(BF16)
        dcq, dgq = _rms_bwd(cq_ref[...], gq_ref[...], _dot_nt(dqbo_ref[...], wqu_ref[...]), MLA_Q_RANK)
        dckv, dgkv = _rms_bwd(ckv_ref[...], gkv_ref[...], _dot_nt(dkvo_ref[...], wkv_ref[...]), MLA_KV_RANK)
        for c in range(SWA_HEADS):
            sl = slice(LANE * c, LANE * (c + 1))
            dp_ref[:, PO_QA + LANE * c:PO_QA + LANE * (c + 1)] = _rope_t(dqa_ref[:, sl], ca, sa1, sa2, 32).astype(BF16)
        last = slice(tm - BLOCK, tm)
        more = pl.program_id(0) < t // tm - 1
        for c in range(SWA_KV_HEADS):
            sl = slice(LANE * c, LANE * (c + 1))
            dk = dka_ref[:, sl]
            dk_last = dk[tm - BLOCK:] + jnp.where(more, dkan_ref[:, sl], 0.0)
            cols = slice(PO_KA + LANE * c, PO_KA + LANE * (c + 1))
            if tm > BLOCK:
                dp_ref[:tm - BLOCK, cols] = _rope_t(dk[:tm - BLOCK], ca[:tm - BLOCK], sa1[:tm - BLOCK], sa2[:tm - BLOCK],
                                                    32).astype(BF16)
            dp_ref[last, cols] = _rope_t(dk_last, ca[tm - BLOCK:], sa1[tm - BLOCK:], sa2[tm - BLOCK:], 32).astype(BF16)
        if tm > BLOCK:
            dp_ref[:tm - BLOCK, PO_VA:PO_CQ] = dva_ref[:tm - BLOCK, :].astype(BF16)
        dp_ref[last, PO_VA:PO_CQ] = (dva_ref[tm - BLOCK:, :] + jnp.where(more, dvan_ref[...], 0.0)).astype(BF16)
        dp_ref[:, PO_CQ:PO_CKV] = dcq.astype(BF16)
        dp_ref[:, PO_CKV:PO_KR] = dckv.astype(BF16)
        dp_ref[:, PO_KR:PW_IN] = _rope_t(dkr, ck, sb1, sb2, 16).astype(BF16)
        dx, dg1 = _rms_bwd(h_ref[...], g1_ref[...], _dot_nt(dp_ref[...], win_ref[...]), D_MODEL)
        dh_ref[...] = dh2_ref[...] + dx
        _acc(dg1_ref, dg1, first)
        _acc(dgq_ref, dgq, first)
        _acc(dgkv_ref, dgkv, first)

    kv = _row(tm, 2 * LANE)
    nxt = pl.BlockSpec((BLOCK, 2 * LANE), lambda i: (jnp.minimum(i + 1, t // tm - 1), 0))
    return pl.pallas_call(
        body, name="pre_bwd", grid=(t // tm,),
        in_specs=[_row(tm, D_MODEL), _row(tm, D_MODEL), _row(tm, MLA_Q_RANK), _row(tm, MLA_KV_RANK), _row(tm, HP),
                  kv, nxt, kv, nxt, _row(tm, HP), _row(tm, HP), _row(tm, HP),
                  _const(g1.shape), _const(win.shape), _const(gq.shape), _const(wqu.shape), _const(gkv.shape),
                  _const(wkv.shape), _row(tm, N_TAB * LANE)],
        out_specs=[_row(tm, D_MODEL), _row(tm, PW_IN), _row(tm, HP), _row(tm, 2 * HP),
                   _const((1, D_MODEL)), _const((1, MLA_Q_RANK)), _const((1, MLA_KV_RANK))],
        out_shape=[jax.ShapeDtypeStruct((t, D_MODEL), F32), jax.ShapeDtypeStruct((t, PW_IN), BF16),
                   jax.ShapeDtypeStruct((t, HP), BF16), jax.ShapeDtypeStruct((t, 2 * HP), BF16),
                   jax.ShapeDtypeStruct((1, D_MODEL), F32), jax.ShapeDtypeStruct((1, MLA_Q_RANK), F32),
                   jax.ShapeDtypeStruct((1, MLA_KV_RANK), F32)],
        compiler_params=_params("arbitrary"),
    )(dh2, h, cq, ckv, dqa, dka, dka_next, dva, dva_next, dqb, dkf, dvb, g1, win, gq, wqu, gkv, wkv, tabs)


def _rope_tables(t):
    pos = (jnp.arange(t, dtype=jnp.int32) - FRONT).astype(F32)[:, None]
    lane = jnp.arange(LANE)[None, :]

    def table(dim, start):
        half = dim // 2
        inv = ROPE_THETA ** (-jnp.arange(0, dim, 2, dtype=F32) / dim)
        ang = pos * inv[None, :]
        cos = jnp.concatenate([jnp.cos(ang)] * 2, axis=1)
        sin = jnp.concatenate([jnp.sin(ang)] * 2, axis=1)
        pad = lambda a: jnp.pad(a, ((0, 0), (start, LANE - start - dim)))
        first = (lane >= start) & (lane < start + half)
        second = (lane >= start + half) & (lane < start + dim)
        return pad(cos), jnp.where(first, -pad(sin), 0.0), jnp.where(second, pad(sin), 0.0)

    ca, sa1, sa2 = table(SWA_HEAD_DIM, 0)
    ck, sb1, sb2 = table(MLA_ROPE_DIM, MLA_NOPE_DIM)
    cb = jnp.where(lane < MLA_NOPE_DIM, 1.0, ck)
    return jnp.concatenate([ca, sa1, sa2, cb, sb1, sb2, ck], axis=1)


def _pad_heads(w, heads, dim, axis):
    shp = w.shape
    w = w.reshape(shp[:axis] + (heads, dim) + shp[axis + 1:])
    pad = [(0, 0)] * w.ndim
    pad[axis + 1] = (0, LANE - dim)
    return jnp.pad(w, pad).reshape(shp[:axis] + (heads * LANE,) + shp[axis + 1:])


def _unpad_heads(w, heads, dim, axis):
    shp = w.shape
    w = w.reshape(shp[:axis] + (heads, LANE) + shp[axis + 1:])
    w = lax.slice_in_dim(w, 0, dim, axis=axis + 1)
    return w.reshape(shp[:axis] + (heads * dim,) + shp[axis + 1:])


def _pad_layer(w_in, w_q_up, w_kv_up):
    o1 = SWA_Q_W
    o2 = o1 + SWA_KV_W
    o3 = o2 + SWA_KV_W
    o4 = o3 + MLA_Q_RANK
    o5 = o4 + MLA_KV_RANK
    kr = jnp.pad(w_in[:, o5:], ((0, 0), (MLA_NOPE_DIM, LANE - MLA_QK_DIM)))
    win = jnp.concatenate([
        _pad_heads(w_in[:, :o1], SWA_HEADS, SWA_HEAD_DIM, 1),
        _pad_heads(w_in[:, o1:o2], SWA_KV_HEADS, SWA_HEAD_DIM, 1),
        _pad_heads(w_in[:, o2:o3], SWA_KV_HEADS, SWA_HEAD_DIM, 1),
        w_in[:, o3:o5], kr], axis=1)
    wqu = _pad_heads(w_q_up, MLA_HEADS, MLA_QK_DIM, 1)
    kv = w_kv_up.reshape(MLA_KV_RANK, MLA_HEADS, MLA_NOPE_DIM + MLA_V_DIM)
    wkv = jnp.concatenate([
        _pad_heads(kv[:, :, :MLA_NOPE_DIM].reshape(MLA_KV_RANK, -1), MLA_HEADS, MLA_NOPE_DIM, 1),
        _pad_heads(kv[:, :, MLA_NOPE_DIM:].reshape(MLA_KV_RANK, -1), MLA_HEADS, MLA_V_DIM, 1)], axis=1)
    return win, wqu, wkv


def _unpad_layer(dwin, dwqu, dwkv):
    d_w_in = jnp.concatenate([
        _unpad_heads(dwin[:, PO_QA:PO_KA], SWA_HEADS, SWA_HEAD_DIM, 1),
        _unpad_heads(dwin[:, PO_KA:PO_VA], SWA_KV_HEADS, SWA_HEAD_DIM, 1),
        _unpad_heads(dwin[:, PO_VA:PO_CQ], SWA_KV_HEADS, SWA_HEAD_DIM, 1),
        dwin[:, PO_CQ:PO_KR], dwin[:, PO_KR + MLA_NOPE_DIM:PO_KR + MLA_QK_DIM]], axis=1)
    d_w_q_up = _unpad_heads(dwqu, MLA_HEADS, MLA_QK_DIM, 1)
    dk = _unpad_heads(dwkv[:, :HP], MLA_HEADS, MLA_NOPE_DIM, 1).reshape(MLA_KV_RANK, MLA_HEADS, MLA_NOPE_DIM)
    dv = _unpad_heads(dwkv[:, HP:], MLA_HEADS, MLA_V_DIM, 1).reshape(MLA_KV_RANK, MLA_HEADS, MLA_V_DIM)
    d_w_kv_up = jnp.concatenate([dk, dv], axis=2).reshape(MLA_KV_RANK, -1)
    return d_w_in, d_w_q_up, d_w_kv_up


def _train_example(x, target, meta, vec, weights):
    s = x.shape[0]
    depth = vec["attn_norm"].shape[0]
    t = FRONT + N_META + s
    assert t % BLOCK == 0
    tabs = _rope_tables(t)
    h = jnp.concatenate([jnp.zeros((FRONT, D_MODEL), F32), meta, x], axis=0)
    tgt = jnp.concatenate([jnp.zeros((FRONT + N_META, D_MODEL), F32), target], axis=0)
    row = lambda v: v[None, :]

    saved = []
    for l in range(depth):
        w_in, w_q_up, w_kv_up, wo = weights.attn(l)
        win, wqu, wkv = _pad_layer(w_in, w_q_up, w_kv_up)
        g1, gq, gkv, g2, ga, gb = (row(vec[n][l]) for n in ("attn_norm", "q_norm", "kv_norm", "ffn_norm",
                                                            "out_norm_swa", "out_norm_mla"))
        sk = row(vec["sinks"][l])
        u, qa, ka, va, cq, ckv, qn, kvn, qb, kf, vb = _pre_fwd(h, g1, win, gq, wqu, gkv, wkv, tabs)
        oa = _swa_fwd(sk, qa, ka, va)
        ob, lse = weights.mla_fwd(l, qb, kf, vb)
        h2, mix, u2 = _mix_fwd(h, oa, ob, ga, gb, wo, g2)
        wg, wu, wd = weights.ffn(l)
        h3, gt, up = _ffn_fwd(h2, u2, wg, wu, wd)
        saved.append((h, u, qa, ka, va, cq, ckv, qn, kvn, qb, kf, vb, oa, ob, lse, h2, mix, u2, gt, up,
                      win, wqu, wkv, wo, ga, gb, g1, gq, gkv, g2, sk, wg, wu, wd))
        h = h3

    dh, d_final, loss = _loss_bwd(h, row(vec["final_norm"]), tgt)

    grads = []
    for l in reversed(range(depth)):
        (h0, u, qa, ka, va, cq, ckv, qn, kvn, qb, kf, vb, oa, ob, lse, h2, mix, u2, gt, up,
         win, wqu, wkv, wo, ga, gb, g1, gq, gkv, g2, sk, wg, wu, wd) = saved[l]
        dff = wg.shape[1]
        act, dgu, dhb = _ffn_bwd_a(dh, gt, up, wd)
        weights.ffn_grads(l, _tn_matmul(u2, dgu, "dw_gate", (0, dff)), _tn_matmul(u2, dgu, "dw_up", (dff, dff)),
                          _tn_matmul(act, dhb, "dw_down"))
        dh2, dh2b, d_g2 = _ffn_bwd_b(dh, dgu, h2, g2, wg, wu)
        d_wo = _tn_matmul(mix, dh2b, "dw_o")
        doa, dob, d_ga, d_gb = _mix_bwd(dh2b, oa, ob, ga, gb, wo)
        dqa, dkc, dkp, dvc, dvp, dsink = _swa_bwd(sk, qa, ka, va, oa, doa)
        dqb, dl = _mla_bwd_dq(qb, kf, vb, ob, dob, lse[0])
        dkf, dvb = weights.mla_bwd_dkv(l, qb, kf, vb, dob, lse[1], dl)
        dh, dp, dqbo, dkvo, d_g1, d_gq, d_gkv = _pre_bwd(
            dh2, h0, cq, ckv, dqa, dkc, dkp, dvc, dvp, dqb, dkf, dvb,
            g1, win, gq, wqu, gkv, wkv, tabs)
        d_win = _tn_matmul(u, dp, "dw_in")
        d_wqu = _tn_matmul(qn, dqbo, "dw_q_up")
        d_wkv = _tn_matmul(kvn, dkvo, "dw_kv_up")
        weights.attn_grads(l, *_unpad_layer(d_win, d_wqu, d_wkv), d_wo)
        grads.append(dict(attn_norm=d_g1[0], q_norm=d_gq[0], kv_norm=d_gkv[0], sinks=dsink[:, 0], out_norm_swa=d_ga[0],
                          out_norm_mla=d_gb[0], ffn_norm=d_g2[0]))
    grads = grads[::-1]
    stacked = {k: jnp.stack([g[k] for g in grads]) for k in grads[0]}
    stacked["final_norm"] = d_final[0]
    return loss[0, 0], dh[FRONT + N_META:], dh[FRONT:FRONT + N_META], stacked


MESH = pl.DeviceIdType.MESH
ANY = pl.BlockSpec(memory_space=pl.ANY)


def _place():
    return lax.axis_index("x"), lax.axis_index("y"), lax.axis_index("c")


def _index(x, y, c):
    return 4 * x + 2 * y + c


def _comm_sems(n):
    return [pltpu.SemaphoreType.DMA((n, N_DEV - 1)), pltpu.SemaphoreType.DMA((n, N_DEV - 1)),
            pltpu.SemaphoreType.DMA((n,))]


class _gather_plan:
    def __init__(self, x_refs, out_refs, send_sems, recv_sems, local_sems):
        self.x_refs, self.out_refs = x_refs, out_refs
        self.send_sems, self.recv_sems, self.local_sems = send_sems, recv_sems, local_sems
        self.n = len(x_refs)

    def _where(self):
        x, y, c = _place()
        return (x, y, c), (x, y, 1 - c), [(1 - x, y), (x, 1 - y), (1 - x, 1 - y)], c

    def _copy(self, i, k, block, to, from_input=False):
        slot = self.out_refs[i].at[_index(*block)]
        return pltpu.make_async_remote_copy(
            src_ref=self.x_refs[i] if from_input else slot, dst_ref=slot,
            send_sem=self.send_sems.at[i, k], recv_sem=self.recv_sems.at[i, k], device_id=to, device_id_type=MESH)

    def _mine(self, i, me):
        return pltpu.make_async_copy(self.x_refs[i], self.out_refs[i].at[_index(*me)], self.local_sems.at[i])

    def _first(self, me, sibling, chips, c):
        out = [self._copy(i, 1 + j, me, (*chip, c), True) for j, chip in enumerate(chips) for i in range(self.n)]
        return out + [self._copy(i, 0, me, sibling, True) for i in range(self.n)]

    def start(self):
        me, sibling, chips, c = self._where()
        for i in range(self.n):
            self._mine(i, me).start()
        for cp in self._first(me, sibling, chips, c):
            cp.start()

    def forward(self):
        me, sibling, chips, c = self._where()
        for j, chip in enumerate(chips):
            for i in range(self.n):
                self._copy(i, 1 + j, (*chip, c), me).wait_recv()
                self._copy(i, 4 + j, (*chip, c), sibling).start()

    def finish(self):
        me, sibling, chips, c = self._where()
        for i in range(self.n):
            self._copy(i, 0, sibling, me).wait_recv()
            for j, chip in enumerate(chips):
                self._copy(i, 4 + j, (*chip, 1 - c), me).wait_recv()
        for cp in self._first(me, sibling, chips, c):
            cp.wait_send()
        for j, chip in enumerate(chips):
            for i in range(self.n):
                self._copy(i, 4 + j, (*chip, c), sibling).wait_send()
        for i in range(self.n):
            self._mine(i, me).wait()


class _exchange_plan:
    def __init__(self, in_refs, out_refs, send_sems, recv_sems, local_sems):
        self.in_refs, self.out_refs = in_refs, out_refs
        self.send_sems, self.recv_sems, self.local_sems = send_sems, recv_sems, local_sems
        self.n = len(in_refs)

    def _copies(self):
        x, y, c = _place()
        me = _index(x, y, c)
        mine = [pltpu.make_async_copy(self.in_refs[i].at[me], self.out_refs[i].at[me], self.local_sems.at[i])
                for i in range(self.n)]
        remote = []
        for k in range(1, N_DEV):
            peer = (1 - x if k & 4 else x, 1 - y if k & 2 else y, 1 - c if k & 1 else c)
            remote += [pltpu.make_async_remote_copy(
                src_ref=self.in_refs[i].at[_index(*peer)], dst_ref=self.out_refs[i].at[me],
                send_sem=self.send_sems.at[i, k - 1], recv_sem=self.recv_sems.at[i, k - 1],
                device_id=peer, device_id_type=MESH) for i in range(self.n)]
        return mine, remote

    def start(self):
        mine, remote = self._copies()
        for cp in mine + remote:
            cp.start()

    def finish(self):
        mine, remote = self._copies()
        for cp in remote:
            cp.wait_recv()
        for cp in remote:
            cp.wait_send()
        for cp in mine:
            cp.wait()


def _all_gather(shards, name):
    n = len(shards)

    def body(*refs):
        plan = _gather_plan(refs[:n], refs[n:2 * n], *refs[2 * n:])
        plan.start()
        plan.forward()
        plan.finish()

    return pl.pallas_call(
        body, name=name, in_specs=[ANY] * n, out_specs=[ANY] * n, scratch_shapes=_comm_sems(n),
        out_shape=[jax.ShapeDtypeStruct((N_DEV,) + a.shape, a.dtype) for a in shards],
    )(*shards)


def _exchange(slabs, name):
    n = len(slabs)

    def body(*refs):
        plan = _exchange_plan(refs[:n], refs[n:2 * n], *refs[2 * n:])
        plan.start()
        plan.finish()

    return pl.pallas_call(
        body, name=name, in_specs=[ANY] * n, out_specs=[ANY] * n, scratch_shapes=_comm_sems(n),
        out_shape=[jax.ShapeDtypeStruct(a.shape, a.dtype) for a in slabs],
    )(*slabs)


def _adamw(w, g, m, v):
    m = ADAM_B1 * m + (1.0 - ADAM_B1) * g
    v = ADAM_B2 * v + (1.0 - ADAM_B2) * (g * g)
    m_hat = m / (1.0 - ADAM_B1 ** ADAM_STEP)
    v_hat = v / (1.0 - ADAM_B2 ** ADAM_STEP)
    return -ADAM_LR * (m_hat / (jnp.sqrt(v_hat) + ADAM_EPS) + ADAM_WD * w), m, v


def _sum_slots(ref):
    g = ref[0].astype(F32)
    for s in range(1, N_DEV):
        g = g + ref[s].astype(F32)
    return g


def _reduce_adamw(parts, w, m, v, layer, outs, name):
    l, r, c = w.shape
    tile = next(t for t in (256, 128, r) if r % t == 0)

    def body(p_ref, w_ref, m_ref, v_ref, g0, d0, m0, v0, g_ref, d_ref, nm_ref, nv_ref):
        g = _sum_slots(p_ref)
        g_ref[...] = g
        d_ref[...], nm_ref[...], nv_ref[...] = _adamw(w_ref[...], g, m_ref[...], v_ref[...])

    blk = pl.BlockSpec((None, tile, c), lambda j: (layer, j, 0))
    return pl.pallas_call(
        body, name=name, grid=(r // tile,),
        in_specs=[pl.BlockSpec((N_DEV, tile, c), lambda j: (0, j, 0)), blk, blk, blk] + [ANY] * 4, out_specs=[blk] * 4,
        out_shape=[jax.ShapeDtypeStruct((l, r, c), F32)] * 4,
        input_output_aliases={4: 0, 5: 1, 6: 2, 7: 3},
        compiler_params=_params("parallel"),
    )(parts, w, m, v, *outs)


def _sum_parts(parts, name):
    _, r, c = parts.shape

    def body(p_ref, g_ref):
        g_ref[...] = _sum_slots(p_ref)

    return pl.pallas_call(body, name=name, out_shape=jax.ShapeDtypeStruct((r, c), F32))(parts)


def _adamw_call(w, g, m, v, name):
    def body(w_ref, g_ref, m_ref, v_ref, d_ref, nm_ref, nv_ref):
        d_ref[...], nm_ref[...], nv_ref[...] = _adamw(w_ref[...], g_ref[...], m_ref[...], v_ref[...])

    return pl.pallas_call(body, name=name, out_shape=[jax.ShapeDtypeStruct(w.shape, F32)] * 3)(w, g, m, v)


ATTN = ("w_in", "w_q_up", "w_kv_up", "w_o")
FFN = ("w_gate", "w_up", "w_down")
SHARD_AXIS = dict(w_in=1, w_q_up=1, w_kv_up=1, w_o=0, w_gate=1, w_up=1, w_down=0)
SMALL = ("attn_norm", "ffn_norm", "final_norm", "out_norm_swa", "out_norm_mla", "q_norm", "kv_norm", "sinks")
PACK_W = 1024
SMALL_ROWS = 16


def _pack(arrs, dtype):
    flat = jnp.concatenate([a.astype(dtype).reshape(-1) for a in arrs])
    return flat.reshape(-1, PACK_W)


def _unpack(packed, like):
    flat = packed.reshape(-1)
    out, off = [], 0
    for a in like:
        out.append(flat[off:off + a.size].reshape(a.shape))
        off += a.size
    return out


def _gather_to_full(gathered, axis):
    shp = list(gathered.shape[1:])
    shp[axis] *= N_DEV
    return jnp.moveaxis(gathered, 0, axis).reshape(shp)


def _full_to_slabs(full, axis):
    shp = list(full.shape)
    shp[axis:axis + 1] = [N_DEV, shp[axis] // N_DEV]
    return jnp.moveaxis(full.reshape(shp), axis, 0)


class _ShardedWeights:
    def __init__(self, shards, depth):
        self.shards, self.depth = shards, depth
        self.gathered, self.pending, self.parts = {}, {}, {}
        self._gather([(n, 0) for n in ATTN], lambda xs: _all_gather(xs, "gather_attn0"))

    def _gather(self, keys, run):
        self.gathered.update(zip(keys, run([self.shards[n][l] for n, l in keys])))

    def _full(self, names, l):
        return tuple(_gather_to_full(self.gathered[n, l], SHARD_AXIS[n]) for n in names)

    def attn(self, l):
        return self._full(ATTN, l)

    def ffn(self, l):
        return self._full(FFN, l)

    def mla_fwd(self, l, q, k, v):
        keys = [(n, l) for n in FFN] + ([(n, l + 1) for n in ATTN] if l + 1 < self.depth else [])
        out = []
        self._gather(keys, lambda xs: out.extend(_mla_fwd(q, k, v, xs)) or out[2])
        return out[0], out[1]

    def _add(self, names, l, grads):
        for n, g in zip(names, grads):
            self.pending[n, l] = _full_to_slabs(g, SHARD_AXIS[n])

    def ffn_grads(self, l, *grads):
        self._add(FFN, l, grads)

    def attn_grads(self, l, *grads):
        self._add(ATTN, l, grads)

    def _exchange(self, run):
        keys = list(self.pending)
        self.parts.update(zip(keys, run([self.pending.pop(k) for k in keys])))

    def mla_bwd_dkv(self, l, *args):
        out = []
        self._exchange(lambda xs: out.extend(_mla_bwd_dkv(*args, xs)) or out[2])
        return out[0], out[1]

    def flush(self):
        self._exchange(lambda xs: _exchange(xs, "exchange_attn0"))


def kernel(x, meta_tokens, attn_norm, w_in, q_norm, w_q_up, kv_norm, w_kv_up, sinks, out_norm_swa, out_norm_mla, w_o, ffn_norm, w_gate, w_up, w_down, final_norm, loss_target, m_meta_tokens, m_attn_norm, m_w_in, m_q_norm, m_w_q_up, m_kv_norm, m_w_kv_up, m_sinks, m_out_norm_swa, m_out_norm_mla, m_w_o, m_ffn_norm, m_w_gate, m_w_up, m_w_down, m_final_norm, v_meta_tokens, v_attn_norm, v_w_in, v_q_norm, v_w_q_up, v_kv_norm, v_w_kv_up, v_sinks, v_out_norm_swa, v_out_norm_mla, v_w_o, v_ffn_norm, v_w_gate, v_w_up, v_w_down, v_final_norm):
    w = dict(meta_tokens=meta_tokens, attn_norm=attn_norm, w_in=w_in, q_norm=q_norm, w_q_up=w_q_up, kv_norm=kv_norm,
             w_kv_up=w_kv_up, sinks=sinks, out_norm_swa=out_norm_swa, out_norm_mla=out_norm_mla, w_o=w_o,
             ffn_norm=ffn_norm, w_gate=w_gate, w_up=w_up, w_down=w_down, final_norm=final_norm)
    m = dict(meta_tokens=m_meta_tokens, attn_norm=m_attn_norm, w_in=m_w_in, q_norm=m_q_norm, w_q_up=m_w_q_up,
             kv_norm=m_kv_norm, w_kv_up=m_w_kv_up, sinks=m_sinks, out_norm_swa=m_out_norm_swa,
             out_norm_mla=m_out_norm_mla, w_o=m_w_o, ffn_norm=m_ffn_norm, w_gate=m_w_gate, w_up=m_w_up,
             w_down=m_w_down, final_norm=m_final_norm)
    v = dict(meta_tokens=v_meta_tokens, attn_norm=v_attn_norm, w_in=v_w_in, q_norm=v_q_norm, w_q_up=v_w_q_up,
             kv_norm=v_kv_norm, w_kv_up=v_w_kv_up, sinks=v_sinks, out_norm_swa=v_out_norm_swa,
             out_norm_mla=v_out_norm_mla, w_o=v_w_o, ffn_norm=v_ffn_norm, w_gate=v_w_gate, w_up=v_w_up,
             w_down=v_w_down, final_norm=v_final_norm)
    names = list(w)
    big = ATTN + FFN
    depth = w_in.shape[0]
    me = _index(*_place())

    weights = _ShardedWeights({n: w[n].astype(BF16) for n in big}, depth)
    meta = jnp.moveaxis(_all_gather([meta_tokens], "gather_meta")[0], 0, 1).reshape(N_META, D_MODEL)
    loss, grad_x, d_meta, grads = _train_example(x[0], loss_target[0], meta, {n: w[n] for n in SMALL}, weights)
    weights.flush()

    g_big, d_big, m_big, v_big = {}, {}, {}, {}
    for n in big:
        outs = [lax.empty(w[n].shape, F32) for _ in range(4)]
        for l in reversed(range(depth)):
            outs = _reduce_adamw(weights.parts[n, l], w[n], m[n], v[n], l, outs, "reduce_adamw_" + n)
        g_big[n], d_big[n], m_big[n], v_big[n] = outs

    small = [grads[n] for n in SMALL] + [loss.reshape(1)]
    pad = SMALL_ROWS * PACK_W - sum(a.size for a in small)
    part = jnp.concatenate([_pack(small + [jnp.zeros((pad,), F32)], F32), d_meta], axis=0)
    total = _sum_parts(_all_gather([part], "gather_small")[0], "sum_small")
    small_w = [w[n] for n in SMALL]
    packs = [_pack([d[n] for n in SMALL] + [jnp.zeros((pad + 1,), F32)], F32) for d in (w, m, v)]
    upd = _adamw_call(packs[0], total[:SMALL_ROWS], packs[1], packs[2], "adamw_small")
    g_small, d_small, m_small, v_small = [dict(zip(SMALL, _unpack(p, small_w))) for p in (total[:SMALL_ROWS],) + tuple(upd)]
    loss_total = total[:SMALL_ROWS].reshape(-1)[SMALL_ROWS * PACK_W - pad - 1]
    g_meta = lax.dynamic_slice_in_dim(total[SMALL_ROWS:], me * LANE, LANE, axis=1)
    d_mt, m_mt, v_mt = _adamw_call(meta_tokens, g_meta, m_meta_tokens, v_meta_tokens, "adamw_meta")

    outs = []
    for got in ({**g_big, **g_small, "meta_tokens": g_meta}, {**d_big, **d_small, "meta_tokens": d_mt},
                {**m_big, **m_small, "meta_tokens": m_mt}, {**v_big, **v_small, "meta_tokens": v_mt}):
        outs += [got[n] for n in names]
    return (loss_total, grad_x[None], *outs)
```

```python
import jax
import jax.numpy as jnp
from jax import lax
from jax.experimental import pallas as pl
from jax.experimental.pallas import tpu as pltpu

F32 = jnp.float32
BF16 = jnp.bfloat16

D_MODEL = 1024
N_META = 16
BLOCK = 128
FRONT = (-N_META) % BLOCK
ROPE_THETA = 10000.0
EPS = 1e-6
NEG = -1e30
SWA_HEADS = 8
SWA_KV_HEADS = 2
SWA_GROUP = SWA_HEADS // SWA_KV_HEADS
SWA_HEAD_DIM = 64
MLA_HEADS = 8
MLA_Q_RANK = 256
MLA_KV_RANK = 128
MLA_NOPE_DIM = 64
MLA_ROPE_DIM = 32
MLA_V_DIM = 64
MLA_QK_DIM = MLA_NOPE_DIM + MLA_ROPE_DIM
SWA_Q_W = SWA_HEADS * SWA_HEAD_DIM
SWA_KV_W = SWA_KV_HEADS * SWA_HEAD_DIM
MLA_OUT_W = MLA_HEADS * MLA_V_DIM
SCALE_A = SWA_HEAD_DIM ** -0.5
SCALE_B = MLA_QK_DIM ** -0.5
LOG2E = 1.4426950408889634
Q_SCALE = SCALE_B * LOG2E
ADAM_LR = 0.001
ADAM_B1 = 0.9
ADAM_B2 = 0.999
ADAM_EPS = 1e-08
ADAM_WD = 0.01
ADAM_STEP = 10

LANE = 128
N_DEV = 8
HP = 8 * LANE
PO_QA, PO_KA, PO_VA = 0, HP, HP + 2 * LANE
PO_CQ = PO_VA + 2 * LANE
PO_CKV = PO_CQ + MLA_Q_RANK
PO_KR = PO_CKV + MLA_KV_RANK
PW_IN = PO_KR + LANE
N_TAB = 7
VMEM_LIMIT = 56 * 2 ** 20
TN_VMEM_BUDGET = 36 * 2 ** 20
MLA_HB = 4
MLA_HB_BWD = 8
HALF = LANE // 2
assert SWA_HEAD_DIM == HALF and MLA_V_DIM == HALF

NT = (((1,), (1,)), ((), ()))
TN = (((0,), (0,)), ((), ()))


def _tile(t):
    return 384 if t % 384 == 0 else 128


def _params(*sem):
    return pltpu.CompilerParams(dimension_semantics=sem, vmem_limit_bytes=VMEM_LIMIT)


def _row(tm, n):
    return pl.BlockSpec((tm, n), lambda i: (i, 0))


def _const(shape):
    return pl.BlockSpec(shape, lambda i: (0,) * len(shape))


def _dot(a, b):
    return jnp.dot(a, b, preferred_element_type=F32)


def _dot_nt(a, b):
    return lax.dot_general(a, b, NT, preferred_element_type=F32)


def _dot_tn(a, b):
    return lax.dot_general(a, b, TN, preferred_element_type=F32)


def _rope(x, c, s1, s2, shift):
    return x * c + pltpu.roll(x, LANE - shift, 1) * s1 + pltpu.roll(x, shift, 1) * s2


def _rope_t(dy, c, s1, s2, shift):
    return dy * c + pltpu.roll(dy * s1, shift, 1) + pltpu.roll(dy * s2, LANE - shift, 1)


def _rms_r(x, n):
    return lax.rsqrt(jnp.sum(x * x, axis=-1, keepdims=True) * (1.0 / n) + EPS)


def _rms_bwd(x, g, dy, n):
    r = _rms_r(x, n)
    xh = x * r
    dxh = dy * g
    dx = r * (dxh - xh * (jnp.sum(dxh * xh, axis=-1, keepdims=True) * (1.0 / n)))
    return dx, jnp.sum(dy * xh, axis=0, keepdims=True)


def _acc(ref, val, first):
    @pl.when(first)
    def _():
        ref[...] = val

    @pl.when(jnp.logical_not(first))
    def _():
        ref[...] += val


def _pack_pair(even, odd):
    return even + pltpu.roll(odd, HALF, 1)


def _pair_half(slab, half):
    return slab if half == 0 else pltpu.roll(slab, HALF, 1)


def _unpack_pair(slab, half):
    x = _pair_half(slab, half)
    return jnp.where(lax.broadcasted_iota(jnp.int32, x.shape, 1) < HALF, x, 0.0)


def _tabs(tab_ref):
    return [tab_ref[:, LANE * i:LANE * (i + 1)] for i in range(N_TAB)]


def _pre_fwd(h, g1, win, gq, wqu, gkv, wkv, tabs):
    t = h.shape[0]
    tm = _tile(t)

    def body(h_ref, g1_ref, win_ref, gq_ref, wqu_ref, gkv_ref, wkv_ref, tab_ref,
             u_ref, qa_ref, ka_ref, va_ref, cq_ref, ckv_ref, qn_ref, kvn_ref, qb_ref, kf_ref, vb_ref):
        ca, sa1, sa2, cb, sb1, sb2, ck = _tabs(tab_ref)
        hv = h_ref[...]
        u = (hv * _rms_r(hv, D_MODEL) * g1_ref[...]).astype(BF16)
        u_ref[...] = u
        p = _dot(u, win_ref[...])
        for c in range(SWA_HEADS):
            sl = slice(LANE * c, LANE * (c + 1))
            qa_ref[:, sl] = _rope(p[:, PO_QA + LANE * c:PO_QA + LANE * (c + 1)], ca, sa1, sa2, 32).astype(BF16)
        for c in range(SWA_KV_HEADS):
            sl = slice(LANE * c, LANE * (c + 1))
            ka_ref[:, sl] = _rope(p[:, PO_KA + LANE * c:PO_KA + LANE * (c + 1)], ca, sa1, sa2, 32).astype(BF16)
        va_ref[...] = p[:, PO_VA:PO_CQ].astype(BF16)
        cq = p[:, PO_CQ:PO_CKV]
        ckv = p[:, PO_CKV:PO_KR]
        cq_ref[...] = cq
        ckv_ref[...] = ckv
        qn = (cq * _rms_r(cq, MLA_Q_RANK) * gq_ref[...]).astype(BF16)
        qn_ref[...] = qn
        qb = _dot(qn, wqu_ref[...])
        kvn = (ckv * _rms_r(ckv, MLA_KV_RANK) * gkv_ref[...]).astype(BF16)
        kvn_ref[...] = kvn
        kv = _dot(kvn, wkv_ref[...])
        kr = _rope(p[:, PO_KR:PW_IN], ck, sb1, sb2, 16)
        for c in range(MLA_HEADS):
            sl = slice(LANE * c, LANE * (c + 1))
            qb_ref[:, sl] = (_rope(qb[:, sl], cb, sb1, sb2, 16) * Q_SCALE).astype(BF16)
            kf_ref[:, sl] = (kv[:, sl] + kr).astype(BF16)
        vb_ref[...] = kv[:, HP:].astype(BF16)

    widths = [(D_MODEL, BF16), (HP, BF16), (2 * LANE, BF16), (2 * LANE, BF16), (MLA_Q_RANK, F32),
              (MLA_KV_RANK, F32), (MLA_Q_RANK, BF16), (MLA_KV_RANK, BF16), (HP, BF16), (HP, BF16), (HP, BF16)]
    return pl.pallas_call(
        body, name="pre_fwd", grid=(t // tm,),
        in_specs=[_row(tm, D_MODEL), _const(g1.shape), _const(win.shape), _const(gq.shape), _const(wqu.shape),
                  _const(gkv.shape), _const(wkv.shape), _row(tm, N_TAB * LANE)],
        out_specs=[_row(tm, w) for w, _ in widths],
        out_shape=[jax.ShapeDtypeStruct((t, w), d) for w, d in widths],
        compiler_params=_params("parallel"),
    )(h, g1, win, gq, wqu, gkv, wkv, tabs)


def _swa_mask(nb):
    row = lax.broadcasted_iota(jnp.int32, (SWA_GROUP * BLOCK, 2 * BLOCK), 0) & (BLOCK - 1)
    col = lax.broadcasted_iota(jnp.int32, (SWA_GROUP * BLOCK, 2 * BLOCK), 1)
    return (col > row) & (col <= row + BLOCK) & (col + (nb - 1) * BLOCK >= FRONT)


def _swa_group(ref, rows, j):
    return jnp.concatenate([ref[rows, LANE * (SWA_GROUP * j + g):LANE * (SWA_GROUP * j + g + 1)]
                            for g in range(SWA_GROUP)], axis=0)


def _swa_packed_group(ref, rows, j):
    heads = [SWA_GROUP * j + g for g in range(SWA_GROUP)]
    return jnp.concatenate([_pair_half(ref[rows, LANE * (hd // 2):LANE * (hd // 2 + 1)], hd % 2) for hd in heads], axis=0)


def _swa_sinks(sink_ref, j):
    return jnp.concatenate([jnp.full((BLOCK, 1), sink_ref[0, SWA_GROUP * j + g], F32) for g in range(SWA_GROUP)], axis=0)


def _swa_keys(prev_ref, cur_ref, rb, j):
    sl = slice(LANE * j, LANE * (j + 1))
    if rb == 0:
        return jnp.concatenate([prev_ref[:, sl], cur_ref[:BLOCK, sl]], axis=0)
    return cur_ref[BLOCK * (rb - 1):BLOCK * (rb + 1), sl]


def _swa_chains(t):
    return [(rb, j) for rb in range(_tile(t) // BLOCK) for j in range(SWA_KV_HEADS)]


def _swa_scores(sink_ref, q_ref, kp_ref, kc_ref, n, t):
    r = _tile(t) // BLOCK
    chains = _swa_chains(t)
    qs = [_swa_group(q_ref, slice(BLOCK * rb, BLOCK * (rb + 1)), j) for rb, j in chains]
    ks = [_swa_keys(kp_ref, kc_ref, rb, j) for rb, j in chains]
    ss = [_dot_nt(q4, k2) for q4, k2 in zip(qs, ks)]
    masks = [_swa_mask(n * r + rb) for rb in range(r)]
    out = []
    for (rb, j), s in zip(chains, ss):
        sink = _swa_sinks(sink_ref, j)
        s = jnp.where(masks[rb], s * SCALE_A, NEG)
        m = jnp.maximum(jnp.max(s, axis=1, keepdims=True), sink)
        e = jnp.exp(s - m)
        es = jnp.exp(sink - m)
        inv = 1.0 / (jnp.sum(e, axis=1, keepdims=True) + es)
        out.append((e * inv, es * inv))
    return qs, ks, out


def _swa_specs(t):
    ts = _tile(t)
    r = ts // BLOCK
    prev = lambda n: (jnp.maximum(n * r - 1, 0), 0)
    cur = lambda n: (n, 0)
    return [pl.BlockSpec(memory_space=pltpu.SMEM), pl.BlockSpec((ts, HP), cur),
            pl.BlockSpec((BLOCK, 2 * LANE), prev), pl.BlockSpec((ts, 2 * LANE), cur),
            pl.BlockSpec((BLOCK, 2 * LANE), prev), pl.BlockSpec((ts, 2 * LANE), cur)]


def _swa_fwd(sinks, q, k, v):
    t = q.shape[0]
    ts = _tile(t)

    def body(sink_ref, q_ref, kp_ref, kc_ref, vp_ref, vc_ref, o_ref):
        chains = _swa_chains(t)
        _, _, probs = _swa_scores(sink_ref, q_ref, kp_ref, kc_ref, pl.program_id(0), t)
        os_ = [_dot(p.astype(BF16), _swa_keys(vp_ref, vc_ref, rb, j)) for (rb, j), (p, _) in zip(chains, probs)]
        for (rb, j), o4 in zip(chains, os_):
            for g in range(0, SWA_GROUP, 2):
                pair = (SWA_GROUP * j + g) // 2
                o_ref[BLOCK * rb:BLOCK * (rb + 1), LANE * pair:LANE * (pair + 1)] = _pack_pair(
                    o4[BLOCK * g:BLOCK * (g + 1)], o4[BLOCK * (g + 1):BLOCK * (g + 2)])

    return pl.pallas_call(
        body, name="swa_fwd", grid=(t // ts,),
        in_specs=_swa_specs(t),
        out_specs=pl.BlockSpec((ts, SWA_Q_W), lambda n: (n, 0)),
        out_shape=jax.ShapeDtypeStruct((t, SWA_Q_W), F32),
        compiler_params=_params("parallel"),
    )(sinks, q, k, k, v, v)


def _causal_mask(q0, k0, tq, tk, transposed):
    if transposed:
        key = k0 + lax.broadcasted_iota(jnp.int32, (tk, tq), 0)
        qry = q0 + lax.broadcasted_iota(jnp.int32, (tk, tq), 1)
    else:
        qry = q0 + lax.broadcasted_iota(jnp.int32, (tq, tk), 0)
        key = k0 + lax.broadcasted_iota(jnp.int32, (tq, tk), 1)
    return (key <= qry) & (key >= FRONT)


def _heads(ref, hb, rows=slice(None)):
    return [ref[rows, LANE * a:LANE * (a + 1)] for a in range(hb)]


def _as_row(col):
    return jnp.broadcast_to(col, (col.shape[0], LANE)).T[:1, :]


def _row_stats(t, hb):
    tq = _tile(t)
    return (jax.ShapeDtypeStruct((MLA_HEADS, t // tq, 1, tq), F32),
            pl.BlockSpec((hb, None, 1, tq), lambda h, i: (h, i, 0, 0)))


def _mla_fwd(q, k, v, shards=()):
    t = q.shape[0]
    tq = _tile(t)
    nq = t // tq
    n = len(shards)
    steps = (MLA_HEADS // MLA_HB) * nq

    def body(q_ref, k_ref, v_ref, *rest):
        x_refs, (o_ref, lse_ref, lser_ref), out_refs = rest[:n], rest[n:n + 3], rest[n + 3:2 * n + 3]
        acc_sc, sems = rest[2 * n + 3], rest[2 * n + 4:]
        i = pl.program_id(1)
        step_id = pl.program_id(0) * nq + i
        if n:
            plan = _gather_plan(x_refs, out_refs, *sems)
            pl.when(step_id == 0)(plan.start)
            pl.when(step_id == steps // 2)(plan.forward)
        qs = _heads(q_ref, MLA_HB)
        acc_sc[...] = jnp.zeros(acc_sc.shape, F32)

        def step(j, carry, masked):
            rows = pl.ds(pl.multiple_of(j * tq, tq), tq)
            ks, vs = _heads(k_ref, MLA_HB, rows), _heads(v_ref, MLA_HB, rows)
            ss = [_dot_nt(qh, kh) for qh, kh in zip(qs, ks)]
            if masked:
                mask = _causal_mask(i * tq, j * tq, tq, tq, False)
                ss = [jnp.where(mask, s, NEG) for s in ss]
            mid, out = [], []
            for s, (m, l) in zip(ss, carry):
                mn = jnp.maximum(m, jnp.max(s, axis=1, keepdims=True))
                al = jnp.exp2(m - mn)
                p = jnp.exp2(s - mn)
                out.append((mn, al * l + jnp.sum(p, axis=1, keepdims=True)))
                mid.append((al, p.astype(BF16)))
            for a, ((al, p), vh) in enumerate(zip(mid, vs)):
                acc_sc[a] = al * acc_sc[a] + _dot(p, vh)
            return tuple(out)

        init = ((jnp.full((tq, 1), NEG, F32), jnp.zeros((tq, 1), F32)),) * MLA_HB
        carry = lax.fori_loop(0, jnp.minimum(i, 1) + 1, lambda it, c: step(it * i, c, True), init)
        carry = lax.fori_loop(1, i, lambda j, c: step(j, c, False), carry)
        outs = [acc_sc[a] * (1.0 / l) for a, (_, l) in enumerate(carry)]
        for a in range(0, MLA_HB, 2):
            o_ref[:, HALF * a:HALF * (a + 2)] = _pack_pair(outs[a], outs[a + 1])
        for a, (m, l) in enumerate(carry):
            lse = m + jnp.log2(l)
            lse_ref[:, LANE * a:LANE * (a + 1)] = jnp.broadcast_to(lse, (tq, LANE))
            lser_ref[a] = _as_row(lse)
        if n:
            pl.when(step_id == steps - 1)(plan.finish)

    blk = pl.BlockSpec((tq, MLA_HB * LANE), lambda h, i: (i, h))
    full = pl.BlockSpec((t, MLA_HB * LANE), lambda h, i: (0, h))
    rows_shape, rows_spec = _row_stats(t, MLA_HB)
    packed = pl.BlockSpec((tq, MLA_HB * HALF), lambda h, i: (i, h))
    out = pl.pallas_call(
        body, name="mla_fwd_gather" if n else "mla_fwd", grid=(MLA_HEADS // MLA_HB, nq),
        in_specs=[blk, full, full] + [ANY] * n, out_specs=[packed, blk, rows_spec] + [ANY] * n,
        out_shape=[jax.ShapeDtypeStruct((t, MLA_OUT_W), F32), jax.ShapeDtypeStruct((t, HP), F32), rows_shape]
        + [jax.ShapeDtypeStruct((N_DEV,) + a.shape, a.dtype) for a in shards],
        scratch_shapes=[pltpu.VMEM((MLA_HB, tq, LANE), F32)] + (_comm_sems(n) if n else []),
        compiler_params=_params("arbitrary", "arbitrary"),
    )(q, k, v, *shards)
    return out[0], (out[1], out[2]), out[3:]


def _mix_fwd(h, oa, ob, ga, gb, wo, g2):
    t = h.shape[0]
    tm = _tile(t)

    def body(h_ref, oa_ref, ob_ref, ga_ref, gb_ref, wo_ref, g2_ref, h2_ref, mix_ref, u2_ref):
        oa_v = oa_ref[...]
        ob_v = ob_ref[...]
        na = (oa_v * _rms_r(oa_v, SWA_Q_W) * ga_ref[...]).astype(BF16)
        nb = (ob_v * _rms_r(ob_v, MLA_OUT_W) * gb_ref[...]).astype(BF16)
        mix_ref[:, :SWA_Q_W] = na
        mix_ref[:, SWA_Q_W:] = nb
        h2 = h_ref[...] + _dot(na, wo_ref[:SWA_Q_W, :]) + _dot(nb, wo_ref[SWA_Q_W:, :])
        h2_ref[...] = h2
        u2_ref[...] = (h2 * _rms_r(h2, D_MODEL) * g2_ref[...]).astype(BF16)

    mix_w = SWA_Q_W + MLA_OUT_W
    return pl.pallas_call(
        body, name="mix_fwd", grid=(t // tm,),
        in_specs=[_row(tm, D_MODEL), _row(tm, SWA_Q_W), _row(tm, MLA_OUT_W), _const(ga.shape), _const(gb.shape),
                  _const(wo.shape), _const(g2.shape)],
        out_specs=[_row(tm, D_MODEL), _row(tm, mix_w), _row(tm, D_MODEL)],
        out_shape=[jax.ShapeDtypeStruct((t, D_MODEL), F32), jax.ShapeDtypeStruct((t, mix_w), BF16),
                   jax.ShapeDtypeStruct((t, D_MODEL), BF16)],
        compiler_params=_params("parallel"),
    )(h, oa, ob, ga, gb, wo, g2)


def _ffn_fwd(h2, u2, wg_t, wu_t, wd):
    t = h2.shape[0]
    tm = _tile(t)
    dff = wd.shape[0]

    def body(h2_ref, u2_ref, wg_ref, wu_ref, wd_ref, h3_ref, g_ref, up_ref):
        u2v = u2_ref[...]
        g = _dot_nt(u2v, wg_ref[...])
        up = _dot_nt(u2v, wu_ref[...])
        g_ref[...] = g.astype(BF16)
        up_ref[...] = up.astype(BF16)
        a = (g * jax.nn.sigmoid(g) * up).astype(BF16)
        h3_ref[...] = h2_ref[...] + _dot(a, wd_ref[...])

    return pl.pallas_call(
        body, name="ffn_fwd", grid=(t // tm,),
        in_specs=[_row(tm, D_MODEL), _row(tm, D_MODEL), _const(wg_t.shape), _const(wu_t.shape), _const(wd.shape)],
        out_specs=[_row(tm, D_MODEL), _row(tm, dff), _row(tm, dff)],
        out_shape=[jax.ShapeDtypeStruct((t, D_MODEL), F32), jax.ShapeDtypeStruct((t, dff), BF16),
                   jax.ShapeDtypeStruct((t, dff), BF16)],
        compiler_params=_params("parallel"),
    )(h2, u2, wg_t, wu_t, wd)


def _loss_bwd(h, gf, target):
    t = h.shape[0]
    tm = _tile(t)
    first_row = FRONT + N_META

    def body(h_ref, gf_ref, t_ref, dh_ref, dgf_ref, loss_ref):
        i = pl.program_id(0)
        hv = h_ref[...]
        y = hv * _rms_r(hv, D_MODEL) * gf_ref[...]
        row = i * tm + lax.broadcasted_iota(jnp.int32, (tm, 1), 0)
        err = jnp.where(row >= first_row, y - t_ref[...], 0.0)
        dx, dg = _rms_bwd(hv, gf_ref[...], err * (1.0 / D_MODEL), D_MODEL)
        dh_ref[...] = dx
        _acc(dgf_ref, dg, i == 0)
        part = 0.5 * jnp.sum(jnp.sum(err * err, axis=1, keepdims=True) * (1.0 / D_MODEL), axis=0, keepdims=True)
        _acc(loss_ref, jnp.broadcast_to(part, (1, LANE)), i == 0)

    return pl.pallas_call(
        body, name="loss_bwd", grid=(t // tm,),
        in_specs=[_row(tm, D_MODEL), _const(gf.shape), _row(tm, D_MODEL)],
        out_specs=[_row(tm, D_MODEL), _const((1, D_MODEL)), _const((1, LANE))],
        out_shape=[jax.ShapeDtypeStruct((t, D_MODEL), F32), jax.ShapeDtypeStruct((1, D_MODEL), F32),
                   jax.ShapeDtypeStruct((1, LANE), F32)],
        compiler_params=_params("arbitrary"),
    )(h, gf, target)


def _tn_matmul(a, b, name, cols=None):
    t, n = b.shape
    first, k = cols or (0, a.shape[1])
    tk = next(c for c in (k, 1024, 512, 256, 128) if k % c == 0 and first % c == 0 and c <= 1024)
    fits = lambda c: 2 * (t * (tk + c) * 2 + tk * c * 2) <= TN_VMEM_BUDGET
    tn = next(c for c in (n, 1024, 512, 256, 128) if n % c == 0 and fits(c))

    def body(a_ref, b_ref, o_ref):
        o_ref[...] = _dot_tn(a_ref[...], b_ref[...]).astype(BF16)

    return pl.pallas_call(
        body, name=name, grid=(k // tk, n // tn),
        in_specs=[pl.BlockSpec((t, tk), lambda i, j: (0, i + first // tk)), pl.BlockSpec((t, tn), lambda i, j: (0, j))],
        out_specs=pl.BlockSpec((tk, tn), lambda i, j: (i, j)),
        out_shape=jax.ShapeDtypeStruct((k, n), BF16),
        compiler_params=_params("parallel", "parallel"),
    )(a, b)


def _ffn_bwd_a(dh3, g, up, wd):
    t = dh3.shape[0]
    tm = _tile(t)
    dff = wd.shape[0]

    def body(dh3_ref, g_ref, up_ref, wd_ref, a_ref, dgu_ref, dh3b_ref):
        dh3b = dh3_ref[...].astype(BF16)
        dh3b_ref[...] = dh3b
        da = _dot_nt(dh3b, wd_ref[...])
        gv = g_ref[...].astype(F32)
        upv = up_ref[...].astype(F32)
        sg = jax.nn.sigmoid(gv)
        silu = gv * sg
        a_ref[...] = (silu * upv).astype(BF16)
        dgu_ref[:, :dff] = (da * upv * (sg * (1.0 + gv * (1.0 - sg)))).astype(BF16)
        dgu_ref[:, dff:] = (da * silu).astype(BF16)

    return pl.pallas_call(
        body, name="ffn_bwd_a", grid=(t // tm,),
        in_specs=[_row(tm, D_MODEL), _row(tm, dff), _row(tm, dff), _const(wd.shape)],
        out_specs=[_row(tm, dff), _row(tm, 2 * dff), _row(tm, D_MODEL)],
        out_shape=[jax.ShapeDtypeStruct((t, dff), BF16), jax.ShapeDtypeStruct((t, 2 * dff), BF16),
                   jax.ShapeDtypeStruct((t, D_MODEL), BF16)],
        compiler_params=_params("parallel"),
    )(dh3, g, up, wd)


def _ffn_bwd_b(dh3, dgu, h2, g2, wg_t, wu_t):
    t = dh3.shape[0]
    tm = _tile(t)
    dff = wg_t.shape[0]

    def body(dh3_ref, dgu_ref, h2_ref, g2_ref, wg_ref, wu_ref, dh2_ref, dh2b_ref, dg2_ref):
        du2 = _dot(dgu_ref[:, :dff], wg_ref[...]) + _dot(dgu_ref[:, dff:], wu_ref[...])
        dx, dg = _rms_bwd(h2_ref[...], g2_ref[...], du2, D_MODEL)
        dh2 = dh3_ref[...] + dx
        dh2_ref[...] = dh2
        dh2b_ref[...] = dh2.astype(BF16)
        _acc(dg2_ref, dg, pl.program_id(0) == 0)

    return pl.pallas_call(
        body, name="ffn_bwd_b", grid=(t // tm,),
        in_specs=[_row(tm, D_MODEL), _row(tm, 2 * dff), _row(tm, D_MODEL), _const(g2.shape), _const(wg_t.shape),
                  _const(wu_t.shape)],
        out_specs=[_row(tm, D_MODEL), _row(tm, D_MODEL), _const((1, D_MODEL))],
        out_shape=[jax.ShapeDtypeStruct((t, D_MODEL), F32), jax.ShapeDtypeStruct((t, D_MODEL), BF16),
                   jax.ShapeDtypeStruct((1, D_MODEL), F32)],
        compiler_params=_params("arbitrary"),
    )(dh3, dgu, h2, g2, wg_t, wu_t)


def _mix_bwd(dh2, oa, ob, ga, gb, wo):
    t = dh2.shape[0]
    tm = _tile(t)

    def body(dh2_ref, oa_ref, ob_ref, ga_ref, gb_ref, wo_ref, doa_ref, dob_ref, dga_ref, dgb_ref):
        first = pl.program_id(0) == 0
        d = dh2_ref[...]
        dxa, dga = _rms_bwd(oa_ref[...], ga_ref[...], _dot_nt(d, wo_ref[:SWA_Q_W, :]), SWA_Q_W)
        dxb, dgb = _rms_bwd(ob_ref[...], gb_ref[...], _dot_nt(d, wo_ref[SWA_Q_W:, :]), MLA_OUT_W)
        for ref, dx, heads in ((doa_ref, dxa, SWA_HEADS), (dob_ref, dxb, MLA_HEADS)):
            for hd in range(heads):
                slab = dx[:, LANE * (hd // 2):LANE * (hd // 2 + 1)]
                ref[:, LANE * hd:LANE * (hd + 1)] = _unpack_pair(slab, hd % 2).astype(BF16)
        _acc(dga_ref, dga, first)
        _acc(dgb_ref, dgb, first)

    return pl.pallas_call(
        body, name="mix_bwd", grid=(t // tm,),
        in_specs=[_row(tm, D_MODEL), _row(tm, SWA_Q_W), _row(tm, MLA_OUT_W), _const(ga.shape), _const(gb.shape),
                  _const(wo.shape)],
        out_specs=[_row(tm, HP), _row(tm, HP), _const((1, SWA_Q_W)), _const((1, MLA_OUT_W))],
        out_shape=[jax.ShapeDtypeStruct((t, HP), BF16), jax.ShapeDtypeStruct((t, HP), BF16),
                   jax.ShapeDtypeStruct((1, SWA_Q_W), F32), jax.ShapeDtypeStruct((1, MLA_OUT_W), F32)],
        compiler_params=_params("arbitrary"),
    )(dh2, oa, ob, ga, gb, wo)


def _swa_bwd(sinks, q, k, v, o, do):
    t = q.shape[0]
    ts = _tile(t)

    def body(sink_ref, q_ref, kp_ref, kc_ref, vp_ref, vc_ref, o_ref, do_ref,
             dq_ref, dkc_ref, dkp_ref, dvc_ref, dvp_ref, dsink_ref):
        n = pl.program_id(0)
        chains = _swa_chains(t)
        qs, ks, probs = _swa_scores(sink_ref, q_ref, kp_ref, kc_ref, n, t)
        dos = [_swa_group(do_ref, slice(BLOCK * rb, BLOCK * (rb + 1)), j) for rb, j in chains]
        vs = [_swa_keys(vp_ref, vc_ref, rb, j) for rb, j in chains]
        dps = [_dot_nt(do4, v2) for do4, v2 in zip(dos, vs)]
        dss, dsks = [], []
        for (rb, j), (p, psink), do4, dp in zip(chains, probs, dos, dps):
            o4 = _swa_packed_group(o_ref, slice(BLOCK * rb, BLOCK * (rb + 1)), j)
            delta = jnp.sum(o4 * do4.astype(F32), axis=1, keepdims=True)
            dss.append(p * (dp - delta) * SCALE_A)
            dsks.append(-psink * delta)
        dqs = [_dot(ds.astype(BF16), k2) for ds, k2 in zip(dss, ks)]
        dks = [_dot(ds.T.astype(BF16), q4) for ds, q4 in zip(dss, qs)]
        dvs = [_dot(p.T.astype(BF16), do4) for (p, _), do4 in zip(probs, dos)]
        dsink = [jnp.zeros((1, LANE), F32)] * SWA_HEADS
        ext = {}
        for (rb, j), dq4, dk2, dv2, dsk in zip(chains, dqs, dks, dvs, dsks):
            for g in range(SWA_GROUP):
                hd = SWA_GROUP * j + g
                rows = slice(BLOCK * g, BLOCK * (g + 1))
                dq_ref[BLOCK * rb:BLOCK * (rb + 1), LANE * hd:LANE * (hd + 1)] = dq4[rows]
                dsink[hd] = dsink[hd] + jnp.sum(dsk[rows], axis=0, keepdims=True)
            for half in range(2):
                key = (j, rb + half)
                part = (dk2[BLOCK * half:BLOCK * (half + 1)], dv2[BLOCK * half:BLOCK * (half + 1)])
                ext[key] = part if key not in ext else (ext[key][0] + part[0], ext[key][1] + part[1])
        for (j, blk), (dk, dv) in ext.items():
            sl = slice(LANE * j, LANE * (j + 1))
            if blk == 0:
                dkp_ref[:, sl] = dk
                dvp_ref[:, sl] = dv
            else:
                dkc_ref[BLOCK * (blk - 1):BLOCK * blk, sl] = dk
                dvc_ref[BLOCK * (blk - 1):BLOCK * blk, sl] = dv
        for hd in range(SWA_HEADS):
            _acc(dsink_ref.at[hd:hd + 1, :], jnp.broadcast_to(dsink[hd], (1, LANE)), n == 0)

    cur = lambda n: (n, 0)
    kv = pl.BlockSpec((ts, 2 * LANE), cur)
    kvp = pl.BlockSpec((BLOCK, 2 * LANE), cur)
    hp = pl.BlockSpec((ts, HP), cur)
    kvs = jax.ShapeDtypeStruct((t, 2 * LANE), F32)
    kvps = jax.ShapeDtypeStruct((t // ts * BLOCK, 2 * LANE), F32)
    return pl.pallas_call(
        body, name="swa_bwd", grid=(t // ts,),
        in_specs=_swa_specs(t) + [pl.BlockSpec((ts, SWA_Q_W), cur), hp],
        out_specs=[hp, kv, kvp, kv, kvp, _const((SWA_HEADS, LANE))],
        out_shape=[jax.ShapeDtypeStruct((t, HP), F32), kvs, kvps, kvs, kvps,
                   jax.ShapeDtypeStruct((SWA_HEADS, LANE), F32)],
        compiler_params=_params("arbitrary"),
    )(sinks, q, k, k, v, v, o, do)


def _mla_bwd_dq(q, k, v, o, do, lse, slabs=()):
    t = q.shape[0]
    tq = _tile(t)
    nq = t // tq
    n = len(slabs)
    hb = MLA_HB_BWD
    steps = (MLA_HEADS // hb) * nq

    def body(q_ref, k_ref, v_ref, o_ref, do_ref, lse_ref, *rest):
        in_refs, (dq_ref, dl_ref), out_refs = rest[:n], rest[n:n + 2], rest[n + 2:2 * n + 2]
        dq_sc, sems = rest[2 * n + 2], rest[2 * n + 3:]
        i = pl.program_id(1)
        step_id = pl.program_id(0) * nq + i
        if n:
            plan = _exchange_plan(in_refs, out_refs, *sems)
            pl.when(step_id == 0)(plan.start)
        qs, dos = _heads(q_ref, hb), _heads(do_ref, hb)
        os_ = [_pair_half(o_ref[:, LANE * (a // 2):LANE * (a // 2 + 1)], a % 2) for a in range(hb)]
        deltas = [jnp.sum(oh * doh.astype(F32), axis=1, keepdims=True) for oh, doh in zip(os_, dos)]
        lses = [lh[:, :1] for lh in _heads(lse_ref, hb)]

        dq_sc[...] = jnp.zeros(dq_sc.shape, F32)

        def step(j, carry, masked):
            rows = pl.ds(pl.multiple_of(j * tq, tq), tq)
            ks, vs = _heads(k_ref, hb, rows), _heads(v_ref, hb, rows)
            ss = [_dot_nt(qh, kh) for qh, kh in zip(qs, ks)]
            dps = [_dot_nt(doh, vh) for doh, vh in zip(dos, vs)]
            if masked:
                mask = _causal_mask(i * tq, j * tq, tq, tq, False)
                ss = [jnp.where(mask, s, NEG) for s in ss]
            dss = [(jnp.exp2(s - lh) * (dp - dl)).astype(BF16) for s, dp, lh, dl in zip(ss, dps, lses, deltas)]
            for a, (ds, kh) in enumerate(zip(dss, ks)):
                dq_sc[a] += _dot(ds, kh)
            return carry

        lax.fori_loop(0, jnp.minimum(i, 1) + 1, lambda it, c: step(it * i, c, True), 0)
        lax.fori_loop(1, i, lambda j, c: step(j, c, False), 0)
        for a, dl in enumerate(deltas):
            dq_ref[:, LANE * a:LANE * (a + 1)] = dq_sc[a] * SCALE_B
            dl_ref[a] = _as_row(dl)
        if n:
            pl.when(step_id == steps - 1)(plan.finish)

    blk = pl.BlockSpec((tq, hb * LANE), lambda h, i: (i, h))
    full = pl.BlockSpec((t, hb * LANE), lambda h, i: (0, h))
    packed = pl.BlockSpec((tq, hb * HALF), lambda h, i: (i, h))
    rows_shape, rows_spec = _row_stats(t, hb)
    out = pl.pallas_call(
        body, name="mla_bwd_dq_exchange" if n else "mla_bwd_dq", grid=(MLA_HEADS // hb, nq),
        in_specs=[blk, full, full, packed, blk, blk] + [ANY] * n, out_specs=[blk, rows_spec] + [ANY] * n,
        out_shape=[jax.ShapeDtypeStruct((t, HP), F32), rows_shape] + [jax.ShapeDtypeStruct(a.shape, a.dtype) for a in slabs],
        scratch_shapes=[pltpu.VMEM((hb, tq, LANE), F32)] + (_comm_sems(n) if n else []),
        compiler_params=_params("arbitrary", "arbitrary"),
    )(q, k, v, o, do, lse, *slabs)
    return out[0], out[1], out[2:]


def _mla_bwd_dkv(q, k, v, do, lse_t, dl_t, slabs=()):
    t = q.shape[0]
    tq = _tile(t)
    nq = t // tq
    n = len(slabs)
    hb = MLA_HB_BWD
    steps = (MLA_HEADS // hb) * nq

    def body(k_ref, v_ref, q_ref, do_ref, lse_ref, dl_ref, *rest):
        in_refs, (dk_ref, dv_ref), out_refs = rest[:n], rest[n:n + 2], rest[n + 2:2 * n + 2]
        (dk_sc, dv_sc), sems = rest[2 * n + 2:2 * n + 4], rest[2 * n + 4:]
        group = pl.program_id(0)
        j = pl.program_id(1)
        step_id = group * nq + j
        if n:
            plan = _exchange_plan(in_refs, out_refs, *sems)
            pl.when(step_id == 0)(plan.start)
        ks, vs = _heads(k_ref, hb), _heads(v_ref, hb)

        dk_sc[...] = jnp.zeros(dk_sc.shape, F32)
        dv_sc[...] = jnp.zeros(dv_sc.shape, F32)

        def step(i, carry, masked):
            rows = pl.ds(pl.multiple_of(i * tq, tq), tq)
            qs, dos = _heads(q_ref, hb, rows), _heads(do_ref, hb, rows)
            sts = [_dot_nt(kh, qh) for kh, qh in zip(ks, qs)]
            dpts = [_dot_nt(vh, doh) for vh, doh in zip(vs, dos)]
            if masked:
                mask = _causal_mask(i * tq, j * tq, tq, tq, True)
                sts = [jnp.where(mask, st, NEG) for st in sts]
            pts = [jnp.exp2(st - lse_ref[group * hb + a, i]) for a, st in enumerate(sts)]
            dsts = [(pt * (dpt - dl_ref[group * hb + a, i])).astype(BF16) for a, (pt, dpt) in enumerate(zip(pts, dpts))]
            for a, (dst, pt, qh, doh) in enumerate(zip(dsts, pts, qs, dos)):
                dk_sc[a] += _dot(dst, qh)
                dv_sc[a] += _dot(pt.astype(BF16), doh)
            return carry

        split = jnp.where(j == 0, nq, j + 1)
        lax.fori_loop(j, split, lambda i, c: step(i, c, True), 0)
        lax.fori_loop(split, nq, lambda i, c: step(i, c, False), 0)
        for a in range(hb):
            dk_ref[:, LANE * a:LANE * (a + 1)] = dk_sc[a] * (1.0 / LOG2E)
            dv_ref[:, LANE * a:LANE * (a + 1)] = dv_sc[a]
        if n:
            pl.when(step_id == steps - 1)(plan.finish)

    blk = pl.BlockSpec((tq, hb * LANE), lambda h, j: (j, h))
    full = pl.BlockSpec((t, hb * LANE), lambda h, j: (0, h))
    rows = pl.BlockSpec((MLA_HEADS, nq, 1, tq), lambda h, j: (0, 0, 0, 0))
    out = pl.pallas_call(
        body, name="mla_bwd_dkv_exchange" if n else "mla_bwd_dkv", grid=(MLA_HEADS // hb, nq),
        in_specs=[blk, blk, full, full, rows, rows] + [ANY] * n, out_specs=[blk, blk] + [ANY] * n,
        out_shape=[jax.ShapeDtypeStruct((t, HP), F32)] * 2 + [jax.ShapeDtypeStruct(a.shape, a.dtype) for a in slabs],
        scratch_shapes=[pltpu.VMEM((hb, tq, LANE), F32)] * 2 + (_comm_sems(n) if n else []),
        compiler_params=_params("arbitrary", "arbitrary"),
    )(k, v, q, do, lse_t, dl_t, *slabs)
    return out[0], out[1], out[2:]


def _pre_bwd(dh2, h, cq, ckv, dqa, dka, dka_next, dva, dva_next, dqb, dkf, dvb, g1, win, gq, wqu, gkv, wkv, tabs):
    t = h.shape[0]
    tm = _tile(t)

    def body(dh2_ref, h_ref, cq_ref, ckv_ref, dqa_ref, dka_ref, dkan_ref, dva_ref, dvan_ref, dqb_ref, dkf_ref, dvb_ref,
             g1_ref, win_ref, gq_ref, wqu_ref, gkv_ref, wkv_ref, tab_ref,
             dh_ref, dp_ref, dqbo_ref, dkvo_ref, dg1_ref, dgq_ref, dgkv_ref):
        first = pl.program_id(0) == 0
        ca, sa1, sa2, cb, sb1, sb2, ck = _tabs(tab_ref)
        dkr = jnp.zeros((tm, LANE), F32)
        for c in range(MLA_HEADS):
            sl = slice(LANE * c, LANE * (c + 1))
            dqbo_ref[:, sl] = _rope_t(dqb_ref[:, sl], cb, sb1, sb2, 16).astype(BF16)
            dkr += dkf_ref[:, sl]
        dkvo_ref[:, :HP] = dkf_ref[...].astype(BF16)
        dkvo_ref[:, HP:] = dvb_ref[...].astype(BF16)
        dcq, dgq = _rms_bwd(cq_ref[...], gq_ref[...], _dot_nt(dqbo_ref[...], wqu_ref[...]), MLA_Q_RANK)
        dckv, dgkv = _rms_bwd(ckv_ref[...], gkv_ref[...], _dot_nt(dkvo_ref[...], wkv_ref[...]), MLA_KV_RANK)
        for c in range(SWA_HEADS):
            sl = slice(LANE * c, LANE * (c + 1))
            dp_ref[:, PO_QA + LANE * c:PO_QA + LANE * (c + 1)] = _rope_t(dqa_ref[:, sl], ca, sa1, sa2, 32).astype(BF16)
        last = slice(tm - BLOCK, tm)
        more = pl.program_id(0) < t // tm - 1
        for c in range(SWA_KV_HEADS):
            sl = slice(LANE * c, LANE * (c + 1))
            dk = dka_ref[:, sl]
            dk_last = dk[tm - BLOCK:] + jnp.where(more, dkan_ref[:, sl], 0.0)
            cols = slice(PO_KA + LANE * c, PO_KA + LANE * (c + 1))
            if tm > BLOCK:
                dp_ref[:tm - BLOCK, cols] = _rope_t(dk[:tm - BLOCK], ca[:tm - BLOCK], sa1[:tm - BLOCK], sa2[:tm - BLOCK],
                                                    32).astype(BF16)
            dp_ref[last, cols] = _rope_t(dk_last, ca[tm - BLOCK:], sa1[tm - BLOCK:], sa2[tm - BLOCK:], 32).astype(BF16)
        if tm > BLOCK:
            dp_ref[:tm - BLOCK, PO_VA:PO_CQ] = dva_ref[:tm - BLOCK, :].astype(BF16)
        dp_ref[last, PO_VA:PO_CQ] = (dva_ref[tm - BLOCK:, :] + jnp.where(more, dvan_ref[...], 0.0)).astype(BF16)
        dp_ref[:, PO_CQ:PO_CKV] = dcq.astype(BF16)
        dp_ref[:, PO_CKV:PO_KR] = dckv.astype(BF16)
        dp_ref[:, PO_KR:PW_IN] = _rope_t(dkr, ck, sb1, sb2, 16).astype(BF16)
        dx, dg1 = _rms_bwd(h_ref[...], g1_ref[...], _dot_nt(dp_ref[...], win_ref[...]), D_MODEL)
        dh_ref[...] = dh2_ref[...] + dx
        _acc(dg1_ref, dg1, first)
        _acc(dgq_ref, dgq, first)
        _acc(dgkv_ref, dgkv, first)

    kv = _row(tm, 2 * LANE)
    nxt = pl.BlockSpec((BLOCK, 2 * LANE), lambda i: (jnp.minimum(i + 1, t // tm - 1), 0))
    return pl.pallas_call(
        body, name="pre_bwd", grid=(t // tm,),
        in_specs=[_row(tm, D_MODEL), _row(tm, D_MODEL), _row(tm, MLA_Q_RANK), _row(tm, MLA_KV_RANK), _row(tm, HP),
                  kv, nxt, kv, nxt, _row(tm, HP), _row(tm, HP), _row(tm, HP),
                  _const(g1.shape), _const(win.shape), _const(gq.shape), _const(wqu.shape), _const(gkv.shape),
                  _const(wkv.shape), _row(tm, N_TAB * LANE)],
        out_specs=[_row(tm, D_MODEL), _row(tm, PW_IN), _row(tm, HP), _row(tm, 2 * HP),
                   _const((1, D_MODEL)), _const((1, MLA_Q_RANK)), _const((1, MLA_KV_RANK))],
        out_shape=[jax.ShapeDtypeStruct((t, D_MODEL), F32), jax.ShapeDtypeStruct((t, PW_IN), BF16),
                   jax.ShapeDtypeStruct((t, HP), BF16), jax.ShapeDtypeStruct((t, 2 * HP), BF16),
                   jax.ShapeDtypeStruct((1, D_MODEL), F32), jax.ShapeDtypeStruct((1, MLA_Q_RANK), F32),
                   jax.ShapeDtypeStruct((1, MLA_KV_RANK), F32)],
        compiler_params=_params("arbitrary"),
    )(dh2, h, cq, ckv, dqa, dka, dka_next, dva, dva_next, dqb, dkf, dvb, g1, win, gq, wqu, gkv, wkv, tabs)


def _rope_tables(t):
    pos = (jnp.arange(t, dtype=jnp.int32) - FRONT).astype(F32)[:, None]
    lane = jnp.arange(LANE)[None, :]

    def table(dim, start):
        half = dim // 2
        inv = ROPE_THETA ** (-jnp.arange(0, dim, 2, dtype=F32) / dim)
        ang = pos * inv[None, :]
        cos = jnp.concatenate([jnp.cos(ang)] * 2, axis=1)
        sin = jnp.concatenate([jnp.sin(ang)] * 2, axis=1)
        pad = lambda a: jnp.pad(a, ((0, 0), (start, LANE - start - dim)))
        first = (lane >= start) & (lane < start + half)
        second = (lane >= start + half) & (lane < start + dim)
        return pad(cos), jnp.where(first, -pad(sin), 0.0), jnp.where(second, pad(sin), 0.0)

    ca, sa1, sa2 = table(SWA_HEAD_DIM, 0)
    ck, sb1, sb2 = table(MLA_ROPE_DIM, MLA_NOPE_DIM)
    cb = jnp.where(lane < MLA_NOPE_DIM, 1.0, ck)
    return jnp.concatenate([ca, sa1, sa2, cb, sb1, sb2, ck], axis=1)


def _pad_heads(w, heads, dim, axis):
    shp = w.shape
    w = w.reshape(shp[:axis] + (heads, dim) + shp[axis + 1:])
    pad = [(0, 0)] * w.ndim
    pad[axis + 1] = (0, LANE - dim)
    return jnp.pad(w, pad).reshape(shp[:axis] + (heads * LANE,) + shp[axis + 1:])


def _unpad_heads(w, heads, dim, axis):
    shp = w.shape
    w = w.reshape(shp[:axis] + (heads, LANE) + shp[axis + 1:])
    w = lax.slice_in_dim(w, 0, dim, axis=axis + 1)
    return w.reshape(shp[:axis] + (heads * dim,) + shp[axis + 1:])


def _pad_layer(w_in, w_q_up, w_kv_up):
    o1 = SWA_Q_W
    o2 = o1 + SWA_KV_W
    o3 = o2 + SWA_KV_W
    o4 = o3 + MLA_Q_RANK
    o5 = o4 + MLA_KV_RANK
    kr = jnp.pad(w_in[:, o5:], ((0, 0), (MLA_NOPE_DIM, LANE - MLA_QK_DIM)))
    win = jnp.concatenate([
        _pad_heads(w_in[:, :o1], SWA_HEADS, SWA_HEAD_DIM, 1),
        _pad_heads(w_in[:, o1:o2], SWA_KV_HEADS, SWA_HEAD_DIM, 1),
        _pad_heads(w_in[:, o2:o3], SWA_KV_HEADS, SWA_HEAD_DIM, 1),
        w_in[:, o3:o5], kr], axis=1)
    wqu = _pad_heads(w_q_up, MLA_HEADS, MLA_QK_DIM, 1)
    kv = w_kv_up.reshape(MLA_KV_RANK, MLA_HEADS, MLA_NOPE_DIM + MLA_V_DIM)
    wkv = jnp.concatenate([
        _pad_heads(kv[:, :, :MLA_NOPE_DIM].reshape(MLA_KV_RANK, -1), MLA_HEADS, MLA_NOPE_DIM, 1),
        _pad_heads(kv[:, :, MLA_NOPE_DIM:].reshape(MLA_KV_RANK, -1), MLA_HEADS, MLA_V_DIM, 1)], axis=1)
    return win, wqu, wkv


def _unpad_layer(dwin, dwqu, dwkv):
    d_w_in = jnp.concatenate([
        _unpad_heads(dwin[:, PO_QA:PO_KA], SWA_HEADS, SWA_HEAD_DIM, 1),
        _unpad_heads(dwin[:, PO_KA:PO_VA], SWA_KV_HEADS, SWA_HEAD_DIM, 1),
        _unpad_heads(dwin[:, PO_VA:PO_CQ], SWA_KV_HEADS, SWA_HEAD_DIM, 1),
        dwin[:, PO_CQ:PO_KR], dwin[:, PO_KR + MLA_NOPE_DIM:PO_KR + MLA_QK_DIM]], axis=1)
    d_w_q_up = _unpad_heads(dwqu, MLA_HEADS, MLA_QK_DIM, 1)
    dk = _unpad_heads(dwkv[:, :HP], MLA_HEADS, MLA_NOPE_DIM, 1).reshape(MLA_KV_RANK, MLA_HEADS, MLA_NOPE_DIM)
    dv = _unpad_heads(dwkv[:, HP:], MLA_HEADS, MLA_V_DIM, 1).reshape(MLA_KV_RANK, MLA_HEADS, MLA_V_DIM)
    d_w_kv_up = jnp.concatenate([dk, dv], axis=2).reshape(MLA_KV_RANK, -1)
    return d_w_in, d_w_q_up, d_w_kv_up


def _train_example(x, target, meta, vec, weights):
    s = x.shape[0]
    depth = vec["attn_norm"].shape[0]
    t = FRONT + N_META + s
    assert t % BLOCK == 0
    tabs = _rope_tables(t)
    h = jnp.concatenate([jnp.zeros((FRONT, D_MODEL), F32), meta, x], axis=0)
    tgt = jnp.concatenate([jnp.zeros((FRONT + N_META, D_MODEL), F32), target], axis=0)
    row = lambda v: v[None, :]

    saved = []
    for l in range(depth):
        w_in, w_q_up, w_kv_up, wo = weights.attn(l)
        win, wqu, wkv = _pad_layer(w_in, w_q_up, w_kv_up)
        g1, gq, gkv, g2, ga, gb = (row(vec[n][l]) for n in ("attn_norm", "q_norm", "kv_norm", "ffn_norm",
                                                            "out_norm_swa", "out_norm_mla"))
        sk = row(vec["sinks"][l])
        u, qa, ka, va, cq, ckv, qn, kvn, qb, kf, vb = _pre_fwd(h, g1, win, gq, wqu, gkv, wkv, tabs)
        oa = _swa_fwd(sk, qa, ka, va)
        ob, lse = weights.mla_fwd(l, qb, kf, vb)
        h2, mix, u2 = _mix_fwd(h, oa, ob, ga, gb, wo, g2)
        wg, wu, wd = weights.ffn(l)
        h3, gt, up = _ffn_fwd(h2, u2, wg, wu, wd)
        saved.append((h, u, qa, ka, va, cq, ckv, qn, kvn, qb, kf, vb, oa, ob, lse, h2, mix, u2, gt, up,
                      win, wqu, wkv, wo, ga, gb, g1, gq, gkv, g2, sk, wg, wu, wd))
        h = h3

    dh, d_final, loss = _loss_bwd(h, row(vec["final_norm"]), tgt)

    grads = []
    for l in reversed(range(depth)):
        (h0, u, qa, ka, va, cq, ckv, qn, kvn, qb, kf, vb, oa, ob, lse, h2, mix, u2, gt, up,
         win, wqu, wkv, wo, ga, gb, g1, gq, gkv, g2, sk, wg, wu, wd) = saved[l]
        dff = wd.shape[0]
        act, dgu, dhb = _ffn_bwd_a(dh, gt, up, wd)
        weights.ffn_grads(l, _tn_matmul(dgu, u2, "dw_gate", (0, dff)), _tn_matmul(dgu, u2, "dw_up", (dff, dff)),
                          _tn_matmul(act, dhb, "dw_down"))
        dh2, dh2b, d_g2 = _ffn_bwd_b(dh, dgu, h2, g2, wg, wu)
        d_wo = _tn_matmul(mix, dh2b, "dw_o")
        doa, dob, d_ga, d_gb = _mix_bwd(dh2b, oa, ob, ga, gb, wo)
        dqa, dkc, dkp, dvc, dvp, dsink = _swa_bwd(sk, qa, ka, va, oa, doa)
        dqb, dl = weights.mla_bwd_dq(l, qb, kf, vb, ob, dob, lse[0])
        dkf, dvb = weights.mla_bwd_dkv(l, qb, kf, vb, dob, lse[1], dl)
        dh, dp, dqbo, dkvo, d_g1, d_gq, d_gkv = _pre_bwd(
            dh2, h0, cq, ckv, dqa, dkc, dkp, dvc, dvp, dqb, dkf, dvb,
            g1, win, gq, wqu, gkv, wkv, tabs)
        d_win = _tn_matmul(u, dp, "dw_in")
        d_wqu = _tn_matmul(qn, dqbo, "dw_q_up")
        d_wkv = _tn_matmul(kvn, dkvo, "dw_kv_up")
        weights.attn_grads(l, *_unpad_layer(d_win, d_wqu, d_wkv), d_wo)
        grads.append(dict(attn_norm=d_g1[0], q_norm=d_gq[0], kv_norm=d_gkv[0], sinks=dsink[:, 0], out_norm_swa=d_ga[0],
                          out_norm_mla=d_gb[0], ffn_norm=d_g2[0]))
    grads = grads[::-1]
    stacked = {k: jnp.stack([g[k] for g in grads]) for k in grads[0]}
    stacked["final_norm"] = d_final[0]
    return loss[0, 0], dh[FRONT + N_META:], dh[FRONT:FRONT + N_META], stacked


MESH = pl.DeviceIdType.MESH
ANY = pl.BlockSpec(memory_space=pl.ANY)


def _place():
    return lax.axis_index("x"), lax.axis_index("y"), lax.axis_index("c")


def _index(x, y, c):
    return 4 * x + 2 * y + c


def _comm_sems(n):
    return [pltpu.SemaphoreType.DMA((n, N_DEV - 1)), pltpu.SemaphoreType.DMA((n, N_DEV - 1)),
            pltpu.SemaphoreType.DMA((n,))]


class _gather_plan:
    def __init__(self, x_refs, out_refs, send_sems, recv_sems, local_sems):
        self.x_refs, self.out_refs = x_refs, out_refs
        self.send_sems, self.recv_sems, self.local_sems = send_sems, recv_sems, local_sems
        self.n = len(x_refs)

    def _where(self):
        x, y, c = _place()
        return (x, y, c), (x, y, 1 - c), [(1 - x, y), (x, 1 - y), (1 - x, 1 - y)], c

    def _copy(self, i, k, block, to, from_input=False):
        slot = self.out_refs[i].at[_index(*block)]
        return pltpu.make_async_remote_copy(
            src_ref=self.x_refs[i] if from_input else slot, dst_ref=slot,
            send_sem=self.send_sems.at[i, k], recv_sem=self.recv_sems.at[i, k], device_id=to, device_id_type=MESH)

    def _mine(self, i, me):
        return pltpu.make_async_copy(self.x_refs[i], self.out_refs[i].at[_index(*me)], self.local_sems.at[i])

    def _first(self, me, sibling, chips, c):
        out = [self._copy(i, 1 + j, me, (*chip, c), True) for j, chip in enumerate(chips) for i in range(self.n)]
        return out + [self._copy(i, 0, me, sibling, True) for i in range(self.n)]

    def start(self):
        me, sibling, chips, c = self._where()
        for i in range(self.n):
            self._mine(i, me).start()
        for cp in self._first(me, sibling, chips, c):
            cp.start()

    def forward(self):
        me, sibling, chips, c = self._where()
        for j, chip in enumerate(chips):
            for i in range(self.n):
                self._copy(i, 1 + j, (*chip, c), me).wait_recv()
                self._copy(i, 4 + j, (*chip, c), sibling).start()

    def finish(self):
        me, sibling, chips, c = self._where()
        for i in range(self.n):
            self._copy(i, 0, sibling, me).wait_recv()
            for j, chip in enumerate(chips):
                self._copy(i, 4 + j, (*chip, 1 - c), me).wait_recv()
        for cp in self._first(me, sibling, chips, c):
            cp.wait_send()
        for j, chip in enumerate(chips):
            for i in range(self.n):
                self._copy(i, 4 + j, (*chip, c), sibling).wait_send()
        for i in range(self.n):
            self._mine(i, me).wait()


class _exchange_plan:
    def __init__(self, in_refs, out_refs, send_sems, recv_sems, local_sems):
        self.in_refs, self.out_refs = in_refs, out_refs
        self.send_sems, self.recv_sems, self.local_sems = send_sems, recv_sems, local_sems
        self.n = len(in_refs)

    def _copies(self):
        x, y, c = _place()
        me = _index(x, y, c)
        mine = [pltpu.make_async_copy(self.in_refs[i].at[me], self.out_refs[i].at[me], self.local_sems.at[i])
                for i in range(self.n)]
        remote = []
        for k in range(1, N_DEV):
            peer = (1 - x if k & 4 else x, 1 - y if k & 2 else y, 1 - c if k & 1 else c)
            remote += [pltpu.make_async_remote_copy(
                src_ref=self.in_refs[i].at[_index(*peer)], dst_ref=self.out_refs[i].at[me],
                send_sem=self.send_sems.at[i, k - 1], recv_sem=self.recv_sems.at[i, k - 1],
                device_id=peer, device_id_type=MESH) for i in range(self.n)]
        return mine, remote

    def start(self):
        mine, remote = self._copies()
        for cp in mine + remote:
            cp.start()

    def finish(self):
        mine, remote = self._copies()
        for cp in remote:
            cp.wait_recv()
        for cp in remote:
            cp.wait_send()
        for cp in mine:
            cp.wait()


def _all_gather(shards, name):
    n = len(shards)

    def body(*refs):
        plan = _gather_plan(refs[:n], refs[n:2 * n], *refs[2 * n:])
        plan.start()
        plan.forward()
        plan.finish()

    return pl.pallas_call(
        body, name=name, in_specs=[ANY] * n, out_specs=[ANY] * n, scratch_shapes=_comm_sems(n),
        out_shape=[jax.ShapeDtypeStruct((N_DEV,) + a.shape, a.dtype) for a in shards],
    )(*shards)


def _exchange(slabs, name):
    n = len(slabs)

    def body(*refs):
        plan = _exchange_plan(refs[:n], refs[n:2 * n], *refs[2 * n:])
        plan.start()
        plan.finish()

    return pl.pallas_call(
        body, name=name, in_specs=[ANY] * n, out_specs=[ANY] * n, scratch_shapes=_comm_sems(n),
        out_shape=[jax.ShapeDtypeStruct(a.shape, a.dtype) for a in slabs],
    )(*slabs)


def _adamw(w, g, m, v):
    m = ADAM_B1 * m + (1.0 - ADAM_B1) * g
    v = ADAM_B2 * v + (1.0 - ADAM_B2) * (g * g)
    m_hat = m / (1.0 - ADAM_B1 ** ADAM_STEP)
    v_hat = v / (1.0 - ADAM_B2 ** ADAM_STEP)
    return -ADAM_LR * (m_hat / (jnp.sqrt(v_hat) + ADAM_EPS) + ADAM_WD * w), m, v


def _sum_slots(ref):
    g = ref[0].astype(F32)
    for s in range(1, N_DEV):
        g = g + ref[s].astype(F32)
    return g


def _reduce_adamw(parts, w, m, v, layer, outs, name):
    l, r, c = w.shape
    tile = next(t for t in (256, 128, r) if r % t == 0)

    def body(p_ref, w_ref, m_ref, v_ref, g0, d0, m0, v0, g_ref, d_ref, nm_ref, nv_ref):
        g = _sum_slots(p_ref)
        g_ref[...] = g
        d_ref[...], nm_ref[...], nv_ref[...] = _adamw(w_ref[...], g, m_ref[...], v_ref[...])

    blk = pl.BlockSpec((None, tile, c), lambda j: (layer, j, 0))
    return pl.pallas_call(
        body, name=name, grid=(r // tile,),
        in_specs=[pl.BlockSpec((N_DEV, tile, c), lambda j: (0, j, 0)), blk, blk, blk] + [ANY] * 4, out_specs=[blk] * 4,
        out_shape=[jax.ShapeDtypeStruct((l, r, c), F32)] * 4,
        input_output_aliases={4: 0, 5: 1, 6: 2, 7: 3},
        compiler_params=_params("parallel"),
    )(parts, w, m, v, *outs)


def _sum_parts(parts, name):
    _, r, c = parts.shape

    def body(p_ref, g_ref):
        g_ref[...] = _sum_slots(p_ref)

    return pl.pallas_call(body, name=name, out_shape=jax.ShapeDtypeStruct((r, c), F32))(parts)


def _adamw_call(w, g, m, v, name):
    def body(w_ref, g_ref, m_ref, v_ref, d_ref, nm_ref, nv_ref):
        d_ref[...], nm_ref[...], nv_ref[...] = _adamw(w_ref[...], g_ref[...], m_ref[...], v_ref[...])

    return pl.pallas_call(body, name=name, out_shape=[jax.ShapeDtypeStruct(w.shape, F32)] * 3)(w, g, m, v)


ATTN = ("w_in", "w_q_up", "w_kv_up", "w_o")
FFN = ("w_gate", "w_up", "w_down")
TRANSPOSED = ("w_gate", "w_up")
SHARD_AXIS = dict(w_in=1, w_q_up=1, w_kv_up=1, w_o=0, w_gate=0, w_up=0, w_down=0)
SMALL = ("attn_norm", "ffn_norm", "final_norm", "out_norm_swa", "out_norm_mla", "q_norm", "kv_norm", "sinks")
PACK_W = 1024
SMALL_ROWS = 16


def _pack(arrs, dtype):
    flat = jnp.concatenate([a.astype(dtype).reshape(-1) for a in arrs])
    return flat.reshape(-1, PACK_W)


def _unpack(packed, like):
    flat = packed.reshape(-1)
    out, off = [], 0
    for a in like:
        out.append(flat[off:off + a.size].reshape(a.shape))
        off += a.size
    return out


def _gather_to_full(gathered, axis):
    shp = list(gathered.shape[1:])
    shp[axis] *= N_DEV
    return jnp.moveaxis(gathered, 0, axis).reshape(shp)


def _full_to_slabs(full, axis):
    shp = list(full.shape)
    shp[axis:axis + 1] = [N_DEV, shp[axis] // N_DEV]
    return jnp.moveaxis(full.reshape(shp), axis, 0)


class _ShardedWeights:
    def __init__(self, shards, depth):
        self.shards, self.depth = shards, depth
        self.gathered, self.pending, self.parts = {}, {}, {}
        self._gather([(n, 0) for n in ATTN], lambda xs: _all_gather(xs, "gather_attn0"))

    def _gather(self, keys, run):
        self.gathered.update(zip(keys, run([self.shards[n][l] for n, l in keys])))

    def _full(self, names, l):
        return tuple(_gather_to_full(self.gathered[n, l], SHARD_AXIS[n]) for n in names)

    def attn(self, l):
        return self._full(ATTN, l)

    def ffn(self, l):
        return self._full(FFN, l)

    def mla_fwd(self, l, q, k, v):
        keys = [(n, l) for n in FFN] + ([(n, l + 1) for n in ATTN] if l + 1 < self.depth else [])
        out = []
        self._gather(keys, lambda xs: out.extend(_mla_fwd(q, k, v, xs)) or out[2])
        return out[0], out[1]

    def _add(self, names, l, grads):
        for n, g in zip(names, grads):
            self.pending[n, l] = _full_to_slabs(g, SHARD_AXIS[n])

    def ffn_grads(self, l, *grads):
        self._add(FFN, l, grads)

    def attn_grads(self, l, *grads):
        self._add(ATTN, l, grads)

    def _exchange(self, run, names=None):
        keys = [k for k in self.pending if names is None or k[0] in names]
        self.parts.update(zip(keys, run([self.pending.pop(k) for k in keys])))

    def mla_bwd_dq(self, l, *args):
        out = []
        self._exchange(lambda xs: out.extend(_mla_bwd_dq(*args, xs)) or out[2], ("w_gate", "w_up"))
        return out[0], out[1]

    def mla_bwd_dkv(self, l, *args):
        out = []
        self._exchange(lambda xs: out.extend(_mla_bwd_dkv(*args, xs)) or out[2])
        return out[0], out[1]

    def flush(self):
        self._exchange(lambda xs: _exchange(xs, "exchange_attn0"))


def kernel(x, meta_tokens, attn_norm, w_in, q_norm, w_q_up, kv_norm, w_kv_up, sinks, out_norm_swa, out_norm_mla, w_o, ffn_norm, w_gate, w_up, w_down, final_norm, loss_target, m_meta_tokens, m_attn_norm, m_w_in, m_q_norm, m_w_q_up, m_kv_norm, m_w_kv_up, m_sinks, m_out_norm_swa, m_out_norm_mla, m_w_o, m_ffn_norm, m_w_gate, m_w_up, m_w_down, m_final_norm, v_meta_tokens, v_attn_norm, v_w_in, v_q_norm, v_w_q_up, v_kv_norm, v_w_kv_up, v_sinks, v_out_norm_swa, v_out_norm_mla, v_w_o, v_ffn_norm, v_w_gate, v_w_up, v_w_down, v_final_norm):
    w = dict(meta_tokens=meta_tokens, attn_norm=attn_norm, w_in=w_in, q_norm=q_norm, w_q_up=w_q_up, kv_norm=kv_norm,
             w_kv_up=w_kv_up, sinks=sinks, out_norm_swa=out_norm_swa, out_norm_mla=out_norm_mla, w_o=w_o,
             ffn_norm=ffn_norm, w_gate=w_gate, w_up=w_up, w_down=w_down, final_norm=final_norm)
    m = dict(meta_tokens=m_meta_tokens, attn_norm=m_attn_norm, w_in=m_w_in, q_norm=m_q_norm, w_q_up=m_w_q_up,
             kv_norm=m_kv_norm, w_kv_up=m_w_kv_up, sinks=m_sinks, out_norm_swa=m_out_norm_swa,
             out_norm_mla=m_out_norm_mla, w_o=m_w_o, ffn_norm=m_ffn_norm, w_gate=m_w_gate, w_up=m_w_up,
             w_down=m_w_down, final_norm=m_final_norm)
    v = dict(meta_tokens=v_meta_tokens, attn_norm=v_attn_norm, w_in=v_w_in, q_norm=v_q_norm, w_q_up=v_w_q_up,
             kv_norm=v_kv_norm, w_kv_up=v_w_kv_up, sinks=v_sinks, out_norm_swa=v_out_norm_swa,
             out_norm_mla=v_out_norm_mla, w_o=v_w_o, ffn_norm=v_ffn_norm, w_gate=v_w_gate, w_up=v_w_up,
             w_down=v_w_down, final_norm=v_final_norm)
    names = list(w)
    big = ATTN + FFN
    depth = w_in.shape[0]
    me = _index(*_place())

    as_held = lambda n, a: jnp.swapaxes(a, 1, 2) if n in TRANSPOSED else a
    weights = _ShardedWeights({n: as_held(n, w[n]).astype(BF16) for n in big}, depth)
    meta = jnp.moveaxis(_all_gather([meta_tokens], "gather_meta")[0], 0, 1).reshape(N_META, D_MODEL)
    loss, grad_x, d_meta, grads = _train_example(x[0], loss_target[0], meta, {n: w[n] for n in SMALL}, weights)
    weights.flush()

    g_big, d_big, m_big, v_big = {}, {}, {}, {}
    for n in big:
        held = [as_held(n, a) for a in (w[n], m[n], v[n])]
        outs = [lax.empty(held[0].shape, F32) for _ in range(4)]
        for l in reversed(range(depth)):
            outs = _reduce_adamw(weights.parts[n, l], *held, l, outs, "reduce_adamw_" + n)
        g_big[n], d_big[n], m_big[n], v_big[n] = [as_held(n, a) for a in outs]

    small = [grads[n] for n in SMALL] + [loss.reshape(1)]
    pad = SMALL_ROWS * PACK_W - sum(a.size for a in small)
    part = jnp.concatenate([_pack(small + [jnp.zeros((pad,), F32)], F32), d_meta], axis=0)
    total = _sum_parts(_all_gather([part], "gather_small")[0], "sum_small")
    small_w = [w[n] for n in SMALL]
    packs = [_pack([d[n] for n in SMALL] + [jnp.zeros((pad + 1,), F32)], F32) for d in (w, m, v)]
    upd = _adamw_call(packs[0], total[:SMALL_ROWS], packs[1], packs[2], "adamw_small")
    g_small, d_small, m_small, v_small = [dict(zip(SMALL, _unpack(p, small_w))) for p in (total[:SMALL_ROWS],) + tuple(upd)]
    loss_total = total[:SMALL_ROWS].reshape(-1)[SMALL_ROWS * PACK_W - pad - 1]
    g_meta = lax.dynamic_slice_in_dim(total[SMALL_ROWS:], me * LANE, LANE, axis=1)
    d_mt, m_mt, v_mt = _adamw_call(meta_tokens, g_meta, m_meta_tokens, v_meta_tokens, "adamw_meta")

    outs = []
    for got in ({**g_big, **g_small, "meta_tokens": g_meta}, {**d_big, **d_small, "meta_tokens": d_mt},
                {**m_big, **m_small, "meta_tokens": m_mt}, {**v_big, **v_small, "meta_tokens": v_mt}):
        outs += [got[n] for n in names]
    return (loss_total, grad_x[None], *outs)
```

```python
import jax
import jax.numpy as jnp
from jax import lax
from jax.experimental import pallas as pl
from jax.experimental.pallas import tpu as pltpu

F32 = jnp.float32
BF16 = jnp.bfloat16

D_MODEL = 1024
N_META = 16
BLOCK = 128
FRONT = (-N_META) % BLOCK
ROPE_THETA = 10000.0
EPS = 1e-6
NEG = -1e30
SWA_HEADS = 8
SWA_KV_HEADS = 2
SWA_GROUP = SWA_HEADS // SWA_KV_HEADS
SWA_HEAD_DIM = 64
MLA_HEADS = 8
MLA_Q_RANK = 256
MLA_KV_RANK = 128
MLA_NOPE_DIM = 64
MLA_ROPE_DIM = 32
MLA_V_DIM = 64
MLA_QK_DIM = MLA_NOPE_DIM + MLA_ROPE_DIM
SWA_Q_W = SWA_HEADS * SWA_HEAD_DIM
SWA_KV_W = SWA_KV_HEADS * SWA_HEAD_DIM
MLA_OUT_W = MLA_HEADS * MLA_V_DIM
SCALE_A = SWA_HEAD_DIM ** -0.5
SCALE_B = MLA_QK_DIM ** -0.5
LOG2E = 1.4426950408889634
Q_SCALE = SCALE_B * LOG2E
ADAM_LR = 0.001
ADAM_B1 = 0.9
ADAM_B2 = 0.999
ADAM_EPS = 1e-08
ADAM_WD = 0.01
ADAM_STEP = 10

LANE = 128
N_DEV = 8
HP = 8 * LANE
PO_QA, PO_KA, PO_VA = 0, HP, HP + 2 * LANE
PO_CQ = PO_VA + 2 * LANE
PO_CKV = PO_CQ + MLA_Q_RANK
PO_KR = PO_CKV + MLA_KV_RANK
PW_IN = PO_KR + LANE
N_TAB = 7
VMEM_LIMIT = 56 * 2 ** 20
TN_VMEM_BUDGET = 36 * 2 ** 20
MLA_HB = 4
MLA_HB_BWD = 8
HALF = LANE // 2
assert SWA_HEAD_DIM == HALF and MLA_V_DIM == HALF

NT = (((1,), (1,)), ((), ()))
TN = (((0,), (0,)), ((), ()))


def _tile(t):
    return 384 if t % 384 == 0 else 128


def _params(*sem):
    return pltpu.CompilerParams(dimension_semantics=sem, vmem_limit_bytes=VMEM_LIMIT)


def _row(tm, n):
    return pl.BlockSpec((tm, n), lambda i: (i, 0))


def _const(shape):
    return pl.BlockSpec(shape, lambda i: (0,) * len(shape))


def _dot(a, b):
    return jnp.dot(a, b, preferred_element_type=F32)


def _dot_nt(a, b):
    return lax.dot_general(a, b, NT, preferred_element_type=F32)


def _dot_tn(a, b):
    return lax.dot_general(a, b, TN, preferred_element_type=F32)


def _rope(x, c, s1, s2, shift):
    return x * c + pltpu.roll(x, LANE - shift, 1) * s1 + pltpu.roll(x, shift, 1) * s2


def _rope_t(dy, c, s1, s2, shift):
    return dy * c + pltpu.roll(dy * s1, shift, 1) + pltpu.roll(dy * s2, LANE - shift, 1)


def _rms_r(x, n):
    return lax.rsqrt(jnp.sum(x * x, axis=-1, keepdims=True) * (1.0 / n) + EPS)


def _rms_bwd(x, g, dy, n):
    r = _rms_r(x, n)
    xh = x * r
    dxh = dy * g
    dx = r * (dxh - xh * (jnp.sum(dxh * xh, axis=-1, keepdims=True) * (1.0 / n)))
    return dx, jnp.sum(dy * xh, axis=0, keepdims=True)


def _acc(ref, val, first):
    @pl.when(first)
    def _():
        ref[...] = val

    @pl.when(jnp.logical_not(first))
    def _():
        ref[...] += val


def _pack_pair(even, odd):
    return even + pltpu.roll(odd, HALF, 1)


def _pair_half(slab, half):
    return slab if half == 0 else pltpu.roll(slab, HALF, 1)


def _unpack_pair(slab, half):
    x = _pair_half(slab, half)
    return jnp.where(lax.broadcasted_iota(jnp.int32, x.shape, 1) < HALF, x, 0.0)


def _tabs(tab_ref):
    return [tab_ref[:, LANE * i:LANE * (i + 1)] for i in range(N_TAB)]


def _pre_fwd(h, g1, win, gq, wqu, gkv, wkv, tabs):
    t = h.shape[0]
    tm = _tile(t)

    def body(h_ref, g1_ref, win_ref, gq_ref, wqu_ref, gkv_ref, wkv_ref, tab_ref,
             u_ref, qa_ref, ka_ref, va_ref, cq_ref, ckv_ref, qn_ref, kvn_ref, qb_ref, kf_ref, vb_ref):
        ca, sa1, sa2, cb, sb1, sb2, ck = _tabs(tab_ref)
        hv = h_ref[...]
        u = (hv * _rms_r(hv, D_MODEL) * g1_ref[...]).astype(BF16)
        u_ref[...] = u
        p = _dot_nt(u, win_ref[...])
        for c in range(SWA_HEADS):
            sl = slice(LANE * c, LANE * (c + 1))
            qa_ref[:, sl] = _rope(p[:, PO_QA + LANE * c:PO_QA + LANE * (c + 1)], ca, sa1, sa2, 32).astype(BF16)
        for c in range(SWA_KV_HEADS):
            sl = slice(LANE * c, LANE * (c + 1))
            ka_ref[:, sl] = _rope(p[:, PO_KA + LANE * c:PO_KA + LANE * (c + 1)], ca, sa1, sa2, 32).astype(BF16)
        va_ref[...] = p[:, PO_VA:PO_CQ].astype(BF16)
        cq = p[:, PO_CQ:PO_CKV]
        ckv = p[:, PO_CKV:PO_KR]
        cq_ref[...] = cq
        ckv_ref[...] = ckv
        qn = (cq * _rms_r(cq, MLA_Q_RANK) * gq_ref[...]).astype(BF16)
        qn_ref[...] = qn
        qb = _dot_nt(qn, wqu_ref[...])
        kvn = (ckv * _rms_r(ckv, MLA_KV_RANK) * gkv_ref[...]).astype(BF16)
        kvn_ref[...] = kvn
        kv = _dot_nt(kvn, wkv_ref[...])
        kr = _rope(p[:, PO_KR:PW_IN], ck, sb1, sb2, 16)
        for c in range(MLA_HEADS):
            sl = slice(LANE * c, LANE * (c + 1))
            qb_ref[:, sl] = (_rope(qb[:, sl], cb, sb1, sb2, 16) * Q_SCALE).astype(BF16)
            kf_ref[:, sl] = (kv[:, sl] + kr).astype(BF16)
        vb_ref[...] = kv[:, HP:].astype(BF16)

    widths = [(D_MODEL, BF16), (HP, BF16), (2 * LANE, BF16), (2 * LANE, BF16), (MLA_Q_RANK, F32),
              (MLA_KV_RANK, F32), (MLA_Q_RANK, BF16), (MLA_KV_RANK, BF16), (HP, BF16), (HP, BF16), (HP, BF16)]
    return pl.pallas_call(
        body, name="pre_fwd", grid=(t // tm,),
        in_specs=[_row(tm, D_MODEL), _const(g1.shape), _const(win.shape), _const(gq.shape), _const(wqu.shape),
                  _const(gkv.shape), _const(wkv.shape), _row(tm, N_TAB * LANE)],
        out_specs=[_row(tm, w) for w, _ in widths],
        out_shape=[jax.ShapeDtypeStruct((t, w), d) for w, d in widths],
        compiler_params=_params("parallel"),
    )(h, g1, win, gq, wqu, gkv, wkv, tabs)


def _swa_mask(nb):
    row = lax.broadcasted_iota(jnp.int32, (SWA_GROUP * BLOCK, 2 * BLOCK), 0) & (BLOCK - 1)
    col = lax.broadcasted_iota(jnp.int32, (SWA_GROUP * BLOCK, 2 * BLOCK), 1)
    return (col > row) & (col <= row + BLOCK) & (col + (nb - 1) * BLOCK >= FRONT)


def _swa_group(ref, rows, j):
    return jnp.concatenate([ref[rows, LANE * (SWA_GROUP * j + g):LANE * (SWA_GROUP * j + g + 1)]
                            for g in range(SWA_GROUP)], axis=0)


def _swa_packed_group(ref, rows, j):
    heads = [SWA_GROUP * j + g for g in range(SWA_GROUP)]
    return jnp.concatenate([_pair_half(ref[rows, LANE * (hd // 2):LANE * (hd // 2 + 1)], hd % 2) for hd in heads], axis=0)


def _swa_sinks(sink_ref, j):
    return jnp.concatenate([jnp.full((BLOCK, 1), sink_ref[0, SWA_GROUP * j + g], F32) for g in range(SWA_GROUP)], axis=0)


def _swa_keys(prev_ref, cur_ref, rb, j):
    sl = slice(LANE * j, LANE * (j + 1))
    if rb == 0:
        return jnp.concatenate([prev_ref[:, sl], cur_ref[:BLOCK, sl]], axis=0)
    return cur_ref[BLOCK * (rb - 1):BLOCK * (rb + 1), sl]


def _swa_chains(t):
    return [(rb, j) for rb in range(_tile(t) // BLOCK) for j in range(SWA_KV_HEADS)]


def _swa_scores(sink_ref, q_ref, kp_ref, kc_ref, n, t):
    r = _tile(t) // BLOCK
    chains = _swa_chains(t)
    qs = [_swa_group(q_ref, slice(BLOCK * rb, BLOCK * (rb + 1)), j) for rb, j in chains]
    ks = [_swa_keys(kp_ref, kc_ref, rb, j) for rb, j in chains]
    ss = [_dot_nt(q4, k2) for q4, k2 in zip(qs, ks)]
    masks = [_swa_mask(n * r + rb) for rb in range(r)]
    out = []
    for (rb, j), s in zip(chains, ss):
        sink = _swa_sinks(sink_ref, j)
        s = jnp.where(masks[rb], s * SCALE_A, NEG)
        m = jnp.maximum(jnp.max(s, axis=1, keepdims=True), sink)
        e = jnp.exp(s - m)
        es = jnp.exp(sink - m)
        inv = 1.0 / (jnp.sum(e, axis=1, keepdims=True) + es)
        out.append((e * inv, es * inv))
    return qs, ks, out


def _swa_specs(t):
    ts = _tile(t)
    r = ts // BLOCK
    prev = lambda n: (jnp.maximum(n * r - 1, 0), 0)
    cur = lambda n: (n, 0)
    return [pl.BlockSpec(memory_space=pltpu.SMEM), pl.BlockSpec((ts, HP), cur),
            pl.BlockSpec((BLOCK, 2 * LANE), prev), pl.BlockSpec((ts, 2 * LANE), cur),
            pl.BlockSpec((BLOCK, 2 * LANE), prev), pl.BlockSpec((ts, 2 * LANE), cur)]


def _swa_fwd(sinks, q, k, v):
    t = q.shape[0]
    ts = _tile(t)

    def body(sink_ref, q_ref, kp_ref, kc_ref, vp_ref, vc_ref, o_ref):
        chains = _swa_chains(t)
        _, _, probs = _swa_scores(sink_ref, q_ref, kp_ref, kc_ref, pl.program_id(0), t)
        os_ = [_dot(p.astype(BF16), _swa_keys(vp_ref, vc_ref, rb, j)) for (rb, j), (p, _) in zip(chains, probs)]
        for (rb, j), o4 in zip(chains, os_):
            for g in range(0, SWA_GROUP, 2):
                pair = (SWA_GROUP * j + g) // 2
                o_ref[BLOCK * rb:BLOCK * (rb + 1), LANE * pair:LANE * (pair + 1)] = _pack_pair(
                    o4[BLOCK * g:BLOCK * (g + 1)], o4[BLOCK * (g + 1):BLOCK * (g + 2)])

    return pl.pallas_call(
        body, name="swa_fwd", grid=(t // ts,),
        in_specs=_swa_specs(t),
        out_specs=pl.BlockSpec((ts, SWA_Q_W), lambda n: (n, 0)),
        out_shape=jax.ShapeDtypeStruct((t, SWA_Q_W), F32),
        compiler_params=_params("parallel"),
    )(sinks, q, k, k, v, v)


def _causal_mask(q0, k0, tq, tk, transposed):
    if transposed:
        key = k0 + lax.broadcasted_iota(jnp.int32, (tk, tq), 0)
        qry = q0 + lax.broadcasted_iota(jnp.int32, (tk, tq), 1)
    else:
        qry = q0 + lax.broadcasted_iota(jnp.int32, (tq, tk), 0)
        key = k0 + lax.broadcasted_iota(jnp.int32, (tq, tk), 1)
    return (key <= qry) & (key >= FRONT)


def _heads(ref, hb, rows=slice(None)):
    return [ref[rows, LANE * a:LANE * (a + 1)] for a in range(hb)]


def _as_row(col):
    return jnp.broadcast_to(col, (col.shape[0], LANE)).T[:1, :]


def _row_stats(t, hb):
    tq = _tile(t)
    return (jax.ShapeDtypeStruct((MLA_HEADS, t // tq, 1, tq), F32),
            pl.BlockSpec((hb, None, 1, tq), lambda h, i: (h, i, 0, 0)))


def _mla_fwd(q, k, v, shards=()):
    t = q.shape[0]
    tq = _tile(t)
    nq = t // tq
    n = len(shards)
    steps = (MLA_HEADS // MLA_HB) * nq

    def body(q_ref, k_ref, v_ref, *rest):
        x_refs, (o_ref, lse_ref, lser_ref), out_refs = rest[:n], rest[n:n + 3], rest[n + 3:2 * n + 3]
        acc_sc, sems = rest[2 * n + 3], rest[2 * n + 4:]
        i = pl.program_id(1)
        step_id = pl.program_id(0) * nq + i
        if n:
            plan = _gather_plan(x_refs, out_refs, *sems)
            pl.when(step_id == 0)(plan.start)
            pl.when(step_id == steps // 2)(plan.forward)
        qs = _heads(q_ref, MLA_HB)
        acc_sc[...] = jnp.zeros(acc_sc.shape, F32)

        def step(j, carry, masked):
            rows = pl.ds(pl.multiple_of(j * tq, tq), tq)
            ks, vs = _heads(k_ref, MLA_HB, rows), _heads(v_ref, MLA_HB, rows)
            ss = [_dot_nt(qh, kh) for qh, kh in zip(qs, ks)]
            if masked:
                mask = _causal_mask(i * tq, j * tq, tq, tq, False)
                ss = [jnp.where(mask, s, NEG) for s in ss]
            mid, out = [], []
            for s, (m, l) in zip(ss, carry):
                mn = jnp.maximum(m, jnp.max(s, axis=1, keepdims=True))
                al = jnp.exp2(m - mn)
                p = jnp.exp2(s - mn)
                out.append((mn, al * l + jnp.sum(p, axis=1, keepdims=True)))
                mid.append((al, p.astype(BF16)))
            for a, ((al, p), vh) in enumerate(zip(mid, vs)):
                acc_sc[a] = al * acc_sc[a] + _dot(p, vh)
            return tuple(out)

        init = ((jnp.full((tq, 1), NEG, F32), jnp.zeros((tq, 1), F32)),) * MLA_HB
        carry = lax.fori_loop(0, jnp.minimum(i, 1) + 1, lambda it, c: step(it * i, c, True), init)
        carry = lax.fori_loop(1, i, lambda j, c: step(j, c, False), carry)
        outs = [acc_sc[a] * (1.0 / l) for a, (_, l) in enumerate(carry)]
        for a in range(0, MLA_HB, 2):
            o_ref[:, HALF * a:HALF * (a + 2)] = _pack_pair(outs[a], outs[a + 1])
        for a, (m, l) in enumerate(carry):
            lse = m + jnp.log2(l)
            lse_ref[:, LANE * a:LANE * (a + 1)] = jnp.broadcast_to(lse, (tq, LANE))
            lser_ref[a] = _as_row(lse)
        if n:
            pl.when(step_id == steps - 1)(plan.finish)

    blk = pl.BlockSpec((tq, MLA_HB * LANE), lambda h, i: (i, h))
    full = pl.BlockSpec((t, MLA_HB * LANE), lambda h, i: (0, h))
    rows_shape, rows_spec = _row_stats(t, MLA_HB)
    packed = pl.BlockSpec((tq, MLA_HB * HALF), lambda h, i: (i, h))
    out = pl.pallas_call(
        body, name="mla_fwd_gather" if n else "mla_fwd", grid=(MLA_HEADS // MLA_HB, nq),
        in_specs=[blk, full, full] + [ANY] * n, out_specs=[packed, blk, rows_spec] + [ANY] * n,
        out_shape=[jax.ShapeDtypeStruct((t, MLA_OUT_W), F32), jax.ShapeDtypeStruct((t, HP), F32), rows_shape]
        + [jax.ShapeDtypeStruct((N_DEV,) + a.shape, a.dtype) for a in shards],
        scratch_shapes=[pltpu.VMEM((MLA_HB, tq, LANE), F32)] + (_comm_sems(n) if n else []),
        compiler_params=_params("arbitrary", "arbitrary"),
    )(q, k, v, *shards)
    return out[0], (out[1], out[2]), out[3:]


def _mix_fwd(h, oa, ob, ga, gb, wo, g2):
    t = h.shape[0]
    tm = _tile(t)

    def body(h_ref, oa_ref, ob_ref, ga_ref, gb_ref, wo_ref, g2_ref, h2_ref, mix_ref, u2_ref):
        oa_v = oa_ref[...]
        ob_v = ob_ref[...]
        na = (oa_v * _rms_r(oa_v, SWA_Q_W) * ga_ref[...]).astype(BF16)
        nb = (ob_v * _rms_r(ob_v, MLA_OUT_W) * gb_ref[...]).astype(BF16)
        mix_ref[:, :SWA_Q_W] = na
        mix_ref[:, SWA_Q_W:] = nb
        h2 = h_ref[...] + _dot(na, wo_ref[:SWA_Q_W, :]) + _dot(nb, wo_ref[SWA_Q_W:, :])
        h2_ref[...] = h2
        u2_ref[...] = (h2 * _rms_r(h2, D_MODEL) * g2_ref[...]).astype(BF16)

    mix_w = SWA_Q_W + MLA_OUT_W
    return pl.pallas_call(
        body, name="mix_fwd", grid=(t // tm,),
        in_specs=[_row(tm, D_MODEL), _row(tm, SWA_Q_W), _row(tm, MLA_OUT_W), _const(ga.shape), _const(gb.shape),
                  _const(wo.shape), _const(g2.shape)],
        out_specs=[_row(tm, D_MODEL), _row(tm, mix_w), _row(tm, D_MODEL)],
        out_shape=[jax.ShapeDtypeStruct((t, D_MODEL), F32), jax.ShapeDtypeStruct((t, mix_w), BF16),
                   jax.ShapeDtypeStruct((t, D_MODEL), BF16)],
        compiler_params=_params("parallel"),
    )(h, oa, ob, ga, gb, wo, g2)


def _ffn_fwd(h2, u2, wg_t, wu_t, wd):
    t = h2.shape[0]
    tm = _tile(t)
    dff = wd.shape[0]

    def body(h2_ref, u2_ref, wg_ref, wu_ref, wd_ref, h3_ref, g_ref, up_ref):
        u2v = u2_ref[...]
        g = _dot_nt(u2v, wg_ref[...])
        up = _dot_nt(u2v, wu_ref[...])
        g_ref[...] = g.astype(BF16)
        up_ref[...] = up.astype(BF16)
        a = (g * jax.nn.sigmoid(g) * up).astype(BF16)
        h3_ref[...] = h2_ref[...] + _dot(a, wd_ref[...])

    return pl.pallas_call(
        body, name="ffn_fwd", grid=(t // tm,),
        in_specs=[_row(tm, D_MODEL), _row(tm, D_MODEL), _const(wg_t.shape), _const(wu_t.shape), _const(wd.shape)],
        out_specs=[_row(tm, D_MODEL), _row(tm, dff), _row(tm, dff)],
        out_shape=[jax.ShapeDtypeStruct((t, D_MODEL), F32), jax.ShapeDtypeStruct((t, dff), BF16),
                   jax.ShapeDtypeStruct((t, dff), BF16)],
        compiler_params=_params("parallel"),
    )(h2, u2, wg_t, wu_t, wd)


def _loss_bwd(h, gf, target):
    t = h.shape[0]
    tm = _tile(t)
    first_row = FRONT + N_META

    def body(h_ref, gf_ref, t_ref, dh_ref, dgf_ref, loss_ref):
        i = pl.program_id(0)
        hv = h_ref[...]
        y = hv * _rms_r(hv, D_MODEL) * gf_ref[...]
        row = i * tm + lax.broadcasted_iota(jnp.int32, (tm, 1), 0)
        err = jnp.where(row >= first_row, y - t_ref[...], 0.0)
        dx, dg = _rms_bwd(hv, gf_ref[...], err * (1.0 / D_MODEL), D_MODEL)
        dh_ref[...] = dx
        _acc(dgf_ref, dg, i == 0)
        part = 0.5 * jnp.sum(jnp.sum(err * err, axis=1, keepdims=True) * (1.0 / D_MODEL), axis=0, keepdims=True)
        _acc(loss_ref, jnp.broadcast_to(part, (1, LANE)), i == 0)

    return pl.pallas_call(
        body, name="loss_bwd", grid=(t // tm,),
        in_specs=[_row(tm, D_MODEL), _const(gf.shape), _row(tm, D_MODEL)],
        out_specs=[_row(tm, D_MODEL), _const((1, D_MODEL)), _const((1, LANE))],
        out_shape=[jax.ShapeDtypeStruct((t, D_MODEL), F32), jax.ShapeDtypeStruct((1, D_MODEL), F32),
                   jax.ShapeDtypeStruct((1, LANE), F32)],
        compiler_params=_params("arbitrary"),
    )(h, gf, target)


def _tn_matmul(a, b, name, cols=None):
    t, n = b.shape
    first, k = cols or (0, a.shape[1])
    tk = next(c for c in (k, 1024, 512, 256, 128) if k % c == 0 and first % c == 0 and c <= 1024)
    fits = lambda c: 2 * (t * (tk + c) * 2 + tk * c * 2) <= TN_VMEM_BUDGET
    tn = next(c for c in (n, 1024, 512, 256, 128) if n % c == 0 and fits(c))

    def body(a_ref, b_ref, o_ref):
        o_ref[...] = _dot_tn(a_ref[...], b_ref[...]).astype(BF16)

    return pl.pallas_call(
        body, name=name, grid=(k // tk, n // tn),
        in_specs=[pl.BlockSpec((t, tk), lambda i, j: (0, i + first // tk)), pl.BlockSpec((t, tn), lambda i, j: (0, j))],
        out_specs=pl.BlockSpec((tk, tn), lambda i, j: (i, j)),
        out_shape=jax.ShapeDtypeStruct((k, n), BF16),
        compiler_params=_params("parallel", "parallel"),
    )(a, b)


def _ffn_bwd_a(dh3, g, up, wd):
    t = dh3.shape[0]
    tm = _tile(t)
    dff = wd.shape[0]

    def body(dh3_ref, g_ref, up_ref, wd_ref, a_ref, dgu_ref, dh3b_ref):
        dh3b = dh3_ref[...].astype(BF16)
        dh3b_ref[...] = dh3b
        da = _dot_nt(dh3b, wd_ref[...])
        gv = g_ref[...].astype(F32)
        upv = up_ref[...].astype(F32)
        sg = jax.nn.sigmoid(gv)
        silu = gv * sg
        a_ref[...] = (silu * upv).astype(BF16)
        dgu_ref[:, :dff] = (da * upv * (sg * (1.0 + gv * (1.0 - sg)))).astype(BF16)
        dgu_ref[:, dff:] = (da * silu).astype(BF16)

    return pl.pallas_call(
        body, name="ffn_bwd_a", grid=(t // tm,),
        in_specs=[_row(tm, D_MODEL), _row(tm, dff), _row(tm, dff), _const(wd.shape)],
        out_specs=[_row(tm, dff), _row(tm, 2 * dff), _row(tm, D_MODEL)],
        out_shape=[jax.ShapeDtypeStruct((t, dff), BF16), jax.ShapeDtypeStruct((t, 2 * dff), BF16),
                   jax.ShapeDtypeStruct((t, D_MODEL), BF16)],
        compiler_params=_params("parallel"),
    )(dh3, g, up, wd)


def _ffn_bwd_b(dh3, dgu, h2, g2, wg_t, wu_t):
    t = dh3.shape[0]
    tm = _tile(t)
    dff = wg_t.shape[0]

    def body(dh3_ref, dgu_ref, h2_ref, g2_ref, wg_ref, wu_ref, dh2_ref, dh2b_ref, dg2_ref):
        du2 = _dot(dgu_ref[:, :dff], wg_ref[...]) + _dot(dgu_ref[:, dff:], wu_ref[...])
        dx, dg = _rms_bwd(h2_ref[...], g2_ref[...], du2, D_MODEL)
        dh2 = dh3_ref[...] + dx
        dh2_ref[...] = dh2
        dh2b_ref[...] = dh2.astype(BF16)
        _acc(dg2_ref, dg, pl.program_id(0) == 0)

    return pl.pallas_call(
        body, name="ffn_bwd_b", grid=(t // tm,),
        in_specs=[_row(tm, D_MODEL), _row(tm, 2 * dff), _row(tm, D_MODEL), _const(g2.shape), _const(wg_t.shape),
                  _const(wu_t.shape)],
        out_specs=[_row(tm, D_MODEL), _row(tm, D_MODEL), _const((1, D_MODEL))],
        out_shape=[jax.ShapeDtypeStruct((t, D_MODEL), F32), jax.ShapeDtypeStruct((t, D_MODEL), BF16),
                   jax.ShapeDtypeStruct((1, D_MODEL), F32)],
        compiler_params=_params("arbitrary"),
    )(dh3, dgu, h2, g2, wg_t, wu_t)


def _mix_bwd(dh2, oa, ob, ga, gb, wo):
    t = dh2.shape[0]
    tm = _tile(t)

    def body(dh2_ref, oa_ref, ob_ref, ga_ref, gb_ref, wo_ref, doa_ref, dob_ref, dga_ref, dgb_ref):
        first = pl.program_id(0) == 0
        d = dh2_ref[...]
        dxa, dga = _rms_bwd(oa_ref[...], ga_ref[...], _dot_nt(d, wo_ref[:SWA_Q_W, :]), SWA_Q_W)
        dxb, dgb = _rms_bwd(ob_ref[...], gb_ref[...], _dot_nt(d, wo_ref[SWA_Q_W:, :]), MLA_OUT_W)
        for ref, dx, heads in ((doa_ref, dxa, SWA_HEADS), (dob_ref, dxb, MLA_HEADS)):
            for hd in range(heads):
                slab = dx[:, LANE * (hd // 2):LANE * (hd // 2 + 1)]
                ref[:, LANE * hd:LANE * (hd + 1)] = _unpack_pair(slab, hd % 2).astype(BF16)
        _acc(dga_ref, dga, first)
        _acc(dgb_ref, dgb, first)

    return pl.pallas_call(
        body, name="mix_bwd", grid=(t // tm,),
        in_specs=[_row(tm, D_MODEL), _row(tm, SWA_Q_W), _row(tm, MLA_OUT_W), _const(ga.shape), _const(gb.shape),
                  _const(wo.shape)],
        out_specs=[_row(tm, HP), _row(tm, HP), _const((1, SWA_Q_W)), _const((1, MLA_OUT_W))],
        out_shape=[jax.ShapeDtypeStruct((t, HP), BF16), jax.ShapeDtypeStruct((t, HP), BF16),
                   jax.ShapeDtypeStruct((1, SWA_Q_W), F32), jax.ShapeDtypeStruct((1, MLA_OUT_W), F32)],
        compiler_params=_params("arbitrary"),
    )(dh2, oa, ob, ga, gb, wo)


def _swa_bwd(sinks, q, k, v, o, do):
    t = q.shape[0]
    ts = _tile(t)

    def body(sink_ref, q_ref, kp_ref, kc_ref, vp_ref, vc_ref, o_ref, do_ref,
             dq_ref, dkc_ref, dkp_ref, dvc_ref, dvp_ref, dsink_ref):
        n = pl.program_id(0)
        chains = _swa_chains(t)
        qs, ks, probs = _swa_scores(sink_ref, q_ref, kp_ref, kc_ref, n, t)
        dos = [_swa_group(do_ref, slice(BLOCK * rb, BLOCK * (rb + 1)), j) for rb, j in chains]
        vs = [_swa_keys(vp_ref, vc_ref, rb, j) for rb, j in chains]
        dps = [_dot_nt(do4, v2) for do4, v2 in zip(dos, vs)]
        dss, dsks = [], []
        for (rb, j), (p, psink), do4, dp in zip(chains, probs, dos, dps):
            o4 = _swa_packed_group(o_ref, slice(BLOCK * rb, BLOCK * (rb + 1)), j)
            delta = jnp.sum(o4 * do4.astype(F32), axis=1, keepdims=True)
            dss.append(p * (dp - delta) * SCALE_A)
            dsks.append(-psink * delta)
        dqs = [_dot(ds.astype(BF16), k2) for ds, k2 in zip(dss, ks)]
        dks = [_dot(ds.T.astype(BF16), q4) for ds, q4 in zip(dss, qs)]
        dvs = [_dot(p.T.astype(BF16), do4) for (p, _), do4 in zip(probs, dos)]
        dsink = [jnp.zeros((1, LANE), F32)] * SWA_HEADS
        ext = {}
        for (rb, j), dq4, dk2, dv2, dsk in zip(chains, dqs, dks, dvs, dsks):
            for g in range(SWA_GROUP):
                hd = SWA_GROUP * j + g
                rows = slice(BLOCK * g, BLOCK * (g + 1))
                dq_ref[BLOCK * rb:BLOCK * (rb + 1), LANE * hd:LANE * (hd + 1)] = dq4[rows].astype(BF16)
                dsink[hd] = dsink[hd] + jnp.sum(dsk[rows], axis=0, keepdims=True)
            for half in range(2):
                key = (j, rb + half)
                part = (dk2[BLOCK * half:BLOCK * (half + 1)], dv2[BLOCK * half:BLOCK * (half + 1)])
                ext[key] = part if key not in ext else (ext[key][0] + part[0], ext[key][1] + part[1])
        for (j, blk), (dk, dv) in ext.items():
            sl = slice(LANE * j, LANE * (j + 1))
            if blk == 0:
                dkp_ref[:, sl] = dk
                dvp_ref[:, sl] = dv
            else:
                dkc_ref[BLOCK * (blk - 1):BLOCK * blk, sl] = dk
                dvc_ref[BLOCK * (blk - 1):BLOCK * blk, sl] = dv
        for hd in range(SWA_HEADS):
            _acc(dsink_ref.at[hd:hd + 1, :], jnp.broadcast_to(dsink[hd], (1, LANE)), n == 0)

    cur = lambda n: (n, 0)
    kv = pl.BlockSpec((ts, 2 * LANE), cur)
    kvp = pl.BlockSpec((BLOCK, 2 * LANE), cur)
    hp = pl.BlockSpec((ts, HP), cur)
    kvs = jax.ShapeDtypeStruct((t, 2 * LANE), F32)
    kvps = jax.ShapeDtypeStruct((t // ts * BLOCK, 2 * LANE), F32)
    return pl.pallas_call(
        body, name="swa_bwd", grid=(t // ts,),
        in_specs=_swa_specs(t) + [pl.BlockSpec((ts, SWA_Q_W), cur), hp],
        out_specs=[hp, kv, kvp, kv, kvp, _const((SWA_HEADS, LANE))],
        out_shape=[jax.ShapeDtypeStruct((t, HP), BF16), kvs, kvps, kvs, kvps,
                   jax.ShapeDtypeStruct((SWA_HEADS, LANE), F32)],
        compiler_params=_params("arbitrary"),
    )(sinks, q, k, k, v, v, o, do)


def _mla_bwd_dq(q, k, v, o, do, lse, slabs=()):
    t = q.shape[0]
    tq = _tile(t)
    nq = t // tq
    n = len(slabs)
    hb = MLA_HB_BWD
    steps = (MLA_HEADS // hb) * nq

    def body(q_ref, k_ref, v_ref, o_ref, do_ref, lse_ref, *rest):
        in_refs, (dq_ref, dl_ref), out_refs = rest[:n], rest[n:n + 2], rest[n + 2:2 * n + 2]
        dq_sc, sems = rest[2 * n + 2], rest[2 * n + 3:]
        i = pl.program_id(1)
        step_id = pl.program_id(0) * nq + i
        if n:
            plan = _exchange_plan(in_refs, out_refs, *sems)
            pl.when(step_id == 0)(plan.start)
        qs, dos = _heads(q_ref, hb), _heads(do_ref, hb)
        os_ = [_pair_half(o_ref[:, LANE * (a // 2):LANE * (a // 2 + 1)], a % 2) for a in range(hb)]
        deltas = [jnp.sum(oh * doh.astype(F32), axis=1, keepdims=True) for oh, doh in zip(os_, dos)]
        lses = [lh[:, :1] for lh in _heads(lse_ref, hb)]

        dq_sc[...] = jnp.zeros(dq_sc.shape, F32)

        def step(j, carry, masked):
            rows = pl.ds(pl.multiple_of(j * tq, tq), tq)
            ks, vs = _heads(k_ref, hb, rows), _heads(v_ref, hb, rows)
            ss = [_dot_nt(qh, kh) for qh, kh in zip(qs, ks)]
            dps = [_dot_nt(doh, vh) for doh, vh in zip(dos, vs)]
            if masked:
                mask = _causal_mask(i * tq, j * tq, tq, tq, False)
                ss = [jnp.where(mask, s, NEG) for s in ss]
            dss = [(jnp.exp2(s - lh) * (dp - dl)).astype(BF16) for s, dp, lh, dl in zip(ss, dps, lses, deltas)]
            for a, (ds, kh) in enumerate(zip(dss, ks)):
                dq_sc[a] += _dot(ds, kh)
            return carry

        lax.fori_loop(0, jnp.minimum(i, 1) + 1, lambda it, c: step(it * i, c, True), 0)
        lax.fori_loop(1, i, lambda j, c: step(j, c, False), 0)
        for a, dl in enumerate(deltas):
            dq_ref[:, LANE * a:LANE * (a + 1)] = (dq_sc[a] * SCALE_B).astype(BF16)
            dl_ref[a] = _as_row(dl)
        if n:
            pl.when(step_id == steps - 1)(plan.finish)

    blk = pl.BlockSpec((tq, hb * LANE), lambda h, i: (i, h))
    full = pl.BlockSpec((t, hb * LANE), lambda h, i: (0, h))
    packed = pl.BlockSpec((tq, hb * HALF), lambda h, i: (i, h))
    rows_shape, rows_spec = _row_stats(t, hb)
    out = pl.pallas_call(
        body, name="mla_bwd_dq_exchange" if n else "mla_bwd_dq", grid=(MLA_HEADS // hb, nq),
        in_specs=[blk, full, full, packed, blk, blk] + [ANY] * n, out_specs=[blk, rows_spec] + [ANY] * n,
        out_shape=[jax.ShapeDtypeStruct((t, HP), BF16), rows_shape] + [jax.ShapeDtypeStruct(a.shape, a.dtype) for a in slabs],
        scratch_shapes=[pltpu.VMEM((hb, tq, LANE), F32)] + (_comm_sems(n) if n else []),
        compiler_params=_params("arbitrary", "arbitrary"),
    )(q, k, v, o, do, lse, *slabs)
    return out[0], out[1], out[2:]


def _mla_bwd_dkv(q, k, v, do, lse_t, dl_t, slabs=()):
    t = q.shape[0]
    tq = _tile(t)
    nq = t // tq
    n = len(slabs)
    hb = MLA_HB_BWD
    steps = (MLA_HEADS // hb) * nq

    def body(k_ref, v_ref, q_ref, do_ref, lse_ref, dl_ref, *rest):
        in_refs, (dk_ref, dv_ref), out_refs = rest[:n], rest[n:n + 2], rest[n + 2:2 * n + 2]
        (dk_sc, dv_sc), sems = rest[2 * n + 2:2 * n + 4], rest[2 * n + 4:]
        group = pl.program_id(0)
        j = pl.program_id(1)
        step_id = group * nq + j
        if n:
            plan = _exchange_plan(in_refs, out_refs, *sems)
            pl.when(step_id == 0)(plan.start)
        ks, vs = _heads(k_ref, hb), _heads(v_ref, hb)

        dk_sc[...] = jnp.zeros(dk_sc.shape, F32)
        dv_sc[...] = jnp.zeros(dv_sc.shape, F32)

        def step(i, carry, masked):
            rows = pl.ds(pl.multiple_of(i * tq, tq), tq)
            qs, dos = _heads(q_ref, hb, rows), _heads(do_ref, hb, rows)
            sts = [_dot_nt(kh, qh) for kh, qh in zip(ks, qs)]
            dpts = [_dot_nt(vh, doh) for vh, doh in zip(vs, dos)]
            if masked:
                mask = _causal_mask(i * tq, j * tq, tq, tq, True)
                sts = [jnp.where(mask, st, NEG) for st in sts]
            pts = [jnp.exp2(st - lse_ref[group * hb + a, i]) for a, st in enumerate(sts)]
            dsts = [(pt * (dpt - dl_ref[group * hb + a, i])).astype(BF16) for a, (pt, dpt) in enumerate(zip(pts, dpts))]
            for a, (dst, pt, qh, doh) in enumerate(zip(dsts, pts, qs, dos)):
                dk_sc[a] += _dot(dst, qh)
                dv_sc[a] += _dot(pt.astype(BF16), doh)
            return carry

        split = jnp.where(j == 0, nq, j + 1)
        lax.fori_loop(j, split, lambda i, c: step(i, c, True), 0)
        lax.fori_loop(split, nq, lambda i, c: step(i, c, False), 0)
        for a in range(hb):
            dk_ref[:, LANE * a:LANE * (a + 1)] = (dk_sc[a] * (1.0 / LOG2E)).astype(BF16)
            dv_ref[:, LANE * a:LANE * (a + 1)] = dv_sc[a].astype(BF16)
        if n:
            pl.when(step_id == steps - 1)(plan.finish)

    blk = pl.BlockSpec((tq, hb * LANE), lambda h, j: (j, h))
    full = pl.BlockSpec((t, hb * LANE), lambda h, j: (0, h))
    rows = pl.BlockSpec((MLA_HEADS, nq, 1, tq), lambda h, j: (0, 0, 0, 0))
    out = pl.pallas_call(
        body, name="mla_bwd_dkv_exchange" if n else "mla_bwd_dkv", grid=(MLA_HEADS // hb, nq),
        in_specs=[blk, blk, full, full, rows, rows] + [ANY] * n, out_specs=[blk, blk] + [ANY] * n,
        out_shape=[jax.ShapeDtypeStruct((t, HP), BF16)] * 2 + [jax.ShapeDtypeStruct(a.shape, a.dtype) for a in slabs],
        scratch_shapes=[pltpu.VMEM((hb, tq, LANE), F32)] * 2 + (_comm_sems(n) if n else []),
        compiler_params=_params("arbitrary", "arbitrary"),
    )(k, v, q, do, lse_t, dl_t, *slabs)
    return out[0], out[1], out[2:]


def _pre_bwd(dh2, h, cq, ckv, dqa, dka, dka_next, dva, dva_next, dqb, dkf, dvb, g1, win, gq, wqu, gkv, wkv, tabs):
    t = h.shape[0]
    tm = _tile(t)

    def body(dh2_ref, h_ref, cq_ref, ckv_ref, dqa_ref, dka_ref, dkan_ref, dva_ref, dvan_ref, dqb_ref, dkf_ref, dvb_ref,
             g1_ref, win_ref, gq_ref, wqu_ref, gkv_ref, wkv_ref, tab_ref,
             dh_ref, dp_ref, dqbo_ref, dkvo_ref, dg1_ref, dgq_ref, dgkv_ref):
        first = pl.program_id(0) == 0
        ca, sa1, sa2, cb, sb1, sb2, ck = _tabs(tab_ref)
        dkr = jnp.zeros((tm, LANE), F32)
        for c in range(MLA_HEADS):
            sl = slice(LANE * c, LANE * (c + 1))
            dqbo_ref[:, sl] = _rope_t(dqb_ref[:, sl].astype(F32), cb, sb1, sb2, 16).astype(BF16)
            dkr += dkf_ref[:, sl].astype(F32)
        dkvo_ref[:, :HP] = dkf_ref[...]
        dkvo_ref[:, HP:] = dvb_ref[...]
        dcq, dgq = _rms_bwd(cq_ref[...], gq_ref[...], _dot(dqbo_ref[...], wqu_ref[...]), MLA_Q_RANK)
        dckv, dgkv = _rms_bwd(ckv_ref[...], gkv_ref[...], _dot(dkvo_ref[...], wkv_ref[...]), MLA_KV_RANK)
        for c in range(SWA_HEADS):
            sl = slice(LANE * c, LANE * (c + 1))
            dp_ref[:, PO_QA + LANE * c:PO_QA + LANE * (c + 1)] = _rope_t(dqa_ref[:, sl].astype(F32), ca, sa1, sa2,
                                                                          32).astype(BF16)
        last = slice(tm - BLOCK, tm)
        more = pl.program_id(0) < t // tm - 1
        for c in range(SWA_KV_HEADS):
            sl = slice(LANE * c, LANE * (c + 1))
            dk = dka_ref[:, sl]
            dk_last = dk[tm - BLOCK:] + jnp.where(more, dkan_ref[:, sl], 0.0)
            cols = slice(PO_KA + LANE * c, PO_KA + LANE * (c + 1))
            if tm > BLOCK:
                dp_ref[:tm - BLOCK, cols] = _rope_t(dk[:tm - BLOCK], ca[:tm - BLOCK], sa1[:tm - BLOCK], sa2[:tm - BLOCK],
                                                    32).astype(BF16)
            dp_ref[last, cols] = _rope_t(dk_last, ca[tm - BLOCK:], sa1[tm - BLOCK:], sa2[tm - BLOCK:], 32).astype(BF16)
        if tm > BLOCK:
            dp_ref[:tm - BLOCK, PO_VA:PO_CQ] = dva_ref[:tm - BLOCK, :].astype(BF16)
        dp_ref[last, PO_VA:PO_CQ] = (dva_ref[tm - BLOCK:, :] + jnp.where(more, dvan_ref[...], 0.0)).astype(BF16)
        dp_ref[:, PO_CQ:PO_CKV] = dcq.astype(BF16)
        dp_ref[:, PO_CKV:PO_KR] = dckv.astype(BF16)
        dp_ref[:, PO_KR:PW_IN] = _rope_t(dkr, ck, sb1, sb2, 16).astype(BF16)
        dx, dg1 = _rms_bwd(h_ref[...], g1_ref[...], _dot(dp_ref[...], win_ref[...]), D_MODEL)
        dh_ref[...] = dh2_ref[...] + dx
        _acc(dg1_ref, dg1, first)
        _acc(dgq_ref, dgq, first)
        _acc(dgkv_ref, dgkv, first)

    kv = _row(tm, 2 * LANE)
    nxt = pl.BlockSpec((BLOCK, 2 * LANE), lambda i: (jnp.minimum(i + 1, t // tm - 1), 0))
    return pl.pallas_call(
        body, name="pre_bwd", grid=(t // tm,),
        in_specs=[_row(tm, D_MODEL), _row(tm, D_MODEL), _row(tm, MLA_Q_RANK), _row(tm, MLA_KV_RANK), _row(tm, HP),
                  kv, nxt, kv, nxt, _row(tm, HP), _row(tm, HP), _row(tm, HP),
                  _const(g1.shape), _const(win.shape), _const(gq.shape), _const(wqu.shape), _const(gkv.shape),
                  _const(wkv.shape), _row(tm, N_TAB * LANE)],
        out_specs=[_row(tm, D_MODEL), _row(tm, PW_IN), _row(tm, HP), _row(tm, 2 * HP),
                   _const((1, D_MODEL)), _const((1, MLA_Q_RANK)), _const((1, MLA_KV_RANK))],
        out_shape=[jax.ShapeDtypeStruct((t, D_MODEL), F32), jax.ShapeDtypeStruct((t, PW_IN), BF16),
                   jax.ShapeDtypeStruct((t, HP), BF16), jax.ShapeDtypeStruct((t, 2 * HP), BF16),
                   jax.ShapeDtypeStruct((1, D_MODEL), F32), jax.ShapeDtypeStruct((1, MLA_Q_RANK), F32),
                   jax.ShapeDtypeStruct((1, MLA_KV_RANK), F32)],
        compiler_params=_params("arbitrary"),
    )(dh2, h, cq, ckv, dqa, dka, dka_next, dva, dva_next, dqb, dkf, dvb, g1, win, gq, wqu, gkv, wkv, tabs)


def _rope_tables(t):
    pos = (jnp.arange(t, dtype=jnp.int32) - FRONT).astype(F32)[:, None]
    lane = jnp.arange(LANE)[None, :]

    def table(dim, start):
        half = dim // 2
        inv = ROPE_THETA ** (-jnp.arange(0, dim, 2, dtype=F32) / dim)
        ang = pos * inv[None, :]
        cos = jnp.concatenate([jnp.cos(ang)] * 2, axis=1)
        sin = jnp.concatenate([jnp.sin(ang)] * 2, axis=1)
        pad = lambda a: jnp.pad(a, ((0, 0), (start, LANE - start - dim)))
        first = (lane >= start) & (lane < start + half)
        second = (lane >= start + half) & (lane < start + dim)
        return pad(cos), jnp.where(first, -pad(sin), 0.0), jnp.where(second, pad(sin), 0.0)

    ca, sa1, sa2 = table(SWA_HEAD_DIM, 0)
    ck, sb1, sb2 = table(MLA_ROPE_DIM, MLA_NOPE_DIM)
    cb = jnp.where(lane < MLA_NOPE_DIM, 1.0, ck)
    return jnp.concatenate([ca, sa1, sa2, cb, sb1, sb2, ck], axis=1)


def _pad_heads(w, heads, dim, axis):
    shp = w.shape
    w = w.reshape(shp[:axis] + (heads, dim) + shp[axis + 1:])
    pad = [(0, 0)] * w.ndim
    pad[axis + 1] = (0, LANE - dim)
    return jnp.pad(w, pad).reshape(shp[:axis] + (heads * LANE,) + shp[axis + 1:])


def _unpad_heads(w, heads, dim, axis):
    shp = w.shape
    w = w.reshape(shp[:axis] + (heads, LANE) + shp[axis + 1:])
    w = lax.slice_in_dim(w, 0, dim, axis=axis + 1)
    return w.reshape(shp[:axis] + (heads * dim,) + shp[axis + 1:])


def _pad_layer(w_in, w_q_up, w_kv_up):
    o1 = SWA_Q_W
    o2 = o1 + SWA_KV_W
    o3 = o2 + SWA_KV_W
    o4 = o3 + MLA_Q_RANK
    o5 = o4 + MLA_KV_RANK
    kr = jnp.pad(w_in[o5:], ((MLA_NOPE_DIM, LANE - MLA_QK_DIM), (0, 0)))
    win = jnp.concatenate([
        _pad_heads(w_in[:o1], SWA_HEADS, SWA_HEAD_DIM, 0),
        _pad_heads(w_in[o1:o2], SWA_KV_HEADS, SWA_HEAD_DIM, 0),
        _pad_heads(w_in[o2:o3], SWA_KV_HEADS, SWA_HEAD_DIM, 0),
        w_in[o3:o5], kr], axis=0)
    wqu = _pad_heads(w_q_up, MLA_HEADS, MLA_QK_DIM, 0)
    kv = w_kv_up.reshape(MLA_HEADS, MLA_NOPE_DIM + MLA_V_DIM, MLA_KV_RANK)
    wkv = jnp.concatenate([
        _pad_heads(kv[:, :MLA_NOPE_DIM].reshape(-1, MLA_KV_RANK), MLA_HEADS, MLA_NOPE_DIM, 0),
        _pad_heads(kv[:, MLA_NOPE_DIM:].reshape(-1, MLA_KV_RANK), MLA_HEADS, MLA_V_DIM, 0)], axis=0)
    return win, wqu, wkv


def _unpad_layer(dwin, dwqu, dwkv):
    d_w_in = jnp.concatenate([
        _unpad_heads(dwin[PO_QA:PO_KA], SWA_HEADS, SWA_HEAD_DIM, 0),
        _unpad_heads(dwin[PO_KA:PO_VA], SWA_KV_HEADS, SWA_HEAD_DIM, 0),
        _unpad_heads(dwin[PO_VA:PO_CQ], SWA_KV_HEADS, SWA_HEAD_DIM, 0),
        dwin[PO_CQ:PO_KR], dwin[PO_KR + MLA_NOPE_DIM:PO_KR + MLA_QK_DIM]], axis=0)
    d_w_q_up = _unpad_heads(dwqu, MLA_HEADS, MLA_QK_DIM, 0)
    dk = _unpad_heads(dwkv[:HP], MLA_HEADS, MLA_NOPE_DIM, 0).reshape(MLA_HEADS, MLA_NOPE_DIM, MLA_KV_RANK)
    dv = _unpad_heads(dwkv[HP:], MLA_HEADS, MLA_V_DIM, 0).reshape(MLA_HEADS, MLA_V_DIM, MLA_KV_RANK)
    d_w_kv_up = jnp.concatenate([dk, dv], axis=1).reshape(-1, MLA_KV_RANK)
    return d_w_in, d_w_q_up, d_w_kv_up


def _train_example(x, target, meta, vec, weights):
    s = x.shape[0]
    depth = vec["attn_norm"].shape[0]
    t = FRONT + N_META + s
    assert t % BLOCK == 0
    tabs = _rope_tables(t)
    h = jnp.concatenate([jnp.zeros((FRONT, D_MODEL), F32), meta, x], axis=0)
    tgt = jnp.concatenate([jnp.zeros((FRONT + N_META, D_MODEL), F32), target], axis=0)
    row = lambda v: v[None, :]

    saved = []
    for l in range(depth):
        win, wqu, wkv = _pad_layer(*weights.attn_in(l))
        g1, gq, gkv, g2, ga, gb = (row(vec[n][l]) for n in ("attn_norm", "q_norm", "kv_norm", "ffn_norm",
                                                            "out_norm_swa", "out_norm_mla"))
        sk = row(vec["sinks"][l])
        u, qa, ka, va, cq, ckv, qn, kvn, qb, kf, vb = _pre_fwd(h, g1, win, gq, wqu, gkv, wkv, tabs)
        oa = _swa_fwd(sk, qa, ka, va)
        ob, lse = weights.mla_fwd(l, qb, kf, vb)
        wo = weights.w_o(l)
        h2, mix, u2 = _mix_fwd(h, oa, ob, ga, gb, wo, g2)
        wg, wu, wd = weights.ffn(l)
        h3, gt, up = _ffn_fwd(h2, u2, wg, wu, wd)
        saved.append((h, u, qa, ka, va, cq, ckv, qn, kvn, qb, kf, vb, oa, ob, lse, h2, mix, u2, gt, up,
                      win, wqu, wkv, wo, ga, gb, g1, gq, gkv, g2, sk, wg, wu, wd))
        h = h3

    dh, d_final, loss = _loss_bwd(h, row(vec["final_norm"]), tgt)

    grads = []
    for l in reversed(range(depth)):
        (h0, u, qa, ka, va, cq, ckv, qn, kvn, qb, kf, vb, oa, ob, lse, h2, mix, u2, gt, up,
         win, wqu, wkv, wo, ga, gb, g1, gq, gkv, g2, sk, wg, wu, wd) = saved[l]
        dff = wd.shape[0]
        act, dgu, dhb = _ffn_bwd_a(dh, gt, up, wd)
        weights.ffn_grads(l, _tn_matmul(dgu, u2, "dw_gate", (0, dff)), _tn_matmul(dgu, u2, "dw_up", (dff, dff)),
                          _tn_matmul(act, dhb, "dw_down"))
        dh2, dh2b, d_g2 = _ffn_bwd_b(dh, dgu, h2, g2, wg, wu)
        weights.attn_grads(l, w_o=_tn_matmul(mix, dh2b, "dw_o"))
        doa, dob, d_ga, d_gb = _mix_bwd(dh2b, oa, ob, ga, gb, wo)
        dqa, dkc, dkp, dvc, dvp, dsink = _swa_bwd(sk, qa, ka, va, oa, doa)
        dqb, dl = weights.mla_bwd_dq(l, qb, kf, vb, ob, dob, lse[0])
        dkf, dvb = weights.mla_bwd_dkv(l, qb, kf, vb, dob, lse[1], dl)
        dh, dp, dqbo, dkvo, d_g1, d_gq, d_gkv = _pre_bwd(
            dh2, h0, cq, ckv, dqa, dkc, dkp, dvc, dvp, dqb, dkf, dvb,
            g1, win, gq, wqu, gkv, wkv, tabs)
        d_win = _tn_matmul(dp, u, "dw_in")
        d_wqu = _tn_matmul(dqbo, qn, "dw_q_up")
        d_wkv = _tn_matmul(dkvo, kvn, "dw_kv_up")
        weights.attn_grads(l, **dict(zip(ATTN_IN, _unpad_layer(d_win, d_wqu, d_wkv))))
        grads.append(dict(attn_norm=d_g1[0], q_norm=d_gq[0], kv_norm=d_gkv[0], sinks=dsink[:, 0], out_norm_swa=d_ga[0],
                          out_norm_mla=d_gb[0], ffn_norm=d_g2[0]))
    grads = grads[::-1]
    stacked = {k: jnp.stack([g[k] for g in grads]) for k in grads[0]}
    stacked["final_norm"] = d_final[0]
    return loss[0, 0], dh[FRONT + N_META:], dh[FRONT:FRONT + N_META], stacked


MESH = pl.DeviceIdType.MESH
ANY = pl.BlockSpec(memory_space=pl.ANY)


def _place():
    return lax.axis_index("x"), lax.axis_index("y"), lax.axis_index("c")


def _index(x, y, c):
    return 4 * x + 2 * y + c


def _comm_sems(n):
    return [pltpu.SemaphoreType.DMA((n, N_DEV - 1)), pltpu.SemaphoreType.DMA((n, N_DEV - 1)),
            pltpu.SemaphoreType.DMA((n,))]


class _gather_plan:
    def __init__(self, x_refs, out_refs, send_sems, recv_sems, local_sems):
        self.x_refs, self.out_refs = x_refs, out_refs
        self.send_sems, self.recv_sems, self.local_sems = send_sems, recv_sems, local_sems
        self.n = len(x_refs)

    def _where(self):
        x, y, c = _place()
        return (x, y, c), (x, y, 1 - c), [(1 - x, y), (x, 1 - y), (1 - x, 1 - y)], c

    def _copy(self, i, k, block, to, from_input=False):
        slot = self.out_refs[i].at[_index(*block)]
        return pltpu.make_async_remote_copy(
            src_ref=self.x_refs[i] if from_input else slot, dst_ref=slot,
            send_sem=self.send_sems.at[i, k], recv_sem=self.recv_sems.at[i, k], device_id=to, device_id_type=MESH)

    def _mine(self, i, me):
        return pltpu.make_async_copy(self.x_refs[i], self.out_refs[i].at[_index(*me)], self.local_sems.at[i])

    def _first(self, me, sibling, chips, c):
        out = [self._copy(i, 1 + j, me, (*chip, c), True) for j, chip in enumerate(chips) for i in range(self.n)]
        return out + [self._copy(i, 0, me, sibling, True) for i in range(self.n)]

    def start(self):
        me, sibling, chips, c = self._where()
        for i in range(self.n):
            self._mine(i, me).start()
        for cp in self._first(me, sibling, chips, c):
            cp.start()

    def forward(self):
        me, sibling, chips, c = self._where()
        for j, chip in enumerate(chips):
            for i in range(self.n):
                self._copy(i, 1 + j, (*chip, c), me).wait_recv()
                self._copy(i, 4 + j, (*chip, c), sibling).start()

    def finish(self):
        me, sibling, chips, c = self._where()
        for i in range(self.n):
            self._copy(i, 0, sibling, me).wait_recv()
            for j, chip in enumerate(chips):
                self._copy(i, 4 + j, (*chip, 1 - c), me).wait_recv()
        for cp in self._first(me, sibling, chips, c):
            cp.wait_send()
        for j, chip in enumerate(chips):
            for i in range(self.n):
                self._copy(i, 4 + j, (*chip, c), sibling).wait_send()
        for i in range(self.n):
            self._mine(i, me).wait()


class _exchange_plan:
    def __init__(self, in_refs, out_refs, send_sems, recv_sems, local_sems):
        self.in_refs, self.out_refs = in_refs, out_refs
        self.send_sems, self.recv_sems, self.local_sems = send_sems, recv_sems, local_sems
        self.n = len(in_refs)

    def _copies(self):
        x, y, c = _place()
        me = _index(x, y, c)
        mine = [pltpu.make_async_copy(self.in_refs[i].at[me], self.out_refs[i].at[me], self.local_sems.at[i])
                for i in range(self.n)]
        remote = []
        for k in range(1, N_DEV):
            peer = (1 - x if k & 4 else x, 1 - y if k & 2 else y, 1 - c if k & 1 else c)
            remote += [pltpu.make_async_remote_copy(
                src_ref=self.in_refs[i].at[_index(*peer)], dst_ref=self.out_refs[i].at[me],
                send_sem=self.send_sems.at[i, k - 1], recv_sem=self.recv_sems.at[i, k - 1],
                device_id=peer, device_id_type=MESH) for i in range(self.n)]
        return mine, remote

    def start(self):
        mine, remote = self._copies()
        for cp in mine + remote:
            cp.start()

    def finish(self):
        mine, remote = self._copies()
        for cp in remote:
            cp.wait_recv()
        for cp in remote:
            cp.wait_send()
        for cp in mine:
            cp.wait()


def _all_gather(shards, name):
    n = len(shards)

    def body(*refs):
        plan = _gather_plan(refs[:n], refs[n:2 * n], *refs[2 * n:])
        plan.start()
        plan.forward()
        plan.finish()

    return pl.pallas_call(
        body, name=name, in_specs=[ANY] * n, out_specs=[ANY] * n, scratch_shapes=_comm_sems(n),
        out_shape=[jax.ShapeDtypeStruct((N_DEV,) + a.shape, a.dtype) for a in shards],
    )(*shards)


def _exchange(slabs, name):
    n = len(slabs)

    def body(*refs):
        plan = _exchange_plan(refs[:n], refs[n:2 * n], *refs[2 * n:])
        plan.start()
        plan.finish()

    return pl.pallas_call(
        body, name=name, in_specs=[ANY] * n, out_specs=[ANY] * n, scratch_shapes=_comm_sems(n),
        out_shape=[jax.ShapeDtypeStruct(a.shape, a.dtype) for a in slabs],
    )(*slabs)


def _adamw(w, g, m, v):
    m = ADAM_B1 * m + (1.0 - ADAM_B1) * g
    v = ADAM_B2 * v + (1.0 - ADAM_B2) * (g * g)
    m_hat = m / (1.0 - ADAM_B1 ** ADAM_STEP)
    v_hat = v / (1.0 - ADAM_B2 ** ADAM_STEP)
    return -ADAM_LR * (m_hat / (jnp.sqrt(v_hat) + ADAM_EPS) + ADAM_WD * w), m, v


def _sum_slots(ref):
    g = ref[0].astype(F32)
    for s in range(1, N_DEV):
        g = g + ref[s].astype(F32)
    return g


def _reduce_adamw(parts, w, m, v, layer, outs, name):
    l, r, c = w.shape
    tile = next(t for t in (256, 128, r) if r % t == 0)

    def body(p_ref, w_ref, m_ref, v_ref, g0, d0, m0, v0, g_ref, d_ref, nm_ref, nv_ref):
        g = _sum_slots(p_ref)
        g_ref[...] = g
        d_ref[...], nm_ref[...], nv_ref[...] = _adamw(w_ref[...], g, m_ref[...], v_ref[...])

    blk = pl.BlockSpec((None, tile, c), lambda j: (layer, j, 0))
    return pl.pallas_call(
        body, name=name, grid=(r // tile,),
        in_specs=[pl.BlockSpec((N_DEV, tile, c), lambda j: (0, j, 0)), blk, blk, blk] + [ANY] * 4, out_specs=[blk] * 4,
        out_shape=[jax.ShapeDtypeStruct((l, r, c), F32)] * 4,
        input_output_aliases={4: 0, 5: 1, 6: 2, 7: 3},
        compiler_params=_params("parallel"),
    )(parts, w, m, v, *outs)


def _sum_parts(parts, name):
    _, r, c = parts.shape

    def body(p_ref, g_ref):
        g_ref[...] = _sum_slots(p_ref)

    return pl.pallas_call(body, name=name, out_shape=jax.ShapeDtypeStruct((r, c), F32))(parts)


def _adamw_call(w, g, m, v, name):
    def body(w_ref, g_ref, m_ref, v_ref, d_ref, nm_ref, nv_ref):
        d_ref[...], nm_ref[...], nv_ref[...] = _adamw(w_ref[...], g_ref[...], m_ref[...], v_ref[...])

    return pl.pallas_call(body, name=name, out_shape=[jax.ShapeDtypeStruct(w.shape, F32)] * 3)(w, g, m, v)


ATTN_IN = ("w_in", "w_q_up", "w_kv_up")
ATTN = ATTN_IN + ("w_o",)
FFN = ("w_gate", "w_up", "w_down")
TRANSPOSED = ("w_in", "w_q_up", "w_kv_up", "w_gate", "w_up")
SMALL = ("attn_norm", "ffn_norm", "final_norm", "out_norm_swa", "out_norm_mla", "q_norm", "kv_norm", "sinks")
PACK_W = 1024
SMALL_ROWS = 16


def _pack(arrs, dtype):
    flat = jnp.concatenate([a.astype(dtype).reshape(-1) for a in arrs])
    return flat.reshape(-1, PACK_W)


def _unpack(packed, like):
    flat = packed.reshape(-1)
    out, off = [], 0
    for a in like:
        out.append(flat[off:off + a.size].reshape(a.shape))
        off += a.size
    return out


def _gather_to_full(gathered):
    return gathered.reshape((-1,) + gathered.shape[2:])


def _full_to_slabs(full):
    return full.reshape((N_DEV, -1) + full.shape[1:])


class _ShardedWeights:
    def __init__(self, shards, depth):
        self.shards, self.depth = shards, depth
        self.gathered, self.pending, self.parts = {}, {}, {}
        self._gather([(n, 0) for n in ATTN_IN], lambda xs: _all_gather(xs, "gather_attn0"))

    def _gather(self, keys, run):
        self.gathered.update(zip(keys, run([self.shards[n][l] for n, l in keys])))

    def _full(self, names, l):
        return tuple(_gather_to_full(self.gathered[n, l]) for n in names)

    def attn_in(self, l):
        return self._full(ATTN_IN, l)

    def w_o(self, l):
        return self._full(("w_o",), l)[0]

    def ffn(self, l):
        return self._full(FFN, l)

    def mla_fwd(self, l, q, k, v):
        keys = [(n, l) for n in ("w_o",) + FFN] + ([(n, l + 1) for n in ATTN_IN] if l + 1 < self.depth else [])
        out = []
        self._gather(keys, lambda xs: out.extend(_mla_fwd(q, k, v, xs)) or out[2])
        return out[0], out[1]

    def _add(self, names, l, grads):
        for n, g in zip(names, grads):
            self.pending[n, l] = _full_to_slabs(g)

    def ffn_grads(self, l, *grads):
        self._add(FFN, l, grads)

    def attn_grads(self, l, **grads):
        self._add(list(grads), l, grads.values())

    def _exchange(self, run, names=None):
        keys = [k for k in self.pending if names is None or k[0] in names]
        self.parts.update(zip(keys, run([self.pending.pop(k) for k in keys])))

    def mla_bwd_dq(self, l, *args):
        out = []
        self._exchange(lambda xs: out.extend(_mla_bwd_dq(*args, xs)) or out[2], ("w_gate", "w_up"))
        return out[0], out[1]

    def mla_bwd_dkv(self, l, *args):
        out = []
        self._exchange(lambda xs: out.extend(_mla_bwd_dkv(*args, xs)) or out[2])
        return out[0], out[1]

    def flush(self):
        self._exchange(lambda xs: _exchange(xs, "exchange_attn0"))


def kernel(x, meta_tokens, attn_norm, w_in, q_norm, w_q_up, kv_norm, w_kv_up, sinks, out_norm_swa, out_norm_mla, w_o, ffn_norm, w_gate, w_up, w_down, final_norm, loss_target, m_meta_tokens, m_attn_norm, m_w_in, m_q_norm, m_w_q_up, m_kv_norm, m_w_kv_up, m_sinks, m_out_norm_swa, m_out_norm_mla, m_w_o, m_ffn_norm, m_w_gate, m_w_up, m_w_down, m_final_norm, v_meta_tokens, v_attn_norm, v_w_in, v_q_norm, v_w_q_up, v_kv_norm, v_w_kv_up, v_sinks, v_out_norm_swa, v_out_norm_mla, v_w_o, v_ffn_norm, v_w_gate, v_w_up, v_w_down, v_final_norm):
    w = dict(meta_tokens=meta_tokens, attn_norm=attn_norm, w_in=w_in, q_norm=q_norm, w_q_up=w_q_up, kv_norm=kv_norm,
             w_kv_up=w_kv_up, sinks=sinks, out_norm_swa=out_norm_swa, out_norm_mla=out_norm_mla, w_o=w_o,
             ffn_norm=ffn_norm, w_gate=w_gate, w_up=w_up, w_down=w_down, final_norm=final_norm)
    m = dict(meta_tokens=m_meta_tokens, attn_norm=m_attn_norm, w_in=m_w_in, q_norm=m_q_norm, w_q_up=m_w_q_up,
             kv_norm=m_kv_norm, w_kv_up=m_w_kv_up, sinks=m_sinks, out_norm_swa=m_out_norm_swa,
             out_norm_mla=m_out_norm_mla, w_o=m_w_o, ffn_norm=m_ffn_norm, w_gate=m_w_gate, w_up=m_w_up,
             w_down=m_w_down, final_norm=m_final_norm)
    v = dict(meta_tokens=v_meta_tokens, attn_norm=v_attn_norm, w_in=v_w_in, q_norm=v_q_norm, w_q_up=v_w_q_up,
             kv_norm=v_kv_norm, w_kv_up=v_w_kv_up, sinks=v_sinks, out_norm_swa=v_out_norm_swa,
             out_norm_mla=v_out_norm_mla, w_o=v_w_o, ffn_norm=v_ffn_norm, w_gate=v_w_gate, w_up=v_w_up,
             w_down=v_w_down, final_norm=v_final_norm)
    names = list(w)
    big = ATTN + FFN
    depth = w_in.shape[0]
    me = _index(*_place())

    as_held = lambda n, a: jnp.swapaxes(a, 1, 2) if n in TRANSPOSED else a
    weights = _ShardedWeights({n: as_held(n, w[n]).astype(BF16) for n in big}, depth)
    meta = jnp.moveaxis(_all_gather([meta_tokens], "gather_meta")[0], 0, 1).reshape(N_META, D_MODEL)
    loss, grad_x, d_meta, grads = _train_example(x[0], loss_target[0], meta, {n: w[n] for n in SMALL}, weights)
    weights.flush()

    g_big, d_big, m_big, v_big = {}, {}, {}, {}
    for n in big:
        held = [as_held(n, a) for a in (w[n], m[n], v[n])]
        outs = [lax.empty(held[0].shape, F32) for _ in range(4)]
        for l in reversed(range(depth)):
            outs = _reduce_adamw(weights.parts[n, l], *held, l, outs, "reduce_adamw_" + n)
        g_big[n], d_big[n], m_big[n], v_big[n] = [as_held(n, a) for a in outs]

    small = [grads[n] for n in SMALL] + [loss.reshape(1)]
    pad = SMALL_ROWS * PACK_W - sum(a.size for a in small)
    part = jnp.concatenate([_pack(small + [jnp.zeros((pad,), F32)], F32), d_meta], axis=0)
    total = _sum_parts(_all_gather([part], "gather_small")[0], "sum_small")
    small_w = [w[n] for n in SMALL]
    packs = [_pack([d[n] for n in SMALL] + [jnp.zeros((pad + 1,), F32)], F32) for d in (w, m, v)]
    upd = _adamw_call(packs[0], total[:SMALL_ROWS], packs[1], packs[2], "adamw_small")
    g_small, d_small, m_small, v_small = [dict(zip(SMALL, _unpack(p, small_w))) for p in (total[:SMALL_ROWS],) + tuple(upd)]
    loss_total = total[:SMALL_ROWS].reshape(-1)[SMALL_ROWS * PACK_W - pad - 1]
    g_meta = lax.dynamic_slice_in_dim(total[SMALL_ROWS:], me * LANE, LANE, axis=1)
    d_mt, m_mt, v_mt = _adamw_call(meta_tokens, g_meta, m_meta_tokens, v_meta_tokens, "adamw_meta")

    outs = []
    for got in ({**g_big, **g_small, "meta_tokens": g_meta}, {**d_big, **d_small, "meta_tokens": d_mt},
                {**m_big, **m_small, "meta_tokens": m_mt}, {**v_big, **v_small, "meta_tokens": v_mt}):
        outs += [got[n] for n in names]
    return (loss_total, grad_x[None], *outs)
```

```python
import jax
import jax.numpy as jnp
from jax import lax
from jax.experimental import pallas as pl
from jax.experimental.pallas import tpu as pltpu

F32 = jnp.float32
BF16 = jnp.bfloat16

D_MODEL = 1024
N_META = 16
BLOCK = 128
FRONT = (-N_META) % BLOCK
ROPE_THETA = 10000.0
EPS = 1e-6
NEG = -1e30
SWA_HEADS = 8
SWA_KV_HEADS = 2
SWA_GROUP = SWA_HEADS // SWA_KV_HEADS
SWA_HEAD_DIM = 64
MLA_HEADS = 8
MLA_Q_RANK = 256
MLA_KV_RANK = 128
MLA_NOPE_DIM = 64
MLA_ROPE_DIM = 32
MLA_V_DIM = 64
MLA_QK_DIM = MLA_NOPE_DIM + MLA_ROPE_DIM
SWA_Q_W = SWA_HEADS * SWA_HEAD_DIM
SWA_KV_W = SWA_KV_HEADS * SWA_HEAD_DIM
MLA_OUT_W = MLA_HEADS * MLA_V_DIM
SCALE_A = SWA_HEAD_DIM ** -0.5
SCALE_B = MLA_QK_DIM ** -0.5
LOG2E = 1.4426950408889634
Q_SCALE = SCALE_B * LOG2E
ADAM_LR = 0.001
ADAM_B1 = 0.9
ADAM_B2 = 0.999
ADAM_EPS = 1e-08
ADAM_WD = 0.01
ADAM_STEP = 10

LANE = 128
N_DEV = 8
HP = 8 * LANE
PO_QA, PO_KA, PO_VA = 0, HP, HP + 2 * LANE
PO_CQ = PO_VA + 2 * LANE
PO_CKV = PO_CQ + MLA_Q_RANK
PO_KR = PO_CKV + MLA_KV_RANK
PW_IN = PO_KR + LANE
N_TAB = 7
VMEM_LIMIT = 56 * 2 ** 20
TN_VMEM_BUDGET = 36 * 2 ** 20
MLA_HB = 4
HALF = LANE // 2
assert SWA_HEAD_DIM == HALF and MLA_V_DIM == HALF

NT = (((1,), (1,)), ((), ()))
TN = (((0,), (0,)), ((), ()))


def _tile(t):
    return 384 if t % 384 == 0 else 128


def _params(*sem):
    return pltpu.CompilerParams(dimension_semantics=sem, vmem_limit_bytes=VMEM_LIMIT)


def _row(tm, n):
    return pl.BlockSpec((tm, n), lambda i: (i, 0))


def _const(shape):
    return pl.BlockSpec(shape, lambda i: (0,) * len(shape))


def _dot(a, b):
    return jnp.dot(a, b, preferred_element_type=F32)


def _dot_nt(a, b):
    return lax.dot_general(a, b, NT, preferred_element_type=F32)


def _dot_tn(a, b):
    return lax.dot_general(a, b, TN, preferred_element_type=F32)


def _rope(x, c, s1, s2, shift):
    return x * c + pltpu.roll(x, LANE - shift, 1) * s1 + pltpu.roll(x, shift, 1) * s2


def _rope_t(dy, c, s1, s2, shift):
    return dy * c + pltpu.roll(dy * s1, shift, 1) + pltpu.roll(dy * s2, LANE - shift, 1)


def _rms_r(x, n):
    return lax.rsqrt(jnp.sum(x * x, axis=-1, keepdims=True) * (1.0 / n) + EPS)


def _rms_bwd(x, g, dy, n):
    r = _rms_r(x, n)
    xh = x * r
    dxh = dy * g
    dx = r * (dxh - xh * (jnp.sum(dxh * xh, axis=-1, keepdims=True) * (1.0 / n)))
    return dx, jnp.sum(dy * xh, axis=0, keepdims=True)


def _acc(ref, val, first):
    @pl.when(first)
    def _():
        ref[...] = val

    @pl.when(jnp.logical_not(first))
    def _():
        ref[...] += val


def _pack_pair(even, odd):
    return even + pltpu.roll(odd, HALF, 1)


def _pair_half(slab, half):
    return slab if half == 0 else pltpu.roll(slab, HALF, 1)


def _unpack_pair(slab, half):
    x = _pair_half(slab, half)
    return jnp.where(lax.broadcasted_iota(jnp.int32, x.shape, 1) < HALF, x, 0.0)


def _tabs(tab_ref):
    return [tab_ref[:, LANE * i:LANE * (i + 1)] for i in range(N_TAB)]


def _pre_fwd(h, g1, win, gq, wqu, gkv, wkv, tabs):
    t = h.shape[0]
    tm = _tile(t)

    def body(h_ref, g1_ref, win_ref, gq_ref, wqu_ref, gkv_ref, wkv_ref, tab_ref,
             u_ref, qa_ref, ka_ref, va_ref, cq_ref, ckv_ref, qn_ref, kvn_ref, qb_ref, kf_ref, vb_ref):
        ca, sa1, sa2, cb, sb1, sb2, ck = _tabs(tab_ref)
        hv = h_ref[...]
        u = (hv * _rms_r(hv, D_MODEL) * g1_ref[...]).astype(BF16)
        u_ref[...] = u
        p = _dot_nt(u, win_ref[...])
        for c in range(SWA_HEADS):
            sl = slice(LANE * c, LANE * (c + 1))
            qa_ref[:, sl] = _rope(p[:, PO_QA + LANE * c:PO_QA + LANE * (c + 1)], ca, sa1, sa2, 32).astype(BF16)
        for c in range(SWA_KV_HEADS):
            sl = slice(LANE * c, LANE * (c + 1))
            ka_ref[:, sl] = _rope(p[:, PO_KA + LANE * c:PO_KA + LANE * (c + 1)], ca, sa1, sa2, 32).astype(BF16)
        va_ref[...] = p[:, PO_VA:PO_CQ].astype(BF16)
        cq = p[:, PO_CQ:PO_CKV]
        ckv = p[:, PO_CKV:PO_KR]
        cq_ref[...] = cq
        ckv_ref[...] = ckv
        qn = (cq * _rms_r(cq, MLA_Q_RANK) * gq_ref[...]).astype(BF16)
        qn_ref[...] = qn
        qb = _dot_nt(qn, wqu_ref[...])
        kvn = (ckv * _rms_r(ckv, MLA_KV_RANK) * gkv_ref[...]).astype(BF16)
        kvn_ref[...] = kvn
        kv = _dot_nt(kvn, wkv_ref[...])
        kr = _rope(p[:, PO_KR:PW_IN], ck, sb1, sb2, 16)
        for c in range(MLA_HEADS):
            sl = slice(LANE * c, LANE * (c + 1))
            qb_ref[:, sl] = (_rope(qb[:, sl], cb, sb1, sb2, 16) * Q_SCALE).astype(BF16)
            kf_ref[:, sl] = (kv[:, sl] + kr).astype(BF16)
        vb_ref[...] = kv[:, HP:].astype(BF16)

    widths = [(D_MODEL, BF16), (HP, BF16), (2 * LANE, BF16), (2 * LANE, BF16), (MLA_Q_RANK, F32),
              (MLA_KV_RANK, F32), (MLA_Q_RANK, BF16), (MLA_KV_RANK, BF16), (HP, BF16), (HP, BF16), (HP, BF16)]
    return pl.pallas_call(
        body, name="pre_fwd", grid=(t // tm,),
        in_specs=[_row(tm, D_MODEL), _const(g1.shape), _const(win.shape), _const(gq.shape), _const(wqu.shape),
                  _const(gkv.shape), _const(wkv.shape), _row(tm, N_TAB * LANE)],
        out_specs=[_row(tm, w) for w, _ in widths],
        out_shape=[jax.ShapeDtypeStruct((t, w), d) for w, d in widths],
        compiler_params=_params("parallel"),
    )(h, g1, win, gq, wqu, gkv, wkv, tabs)


def _swa_mask(nb):
    row = lax.broadcasted_iota(jnp.int32, (SWA_GROUP * BLOCK, 2 * BLOCK), 0) & (BLOCK - 1)
    col = lax.broadcasted_iota(jnp.int32, (SWA_GROUP * BLOCK, 2 * BLOCK), 1)
    return (col > row) & (col <= row + BLOCK) & (col + (nb - 1) * BLOCK >= FRONT)


def _swa_group(ref, rows, j):
    return jnp.concatenate([ref[rows, LANE * (SWA_GROUP * j + g):LANE * (SWA_GROUP * j + g + 1)]
                            for g in range(SWA_GROUP)], axis=0)


def _swa_packed_group(ref, rows, j):
    heads = [SWA_GROUP * j + g for g in range(SWA_GROUP)]
    return jnp.concatenate([_pair_half(ref[rows, LANE * (hd // 2):LANE * (hd // 2 + 1)], hd % 2) for hd in heads], axis=0)


def _swa_sinks(sink_ref, j):
    return jnp.concatenate([jnp.full((BLOCK, 1), sink_ref[0, SWA_GROUP * j + g], F32) for g in range(SWA_GROUP)], axis=0)


def _swa_keys(prev_ref, cur_ref, rb, j):
    sl = slice(LANE * j, LANE * (j + 1))
    if rb == 0:
        return jnp.concatenate([prev_ref[:, sl], cur_ref[:BLOCK, sl]], axis=0)
    return cur_ref[BLOCK * (rb - 1):BLOCK * (rb + 1), sl]


def _swa_chains(t):
    return [(rb, j) for rb in range(_tile(t) // BLOCK) for j in range(SWA_KV_HEADS)]


def _swa_scores(sink_ref, q_ref, kp_ref, kc_ref, n, t):
    r = _tile(t) // BLOCK
    chains = _swa_chains(t)
    qs = [_swa_group(q_ref, slice(BLOCK * rb, BLOCK * (rb + 1)), j) for rb, j in chains]
    ks = [_swa_keys(kp_ref, kc_ref, rb, j) for rb, j in chains]
    ss = [_dot_nt(q4, k2) for q4, k2 in zip(qs, ks)]
    masks = [_swa_mask(n * r + rb) for rb in range(r)]
    out = []
    for (rb, j), s in zip(chains, ss):
        sink = _swa_sinks(sink_ref, j)
        s = jnp.where(masks[rb], s * SCALE_A, NEG)
        m = jnp.maximum(jnp.max(s, axis=1, keepdims=True), sink)
        e = jnp.exp(s - m)
        es = jnp.exp(sink - m)
        inv = 1.0 / (jnp.sum(e, axis=1, keepdims=True) + es)
        out.append((e * inv, es * inv))
    return qs, ks, out


def _swa_specs(t):
    ts = _tile(t)
    r = ts // BLOCK
    prev = lambda n: (jnp.maximum(n * r - 1, 0), 0)
    cur = lambda n: (n, 0)
    return [pl.BlockSpec(memory_space=pltpu.SMEM), pl.BlockSpec((ts, HP), cur),
            pl.BlockSpec((BLOCK, 2 * LANE), prev), pl.BlockSpec((ts, 2 * LANE), cur),
            pl.BlockSpec((BLOCK, 2 * LANE), prev), pl.BlockSpec((ts, 2 * LANE), cur)]


def _swa_fwd(sinks, q, k, v):
    t = q.shape[0]
    ts = _tile(t)

    def body(sink_ref, q_ref, kp_ref, kc_ref, vp_ref, vc_ref, o_ref):
        chains = _swa_chains(t)
        _, _, probs = _swa_scores(sink_ref, q_ref, kp_ref, kc_ref, pl.program_id(0), t)
        os_ = [_dot(p.astype(BF16), _swa_keys(vp_ref, vc_ref, rb, j)) for (rb, j), (p, _) in zip(chains, probs)]
        for (rb, j), o4 in zip(chains, os_):
            for g in range(0, SWA_GROUP, 2):
                pair = (SWA_GROUP * j + g) // 2
                o_ref[BLOCK * rb:BLOCK * (rb + 1), LANE * pair:LANE * (pair + 1)] = _pack_pair(
                    o4[BLOCK * g:BLOCK * (g + 1)], o4[BLOCK * (g + 1):BLOCK * (g + 2)])

    return pl.pallas_call(
        body, name="swa_fwd", grid=(t // ts,),
        in_specs=_swa_specs(t),
        out_specs=pl.BlockSpec((ts, SWA_Q_W), lambda n: (n, 0)),
        out_shape=jax.ShapeDtypeStruct((t, SWA_Q_W), F32),
        compiler_params=_params("parallel"),
    )(sinks, q, k, k, v, v)


def _causal_mask(q0, k0, tq, tk, transposed):
    if transposed:
        key = k0 + lax.broadcasted_iota(jnp.int32, (tk, tq), 0)
        qry = q0 + lax.broadcasted_iota(jnp.int32, (tk, tq), 1)
    else:
        qry = q0 + lax.broadcasted_iota(jnp.int32, (tq, tk), 0)
        key = k0 + lax.broadcasted_iota(jnp.int32, (tq, tk), 1)
    return (key <= qry) & (key >= FRONT)


def _heads(ref, hb, rows=slice(None)):
    return [ref[rows, LANE * a:LANE * (a + 1)] for a in range(hb)]


def _head_stats(t):
    return jax.ShapeDtypeStruct((MLA_HEADS // MLA_HB, t, MLA_HB), F32)


def _mla_fwd(q, k, v, shards=()):
    t = q.shape[0]
    tq = _tile(t)
    nq = t // tq
    n = len(shards)
    steps = (MLA_HEADS // MLA_HB) * nq

    def body(q_ref, k_ref, v_ref, *rest):
        x_refs, (o_ref, lse_ref), out_refs = rest[:n], rest[n:n + 2], rest[n + 2:2 * n + 2]
        acc_sc, sems = rest[2 * n + 2], rest[2 * n + 3:]
        i = pl.program_id(1)
        step_id = pl.program_id(0) * nq + i
        if n:
            plan = _gather_plan(x_refs, out_refs, *sems)
            pl.when(step_id == 0)(plan.start)
            pl.when(step_id == steps // 2)(plan.forward)
        qs = _heads(q_ref, MLA_HB)
        acc_sc[...] = jnp.zeros(acc_sc.shape, F32)

        def step(j, carry, masked):
            rows = pl.ds(pl.multiple_of(j * tq, tq), tq)
            ks, vs = _heads(k_ref, MLA_HB, rows), _heads(v_ref, MLA_HB, rows)
            ss = [_dot_nt(qh, kh) for qh, kh in zip(qs, ks)]
            if masked:
                mask = _causal_mask(i * tq, j * tq, tq, tq, False)
                ss = [jnp.where(mask, s, NEG) for s in ss]
            mid, out = [], []
            for s, (m, l) in zip(ss, carry):
                mn = jnp.maximum(m, jnp.max(s, axis=1, keepdims=True))
                al = jnp.exp2(m - mn)
                p = jnp.exp2(s - mn)
                out.append((mn, al * l + jnp.sum(p, axis=1, keepdims=True)))
                mid.append((al, p.astype(BF16)))
            for a, ((al, p), vh) in enumerate(zip(mid, vs)):
                acc_sc[a] = al * acc_sc[a] + _dot(p, vh)
            return tuple(out)

        init = ((jnp.full((tq, 1), NEG, F32), jnp.zeros((tq, 1), F32)),) * MLA_HB
        carry = lax.fori_loop(0, jnp.minimum(i, 1) + 1, lambda it, c: step(it * i, c, True), init)
        carry = lax.fori_loop(1, i, lambda j, c: step(j, c, False), carry)
        outs = [acc_sc[a] * (1.0 / l) for a, (_, l) in enumerate(carry)]
        for a in range(0, MLA_HB, 2):
            o_ref[:, HALF * a:HALF * (a + 2)] = _pack_pair(outs[a], outs[a + 1])
        for a, (m, l) in enumerate(carry):
            lse_ref[:, a:a + 1] = m + jnp.log2(l)
        if n:
            pl.when(step_id == steps - 1)(plan.finish)

    blk = pl.BlockSpec((tq, MLA_HB * LANE), lambda h, i: (i, h))
    full = pl.BlockSpec((t, MLA_HB * LANE), lambda h, i: (0, h))
    packed = pl.BlockSpec((tq, MLA_HB * HALF), lambda h, i: (i, h))
    out = pl.pallas_call(
        body, name="mla_fwd_gather" if n else "mla_fwd", grid=(MLA_HEADS // MLA_HB, nq),
        in_specs=[blk, full, full] + [ANY] * n,
        out_specs=[packed, pl.BlockSpec((None, tq, MLA_HB), lambda h, i: (h, i, 0))] + [ANY] * n,
        out_shape=[jax.ShapeDtypeStruct((t, MLA_OUT_W), F32), _head_stats(t)]
        + [jax.ShapeDtypeStruct((N_DEV,) + a.shape, a.dtype) for a in shards],
        scratch_shapes=[pltpu.VMEM((MLA_HB, tq, LANE), F32)] + (_comm_sems(n) if n else []),
        compiler_params=_params("arbitrary", "arbitrary"),
    )(q, k, v, *shards)
    return out[0], out[1], out[2:]


def _mix_fwd(h, oa, ob, ga, gb, wo, g2):
    t = h.shape[0]
    tm = _tile(t)

    def body(h_ref, oa_ref, ob_ref, ga_ref, gb_ref, wo_ref, g2_ref, h2_ref, mix_ref, u2_ref):
        oa_v = oa_ref[...]
        ob_v = ob_ref[...]
        na = (oa_v * _rms_r(oa_v, SWA_Q_W) * ga_ref[...]).astype(BF16)
        nb = (ob_v * _rms_r(ob_v, MLA_OUT_W) * gb_ref[...]).astype(BF16)
        mix_ref[:, :SWA_Q_W] = na
        mix_ref[:, SWA_Q_W:] = nb
        h2 = h_ref[...] + _dot(na, wo_ref[:SWA_Q_W, :]) + _dot(nb, wo_ref[SWA_Q_W:, :])
        h2_ref[...] = h2
        u2_ref[...] = (h2 * _rms_r(h2, D_MODEL) * g2_ref[...]).astype(BF16)

    mix_w = SWA_Q_W + MLA_OUT_W
    return pl.pallas_call(
        body, name="mix_fwd", grid=(t // tm,),
        in_specs=[_row(tm, D_MODEL), _row(tm, SWA_Q_W), _row(tm, MLA_OUT_W), _const(ga.shape), _const(gb.shape),
                  _const(wo.shape), _const(g2.shape)],
        out_specs=[_row(tm, D_MODEL), _row(tm, mix_w), _row(tm, D_MODEL)],
        out_shape=[jax.ShapeDtypeStruct((t, D_MODEL), F32), jax.ShapeDtypeStruct((t, mix_w), BF16),
                   jax.ShapeDtypeStruct((t, D_MODEL), BF16)],
        compiler_params=_params("parallel"),
    )(h, oa, ob, ga, gb, wo, g2)


def _ffn_fwd(h2, u2, wg_t, wu_t, wd):
    t = h2.shape[0]
    tm = _tile(t)
    dff = wd.shape[0]

    def body(h2_ref, u2_ref, wg_ref, wu_ref, wd_ref, h3_ref, g_ref, up_ref):
        u2v = u2_ref[...]
        g = _dot_nt(u2v, wg_ref[...])
        up = _dot_nt(u2v, wu_ref[...])
        g_ref[...] = g.astype(BF16)
        up_ref[...] = up.astype(BF16)
        a = (g * jax.nn.sigmoid(g) * up).astype(BF16)
        h3_ref[...] = h2_ref[...] + _dot(a, wd_ref[...])

    return pl.pallas_call(
        body, name="ffn_fwd", grid=(t // tm,),
        in_specs=[_row(tm, D_MODEL), _row(tm, D_MODEL), _const(wg_t.shape), _const(wu_t.shape), _const(wd.shape)],
        out_specs=[_row(tm, D_MODEL), _row(tm, dff), _row(tm, dff)],
        out_shape=[jax.ShapeDtypeStruct((t, D_MODEL), F32), jax.ShapeDtypeStruct((t, dff), BF16),
                   jax.ShapeDtypeStruct((t, dff), BF16)],
        compiler_params=_params("parallel"),
    )(h2, u2, wg_t, wu_t, wd)


def _loss_bwd(h, gf, target):
    t = h.shape[0]
    tm = _tile(t)
    first_row = FRONT + N_META

    def body(h_ref, gf_ref, t_ref, dh_ref, dgf_ref, loss_ref):
        i = pl.program_id(0)
        hv = h_ref[...]
        y = hv * _rms_r(hv, D_MODEL) * gf_ref[...]
        row = i * tm + lax.broadcasted_iota(jnp.int32, (tm, 1), 0)
        err = jnp.where(row >= first_row, y - t_ref[...], 0.0)
        dx, dg = _rms_bwd(hv, gf_ref[...], err * (1.0 / D_MODEL), D_MODEL)
        dh_ref[...] = dx
        _acc(dgf_ref, dg, i == 0)
        part = 0.5 * jnp.sum(jnp.sum(err * err, axis=1, keepdims=True) * (1.0 / D_MODEL), axis=0, keepdims=True)
        _acc(loss_ref, jnp.broadcast_to(part, (1, LANE)), i == 0)

    return pl.pallas_call(
        body, name="loss_bwd", grid=(t // tm,),
        in_specs=[_row(tm, D_MODEL), _const(gf.shape), _row(tm, D_MODEL)],
        out_specs=[_row(tm, D_MODEL), _const((1, D_MODEL)), _const((1, LANE))],
        out_shape=[jax.ShapeDtypeStruct((t, D_MODEL), F32), jax.ShapeDtypeStruct((1, D_MODEL), F32),
                   jax.ShapeDtypeStruct((1, LANE), F32)],
        compiler_params=_params("arbitrary"),
    )(h, gf, target)


def _tn_matmul(a, b, name, cols=None):
    t, n = b.shape
    first, k = cols or (0, a.shape[1])
    tk = next(c for c in (k, 1024, 512, 256, 128) if k % c == 0 and first % c == 0 and c <= 1024)
    fits = lambda c: 2 * (t * (tk + c) * 2 + tk * c * 2) <= TN_VMEM_BUDGET
    tn = next(c for c in (n, 1024, 512, 256, 128) if n % c == 0 and fits(c))

    def body(a_ref, b_ref, o_ref):
        o_ref[...] = _dot_tn(a_ref[...], b_ref[...]).astype(BF16)

    return pl.pallas_call(
        body, name=name, grid=(k // tk, n // tn),
        in_specs=[pl.BlockSpec((t, tk), lambda i, j: (0, i + first // tk)), pl.BlockSpec((t, tn), lambda i, j: (0, j))],
        out_specs=pl.BlockSpec((tk, tn), lambda i, j: (i, j)),
        out_shape=jax.ShapeDtypeStruct((k, n), BF16),
        compiler_params=_params("parallel", "parallel"),
    )(a, b)


def _ffn_bwd_a(dh3, g, up, wd):
    t = dh3.shape[0]
    tm = _tile(t)
    dff = wd.shape[0]

    def body(dh3_ref, g_ref, up_ref, wd_ref, a_ref, dgu_ref, dh3b_ref):
        dh3b = dh3_ref[...].astype(BF16)
        dh3b_ref[...] = dh3b
        da = _dot_nt(dh3b, wd_ref[...])
        gv = g_ref[...].astype(F32)
        upv = up_ref[...].astype(F32)
        sg = jax.nn.sigmoid(gv)
        silu = gv * sg
        a_ref[...] = (silu * upv).astype(BF16)
        dgu_ref[:, :dff] = (da * upv * (sg * (1.0 + gv * (1.0 - sg)))).astype(BF16)
        dgu_ref[:, dff:] = (da * silu).astype(BF16)

    return pl.pallas_call(
        body, name="ffn_bwd_a", grid=(t // tm,),
        in_specs=[_row(tm, D_MODEL), _row(tm, dff), _row(tm, dff), _const(wd.shape)],
        out_specs=[_row(tm, dff), _row(tm, 2 * dff), _row(tm, D_MODEL)],
        out_shape=[jax.ShapeDtypeStruct((t, dff), BF16), jax.ShapeDtypeStruct((t, 2 * dff), BF16),
                   jax.ShapeDtypeStruct((t, D_MODEL), BF16)],
        compiler_params=_params("parallel"),
    )(dh3, g, up, wd)


def _ffn_bwd_b(dh3, dgu, h2, g2, wg_t, wu_t):
    t = dh3.shape[0]
    tm = _tile(t)
    dff = wg_t.shape[0]

    def body(dh3_ref, dgu_ref, h2_ref, g2_ref, wg_ref, wu_ref, dh2_ref, dh2b_ref, dg2_ref):
        du2 = _dot(dgu_ref[:, :dff], wg_ref[...]) + _dot(dgu_ref[:, dff:], wu_ref[...])
        dx, dg = _rms_bwd(h2_ref[...], g2_ref[...], du2, D_MODEL)
        dh2 = dh3_ref[...] + dx
        dh2_ref[...] = dh2
        dh2b_ref[...] = dh2.astype(BF16)
        _acc(dg2_ref, dg, pl.program_id(0) == 0)

    return pl.pallas_call(
        body, name="ffn_bwd_b", grid=(t // tm,),
        in_specs=[_row(tm, D_MODEL), _row(tm, 2 * dff), _row(tm, D_MODEL), _const(g2.shape), _const(wg_t.shape),
                  _const(wu_t.shape)],
        out_specs=[_row(tm, D_MODEL), _row(tm, D_MODEL), _const((1, D_MODEL))],
        out_shape=[jax.ShapeDtypeStruct((t, D_MODEL), F32), jax.ShapeDtypeStruct((t, D_MODEL), BF16),
                   jax.ShapeDtypeStruct((1, D_MODEL), F32)],
        compiler_params=_params("arbitrary"),
    )(dh3, dgu, h2, g2, wg_t, wu_t)


def _mix_bwd(dh2, oa, ob, ga, gb, wo):
    t = dh2.shape[0]
    tm = _tile(t)

    def body(dh2_ref, oa_ref, ob_ref, ga_ref, gb_ref, wo_ref, doa_ref, dob_ref, dl_ref, dga_ref, dgb_ref):
        first = pl.program_id(0) == 0
        d = dh2_ref[...]
        ob_v = ob_ref[...]
        dxa, dga = _rms_bwd(oa_ref[...], ga_ref[...], _dot_nt(d, wo_ref[:SWA_Q_W, :]), SWA_Q_W)
        dxb, dgb = _rms_bwd(ob_v, gb_ref[...], _dot_nt(d, wo_ref[SWA_Q_W:, :]), MLA_OUT_W)
        lower = lax.broadcasted_iota(jnp.int32, (tm, LANE), 1) < HALF
        for hd in range(MLA_HEADS):
            sl = slice(LANE * (hd // 2), LANE * (hd // 2 + 1))
            mine = lower if hd % 2 == 0 else jnp.logical_not(lower)
            delta = jnp.sum(jnp.where(mine, ob_v[:, sl] * dxb[:, sl], 0.0), axis=1, keepdims=True)
            dl_ref[hd // MLA_HB, :, hd % MLA_HB:hd % MLA_HB + 1] = delta
        for ref, dx, heads in ((doa_ref, dxa, SWA_HEADS), (dob_ref, dxb, MLA_HEADS)):
            for hd in range(heads):
                slab = dx[:, LANE * (hd // 2):LANE * (hd // 2 + 1)]
                ref[:, LANE * hd:LANE * (hd + 1)] = _unpack_pair(slab, hd % 2).astype(BF16)
        _acc(dga_ref, dga, first)
        _acc(dgb_ref, dgb, first)

    return pl.pallas_call(
        body, name="mix_bwd", grid=(t // tm,),
        in_specs=[_row(tm, D_MODEL), _row(tm, SWA_Q_W), _row(tm, MLA_OUT_W), _const(ga.shape), _const(gb.shape),
                  _const(wo.shape)],
        out_specs=[_row(tm, HP), _row(tm, HP), pl.BlockSpec((MLA_HEADS // MLA_HB, tm, MLA_HB), lambda i: (0, i, 0)),
                   _const((1, SWA_Q_W)), _const((1, MLA_OUT_W))],
        out_shape=[jax.ShapeDtypeStruct((t, HP), BF16), jax.ShapeDtypeStruct((t, HP), BF16), _head_stats(t),
                   jax.ShapeDtypeStruct((1, SWA_Q_W), F32), jax.ShapeDtypeStruct((1, MLA_OUT_W), F32)],
        compiler_params=_params("arbitrary"),
    )(dh2, oa, ob, ga, gb, wo)


def _swa_bwd(sinks, q, k, v, o, do):
    t = q.shape[0]
    ts = _tile(t)

    def body(sink_ref, q_ref, kp_ref, kc_ref, vp_ref, vc_ref, o_ref, do_ref,
             dq_ref, dkc_ref, dkp_ref, dvc_ref, dvp_ref, dsink_ref):
        n = pl.program_id(0)
        chains = _swa_chains(t)
        qs, ks, probs = _swa_scores(sink_ref, q_ref, kp_ref, kc_ref, n, t)
        dos = [_swa_group(do_ref, slice(BLOCK * rb, BLOCK * (rb + 1)), j) for rb, j in chains]
        vs = [_swa_keys(vp_ref, vc_ref, rb, j) for rb, j in chains]
        dps = [_dot_nt(do4, v2) for do4, v2 in zip(dos, vs)]
        dss, dsks = [], []
        for (rb, j), (p, psink), do4, dp in zip(chains, probs, dos, dps):
            o4 = _swa_packed_group(o_ref, slice(BLOCK * rb, BLOCK * (rb + 1)), j)
            delta = jnp.sum(o4 * do4.astype(F32), axis=1, keepdims=True)
            dss.append(p * (dp - delta) * SCALE_A)
            dsks.append(-psink * delta)
        dqs = [_dot(ds.astype(BF16), k2) for ds, k2 in zip(dss, ks)]
        dks = [_dot(ds.T.astype(BF16), q4) for ds, q4 in zip(dss, qs)]
        dvs = [_dot(p.T.astype(BF16), do4) for (p, _), do4 in zip(probs, dos)]
        dsink = [jnp.zeros((1, LANE), F32)] * SWA_HEADS
        ext = {}
        for (rb, j), dq4, dk2, dv2, dsk in zip(chains, dqs, dks, dvs, dsks):
            for g in range(SWA_GROUP):
                hd = SWA_GROUP * j + g
                rows = slice(BLOCK * g, BLOCK * (g + 1))
                dq_ref[BLOCK * rb:BLOCK * (rb + 1), LANE * hd:LANE * (hd + 1)] = dq4[rows].astype(BF16)
                dsink[hd] = dsink[hd] + jnp.sum(dsk[rows], axis=0, keepdims=True)
            for half in range(2):
                key = (j, rb + half)
                part = (dk2[BLOCK * half:BLOCK * (half + 1)], dv2[BLOCK * half:BLOCK * (half + 1)])
                ext[key] = part if key not in ext else (ext[key][0] + part[0], ext[key][1] + part[1])
        for (j, blk), (dk, dv) in ext.items():
            sl = slice(LANE * j, LANE * (j + 1))
            if blk == 0:
                dkp_ref[:, sl] = dk
                dvp_ref[:, sl] = dv
            else:
                dkc_ref[BLOCK * (blk - 1):BLOCK * blk, sl] = dk
                dvc_ref[BLOCK * (blk - 1):BLOCK * blk, sl] = dv
        for hd in range(SWA_HEADS):
            _acc(dsink_ref.at[hd:hd + 1, :], jnp.broadcast_to(dsink[hd], (1, LANE)), n == 0)

    cur = lambda n: (n, 0)
    kv = pl.BlockSpec((ts, 2 * LANE), cur)
    kvp = pl.BlockSpec((BLOCK, 2 * LANE), cur)
    hp = pl.BlockSpec((ts, HP), cur)
    kvs = jax.ShapeDtypeStruct((t, 2 * LANE), F32)
    kvps = jax.ShapeDtypeStruct((t // ts * BLOCK, 2 * LANE), F32)
    return pl.pallas_call(
        body, name="swa_bwd", grid=(t // ts,),
        in_specs=_swa_specs(t) + [pl.BlockSpec((ts, SWA_Q_W), cur), hp],
        out_specs=[hp, kv, kvp, kv, kvp, _const((SWA_HEADS, LANE))],
        out_shape=[jax.ShapeDtypeStruct((t, HP), BF16), kvs, kvps, kvs, kvps,
                   jax.ShapeDtypeStruct((SWA_HEADS, LANE), F32)],
        compiler_params=_params("arbitrary"),
    )(sinks, q, k, k, v, v, o, do)


def _mla_bwd(q, k, v, do, lse, dl, slabs=()):
    t = q.shape[0]
    tq = _tile(t)
    nq = t // tq
    n = len(slabs)
    hb = MLA_HB
    steps = (MLA_HEADS // hb) * nq

    def body(k_ref, v_ref, q_ref, do_ref, lse_ref, dl_ref, *rest):
        in_refs, (dq_ref, dk_ref, dv_ref), out_refs = rest[:n], rest[n:n + 3], rest[n + 3:2 * n + 3]
        (dq_sc, dk_sc, dv_sc), sems = rest[2 * n + 3:2 * n + 6], rest[2 * n + 6:]
        j = pl.program_id(1)
        step_id = pl.program_id(0) * nq + j
        if n:
            plan = _exchange_plan(in_refs, out_refs, *sems)
            pl.when(step_id == 0)(plan.start)

        @pl.when(j == 0)
        def _():
            dq_sc[...] = jnp.zeros(dq_sc.shape, F32)

        dk_sc[...] = jnp.zeros(dk_sc.shape, F32)
        dv_sc[...] = jnp.zeros(dv_sc.shape, F32)
        ks, vs = _heads(k_ref, hb), _heads(v_ref, hb)

        def step(i, carry, masked):
            rows = pl.ds(pl.multiple_of(i * tq, tq), tq)
            qs, dos = _heads(q_ref, hb, rows), _heads(do_ref, hb, rows)
            ss = [_dot_nt(qh, kh) for qh, kh in zip(qs, ks)]
            dps = [_dot_nt(doh, vh) for doh, vh in zip(dos, vs)]
            if masked:
                mask = _causal_mask(i * tq, j * tq, tq, tq, False)
                ss = [jnp.where(mask, s_, NEG) for s_ in ss]
            ps = [jnp.exp2(s_ - lse_ref[rows, a:a + 1]) for a, s_ in enumerate(ss)]
            dss = [(p * (dp - dl_ref[rows, a:a + 1])).astype(BF16) for a, (p, dp) in enumerate(zip(ps, dps))]
            for a, (ds, p, qh, kh, doh) in enumerate(zip(dss, ps, qs, ks, dos)):
                dq_sc[a, rows, :] += _dot(ds, kh)
                dk_sc[a] += _dot_tn(ds, qh)
                dv_sc[a] += _dot_tn(p.astype(BF16), doh)
            return carry

        split = jnp.where(j == 0, nq, j + 1)
        lax.fori_loop(j, split, lambda i, c: step(i, c, True), 0)
        lax.fori_loop(split, nq, lambda i, c: step(i, c, False), 0)
        for a in range(hb):
            dk_ref[:, LANE * a:LANE * (a + 1)] = (dk_sc[a] * (1.0 / LOG2E)).astype(BF16)
            dv_ref[:, LANE * a:LANE * (a + 1)] = dv_sc[a].astype(BF16)

        @pl.when(j == nq - 1)
        def _():
            for a in range(hb):
                dq_ref[:, LANE * a:LANE * (a + 1)] = (dq_sc[a] * SCALE_B).astype(BF16)

        if n:
            pl.when(step_id == steps - 1)(plan.finish)

    blk = pl.BlockSpec((tq, hb * LANE), lambda h, j: (j, h))
    full = pl.BlockSpec((t, hb * LANE), lambda h, j: (0, h))
    cols = pl.BlockSpec((None, t, hb), lambda h, j: (h, 0, 0))
    out = pl.pallas_call(
        body, name="mla_bwd_exchange" if n else "mla_bwd", grid=(MLA_HEADS // hb, nq),
        in_specs=[blk, blk, full, full, cols, cols] + [ANY] * n, out_specs=[full, blk, blk] + [ANY] * n,
        out_shape=[jax.ShapeDtypeStruct((t, HP), BF16)] * 3 + [jax.ShapeDtypeStruct(a.shape, a.dtype) for a in slabs],
        scratch_shapes=[pltpu.VMEM((hb, t, LANE), F32)] + [pltpu.VMEM((hb, tq, LANE), F32)] * 2
        + (_comm_sems(n) if n else []),
        compiler_params=_params("arbitrary", "arbitrary"),
    )(k, v, q, do, lse, dl, *slabs)
    return out[0], out[1], out[2], out[3:]


def _pre_bwd(dh2, h, cq, ckv, dqa, dka, dka_next, dva, dva_next, dqb, dkf, dvb, g1, win, gq, wqu, gkv, wkv, tabs):
    t = h.shape[0]
    tm = _tile(t)

    def body(dh2_ref, h_ref, cq_ref, ckv_ref, dqa_ref, dka_ref, dkan_ref, dva_ref, dvan_ref, dqb_ref, dkf_ref, dvb_ref,
             g1_ref, win_ref, gq_ref, wqu_ref, gkv_ref, wkv_ref, tab_ref,
             dh_ref, dp_ref, dqbo_ref, dkvo_ref, dg1_ref, dgq_ref, dgkv_ref):
        first = pl.program_id(0) == 0
        ca, sa1, sa2, cb, sb1, sb2, ck = _tabs(tab_ref)
        dkr = jnp.zeros((tm, LANE), F32)
        for c in range(MLA_HEADS):
            sl = slice(LANE * c, LANE * (c + 1))
            dqbo_ref[:, sl] = _rope_t(dqb_ref[:, sl].astype(F32), cb, sb1, sb2, 16).astype(BF16)
            dkr += dkf_ref[:, sl].astype(F32)
        dkvo_ref[:, :HP] = dkf_ref[...]
        dkvo_ref[:, HP:] = dvb_ref[...]
        dcq, dgq = _rms_bwd(cq_ref[...], gq_ref[...], _dot(dqbo_ref[...], wqu_ref[...]), MLA_Q_RANK)
        dckv, dgkv = _rms_bwd(ckv_ref[...], gkv_ref[...], _dot(dkvo_ref[...], wkv_ref[...]), MLA_KV_RANK)
        for c in range(SWA_HEADS):
            sl = slice(LANE * c, LANE * (c + 1))
            dp_ref[:, PO_QA + LANE * c:PO_QA + LANE * (c + 1)] = _rope_t(dqa_ref[:, sl].astype(F32), ca, sa1, sa2,
                                                                          32).astype(BF16)
        last = slice(tm - BLOCK, tm)
        more = pl.program_id(0) < t // tm - 1
        for c in range(SWA_KV_HEADS):
            sl = slice(LANE * c, LANE * (c + 1))
            dk = dka_ref[:, sl]
            dk_last = dk[tm - BLOCK:] + jnp.where(more, dkan_ref[:, sl], 0.0)
            cols = slice(PO_KA + LANE * c, PO_KA + LANE * (c + 1))
            if tm > BLOCK:
                dp_ref[:tm - BLOCK, cols] = _rope_t(dk[:tm - BLOCK], ca[:tm - BLOCK], sa1[:tm - BLOCK], sa2[:tm - BLOCK],
                                                    32).astype(BF16)
            dp_ref[last, cols] = _rope_t(dk_last, ca[tm - BLOCK:], sa1[tm - BLOCK:], sa2[tm - BLOCK:], 32).astype(BF16)
        if tm > BLOCK:
            dp_ref[:tm - BLOCK, PO_VA:PO_CQ] = dva_ref[:tm - BLOCK, :].astype(BF16)
        dp_ref[last, PO_VA:PO_CQ] = (dva_ref[tm - BLOCK:, :] + jnp.where(more, dvan_ref[...], 0.0)).astype(BF16)
        dp_ref[:, PO_CQ:PO_CKV] = dcq.astype(BF16)
        dp_ref[:, PO_CKV:PO_KR] = dckv.astype(BF16)
        dp_ref[:, PO_KR:PW_IN] = _rope_t(dkr, ck, sb1, sb2, 16).astype(BF16)
        dx, dg1 = _rms_bwd(h_ref[...], g1_ref[...], _dot(dp_ref[...], win_ref[...]), D_MODEL)
        dh_ref[...] = dh2_ref[...] + dx
        _acc(dg1_ref, dg1, first)
        _acc(dgq_ref, dgq, first)
        _acc(dgkv_ref, dgkv, first)

    kv = _row(tm, 2 * LANE)
    nxt = pl.BlockSpec((BLOCK, 2 * LANE), lambda i: (jnp.minimum(i + 1, t // tm - 1), 0))
    return pl.pallas_call(
        body, name="pre_bwd", grid=(t // tm,),
        in_specs=[_row(tm, D_MODEL), _row(tm, D_MODEL), _row(tm, MLA_Q_RANK), _row(tm, MLA_KV_RANK), _row(tm, HP),
                  kv, nxt, kv, nxt, _row(tm, HP), _row(tm, HP), _row(tm, HP),
                  _const(g1.shape), _const(win.shape), _const(gq.shape), _const(wqu.shape), _const(gkv.shape),
                  _const(wkv.shape), _row(tm, N_TAB * LANE)],
        out_specs=[_row(tm, D_MODEL), _row(tm, PW_IN), _row(tm, HP), _row(tm, 2 * HP),
                   _const((1, D_MODEL)), _const((1, MLA_Q_RANK)), _const((1, MLA_KV_RANK))],
        out_shape=[jax.ShapeDtypeStruct((t, D_MODEL), F32), jax.ShapeDtypeStruct((t, PW_IN), BF16),
                   jax.ShapeDtypeStruct((t, HP), BF16), jax.ShapeDtypeStruct((t, 2 * HP), BF16),
                   jax.ShapeDtypeStruct((1, D_MODEL), F32), jax.ShapeDtypeStruct((1, MLA_Q_RANK), F32),
                   jax.ShapeDtypeStruct((1, MLA_KV_RANK), F32)],
        compiler_params=_params("arbitrary"),
    )(dh2, h, cq, ckv, dqa, dka, dka_next, dva, dva_next, dqb, dkf, dvb, g1, win, gq, wqu, gkv, wkv, tabs)


def _rope_tables(t):
    pos = (jnp.arange(t, dtype=jnp.int32) - FRONT).astype(F32)[:, None]
    lane = jnp.arange(LANE)[None, :]

    def table(dim, start):
        half = dim // 2
        inv = ROPE_THETA ** (-jnp.arange(0, dim, 2, dtype=F32) / dim)
        ang = pos * inv[None, :]
        cos = jnp.concatenate([jnp.cos(ang)] * 2, axis=1)
        sin = jnp.concatenate([jnp.sin(ang)] * 2, axis=1)
        pad = lambda a: jnp.pad(a, ((0, 0), (start, LANE - start - dim)))
        first = (lane >= start) & (lane < start + half)
        second = (lane >= start + half) & (lane < start + dim)
        return pad(cos), jnp.where(first, -pad(sin), 0.0), jnp.where(second, pad(sin), 0.0)

    ca, sa1, sa2 = table(SWA_HEAD_DIM, 0)
    ck, sb1, sb2 = table(MLA_ROPE_DIM, MLA_NOPE_DIM)
    cb = jnp.where(lane < MLA_NOPE_DIM, 1.0, ck)
    return jnp.concatenate([ca, sa1, sa2, cb, sb1, sb2, ck], axis=1)


def _pad_heads(w, heads, dim, axis):
    shp = w.shape
    w = w.reshape(shp[:axis] + (heads, dim) + shp[axis + 1:])
    pad = [(0, 0)] * w.ndim
    pad[axis + 1] = (0, LANE - dim)
    return jnp.pad(w, pad).reshape(shp[:axis] + (heads * LANE,) + shp[axis + 1:])


def _unpad_heads(w, heads, dim, axis):
    shp = w.shape
    w = w.reshape(shp[:axis] + (heads, LANE) + shp[axis + 1:])
    w = lax.slice_in_dim(w, 0, dim, axis=axis + 1)
    return w.reshape(shp[:axis] + (heads * dim,) + shp[axis + 1:])


def _pad_layer(w_in, w_q_up, w_kv_up):
    o1 = SWA_Q_W
    o2 = o1 + SWA_KV_W
    o3 = o2 + SWA_KV_W
    o4 = o3 + MLA_Q_RANK
    o5 = o4 + MLA_KV_RANK
    kr = jnp.pad(w_in[o5:], ((MLA_NOPE_DIM, LANE - MLA_QK_DIM), (0, 0)))
    win = jnp.concatenate([
        _pad_heads(w_in[:o1], SWA_HEADS, SWA_HEAD_DIM, 0),
        _pad_heads(w_in[o1:o2], SWA_KV_HEADS, SWA_HEAD_DIM, 0),
        _pad_heads(w_in[o2:o3], SWA_KV_HEADS, SWA_HEAD_DIM, 0),
        w_in[o3:o5], kr], axis=0)
    wqu = _pad_heads(w_q_up, MLA_HEADS, MLA_QK_DIM, 0)
    kv = w_kv_up.reshape(MLA_HEADS, MLA_NOPE_DIM + MLA_V_DIM, MLA_KV_RANK)
    wkv = jnp.concatenate([
        _pad_heads(kv[:, :MLA_NOPE_DIM].reshape(-1, MLA_KV_RANK), MLA_HEADS, MLA_NOPE_DIM, 0),
        _pad_heads(kv[:, MLA_NOPE_DIM:].reshape(-1, MLA_KV_RANK), MLA_HEADS, MLA_V_DIM, 0)], axis=0)
    return win, wqu, wkv


def _unpad_layer(dwin, dwqu, dwkv):
    d_w_in = jnp.concatenate([
        _unpad_heads(dwin[PO_QA:PO_KA], SWA_HEADS, SWA_HEAD_DIM, 0),
        _unpad_heads(dwin[PO_KA:PO_VA], SWA_KV_HEADS, SWA_HEAD_DIM, 0),
        _unpad_heads(dwin[PO_VA:PO_CQ], SWA_KV_HEADS, SWA_HEAD_DIM, 0),
        dwin[PO_CQ:PO_KR], dwin[PO_KR + MLA_NOPE_DIM:PO_KR + MLA_QK_DIM]], axis=0)
    d_w_q_up = _unpad_heads(dwqu, MLA_HEADS, MLA_QK_DIM, 0)
    dk = _unpad_heads(dwkv[:HP], MLA_HEADS, MLA_NOPE_DIM, 0).reshape(MLA_HEADS, MLA_NOPE_DIM, MLA_KV_RANK)
    dv = _unpad_heads(dwkv[HP:], MLA_HEADS, MLA_V_DIM, 0).reshape(MLA_HEADS, MLA_V_DIM, MLA_KV_RANK)
    d_w_kv_up = jnp.concatenate([dk, dv], axis=1).reshape(-1, MLA_KV_RANK)
    return d_w_in, d_w_q_up, d_w_kv_up


def _train_example(x, target, meta, vec, weights):
    s = x.shape[0]
    depth = vec["attn_norm"].shape[0]
    t = FRONT + N_META + s
    assert t % BLOCK == 0
    tabs = _rope_tables(t)
    h = jnp.concatenate([jnp.zeros((FRONT, D_MODEL), F32), meta, x], axis=0)
    tgt = jnp.concatenate([jnp.zeros((FRONT + N_META, D_MODEL), F32), target], axis=0)
    row = lambda v: v[None, :]

    saved = []
    for l in range(depth):
        win, wqu, wkv = _pad_layer(*weights.attn_in(l))
        g1, gq, gkv, g2, ga, gb = (row(vec[n][l]) for n in ("attn_norm", "q_norm", "kv_norm", "ffn_norm",
                                                            "out_norm_swa", "out_norm_mla"))
        sk = row(vec["sinks"][l])
        u, qa, ka, va, cq, ckv, qn, kvn, qb, kf, vb = _pre_fwd(h, g1, win, gq, wqu, gkv, wkv, tabs)
        oa = _swa_fwd(sk, qa, ka, va)
        ob, lse = weights.mla_fwd(l, qb, kf, vb)
        wo = weights.w_o(l)
        h2, mix, u2 = _mix_fwd(h, oa, ob, ga, gb, wo, g2)
        wg, wu, wd = weights.ffn(l)
        h3, gt, up = _ffn_fwd(h2, u2, wg, wu, wd)
        saved.append((h, u, qa, ka, va, cq, ckv, qn, kvn, qb, kf, vb, oa, ob, lse, h2, mix, u2, gt, up,
                      win, wqu, wkv, wo, ga, gb, g1, gq, gkv, g2, sk, wg, wu, wd))
        h = h3

    dh, d_final, loss = _loss_bwd(h, row(vec["final_norm"]), tgt)

    grads = []
    for l in reversed(range(depth)):
        (h0, u, qa, ka, va, cq, ckv, qn, kvn, qb, kf, vb, oa, ob, lse, h2, mix, u2, gt, up,
         win, wqu, wkv, wo, ga, gb, g1, gq, gkv, g2, sk, wg, wu, wd) = saved[l]
        dff = wd.shape[0]
        act, dgu, dhb = _ffn_bwd_a(dh, gt, up, wd)
        weights.ffn_grads(l, _tn_matmul(dgu, u2, "dw_gate", (0, dff)), _tn_matmul(dgu, u2, "dw_up", (dff, dff)),
                          _tn_matmul(act, dhb, "dw_down"))
        dh2, dh2b, d_g2 = _ffn_bwd_b(dh, dgu, h2, g2, wg, wu)
        weights.attn_grads(l, w_o=_tn_matmul(mix, dh2b, "dw_o"))
        doa, dob, dl, d_ga, d_gb = _mix_bwd(dh2b, oa, ob, ga, gb, wo)
        dqa, dkc, dkp, dvc, dvp, dsink = _swa_bwd(sk, qa, ka, va, oa, doa)
        dqb, dkf, dvb = weights.mla_bwd(l, qb, kf, vb, dob, lse, dl)
        dh, dp, dqbo, dkvo, d_g1, d_gq, d_gkv = _pre_bwd(
            dh2, h0, cq, ckv, dqa, dkc, dkp, dvc, dvp, dqb, dkf, dvb,
            g1, win, gq, wqu, gkv, wkv, tabs)
        d_win = _tn_matmul(dp, u, "dw_in")
        d_wqu = _tn_matmul(dqbo, qn, "dw_q_up")
        d_wkv = _tn_matmul(dkvo, kvn, "dw_kv_up")
        weights.attn_grads(l, **dict(zip(ATTN_IN, _unpad_layer(d_win, d_wqu, d_wkv))))
        grads.append(dict(attn_norm=d_g1[0], q_norm=d_gq[0], kv_norm=d_gkv[0], sinks=dsink[:, 0], out_norm_swa=d_ga[0],
                          out_norm_mla=d_gb[0], ffn_norm=d_g2[0]))
    grads = grads[::-1]
    stacked = {k: jnp.stack([g[k] for g in grads]) for k in grads[0]}
    stacked["final_norm"] = d_final[0]
    return loss[0, 0], dh[FRONT + N_META:], dh[FRONT:FRONT + N_META], stacked


MESH = pl.DeviceIdType.MESH
ANY = pl.BlockSpec(memory_space=pl.ANY)


def _place():
    return lax.axis_index("x"), lax.axis_index("y"), lax.axis_index("c")


def _index(x, y, c):
    return 4 * x + 2 * y + c


def _comm_sems(n):
    return [pltpu.SemaphoreType.DMA((n, N_DEV - 1)), pltpu.SemaphoreType.DMA((n, N_DEV - 1)),
            pltpu.SemaphoreType.DMA((n,))]


class _gather_plan:
    def __init__(self, x_refs, out_refs, send_sems, recv_sems, local_sems):
        self.x_refs, self.out_refs = x_refs, out_refs
        self.send_sems, self.recv_sems, self.local_sems = send_sems, recv_sems, local_sems
        self.n = len(x_refs)

    def _where(self):
        x, y, c = _place()
        return (x, y, c), (x, y, 1 - c), [(1 - x, y), (x, 1 - y), (1 - x, 1 - y)], c

    def _copy(self, i, k, block, to, from_input=False):
        slot = self.out_refs[i].at[_index(*block)]
        return pltpu.make_async_remote_copy(
            src_ref=self.x_refs[i] if from_input else slot, dst_ref=slot,
            send_sem=self.send_sems.at[i, k], recv_sem=self.recv_sems.at[i, k], device_id=to, device_id_type=MESH)

    def _mine(self, i, me):
        return pltpu.make_async_copy(self.x_refs[i], self.out_refs[i].at[_index(*me)], self.local_sems.at[i])

    def _first(self, me, sibling, chips, c):
        out = [self._copy(i, 1 + j, me, (*chip, c), True) for j, chip in enumerate(chips) for i in range(self.n)]
        return out + [self._copy(i, 0, me, sibling, True) for i in range(self.n)]

    def start(self):
        me, sibling, chips, c = self._where()
        for i in range(self.n):
            self._mine(i, me).start()
        for cp in self._first(me, sibling, chips, c):
            cp.start()

    def forward(self):
        me, sibling, chips, c = self._where()
        for j, chip in enumerate(chips):
            for i in range(self.n):
                self._copy(i, 1 + j, (*chip, c), me).wait_recv()
                self._copy(i, 4 + j, (*chip, c), sibling).start()

    def finish(self):
        me, sibling, chips, c = self._where()
        for i in range(self.n):
            self._copy(i, 0, sibling, me).wait_recv()
            for j, chip in enumerate(chips):
                self._copy(i, 4 + j, (*chip, 1 - c), me).wait_recv()
        for cp in self._first(me, sibling, chips, c):
            cp.wait_send()
        for j, chip in enumerate(chips):
            for i in range(self.n):
                self._copy(i, 4 + j, (*chip, c), sibling).wait_send()
        for i in range(self.n):
            self._mine(i, me).wait()


class _exchange_plan:
    def __init__(self, in_refs, out_refs, send_sems, recv_sems, local_sems):
        self.in_refs, self.out_refs = in_refs, out_refs
        self.send_sems, self.recv_sems, self.local_sems = send_sems, recv_sems, local_sems
        self.n = len(in_refs)

    def _copies(self):
        x, y, c = _place()
        me = _index(x, y, c)
        mine = [pltpu.make_async_copy(self.in_refs[i].at[me], self.out_refs[i].at[me], self.local_sems.at[i])
                for i in range(self.n)]
        remote = []
        for k in range(1, N_DEV):
            peer = (1 - x if k & 4 else x, 1 - y if k & 2 else y, 1 - c if k & 1 else c)
            remote += [pltpu.make_async_remote_copy(
                src_ref=self.in_refs[i].at[_index(*peer)], dst_ref=self.out_refs[i].at[me],
                send_sem=self.send_sems.at[i, k - 1], recv_sem=self.recv_sems.at[i, k - 1],
                device_id=peer, device_id_type=MESH) for i in range(self.n)]
        return mine, remote

    def start(self):
        mine, remote = self._copies()
        for cp in mine + remote:
            cp.start()

    def finish(self):
        mine, remote = self._copies()
        for cp in remote:
            cp.wait_recv()
        for cp in remote:
            cp.wait_send()
        for cp in mine:
            cp.wait()


def _all_gather(shards, name):
    n = len(shards)

    def body(*refs):
        plan = _gather_plan(refs[:n], refs[n:2 * n], *refs[2 * n:])
        plan.start()
        plan.forward()
        plan.finish()

    return pl.pallas_call(
        body, name=name, in_specs=[ANY] * n, out_specs=[ANY] * n, scratch_shapes=_comm_sems(n),
        out_shape=[jax.ShapeDtypeStruct((N_DEV,) + a.shape, a.dtype) for a in shards],
    )(*shards)


def _exchange(slabs, name):
    n = len(slabs)

    def body(*refs):
        plan = _exchange_plan(refs[:n], refs[n:2 * n], *refs[2 * n:])
        plan.start()
        plan.finish()

    return pl.pallas_call(
        body, name=name, in_specs=[ANY] * n, out_specs=[ANY] * n, scratch_shapes=_comm_sems(n),
        out_shape=[jax.ShapeDtypeStruct(a.shape, a.dtype) for a in slabs],
    )(*slabs)


def _adamw(w, g, m, v):
    m = ADAM_B1 * m + (1.0 - ADAM_B1) * g
    v = ADAM_B2 * v + (1.0 - ADAM_B2) * (g * g)
    m_hat = m / (1.0 - ADAM_B1 ** ADAM_STEP)
    v_hat = v / (1.0 - ADAM_B2 ** ADAM_STEP)
    return -ADAM_LR * (m_hat / (jnp.sqrt(v_hat) + ADAM_EPS) + ADAM_WD * w), m, v


def _sum_slots(ref):
    g = ref[0].astype(F32)
    for s in range(1, N_DEV):
        g = g + ref[s].astype(F32)
    return g


def _reduce_adamw(parts, w, m, v, layer, outs, name):
    l, r, c = w.shape
    tile = next(t for t in (256, 128, r) if r % t == 0)

    def body(p_ref, w_ref, m_ref, v_ref, g0, d0, m0, v0, g_ref, d_ref, nm_ref, nv_ref):
        g = _sum_slots(p_ref)
        g_ref[...] = g
        d_ref[...], nm_ref[...], nv_ref[...] = _adamw(w_ref[...], g, m_ref[...], v_ref[...])

    blk = pl.BlockSpec((None, tile, c), lambda j: (layer, j, 0))
    return pl.pallas_call(
        body, name=name, grid=(r // tile,),
        in_specs=[pl.BlockSpec((N_DEV, tile, c), lambda j: (0, j, 0)), blk, blk, blk] + [ANY] * 4, out_specs=[blk] * 4,
        out_shape=[jax.ShapeDtypeStruct((l, r, c), F32)] * 4,
        input_output_aliases={4: 0, 5: 1, 6: 2, 7: 3},
        compiler_params=_params("parallel"),
    )(parts, w, m, v, *outs)


def _sum_parts(parts, name):
    _, r, c = parts.shape

    def body(p_ref, g_ref):
        g_ref[...] = _sum_slots(p_ref)

    return pl.pallas_call(body, name=name, out_shape=jax.ShapeDtypeStruct((r, c), F32))(parts)


def _adamw_call(w, g, m, v, name):
    def body(w_ref, g_ref, m_ref, v_ref, d_ref, nm_ref, nv_ref):
        d_ref[...], nm_ref[...], nv_ref[...] = _adamw(w_ref[...], g_ref[...], m_ref[...], v_ref[...])

    return pl.pallas_call(body, name=name, out_shape=[jax.ShapeDtypeStruct(w.shape, F32)] * 3)(w, g, m, v)


ATTN_IN = ("w_in", "w_q_up", "w_kv_up")
ATTN = ATTN_IN + ("w_o",)
FFN = ("w_gate", "w_up", "w_down")
TRANSPOSED = ("w_in", "w_q_up", "w_kv_up", "w_gate", "w_up")
SMALL = ("attn_norm", "ffn_norm", "final_norm", "out_norm_swa", "out_norm_mla", "q_norm", "kv_norm", "sinks")
PACK_W = 1024
SMALL_ROWS = 16


def _pack(arrs, dtype):
    flat = jnp.concatenate([a.astype(dtype).reshape(-1) for a in arrs])
    return flat.reshape(-1, PACK_W)


def _unpack(packed, like):
    flat = packed.reshape(-1)
    out, off = [], 0
    for a in like:
        out.append(flat[off:off + a.size].reshape(a.shape))
        off += a.size
    return out


def _gather_to_full(gathered):
    return gathered.reshape((-1,) + gathered.shape[2:])


def _full_to_slabs(full):
    return full.reshape((N_DEV, -1) + full.shape[1:])


class _ShardedWeights:
    def __init__(self, shards, depth):
        self.shards, self.depth = shards, depth
        self.gathered, self.pending, self.parts = {}, {}, {}
        self._gather([(n, 0) for n in ATTN_IN], lambda xs: _all_gather(xs, "gather_attn0"))

    def _gather(self, keys, run):
        self.gathered.update(zip(keys, run([self.shards[n][l] for n, l in keys])))

    def _full(self, names, l):
        return tuple(_gather_to_full(self.gathered[n, l]) for n in names)

    def attn_in(self, l):
        return self._full(ATTN_IN, l)

    def w_o(self, l):
        return self._full(("w_o",), l)[0]

    def ffn(self, l):
        return self._full(FFN, l)

    def mla_fwd(self, l, q, k, v):
        keys = [(n, l) for n in ("w_o",) + FFN] + ([(n, l + 1) for n in ATTN_IN] if l + 1 < self.depth else [])
        out = []
        self._gather(keys, lambda xs: out.extend(_mla_fwd(q, k, v, xs)) or out[2])
        return out[0], out[1]

    def _add(self, names, l, grads):
        for n, g in zip(names, grads):
            self.pending[n, l] = _full_to_slabs(g)

    def ffn_grads(self, l, *grads):
        self._add(FFN, l, grads)

    def attn_grads(self, l, **grads):
        self._add(list(grads), l, grads.values())

    def _exchange(self, run):
        keys = list(self.pending)
        self.parts.update(zip(keys, run([self.pending.pop(k) for k in keys])))

    def mla_bwd(self, l, *args):
        out = []
        self._exchange(lambda xs: out.extend(_mla_bwd(*args, xs)) or out[3])
        return out[0], out[1], out[2]

    def flush(self):
        self._exchange(lambda xs: _exchange(xs, "exchange_attn0"))


def kernel(x, meta_tokens, attn_norm, w_in, q_norm, w_q_up, kv_norm, w_kv_up, sinks, out_norm_swa, out_norm_mla, w_o, ffn_norm, w_gate, w_up, w_down, final_norm, loss_target, m_meta_tokens, m_attn_norm, m_w_in, m_q_norm, m_w_q_up, m_kv_norm, m_w_kv_up, m_sinks, m_out_norm_swa, m_out_norm_mla, m_w_o, m_ffn_norm, m_w_gate, m_w_up, m_w_down, m_final_norm, v_meta_tokens, v_attn_norm, v_w_in, v_q_norm, v_w_q_up, v_kv_norm, v_w_kv_up, v_sinks, v_out_norm_swa, v_out_norm_mla, v_w_o, v_ffn_norm, v_w_gate, v_w_up, v_w_down, v_final_norm):
    w = dict(meta_tokens=meta_tokens, attn_norm=attn_norm, w_in=w_in, q_norm=q_norm, w_q_up=w_q_up, kv_norm=kv_norm,
             w_kv_up=w_kv_up, sinks=sinks, out_norm_swa=out_norm_swa, out_norm_mla=out_norm_mla, w_o=w_o,
             ffn_norm=ffn_norm, w_gate=w_gate, w_up=w_up, w_down=w_down, final_norm=final_norm)
    m = dict(meta_tokens=m_meta_tokens, attn_norm=m_attn_norm, w_in=m_w_in, q_norm=m_q_norm, w_q_up=m_w_q_up,
             kv_norm=m_kv_norm, w_kv_up=m_w_kv_up, sinks=m_sinks, out_norm_swa=m_out_norm_swa,
             out_norm_mla=m_out_norm_mla, w_o=m_w_o, ffn_norm=m_ffn_norm, w_gate=m_w_gate, w_up=m_w_up,
             w_down=m_w_down, final_norm=m_final_norm)
    v = dict(meta_tokens=v_meta_tokens, attn_norm=v_attn_norm, w_in=v_w_in, q_norm=v_q_norm, w_q_up=v_w_q_up,
             kv_norm=v_kv_norm, w_kv_up=v_w_kv_up, sinks=v_sinks, out_norm_swa=v_out_norm_swa,
             out_norm_mla=v_out_norm_mla, w_o=v_w_o, ffn_norm=v_ffn_norm, w_gate=v_w_gate, w_up=v_w_up,
             w_down=v_w_down, final_norm=v_final_norm)
    names = list(w)
    big = ATTN + FFN
    depth = w_in.shape[0]
    me = _index(*_place())

    as_held = lambda n, a: jnp.swapaxes(a, 1, 2) if n in TRANSPOSED else a
    weights = _ShardedWeights({n: as_held(n, w[n]).astype(BF16) for n in big}, depth)
    meta = jnp.moveaxis(_all_gather([meta_tokens], "gather_meta")[0], 0, 1).reshape(N_META, D_MODEL)
    loss, grad_x, d_meta, grads = _train_example(x[0], loss_target[0], meta, {n: w[n] for n in SMALL}, weights)
    weights.flush()

    g_big, d_big, m_big, v_big = {}, {}, {}, {}
    for n in big:
        held = [as_held(n, a) for a in (w[n], m[n], v[n])]
        outs = [lax.empty(held[0].shape, F32) for _ in range(4)]
        for l in reversed(range(depth)):
            outs = _reduce_adamw(weights.parts[n, l], *held, l, outs, "reduce_adamw_" + n)
        g_big[n], d_big[n], m_big[n], v_big[n] = [as_held(n, a) for a in outs]

    small = [grads[n] for n in SMALL] + [loss.reshape(1)]
    pad = SMALL_ROWS * PACK_W - sum(a.size for a in small)
    part = jnp.concatenate([_pack(small + [jnp.zeros((pad,), F32)], F32), d_meta], axis=0)
    total = _sum_parts(_all_gather([part], "gather_small")[0], "sum_small")
    small_w = [w[n] for n in SMALL]
    packs = [_pack([d[n] for n in SMALL] + [jnp.zeros((pad + 1,), F32)], F32) for d in (w, m, v)]
    upd = _adamw_call(packs[0], total[:SMALL_ROWS], packs[1], packs[2], "adamw_small")
    g_small, d_small, m_small, v_small = [dict(zip(SMALL, _unpack(p, small_w))) for p in (total[:SMALL_ROWS],) + tuple(upd)]
    loss_total = total[:SMALL_ROWS].reshape(-1)[SMALL_ROWS * PACK_W - pad - 1]
    g_meta = lax.dynamic_slice_in_dim(total[SMALL_ROWS:], me * LANE, LANE, axis=1)
    d_mt, m_mt, v_mt = _adamw_call(meta_tokens, g_meta, m_meta_tokens, v_meta_tokens, "adamw_meta")

    outs = []
    for got in ({**g_big, **g_small, "meta_tokens": g_meta}, {**d_big, **d_small, "meta_tokens": d_mt},
                {**m_big, **m_small, "meta_tokens": m_mt}, {**v_big, **v_small, "meta_tokens": v_mt}):
        outs += [got[n] for n in names]
    return (loss_total, grad_x[None], *outs)
```

```python
import jax
import jax.numpy as jnp
from jax import lax
from jax.experimental import pallas as pl
from jax.experimental.pallas import tpu as pltpu

F32 = jnp.float32
BF16 = jnp.bfloat16

D_MODEL = 1024
N_META = 16
BLOCK = 128
FRONT = (-N_META) % BLOCK
ROPE_THETA = 10000.0
EPS = 1e-6
NEG = -1e30
SWA_HEADS = 8
SWA_KV_HEADS = 2
SWA_GROUP = SWA_HEADS // SWA_KV_HEADS
SWA_HEAD_DIM = 64
MLA_HEADS = 8
MLA_Q_RANK = 256
MLA_KV_RANK = 128
MLA_NOPE_DIM = 64
MLA_ROPE_DIM = 32
MLA_V_DIM = 64
MLA_QK_DIM = MLA_NOPE_DIM + MLA_ROPE_DIM
SWA_Q_W = SWA_HEADS * SWA_HEAD_DIM
SWA_KV_W = SWA_KV_HEADS * SWA_HEAD_DIM
MLA_OUT_W = MLA_HEADS * MLA_V_DIM
SCALE_A = SWA_HEAD_DIM ** -0.5
SCALE_B = MLA_QK_DIM ** -0.5
LOG2E = 1.4426950408889634
Q_SCALE = SCALE_B * LOG2E
ADAM_LR = 0.001
ADAM_B1 = 0.9
ADAM_B2 = 0.999
ADAM_EPS = 1e-08
ADAM_WD = 0.01
ADAM_STEP = 10

LANE = 128
N_DEV = 8
HP = 8 * LANE
PO_QA, PO_KA, PO_VA = 0, HP, HP + 2 * LANE
PO_CQ = PO_VA + 2 * LANE
PO_CKV = PO_CQ + MLA_Q_RANK
PO_KR = PO_CKV + MLA_KV_RANK
PW_IN = PO_KR + LANE
N_TAB = 7
VMEM_LIMIT = 56 * 2 ** 20
TN_VMEM_BUDGET = 36 * 2 ** 20
MLA_HB = 4
HALF = LANE // 2
assert SWA_HEAD_DIM == HALF and MLA_V_DIM == HALF

NT = (((1,), (1,)), ((), ()))
TN = (((0,), (0,)), ((), ()))


def _tile(t):
    return 384 if t % 384 == 0 else 128


def _params(*sem):
    return pltpu.CompilerParams(dimension_semantics=sem, vmem_limit_bytes=VMEM_LIMIT)


def _row(tm, n):
    return pl.BlockSpec((tm, n), lambda i: (i, 0))


def _const(shape):
    return pl.BlockSpec(shape, lambda i: (0,) * len(shape))


def _dot(a, b):
    return jnp.dot(a, b, preferred_element_type=F32)


def _dot_nt(a, b):
    return lax.dot_general(a, b, NT, preferred_element_type=F32)


def _dot_tn(a, b):
    return lax.dot_general(a, b, TN, preferred_element_type=F32)


def _rope(x, c, s1, s2, shift):
    return x * c + pltpu.roll(x, LANE - shift, 1) * s1 + pltpu.roll(x, shift, 1) * s2


def _rope_t(dy, c, s1, s2, shift):
    return dy * c + pltpu.roll(dy * s1, shift, 1) + pltpu.roll(dy * s2, LANE - shift, 1)


def _rms_r(x, n):
    return lax.rsqrt(jnp.sum(x * x, axis=-1, keepdims=True) * (1.0 / n) + EPS)


def _rms_bwd(x, g, dy, n):
    r = _rms_r(x, n)
    xh = x * r
    dxh = dy * g
    dx = r * (dxh - xh * (jnp.sum(dxh * xh, axis=-1, keepdims=True) * (1.0 / n)))
    return dx, jnp.sum(dy * xh, axis=0, keepdims=True)


def _acc(ref, val, first):
    @pl.when(first)
    def _():
        ref[...] = val

    @pl.when(jnp.logical_not(first))
    def _():
        ref[...] += val


def _pack_pair(even, odd):
    return even + pltpu.roll(odd, HALF, 1)


def _pair_half(slab, half):
    return slab if half == 0 else pltpu.roll(slab, HALF, 1)


def _unpack_pair(slab, half):
    x = _pair_half(slab, half)
    return jnp.where(lax.broadcasted_iota(jnp.int32, x.shape, 1) < HALF, x, 0.0)


def _tabs(tab_ref):
    return [tab_ref[:, LANE * i:LANE * (i + 1)] for i in range(N_TAB)]


def _pre_fwd(h, g1, win, gq, wqu, gkv, wkv, tabs):
    t = h.shape[0]
    tm = _tile(t)

    def body(h_ref, g1_ref, win_ref, gq_ref, wqu_ref, gkv_ref, wkv_ref, tab_ref,
             u_ref, qa_ref, ka_ref, va_ref, cq_ref, ckv_ref, qn_ref, kvn_ref, qb_ref, kf_ref, vb_ref):
        ca, sa1, sa2, cb, sb1, sb2, ck = _tabs(tab_ref)
        hv = h_ref[...]
        u = (hv * _rms_r(hv, D_MODEL) * g1_ref[...]).astype(BF16)
        u_ref[...] = u
        p = _dot_nt(u, win_ref[...])
        for c in range(SWA_HEADS):
            sl = slice(LANE * c, LANE * (c + 1))
            qa_ref[:, sl] = _rope(p[:, PO_QA + LANE * c:PO_QA + LANE * (c + 1)], ca, sa1, sa2, 32).astype(BF16)
        for c in range(SWA_KV_HEADS):
            sl = slice(LANE * c, LANE * (c + 1))
            ka_ref[:, sl] = _rope(p[:, PO_KA + LANE * c:PO_KA + LANE * (c + 1)], ca, sa1, sa2, 32).astype(BF16)
        va_ref[...] = p[:, PO_VA:PO_CQ].astype(BF16)
        cq = p[:, PO_CQ:PO_CKV]
        ckv = p[:, PO_CKV:PO_KR]
        cq_ref[...] = cq
        ckv_ref[...] = ckv
        qn = (cq * _rms_r(cq, MLA_Q_RANK) * gq_ref[...]).astype(BF16)
        qn_ref[...] = qn
        qb = _dot_nt(qn, wqu_ref[...])
        kvn = (ckv * _rms_r(ckv, MLA_KV_RANK) * gkv_ref[...]).astype(BF16)
        kvn_ref[...] = kvn
        kv = _dot_nt(kvn, wkv_ref[...])
        kr = _rope(p[:, PO_KR:PW_IN], ck, sb1, sb2, 16)
        for c in range(MLA_HEADS):
            sl = slice(LANE * c, LANE * (c + 1))
            qb_ref[:, sl] = (_rope(qb[:, sl], cb, sb1, sb2, 16) * Q_SCALE).astype(BF16)
            kf_ref[:, sl] = (kv[:, sl] + kr).astype(BF16)
        vb_ref[...] = kv[:, HP:].astype(BF16)

    widths = [(D_MODEL, BF16), (HP, BF16), (2 * LANE, BF16), (2 * LANE, BF16), (MLA_Q_RANK, F32),
              (MLA_KV_RANK, F32), (MLA_Q_RANK, BF16), (MLA_KV_RANK, BF16), (HP, BF16), (HP, BF16), (HP, BF16)]
    return pl.pallas_call(
        body, name="pre_fwd", grid=(t // tm,),
        in_specs=[_row(tm, D_MODEL), _const(g1.shape), _const(win.shape), _const(gq.shape), _const(wqu.shape),
                  _const(gkv.shape), _const(wkv.shape), _row(tm, N_TAB * LANE)],
        out_specs=[_row(tm, w) for w, _ in widths],
        out_shape=[jax.ShapeDtypeStruct((t, w), d) for w, d in widths],
        compiler_params=_params("parallel"),
    )(h, g1, win, gq, wqu, gkv, wkv, tabs)


def _swa_mask(nb):
    key = lax.broadcasted_iota(jnp.int32, (2 * BLOCK, SWA_GROUP * BLOCK), 0)
    qry = lax.broadcasted_iota(jnp.int32, (2 * BLOCK, SWA_GROUP * BLOCK), 1) & (BLOCK - 1)
    return (key > qry) & (key <= qry + BLOCK) & (key + (nb - 1) * BLOCK >= FRONT)


def _swa_group(ref, rows, j):
    return jnp.concatenate([ref[rows, LANE * (SWA_GROUP * j + g):LANE * (SWA_GROUP * j + g + 1)]
                            for g in range(SWA_GROUP)], axis=0)


def _swa_packed_group(ref, rows, j):
    heads = [SWA_GROUP * j + g for g in range(SWA_GROUP)]
    return jnp.concatenate([_pair_half(ref[rows, LANE * (hd // 2):LANE * (hd // 2 + 1)], hd % 2) for hd in heads], axis=0)


def _swa_sinks(sink_ref, j):
    return jnp.concatenate([jnp.full((1, BLOCK), sink_ref[0, SWA_GROUP * j + g], F32) for g in range(SWA_GROUP)], axis=1)


def _swa_keys(prev_ref, cur_ref, rb, j):
    sl = slice(LANE * j, LANE * (j + 1))
    if rb == 0:
        return jnp.concatenate([prev_ref[:, sl], cur_ref[:BLOCK, sl]], axis=0)
    return cur_ref[BLOCK * (rb - 1):BLOCK * (rb + 1), sl]


def _swa_chains(t):
    return [(rb, j) for rb in range(_tile(t) // BLOCK) for j in range(SWA_KV_HEADS)]


def _swa_scores(sink_ref, q_ref, kp_ref, kc_ref, n, t):
    r = _tile(t) // BLOCK
    chains = _swa_chains(t)
    qs = [_swa_group(q_ref, slice(BLOCK * rb, BLOCK * (rb + 1)), j) for rb, j in chains]
    ks = [_swa_keys(kp_ref, kc_ref, rb, j) for rb, j in chains]
    ss = [_dot_nt(k2, q4) for q4, k2 in zip(qs, ks)]
    masks = [_swa_mask(n * r + rb) for rb in range(r)]
    out = []
    for (rb, j), s in zip(chains, ss):
        sink = _swa_sinks(sink_ref, j)
        s = jnp.where(masks[rb], s * SCALE_A, NEG)
        m = jnp.maximum(jnp.max(s, axis=0, keepdims=True), sink)
        e = jnp.exp(s - m)
        es = jnp.exp(sink - m)
        inv = 1.0 / (jnp.sum(e, axis=0, keepdims=True) + es)
        out.append((e * inv, es * inv))
    return qs, ks, out


def _swa_specs(t):
    ts = _tile(t)
    r = ts // BLOCK
    prev = lambda n: (jnp.maximum(n * r - 1, 0), 0)
    cur = lambda n: (n, 0)
    return [pl.BlockSpec(memory_space=pltpu.SMEM), pl.BlockSpec((ts, HP), cur),
            pl.BlockSpec((BLOCK, 2 * LANE), prev), pl.BlockSpec((ts, 2 * LANE), cur),
            pl.BlockSpec((BLOCK, 2 * LANE), prev), pl.BlockSpec((ts, 2 * LANE), cur)]


def _swa_fwd(sinks, q, k, v):
    t = q.shape[0]
    ts = _tile(t)

    def body(sink_ref, q_ref, kp_ref, kc_ref, vp_ref, vc_ref, o_ref):
        chains = _swa_chains(t)
        _, _, probs = _swa_scores(sink_ref, q_ref, kp_ref, kc_ref, pl.program_id(0), t)
        os_ = [_dot_tn(p.astype(BF16), _swa_keys(vp_ref, vc_ref, rb, j)) for (rb, j), (p, _) in zip(chains, probs)]
        for (rb, j), o4 in zip(chains, os_):
            for g in range(0, SWA_GROUP, 2):
                pair = (SWA_GROUP * j + g) // 2
                o_ref[BLOCK * rb:BLOCK * (rb + 1), LANE * pair:LANE * (pair + 1)] = _pack_pair(
                    o4[BLOCK * g:BLOCK * (g + 1)], o4[BLOCK * (g + 1):BLOCK * (g + 2)])

    return pl.pallas_call(
        body, name="swa_fwd", grid=(t // ts,),
        in_specs=_swa_specs(t),
        out_specs=pl.BlockSpec((ts, SWA_Q_W), lambda n: (n, 0)),
        out_shape=jax.ShapeDtypeStruct((t, SWA_Q_W), F32),
        compiler_params=_params("parallel"),
    )(sinks, q, k, k, v, v)


def _causal_mask(q0, k0, tq, tk, transposed):
    if transposed:
        key = k0 + lax.broadcasted_iota(jnp.int32, (tk, tq), 0)
        qry = q0 + lax.broadcasted_iota(jnp.int32, (tk, tq), 1)
    else:
        qry = q0 + lax.broadcasted_iota(jnp.int32, (tq, tk), 0)
        key = k0 + lax.broadcasted_iota(jnp.int32, (tq, tk), 1)
    return (key <= qry) & (key >= FRONT)


def _heads(ref, hb, rows=slice(None)):
    return [ref[rows, LANE * a:LANE * (a + 1)] for a in range(hb)]


def _head_stats(t):
    return jax.ShapeDtypeStruct((MLA_HEADS // MLA_HB, t, MLA_HB), F32)


def _mla_fwd(q, k, v, shards=()):
    t = q.shape[0]
    tq = _tile(t)
    nq = t // tq
    n = len(shards)
    steps = (MLA_HEADS // MLA_HB) * nq

    def body(q_ref, k_ref, v_ref, *rest):
        x_refs, (o_ref, lse_ref), out_refs = rest[:n], rest[n:n + 2], rest[n + 2:2 * n + 2]
        acc_sc, sems = rest[2 * n + 2], rest[2 * n + 3:]
        i = pl.program_id(1)
        step_id = pl.program_id(0) * nq + i
        if n:
            plan = _gather_plan(x_refs, out_refs, *sems)
            pl.when(step_id == 0)(plan.start)
            pl.when(step_id == steps // 2)(plan.forward)
        qs = _heads(q_ref, MLA_HB)
        acc_sc[...] = jnp.zeros(acc_sc.shape, F32)

        def step(j, carry, masked):
            rows = pl.ds(pl.multiple_of(j * tq, tq), tq)
            ks, vs = _heads(k_ref, MLA_HB, rows), _heads(v_ref, MLA_HB, rows)
            ss = [_dot_nt(kh, qh) for qh, kh in zip(qs, ks)]
            if masked:
                mask = _causal_mask(i * tq, j * tq, tq, tq, True)
                ss = [jnp.where(mask, s, NEG) for s in ss]
            mid, out = [], []
            for s, (m, l) in zip(ss, carry):
                mn = jnp.maximum(m, jnp.max(s, axis=0, keepdims=True))
                al = jnp.exp2(m - mn)
                p = jnp.exp2(s - mn)
                out.append((mn, al * l + jnp.sum(p, axis=0, keepdims=True)))
                mid.append((al, p.astype(BF16)))
            for a, ((al, p), vh) in enumerate(zip(mid, vs)):
                acc_sc[a] = al * acc_sc[a] + _dot_tn(vh, p)
            return tuple(out)

        init = ((jnp.full((1, tq), NEG, F32), jnp.zeros((1, tq), F32)),) * MLA_HB
        carry = lax.fori_loop(0, jnp.minimum(i, 1) + 1, lambda it, c: step(it * i, c, True), init)
        carry = lax.fori_loop(1, i, lambda j, c: step(j, c, False), carry)
        outs = [(acc_sc[a] * (1.0 / l)).T for a, (_, l) in enumerate(carry)]
        for a in range(0, MLA_HB, 2):
            o_ref[:, HALF * a:HALF * (a + 2)] = _pack_pair(outs[a], outs[a + 1])
        for a, (m, l) in enumerate(carry):
            lse_ref[:, a:a + 1] = jnp.broadcast_to(m + jnp.log2(l), (LANE, tq)).T[:, :1]
        if n:
            pl.when(step_id == steps - 1)(plan.finish)

    blk = pl.BlockSpec((tq, MLA_HB * LANE), lambda h, i: (i, h))
    full = pl.BlockSpec((t, MLA_HB * LANE), lambda h, i: (0, h))
    packed = pl.BlockSpec((tq, MLA_HB * HALF), lambda h, i: (i, h))
    out = pl.pallas_call(
        body, name="mla_fwd_gather" if n else "mla_fwd", grid=(MLA_HEADS // MLA_HB, nq),
        in_specs=[blk, full, full] + [ANY] * n,
        out_specs=[packed, pl.BlockSpec((None, tq, MLA_HB), lambda h, i: (h, i, 0))] + [ANY] * n,
        out_shape=[jax.ShapeDtypeStruct((t, MLA_OUT_W), F32), _head_stats(t)]
        + [jax.ShapeDtypeStruct((N_DEV,) + a.shape, a.dtype) for a in shards],
        scratch_shapes=[pltpu.VMEM((MLA_HB, LANE, tq), F32)] + (_comm_sems(n) if n else []),
        compiler_params=_params("arbitrary", "arbitrary"),
    )(q, k, v, *shards)
    return out[0], out[1], out[2:]


def _mix_fwd(h, oa, ob, ga, gb, wo, g2):
    t = h.shape[0]
    tm = _tile(t)

    def body(h_ref, oa_ref, ob_ref, ga_ref, gb_ref, wo_ref, g2_ref, h2_ref, mix_ref, u2_ref):
        oa_v = oa_ref[...]
        ob_v = ob_ref[...]
        na = (oa_v * _rms_r(oa_v, SWA_Q_W) * ga_ref[...]).astype(BF16)
        nb = (ob_v * _rms_r(ob_v, MLA_OUT_W) * gb_ref[...]).astype(BF16)
        mix_ref[:, :SWA_Q_W] = na
        mix_ref[:, SWA_Q_W:] = nb
        h2 = h_ref[...] + _dot(na, wo_ref[:SWA_Q_W, :]) + _dot(nb, wo_ref[SWA_Q_W:, :])
        h2_ref[...] = h2
        u2_ref[...] = (h2 * _rms_r(h2, D_MODEL) * g2_ref[...]).astype(BF16)

    mix_w = SWA_Q_W + MLA_OUT_W
    return pl.pallas_call(
        body, name="mix_fwd", grid=(t // tm,),
        in_specs=[_row(tm, D_MODEL), _row(tm, SWA_Q_W), _row(tm, MLA_OUT_W), _const(ga.shape), _const(gb.shape),
                  _const(wo.shape), _const(g2.shape)],
        out_specs=[_row(tm, D_MODEL), _row(tm, mix_w), _row(tm, D_MODEL)],
        out_shape=[jax.ShapeDtypeStruct((t, D_MODEL), F32), jax.ShapeDtypeStruct((t, mix_w), BF16),
                   jax.ShapeDtypeStruct((t, D_MODEL), BF16)],
        compiler_params=_params("parallel"),
    )(h, oa, ob, ga, gb, wo, g2)


def _ffn_fwd(h2, u2, wg_t, wu_t, wd):
    t = h2.shape[0]
    tm = _tile(t)
    dff = wd.shape[0]

    def body(h2_ref, u2_ref, wg_ref, wu_ref, wd_ref, h3_ref, g_ref, up_ref):
        u2v = u2_ref[...]
        g = _dot_nt(u2v, wg_ref[...])
        up = _dot_nt(u2v, wu_ref[...])
        g_ref[...] = g.astype(BF16)
        up_ref[...] = up.astype(BF16)
        a = (g * jax.nn.sigmoid(g) * up).astype(BF16)
        h3_ref[...] = h2_ref[...] + _dot(a, wd_ref[...])

    return pl.pallas_call(
        body, name="ffn_fwd", grid=(t // tm,),
        in_specs=[_row(tm, D_MODEL), _row(tm, D_MODEL), _const(wg_t.shape), _const(wu_t.shape), _const(wd.shape)],
        out_specs=[_row(tm, D_MODEL), _row(tm, dff), _row(tm, dff)],
        out_shape=[jax.ShapeDtypeStruct((t, D_MODEL), F32), jax.ShapeDtypeStruct((t, dff), BF16),
                   jax.ShapeDtypeStruct((t, dff), BF16)],
        compiler_params=_params("parallel"),
    )(h2, u2, wg_t, wu_t, wd)


def _loss_bwd(h, gf, target):
    t = h.shape[0]
    tm = _tile(t)
    first_row = FRONT + N_META

    def body(h_ref, gf_ref, t_ref, dh_ref, dgf_ref, loss_ref):
        i = pl.program_id(0)
        hv = h_ref[...]
        y = hv * _rms_r(hv, D_MODEL) * gf_ref[...]
        row = i * tm + lax.broadcasted_iota(jnp.int32, (tm, 1), 0)
        err = jnp.where(row >= first_row, y - t_ref[...], 0.0)
        dx, dg = _rms_bwd(hv, gf_ref[...], err * (1.0 / D_MODEL), D_MODEL)
        dh_ref[...] = dx
        _acc(dgf_ref, dg, i == 0)
        part = 0.5 * jnp.sum(jnp.sum(err * err, axis=1, keepdims=True) * (1.0 / D_MODEL), axis=0, keepdims=True)
        _acc(loss_ref, jnp.broadcast_to(part, (1, LANE)), i == 0)

    return pl.pallas_call(
        body, name="loss_bwd", grid=(t // tm,),
        in_specs=[_row(tm, D_MODEL), _const(gf.shape), _row(tm, D_MODEL)],
        out_specs=[_row(tm, D_MODEL), _const((1, D_MODEL)), _const((1, LANE))],
        out_shape=[jax.ShapeDtypeStruct((t, D_MODEL), F32), jax.ShapeDtypeStruct((1, D_MODEL), F32),
                   jax.ShapeDtypeStruct((1, LANE), F32)],
        compiler_params=_params("arbitrary"),
    )(h, gf, target)


def _tn_matmul(a, b, name, cols=None):
    t, n = b.shape
    first, k = cols or (0, a.shape[1])
    tk = next(c for c in (k, 1024, 512, 256, 128) if k % c == 0 and first % c == 0 and c <= 1024)
    fits = lambda c: 2 * (t * (tk + c) * 2 + tk * c * 2) <= TN_VMEM_BUDGET
    tn = next(c for c in (n, 1024, 512, 256, 128) if n % c == 0 and fits(c))

    def body(a_ref, b_ref, o_ref):
        o_ref[...] = _dot_tn(a_ref[...], b_ref[...]).astype(BF16)

    return pl.pallas_call(
        body, name=name, grid=(k // tk, n // tn),
        in_specs=[pl.BlockSpec((t, tk), lambda i, j: (0, i + first // tk)), pl.BlockSpec((t, tn), lambda i, j: (0, j))],
        out_specs=pl.BlockSpec((tk, tn), lambda i, j: (i, j)),
        out_shape=jax.ShapeDtypeStruct((k, n), BF16),
        compiler_params=_params("parallel", "parallel"),
    )(a, b)


def _ffn_bwd_a(dh3, g, up, wd):
    t = dh3.shape[0]
    tm = _tile(t)
    dff = wd.shape[0]

    def body(dh3_ref, g_ref, up_ref, wd_ref, a_ref, dgu_ref, dh3b_ref):
        dh3b = dh3_ref[...].astype(BF16)
        dh3b_ref[...] = dh3b
        da = _dot_nt(dh3b, wd_ref[...])
        gv = g_ref[...].astype(F32)
        upv = up_ref[...].astype(F32)
        sg = jax.nn.sigmoid(gv)
        silu = gv * sg
        a_ref[...] = (silu * upv).astype(BF16)
        dgu_ref[:, :dff] = (da * upv * (sg * (1.0 + gv * (1.0 - sg)))).astype(BF16)
        dgu_ref[:, dff:] = (da * silu).astype(BF16)

    return pl.pallas_call(
        body, name="ffn_bwd_a", grid=(t // tm,),
        in_specs=[_row(tm, D_MODEL), _row(tm, dff), _row(tm, dff), _const(wd.shape)],
        out_specs=[_row(tm, dff), _row(tm, 2 * dff), _row(tm, D_MODEL)],
        out_shape=[jax.ShapeDtypeStruct((t, dff), BF16), jax.ShapeDtypeStruct((t, 2 * dff), BF16),
                   jax.ShapeDtypeStruct((t, D_MODEL), BF16)],
        compiler_params=_params("parallel"),
    )(dh3, g, up, wd)


def _ffn_bwd_b(dh3, dgu, h2, g2, wg_t, wu_t):
    t = dh3.shape[0]
    tm = _tile(t)
    dff = wg_t.shape[0]

    def body(dh3_ref, dgu_ref, h2_ref, g2_ref, wg_ref, wu_ref, dh2_ref, dh2b_ref, dg2_ref):
        du2 = _dot(dgu_ref[:, :dff], wg_ref[...]) + _dot(dgu_ref[:, dff:], wu_ref[...])
        dx, dg = _rms_bwd(h2_ref[...], g2_ref[...], du2, D_MODEL)
        dh2 = dh3_ref[...] + dx
        dh2_ref[...] = dh2
        dh2b_ref[...] = dh2.astype(BF16)
        _acc(dg2_ref, dg, pl.program_id(0) == 0)

    return pl.pallas_call(
        body, name="ffn_bwd_b", grid=(t // tm,),
        in_specs=[_row(tm, D_MODEL), _row(tm, 2 * dff), _row(tm, D_MODEL), _const(g2.shape), _const(wg_t.shape),
                  _const(wu_t.shape)],
        out_specs=[_row(tm, D_MODEL), _row(tm, D_MODEL), _const((1, D_MODEL))],
        out_shape=[jax.ShapeDtypeStruct((t, D_MODEL), F32), jax.ShapeDtypeStruct((t, D_MODEL), BF16),
                   jax.ShapeDtypeStruct((1, D_MODEL), F32)],
        compiler_params=_params("arbitrary"),
    )(dh3, dgu, h2, g2, wg_t, wu_t)


def _mix_bwd(dh2, oa, ob, ga, gb, wo):
    t = dh2.shape[0]
    tm = _tile(t)

    def body(dh2_ref, oa_ref, ob_ref, ga_ref, gb_ref, wo_ref, doa_ref, dob_ref, dl_ref, dga_ref, dgb_ref):
        first = pl.program_id(0) == 0
        d = dh2_ref[...]
        ob_v = ob_ref[...]
        dxa, dga = _rms_bwd(oa_ref[...], ga_ref[...], _dot_nt(d, wo_ref[:SWA_Q_W, :]), SWA_Q_W)
        dxb, dgb = _rms_bwd(ob_v, gb_ref[...], _dot_nt(d, wo_ref[SWA_Q_W:, :]), MLA_OUT_W)
        lower = lax.broadcasted_iota(jnp.int32, (tm, LANE), 1) < HALF
        for hd in range(MLA_HEADS):
            sl = slice(LANE * (hd // 2), LANE * (hd // 2 + 1))
            mine = lower if hd % 2 == 0 else jnp.logical_not(lower)
            delta = jnp.sum(jnp.where(mine, ob_v[:, sl] * dxb[:, sl], 0.0), axis=1, keepdims=True)
            dl_ref[hd // MLA_HB, :, hd % MLA_HB:hd % MLA_HB + 1] = delta
        for ref, dx, heads in ((doa_ref, dxa, SWA_HEADS), (dob_ref, dxb, MLA_HEADS)):
            for hd in range(heads):
                slab = dx[:, LANE * (hd // 2):LANE * (hd // 2 + 1)]
                ref[:, LANE * hd:LANE * (hd + 1)] = _unpack_pair(slab, hd % 2).astype(BF16)
        _acc(dga_ref, dga, first)
        _acc(dgb_ref, dgb, first)

    return pl.pallas_call(
        body, name="mix_bwd", grid=(t // tm,),
        in_specs=[_row(tm, D_MODEL), _row(tm, SWA_Q_W), _row(tm, MLA_OUT_W), _const(ga.shape), _const(gb.shape),
                  _const(wo.shape)],
        out_specs=[_row(tm, HP), _row(tm, HP), pl.BlockSpec((MLA_HEADS // MLA_HB, tm, MLA_HB), lambda i: (0, i, 0)),
                   _const((1, SWA_Q_W)), _const((1, MLA_OUT_W))],
        out_shape=[jax.ShapeDtypeStruct((t, HP), BF16), jax.ShapeDtypeStruct((t, HP), BF16), _head_stats(t),
                   jax.ShapeDtypeStruct((1, SWA_Q_W), F32), jax.ShapeDtypeStruct((1, MLA_OUT_W), F32)],
        compiler_params=_params("arbitrary"),
    )(dh2, oa, ob, ga, gb, wo)


def _swa_bwd(sinks, q, k, v, o, do):
    t = q.shape[0]
    ts = _tile(t)

    def body(sink_ref, q_ref, kp_ref, kc_ref, vp_ref, vc_ref, o_ref, do_ref,
             dq_ref, dkc_ref, dkp_ref, dvc_ref, dvp_ref, dsink_ref):
        n = pl.program_id(0)
        chains = _swa_chains(t)
        qs, ks, probs = _swa_scores(sink_ref, q_ref, kp_ref, kc_ref, n, t)
        dos = [_swa_group(do_ref, slice(BLOCK * rb, BLOCK * (rb + 1)), j) for rb, j in chains]
        vs = [_swa_keys(vp_ref, vc_ref, rb, j) for rb, j in chains]
        dps = [_dot_nt(v2, do4) for do4, v2 in zip(dos, vs)]
        dss, dsks = [], []
        for (rb, j), (p, psink), do4, dp in zip(chains, probs, dos, dps):
            o4 = _swa_packed_group(o_ref, slice(BLOCK * rb, BLOCK * (rb + 1)), j)
            delta = jnp.sum(o4 * do4.astype(F32), axis=1, keepdims=True)
            delta = jnp.broadcast_to(delta, (SWA_GROUP * BLOCK, LANE)).T[:1, :]
            dss.append((p * (dp - delta) * SCALE_A).astype(BF16))
            dsks.append(-psink * delta)
        dqs = [_dot_tn(ds, k2) for ds, k2 in zip(dss, ks)]
        dks = [_dot(ds, q4) for ds, q4 in zip(dss, qs)]
        dvs = [_dot(p.astype(BF16), do4) for (p, _), do4 in zip(probs, dos)]
        dsink = [jnp.zeros((1, LANE), F32)] * SWA_HEADS
        ext = {}
        for (rb, j), dq4, dk2, dv2, dsk in zip(chains, dqs, dks, dvs, dsks):
            for g in range(SWA_GROUP):
                hd = SWA_GROUP * j + g
                rows = slice(BLOCK * g, BLOCK * (g + 1))
                dq_ref[BLOCK * rb:BLOCK * (rb + 1), LANE * hd:LANE * (hd + 1)] = dq4[rows].astype(BF16)
                dsink[hd] = dsink[hd] + jnp.sum(dsk[:, rows], axis=1, keepdims=True)
            for half in range(2):
                key = (j, rb + half)
                part = (dk2[BLOCK * half:BLOCK * (half + 1)], dv2[BLOCK * half:BLOCK * (half + 1)])
                ext[key] = part if key not in ext else (ext[key][0] + part[0], ext[key][1] + part[1])
        for (j, blk), (dk, dv) in ext.items():
            sl = slice(LANE * j, LANE * (j + 1))
            if blk == 0:
                dkp_ref[:, sl] = dk
                dvp_ref[:, sl] = dv
            else:
                dkc_ref[BLOCK * (blk - 1):BLOCK * blk, sl] = dk
                dvc_ref[BLOCK * (blk - 1):BLOCK * blk, sl] = dv
        for hd in range(SWA_HEADS):
            _acc(dsink_ref.at[hd:hd + 1, :], jnp.broadcast_to(dsink[hd], (1, LANE)), n == 0)

    cur = lambda n: (n, 0)
    kv = pl.BlockSpec((ts, 2 * LANE), cur)
    kvp = pl.BlockSpec((BLOCK, 2 * LANE), cur)
    hp = pl.BlockSpec((ts, HP), cur)
    kvs = jax.ShapeDtypeStruct((t, 2 * LANE), F32)
    kvps = jax.ShapeDtypeStruct((t // ts * BLOCK, 2 * LANE), F32)
    return pl.pallas_call(
        body, name="swa_bwd", grid=(t // ts,),
        in_specs=_swa_specs(t) + [pl.BlockSpec((ts, SWA_Q_W), cur), hp],
        out_specs=[hp, kv, kvp, kv, kvp, _const((SWA_HEADS, LANE))],
        out_shape=[jax.ShapeDtypeStruct((t, HP), BF16), kvs, kvps, kvs, kvps,
                   jax.ShapeDtypeStruct((SWA_HEADS, LANE), F32)],
        compiler_params=_params("arbitrary"),
    )(sinks, q, k, k, v, v, o, do)


def _mla_bwd(q, k, v, do, lse, dl, slabs=()):
    t = q.shape[0]
    tq = _tile(t)
    nq = t // tq
    n = len(slabs)
    hb = MLA_HB
    steps = (MLA_HEADS // hb) * nq

    def body(k_ref, v_ref, q_ref, do_ref, lse_ref, dl_ref, *rest):
        in_refs, (dq_ref, dk_ref, dv_ref), out_refs = rest[:n], rest[n:n + 3], rest[n + 3:2 * n + 3]
        (dq_sc, dk_sc, dv_sc), sems = rest[2 * n + 3:2 * n + 6], rest[2 * n + 6:]
        j = pl.program_id(1)
        step_id = pl.program_id(0) * nq + j
        if n:
            plan = _exchange_plan(in_refs, out_refs, *sems)
            pl.when(step_id == 0)(plan.start)

        @pl.when(j == 0)
        def _():
            dq_sc[...] = jnp.zeros(dq_sc.shape, F32)

        dk_sc[...] = jnp.zeros(dk_sc.shape, F32)
        dv_sc[...] = jnp.zeros(dv_sc.shape, F32)
        ks, vs = _heads(k_ref, hb), _heads(v_ref, hb)

        def step(i, carry, masked):
            rows = pl.ds(pl.multiple_of(i * tq, tq), tq)
            qs, dos = _heads(q_ref, hb, rows), _heads(do_ref, hb, rows)
            ss = [_dot_nt(qh, kh) for qh, kh in zip(qs, ks)]
            dps = [_dot_nt(doh, vh) for doh, vh in zip(dos, vs)]
            if masked:
                mask = _causal_mask(i * tq, j * tq, tq, tq, False)
                ss = [jnp.where(mask, s_, NEG) for s_ in ss]
            ps = [jnp.exp2(s_ - lse_ref[rows, a:a + 1]) for a, s_ in enumerate(ss)]
            dss = [(p * (dp - dl_ref[rows, a:a + 1])).astype(BF16) for a, (p, dp) in enumerate(zip(ps, dps))]
            for a, (ds, p, qh, kh, doh) in enumerate(zip(dss, ps, qs, ks, dos)):
                dq_sc[a, rows, :] += _dot(ds, kh)
                dk_sc[a] += _dot_tn(ds, qh)
                dv_sc[a] += _dot_tn(p.astype(BF16), doh)
            return carry

        split = jnp.where(j == 0, nq, j + 1)
        lax.fori_loop(j, split, lambda i, c: step(i, c, True), 0)
        lax.fori_loop(split, nq, lambda i, c: step(i, c, False), 0)
        for a in range(hb):
            dk_ref[:, LANE * a:LANE * (a + 1)] = (dk_sc[a] * (1.0 / LOG2E)).astype(BF16)
            dv_ref[:, LANE * a:LANE * (a + 1)] = dv_sc[a].astype(BF16)

        @pl.when(j == nq - 1)
        def _():
            for a in range(hb):
                dq_ref[:, LANE * a:LANE * (a + 1)] = (dq_sc[a] * SCALE_B).astype(BF16)

        if n:
            pl.when(step_id == steps - 1)(plan.finish)

    blk = pl.BlockSpec((tq, hb * LANE), lambda h, j: (j, h))
    full = pl.BlockSpec((t, hb * LANE), lambda h, j: (0, h))
    cols = pl.BlockSpec((None, t, hb), lambda h, j: (h, 0, 0))
    out = pl.pallas_call(
        body, name="mla_bwd_exchange" if n else "mla_bwd", grid=(MLA_HEADS // hb, nq),
        in_specs=[blk, blk, full, full, cols, cols] + [ANY] * n, out_specs=[full, blk, blk] + [ANY] * n,
        out_shape=[jax.ShapeDtypeStruct((t, HP), BF16)] * 3 + [jax.ShapeDtypeStruct(a.shape, a.dtype) for a in slabs],
        scratch_shapes=[pltpu.VMEM((hb, t, LANE), F32)] + [pltpu.VMEM((hb, tq, LANE), F32)] * 2
        + (_comm_sems(n) if n else []),
        compiler_params=_params("arbitrary", "arbitrary"),
    )(k, v, q, do, lse, dl, *slabs)
    return out[0], out[1], out[2], out[3:]


def _pre_bwd(dh2, h, cq, ckv, dqa, dka, dka_next, dva, dva_next, dqb, dkf, dvb, g1, win, gq, wqu, gkv, wkv, tabs):
    t = h.shape[0]
    tm = _tile(t)

    def body(dh2_ref, h_ref, cq_ref, ckv_ref, dqa_ref, dka_ref, dkan_ref, dva_ref, dvan_ref, dqb_ref, dkf_ref, dvb_ref,
             g1_ref, win_ref, gq_ref, wqu_ref, gkv_ref, wkv_ref, tab_ref,
             dh_ref, dp_ref, dqbo_ref, dkvo_ref, dg1_ref, dgq_ref, dgkv_ref):
        first = pl.program_id(0) == 0
        ca, sa1, sa2, cb, sb1, sb2, ck = _tabs(tab_ref)
        dkr = jnp.zeros((tm, LANE), F32)
        for c in range(MLA_HEADS):
            sl = slice(LANE * c, LANE * (c + 1))
            dqbo_ref[:, sl] = _rope_t(dqb_ref[:, sl].astype(F32), cb, sb1, sb2, 16).astype(BF16)
            dkr += dkf_ref[:, sl].astype(F32)
        dkvo_ref[:, :HP] = dkf_ref[...]
        dkvo_ref[:, HP:] = dvb_ref[...]
        dcq, dgq = _rms_bwd(cq_ref[...], gq_ref[...], _dot(dqbo_ref[...], wqu_ref[...]), MLA_Q_RANK)
        dckv, dgkv = _rms_bwd(ckv_ref[...], gkv_ref[...], _dot(dkvo_ref[...], wkv_ref[...]), MLA_KV_RANK)
        for c in range(SWA_HEADS):
            sl = slice(LANE * c, LANE * (c + 1))
            dp_ref[:, PO_QA + LANE * c:PO_QA + LANE * (c + 1)] = _rope_t(dqa_ref[:, sl].astype(F32), ca, sa1, sa2,
                                                                          32).astype(BF16)
        last = slice(tm - BLOCK, tm)
        more = pl.program_id(0) < t // tm - 1
        for c in range(SWA_KV_HEADS):
            sl = slice(LANE * c, LANE * (c + 1))
            dk = dka_ref[:, sl]
            dk_last = dk[tm - BLOCK:] + jnp.where(more, dkan_ref[:, sl], 0.0)
            cols = slice(PO_KA + LANE * c, PO_KA + LANE * (c + 1))
            if tm > BLOCK:
                dp_ref[:tm - BLOCK, cols] = _rope_t(dk[:tm - BLOCK], ca[:tm - BLOCK], sa1[:tm - BLOCK], sa2[:tm - BLOCK],
                                                    32).astype(BF16)
            dp_ref[last, cols] = _rope_t(dk_last, ca[tm - BLOCK:], sa1[tm - BLOCK:], sa2[tm - BLOCK:], 32).astype(BF16)
        if tm > BLOCK:
            dp_ref[:tm - BLOCK, PO_VA:PO_CQ] = dva_ref[:tm - BLOCK, :].astype(BF16)
        dp_ref[last, PO_VA:PO_CQ] = (dva_ref[tm - BLOCK:, :] + jnp.where(more, dvan_ref[...], 0.0)).astype(BF16)
        dp_ref[:, PO_CQ:PO_CKV] = dcq.astype(BF16)
        dp_ref[:, PO_CKV:PO_KR] = dckv.astype(BF16)
        dp_ref[:, PO_KR:PW_IN] = _rope_t(dkr, ck, sb1, sb2, 16).astype(BF16)
        dx, dg1 = _rms_bwd(h_ref[...], g1_ref[...], _dot(dp_ref[...], win_ref[...]), D_MODEL)
        dh_ref[...] = dh2_ref[...] + dx
        _acc(dg1_ref, dg1, first)
        _acc(dgq_ref, dgq, first)
        _acc(dgkv_ref, dgkv, first)

    kv = _row(tm, 2 * LANE)
    nxt = pl.BlockSpec((BLOCK, 2 * LANE), lambda i: (jnp.minimum(i + 1, t // tm - 1), 0))
    return pl.pallas_call(
        body, name="pre_bwd", grid=(t // tm,),
        in_specs=[_row(tm, D_MODEL), _row(tm, D_MODEL), _row(tm, MLA_Q_RANK), _row(tm, MLA_KV_RANK), _row(tm, HP),
                  kv, nxt, kv, nxt, _row(tm, HP), _row(tm, HP), _row(tm, HP),
                  _const(g1.shape), _const(win.shape), _const(gq.shape), _const(wqu.shape), _const(gkv.shape),
                  _const(wkv.shape), _row(tm, N_TAB * LANE)],
        out_specs=[_row(tm, D_MODEL), _row(tm, PW_IN), _row(tm, HP), _row(tm, 2 * HP),
                   _const((1, D_MODEL)), _const((1, MLA_Q_RANK)), _const((1, MLA_KV_RANK))],
        out_shape=[jax.ShapeDtypeStruct((t, D_MODEL), F32), jax.ShapeDtypeStruct((t, PW_IN), BF16),
                   jax.ShapeDtypeStruct((t, HP), BF16), jax.ShapeDtypeStruct((t, 2 * HP), BF16),
                   jax.ShapeDtypeStruct((1, D_MODEL), F32), jax.ShapeDtypeStruct((1, MLA_Q_RANK), F32),
                   jax.ShapeDtypeStruct((1, MLA_KV_RANK), F32)],
        compiler_params=_params("arbitrary"),
    )(dh2, h, cq, ckv, dqa, dka, dka_next, dva, dva_next, dqb, dkf, dvb, g1, win, gq, wqu, gkv, wkv, tabs)


def _rope_tables(t):
    pos = (jnp.arange(t, dtype=jnp.int32) - FRONT).astype(F32)[:, None]
    lane = jnp.arange(LANE)[None, :]

    def table(dim, start):
        half = dim // 2
        inv = ROPE_THETA ** (-jnp.arange(0, dim, 2, dtype=F32) / dim)
        ang = pos * inv[None, :]
        cos = jnp.concatenate([jnp.cos(ang)] * 2, axis=1)
        sin = jnp.concatenate([jnp.sin(ang)] * 2, axis=1)
        pad = lambda a: jnp.pad(a, ((0, 0), (start, LANE - start - dim)))
        first = (lane >= start) & (lane < start + half)
        second = (lane >= start + half) & (lane < start + dim)
        return pad(cos), jnp.where(first, -pad(sin), 0.0), jnp.where(second, pad(sin), 0.0)

    ca, sa1, sa2 = table(SWA_HEAD_DIM, 0)
    ck, sb1, sb2 = table(MLA_ROPE_DIM, MLA_NOPE_DIM)
    cb = jnp.where(lane < MLA_NOPE_DIM, 1.0, ck)
    return jnp.concatenate([ca, sa1, sa2, cb, sb1, sb2, ck], axis=1)


def _pad_heads(w, heads, dim, axis):
    shp = w.shape
    w = w.reshape(shp[:axis] + (heads, dim) + shp[axis + 1:])
    pad = [(0, 0)] * w.ndim
    pad[axis + 1] = (0, LANE - dim)
    return jnp.pad(w, pad).reshape(shp[:axis] + (heads * LANE,) + shp[axis + 1:])


def _unpad_heads(w, heads, dim, axis):
    shp = w.shape
    w = w.reshape(shp[:axis] + (heads, LANE) + shp[axis + 1:])
    w = lax.slice_in_dim(w, 0, dim, axis=axis + 1)
    return w.reshape(shp[:axis] + (heads * dim,) + shp[axis + 1:])


def _pad_layer(w_in, w_q_up, w_kv_up):
    o1 = SWA_Q_W
    o2 = o1 + SWA_KV_W
    o3 = o2 + SWA_KV_W
    o4 = o3 + MLA_Q_RANK
    o5 = o4 + MLA_KV_RANK
    kr = jnp.pad(w_in[o5:], ((MLA_NOPE_DIM, LANE - MLA_QK_DIM), (0, 0)))
    win = jnp.concatenate([
        _pad_heads(w_in[:o1], SWA_HEADS, SWA_HEAD_DIM, 0),
        _pad_heads(w_in[o1:o2], SWA_KV_HEADS, SWA_HEAD_DIM, 0),
        _pad_heads(w_in[o2:o3], SWA_KV_HEADS, SWA_HEAD_DIM, 0),
        w_in[o3:o5], kr], axis=0)
    wqu = _pad_heads(w_q_up, MLA_HEADS, MLA_QK_DIM, 0)
    kv = w_kv_up.reshape(MLA_HEADS, MLA_NOPE_DIM + MLA_V_DIM, MLA_KV_RANK)
    wkv = jnp.concatenate([
        _pad_heads(kv[:, :MLA_NOPE_DIM].reshape(-1, MLA_KV_RANK), MLA_HEADS, MLA_NOPE_DIM, 0),
        _pad_heads(kv[:, MLA_NOPE_DIM:].reshape(-1, MLA_KV_RANK), MLA_HEADS, MLA_V_DIM, 0)], axis=0)
    return win, wqu, wkv


def _unpad_layer(dwin, dwqu, dwkv):
    d_w_in = jnp.concatenate([
        _unpad_heads(dwin[PO_QA:PO_KA], SWA_HEADS, SWA_HEAD_DIM, 0),
        _unpad_heads(dwin[PO_KA:PO_VA], SWA_KV_HEADS, SWA_HEAD_DIM, 0),
        _unpad_heads(dwin[PO_VA:PO_CQ], SWA_KV_HEADS, SWA_HEAD_DIM, 0),
        dwin[PO_CQ:PO_KR], dwin[PO_KR + MLA_NOPE_DIM:PO_KR + MLA_QK_DIM]], axis=0)
    d_w_q_up = _unpad_heads(dwqu, MLA_HEADS, MLA_QK_DIM, 0)
    dk = _unpad_heads(dwkv[:HP], MLA_HEADS, MLA_NOPE_DIM, 0).reshape(MLA_HEADS, MLA_NOPE_DIM, MLA_KV_RANK)
    dv = _unpad_heads(dwkv[HP:], MLA_HEADS, MLA_V_DIM, 0).reshape(MLA_HEADS, MLA_V_DIM, MLA_KV_RANK)
    d_w_kv_up = jnp.concatenate([dk, dv], axis=1).reshape(-1, MLA_KV_RANK)
    return d_w_in, d_w_q_up, d_w_kv_up


def _train_example(x, target, meta, vec, weights):
    s = x.shape[0]
    depth = vec["attn_norm"].shape[0]
    t = FRONT + N_META + s
    assert t % BLOCK == 0
    tabs = _rope_tables(t)
    h = jnp.concatenate([jnp.zeros((FRONT, D_MODEL), F32), meta, x], axis=0)
    tgt = jnp.concatenate([jnp.zeros((FRONT + N_META, D_MODEL), F32), target], axis=0)
    row = lambda v: v[None, :]

    saved = []
    for l in range(depth):
        win, wqu, wkv = _pad_layer(*weights.attn_in(l))
        g1, gq, gkv, g2, ga, gb = (row(vec[n][l]) for n in ("attn_norm", "q_norm", "kv_norm", "ffn_norm",
                                                            "out_norm_swa", "out_norm_mla"))
        sk = row(vec["sinks"][l])
        u, qa, ka, va, cq, ckv, qn, kvn, qb, kf, vb = _pre_fwd(h, g1, win, gq, wqu, gkv, wkv, tabs)
        oa = _swa_fwd(sk, qa, ka, va)
        ob, lse = weights.mla_fwd(l, qb, kf, vb)
        wo = weights.w_o(l)
        h2, mix, u2 = _mix_fwd(h, oa, ob, ga, gb, wo, g2)
        wg, wu, wd = weights.ffn(l)
        h3, gt, up = _ffn_fwd(h2, u2, wg, wu, wd)
        saved.append((h, u, qa, ka, va, cq, ckv, qn, kvn, qb, kf, vb, oa, ob, lse, h2, mix, u2, gt, up,
                      win, wqu, wkv, wo, ga, gb, g1, gq, gkv, g2, sk, wg, wu, wd))
        h = h3

    dh, d_final, loss = _loss_bwd(h, row(vec["final_norm"]), tgt)

    grads = []
    for l in reversed(range(depth)):
        (h0, u, qa, ka, va, cq, ckv, qn, kvn, qb, kf, vb, oa, ob, lse, h2, mix, u2, gt, up,
         win, wqu, wkv, wo, ga, gb, g1, gq, gkv, g2, sk, wg, wu, wd) = saved[l]
        dff = wd.shape[0]
        act, dgu, dhb = _ffn_bwd_a(dh, gt, up, wd)
        weights.ffn_grads(l, _tn_matmul(dgu, u2, "dw_gate", (0, dff)), _tn_matmul(dgu, u2, "dw_up", (dff, dff)),
                          _tn_matmul(act, dhb, "dw_down"))
        dh2, dh2b, d_g2 = _ffn_bwd_b(dh, dgu, h2, g2, wg, wu)
        weights.attn_grads(l, w_o=_tn_matmul(mix, dh2b, "dw_o"))
        doa, dob, dl, d_ga, d_gb = _mix_bwd(dh2b, oa, ob, ga, gb, wo)
        dqa, dkc, dkp, dvc, dvp, dsink = _swa_bwd(sk, qa, ka, va, oa, doa)
        dqb, dkf, dvb = weights.mla_bwd(l, qb, kf, vb, dob, lse, dl)
        dh, dp, dqbo, dkvo, d_g1, d_gq, d_gkv = _pre_bwd(
            dh2, h0, cq, ckv, dqa, dkc, dkp, dvc, dvp, dqb, dkf, dvb,
            g1, win, gq, wqu, gkv, wkv, tabs)
        d_win = _tn_matmul(dp, u, "dw_in")
        d_wqu = _tn_matmul(dqbo, qn, "dw_q_up")
        d_wkv = _tn_matmul(dkvo, kvn, "dw_kv_up")
        weights.attn_grads(l, **dict(zip(ATTN_IN, _unpad_layer(d_win, d_wqu, d_wkv))))
        grads.append(dict(attn_norm=d_g1[0], q_norm=d_gq[0], kv_norm=d_gkv[0], sinks=dsink[:, 0], out_norm_swa=d_ga[0],
                          out_norm_mla=d_gb[0], ffn_norm=d_g2[0]))
    grads = grads[::-1]
    stacked = {k: jnp.stack([g[k] for g in grads]) for k in grads[0]}
    stacked["final_norm"] = d_final[0]
    return loss[0, 0], dh[FRONT + N_META:], dh[FRONT:FRONT + N_META], stacked


MESH = pl.DeviceIdType.MESH
ANY = pl.BlockSpec(memory_space=pl.ANY)


def _place():
    return lax.axis_index("x"), lax.axis_index("y"), lax.axis_index("c")


def _index(x, y, c):
    return 4 * x + 2 * y + c


def _comm_sems(n):
    return [pltpu.SemaphoreType.DMA((n, N_DEV - 1)), pltpu.SemaphoreType.DMA((n, N_DEV - 1)),
            pltpu.SemaphoreType.DMA((n,))]


class _gather_plan:
    def __init__(self, x_refs, out_refs, send_sems, recv_sems, local_sems):
        self.x_refs, self.out_refs = x_refs, out_refs
        self.send_sems, self.recv_sems, self.local_sems = send_sems, recv_sems, local_sems
        self.n = len(x_refs)

    def _where(self):
        x, y, c = _place()
        return (x, y, c), (x, y, 1 - c), [(1 - x, y), (x, 1 - y), (1 - x, 1 - y)], c

    def _copy(self, i, k, block, to, from_input=False):
        slot = self.out_refs[i].at[_index(*block)]
        return pltpu.make_async_remote_copy(
            src_ref=self.x_refs[i] if from_input else slot, dst_ref=slot,
            send_sem=self.send_sems.at[i, k], recv_sem=self.recv_sems.at[i, k], device_id=to, device_id_type=MESH)

    def _mine(self, i, me):
        return pltpu.make_async_copy(self.x_refs[i], self.out_refs[i].at[_index(*me)], self.local_sems.at[i])

    def _first(self, me, sibling, chips, c):
        out = [self._copy(i, 1 + j, me, (*chip, c), True) for j, chip in enumerate(chips) for i in range(self.n)]
        return out + [self._copy(i, 0, me, sibling, True) for i in range(self.n)]

    def start(self):
        me, sibling, chips, c = self._where()
        for i in range(self.n):
            self._mine(i, me).start()
        for cp in self._first(me, sibling, chips, c):
            cp.start()

    def forward(self):
        me, sibling, chips, c = self._where()
        for j, chip in enumerate(chips):
            for i in range(self.n):
                self._copy(i, 1 + j, (*chip, c), me).wait_recv()
                self._copy(i, 4 + j, (*chip, c), sibling).start()

    def finish(self):
        me, sibling, chips, c = self._where()
        for i in range(self.n):
            self._copy(i, 0, sibling, me).wait_recv()
            for j, chip in enumerate(chips):
                self._copy(i, 4 + j, (*chip, 1 - c), me).wait_recv()
        for cp in self._first(me, sibling, chips, c):
            cp.wait_send()
        for j, chip in enumerate(chips):
            for i in range(self.n):
                self._copy(i, 4 + j, (*chip, c), sibling).wait_send()
        for i in range(self.n):
            self._mine(i, me).wait()


class _exchange_plan:
    def __init__(self, in_refs, out_refs, send_sems, recv_sems, local_sems):
        self.in_refs, self.out_refs = in_refs, out_refs
        self.send_sems, self.recv_sems, self.local_sems = send_sems, recv_sems, local_sems
        self.n = len(in_refs)

    def _copies(self):
        x, y, c = _place()
        me = _index(x, y, c)
        mine = [pltpu.make_async_copy(self.in_refs[i].at[me], self.out_refs[i].at[me], self.local_sems.at[i])
                for i in range(self.n)]
        remote = []
        for k in range(1, N_DEV):
            peer = (1 - x if k & 4 else x, 1 - y if k & 2 else y, 1 - c if k & 1 else c)
            remote += [pltpu.make_async_remote_copy(
                src_ref=self.in_refs[i].at[_index(*peer)], dst_ref=self.out_refs[i].at[me],
                send_sem=self.send_sems.at[i, k - 1], recv_sem=self.recv_sems.at[i, k - 1],
                device_id=peer, device_id_type=MESH) for i in range(self.n)]
        return mine, remote

    def start(self):
        mine, remote = self._copies()
        for cp in mine + remote:
            cp.start()

    def finish(self):
        mine, remote = self._copies()
        for cp in remote:
            cp.wait_recv()
        for cp in remote:
            cp.wait_send()
        for cp in mine:
            cp.wait()


def _all_gather(shards, name):
    n = len(shards)

    def body(*refs):
        plan = _gather_plan(refs[:n], refs[n:2 * n], *refs[2 * n:])
        plan.start()
        plan.forward()
        plan.finish()

    return pl.pallas_call(
        body, name=name, in_specs=[ANY] * n, out_specs=[ANY] * n, scratch_shapes=_comm_sems(n),
        out_shape=[jax.ShapeDtypeStruct((N_DEV,) + a.shape, a.dtype) for a in shards],
    )(*shards)


def _exchange(slabs, name):
    n = len(slabs)

    def body(*refs):
        plan = _exchange_plan(refs[:n], refs[n:2 * n], *refs[2 * n:])
        plan.start()
        plan.finish()

    return pl.pallas_call(
        body, name=name, in_specs=[ANY] * n, out_specs=[ANY] * n, scratch_shapes=_comm_sems(n),
        out_shape=[jax.ShapeDtypeStruct(a.shape, a.dtype) for a in slabs],
    )(*slabs)


def _adamw(w, g, m, v):
    m = ADAM_B1 * m + (1.0 - ADAM_B1) * g
    v = ADAM_B2 * v + (1.0 - ADAM_B2) * (g * g)
    m_hat = m / (1.0 - ADAM_B1 ** ADAM_STEP)
    v_hat = v / (1.0 - ADAM_B2 ** ADAM_STEP)
    return -ADAM_LR * (m_hat / (jnp.sqrt(v_hat) + ADAM_EPS) + ADAM_WD * w), m, v


def _sum_slots(ref):
    g = ref[0].astype(F32)
    for s in range(1, N_DEV):
        g = g + ref[s].astype(F32)
    return g


def _reduce_adamw(parts, w, m, v, layer, outs, name):
    l, r, c = w.shape
    tile = next(t for t in (256, 128, r) if r % t == 0)

    def body(p_ref, w_ref, m_ref, v_ref, g0, d0, m0, v0, g_ref, d_ref, nm_ref, nv_ref):
        g = _sum_slots(p_ref)
        g_ref[...] = g
        d_ref[...], nm_ref[...], nv_ref[...] = _adamw(w_ref[...], g, m_ref[...], v_ref[...])

    blk = pl.BlockSpec((None, tile, c), lambda j: (layer, j, 0))
    return pl.pallas_call(
        body, name=name, grid=(r // tile,),
        in_specs=[pl.BlockSpec((N_DEV, tile, c), lambda j: (0, j, 0)), blk, blk, blk] + [ANY] * 4, out_specs=[blk] * 4,
        out_shape=[jax.ShapeDtypeStruct((l, r, c), F32)] * 4,
        input_output_aliases={4: 0, 5: 1, 6: 2, 7: 3},
        compiler_params=_params("parallel"),
    )(parts, w, m, v, *outs)


def _sum_parts(parts, name):
    _, r, c = parts.shape

    def body(p_ref, g_ref):
        g_ref[...] = _sum_slots(p_ref)

    return pl.pallas_call(body, name=name, out_shape=jax.ShapeDtypeStruct((r, c), F32))(parts)


def _adamw_call(w, g, m, v, name):
    def body(w_ref, g_ref, m_ref, v_ref, d_ref, nm_ref, nv_ref):
        d_ref[...], nm_ref[...], nv_ref[...] = _adamw(w_ref[...], g_ref[...], m_ref[...], v_ref[...])

    return pl.pallas_call(body, name=name, out_shape=[jax.ShapeDtypeStruct(w.shape, F32)] * 3)(w, g, m, v)


ATTN_IN = ("w_in", "w_q_up", "w_kv_up")
ATTN = ATTN_IN + ("w_o",)
FFN = ("w_gate", "w_up", "w_down")
TRANSPOSED = ("w_in", "w_q_up", "w_kv_up", "w_gate", "w_up")
SMALL = ("attn_norm", "ffn_norm", "final_norm", "out_norm_swa", "out_norm_mla", "q_norm", "kv_norm", "sinks")
PACK_W = 1024
SMALL_ROWS = 16


def _pack(arrs, dtype):
    flat = jnp.concatenate([a.astype(dtype).reshape(-1) for a in arrs])
    return flat.reshape(-1, PACK_W)


def _unpack(packed, like):
    flat = packed.reshape(-1)
    out, off = [], 0
    for a in like:
        out.append(flat[off:off + a.size].reshape(a.shape))
        off += a.size
    return out


def _gather_to_full(gathered):
    return gathered.reshape((-1,) + gathered.shape[2:])


def _full_to_slabs(full):
    return full.reshape((N_DEV, -1) + full.shape[1:])


class _ShardedWeights:
    def __init__(self, shards, depth):
        self.shards, self.depth = shards, depth
        self.gathered, self.pending, self.parts = {}, {}, {}
        self._gather([(n, 0) for n in ATTN_IN], lambda xs: _all_gather(xs, "gather_attn0"))

    def _gather(self, keys, run):
        self.gathered.update(zip(keys, run([self.shards[n][l] for n, l in keys])))

    def _full(self, names, l):
        return tuple(_gather_to_full(self.gathered[n, l]) for n in names)

    def attn_in(self, l):
        return self._full(ATTN_IN, l)

    def w_o(self, l):
        return self._full(("w_o",), l)[0]

    def ffn(self, l):
        return self._full(FFN, l)

    def mla_fwd(self, l, q, k, v):
        keys = [(n, l) for n in ("w_o",) + FFN] + ([(n, l + 1) for n in ATTN_IN] if l + 1 < self.depth else [])
        out = []
        self._gather(keys, lambda xs: out.extend(_mla_fwd(q, k, v, xs)) or out[2])
        return out[0], out[1]

    def _add(self, names, l, grads):
        for n, g in zip(names, grads):
            self.pending[n, l] = _full_to_slabs(g)

    def ffn_grads(self, l, *grads):
        self._add(FFN, l, grads)

    def attn_grads(self, l, **grads):
        self._add(list(grads), l, grads.values())

    def _exchange(self, run):
        keys = list(self.pending)
        self.parts.update(zip(keys, run([self.pending.pop(k) for k in keys])))

    def mla_bwd(self, l, *args):
        out = []
        self._exchange(lambda xs: out.extend(_mla_bwd(*args, xs)) or out[3])
        return out[0], out[1], out[2]

    def flush(self):
        self._exchange(lambda xs: _exchange(xs, "exchange_attn0"))


def kernel(x, meta_tokens, attn_norm, w_in, q_norm, w_q_up, kv_norm, w_kv_up, sinks, out_norm_swa, out_norm_mla, w_o, ffn_norm, w_gate, w_up, w_down, final_norm, loss_target, m_meta_tokens, m_attn_norm, m_w_in, m_q_norm, m_w_q_up, m_kv_norm, m_w_kv_up, m_sinks, m_out_norm_swa, m_out_norm_mla, m_w_o, m_ffn_norm, m_w_gate, m_w_up, m_w_down, m_final_norm, v_meta_tokens, v_attn_norm, v_w_in, v_q_norm, v_w_q_up, v_kv_norm, v_w_kv_up, v_sinks, v_out_norm_swa, v_out_norm_mla, v_w_o, v_ffn_norm, v_w_gate, v_w_up, v_w_down, v_final_norm):
    w = dict(meta_tokens=meta_tokens, attn_norm=attn_norm, w_in=w_in, q_norm=q_norm, w_q_up=w_q_up, kv_norm=kv_norm,
             w_kv_up=w_kv_up, sinks=sinks, out_norm_swa=out_norm_swa, out_norm_mla=out_norm_mla, w_o=w_o,
             ffn_norm=ffn_norm, w_gate=w_gate, w_up=w_up, w_down=w_down, final_norm=final_norm)
    m = dict(meta_tokens=m_meta_tokens, attn_norm=m_attn_norm, w_in=m_w_in, q_norm=m_q_norm, w_q_up=m_w_q_up,
             kv_norm=m_kv_norm, w_kv_up=m_w_kv_up, sinks=m_sinks, out_norm_swa=m_out_norm_swa,
             out_norm_mla=m_out_norm_mla, w_o=m_w_o, ffn_norm=m_ffn_norm, w_gate=m_w_gate, w_up=m_w_up,
             w_down=m_w_down, final_norm=m_final_norm)
    v = dict(meta_tokens=v_meta_tokens, attn_norm=v_attn_norm, w_in=v_w_in, q_norm=v_q_norm, w_q_up=v_w_q_up,
             kv_norm=v_kv_norm, w_kv_up=v_w_kv_up, sinks=v_sinks, out_norm_swa=v_out_norm_swa,
             out_norm_mla=v_out_norm_mla, w_o=v_w_o, ffn_norm=v_ffn_norm, w_gate=v_w_gate, w_up=v_w_up,
             w_down=v_w_down, final_norm=v_final_norm)
    names = list(w)
    big = ATTN + FFN
    depth = w_in.shape[0]
    me = _index(*_place())

    as_held = lambda n, a: jnp.swapaxes(a, 1, 2) if n in TRANSPOSED else a
    weights = _ShardedWeights({n: as_held(n, w[n]).astype(BF16) for n in big}, depth)
    meta = jnp.moveaxis(_all_gather([meta_tokens], "gather_meta")[0], 0, 1).reshape(N_META, D_MODEL)
    loss, grad_x, d_meta, grads = _train_example(x[0], loss_target[0], meta, {n: w[n] for n in SMALL}, weights)
    weights.flush()

    g_big, d_big, m_big, v_big = {}, {}, {}, {}
    for n in big:
        held = [as_held(n, a) for a in (w[n], m[n], v[n])]
        outs = [lax.empty(held[0].shape, F32) for _ in range(4)]
        for l in reversed(range(depth)):
            outs = _reduce_adamw(weights.parts[n, l], *held, l, outs, "reduce_adamw_" + n)
        g_big[n], d_big[n], m_big[n], v_big[n] = [as_held(n, a) for a in outs]

    small = [grads[n] for n in SMALL] + [loss.reshape(1)]
    pad = SMALL_ROWS * PACK_W - sum(a.size for a in small)
    part = jnp.concatenate([_pack(small + [jnp.zeros((pad,), F32)], F32), d_meta], axis=0)
    total = _sum_parts(_all_gather([part], "gather_small")[0], "sum_small")
    small_w = [w[n] for n in SMALL]
    packs = [_pack([d[n] for n in SMALL] + [jnp.zeros((pad + 1,), F32)], F32) for d in (w, m, v)]
    upd = _adamw_call(packs[0], total[:SMALL_ROWS], packs[1], packs[2], "adamw_small")
    g_small, d_small, m_small, v_small = [dict(zip(SMALL, _unpack(p, small_w))) for p in (total[:SMALL_ROWS],) + tuple(upd)]
    loss_total = total[:SMALL_ROWS].reshape(-1)[SMALL_ROWS * PACK_W - pad - 1]
    g_meta = lax.dynamic_slice_in_dim(total[SMALL_ROWS:], me * LANE, LANE, axis=1)
    d_mt, m_mt, v_mt = _adamw_call(meta_tokens, g_meta, m_meta_tokens, v_meta_tokens, "adamw_meta")

    outs = []
    for got in ({**g_big, **g_small, "meta_tokens": g_meta}, {**d_big, **d_small, "meta_tokens": d_mt},
                {**m_big, **m_small, "meta_tokens": m_mt}, {**v_big, **v_small, "meta_tokens": v_mt}):
        outs += [got[n] for n in names]
    return (loss_total, grad_x[None], *outs)
```

```python
import jax
import jax.numpy as jnp
from jax import lax
from jax.experimental import pallas as pl
from jax.experimental.pallas import tpu as pltpu

F32 = jnp.float32
BF16 = jnp.bfloat16

D_MODEL = 1024
N_META = 16
BLOCK = 128
FRONT = (-N_META) % BLOCK
ROPE_THETA = 10000.0
EPS = 1e-6
NEG = -1e30
SWA_HEADS = 8
SWA_KV_HEADS = 2
SWA_GROUP = SWA_HEADS // SWA_KV_HEADS
SWA_HEAD_DIM = 64
MLA_HEADS = 8
MLA_Q_RANK = 256
MLA_KV_RANK = 128
MLA_NOPE_DIM = 64
MLA_ROPE_DIM = 32
MLA_V_DIM = 64
MLA_QK_DIM = MLA_NOPE_DIM + MLA_ROPE_DIM
SWA_Q_W = SWA_HEADS * SWA_HEAD_DIM
SWA_KV_W = SWA_KV_HEADS * SWA_HEAD_DIM
MLA_OUT_W = MLA_HEADS * MLA_V_DIM
SCALE_A = SWA_HEAD_DIM ** -0.5
SCALE_B = MLA_QK_DIM ** -0.5
LOG2E = 1.4426950408889634
Q_SCALE = SCALE_B * LOG2E
ADAM_LR = 0.001
ADAM_B1 = 0.9
ADAM_B2 = 0.999
ADAM_EPS = 1e-08
ADAM_WD = 0.01
ADAM_STEP = 10

LANE = 128
N_DEV = 8
HP = 8 * LANE
PO_QA, PO_KA, PO_VA = 0, HP, HP + 2 * LANE
PO_CQ = PO_VA + 2 * LANE
PO_CKV = PO_CQ + MLA_Q_RANK
PO_KR = PO_CKV + MLA_KV_RANK
PW_IN = PO_KR + LANE
N_TAB = 7
VMEM_LIMIT = 56 * 2 ** 20
TN_VMEM_BUDGET = 36 * 2 ** 20
MLA_HB = 4
HALF = LANE // 2
assert SWA_HEAD_DIM == HALF and MLA_V_DIM == HALF

NT = (((1,), (1,)), ((), ()))
TN = (((0,), (0,)), ((), ()))


def _tile(t):
    return 384 if t % 384 == 0 else 128


def _params(*sem):
    return pltpu.CompilerParams(dimension_semantics=sem, vmem_limit_bytes=VMEM_LIMIT)


def _row(tm, n):
    return pl.BlockSpec((tm, n), lambda i: (i, 0))


def _const(shape):
    return pl.BlockSpec(shape, lambda i: (0,) * len(shape))


def _dot(a, b):
    return jnp.dot(a, b, preferred_element_type=F32)


def _dot_nt(a, b):
    return lax.dot_general(a, b, NT, preferred_element_type=F32)


def _dot_tn(a, b):
    return lax.dot_general(a, b, TN, preferred_element_type=F32)


def _rope(x, c, s1, s2, shift):
    return x * c + pltpu.roll(x, LANE - shift, 1) * s1 + pltpu.roll(x, shift, 1) * s2


def _rope_t(dy, c, s1, s2, shift):
    return dy * c + pltpu.roll(dy * s1, shift, 1) + pltpu.roll(dy * s2, LANE - shift, 1)


def _rms_r(x, n):
    return lax.rsqrt(jnp.sum(x * x, axis=-1, keepdims=True) * (1.0 / n) + EPS)


def _rms_bwd(x, g, dy, n):
    r = _rms_r(x, n)
    xh = x * r
    dxh = dy * g
    dx = r * (dxh - xh * (jnp.sum(dxh * xh, axis=-1, keepdims=True) * (1.0 / n)))
    return dx, jnp.sum(dy * xh, axis=0, keepdims=True)


def _acc(ref, val, first):
    @pl.when(first)
    def _():
        ref[...] = val

    @pl.when(jnp.logical_not(first))
    def _():
        ref[...] += val


def _pack_pair(even, odd):
    return even + pltpu.roll(odd, HALF, 1)


def _pair_half(slab, half):
    return slab if half == 0 else pltpu.roll(slab, HALF, 1)


def _unpack_pair(slab, half):
    x = _pair_half(slab, half)
    return jnp.where(lax.broadcasted_iota(jnp.int32, x.shape, 1) < HALF, x, 0.0)


def _tabs(tab_ref):
    return [tab_ref[:, LANE * i:LANE * (i + 1)] for i in range(N_TAB)]


def _pre_fwd(h, g1, win, gq, wqu, gkv, wkv, tabs):
    t = h.shape[0]
    tm = _tile(t)

    def body(h_ref, g1_ref, win_ref, gq_ref, wqu_ref, gkv_ref, wkv_ref, tab_ref,
             u_ref, qa_ref, ka_ref, va_ref, cq_ref, ckv_ref, qn_ref, kvn_ref, qb_ref, kf_ref, vb_ref):
        ca, sa1, sa2, cb, sb1, sb2, ck = _tabs(tab_ref)
        hv = h_ref[...]
        u = (hv * _rms_r(hv, D_MODEL) * g1_ref[...]).astype(BF16)
        u_ref[...] = u
        p = _dot_nt(u, win_ref[...])
        for c in range(SWA_HEADS):
            sl = slice(LANE * c, LANE * (c + 1))
            qa_ref[:, sl] = _rope(p[:, PO_QA + LANE * c:PO_QA + LANE * (c + 1)], ca, sa1, sa2, 32).astype(BF16)
        for c in range(SWA_KV_HEADS):
            sl = slice(LANE * c, LANE * (c + 1))
            ka_ref[:, sl] = _rope(p[:, PO_KA + LANE * c:PO_KA + LANE * (c + 1)], ca, sa1, sa2, 32).astype(BF16)
        va_ref[...] = p[:, PO_VA:PO_CQ].astype(BF16)
        cq = p[:, PO_CQ:PO_CKV]
        ckv = p[:, PO_CKV:PO_KR]
        cq_ref[...] = cq
        ckv_ref[...] = ckv
        qn = (cq * _rms_r(cq, MLA_Q_RANK) * gq_ref[...]).astype(BF16)
        qn_ref[...] = qn
        qb = _dot_nt(qn, wqu_ref[...])
        kvn = (ckv * _rms_r(ckv, MLA_KV_RANK) * gkv_ref[...]).astype(BF16)
        kvn_ref[...] = kvn
        kv = _dot_nt(kvn, wkv_ref[...])
        kr = _rope(p[:, PO_KR:PW_IN], ck, sb1, sb2, 16)
        for c in range(MLA_HEADS):
            sl = slice(LANE * c, LANE * (c + 1))
            qb_ref[:, sl] = (_rope(qb[:, sl], cb, sb1, sb2, 16) * Q_SCALE).astype(BF16)
            kf_ref[:, sl] = (kv[:, sl] + kr).astype(BF16)
        vb_ref[...] = kv[:, HP:].astype(BF16)

    widths = [(D_MODEL, BF16), (HP, BF16), (2 * LANE, BF16), (2 * LANE, BF16), (MLA_Q_RANK, F32),
              (MLA_KV_RANK, F32), (MLA_Q_RANK, BF16), (MLA_KV_RANK, BF16), (HP, BF16), (HP, BF16), (HP, BF16)]
    return pl.pallas_call(
        body, name="pre_fwd", grid=(t // tm,),
        in_specs=[_row(tm, D_MODEL), _const(g1.shape), _const(win.shape), _const(gq.shape), _const(wqu.shape),
                  _const(gkv.shape), _const(wkv.shape), _row(tm, N_TAB * LANE)],
        out_specs=[_row(tm, w) for w, _ in widths],
        out_shape=[jax.ShapeDtypeStruct((t, w), d) for w, d in widths],
        compiler_params=_params("parallel"),
    )(h, g1, win, gq, wqu, gkv, wkv, tabs)


def _swa_mask(nb):
    key = lax.broadcasted_iota(jnp.int32, (2 * BLOCK, SWA_GROUP * BLOCK), 0)
    qry = lax.broadcasted_iota(jnp.int32, (2 * BLOCK, SWA_GROUP * BLOCK), 1) & (BLOCK - 1)
    return (key > qry) & (key <= qry + BLOCK) & (key + (nb - 1) * BLOCK >= FRONT)


def _swa_group(ref, rows, j):
    return jnp.concatenate([ref[rows, LANE * (SWA_GROUP * j + g):LANE * (SWA_GROUP * j + g + 1)]
                            for g in range(SWA_GROUP)], axis=0)


def _swa_packed_group(ref, rows, j):
    heads = [SWA_GROUP * j + g for g in range(SWA_GROUP)]
    return jnp.concatenate([_pair_half(ref[rows, LANE * (hd // 2):LANE * (hd // 2 + 1)], hd % 2) for hd in heads], axis=0)


def _swa_sinks(sink_ref, j):
    return jnp.concatenate([jnp.full((1, BLOCK), sink_ref[0, SWA_GROUP * j + g], F32) for g in range(SWA_GROUP)], axis=1)


def _swa_keys(prev_ref, cur_ref, rb, j):
    sl = slice(LANE * j, LANE * (j + 1))
    if rb == 0:
        return jnp.concatenate([prev_ref[:, sl], cur_ref[:BLOCK, sl]], axis=0)
    return cur_ref[BLOCK * (rb - 1):BLOCK * (rb + 1), sl]


def _swa_chains(t):
    return [(rb, j) for rb in range(_tile(t) // BLOCK) for j in range(SWA_KV_HEADS)]


def _swa_scores(sink_ref, q_ref, kp_ref, kc_ref, n, t):
    r = _tile(t) // BLOCK
    chains = _swa_chains(t)
    qs = [_swa_group(q_ref, slice(BLOCK * rb, BLOCK * (rb + 1)), j) for rb, j in chains]
    ks = [_swa_keys(kp_ref, kc_ref, rb, j) for rb, j in chains]
    ss = [_dot_nt(k2, q4) for q4, k2 in zip(qs, ks)]
    masks = [_swa_mask(n * r + rb) for rb in range(r)]
    out = []
    for (rb, j), s in zip(chains, ss):
        sink = _swa_sinks(sink_ref, j)
        s = jnp.where(masks[rb], s * SCALE_A, NEG)
        m = jnp.maximum(jnp.max(s, axis=0, keepdims=True), sink)
        e = jnp.exp(s - m)
        es = jnp.exp(sink - m)
        inv = 1.0 / (jnp.sum(e, axis=0, keepdims=True) + es)
        out.append((e * inv, es * inv))
    return qs, ks, out


def _swa_specs(t):
    ts = _tile(t)
    r = ts // BLOCK
    prev = lambda n: (jnp.maximum(n * r - 1, 0), 0)
    cur = lambda n: (n, 0)
    return [pl.BlockSpec(memory_space=pltpu.SMEM), pl.BlockSpec((ts, HP), cur),
            pl.BlockSpec((BLOCK, 2 * LANE), prev), pl.BlockSpec((ts, 2 * LANE), cur),
            pl.BlockSpec((BLOCK, 2 * LANE), prev), pl.BlockSpec((ts, 2 * LANE), cur)]


def _swa_fwd(sinks, q, k, v):
    t = q.shape[0]
    ts = _tile(t)

    def body(sink_ref, q_ref, kp_ref, kc_ref, vp_ref, vc_ref, o_ref):
        chains = _swa_chains(t)
        _, _, probs = _swa_scores(sink_ref, q_ref, kp_ref, kc_ref, pl.program_id(0), t)
        os_ = [_dot_tn(p.astype(BF16), _swa_keys(vp_ref, vc_ref, rb, j)) for (rb, j), (p, _) in zip(chains, probs)]
        for (rb, j), o4 in zip(chains, os_):
            for g in range(0, SWA_GROUP, 2):
                pair = (SWA_GROUP * j + g) // 2
                o_ref[BLOCK * rb:BLOCK * (rb + 1), LANE * pair:LANE * (pair + 1)] = _pack_pair(
                    o4[BLOCK * g:BLOCK * (g + 1)], o4[BLOCK * (g + 1):BLOCK * (g + 2)])

    return pl.pallas_call(
        body, name="swa_fwd", grid=(t // ts,),
        in_specs=_swa_specs(t),
        out_specs=pl.BlockSpec((ts, SWA_Q_W), lambda n: (n, 0)),
        out_shape=jax.ShapeDtypeStruct((t, SWA_Q_W), F32),
        compiler_params=_params("parallel"),
    )(sinks, q, k, k, v, v)


def _causal_mask(q0, k0, tq, tk, transposed):
    if transposed:
        key = k0 + lax.broadcasted_iota(jnp.int32, (tk, tq), 0)
        qry = q0 + lax.broadcasted_iota(jnp.int32, (tk, tq), 1)
    else:
        qry = q0 + lax.broadcasted_iota(jnp.int32, (tq, tk), 0)
        key = k0 + lax.broadcasted_iota(jnp.int32, (tq, tk), 1)
    return (key <= qry) & (key >= FRONT)


def _heads(ref, hb, rows=slice(None)):
    return [ref[rows, LANE * a:LANE * (a + 1)] for a in range(hb)]


def _head_stats(t):
    return jax.ShapeDtypeStruct((MLA_HEADS // MLA_HB, t, MLA_HB), F32)


def _mla_fwd(q, k, v, shards=()):
    t = q.shape[0]
    tq = _tile(t)
    nq = t // tq
    n = len(shards)
    steps = (MLA_HEADS // MLA_HB) * nq

    def body(q_ref, k_ref, v_ref, *rest):
        x_refs, (o_ref, lse_ref), out_refs = rest[:n], rest[n:n + 2], rest[n + 2:2 * n + 2]
        acc_sc, sems = rest[2 * n + 2], rest[2 * n + 3:]
        i = pl.program_id(1)
        step_id = pl.program_id(0) * nq + i
        if n:
            plan = _gather_plan(x_refs, out_refs, *sems)
            pl.when(step_id == 0)(plan.start)
            pl.when(step_id == (3 * steps) // 4)(plan.forward)
        qs = _heads(q_ref, MLA_HB)
        acc_sc[...] = jnp.zeros(acc_sc.shape, F32)

        def step(j, carry, masked):
            rows = pl.ds(pl.multiple_of(j * tq, tq), tq)
            ks, vs = _heads(k_ref, MLA_HB, rows), _heads(v_ref, MLA_HB, rows)
            ss = [_dot_nt(kh, qh) for qh, kh in zip(qs, ks)]
            if masked:
                mask = _causal_mask(i * tq, j * tq, tq, tq, True)
                ss = [jnp.where(mask, s, NEG) for s in ss]
            mid, out = [], []
            for s, (m, l) in zip(ss, carry):
                mn = jnp.maximum(m, jnp.max(s, axis=0, keepdims=True))
                al = jnp.exp2(m - mn)
                p = jnp.exp2(s - mn)
                out.append((mn, al * l + jnp.sum(p, axis=0, keepdims=True)))
                mid.append((al, p.astype(BF16)))
            for a, ((al, p), vh) in enumerate(zip(mid, vs)):
                acc_sc[a] = al * acc_sc[a] + _dot_tn(vh, p)
            return tuple(out)

        init = ((jnp.full((1, tq), NEG, F32), jnp.zeros((1, tq), F32)),) * MLA_HB
        carry = lax.fori_loop(0, jnp.minimum(i, 1) + 1, lambda it, c: step(it * i, c, True), init)
        carry = lax.fori_loop(1, i, lambda j, c: step(j, c, False), carry)
        outs = [(acc_sc[a] * (1.0 / l)).T for a, (_, l) in enumerate(carry)]
        for a in range(0, MLA_HB, 2):
            o_ref[:, HALF * a:HALF * (a + 2)] = _pack_pair(outs[a], outs[a + 1])
        for a, (m, l) in enumerate(carry):
            lse_ref[:, a:a + 1] = jnp.broadcast_to(m + jnp.log2(l), (LANE, tq)).T[:, :1]
        if n:
            pl.when(step_id == steps - 1)(plan.finish)

    blk = pl.BlockSpec((tq, MLA_HB * LANE), lambda h, i: (i, h))
    full = pl.BlockSpec((t, MLA_HB * LANE), lambda h, i: (0, h))
    packed = pl.BlockSpec((tq, MLA_HB * HALF), lambda h, i: (i, h))
    out = pl.pallas_call(
        body, name="mla_fwd_gather" if n else "mla_fwd", grid=(MLA_HEADS // MLA_HB, nq),
        in_specs=[blk, full, full] + [ANY] * n,
        out_specs=[packed, pl.BlockSpec((None, tq, MLA_HB), lambda h, i: (h, i, 0))] + [ANY] * n,
        out_shape=[jax.ShapeDtypeStruct((t, MLA_OUT_W), F32), _head_stats(t)]
        + [jax.ShapeDtypeStruct((N_DEV,) + a.shape, a.dtype) for a in shards],
        scratch_shapes=[pltpu.VMEM((MLA_HB, LANE, tq), F32)] + (_comm_sems(n) if n else []),
        compiler_params=_params("arbitrary", "arbitrary"),
    )(q, k, v, *shards)
    return out[0], out[1], out[2:]


def _mix_fwd(h, oa, ob, ga, gb, wo, g2):
    t = h.shape[0]
    tm = _tile(t)

    def body(h_ref, oa_ref, ob_ref, ga_ref, gb_ref, wo_ref, g2_ref, h2_ref, mix_ref, u2_ref):
        oa_v = oa_ref[...]
        ob_v = ob_ref[...]
        na = (oa_v * _rms_r(oa_v, SWA_Q_W) * ga_ref[...]).astype(BF16)
        nb = (ob_v * _rms_r(ob_v, MLA_OUT_W) * gb_ref[...]).astype(BF16)
        mix_ref[:, :SWA_Q_W] = na
        mix_ref[:, SWA_Q_W:] = nb
        h2 = h_ref[...] + _dot(na, wo_ref[:SWA_Q_W, :]) + _dot(nb, wo_ref[SWA_Q_W:, :])
        h2_ref[...] = h2
        u2_ref[...] = (h2 * _rms_r(h2, D_MODEL) * g2_ref[...]).astype(BF16)

    mix_w = SWA_Q_W + MLA_OUT_W
    return pl.pallas_call(
        body, name="mix_fwd", grid=(t // tm,),
        in_specs=[_row(tm, D_MODEL), _row(tm, SWA_Q_W), _row(tm, MLA_OUT_W), _const(ga.shape), _const(gb.shape),
                  _const(wo.shape), _const(g2.shape)],
        out_specs=[_row(tm, D_MODEL), _row(tm, mix_w), _row(tm, D_MODEL)],
        out_shape=[jax.ShapeDtypeStruct((t, D_MODEL), F32), jax.ShapeDtypeStruct((t, mix_w), BF16),
                   jax.ShapeDtypeStruct((t, D_MODEL), BF16)],
        compiler_params=_params("parallel"),
    )(h, oa, ob, ga, gb, wo, g2)


def _ffn_fwd(h2, u2, wg_t, wu_t, wd):
    t = h2.shape[0]
    tm = _tile(t)
    dff = wd.shape[0]

    def body(h2_ref, u2_ref, wg_ref, wu_ref, wd_ref, h3_ref, g_ref, up_ref):
        u2v = u2_ref[...]
        g = _dot_nt(u2v, wg_ref[...])
        up = _dot_nt(u2v, wu_ref[...])
        g_ref[...] = g.astype(BF16)
        up_ref[...] = up.astype(BF16)
        a = (g * jax.nn.sigmoid(g) * up).astype(BF16)
        h3_ref[...] = h2_ref[...] + _dot(a, wd_ref[...])

    return pl.pallas_call(
        body, name="ffn_fwd", grid=(t // tm,),
        in_specs=[_row(tm, D_MODEL), _row(tm, D_MODEL), _const(wg_t.shape), _const(wu_t.shape), _const(wd.shape)],
        out_specs=[_row(tm, D_MODEL), _row(tm, dff), _row(tm, dff)],
        out_shape=[jax.ShapeDtypeStruct((t, D_MODEL), F32), jax.ShapeDtypeStruct((t, dff), BF16),
                   jax.ShapeDtypeStruct((t, dff), BF16)],
        compiler_params=_params("parallel"),
    )(h2, u2, wg_t, wu_t, wd)


def _loss_bwd(h, gf, target):
    t = h.shape[0]
    tm = _tile(t)
    first_row = FRONT + N_META

    def body(h_ref, gf_ref, t_ref, dh_ref, dgf_ref, loss_ref):
        i = pl.program_id(0)
        hv = h_ref[...]
        y = hv * _rms_r(hv, D_MODEL) * gf_ref[...]
        row = i * tm + lax.broadcasted_iota(jnp.int32, (tm, 1), 0)
        err = jnp.where(row >= first_row, y - t_ref[...], 0.0)
        dx, dg = _rms_bwd(hv, gf_ref[...], err * (1.0 / D_MODEL), D_MODEL)
        dh_ref[...] = dx
        _acc(dgf_ref, dg, i == 0)
        part = 0.5 * jnp.sum(jnp.sum(err * err, axis=1, keepdims=True) * (1.0 / D_MODEL), axis=0, keepdims=True)
        _acc(loss_ref, jnp.broadcast_to(part, (1, LANE)), i == 0)

    return pl.pallas_call(
        body, name="loss_bwd", grid=(t // tm,),
        in_specs=[_row(tm, D_MODEL), _const(gf.shape), _row(tm, D_MODEL)],
        out_specs=[_row(tm, D_MODEL), _const((1, D_MODEL)), _const((1, LANE))],
        out_shape=[jax.ShapeDtypeStruct((t, D_MODEL), F32), jax.ShapeDtypeStruct((1, D_MODEL), F32),
                   jax.ShapeDtypeStruct((1, LANE), F32)],
        compiler_params=_params("arbitrary"),
    )(h, gf, target)


def _tn_matmul(a, b, name, cols=None):
    t, n = b.shape
    first, k = cols or (0, a.shape[1])
    tk = next(c for c in (k, 1024, 512, 256, 128) if k % c == 0 and first % c == 0 and c <= 1024)
    fits = lambda c: 2 * (t * (tk + c) * 2 + tk * c * 2) <= TN_VMEM_BUDGET
    tn = next(c for c in (n, 1024, 512, 256, 128) if n % c == 0 and fits(c))

    def body(a_ref, b_ref, o_ref):
        o_ref[...] = _dot_tn(a_ref[...], b_ref[...]).astype(BF16)

    return pl.pallas_call(
        body, name=name, grid=(k // tk, n // tn),
        in_specs=[pl.BlockSpec((t, tk), lambda i, j: (0, i + first // tk)), pl.BlockSpec((t, tn), lambda i, j: (0, j))],
        out_specs=pl.BlockSpec((tk, tn), lambda i, j: (i, j)),
        out_shape=jax.ShapeDtypeStruct((k, n), BF16),
        compiler_params=_params("parallel", "parallel"),
    )(a, b)


def _ffn_bwd_a(dh3, g, up, wd):
    t = dh3.shape[0]
    tm = _tile(t)
    dff = wd.shape[0]

    def body(dh3_ref, g_ref, up_ref, wd_ref, a_ref, dgu_ref, dh3b_ref):
        dh3b = dh3_ref[...].astype(BF16)
        dh3b_ref[...] = dh3b
        da = _dot_nt(dh3b, wd_ref[...])
        gv = g_ref[...].astype(F32)
        upv = up_ref[...].astype(F32)
        sg = jax.nn.sigmoid(gv)
        silu = gv * sg
        a_ref[...] = (silu * upv).astype(BF16)
        dgu_ref[:, :dff] = (da * upv * (sg * (1.0 + gv * (1.0 - sg)))).astype(BF16)
        dgu_ref[:, dff:] = (da * silu).astype(BF16)

    return pl.pallas_call(
        body, name="ffn_bwd_a", grid=(t // tm,),
        in_specs=[_row(tm, D_MODEL), _row(tm, dff), _row(tm, dff), _const(wd.shape)],
        out_specs=[_row(tm, dff), _row(tm, 2 * dff), _row(tm, D_MODEL)],
        out_shape=[jax.ShapeDtypeStruct((t, dff), BF16), jax.ShapeDtypeStruct((t, 2 * dff), BF16),
                   jax.ShapeDtypeStruct((t, D_MODEL), BF16)],
        compiler_params=_params("parallel"),
    )(dh3, g, up, wd)


def _ffn_bwd_b(dh3, dgu, h2, g2, wg_t, wu_t, slabs=()):
    t = dh3.shape[0]
    tm = _tile(t)
    dff = wg_t.shape[0]
    n = len(slabs)

    def body(dh3_ref, dgu_ref, h2_ref, g2_ref, wg_ref, wu_ref, *rest):
        in_refs, (dh2_ref, dh2b_ref, dg2_ref), out_refs, sems = rest[:n], rest[n:n + 3], rest[n + 3:2 * n + 3], rest[2 * n + 3:]
        if n:
            plan = _exchange_plan(in_refs, out_refs, *sems)
            pl.when(pl.program_id(0) == 0)(plan.start)
        du2 = _dot(dgu_ref[:, :dff], wg_ref[...]) + _dot(dgu_ref[:, dff:], wu_ref[...])
        dx, dg = _rms_bwd(h2_ref[...], g2_ref[...], du2, D_MODEL)
        dh2 = dh3_ref[...] + dx
        dh2_ref[...] = dh2
        dh2b_ref[...] = dh2.astype(BF16)
        _acc(dg2_ref, dg, pl.program_id(0) == 0)
        if n:
            pl.when(pl.program_id(0) == t // tm - 1)(plan.finish)

    out = pl.pallas_call(
        body, name="ffn_bwd_b_exchange" if n else "ffn_bwd_b", grid=(t // tm,),
        in_specs=[_row(tm, D_MODEL), _row(tm, 2 * dff), _row(tm, D_MODEL), _const(g2.shape), _const(wg_t.shape),
                  _const(wu_t.shape)] + [ANY] * n,
        out_specs=[_row(tm, D_MODEL), _row(tm, D_MODEL), _const((1, D_MODEL))] + [ANY] * n,
        out_shape=[jax.ShapeDtypeStruct((t, D_MODEL), F32), jax.ShapeDtypeStruct((t, D_MODEL), BF16),
                   jax.ShapeDtypeStruct((1, D_MODEL), F32)] + [jax.ShapeDtypeStruct(a.shape, a.dtype) for a in slabs],
        scratch_shapes=_comm_sems(n) if n else [],
        compiler_params=_params("arbitrary"),
    )(dh3, dgu, h2, g2, wg_t, wu_t, *slabs)
    return out[:3], out[3:]


def _mix_bwd(dh2, oa, ob, ga, gb, wo):
    t = dh2.shape[0]
    tm = _tile(t)

    def body(dh2_ref, oa_ref, ob_ref, ga_ref, gb_ref, wo_ref, doa_ref, dob_ref, dl_ref, dga_ref, dgb_ref):
        first = pl.program_id(0) == 0
        d = dh2_ref[...]
        ob_v = ob_ref[...]
        dxa, dga = _rms_bwd(oa_ref[...], ga_ref[...], _dot_nt(d, wo_ref[:SWA_Q_W, :]), SWA_Q_W)
        dxb, dgb = _rms_bwd(ob_v, gb_ref[...], _dot_nt(d, wo_ref[SWA_Q_W:, :]), MLA_OUT_W)
        lower = lax.broadcasted_iota(jnp.int32, (tm, LANE), 1) < HALF
        for hd in range(MLA_HEADS):
            sl = slice(LANE * (hd // 2), LANE * (hd // 2 + 1))
            mine = lower if hd % 2 == 0 else jnp.logical_not(lower)
            delta = jnp.sum(jnp.where(mine, ob_v[:, sl] * dxb[:, sl], 0.0), axis=1, keepdims=True)
            dl_ref[hd // MLA_HB, :, hd % MLA_HB:hd % MLA_HB + 1] = delta
        for ref, dx, heads in ((doa_ref, dxa, SWA_HEADS), (dob_ref, dxb, MLA_HEADS)):
            for hd in range(heads):
                slab = dx[:, LANE * (hd // 2):LANE * (hd // 2 + 1)]
                ref[:, LANE * hd:LANE * (hd + 1)] = _unpack_pair(slab, hd % 2).astype(BF16)
        _acc(dga_ref, dga, first)
        _acc(dgb_ref, dgb, first)

    return pl.pallas_call(
        body, name="mix_bwd", grid=(t // tm,),
        in_specs=[_row(tm, D_MODEL), _row(tm, SWA_Q_W), _row(tm, MLA_OUT_W), _const(ga.shape), _const(gb.shape),
                  _const(wo.shape)],
        out_specs=[_row(tm, HP), _row(tm, HP), pl.BlockSpec((MLA_HEADS // MLA_HB, tm, MLA_HB), lambda i: (0, i, 0)),
                   _const((1, SWA_Q_W)), _const((1, MLA_OUT_W))],
        out_shape=[jax.ShapeDtypeStruct((t, HP), BF16), jax.ShapeDtypeStruct((t, HP), BF16), _head_stats(t),
                   jax.ShapeDtypeStruct((1, SWA_Q_W), F32), jax.ShapeDtypeStruct((1, MLA_OUT_W), F32)],
        compiler_params=_params("arbitrary"),
    )(dh2, oa, ob, ga, gb, wo)


def _swa_bwd(sinks, q, k, v, o, do, slabs=()):
    t = q.shape[0]
    ts = _tile(t)
    ns = len(slabs)

    def body(sink_ref, q_ref, kp_ref, kc_ref, vp_ref, vc_ref, o_ref, do_ref, *rest):
        in_refs, outs, out_refs, sems = rest[:ns], rest[ns:ns + 6], rest[ns + 6:2 * ns + 6], rest[2 * ns + 6:]
        dq_ref, dkc_ref, dkp_ref, dvc_ref, dvp_ref, dsink_ref = outs
        n = pl.program_id(0)
        if ns:
            plan = _exchange_plan(in_refs, out_refs, *sems)
            pl.when(n == 0)(plan.start)
        chains = _swa_chains(t)
        qs, ks, probs = _swa_scores(sink_ref, q_ref, kp_ref, kc_ref, n, t)
        dos = [_swa_group(do_ref, slice(BLOCK * rb, BLOCK * (rb + 1)), j) for rb, j in chains]
        vs = [_swa_keys(vp_ref, vc_ref, rb, j) for rb, j in chains]
        dps = [_dot_nt(v2, do4) for do4, v2 in zip(dos, vs)]
        dss, dsks = [], []
        for (rb, j), (p, psink), do4, dp in zip(chains, probs, dos, dps):
            o4 = _swa_packed_group(o_ref, slice(BLOCK * rb, BLOCK * (rb + 1)), j)
            delta = jnp.sum(o4 * do4.astype(F32), axis=1, keepdims=True)
            delta = jnp.broadcast_to(delta, (SWA_GROUP * BLOCK, LANE)).T[:1, :]
            dss.append((p * (dp - delta) * SCALE_A).astype(BF16))
            dsks.append(-psink * delta)
        dqs = [_dot_tn(ds, k2) for ds, k2 in zip(dss, ks)]
        dks = [_dot(ds, q4) for ds, q4 in zip(dss, qs)]
        dvs = [_dot(p.astype(BF16), do4) for (p, _), do4 in zip(probs, dos)]
        dsink = [jnp.zeros((1, LANE), F32)] * SWA_HEADS
        ext = {}
        for (rb, j), dq4, dk2, dv2, dsk in zip(chains, dqs, dks, dvs, dsks):
            for g in range(SWA_GROUP):
                hd = SWA_GROUP * j + g
                rows = slice(BLOCK * g, BLOCK * (g + 1))
                dq_ref[BLOCK * rb:BLOCK * (rb + 1), LANE * hd:LANE * (hd + 1)] = dq4[rows].astype(BF16)
                dsink[hd] = dsink[hd] + jnp.sum(dsk[:, rows], axis=1, keepdims=True)
            for half in range(2):
                key = (j, rb + half)
                part = (dk2[BLOCK * half:BLOCK * (half + 1)], dv2[BLOCK * half:BLOCK * (half + 1)])
                ext[key] = part if key not in ext else (ext[key][0] + part[0], ext[key][1] + part[1])
        for (j, blk), (dk, dv) in ext.items():
            sl = slice(LANE * j, LANE * (j + 1))
            if blk == 0:
                dkp_ref[:, sl] = dk
                dvp_ref[:, sl] = dv
            else:
                dkc_ref[BLOCK * (blk - 1):BLOCK * blk, sl] = dk
                dvc_ref[BLOCK * (blk - 1):BLOCK * blk, sl] = dv
        for hd in range(SWA_HEADS):
            _acc(dsink_ref.at[hd:hd + 1, :], jnp.broadcast_to(dsink[hd], (1, LANE)), n == 0)
        if ns:
            pl.when(n == t // ts - 1)(plan.finish)

    cur = lambda n: (n, 0)
    kv = pl.BlockSpec((ts, 2 * LANE), cur)
    kvp = pl.BlockSpec((BLOCK, 2 * LANE), cur)
    hp = pl.BlockSpec((ts, HP), cur)
    kvs = jax.ShapeDtypeStruct((t, 2 * LANE), F32)
    kvps = jax.ShapeDtypeStruct((t // ts * BLOCK, 2 * LANE), F32)
    out = pl.pallas_call(
        body, name="swa_bwd_exchange" if ns else "swa_bwd", grid=(t // ts,),
        in_specs=_swa_specs(t) + [pl.BlockSpec((ts, SWA_Q_W), cur), hp] + [ANY] * ns,
        out_specs=[hp, kv, kvp, kv, kvp, _const((SWA_HEADS, LANE))] + [ANY] * ns,
        out_shape=[jax.ShapeDtypeStruct((t, HP), BF16), kvs, kvps, kvs, kvps,
                   jax.ShapeDtypeStruct((SWA_HEADS, LANE), F32)] + [jax.ShapeDtypeStruct(a.shape, a.dtype) for a in slabs],
        scratch_shapes=_comm_sems(ns) if ns else [],
        compiler_params=_params("arbitrary"),
    )(sinks, q, k, k, v, v, o, do, *slabs)
    return out[:6], out[6:]


def _mla_bwd(q, k, v, do, lse, dl, slabs=()):
    t = q.shape[0]
    tq = _tile(t)
    nq = t // tq
    n = len(slabs)
    hb = MLA_HB
    steps = (MLA_HEADS // hb) * nq

    def body(k_ref, v_ref, q_ref, do_ref, lse_ref, dl_ref, *rest):
        in_refs, (dq_ref, dk_ref, dv_ref), out_refs = rest[:n], rest[n:n + 3], rest[n + 3:2 * n + 3]
        (dq_sc, dk_sc, dv_sc), sems = rest[2 * n + 3:2 * n + 6], rest[2 * n + 6:]
        j = pl.program_id(1)
        step_id = pl.program_id(0) * nq + j
        if n:
            plan = _exchange_plan(in_refs, out_refs, *sems)
            pl.when(step_id == 0)(plan.start)

        @pl.when(j == 0)
        def _():
            dq_sc[...] = jnp.zeros(dq_sc.shape, F32)

        dk_sc[...] = jnp.zeros(dk_sc.shape, F32)
        dv_sc[...] = jnp.zeros(dv_sc.shape, F32)
        ks, vs = _heads(k_ref, hb), _heads(v_ref, hb)

        def step(i, carry, masked):
            rows = pl.ds(pl.multiple_of(i * tq, tq), tq)
            qs, dos = _heads(q_ref, hb, rows), _heads(do_ref, hb, rows)
            ss = [_dot_nt(qh, kh) for qh, kh in zip(qs, ks)]
            dps = [_dot_nt(doh, vh) for doh, vh in zip(dos, vs)]
            if masked:
                mask = _causal_mask(i * tq, j * tq, tq, tq, False)
                ss = [jnp.where(mask, s_, NEG) for s_ in ss]
            ps = [jnp.exp2(s_ - lse_ref[rows, a:a + 1]) for a, s_ in enumerate(ss)]
            dss = [(p * (dp - dl_ref[rows, a:a + 1])).astype(BF16) for a, (p, dp) in enumerate(zip(ps, dps))]
            for a, (ds, p, qh, kh, doh) in enumerate(zip(dss, ps, qs, ks, dos)):
                dq_sc[a, rows, :] += _dot(ds, kh)
                dk_sc[a] += _dot_tn(ds, qh)
                dv_sc[a] += _dot_tn(p.astype(BF16), doh)
            return carry

        split = jnp.where(j == 0, nq, j + 1)
        lax.fori_loop(j, split, lambda i, c: step(i, c, True), 0)
        lax.fori_loop(split, nq, lambda i, c: step(i, c, False), 0)
        for a in range(hb):
            dk_ref[:, LANE * a:LANE * (a + 1)] = (dk_sc[a] * (1.0 / LOG2E)).astype(BF16)
            dv_ref[:, LANE * a:LANE * (a + 1)] = dv_sc[a].astype(BF16)

        @pl.when(j == nq - 1)
        def _():
            for a in range(hb):
                dq_ref[:, LANE * a:LANE * (a + 1)] = (dq_sc[a] * SCALE_B).astype(BF16)

        if n:
            pl.when(step_id == steps - 1)(plan.finish)

    blk = pl.BlockSpec((tq, hb * LANE), lambda h, j: (j, h))
    full = pl.BlockSpec((t, hb * LANE), lambda h, j: (0, h))
    cols = pl.BlockSpec((None, t, hb), lambda h, j: (h, 0, 0))
    out = pl.pallas_call(
        body, name="mla_bwd_exchange" if n else "mla_bwd", grid=(MLA_HEADS // hb, nq),
        in_specs=[blk, blk, full, full, cols, cols] + [ANY] * n, out_specs=[full, blk, blk] + [ANY] * n,
        out_shape=[jax.ShapeDtypeStruct((t, HP), BF16)] * 3 + [jax.ShapeDtypeStruct(a.shape, a.dtype) for a in slabs],
        scratch_shapes=[pltpu.VMEM((hb, t, LANE), F32)] + [pltpu.VMEM((hb, tq, LANE), F32)] * 2
        + (_comm_sems(n) if n else []),
        compiler_params=_params("arbitrary", "arbitrary"),
    )(k, v, q, do, lse, dl, *slabs)
    return out[:3], out[3:]


def _pre_bwd(dh2, h, cq, ckv, dqa, dka, dka_next, dva, dva_next, dqb, dkf, dvb, g1, win, gq, wqu, gkv, wkv, tabs):
    t = h.shape[0]
    tm = _tile(t)

    def body(dh2_ref, h_ref, cq_ref, ckv_ref, dqa_ref, dka_ref, dkan_ref, dva_ref, dvan_ref, dqb_ref, dkf_ref, dvb_ref,
             g1_ref, win_ref, gq_ref, wqu_ref, gkv_ref, wkv_ref, tab_ref,
             dh_ref, dp_ref, dqbo_ref, dkvo_ref, dg1_ref, dgq_ref, dgkv_ref):
        first = pl.program_id(0) == 0
        ca, sa1, sa2, cb, sb1, sb2, ck = _tabs(tab_ref)
        dkr = jnp.zeros((tm, LANE), F32)
        for c in range(MLA_HEADS):
            sl = slice(LANE * c, LANE * (c + 1))
            dqbo_ref[:, sl] = _rope_t(dqb_ref[:, sl].astype(F32), cb, sb1, sb2, 16).astype(BF16)
            dkr += dkf_ref[:, sl].astype(F32)
        dkvo_ref[:, :HP] = dkf_ref[...]
        dkvo_ref[:, HP:] = dvb_ref[...]
        dcq, dgq = _rms_bwd(cq_ref[...], gq_ref[...], _dot(dqbo_ref[...], wqu_ref[...]), MLA_Q_RANK)
        dckv, dgkv = _rms_bwd(ckv_ref[...], gkv_ref[...], _dot(dkvo_ref[...], wkv_ref[...]), MLA_KV_RANK)
        for c in range(SWA_HEADS):
            sl = slice(LANE * c, LANE * (c + 1))
            dp_ref[:, PO_QA + LANE * c:PO_QA + LANE * (c + 1)] = _rope_t(dqa_ref[:, sl].astype(F32), ca, sa1, sa2,
                                                                          32).astype(BF16)
        last = slice(tm - BLOCK, tm)
        more = pl.program_id(0) < t // tm - 1
        for c in range(SWA_KV_HEADS):
            sl = slice(LANE * c, LANE * (c + 1))
            dk = dka_ref[:, sl]
            dk_last = dk[tm - BLOCK:] + jnp.where(more, dkan_ref[:, sl], 0.0)
            cols = slice(PO_KA + LANE * c, PO_KA + LANE * (c + 1))
            if tm > BLOCK:
                dp_ref[:tm - BLOCK, cols] = _rope_t(dk[:tm - BLOCK], ca[:tm - BLOCK], sa1[:tm - BLOCK], sa2[:tm - BLOCK],
                                                    32).astype(BF16)
            dp_ref[last, cols] = _rope_t(dk_last, ca[tm - BLOCK:], sa1[tm - BLOCK:], sa2[tm - BLOCK:], 32).astype(BF16)
        if tm > BLOCK:
            dp_ref[:tm - BLOCK, PO_VA:PO_CQ] = dva_ref[:tm - BLOCK, :].astype(BF16)
        dp_ref[last, PO_VA:PO_CQ] = (dva_ref[tm - BLOCK:, :] + jnp.where(more, dvan_ref[...], 0.0)).astype(BF16)
        dp_ref[:, PO_CQ:PO_CKV] = dcq.astype(BF16)
        dp_ref[:, PO_CKV:PO_KR] = dckv.astype(BF16)
        dp_ref[:, PO_KR:PW_IN] = _rope_t(dkr, ck, sb1, sb2, 16).astype(BF16)
        dx, dg1 = _rms_bwd(h_ref[...], g1_ref[...], _dot(dp_ref[...], win_ref[...]), D_MODEL)
        dh_ref[...] = dh2_ref[...] + dx
        _acc(dg1_ref, dg1, first)
        _acc(dgq_ref, dgq, first)
        _acc(dgkv_ref, dgkv, first)

    kv = _row(tm, 2 * LANE)
    nxt = pl.BlockSpec((BLOCK, 2 * LANE), lambda i: (jnp.minimum(i + 1, t // tm - 1), 0))
    return pl.pallas_call(
        body, name="pre_bwd", grid=(t // tm,),
        in_specs=[_row(tm, D_MODEL), _row(tm, D_MODEL), _row(tm, MLA_Q_RANK), _row(tm, MLA_KV_RANK), _row(tm, HP),
                  kv, nxt, kv, nxt, _row(tm, HP), _row(tm, HP), _row(tm, HP),
                  _const(g1.shape), _const(win.shape), _const(gq.shape), _const(wqu.shape), _const(gkv.shape),
                  _const(wkv.shape), _row(tm, N_TAB * LANE)],
        out_specs=[_row(tm, D_MODEL), _row(tm, PW_IN), _row(tm, HP), _row(tm, 2 * HP),
                   _const((1, D_MODEL)), _const((1, MLA_Q_RANK)), _const((1, MLA_KV_RANK))],
        out_shape=[jax.ShapeDtypeStruct((t, D_MODEL), F32), jax.ShapeDtypeStruct((t, PW_IN), BF16),
                   jax.ShapeDtypeStruct((t, HP), BF16), jax.ShapeDtypeStruct((t, 2 * HP), BF16),
                   jax.ShapeDtypeStruct((1, D_MODEL), F32), jax.ShapeDtypeStruct((1, MLA_Q_RANK), F32),
                   jax.ShapeDtypeStruct((1, MLA_KV_RANK), F32)],
        compiler_params=_params("arbitrary"),
    )(dh2, h, cq, ckv, dqa, dka, dka_next, dva, dva_next, dqb, dkf, dvb, g1, win, gq, wqu, gkv, wkv, tabs)


def _rope_tables(t):
    pos = (jnp.arange(t, dtype=jnp.int32) - FRONT).astype(F32)[:, None]
    lane = jnp.arange(LANE)[None, :]

    def table(dim, start):
        half = dim // 2
        inv = ROPE_THETA ** (-jnp.arange(0, dim, 2, dtype=F32) / dim)
        ang = pos * inv[None, :]
        cos = jnp.concatenate([jnp.cos(ang)] * 2, axis=1)
        sin = jnp.concatenate([jnp.sin(ang)] * 2, axis=1)
        pad = lambda a: jnp.pad(a, ((0, 0), (start, LANE - start - dim)))
        first = (lane >= start) & (lane < start + half)
        second = (lane >= start + half) & (lane < start + dim)
        return pad(cos), jnp.where(first, -pad(sin), 0.0), jnp.where(second, pad(sin), 0.0)

    ca, sa1, sa2 = table(SWA_HEAD_DIM, 0)
    ck, sb1, sb2 = table(MLA_ROPE_DIM, MLA_NOPE_DIM)
    cb = jnp.where(lane < MLA_NOPE_DIM, 1.0, ck)
    return jnp.concatenate([ca, sa1, sa2, cb, sb1, sb2, ck], axis=1)


def _pad_heads(w, heads, dim, axis):
    shp = w.shape
    w = w.reshape(shp[:axis] + (heads, dim) + shp[axis + 1:])
    pad = [(0, 0)] * w.ndim
    pad[axis + 1] = (0, LANE - dim)
    return jnp.pad(w, pad).reshape(shp[:axis] + (heads * LANE,) + shp[axis + 1:])


def _unpad_heads(w, heads, dim, axis):
    shp = w.shape
    w = w.reshape(shp[:axis] + (heads, LANE) + shp[axis + 1:])
    w = lax.slice_in_dim(w, 0, dim, axis=axis + 1)
    return w.reshape(shp[:axis] + (heads * dim,) + shp[axis + 1:])


def _pad_layer(w_in, w_q_up, w_kv_up):
    o1 = SWA_Q_W
    o2 = o1 + SWA_KV_W
    o3 = o2 + SWA_KV_W
    o4 = o3 + MLA_Q_RANK
    o5 = o4 + MLA_KV_RANK
    kr = jnp.pad(w_in[o5:], ((MLA_NOPE_DIM, LANE - MLA_QK_DIM), (0, 0)))
    win = jnp.concatenate([
        _pad_heads(w_in[:o1], SWA_HEADS, SWA_HEAD_DIM, 0),
        _pad_heads(w_in[o1:o2], SWA_KV_HEADS, SWA_HEAD_DIM, 0),
        _pad_heads(w_in[o2:o3], SWA_KV_HEADS, SWA_HEAD_DIM, 0),
        w_in[o3:o5], kr], axis=0)
    wqu = _pad_heads(w_q_up, MLA_HEADS, MLA_QK_DIM, 0)
    kv = w_kv_up.reshape(MLA_HEADS, MLA_NOPE_DIM + MLA_V_DIM, MLA_KV_RANK)
    wkv = jnp.concatenate([
        _pad_heads(kv[:, :MLA_NOPE_DIM].reshape(-1, MLA_KV_RANK), MLA_HEADS, MLA_NOPE_DIM, 0),
        _pad_heads(kv[:, MLA_NOPE_DIM:].reshape(-1, MLA_KV_RANK), MLA_HEADS, MLA_V_DIM, 0)], axis=0)
    return win, wqu, wkv


def _unpad_layer(dwin, dwqu, dwkv):
    d_w_in = jnp.concatenate([
        _unpad_heads(dwin[PO_QA:PO_KA], SWA_HEADS, SWA_HEAD_DIM, 0),
        _unpad_heads(dwin[PO_KA:PO_VA], SWA_KV_HEADS, SWA_HEAD_DIM, 0),
        _unpad_heads(dwin[PO_VA:PO_CQ], SWA_KV_HEADS, SWA_HEAD_DIM, 0),
        dwin[PO_CQ:PO_KR], dwin[PO_KR + MLA_NOPE_DIM:PO_KR + MLA_QK_DIM]], axis=0)
    d_w_q_up = _unpad_heads(dwqu, MLA_HEADS, MLA_QK_DIM, 0)
    dk = _unpad_heads(dwkv[:HP], MLA_HEADS, MLA_NOPE_DIM, 0).reshape(MLA_HEADS, MLA_NOPE_DIM, MLA_KV_RANK)
    dv = _unpad_heads(dwkv[HP:], MLA_HEADS, MLA_V_DIM, 0).reshape(MLA_HEADS, MLA_V_DIM, MLA_KV_RANK)
    d_w_kv_up = jnp.concatenate([dk, dv], axis=1).reshape(-1, MLA_KV_RANK)
    return d_w_in, d_w_q_up, d_w_kv_up


def _train_example(x, target, meta, vec, weights):
    s = x.shape[0]
    depth = vec["attn_norm"].shape[0]
    t = FRONT + N_META + s
    assert t % BLOCK == 0
    tabs = _rope_tables(t)
    h = jnp.concatenate([jnp.zeros((FRONT, D_MODEL), F32), meta, x], axis=0)
    tgt = jnp.concatenate([jnp.zeros((FRONT + N_META, D_MODEL), F32), target], axis=0)
    row = lambda v: v[None, :]

    saved = []
    for l in range(depth):
        win, wqu, wkv = _pad_layer(*weights.attn_in(l))
        g1, gq, gkv, g2, ga, gb = (row(vec[n][l]) for n in ("attn_norm", "q_norm", "kv_norm", "ffn_norm",
                                                            "out_norm_swa", "out_norm_mla"))
        sk = row(vec["sinks"][l])
        u, qa, ka, va, cq, ckv, qn, kvn, qb, kf, vb = _pre_fwd(h, g1, win, gq, wqu, gkv, wkv, tabs)
        oa = _swa_fwd(sk, qa, ka, va)
        ob, lse = weights.mla_fwd(l, qb, kf, vb)
        wo = weights.w_o(l)
        h2, mix, u2 = _mix_fwd(h, oa, ob, ga, gb, wo, g2)
        wg, wu, wd = weights.ffn(l)
        h3, gt, up = _ffn_fwd(h2, u2, wg, wu, wd)
        saved.append((h, u, qa, ka, va, cq, ckv, qn, kvn, qb, kf, vb, oa, ob, lse, h2, mix, u2, gt, up,
                      win, wqu, wkv, wo, ga, gb, g1, gq, gkv, g2, sk, wg, wu, wd))
        h = h3

    dh, d_final, loss = _loss_bwd(h, row(vec["final_norm"]), tgt)

    grads = []
    for l in reversed(range(depth)):
        (h0, u, qa, ka, va, cq, ckv, qn, kvn, qb, kf, vb, oa, ob, lse, h2, mix, u2, gt, up,
         win, wqu, wkv, wo, ga, gb, g1, gq, gkv, g2, sk, wg, wu, wd) = saved[l]
        dff = wd.shape[0]
        act, dgu, dhb = _ffn_bwd_a(dh, gt, up, wd)
        weights.ffn_grads(l, _tn_matmul(dgu, u2, "dw_gate", (0, dff)), _tn_matmul(dgu, u2, "dw_up", (dff, dff)),
                          _tn_matmul(act, dhb, "dw_down"))
        dh2, dh2b, d_g2 = weights.ffn_bwd_b(l, dh, dgu, h2, g2, wg, wu)
        weights.attn_grads(l, w_o=_tn_matmul(mix, dh2b, "dw_o"))
        doa, dob, dl, d_ga, d_gb = _mix_bwd(dh2b, oa, ob, ga, gb, wo)
        dqa, dkc, dkp, dvc, dvp, dsink = weights.swa_bwd(l, sk, qa, ka, va, oa, doa)
        dqb, dkf, dvb = weights.mla_bwd(l, qb, kf, vb, dob, lse, dl)
        dh, dp, dqbo, dkvo, d_g1, d_gq, d_gkv = _pre_bwd(
            dh2, h0, cq, ckv, dqa, dkc, dkp, dvc, dvp, dqb, dkf, dvb,
            g1, win, gq, wqu, gkv, wkv, tabs)
        d_win = _tn_matmul(dp, u, "dw_in")
        d_wqu = _tn_matmul(dqbo, qn, "dw_q_up")
        d_wkv = _tn_matmul(dkvo, kvn, "dw_kv_up")
        weights.attn_grads(l, **dict(zip(ATTN_IN, _unpad_layer(d_win, d_wqu, d_wkv))))
        grads.append(dict(attn_norm=d_g1[0], q_norm=d_gq[0], kv_norm=d_gkv[0], sinks=dsink[:, 0], out_norm_swa=d_ga[0],
                          out_norm_mla=d_gb[0], ffn_norm=d_g2[0]))
    grads = grads[::-1]
    stacked = {k: jnp.stack([g[k] for g in grads]) for k in grads[0]}
    stacked["final_norm"] = d_final[0]
    return loss[0, 0], dh[FRONT + N_META:], dh[FRONT:FRONT + N_META], stacked


MESH = pl.DeviceIdType.MESH
ANY = pl.BlockSpec(memory_space=pl.ANY)


def _place():
    return lax.axis_index("x"), lax.axis_index("y"), lax.axis_index("c")


def _index(x, y, c):
    return 4 * x + 2 * y + c


def _comm_sems(n):
    return [pltpu.SemaphoreType.DMA((n, N_DEV - 1)), pltpu.SemaphoreType.DMA((n, N_DEV - 1)),
            pltpu.SemaphoreType.DMA((n,))]


class _gather_plan:
    def __init__(self, x_refs, out_refs, send_sems, recv_sems, local_sems):
        self.x_refs, self.out_refs = x_refs, out_refs
        self.send_sems, self.recv_sems, self.local_sems = send_sems, recv_sems, local_sems
        self.n = len(x_refs)

    def _where(self):
        x, y, c = _place()
        return (x, y, c), (x, y, 1 - c), [(1 - x, y), (x, 1 - y), (1 - x, 1 - y)], c

    def _copy(self, i, k, block, to, from_input=False):
        slot = self.out_refs[i].at[_index(*block)]
        return pltpu.make_async_remote_copy(
            src_ref=self.x_refs[i] if from_input else slot, dst_ref=slot,
            send_sem=self.send_sems.at[i, k], recv_sem=self.recv_sems.at[i, k], device_id=to, device_id_type=MESH)

    def _mine(self, i, me):
        return pltpu.make_async_copy(self.x_refs[i], self.out_refs[i].at[_index(*me)], self.local_sems.at[i])

    def _first(self, me, sibling, chips, c):
        out = [self._copy(i, 1 + j, me, (*chip, c), True) for j, chip in enumerate(chips) for i in range(self.n)]
        return out + [self._copy(i, 0, me, sibling, True) for i in range(self.n)]

    def start(self):
        me, sibling, chips, c = self._where()
        for i in range(self.n):
            self._mine(i, me).start()
        for cp in self._first(me, sibling, chips, c):
            cp.start()

    def forward(self):
        me, sibling, chips, c = self._where()
        for j, chip in enumerate(chips):
            for i in range(self.n):
                self._copy(i, 1 + j, (*chip, c), me).wait_recv()
                self._copy(i, 4 + j, (*chip, c), sibling).start()

    def finish(self):
        me, sibling, chips, c = self._where()
        for i in range(self.n):
            self._copy(i, 0, sibling, me).wait_recv()
            for j, chip in enumerate(chips):
                self._copy(i, 4 + j, (*chip, 1 - c), me).wait_recv()
        for cp in self._first(me, sibling, chips, c):
            cp.wait_send()
        for j, chip in enumerate(chips):
            for i in range(self.n):
                self._copy(i, 4 + j, (*chip, c), sibling).wait_send()
        for i in range(self.n):
            self._mine(i, me).wait()


class _exchange_plan:
    def __init__(self, in_refs, out_refs, send_sems, recv_sems, local_sems):
        self.in_refs, self.out_refs = in_refs, out_refs
        self.send_sems, self.recv_sems, self.local_sems = send_sems, recv_sems, local_sems
        self.n = len(in_refs)

    def _copies(self):
        x, y, c = _place()
        me = _index(x, y, c)
        mine = [pltpu.make_async_copy(self.in_refs[i].at[me], self.out_refs[i].at[me], self.local_sems.at[i])
                for i in range(self.n)]
        remote = []
        for k in range(1, N_DEV):
            peer = (1 - x if k & 4 else x, 1 - y if k & 2 else y, 1 - c if k & 1 else c)
            remote += [pltpu.make_async_remote_copy(
                src_ref=self.in_refs[i].at[_index(*peer)], dst_ref=self.out_refs[i].at[me],
                send_sem=self.send_sems.at[i, k - 1], recv_sem=self.recv_sems.at[i, k - 1],
                device_id=peer, device_id_type=MESH) for i in range(self.n)]
        return mine, remote

    def start(self):
        mine, remote = self._copies()
        for cp in mine + remote:
            cp.start()

    def finish(self):
        mine, remote = self._copies()
        for cp in remote:
            cp.wait_recv()
        for cp in remote:
            cp.wait_send()
        for cp in mine:
            cp.wait()


def _all_gather(shards, name):
    n = len(shards)

    def body(*refs):
        plan = _gather_plan(refs[:n], refs[n:2 * n], *refs[2 * n:])
        plan.start()
        plan.forward()
        plan.finish()

    return pl.pallas_call(
        body, name=name, in_specs=[ANY] * n, out_specs=[ANY] * n, scratch_shapes=_comm_sems(n),
        out_shape=[jax.ShapeDtypeStruct((N_DEV,) + a.shape, a.dtype) for a in shards],
    )(*shards)


def _exchange(slabs, name):
    n = len(slabs)

    def body(*refs):
        plan = _exchange_plan(refs[:n], refs[n:2 * n], *refs[2 * n:])
        plan.start()
        plan.finish()

    return pl.pallas_call(
        body, name=name, in_specs=[ANY] * n, out_specs=[ANY] * n, scratch_shapes=_comm_sems(n),
        out_shape=[jax.ShapeDtypeStruct(a.shape, a.dtype) for a in slabs],
    )(*slabs)


def _adamw(w, g, m, v):
    m = ADAM_B1 * m + (1.0 - ADAM_B1) * g
    v = ADAM_B2 * v + (1.0 - ADAM_B2) * (g * g)
    m_hat = m / (1.0 - ADAM_B1 ** ADAM_STEP)
    v_hat = v / (1.0 - ADAM_B2 ** ADAM_STEP)
    return -ADAM_LR * (m_hat / (jnp.sqrt(v_hat) + ADAM_EPS) + ADAM_WD * w), m, v


def _sum_slots(ref):
    g = ref[0].astype(F32)
    for s in range(1, N_DEV):
        g = g + ref[s].astype(F32)
    return g


def _reduce_adamw(parts, w, m, v, layer, outs, name):
    l, r, c = w.shape
    tile = next(t for t in (256, 128, r) if r % t == 0)

    def body(p_ref, w_ref, m_ref, v_ref, g0, d0, m0, v0, g_ref, d_ref, nm_ref, nv_ref):
        g = _sum_slots(p_ref)
        g_ref[...] = g
        d_ref[...], nm_ref[...], nv_ref[...] = _adamw(w_ref[...], g, m_ref[...], v_ref[...])

    blk = pl.BlockSpec((None, tile, c), lambda j: (layer, j, 0))
    return pl.pallas_call(
        body, name=name, grid=(r // tile,),
        in_specs=[pl.BlockSpec((N_DEV, tile, c), lambda j: (0, j, 0)), blk, blk, blk] + [ANY] * 4, out_specs=[blk] * 4,
        out_shape=[jax.ShapeDtypeStruct((l, r, c), F32)] * 4,
        input_output_aliases={4: 0, 5: 1, 6: 2, 7: 3},
        compiler_params=_params("parallel"),
    )(parts, w, m, v, *outs)


def _sum_parts(parts, name):
    _, r, c = parts.shape

    def body(p_ref, g_ref):
        g_ref[...] = _sum_slots(p_ref)

    return pl.pallas_call(body, name=name, out_shape=jax.ShapeDtypeStruct((r, c), F32))(parts)


def _adamw_call(w, g, m, v, name):
    def body(w_ref, g_ref, m_ref, v_ref, d_ref, nm_ref, nv_ref):
        d_ref[...], nm_ref[...], nv_ref[...] = _adamw(w_ref[...], g_ref[...], m_ref[...], v_ref[...])

    return pl.pallas_call(body, name=name, out_shape=[jax.ShapeDtypeStruct(w.shape, F32)] * 3)(w, g, m, v)


ATTN_IN = ("w_in", "w_q_up", "w_kv_up")
ATTN = ATTN_IN + ("w_o",)
FFN = ("w_gate", "w_up", "w_down")
TRANSPOSED = ("w_in", "w_q_up", "w_kv_up", "w_gate", "w_up")
SMALL = ("attn_norm", "ffn_norm", "final_norm", "out_norm_swa", "out_norm_mla", "q_norm", "kv_norm", "sinks")
PACK_W = 1024
SMALL_ROWS = 16


def _pack(arrs, dtype):
    flat = jnp.concatenate([a.astype(dtype).reshape(-1) for a in arrs])
    return flat.reshape(-1, PACK_W)


def _unpack(packed, like):
    flat = packed.reshape(-1)
    out, off = [], 0
    for a in like:
        out.append(flat[off:off + a.size].reshape(a.shape))
        off += a.size
    return out


def _gather_to_full(gathered):
    return gathered.reshape((-1,) + gathered.shape[2:])


def _full_to_slabs(full):
    return full.reshape((N_DEV, -1) + full.shape[1:])


class _ShardedWeights:
    def __init__(self, shards, depth):
        self.shards, self.depth = shards, depth
        self.gathered, self.pending, self.parts = {}, {}, {}
        self._gather([(n, 0) for n in ATTN_IN], lambda xs: _all_gather(xs, "gather_attn0"))

    def _gather(self, keys, run):
        self.gathered.update(zip(keys, run([self.shards[n][l] for n, l in keys])))

    def _full(self, names, l):
        return tuple(_gather_to_full(self.gathered[n, l]) for n in names)

    def attn_in(self, l):
        return self._full(ATTN_IN, l)

    def w_o(self, l):
        return self._full(("w_o",), l)[0]

    def ffn(self, l):
        return self._full(FFN, l)

    def mla_fwd(self, l, q, k, v):
        keys = [(n, l) for n in ("w_o",) + FFN] + ([(n, l + 1) for n in ATTN_IN] if l + 1 < self.depth else [])
        out = []
        self._gather(keys, lambda xs: out.extend(_mla_fwd(q, k, v, xs)) or out[2])
        return out[0], out[1]

    def _add(self, names, l, grads):
        for n, g in zip(names, grads):
            self.pending[n, l] = _full_to_slabs(g)

    def ffn_grads(self, l, *grads):
        self._add(FFN, l, grads)

    def attn_grads(self, l, **grads):
        self._add(list(grads), l, grads.values())

    def _exchange(self, run, names=None):
        keys = [k for k in self.pending if names is None or k[0] in names]
        self.parts.update(zip(keys, run([self.pending.pop(k) for k in keys])))

    def _carried(self, kernel, args, names=None):
        out = []
        self._exchange(lambda xs: out.extend(kernel(*args, xs)) or out[1], names)
        return out[0]

    def ffn_bwd_b(self, l, *args):
        return self._carried(_ffn_bwd_b, args, ("w_gate",))

    def swa_bwd(self, l, *args):
        return self._carried(_swa_bwd, args, ("w_up",))

    def mla_bwd(self, l, *args):
        return self._carried(_mla_bwd, args)

    def flush(self):
        self._exchange(lambda xs: _exchange(xs, "exchange_attn0"))


def kernel(x, meta_tokens, attn_norm, w_in, q_norm, w_q_up, kv_norm, w_kv_up, sinks, out_norm_swa, out_norm_mla, w_o, ffn_norm, w_gate, w_up, w_down, final_norm, loss_target, m_meta_tokens, m_attn_norm, m_w_in, m_q_norm, m_w_q_up, m_kv_norm, m_w_kv_up, m_sinks, m_out_norm_swa, m_out_norm_mla, m_w_o, m_ffn_norm, m_w_gate, m_w_up, m_w_down, m_final_norm, v_meta_tokens, v_attn_norm, v_w_in, v_q_norm, v_w_q_up, v_kv_norm, v_w_kv_up, v_sinks, v_out_norm_swa, v_out_norm_mla, v_w_o, v_ffn_norm, v_w_gate, v_w_up, v_w_down, v_final_norm):
    w = dict(meta_tokens=meta_tokens, attn_norm=attn_norm, w_in=w_in, q_norm=q_norm, w_q_up=w_q_up, kv_norm=kv_norm,
             w_kv_up=w_kv_up, sinks=sinks, out_norm_swa=out_norm_swa, out_norm_mla=out_norm_mla, w_o=w_o,
             ffn_norm=ffn_norm, w_gate=w_gate, w_up=w_up, w_down=w_down, final_norm=final_norm)
    m = dict(meta_tokens=m_meta_tokens, attn_norm=m_attn_norm, w_in=m_w_in, q_norm=m_q_norm, w_q_up=m_w_q_up,
             kv_norm=m_kv_norm, w_kv_up=m_w_kv_up, sinks=m_sinks, out_norm_swa=m_out_norm_swa,
             out_norm_mla=m_out_norm_mla, w_o=m_w_o, ffn_norm=m_ffn_norm, w_gate=m_w_gate, w_up=m_w_up,
             w_down=m_w_down, final_norm=m_final_norm)
    v = dict(meta_tokens=v_meta_tokens, attn_norm=v_attn_norm, w_in=v_w_in, q_norm=v_q_norm, w_q_up=v_w_q_up,
             kv_norm=v_kv_norm, w_kv_up=v_w_kv_up, sinks=v_sinks, out_norm_swa=v_out_norm_swa,
             out_norm_mla=v_out_norm_mla, w_o=v_w_o, ffn_norm=v_ffn_norm, w_gate=v_w_gate, w_up=v_w_up,
             w_down=v_w_down, final_norm=v_final_norm)
    names = list(w)
    big = ATTN + FFN
    depth = w_in.shape[0]
    me = _index(*_place())

    as_held = lambda n, a: jnp.swapaxes(a, 1, 2) if n in TRANSPOSED else a
    weights = _ShardedWeights({n: as_held(n, w[n]).astype(BF16) for n in big}, depth)
    meta = jnp.moveaxis(_all_gather([meta_tokens], "gather_meta")[0], 0, 1).reshape(N_META, D_MODEL)
    loss, grad_x, d_meta, grads = _train_example(x[0], loss_target[0], meta, {n: w[n] for n in SMALL}, weights)
    weights.flush()

    g_big, d_big, m_big, v_big = {}, {}, {}, {}
    for n in big:
        held = [as_held(n, a) for a in (w[n], m[n], v[n])]
        outs = [lax.empty(held[0].shape, F32) for _ in range(4)]
        for l in reversed(range(depth)):
            outs = _reduce_adamw(weights.parts[n, l], *held, l, outs, "reduce_adamw_" + n)
        g_big[n], d_big[n], m_big[n], v_big[n] = [as_held(n, a) for a in outs]

    small = [grads[n] for n in SMALL] + [loss.reshape(1)]
    pad = SMALL_ROWS * PACK_W - sum(a.size for a in small)
    part = jnp.concatenate([_pack(small + [jnp.zeros((pad,), F32)], F32), d_meta], axis=0)
    total = _sum_parts(_all_gather([part], "gather_small")[0], "sum_small")
    small_w = [w[n] for n in SMALL]
    packs = [_pack([d[n] for n in SMALL] + [jnp.zeros((pad + 1,), F32)], F32) for d in (w, m, v)]
    upd = _adamw_call(packs[0], total[:SMALL_ROWS], packs[1], packs[2], "adamw_small")
    g_small, d_small, m_small, v_small = [dict(zip(SMALL, _unpack(p, small_w))) for p in (total[:SMALL_ROWS],) + tuple(upd)]
    loss_total = total[:SMALL_ROWS].reshape(-1)[SMALL_ROWS * PACK_W - pad - 1]
    g_meta = lax.dynamic_slice_in_dim(total[SMALL_ROWS:], me * LANE, LANE, axis=1)
    d_mt, m_mt, v_mt = _adamw_call(meta_tokens, g_meta, m_meta_tokens, v_meta_tokens, "adamw_meta")

    outs = []
    for got in ({**g_big, **g_small, "meta_tokens": g_meta}, {**d_big, **d_small, "meta_tokens": d_mt},
                {**m_big, **m_small, "meta_tokens": m_mt}, {**v_big, **v_small, "meta_tokens": v_mt}):
        outs += [got[n] for n in names]
    return (loss_total, grad_x[None], *outs)
```

```python
import jax
import jax.numpy as jnp
from jax import lax
from jax.experimental import pallas as pl
from jax.experimental.pallas import tpu as pltpu

F32 = jnp.float32
BF16 = jnp.bfloat16

D_MODEL = 1024
N_META = 16
BLOCK = 128
FRONT = (-N_META) % BLOCK
ROPE_THETA = 10000.0
EPS = 1e-6
NEG = -1e30
SWA_HEADS = 8
SWA_KV_HEADS = 2
SWA_GROUP = SWA_HEADS // SWA_KV_HEADS
SWA_HEAD_DIM = 64
MLA_HEADS = 8
MLA_Q_RANK = 256
MLA_KV_RANK = 128
MLA_NOPE_DIM = 64
MLA_ROPE_DIM = 32
MLA_V_DIM = 64
MLA_QK_DIM = MLA_NOPE_DIM + MLA_ROPE_DIM
SWA_Q_W = SWA_HEADS * SWA_HEAD_DIM
SWA_KV_W = SWA_KV_HEADS * SWA_HEAD_DIM
MLA_OUT_W = MLA_HEADS * MLA_V_DIM
SCALE_A = SWA_HEAD_DIM ** -0.5
SCALE_B = MLA_QK_DIM ** -0.5
LOG2E = 1.4426950408889634
Q_SCALE = SCALE_B * LOG2E
ADAM_LR = 0.001
ADAM_B1 = 0.9
ADAM_B2 = 0.999
ADAM_EPS = 1e-08
ADAM_WD = 0.01
ADAM_STEP = 10

LANE = 128
N_DEV = 8
HP = 8 * LANE
PO_QA, PO_KA, PO_VA = 0, HP, HP + 2 * LANE
PO_CQ = PO_VA + 2 * LANE
PO_CKV = PO_CQ + MLA_Q_RANK
PO_KR = PO_CKV + MLA_KV_RANK
PW_IN = PO_KR + LANE
N_TAB = 7
VMEM_LIMIT = 56 * 2 ** 20
TN_VMEM_BUDGET = 36 * 2 ** 20
MLA_HB = 4
MLA_HB_FWD = 8
HALF = LANE // 2
assert SWA_HEAD_DIM == HALF and MLA_V_DIM == HALF

NT = (((1,), (1,)), ((), ()))
TN = (((0,), (0,)), ((), ()))


def _tile(t):
    return 384 if t % 384 == 0 else 128


def _params(*sem):
    return pltpu.CompilerParams(dimension_semantics=sem, vmem_limit_bytes=VMEM_LIMIT)


def _row(tm, n):
    return pl.BlockSpec((tm, n), lambda i: (i, 0))


def _const(shape):
    return pl.BlockSpec(shape, lambda i: (0,) * len(shape))


def _dot(a, b):
    return jnp.dot(a, b, preferred_element_type=F32)


def _dot_nt(a, b):
    return lax.dot_general(a, b, NT, preferred_element_type=F32)


def _dot_tn(a, b):
    return lax.dot_general(a, b, TN, preferred_element_type=F32)


def _rope(x, c, s1, s2, shift):
    return x * c + pltpu.roll(x, LANE - shift, 1) * s1 + pltpu.roll(x, shift, 1) * s2


def _rope_t(dy, c, s1, s2, shift):
    return dy * c + pltpu.roll(dy * s1, shift, 1) + pltpu.roll(dy * s2, LANE - shift, 1)


def _rms_r(x, n):
    return lax.rsqrt(jnp.sum(x * x, axis=-1, keepdims=True) * (1.0 / n) + EPS)


def _rms_bwd(x, g, dy, n):
    r = _rms_r(x, n)
    xh = x * r
    dxh = dy * g
    dx = r * (dxh - xh * (jnp.sum(dxh * xh, axis=-1, keepdims=True) * (1.0 / n)))
    return dx, jnp.sum(dy * xh, axis=0, keepdims=True)


def _acc(ref, val, first):
    @pl.when(first)
    def _():
        ref[...] = val

    @pl.when(jnp.logical_not(first))
    def _():
        ref[...] += val


def _pack_pair(even, odd):
    return even + pltpu.roll(odd, HALF, 1)


def _pair_half(slab, half):
    return slab if half == 0 else pltpu.roll(slab, HALF, 1)


def _unpack_pair(slab, half):
    x = _pair_half(slab, half)
    return jnp.where(lax.broadcasted_iota(jnp.int32, x.shape, 1) < HALF, x, 0.0)


def _tabs(tab_ref):
    return [tab_ref[:, LANE * i:LANE * (i + 1)] for i in range(N_TAB)]


def _pre_fwd(h, g1, win, gq, wqu, gkv, wkv, tabs):
    t = h.shape[0]
    tm = _tile(t)

    def body(h_ref, g1_ref, win_ref, gq_ref, wqu_ref, gkv_ref, wkv_ref, tab_ref,
             u_ref, qa_ref, ka_ref, va_ref, cq_ref, ckv_ref, qn_ref, kvn_ref, qb_ref, kf_ref, vb_ref):
        ca, sa1, sa2, cb, sb1, sb2, ck = _tabs(tab_ref)
        hv = h_ref[...]
        u = (hv * _rms_r(hv, D_MODEL) * g1_ref[...]).astype(BF16)
        u_ref[...] = u
        p = _dot_nt(u, win_ref[...])
        for c in range(SWA_HEADS):
            sl = slice(LANE * c, LANE * (c + 1))
            qa_ref[:, sl] = _rope(p[:, PO_QA + LANE * c:PO_QA + LANE * (c + 1)], ca, sa1, sa2, 32).astype(BF16)
        for c in range(SWA_KV_HEADS):
            sl = slice(LANE * c, LANE * (c + 1))
            ka_ref[:, sl] = _rope(p[:, PO_KA + LANE * c:PO_KA + LANE * (c + 1)], ca, sa1, sa2, 32).astype(BF16)
        va_ref[...] = p[:, PO_VA:PO_CQ].astype(BF16)
        cq = p[:, PO_CQ:PO_CKV]
        ckv = p[:, PO_CKV:PO_KR]
        cq_ref[...] = cq
        ckv_ref[...] = ckv
        qn = (cq * _rms_r(cq, MLA_Q_RANK) * gq_ref[...]).astype(BF16)
        qn_ref[...] = qn
        qb = _dot_nt(qn, wqu_ref[...])
        kvn = (ckv * _rms_r(ckv, MLA_KV_RANK) * gkv_ref[...]).astype(BF16)
        kvn_ref[...] = kvn
        kv = _dot_nt(kvn, wkv_ref[...])
        kr = _rope(p[:, PO_KR:PW_IN], ck, sb1, sb2, 16)
        for c in range(MLA_HEADS):
            sl = slice(LANE * c, LANE * (c + 1))
            qb_ref[:, sl] = (_rope(qb[:, sl], cb, sb1, sb2, 16) * Q_SCALE).astype(BF16)
            kf_ref[:, sl] = (kv[:, sl] + kr).astype(BF16)
        vb_ref[...] = kv[:, HP:].astype(BF16)

    widths = [(D_MODEL, BF16), (HP, BF16), (2 * LANE, BF16), (2 * LANE, BF16), (MLA_Q_RANK, F32),
              (MLA_KV_RANK, F32), (MLA_Q_RANK, BF16), (MLA_KV_RANK, BF16), (HP, BF16), (HP, BF16), (HP, BF16)]
    return pl.pallas_call(
        body, name="pre_fwd", grid=(t // tm,),
        in_specs=[_row(tm, D_MODEL), _const(g1.shape), _const(win.shape), _const(gq.shape), _const(wqu.shape),
                  _const(gkv.shape), _const(wkv.shape), _row(tm, N_TAB * LANE)],
        out_specs=[_row(tm, w) for w, _ in widths],
        out_shape=[jax.ShapeDtypeStruct((t, w), d) for w, d in widths],
        compiler_params=_params("parallel"),
    )(h, g1, win, gq, wqu, gkv, wkv, tabs)


def _swa_mask(nb):
    key = lax.broadcasted_iota(jnp.int32, (2 * BLOCK, SWA_GROUP * BLOCK), 0)
    qry = lax.broadcasted_iota(jnp.int32, (2 * BLOCK, SWA_GROUP * BLOCK), 1) & (BLOCK - 1)
    return (key > qry) & (key <= qry + BLOCK) & (key + (nb - 1) * BLOCK >= FRONT)


def _swa_group(ref, rows, j):
    return jnp.concatenate([ref[rows, LANE * (SWA_GROUP * j + g):LANE * (SWA_GROUP * j + g + 1)]
                            for g in range(SWA_GROUP)], axis=0)


def _swa_packed_group(ref, rows, j):
    heads = [SWA_GROUP * j + g for g in range(SWA_GROUP)]
    return jnp.concatenate([_pair_half(ref[rows, LANE * (hd // 2):LANE * (hd // 2 + 1)], hd % 2) for hd in heads], axis=0)


def _swa_sinks(sink_ref, j):
    return jnp.concatenate([jnp.full((1, BLOCK), sink_ref[0, SWA_GROUP * j + g], F32) for g in range(SWA_GROUP)], axis=1)


def _swa_keys(prev_ref, cur_ref, rb, j):
    sl = slice(LANE * j, LANE * (j + 1))
    if rb == 0:
        return jnp.concatenate([prev_ref[:, sl], cur_ref[:BLOCK, sl]], axis=0)
    return cur_ref[BLOCK * (rb - 1):BLOCK * (rb + 1), sl]


def _swa_chains(t):
    return [(rb, j) for rb in range(_tile(t) // BLOCK) for j in range(SWA_KV_HEADS)]


def _swa_scores(sink_ref, q_ref, kp_ref, kc_ref, n, t):
    r = _tile(t) // BLOCK
    chains = _swa_chains(t)
    qs = [_swa_group(q_ref, slice(BLOCK * rb, BLOCK * (rb + 1)), j) for rb, j in chains]
    ks = [_swa_keys(kp_ref, kc_ref, rb, j) for rb, j in chains]
    ss = [_dot_nt(k2, q4) for q4, k2 in zip(qs, ks)]
    masks = [_swa_mask(n * r + rb) for rb in range(r)]
    out = []
    for (rb, j), s in zip(chains, ss):
        sink = _swa_sinks(sink_ref, j)
        s = jnp.where(masks[rb], s * SCALE_A, NEG)
        m = jnp.maximum(jnp.max(s, axis=0, keepdims=True), sink)
        e = jnp.exp(s - m)
        es = jnp.exp(sink - m)
        inv = 1.0 / (jnp.sum(e, axis=0, keepdims=True) + es)
        out.append((e * inv, es * inv))
    return qs, ks, out


def _swa_specs(t):
    ts = _tile(t)
    r = ts // BLOCK
    prev = lambda n: (jnp.maximum(n * r - 1, 0), 0)
    cur = lambda n: (n, 0)
    return [pl.BlockSpec(memory_space=pltpu.SMEM), pl.BlockSpec((ts, HP), cur),
            pl.BlockSpec((BLOCK, 2 * LANE), prev), pl.BlockSpec((ts, 2 * LANE), cur),
            pl.BlockSpec((BLOCK, 2 * LANE), prev), pl.BlockSpec((ts, 2 * LANE), cur)]


def _swa_fwd(sinks, q, k, v):
    t = q.shape[0]
    ts = _tile(t)

    def body(sink_ref, q_ref, kp_ref, kc_ref, vp_ref, vc_ref, o_ref):
        chains = _swa_chains(t)
        _, _, probs = _swa_scores(sink_ref, q_ref, kp_ref, kc_ref, pl.program_id(0), t)
        os_ = [_dot_tn(p.astype(BF16), _swa_keys(vp_ref, vc_ref, rb, j)) for (rb, j), (p, _) in zip(chains, probs)]
        for (rb, j), o4 in zip(chains, os_):
            for g in range(0, SWA_GROUP, 2):
                pair = (SWA_GROUP * j + g) // 2
                o_ref[BLOCK * rb:BLOCK * (rb + 1), LANE * pair:LANE * (pair + 1)] = _pack_pair(
                    o4[BLOCK * g:BLOCK * (g + 1)], o4[BLOCK * (g + 1):BLOCK * (g + 2)])

    return pl.pallas_call(
        body, name="swa_fwd", grid=(t // ts,),
        in_specs=_swa_specs(t),
        out_specs=pl.BlockSpec((ts, SWA_Q_W), lambda n: (n, 0)),
        out_shape=jax.ShapeDtypeStruct((t, SWA_Q_W), F32),
        compiler_params=_params("parallel"),
    )(sinks, q, k, k, v, v)


def _causal_mask(q0, k0, tq, tk, transposed):
    if transposed:
        key = k0 + lax.broadcasted_iota(jnp.int32, (tk, tq), 0)
        qry = q0 + lax.broadcasted_iota(jnp.int32, (tk, tq), 1)
    else:
        qry = q0 + lax.broadcasted_iota(jnp.int32, (tq, tk), 0)
        key = k0 + lax.broadcasted_iota(jnp.int32, (tq, tk), 1)
    return (key <= qry) & (key >= FRONT)


def _heads(ref, hb, rows=slice(None)):
    return [ref[rows, LANE * a:LANE * (a + 1)] for a in range(hb)]


def _head_stats(t, hb=MLA_HB):
    return jax.ShapeDtypeStruct((MLA_HEADS // hb, t, hb), F32)


def _mla_fwd(q, k, v, shards=()):
    t = q.shape[0]
    tq = _tile(t)
    nq = t // tq
    n = len(shards)
    hb = MLA_HB_FWD
    steps = (MLA_HEADS // hb) * nq

    def body(q_ref, k_ref, v_ref, *rest):
        x_refs, (o_ref, lse_ref), out_refs = rest[:n], rest[n:n + 2], rest[n + 2:2 * n + 2]
        acc_sc, sems = rest[2 * n + 2], rest[2 * n + 3:]
        i = pl.program_id(1)
        step_id = pl.program_id(0) * nq + i
        if n:
            plan = _gather_plan(x_refs, out_refs, *sems)
            pl.when(step_id == 0)(plan.start)
            pl.when(step_id == (3 * steps) // 4)(plan.forward)
        qs = _heads(q_ref, hb)
        acc_sc[...] = jnp.zeros(acc_sc.shape, F32)

        def step(j, carry, masked):
            rows = pl.ds(pl.multiple_of(j * tq, tq), tq)
            ks, vs = _heads(k_ref, hb, rows), _heads(v_ref, hb, rows)
            ss = [_dot_nt(kh, qh) for qh, kh in zip(qs, ks)]
            if masked:
                mask = _causal_mask(i * tq, j * tq, tq, tq, True)
                ss = [jnp.where(mask, s, NEG) for s in ss]
            mid, out = [], []
            for s, (m, l) in zip(ss, carry):
                mn = jnp.maximum(m, jnp.max(s, axis=0, keepdims=True))
                al = jnp.exp2(m - mn)
                p = jnp.exp2(s - mn)
                out.append((mn, al * l + jnp.sum(p, axis=0, keepdims=True)))
                mid.append((al, p.astype(BF16)))
            for a, ((al, p), vh) in enumerate(zip(mid, vs)):
                acc_sc[a] = al * acc_sc[a] + _dot_tn(vh, p)
            return tuple(out)

        init = ((jnp.full((1, tq), NEG, F32), jnp.zeros((1, tq), F32)),) * hb
        carry = lax.fori_loop(0, jnp.minimum(i, 1) + 1, lambda it, c: step(it * i, c, True), init)
        carry = lax.fori_loop(1, i, lambda j, c: step(j, c, False), carry)
        outs = [(acc_sc[a] * (1.0 / l)).T for a, (_, l) in enumerate(carry)]
        for a in range(0, hb, 2):
            o_ref[:, HALF * a:HALF * (a + 2)] = _pack_pair(outs[a], outs[a + 1])
        for a, (m, l) in enumerate(carry):
            lse_ref[:, a:a + 1] = jnp.broadcast_to(m + jnp.log2(l), (LANE, tq)).T[:, :1]
        if n:
            pl.when(step_id == steps - 1)(plan.finish)

    blk = pl.BlockSpec((tq, hb * LANE), lambda h, i: (i, h))
    full = pl.BlockSpec((t, hb * LANE), lambda h, i: (0, h))
    packed = pl.BlockSpec((tq, hb * HALF), lambda h, i: (i, h))
    out = pl.pallas_call(
        body, name="mla_fwd_gather" if n else "mla_fwd", grid=(MLA_HEADS // hb, nq),
        in_specs=[blk, full, full] + [ANY] * n,
        out_specs=[packed, pl.BlockSpec((None, tq, hb), lambda h, i: (h, i, 0))] + [ANY] * n,
        out_shape=[jax.ShapeDtypeStruct((t, MLA_OUT_W), F32), _head_stats(t, hb)]
        + [jax.ShapeDtypeStruct((N_DEV,) + a.shape, a.dtype) for a in shards],
        scratch_shapes=[pltpu.VMEM((hb, LANE, tq), F32)] + (_comm_sems(n) if n else []),
        compiler_params=_params("arbitrary", "arbitrary"),
    )(q, k, v, *shards)
    return out[0], out[1], out[2:]


def _mix_fwd(h, oa, ob, ga, gb, wo, g2):
    t = h.shape[0]
    tm = _tile(t)

    def body(h_ref, oa_ref, ob_ref, ga_ref, gb_ref, wo_ref, g2_ref, h2_ref, mix_ref, u2_ref):
        oa_v = oa_ref[...]
        ob_v = ob_ref[...]
        na = (oa_v * _rms_r(oa_v, SWA_Q_W) * ga_ref[...]).astype(BF16)
        nb = (ob_v * _rms_r(ob_v, MLA_OUT_W) * gb_ref[...]).astype(BF16)
        mix_ref[:, :SWA_Q_W] = na
        mix_ref[:, SWA_Q_W:] = nb
        h2 = h_ref[...] + _dot(na, wo_ref[:SWA_Q_W, :]) + _dot(nb, wo_ref[SWA_Q_W:, :])
        h2_ref[...] = h2
        u2_ref[...] = (h2 * _rms_r(h2, D_MODEL) * g2_ref[...]).astype(BF16)

    mix_w = SWA_Q_W + MLA_OUT_W
    return pl.pallas_call(
        body, name="mix_fwd", grid=(t // tm,),
        in_specs=[_row(tm, D_MODEL), _row(tm, SWA_Q_W), _row(tm, MLA_OUT_W), _const(ga.shape), _const(gb.shape),
                  _const(wo.shape), _const(g2.shape)],
        out_specs=[_row(tm, D_MODEL), _row(tm, mix_w), _row(tm, D_MODEL)],
        out_shape=[jax.ShapeDtypeStruct((t, D_MODEL), F32), jax.ShapeDtypeStruct((t, mix_w), BF16),
                   jax.ShapeDtypeStruct((t, D_MODEL), BF16)],
        compiler_params=_params("parallel"),
    )(h, oa, ob, ga, gb, wo, g2)


def _ffn_fwd(h2, u2, wg_t, wu_t, wd):
    t = h2.shape[0]
    tm = _tile(t)
    dff = wd.shape[0]

    def body(h2_ref, u2_ref, wg_ref, wu_ref, wd_ref, h3_ref, g_ref, up_ref):
        u2v = u2_ref[...]
        g = _dot_nt(u2v, wg_ref[...])
        up = _dot_nt(u2v, wu_ref[...])
        g_ref[...] = g.astype(BF16)
        up_ref[...] = up.astype(BF16)
        a = (g * jax.nn.sigmoid(g) * up).astype(BF16)
        h3_ref[...] = h2_ref[...] + _dot(a, wd_ref[...])

    return pl.pallas_call(
        body, name="ffn_fwd", grid=(t // tm,),
        in_specs=[_row(tm, D_MODEL), _row(tm, D_MODEL), _const(wg_t.shape), _const(wu_t.shape), _const(wd.shape)],
        out_specs=[_row(tm, D_MODEL), _row(tm, dff), _row(tm, dff)],
        out_shape=[jax.ShapeDtypeStruct((t, D_MODEL), F32), jax.ShapeDtypeStruct((t, dff), BF16),
                   jax.ShapeDtypeStruct((t, dff), BF16)],
        compiler_params=_params("parallel"),
    )(h2, u2, wg_t, wu_t, wd)


def _loss_bwd(h, gf, target):
    t = h.shape[0]
    tm = _tile(t)
    first_row = FRONT + N_META

    def body(h_ref, gf_ref, t_ref, dh_ref, dgf_ref, loss_ref):
        i = pl.program_id(0)
        hv = h_ref[...]
        y = hv * _rms_r(hv, D_MODEL) * gf_ref[...]
        row = i * tm + lax.broadcasted_iota(jnp.int32, (tm, 1), 0)
        err = jnp.where(row >= first_row, y - t_ref[...], 0.0)
        dx, dg = _rms_bwd(hv, gf_ref[...], err * (1.0 / D_MODEL), D_MODEL)
        dh_ref[...] = dx
        _acc(dgf_ref, dg, i == 0)
        part = 0.5 * jnp.sum(jnp.sum(err * err, axis=1, keepdims=True) * (1.0 / D_MODEL), axis=0, keepdims=True)
        _acc(loss_ref, jnp.broadcast_to(part, (1, LANE)), i == 0)

    return pl.pallas_call(
        body, name="loss_bwd", grid=(t // tm,),
        in_specs=[_row(tm, D_MODEL), _const(gf.shape), _row(tm, D_MODEL)],
        out_specs=[_row(tm, D_MODEL), _const((1, D_MODEL)), _const((1, LANE))],
        out_shape=[jax.ShapeDtypeStruct((t, D_MODEL), F32), jax.ShapeDtypeStruct((1, D_MODEL), F32),
                   jax.ShapeDtypeStruct((1, LANE), F32)],
        compiler_params=_params("arbitrary"),
    )(h, gf, target)


def _tn_matmul(a, b, name, cols=None):
    t, n = b.shape
    first, k = cols or (0, a.shape[1])
    tk = next(c for c in (k, 1024, 512, 256, 128) if k % c == 0 and first % c == 0 and c <= 1024)
    fits = lambda c: 2 * (t * (tk + c) * 2 + tk * c * 2) <= TN_VMEM_BUDGET
    tn = next(c for c in (n, 1024, 512, 256, 128) if n % c == 0 and fits(c))

    def body(a_ref, b_ref, o_ref):
        o_ref[...] = _dot_tn(a_ref[...], b_ref[...]).astype(BF16)

    return pl.pallas_call(
        body, name=name, grid=(k // tk, n // tn),
        in_specs=[pl.BlockSpec((t, tk), lambda i, j: (0, i + first // tk)), pl.BlockSpec((t, tn), lambda i, j: (0, j))],
        out_specs=pl.BlockSpec((tk, tn), lambda i, j: (i, j)),
        out_shape=jax.ShapeDtypeStruct((k, n), BF16),
        compiler_params=_params("parallel", "parallel"),
    )(a, b)


def _ffn_bwd_a(dh3, g, up, wd):
    t = dh3.shape[0]
    tm = _tile(t)
    dff = wd.shape[0]

    def body(dh3_ref, g_ref, up_ref, wd_ref, a_ref, dgu_ref, dh3b_ref):
        dh3b = dh3_ref[...].astype(BF16)
        dh3b_ref[...] = dh3b
        da = _dot_nt(dh3b, wd_ref[...])
        gv = g_ref[...].astype(F32)
        upv = up_ref[...].astype(F32)
        sg = jax.nn.sigmoid(gv)
        silu = gv * sg
        a_ref[...] = (silu * upv).astype(BF16)
        dgu_ref[:, :dff] = (da * upv * (sg * (1.0 + gv * (1.0 - sg)))).astype(BF16)
        dgu_ref[:, dff:] = (da * silu).astype(BF16)

    return pl.pallas_call(
        body, name="ffn_bwd_a", grid=(t // tm,),
        in_specs=[_row(tm, D_MODEL), _row(tm, dff), _row(tm, dff), _const(wd.shape)],
        out_specs=[_row(tm, dff), _row(tm, 2 * dff), _row(tm, D_MODEL)],
        out_shape=[jax.ShapeDtypeStruct((t, dff), BF16), jax.ShapeDtypeStruct((t, 2 * dff), BF16),
                   jax.ShapeDtypeStruct((t, D_MODEL), BF16)],
        compiler_params=_params("parallel"),
    )(dh3, g, up, wd)


def _ffn_bwd_b(dh3, dgu, h2, g2, wg_t, wu_t):
    t = dh3.shape[0]
    tm = _tile(t)
    dff = wg_t.shape[0]

    def body(dh3_ref, dgu_ref, h2_ref, g2_ref, wg_ref, wu_ref, dh2_ref, dh2b_ref, dg2_ref):
        du2 = _dot(dgu_ref[:, :dff], wg_ref[...]) + _dot(dgu_ref[:, dff:], wu_ref[...])
        dx, dg = _rms_bwd(h2_ref[...], g2_ref[...], du2, D_MODEL)
        dh2 = dh3_ref[...] + dx
        dh2_ref[...] = dh2
        dh2b_ref[...] = dh2.astype(BF16)
        _acc(dg2_ref, dg, pl.program_id(0) == 0)

    return pl.pallas_call(
        body, name="ffn_bwd_b", grid=(t // tm,),
        in_specs=[_row(tm, D_MODEL), _row(tm, 2 * dff), _row(tm, D_MODEL), _const(g2.shape), _const(wg_t.shape),
                  _const(wu_t.shape)],
        out_specs=[_row(tm, D_MODEL), _row(tm, D_MODEL), _const((1, D_MODEL))],
        out_shape=[jax.ShapeDtypeStruct((t, D_MODEL), F32), jax.ShapeDtypeStruct((t, D_MODEL), BF16),
                   jax.ShapeDtypeStruct((1, D_MODEL), F32)],
        compiler_params=_params("arbitrary"),
    )(dh3, dgu, h2, g2, wg_t, wu_t)


def _mix_bwd(dh2, oa, ob, ga, gb, wo):
    t = dh2.shape[0]
    tm = _tile(t)

    def body(dh2_ref, oa_ref, ob_ref, ga_ref, gb_ref, wo_ref, doa_ref, dob_ref, dl_ref, dga_ref, dgb_ref):
        first = pl.program_id(0) == 0
        d = dh2_ref[...]
        ob_v = ob_ref[...]
        dxa, dga = _rms_bwd(oa_ref[...], ga_ref[...], _dot_nt(d, wo_ref[:SWA_Q_W, :]), SWA_Q_W)
        dxb, dgb = _rms_bwd(ob_v, gb_ref[...], _dot_nt(d, wo_ref[SWA_Q_W:, :]), MLA_OUT_W)
        lower = lax.broadcasted_iota(jnp.int32, (tm, LANE), 1) < HALF
        for hd in range(MLA_HEADS):
            sl = slice(LANE * (hd // 2), LANE * (hd // 2 + 1))
            mine = lower if hd % 2 == 0 else jnp.logical_not(lower)
            delta = jnp.sum(jnp.where(mine, ob_v[:, sl] * dxb[:, sl], 0.0), axis=1, keepdims=True)
            dl_ref[hd // MLA_HB, :, hd % MLA_HB:hd % MLA_HB + 1] = delta
        for ref, dx, heads in ((doa_ref, dxa, SWA_HEADS), (dob_ref, dxb, MLA_HEADS)):
            for hd in range(heads):
                slab = dx[:, LANE * (hd // 2):LANE * (hd // 2 + 1)]
                ref[:, LANE * hd:LANE * (hd + 1)] = _unpack_pair(slab, hd % 2).astype(BF16)
        _acc(dga_ref, dga, first)
        _acc(dgb_ref, dgb, first)

    return pl.pallas_call(
        body, name="mix_bwd", grid=(t // tm,),
        in_specs=[_row(tm, D_MODEL), _row(tm, SWA_Q_W), _row(tm, MLA_OUT_W), _const(ga.shape), _const(gb.shape),
                  _const(wo.shape)],
        out_specs=[_row(tm, HP), _row(tm, HP), pl.BlockSpec((MLA_HEADS // MLA_HB, tm, MLA_HB), lambda i: (0, i, 0)),
                   _const((1, SWA_Q_W)), _const((1, MLA_OUT_W))],
        out_shape=[jax.ShapeDtypeStruct((t, HP), BF16), jax.ShapeDtypeStruct((t, HP), BF16), _head_stats(t),
                   jax.ShapeDtypeStruct((1, SWA_Q_W), F32), jax.ShapeDtypeStruct((1, MLA_OUT_W), F32)],
        compiler_params=_params("arbitrary"),
    )(dh2, oa, ob, ga, gb, wo)


def _swa_bwd(sinks, q, k, v, o, do):
    t = q.shape[0]
    ts = _tile(t)

    def body(sink_ref, q_ref, kp_ref, kc_ref, vp_ref, vc_ref, o_ref, do_ref,
             dq_ref, dkc_ref, dkp_ref, dvc_ref, dvp_ref, dsink_ref):
        n = pl.program_id(0)
        chains = _swa_chains(t)
        qs, ks, probs = _swa_scores(sink_ref, q_ref, kp_ref, kc_ref, n, t)
        dos = [_swa_group(do_ref, slice(BLOCK * rb, BLOCK * (rb + 1)), j) for rb, j in chains]
        vs = [_swa_keys(vp_ref, vc_ref, rb, j) for rb, j in chains]
        dps = [_dot_nt(v2, do4) for do4, v2 in zip(dos, vs)]
        dss, dsks = [], []
        for (rb, j), (p, psink), do4, dp in zip(chains, probs, dos, dps):
            o4 = _swa_packed_group(o_ref, slice(BLOCK * rb, BLOCK * (rb + 1)), j)
            delta = jnp.sum(o4 * do4.astype(F32), axis=1, keepdims=True)
            delta = jnp.broadcast_to(delta, (SWA_GROUP * BLOCK, LANE)).T[:1, :]
            dss.append((p * (dp - delta) * SCALE_A).astype(BF16))
            dsks.append(-psink * delta)
        dqs = [_dot_tn(ds, k2) for ds, k2 in zip(dss, ks)]
        dks = [_dot(ds, q4) for ds, q4 in zip(dss, qs)]
        dvs = [_dot(p.astype(BF16), do4) for (p, _), do4 in zip(probs, dos)]
        dsink = [jnp.zeros((1, LANE), F32)] * SWA_HEADS
        ext = {}
        for (rb, j), dq4, dk2, dv2, dsk in zip(chains, dqs, dks, dvs, dsks):
            for g in range(SWA_GROUP):
                hd = SWA_GROUP * j + g
                rows = slice(BLOCK * g, BLOCK * (g + 1))
                dq_ref[BLOCK * rb:BLOCK * (rb + 1), LANE * hd:LANE * (hd + 1)] = dq4[rows].astype(BF16)
                dsink[hd] = dsink[hd] + jnp.sum(dsk[:, rows], axis=1, keepdims=True)
            for half in range(2):
                key = (j, rb + half)
                part = (dk2[BLOCK * half:BLOCK * (half + 1)], dv2[BLOCK * half:BLOCK * (half + 1)])
                ext[key] = part if key not in ext else (ext[key][0] + part[0], ext[key][1] + part[1])
        for (j, blk), (dk, dv) in ext.items():
            sl = slice(LANE * j, LANE * (j + 1))
            if blk == 0:
                dkp_ref[:, sl] = dk
                dvp_ref[:, sl] = dv
            else:
                dkc_ref[BLOCK * (blk - 1):BLOCK * blk, sl] = dk
                dvc_ref[BLOCK * (blk - 1):BLOCK * blk, sl] = dv
        for hd in range(SWA_HEADS):
            _acc(dsink_ref.at[hd:hd + 1, :], jnp.broadcast_to(dsink[hd], (1, LANE)), n == 0)

    cur = lambda n: (n, 0)
    kv = pl.BlockSpec((ts, 2 * LANE), cur)
    kvp = pl.BlockSpec((BLOCK, 2 * LANE), cur)
    hp = pl.BlockSpec((ts, HP), cur)
    kvs = jax.ShapeDtypeStruct((t, 2 * LANE), F32)
    kvps = jax.ShapeDtypeStruct((t // ts * BLOCK, 2 * LANE), F32)
    return pl.pallas_call(
        body, name="swa_bwd", grid=(t // ts,),
        in_specs=_swa_specs(t) + [pl.BlockSpec((ts, SWA_Q_W), cur), hp],
        out_specs=[hp, kv, kvp, kv, kvp, _const((SWA_HEADS, LANE))],
        out_shape=[jax.ShapeDtypeStruct((t, HP), BF16), kvs, kvps, kvs, kvps,
                   jax.ShapeDtypeStruct((SWA_HEADS, LANE), F32)],
        compiler_params=_params("arbitrary"),
    )(sinks, q, k, k, v, v, o, do)


def _mla_bwd(q, k, v, do, lse, dl, slabs=()):
    t = q.shape[0]
    tq = _tile(t)
    nq = t // tq
    n = len(slabs)
    hb = MLA_HB
    steps = (MLA_HEADS // hb) * nq

    def body(k_ref, v_ref, q_ref, do_ref, lse_ref, dl_ref, *rest):
        in_refs, (dq_ref, dk_ref, dv_ref), out_refs = rest[:n], rest[n:n + 3], rest[n + 3:2 * n + 3]
        (dq_sc, dk_sc, dv_sc), sems = rest[2 * n + 3:2 * n + 6], rest[2 * n + 6:]
        j = pl.program_id(1)
        step_id = pl.program_id(0) * nq + j
        if n:
            plan = _exchange_plan(in_refs, out_refs, *sems)
            pl.when(step_id == 0)(plan.start)

        @pl.when(j == 0)
        def _():
            dq_sc[...] = jnp.zeros(dq_sc.shape, F32)

        dk_sc[...] = jnp.zeros(dk_sc.shape, F32)
        dv_sc[...] = jnp.zeros(dv_sc.shape, F32)
        ks, vs = _heads(k_ref, hb), _heads(v_ref, hb)

        def step(i, carry, masked):
            rows = pl.ds(pl.multiple_of(i * tq, tq), tq)
            qs, dos = _heads(q_ref, hb, rows), _heads(do_ref, hb, rows)
            ss = [_dot_nt(qh, kh) for qh, kh in zip(qs, ks)]
            dps = [_dot_nt(doh, vh) for doh, vh in zip(dos, vs)]
            if masked:
                mask = _causal_mask(i * tq, j * tq, tq, tq, False)
                ss = [jnp.where(mask, s_, NEG) for s_ in ss]
            ps = [jnp.exp2(s_ - lse_ref[rows, a:a + 1]) for a, s_ in enumerate(ss)]
            dss = [(p * (dp - dl_ref[rows, a:a + 1])).astype(BF16) for a, (p, dp) in enumerate(zip(ps, dps))]
            for a, (ds, p, qh, kh, doh) in enumerate(zip(dss, ps, qs, ks, dos)):
                dq_sc[a, rows, :] += _dot(ds, kh)
                dk_sc[a] += _dot_tn(ds, qh)
                dv_sc[a] += _dot_tn(p.astype(BF16), doh)
            return carry

        split = jnp.where(j == 0, nq, j + 1)
        lax.fori_loop(j, split, lambda i, c: step(i, c, True), 0)
        lax.fori_loop(split, nq, lambda i, c: step(i, c, False), 0)
        for a in range(hb):
            dk_ref[:, LANE * a:LANE * (a + 1)] = (dk_sc[a] * (1.0 / LOG2E)).astype(BF16)
            dv_ref[:, LANE * a:LANE * (a + 1)] = dv_sc[a].astype(BF16)

        @pl.when(j == nq - 1)
        def _():
            for a in range(hb):
                dq_ref[:, LANE * a:LANE * (a + 1)] = (dq_sc[a] * SCALE_B).astype(BF16)

        if n:
            pl.when(step_id == steps - 1)(plan.finish)

    blk = pl.BlockSpec((tq, hb * LANE), lambda h, j: (j, h))
    full = pl.BlockSpec((t, hb * LANE), lambda h, j: (0, h))
    cols = pl.BlockSpec((None, t, hb), lambda h, j: (h, 0, 0))
    out = pl.pallas_call(
        body, name="mla_bwd_exchange" if n else "mla_bwd", grid=(MLA_HEADS // hb, nq),
        in_specs=[blk, blk, full, full, cols, cols] + [ANY] * n, out_specs=[full, blk, blk] + [ANY] * n,
        out_shape=[jax.ShapeDtypeStruct((t, HP), BF16)] * 3 + [jax.ShapeDtypeStruct(a.shape, a.dtype) for a in slabs],
        scratch_shapes=[pltpu.VMEM((hb, t, LANE), F32)] + [pltpu.VMEM((hb, tq, LANE), F32)] * 2
        + (_comm_sems(n) if n else []),
        compiler_params=_params("arbitrary", "arbitrary"),
    )(k, v, q, do, lse, dl, *slabs)
    return out[:3], out[3:]


def _pre_bwd(dh2, h, cq, ckv, dqa, dka, dka_next, dva, dva_next, dqb, dkf, dvb, g1, win, gq, wqu, gkv, wkv, tabs):
    t = h.shape[0]
    tm = _tile(t)

    def body(dh2_ref, h_ref, cq_ref, ckv_ref, dqa_ref, dka_ref, dkan_ref, dva_ref, dvan_ref, dqb_ref, dkf_ref, dvb_ref,
             g1_ref, win_ref, gq_ref, wqu_ref, gkv_ref, wkv_ref, tab_ref,
             dh_ref, dp_ref, dqbo_ref, dkvo_ref, dg1_ref, dgq_ref, dgkv_ref):
        first = pl.program_id(0) == 0
        ca, sa1, sa2, cb, sb1, sb2, ck = _tabs(tab_ref)
        dkr = jnp.zeros((tm, LANE), F32)
        for c in range(MLA_HEADS):
            sl = slice(LANE * c, LANE * (c + 1))
            dqbo_ref[:, sl] = _rope_t(dqb_ref[:, sl].astype(F32), cb, sb1, sb2, 16).astype(BF16)
            dkr += dkf_ref[:, sl].astype(F32)
        dkvo_ref[:, :HP] = dkf_ref[...]
        dkvo_ref[:, HP:] = dvb_ref[...]
        dcq, dgq = _rms_bwd(cq_ref[...], gq_ref[...], _dot(dqbo_ref[...], wqu_ref[...]), MLA_Q_RANK)
        dckv, dgkv = _rms_bwd(ckv_ref[...], gkv_ref[...], _dot(dkvo_ref[...], wkv_ref[...]), MLA_KV_RANK)
        for c in range(SWA_HEADS):
            sl = slice(LANE * c, LANE * (c + 1))
            dp_ref[:, PO_QA + LANE * c:PO_QA + LANE * (c + 1)] = _rope_t(dqa_ref[:, sl].astype(F32), ca, sa1, sa2,
                                                                          32).astype(BF16)
        last = slice(tm - BLOCK, tm)
        more = pl.program_id(0) < t // tm - 1
        for c in range(SWA_KV_HEADS):
            sl = slice(LANE * c, LANE * (c + 1))
            dk = dka_ref[:, sl]
            dk_last = dk[tm - BLOCK:] + jnp.where(more, dkan_ref[:, sl], 0.0)
            cols = slice(PO_KA + LANE * c, PO_KA + LANE * (c + 1))
            if tm > BLOCK:
                dp_ref[:tm - BLOCK, cols] = _rope_t(dk[:tm - BLOCK], ca[:tm - BLOCK], sa1[:tm - BLOCK], sa2[:tm - BLOCK],
                                                    32).astype(BF16)
            dp_ref[last, cols] = _rope_t(dk_last, ca[tm - BLOCK:], sa1[tm - BLOCK:], sa2[tm - BLOCK:], 32).astype(BF16)
        if tm > BLOCK:
            dp_ref[:tm - BLOCK, PO_VA:PO_CQ] = dva_ref[:tm - BLOCK, :].astype(BF16)
        dp_ref[last, PO_VA:PO_CQ] = (dva_ref[tm - BLOCK:, :] + jnp.where(more, dvan_ref[...], 0.0)).astype(BF16)
        dp_ref[:, PO_CQ:PO_CKV] = dcq.astype(BF16)
        dp_ref[:, PO_CKV:PO_KR] = dckv.astype(BF16)
        dp_ref[:, PO_KR:PW_IN] = _rope_t(dkr, ck, sb1, sb2, 16).astype(BF16)
        dx, dg1 = _rms_bwd(h_ref[...], g1_ref[...], _dot(dp_ref[...], win_ref[...]), D_MODEL)
        dh_ref[...] = dh2_ref[...] + dx
        _acc(dg1_ref, dg1, first)
        _acc(dgq_ref, dgq, first)
        _acc(dgkv_ref, dgkv, first)

    kv = _row(tm, 2 * LANE)
    nxt = pl.BlockSpec((BLOCK, 2 * LANE), lambda i: (jnp.minimum(i + 1, t // tm - 1), 0))
    return pl.pallas_call(
        body, name="pre_bwd", grid=(t // tm,),
        in_specs=[_row(tm, D_MODEL), _row(tm, D_MODEL), _row(tm, MLA_Q_RANK), _row(tm, MLA_KV_RANK), _row(tm, HP),
                  kv, nxt, kv, nxt, _row(tm, HP), _row(tm, HP), _row(tm, HP),
                  _const(g1.shape), _const(win.shape), _const(gq.shape), _const(wqu.shape), _const(gkv.shape),
                  _const(wkv.shape), _row(tm, N_TAB * LANE)],
        out_specs=[_row(tm, D_MODEL), _row(tm, PW_IN), _row(tm, HP), _row(tm, 2 * HP),
                   _const((1, D_MODEL)), _const((1, MLA_Q_RANK)), _const((1, MLA_KV_RANK))],
        out_shape=[jax.ShapeDtypeStruct((t, D_MODEL), F32), jax.ShapeDtypeStruct((t, PW_IN), BF16),
                   jax.ShapeDtypeStruct((t, HP), BF16), jax.ShapeDtypeStruct((t, 2 * HP), BF16),
                   jax.ShapeDtypeStruct((1, D_MODEL), F32), jax.ShapeDtypeStruct((1, MLA_Q_RANK), F32),
                   jax.ShapeDtypeStruct((1, MLA_KV_RANK), F32)],
        compiler_params=_params("arbitrary"),
    )(dh2, h, cq, ckv, dqa, dka, dka_next, dva, dva_next, dqb, dkf, dvb, g1, win, gq, wqu, gkv, wkv, tabs)


def _rope_tables(t):
    pos = (jnp.arange(t, dtype=jnp.int32) - FRONT).astype(F32)[:, None]
    lane = jnp.arange(LANE)[None, :]

    def table(dim, start):
        half = dim // 2
        inv = ROPE_THETA ** (-jnp.arange(0, dim, 2, dtype=F32) / dim)
        ang = pos * inv[None, :]
        cos = jnp.concatenate([jnp.cos(ang)] * 2, axis=1)
        sin = jnp.concatenate([jnp.sin(ang)] * 2, axis=1)
        pad = lambda a: jnp.pad(a, ((0, 0), (start, LANE - start - dim)))
        first = (lane >= start) & (lane < start + half)
        second = (lane >= start + half) & (lane < start + dim)
        return pad(cos), jnp.where(first, -pad(sin), 0.0), jnp.where(second, pad(sin), 0.0)

    ca, sa1, sa2 = table(SWA_HEAD_DIM, 0)
    ck, sb1, sb2 = table(MLA_ROPE_DIM, MLA_NOPE_DIM)
    cb = jnp.where(lane < MLA_NOPE_DIM, 1.0, ck)
    return jnp.concatenate([ca, sa1, sa2, cb, sb1, sb2, ck], axis=1)


def _pad_heads(w, heads, dim, axis):
    shp = w.shape
    w = w.reshape(shp[:axis] + (heads, dim) + shp[axis + 1:])
    pad = [(0, 0)] * w.ndim
    pad[axis + 1] = (0, LANE - dim)
    return jnp.pad(w, pad).reshape(shp[:axis] + (heads * LANE,) + shp[axis + 1:])


def _unpad_heads(w, heads, dim, axis):
    shp = w.shape
    w = w.reshape(shp[:axis] + (heads, LANE) + shp[axis + 1:])
    w = lax.slice_in_dim(w, 0, dim, axis=axis + 1)
    return w.reshape(shp[:axis] + (heads * dim,) + shp[axis + 1:])


def _pad_layer(w_in, w_q_up, w_kv_up):
    o1 = SWA_Q_W
    o2 = o1 + SWA_KV_W
    o3 = o2 + SWA_KV_W
    o4 = o3 + MLA_Q_RANK
    o5 = o4 + MLA_KV_RANK
    kr = jnp.pad(w_in[o5:], ((MLA_NOPE_DIM, LANE - MLA_QK_DIM), (0, 0)))
    win = jnp.concatenate([
        _pad_heads(w_in[:o1], SWA_HEADS, SWA_HEAD_DIM, 0),
        _pad_heads(w_in[o1:o2], SWA_KV_HEADS, SWA_HEAD_DIM, 0),
        _pad_heads(w_in[o2:o3], SWA_KV_HEADS, SWA_HEAD_DIM, 0),
        w_in[o3:o5], kr], axis=0)
    wqu = _pad_heads(w_q_up, MLA_HEADS, MLA_QK_DIM, 0)
    kv = w_kv_up.reshape(MLA_HEADS, MLA_NOPE_DIM + MLA_V_DIM, MLA_KV_RANK)
    wkv = jnp.concatenate([
        _pad_heads(kv[:, :MLA_NOPE_DIM].reshape(-1, MLA_KV_RANK), MLA_HEADS, MLA_NOPE_DIM, 0),
        _pad_heads(kv[:, MLA_NOPE_DIM:].reshape(-1, MLA_KV_RANK), MLA_HEADS, MLA_V_DIM, 0)], axis=0)
    return win, wqu, wkv


def _unpad_layer(dwin, dwqu, dwkv):
    d_w_in = jnp.concatenate([
        _unpad_heads(dwin[PO_QA:PO_KA], SWA_HEADS, SWA_HEAD_DIM, 0),
        _unpad_heads(dwin[PO_KA:PO_VA], SWA_KV_HEADS, SWA_HEAD_DIM, 0),
        _unpad_heads(dwin[PO_VA:PO_CQ], SWA_KV_HEADS, SWA_HEAD_DIM, 0),
        dwin[PO_CQ:PO_KR], dwin[PO_KR + MLA_NOPE_DIM:PO_KR + MLA_QK_DIM]], axis=0)
    d_w_q_up = _unpad_heads(dwqu, MLA_HEADS, MLA_QK_DIM, 0)
    dk = _unpad_heads(dwkv[:HP], MLA_HEADS, MLA_NOPE_DIM, 0).reshape(MLA_HEADS, MLA_NOPE_DIM, MLA_KV_RANK)
    dv = _unpad_heads(dwkv[HP:], MLA_HEADS, MLA_V_DIM, 0).reshape(MLA_HEADS, MLA_V_DIM, MLA_KV_RANK)
    d_w_kv_up = jnp.concatenate([dk, dv], axis=1).reshape(-1, MLA_KV_RANK)
    return d_w_in, d_w_q_up, d_w_kv_up


def _train_example(x, target, meta, vec, weights):
    s = x.shape[0]
    depth = vec["attn_norm"].shape[0]
    t = FRONT + N_META + s
    assert t % BLOCK == 0
    tabs = _rope_tables(t)
    h = jnp.concatenate([jnp.zeros((FRONT, D_MODEL), F32), meta, x], axis=0)
    tgt = jnp.concatenate([jnp.zeros((FRONT + N_META, D_MODEL), F32), target], axis=0)
    row = lambda v: v[None, :]

    saved = []
    for l in range(depth):
        win, wqu, wkv = _pad_layer(*weights.attn_in(l))
        g1, gq, gkv, g2, ga, gb = (row(vec[n][l]) for n in ("attn_norm", "q_norm", "kv_norm", "ffn_norm",
                                                            "out_norm_swa", "out_norm_mla"))
        sk = row(vec["sinks"][l])
        u, qa, ka, va, cq, ckv, qn, kvn, qb, kf, vb = _pre_fwd(h, g1, win, gq, wqu, gkv, wkv, tabs)
        oa = _swa_fwd(sk, qa, ka, va)
        ob, lse = weights.mla_fwd(l, qb, kf, vb)
        lse = jnp.moveaxis(lse.reshape(t, MLA_HEADS // MLA_HB, MLA_HB), 1, 0)
        wo = weights.w_o(l)
        h2, mix, u2 = _mix_fwd(h, oa, ob, ga, gb, wo, g2)
        wg, wu, wd = weights.ffn(l)
        h3, gt, up = _ffn_fwd(h2, u2, wg, wu, wd)
        saved.append((h, u, qa, ka, va, cq, ckv, qn, kvn, qb, kf, vb, oa, ob, lse, h2, mix, u2, gt, up,
                      win, wqu, wkv, wo, ga, gb, g1, gq, gkv, g2, sk, wg, wu, wd))
        h = h3

    dh, d_final, loss = _loss_bwd(h, row(vec["final_norm"]), tgt)

    grads = []
    for l in reversed(range(depth)):
        (h0, u, qa, ka, va, cq, ckv, qn, kvn, qb, kf, vb, oa, ob, lse, h2, mix, u2, gt, up,
         win, wqu, wkv, wo, ga, gb, g1, gq, gkv, g2, sk, wg, wu, wd) = saved[l]
        dff = wd.shape[0]
        act, dgu, dhb = _ffn_bwd_a(dh, gt, up, wd)
        weights.ffn_grads(l, _tn_matmul(dgu, u2, "dw_gate", (0, dff)), _tn_matmul(dgu, u2, "dw_up", (dff, dff)),
                          _tn_matmul(act, dhb, "dw_down"))
        dh2, dh2b, d_g2 = _ffn_bwd_b(dh, dgu, h2, g2, wg, wu)
        weights.attn_grads(l, w_o=_tn_matmul(mix, dh2b, "dw_o"))
        doa, dob, dl, d_ga, d_gb = _mix_bwd(dh2b, oa, ob, ga, gb, wo)
        dqa, dkc, dkp, dvc, dvp, dsink = _swa_bwd(sk, qa, ka, va, oa, doa)
        dqb, dkf, dvb = weights.mla_bwd(l, qb, kf, vb, dob, lse, dl)
        dh, dp, dqbo, dkvo, d_g1, d_gq, d_gkv = _pre_bwd(
            dh2, h0, cq, ckv, dqa, dkc, dkp, dvc, dvp, dqb, dkf, dvb,
            g1, win, gq, wqu, gkv, wkv, tabs)
        d_win = _tn_matmul(dp, u, "dw_in")
        d_wqu = _tn_matmul(dqbo, qn, "dw_q_up")
        d_wkv = _tn_matmul(dkvo, kvn, "dw_kv_up")
        weights.attn_grads(l, **dict(zip(ATTN_IN, _unpad_layer(d_win, d_wqu, d_wkv))))
        grads.append(dict(attn_norm=d_g1[0], q_norm=d_gq[0], kv_norm=d_gkv[0], sinks=dsink[:, 0], out_norm_swa=d_ga[0],
                          out_norm_mla=d_gb[0], ffn_norm=d_g2[0]))
    grads = grads[::-1]
    stacked = {k: jnp.stack([g[k] for g in grads]) for k in grads[0]}
    stacked["final_norm"] = d_final[0]
    return loss[0, 0], dh[FRONT + N_META:], dh[FRONT:FRONT + N_META], stacked


MESH = pl.DeviceIdType.MESH
ANY = pl.BlockSpec(memory_space=pl.ANY)


def _place():
    return lax.axis_index("x"), lax.axis_index("y"), lax.axis_index("c")


def _index(x, y, c):
    return 4 * x + 2 * y + c


def _comm_sems(n):
    return [pltpu.SemaphoreType.DMA((n, N_DEV - 1)), pltpu.SemaphoreType.DMA((n, N_DEV - 1)),
            pltpu.SemaphoreType.DMA((n,))]


class _gather_plan:
    def __init__(self, x_refs, out_refs, send_sems, recv_sems, local_sems):
        self.x_refs, self.out_refs = x_refs, out_refs
        self.send_sems, self.recv_sems, self.local_sems = send_sems, recv_sems, local_sems
        self.n = len(x_refs)

    def _where(self):
        x, y, c = _place()
        return (x, y, c), (x, y, 1 - c), [(1 - x, y), (x, 1 - y), (1 - x, 1 - y)], c

    def _copy(self, i, k, block, to, from_input=False):
        slot = self.out_refs[i].at[_index(*block)]
        return pltpu.make_async_remote_copy(
            src_ref=self.x_refs[i] if from_input else slot, dst_ref=slot,
            send_sem=self.send_sems.at[i, k], recv_sem=self.recv_sems.at[i, k], device_id=to, device_id_type=MESH)

    def _mine(self, i, me):
        return pltpu.make_async_copy(self.x_refs[i], self.out_refs[i].at[_index(*me)], self.local_sems.at[i])

    def _first(self, me, sibling, chips, c):
        out = [self._copy(i, 1 + j, me, (*chip, c), True) for j, chip in enumerate(chips) for i in range(self.n)]
        return out + [self._copy(i, 0, me, sibling, True) for i in range(self.n)]

    def start(self):
        me, sibling, chips, c = self._where()
        for i in range(self.n):
            self._mine(i, me).start()
        for cp in self._first(me, sibling, chips, c):
            cp.start()

    def forward(self):
        me, sibling, chips, c = self._where()
        for j, chip in enumerate(chips):
            for i in range(self.n):
                self._copy(i, 1 + j, (*chip, c), me).wait_recv()
                self._copy(i, 4 + j, (*chip, c), sibling).start()

    def finish(self):
        me, sibling, chips, c = self._where()
        for i in range(self.n):
            self._copy(i, 0, sibling, me).wait_recv()
            for j, chip in enumerate(chips):
                self._copy(i, 4 + j, (*chip, 1 - c), me).wait_recv()
        for cp in self._first(me, sibling, chips, c):
            cp.wait_send()
        for j, chip in enumerate(chips):
            for i in range(self.n):
                self._copy(i, 4 + j, (*chip, c), sibling).wait_send()
        for i in range(self.n):
            self._mine(i, me).wait()


class _exchange_plan:
    def __init__(self, in_refs, out_refs, send_sems, recv_sems, local_sems):
        self.in_refs, self.out_refs = in_refs, out_refs
        self.send_sems, self.recv_sems, self.local_sems = send_sems, recv_sems, local_sems
        self.n = len(in_refs)

    def _copies(self):
        x, y, c = _place()
        me = _index(x, y, c)
        mine = [pltpu.make_async_copy(self.in_refs[i].at[me], self.out_refs[i].at[me], self.local_sems.at[i])
                for i in range(self.n)]
        remote = []
        for k in range(1, N_DEV):
            peer = (1 - x if k & 4 else x, 1 - y if k & 2 else y, 1 - c if k & 1 else c)
            remote += [pltpu.make_async_remote_copy(
                src_ref=self.in_refs[i].at[_index(*peer)], dst_ref=self.out_refs[i].at[me],
                send_sem=self.send_sems.at[i, k - 1], recv_sem=self.recv_sems.at[i, k - 1],
                device_id=peer, device_id_type=MESH) for i in range(self.n)]
        return mine, remote

    def start(self):
        mine, remote = self._copies()
        for cp in mine + remote:
            cp.start()

    def finish(self):
        mine, remote = self._copies()
        for cp in remote:
            cp.wait_recv()
        for cp in remote:
            cp.wait_send()
        for cp in mine:
            cp.wait()


def _all_gather(shards, name):
    n = len(shards)

    def body(*refs):
        plan = _gather_plan(refs[:n], refs[n:2 * n], *refs[2 * n:])
        plan.start()
        plan.forward()
        plan.finish()

    return pl.pallas_call(
        body, name=name, in_specs=[ANY] * n, out_specs=[ANY] * n, scratch_shapes=_comm_sems(n),
        out_shape=[jax.ShapeDtypeStruct((N_DEV,) + a.shape, a.dtype) for a in shards],
    )(*shards)


def _exchange(slabs, name):
    n = len(slabs)

    def body(*refs):
        plan = _exchange_plan(refs[:n], refs[n:2 * n], *refs[2 * n:])
        plan.start()
        plan.finish()

    return pl.pallas_call(
        body, name=name, in_specs=[ANY] * n, out_specs=[ANY] * n, scratch_shapes=_comm_sems(n),
        out_shape=[jax.ShapeDtypeStruct(a.shape, a.dtype) for a in slabs],
    )(*slabs)


def _adamw(w, g, m, v):
    m = ADAM_B1 * m + (1.0 - ADAM_B1) * g
    v = ADAM_B2 * v + (1.0 - ADAM_B2) * (g * g)
    m_hat = m / (1.0 - ADAM_B1 ** ADAM_STEP)
    v_hat = v / (1.0 - ADAM_B2 ** ADAM_STEP)
    return -ADAM_LR * (m_hat / (jnp.sqrt(v_hat) + ADAM_EPS) + ADAM_WD * w), m, v


def _sum_slots(ref):
    g = ref[0].astype(F32)
    for s in range(1, N_DEV):
        g = g + ref[s].astype(F32)
    return g


def _reduce_adamw(parts, w, m, v, layer, outs, name):
    l, r, c = w.shape
    tile = next(t for t in (256, 128, r) if r % t == 0)

    def body(p_ref, w_ref, m_ref, v_ref, g0, d0, m0, v0, g_ref, d_ref, nm_ref, nv_ref):
        g = _sum_slots(p_ref)
        g_ref[...] = g
        d_ref[...], nm_ref[...], nv_ref[...] = _adamw(w_ref[...], g, m_ref[...], v_ref[...])

    blk = pl.BlockSpec((None, tile, c), lambda j: (layer, j, 0))
    return pl.pallas_call(
        body, name=name, grid=(r // tile,),
        in_specs=[pl.BlockSpec((N_DEV, tile, c), lambda j: (0, j, 0)), blk, blk, blk] + [ANY] * 4, out_specs=[blk] * 4,
        out_shape=[jax.ShapeDtypeStruct((l, r, c), F32)] * 4,
        input_output_aliases={4: 0, 5: 1, 6: 2, 7: 3},
        compiler_params=_params("parallel"),
    )(parts, w, m, v, *outs)


def _sum_parts(parts, name):
    _, r, c = parts.shape

    def body(p_ref, g_ref):
        g_ref[...] = _sum_slots(p_ref)

    return pl.pallas_call(body, name=name, out_shape=jax.ShapeDtypeStruct((r, c), F32))(parts)


def _adamw_call(w, g, m, v, name):
    def body(w_ref, g_ref, m_ref, v_ref, d_ref, nm_ref, nv_ref):
        d_ref[...], nm_ref[...], nv_ref[...] = _adamw(w_ref[...], g_ref[...], m_ref[...], v_ref[...])

    return pl.pallas_call(body, name=name, out_shape=[jax.ShapeDtypeStruct(w.shape, F32)] * 3)(w, g, m, v)


ATTN_IN = ("w_in", "w_q_up", "w_kv_up")
ATTN = ATTN_IN + ("w_o",)
FFN = ("w_gate", "w_up", "w_down")
TRANSPOSED = ("w_in", "w_q_up", "w_kv_up", "w_gate", "w_up")
SMALL = ("attn_norm", "ffn_norm", "final_norm", "out_norm_swa", "out_norm_mla", "q_norm", "kv_norm", "sinks")
PACK_W = 1024
SMALL_ROWS = 16


def _pack(arrs, dtype):
    flat = jnp.concatenate([a.astype(dtype).reshape(-1) for a in arrs])
    return flat.reshape(-1, PACK_W)


def _unpack(packed, like):
    flat = packed.reshape(-1)
    out, off = [], 0
    for a in like:
        out.append(flat[off:off + a.size].reshape(a.shape))
        off += a.size
    return out


def _gather_to_full(gathered):
    return gathered.reshape((-1,) + gathered.shape[2:])


def _full_to_slabs(full):
    return full.reshape((N_DEV, -1) + full.shape[1:])


class _ShardedWeights:
    def __init__(self, shards, depth):
        self.shards, self.depth = shards, depth
        self.gathered, self.pending, self.parts = {}, {}, {}
        self._gather([(n, 0) for n in ATTN_IN], lambda xs: _all_gather(xs, "gather_attn0"))

    def _gather(self, keys, run):
        self.gathered.update(zip(keys, run([self.shards[n][l] for n, l in keys])))

    def _full(self, names, l):
        return tuple(_gather_to_full(self.gathered[n, l]) for n in names)

    def attn_in(self, l):
        return self._full(ATTN_IN, l)

    def w_o(self, l):
        return self._full(("w_o",), l)[0]

    def ffn(self, l):
        return self._full(FFN, l)

    def mla_fwd(self, l, q, k, v):
        keys = [(n, l) for n in ("w_o",) + FFN] + ([(n, l + 1) for n in ATTN_IN] if l + 1 < self.depth else [])
        out = []
        self._gather(keys, lambda xs: out.extend(_mla_fwd(q, k, v, xs)) or out[2])
        return out[0], out[1]

    def _add(self, names, l, grads):
        for n, g in zip(names, grads):
            self.pending[n, l] = _full_to_slabs(g)

    def ffn_grads(self, l, *grads):
        self._add(FFN, l, grads)

    def attn_grads(self, l, **grads):
        self._add(list(grads), l, grads.values())

    def _exchange(self, run):
        keys = list(self.pending)
        self.parts.update(zip(keys, run([self.pending.pop(k) for k in keys])))

    def mla_bwd(self, l, *args):
        out = []
        self._exchange(lambda xs: out.extend(_mla_bwd(*args, xs)) or out[1])
        return out[0]

    def flush(self):
        self._exchange(lambda xs: _exchange(xs, "exchange_attn0"))


def kernel(x, meta_tokens, attn_norm, w_in, q_norm, w_q_up, kv_norm, w_kv_up, sinks, out_norm_swa, out_norm_mla, w_o, ffn_norm, w_gate, w_up, w_down, final_norm, loss_target, m_meta_tokens, m_attn_norm, m_w_in, m_q_norm, m_w_q_up, m_kv_norm, m_w_kv_up, m_sinks, m_out_norm_swa, m_out_norm_mla, m_w_o, m_ffn_norm, m_w_gate, m_w_up, m_w_down, m_final_norm, v_meta_tokens, v_attn_norm, v_w_in, v_q_norm, v_w_q_up, v_kv_norm, v_w_kv_up, v_sinks, v_out_norm_swa, v_out_norm_mla, v_w_o, v_ffn_norm, v_w_gate, v_w_up, v_w_down, v_final_norm):
    w = dict(meta_tokens=meta_tokens, attn_norm=attn_norm, w_in=w_in, q_norm=q_norm, w_q_up=w_q_up, kv_norm=kv_norm,
             w_kv_up=w_kv_up, sinks=sinks, out_norm_swa=out_norm_swa, out_norm_mla=out_norm_mla, w_o=w_o,
             ffn_norm=ffn_norm, w_gate=w_gate, w_up=w_up, w_down=w_down, final_norm=final_norm)
    m = dict(meta_tokens=m_meta_tokens, attn_norm=m_attn_norm, w_in=m_w_in, q_norm=m_q_norm, w_q_up=m_w_q_up,
             kv_norm=m_kv_norm, w_kv_up=m_w_kv_up, sinks=m_sinks, out_norm_swa=m_out_norm_swa,
             out_norm_mla=m_out_norm_mla, w_o=m_w_o, ffn_norm=m_ffn_norm, w_gate=m_w_gate, w_up=m_w_up,
             w_down=m_w_down, final_norm=m_final_norm)
    v = dict(meta_tokens=v_meta_tokens, attn_norm=v_attn_norm, w_in=v_w_in, q_norm=v_q_norm, w_q_up=v_w_q_up,
             kv_norm=v_kv_norm, w_kv_up=v_w_kv_up, sinks=v_sinks, out_norm_swa=v_out_norm_swa,
             out_norm_mla=v_out_norm_mla, w_o=v_w_o, ffn_norm=v_ffn_norm, w_gate=v_w_gate, w_up=v_w_up,
             w_down=v_w_down, final_norm=v_final_norm)
    names = list(w)
    big = ATTN + FFN
    depth = w_in.shape[0]
    me = _index(*_place())

    as_held = lambda n, a: jnp.swapaxes(a, 1, 2) if n in TRANSPOSED else a
    weights = _ShardedWeights({n: as_held(n, w[n]).astype(BF16) for n in big}, depth)
    meta = jnp.moveaxis(_all_gather([meta_tokens], "gather_meta")[0], 0, 1).reshape(N_META, D_MODEL)
    loss, grad_x, d_meta, grads = _train_example(x[0], loss_target[0], meta, {n: w[n] for n in SMALL}, weights)
    weights.flush()

    g_big, d_big, m_big, v_big = {}, {}, {}, {}
    for n in big:
        held = [as_held(n, a) for a in (w[n], m[n], v[n])]
        outs = [lax.empty(held[0].shape, F32) for _ in range(4)]
        for l in reversed(range(depth)):
            outs = _reduce_adamw(weights.parts[n, l], *held, l, outs, "reduce_adamw_" + n)
        g_big[n], d_big[n], m_big[n], v_big[n] = [as_held(n, a) for a in outs]

    small = [grads[n] for n in SMALL] + [loss.reshape(1)]
    pad = SMALL_ROWS * PACK_W - sum(a.size for a in small)
    part = jnp.concatenate([_pack(small + [jnp.zeros((pad,), F32)], F32), d_meta], axis=0)
    total = _sum_parts(_all_gather([part], "gather_small")[0], "sum_small")
    small_w = [w[n] for n in SMALL]
    packs = [_pack([d[n] for n in SMALL] + [jnp.zeros((pad + 1,), F32)], F32) for d in (w, m, v)]
    upd = _adamw_call(packs[0], total[:SMALL_ROWS], packs[1], packs[2], "adamw_small")
    g_small, d_small, m_small, v_small = [dict(zip(SMALL, _unpack(p, small_w))) for p in (total[:SMALL_ROWS],) + tuple(upd)]
    loss_total = total[:SMALL_ROWS].reshape(-1)[SMALL_ROWS * PACK_W - pad - 1]
    g_meta = lax.dynamic_slice_in_dim(total[SMALL_ROWS:], me * LANE, LANE, axis=1)
    d_mt, m_mt, v_mt = _adamw_call(meta_tokens, g_meta, m_meta_tokens, v_meta_tokens, "adamw_meta")

    outs = []
    for got in ({**g_big, **g_small, "meta_tokens": g_meta}, {**d_big, **d_small, "meta_tokens": d_mt},
                {**m_big, **m_small, "meta_tokens": m_mt}, {**v_big, **v_small, "meta_tokens": v_mt}):
        outs += [got[n] for n in names]
    return (loss_total, grad_x[None], *outs)
```

```python
import jax
import jax.numpy as jnp
from jax import lax
from jax.experimental import pallas as pl
from jax.experimental.pallas import tpu as pltpu

F32 = jnp.float32
BF16 = jnp.bfloat16

D_MODEL = 1024
N_META = 16
BLOCK = 128
FRONT = (-N_META) % BLOCK
ROPE_THETA = 10000.0
EPS = 1e-6
NEG = -1e30
SWA_HEADS = 8
SWA_KV_HEADS = 2
SWA_GROUP = SWA_HEADS // SWA_KV_HEADS
SWA_HEAD_DIM = 64
MLA_HEADS = 8
MLA_Q_RANK = 256
MLA_KV_RANK = 128
MLA_NOPE_DIM = 64
MLA_ROPE_DIM = 32
MLA_V_DIM = 64
MLA_QK_DIM = MLA_NOPE_DIM + MLA_ROPE_DIM
SWA_Q_W = SWA_HEADS * SWA_HEAD_DIM
SWA_KV_W = SWA_KV_HEADS * SWA_HEAD_DIM
MLA_OUT_W = MLA_HEADS * MLA_V_DIM
SCALE_A = SWA_HEAD_DIM ** -0.5
SCALE_B = MLA_QK_DIM ** -0.5
LOG2E = 1.4426950408889634
Q_SCALE = SCALE_B * LOG2E
ADAM_LR = 0.001
ADAM_B1 = 0.9
ADAM_B2 = 0.999
ADAM_EPS = 1e-08
ADAM_WD = 0.01
ADAM_STEP = 10

LANE = 128
N_DEV = 8
HP = 8 * LANE
PO_QA, PO_KA, PO_VA = 0, HP, HP + 2 * LANE
PO_CQ = PO_VA + 2 * LANE
PO_CKV = PO_CQ + MLA_Q_RANK
PO_KR = PO_CKV + MLA_KV_RANK
PW_IN = PO_KR + LANE
N_TAB = 7
VMEM_LIMIT = 56 * 2 ** 20
TN_VMEM_BUDGET = 36 * 2 ** 20
MLA_HB = 4
MLA_HB_FWD = 8
ADAM_ROWS = 256
HALF = LANE // 2
assert SWA_HEAD_DIM == HALF and MLA_V_DIM == HALF

NT = (((1,), (1,)), ((), ()))
TN = (((0,), (0,)), ((), ()))


def _tile(t):
    return 384 if t % 384 == 0 else 128


def _params(*sem):
    return pltpu.CompilerParams(dimension_semantics=sem, vmem_limit_bytes=VMEM_LIMIT)


def _row(tm, n):
    return pl.BlockSpec((tm, n), lambda i: (i, 0))


def _const(shape):
    return pl.BlockSpec(shape, lambda i: (0,) * len(shape))


def _dot(a, b):
    return jnp.dot(a, b, preferred_element_type=F32)


def _dot_nt(a, b):
    return lax.dot_general(a, b, NT, preferred_element_type=F32)


def _dot_tn(a, b):
    return lax.dot_general(a, b, TN, preferred_element_type=F32)


def _rope(x, c, s1, s2, shift):
    return x * c + pltpu.roll(x, LANE - shift, 1) * s1 + pltpu.roll(x, shift, 1) * s2


def _rope_t(dy, c, s1, s2, shift):
    return dy * c + pltpu.roll(dy * s1, shift, 1) + pltpu.roll(dy * s2, LANE - shift, 1)


def _rms_r(x, n):
    return lax.rsqrt(jnp.sum(x * x, axis=-1, keepdims=True) * (1.0 / n) + EPS)


def _rms_bwd(x, g, dy, n):
    r = _rms_r(x, n)
    xh = x * r
    dxh = dy * g
    dx = r * (dxh - xh * (jnp.sum(dxh * xh, axis=-1, keepdims=True) * (1.0 / n)))
    return dx, jnp.sum(dy * xh, axis=0, keepdims=True)


def _acc(ref, val, first):
    @pl.when(first)
    def _():
        ref[...] = val

    @pl.when(jnp.logical_not(first))
    def _():
        ref[...] += val


def _pack_pair(even, odd):
    return even + pltpu.roll(odd, HALF, 1)


def _pair_half(slab, half):
    return slab if half == 0 else pltpu.roll(slab, HALF, 1)


def _unpack_pair(slab, half):
    x = _pair_half(slab, half)
    return jnp.where(lax.broadcasted_iota(jnp.int32, x.shape, 1) < HALF, x, 0.0)


def _tabs(tab_ref):
    return [tab_ref[:, LANE * i:LANE * (i + 1)] for i in range(N_TAB)]


def _pre_fwd(h, g1, win, gq, wqu, gkv, wkv, tabs):
    t = h.shape[0]
    tm = _tile(t)

    def body(h_ref, g1_ref, win_ref, gq_ref, wqu_ref, gkv_ref, wkv_ref, tab_ref,
             u_ref, qa_ref, ka_ref, va_ref, cq_ref, ckv_ref, qn_ref, kvn_ref, qb_ref, kf_ref, vb_ref):
        ca, sa1, sa2, cb, sb1, sb2, ck = _tabs(tab_ref)
        hv = h_ref[...]
        u = (hv * _rms_r(hv, D_MODEL) * g1_ref[...]).astype(BF16)
        u_ref[...] = u
        p = _dot_nt(u, win_ref[...])
        for c in range(SWA_HEADS):
            sl = slice(LANE * c, LANE * (c + 1))
            qa_ref[:, sl] = _rope(p[:, PO_QA + LANE * c:PO_QA + LANE * (c + 1)], ca, sa1, sa2, 32).astype(BF16)
        for c in range(SWA_KV_HEADS):
            sl = slice(LANE * c, LANE * (c + 1))
            ka_ref[:, sl] = _rope(p[:, PO_KA + LANE * c:PO_KA + LANE * (c + 1)], ca, sa1, sa2, 32).astype(BF16)
        va_ref[...] = p[:, PO_VA:PO_CQ].astype(BF16)
        cq = p[:, PO_CQ:PO_CKV]
        ckv = p[:, PO_CKV:PO_KR]
        cq_ref[...] = cq
        ckv_ref[...] = ckv
        qn = (cq * _rms_r(cq, MLA_Q_RANK) * gq_ref[...]).astype(BF16)
        qn_ref[...] = qn
        qb = _dot_nt(qn, wqu_ref[...])
        kvn = (ckv * _rms_r(ckv, MLA_KV_RANK) * gkv_ref[...]).astype(BF16)
        kvn_ref[...] = kvn
        kv = _dot_nt(kvn, wkv_ref[...])
        kr = _rope(p[:, PO_KR:PW_IN], ck, sb1, sb2, 16)
        for c in range(MLA_HEADS):
            sl = slice(LANE * c, LANE * (c + 1))
            qb_ref[:, sl] = (_rope(qb[:, sl], cb, sb1, sb2, 16) * Q_SCALE).astype(BF16)
            kf_ref[:, sl] = (kv[:, sl] + kr).astype(BF16)
        vb_ref[...] = kv[:, HP:].astype(BF16)

    widths = [(D_MODEL, BF16), (HP, BF16), (2 * LANE, BF16), (2 * LANE, BF16), (MLA_Q_RANK, F32),
              (MLA_KV_RANK, F32), (MLA_Q_RANK, BF16), (MLA_KV_RANK, BF16), (HP, BF16), (HP, BF16), (HP, BF16)]
    return pl.pallas_call(
        body, name="pre_fwd", grid=(t // tm,),
        in_specs=[_row(tm, D_MODEL), _const(g1.shape), _const(win.shape), _const(gq.shape), _const(wqu.shape),
                  _const(gkv.shape), _const(wkv.shape), _row(tm, N_TAB * LANE)],
        out_specs=[_row(tm, w) for w, _ in widths],
        out_shape=[jax.ShapeDtypeStruct((t, w), d) for w, d in widths],
        compiler_params=_params("parallel"),
    )(h, g1, win, gq, wqu, gkv, wkv, tabs)


def _swa_mask(nb):
    key = lax.broadcasted_iota(jnp.int32, (2 * BLOCK, SWA_GROUP * BLOCK), 0)
    qry = lax.broadcasted_iota(jnp.int32, (2 * BLOCK, SWA_GROUP * BLOCK), 1) & (BLOCK - 1)
    return (key > qry) & (key <= qry + BLOCK) & (key + (nb - 1) * BLOCK >= FRONT)


def _swa_group(ref, rows, j):
    return jnp.concatenate([ref[rows, LANE * (SWA_GROUP * j + g):LANE * (SWA_GROUP * j + g + 1)]
                            for g in range(SWA_GROUP)], axis=0)


def _swa_packed_group(ref, rows, j):
    heads = [SWA_GROUP * j + g for g in range(SWA_GROUP)]
    return jnp.concatenate([_pair_half(ref[rows, LANE * (hd // 2):LANE * (hd // 2 + 1)], hd % 2) for hd in heads], axis=0)


def _swa_sinks(sink_ref, j):
    return jnp.concatenate([jnp.full((1, BLOCK), sink_ref[0, SWA_GROUP * j + g], F32) for g in range(SWA_GROUP)], axis=1)


def _swa_keys(prev_ref, cur_ref, rb, j):
    sl = slice(LANE * j, LANE * (j + 1))
    if rb == 0:
        return jnp.concatenate([prev_ref[:, sl], cur_ref[:BLOCK, sl]], axis=0)
    return cur_ref[BLOCK * (rb - 1):BLOCK * (rb + 1), sl]


def _swa_chains(t):
    return [(rb, j) for rb in range(_tile(t) // BLOCK) for j in range(SWA_KV_HEADS)]


def _swa_scores(sink_ref, q_ref, kp_ref, kc_ref, n, t):
    r = _tile(t) // BLOCK
    chains = _swa_chains(t)
    qs = [_swa_group(q_ref, slice(BLOCK * rb, BLOCK * (rb + 1)), j) for rb, j in chains]
    ks = [_swa_keys(kp_ref, kc_ref, rb, j) for rb, j in chains]
    ss = [_dot_nt(k2, q4) for q4, k2 in zip(qs, ks)]
    masks = [_swa_mask(n * r + rb) for rb in range(r)]
    out = []
    for (rb, j), s in zip(chains, ss):
        sink = _swa_sinks(sink_ref, j)
        s = jnp.where(masks[rb], s * SCALE_A, NEG)
        m = jnp.maximum(jnp.max(s, axis=0, keepdims=True), sink)
        e = jnp.exp(s - m)
        es = jnp.exp(sink - m)
        inv = 1.0 / (jnp.sum(e, axis=0, keepdims=True) + es)
        out.append((e * inv, es * inv))
    return qs, ks, out


def _swa_specs(t):
    ts = _tile(t)
    r = ts // BLOCK
    prev = lambda n: (jnp.maximum(n * r - 1, 0), 0)
    cur = lambda n: (n, 0)
    return [pl.BlockSpec(memory_space=pltpu.SMEM), pl.BlockSpec((ts, HP), cur),
            pl.BlockSpec((BLOCK, 2 * LANE), prev), pl.BlockSpec((ts, 2 * LANE), cur),
            pl.BlockSpec((BLOCK, 2 * LANE), prev), pl.BlockSpec((ts, 2 * LANE), cur)]


def _swa_fwd(sinks, q, k, v):
    t = q.shape[0]
    ts = _tile(t)

    def body(sink_ref, q_ref, kp_ref, kc_ref, vp_ref, vc_ref, o_ref):
        chains = _swa_chains(t)
        _, _, probs = _swa_scores(sink_ref, q_ref, kp_ref, kc_ref, pl.program_id(0), t)
        os_ = [_dot_tn(p.astype(BF16), _swa_keys(vp_ref, vc_ref, rb, j)) for (rb, j), (p, _) in zip(chains, probs)]
        for (rb, j), o4 in zip(chains, os_):
            for g in range(0, SWA_GROUP, 2):
                pair = (SWA_GROUP * j + g) // 2
                o_ref[BLOCK * rb:BLOCK * (rb + 1), LANE * pair:LANE * (pair + 1)] = _pack_pair(
                    o4[BLOCK * g:BLOCK * (g + 1)], o4[BLOCK * (g + 1):BLOCK * (g + 2)])

    return pl.pallas_call(
        body, name="swa_fwd", grid=(t // ts,),
        in_specs=_swa_specs(t),
        out_specs=pl.BlockSpec((ts, SWA_Q_W), lambda n: (n, 0)),
        out_shape=jax.ShapeDtypeStruct((t, SWA_Q_W), F32),
        compiler_params=_params("parallel"),
    )(sinks, q, k, k, v, v)


def _causal_mask(q0, k0, tq, tk, transposed):
    if transposed:
        key = k0 + lax.broadcasted_iota(jnp.int32, (tk, tq), 0)
        qry = q0 + lax.broadcasted_iota(jnp.int32, (tk, tq), 1)
    else:
        qry = q0 + lax.broadcasted_iota(jnp.int32, (tq, tk), 0)
        key = k0 + lax.broadcasted_iota(jnp.int32, (tq, tk), 1)
    return (key <= qry) & (key >= FRONT)


def _heads(ref, hb, rows=slice(None)):
    return [ref[rows, LANE * a:LANE * (a + 1)] for a in range(hb)]


def _head_stats(t, hb=MLA_HB):
    return jax.ShapeDtypeStruct((MLA_HEADS // hb, t, hb), F32)


def _mla_fwd(q, k, v, shards=()):
    t = q.shape[0]
    tq = _tile(t)
    nq = t // tq
    n = len(shards)
    hb = MLA_HB_FWD
    steps = (MLA_HEADS // hb) * nq

    def body(q_ref, k_ref, v_ref, *rest):
        x_refs, (o_ref, lse_ref), out_refs = rest[:n], rest[n:n + 2], rest[n + 2:2 * n + 2]
        acc_sc, sems = rest[2 * n + 2], rest[2 * n + 3:]
        i = pl.program_id(1)
        step_id = pl.program_id(0) * nq + i
        if n:
            plan = _gather_plan(x_refs, out_refs, *sems)
            pl.when(step_id == 0)(plan.start)
            pl.when(step_id == (3 * steps) // 4)(plan.forward)
        qs = _heads(q_ref, hb)
        acc_sc[...] = jnp.zeros(acc_sc.shape, F32)

        def step(j, carry, masked):
            rows = pl.ds(pl.multiple_of(j * tq, tq), tq)
            ks, vs = _heads(k_ref, hb, rows), _heads(v_ref, hb, rows)
            ss = [_dot_nt(kh, qh) for qh, kh in zip(qs, ks)]
            if masked:
                mask = _causal_mask(i * tq, j * tq, tq, tq, True)
                ss = [jnp.where(mask, s, NEG) for s in ss]
            mid, out = [], []
            for s, (m, l) in zip(ss, carry):
                mn = jnp.maximum(m, jnp.max(s, axis=0, keepdims=True))
                al = jnp.exp2(m - mn)
                p = jnp.exp2(s - mn)
                out.append((mn, al * l + jnp.sum(p, axis=0, keepdims=True)))
                mid.append((al, p.astype(BF16)))
            for a, ((al, p), vh) in enumerate(zip(mid, vs)):
                acc_sc[a] = al * acc_sc[a] + _dot_tn(vh, p)
            return tuple(out)

        init = ((jnp.full((1, tq), NEG, F32), jnp.zeros((1, tq), F32)),) * hb
        carry = lax.fori_loop(0, jnp.minimum(i, 1) + 1, lambda it, c: step(it * i, c, True), init)
        carry = lax.fori_loop(1, i, lambda j, c: step(j, c, False), carry)
        outs = [(acc_sc[a] * (1.0 / l)).T for a, (_, l) in enumerate(carry)]
        for a in range(0, hb, 2):
            o_ref[:, HALF * a:HALF * (a + 2)] = _pack_pair(outs[a], outs[a + 1])
        for a, (m, l) in enumerate(carry):
            lse_ref[:, a:a + 1] = jnp.broadcast_to(m + jnp.log2(l), (LANE, tq)).T[:, :1]
        if n:
            pl.when(step_id == steps - 1)(plan.finish)

    blk = pl.BlockSpec((tq, hb * LANE), lambda h, i: (i, h))
    full = pl.BlockSpec((t, hb * LANE), lambda h, i: (0, h))
    packed = pl.BlockSpec((tq, hb * HALF), lambda h, i: (i, h))
    out = pl.pallas_call(
        body, name="mla_fwd_gather" if n else "mla_fwd", grid=(MLA_HEADS // hb, nq),
        in_specs=[blk, full, full] + [ANY] * n,
        out_specs=[packed, pl.BlockSpec((None, tq, hb), lambda h, i: (h, i, 0))] + [ANY] * n,
        out_shape=[jax.ShapeDtypeStruct((t, MLA_OUT_W), F32), _head_stats(t, hb)]
        + [jax.ShapeDtypeStruct((N_DEV,) + a.shape, a.dtype) for a in shards],
        scratch_shapes=[pltpu.VMEM((hb, LANE, tq), F32)] + (_comm_sems(n) if n else []),
        compiler_params=_params("arbitrary", "arbitrary"),
    )(q, k, v, *shards)
    return out[0], out[1], out[2:]


def _mix_fwd(h, oa, ob, ga, gb, wo, g2):
    t = h.shape[0]
    tm = _tile(t)

    def body(h_ref, oa_ref, ob_ref, ga_ref, gb_ref, wo_ref, g2_ref, h2_ref, mix_ref, u2_ref):
        oa_v = oa_ref[...]
        ob_v = ob_ref[...]
        na = (oa_v * _rms_r(oa_v, SWA_Q_W) * ga_ref[...]).astype(BF16)
        nb = (ob_v * _rms_r(ob_v, MLA_OUT_W) * gb_ref[...]).astype(BF16)
        mix_ref[:, :SWA_Q_W] = na
        mix_ref[:, SWA_Q_W:] = nb
        h2 = h_ref[...] + _dot(na, wo_ref[:SWA_Q_W, :]) + _dot(nb, wo_ref[SWA_Q_W:, :])
        h2_ref[...] = h2
        u2_ref[...] = (h2 * _rms_r(h2, D_MODEL) * g2_ref[...]).astype(BF16)

    mix_w = SWA_Q_W + MLA_OUT_W
    return pl.pallas_call(
        body, name="mix_fwd", grid=(t // tm,),
        in_specs=[_row(tm, D_MODEL), _row(tm, SWA_Q_W), _row(tm, MLA_OUT_W), _const(ga.shape), _const(gb.shape),
                  _const(wo.shape), _const(g2.shape)],
        out_specs=[_row(tm, D_MODEL), _row(tm, mix_w), _row(tm, D_MODEL)],
        out_shape=[jax.ShapeDtypeStruct((t, D_MODEL), F32), jax.ShapeDtypeStruct((t, mix_w), BF16),
                   jax.ShapeDtypeStruct((t, D_MODEL), BF16)],
        compiler_params=_params("parallel"),
    )(h, oa, ob, ga, gb, wo, g2)


def _ffn_fwd(h2, u2, wg_t, wu_t, wd):
    t = h2.shape[0]
    tm = _tile(t)
    dff = wd.shape[0]

    def body(h2_ref, u2_ref, wg_ref, wu_ref, wd_ref, h3_ref, g_ref, up_ref):
        u2v = u2_ref[...]
        g = _dot_nt(u2v, wg_ref[...])
        up = _dot_nt(u2v, wu_ref[...])
        g_ref[...] = g.astype(BF16)
        up_ref[...] = up.astype(BF16)
        a = (g * jax.nn.sigmoid(g) * up).astype(BF16)
        h3_ref[...] = h2_ref[...] + _dot(a, wd_ref[...])

    return pl.pallas_call(
        body, name="ffn_fwd", grid=(t // tm,),
        in_specs=[_row(tm, D_MODEL), _row(tm, D_MODEL), _const(wg_t.shape), _const(wu_t.shape), _const(wd.shape)],
        out_specs=[_row(tm, D_MODEL), _row(tm, dff), _row(tm, dff)],
        out_shape=[jax.ShapeDtypeStruct((t, D_MODEL), F32), jax.ShapeDtypeStruct((t, dff), BF16),
                   jax.ShapeDtypeStruct((t, dff), BF16)],
        compiler_params=_params("parallel"),
    )(h2, u2, wg_t, wu_t, wd)


def _loss_bwd(h, gf, target):
    t = h.shape[0]
    tm = _tile(t)
    first_row = FRONT + N_META

    def body(h_ref, gf_ref, t_ref, dh_ref, dgf_ref, loss_ref):
        i = pl.program_id(0)
        hv = h_ref[...]
        y = hv * _rms_r(hv, D_MODEL) * gf_ref[...]
        row = i * tm + lax.broadcasted_iota(jnp.int32, (tm, 1), 0)
        err = jnp.where(row >= first_row, y - t_ref[...], 0.0)
        dx, dg = _rms_bwd(hv, gf_ref[...], err * (1.0 / D_MODEL), D_MODEL)
        dh_ref[...] = dx
        _acc(dgf_ref, dg, i == 0)
        part = 0.5 * jnp.sum(jnp.sum(err * err, axis=1, keepdims=True) * (1.0 / D_MODEL), axis=0, keepdims=True)
        _acc(loss_ref, jnp.broadcast_to(part, (1, LANE)), i == 0)

    return pl.pallas_call(
        body, name="loss_bwd", grid=(t // tm,),
        in_specs=[_row(tm, D_MODEL), _const(gf.shape), _row(tm, D_MODEL)],
        out_specs=[_row(tm, D_MODEL), _const((1, D_MODEL)), _const((1, LANE))],
        out_shape=[jax.ShapeDtypeStruct((t, D_MODEL), F32), jax.ShapeDtypeStruct((1, D_MODEL), F32),
                   jax.ShapeDtypeStruct((1, LANE), F32)],
        compiler_params=_params("arbitrary"),
    )(h, gf, target)


def _tn_matmul(a, b, name, cols=None):
    t, n = b.shape
    first, k = cols or (0, a.shape[1])
    tk = next(c for c in (k, 1024, 512, 256, 128) if k % c == 0 and first % c == 0 and c <= 1024)
    fits = lambda c: 2 * (t * (tk + c) * 2 + tk * c * 2) <= TN_VMEM_BUDGET
    tn = next(c for c in (n, 1024, 512, 256, 128) if n % c == 0 and fits(c))

    def body(a_ref, b_ref, o_ref):
        o_ref[...] = _dot_tn(a_ref[...], b_ref[...]).astype(BF16)

    return pl.pallas_call(
        body, name=name, grid=(k // tk, n // tn),
        in_specs=[pl.BlockSpec((t, tk), lambda i, j: (0, i + first // tk)), pl.BlockSpec((t, tn), lambda i, j: (0, j))],
        out_specs=pl.BlockSpec((tk, tn), lambda i, j: (i, j)),
        out_shape=jax.ShapeDtypeStruct((k, n), BF16),
        compiler_params=_params("parallel", "parallel"),
    )(a, b)


def _ffn_bwd_a(dh3, g, up, wd):
    t = dh3.shape[0]
    tm = _tile(t)
    dff = wd.shape[0]

    def body(dh3_ref, g_ref, up_ref, wd_ref, a_ref, dgu_ref, dh3b_ref):
        dh3b = dh3_ref[...].astype(BF16)
        dh3b_ref[...] = dh3b
        da = _dot_nt(dh3b, wd_ref[...])
        gv = g_ref[...].astype(F32)
        upv = up_ref[...].astype(F32)
        sg = jax.nn.sigmoid(gv)
        silu = gv * sg
        a_ref[...] = (silu * upv).astype(BF16)
        dgu_ref[:, :dff] = (da * upv * (sg * (1.0 + gv * (1.0 - sg)))).astype(BF16)
        dgu_ref[:, dff:] = (da * silu).astype(BF16)

    return pl.pallas_call(
        body, name="ffn_bwd_a", grid=(t // tm,),
        in_specs=[_row(tm, D_MODEL), _row(tm, dff), _row(tm, dff), _const(wd.shape)],
        out_specs=[_row(tm, dff), _row(tm, 2 * dff), _row(tm, D_MODEL)],
        out_shape=[jax.ShapeDtypeStruct((t, dff), BF16), jax.ShapeDtypeStruct((t, 2 * dff), BF16),
                   jax.ShapeDtypeStruct((t, D_MODEL), BF16)],
        compiler_params=_params("parallel"),
    )(dh3, g, up, wd)


def _ffn_bwd_b(dh3, dgu, h2, g2, wg_t, wu_t):
    t = dh3.shape[0]
    tm = _tile(t)
    dff = wg_t.shape[0]

    def body(dh3_ref, dgu_ref, h2_ref, g2_ref, wg_ref, wu_ref, dh2_ref, dh2b_ref, dg2_ref):
        du2 = _dot(dgu_ref[:, :dff], wg_ref[...]) + _dot(dgu_ref[:, dff:], wu_ref[...])
        dx, dg = _rms_bwd(h2_ref[...], g2_ref[...], du2, D_MODEL)
        dh2 = dh3_ref[...] + dx
        dh2_ref[...] = dh2
        dh2b_ref[...] = dh2.astype(BF16)
        _acc(dg2_ref, dg, pl.program_id(0) == 0)

    return pl.pallas_call(
        body, name="ffn_bwd_b", grid=(t // tm,),
        in_specs=[_row(tm, D_MODEL), _row(tm, 2 * dff), _row(tm, D_MODEL), _const(g2.shape), _const(wg_t.shape),
                  _const(wu_t.shape)],
        out_specs=[_row(tm, D_MODEL), _row(tm, D_MODEL), _const((1, D_MODEL))],
        out_shape=[jax.ShapeDtypeStruct((t, D_MODEL), F32), jax.ShapeDtypeStruct((t, D_MODEL), BF16),
                   jax.ShapeDtypeStruct((1, D_MODEL), F32)],
        compiler_params=_params("arbitrary"),
    )(dh3, dgu, h2, g2, wg_t, wu_t)


def _mix_bwd(dh2, oa, ob, ga, gb, wo):
    t = dh2.shape[0]
    tm = _tile(t)

    def body(dh2_ref, oa_ref, ob_ref, ga_ref, gb_ref, wo_ref, doa_ref, dob_ref, dl_ref, dga_ref, dgb_ref):
        first = pl.program_id(0) == 0
        d = dh2_ref[...]
        ob_v = ob_ref[...]
        dxa, dga = _rms_bwd(oa_ref[...], ga_ref[...], _dot_nt(d, wo_ref[:SWA_Q_W, :]), SWA_Q_W)
        dxb, dgb = _rms_bwd(ob_v, gb_ref[...], _dot_nt(d, wo_ref[SWA_Q_W:, :]), MLA_OUT_W)
        lower = lax.broadcasted_iota(jnp.int32, (tm, LANE), 1) < HALF
        for hd in range(MLA_HEADS):
            sl = slice(LANE * (hd // 2), LANE * (hd // 2 + 1))
            mine = lower if hd % 2 == 0 else jnp.logical_not(lower)
            delta = jnp.sum(jnp.where(mine, ob_v[:, sl] * dxb[:, sl], 0.0), axis=1, keepdims=True)
            dl_ref[hd // MLA_HB, :, hd % MLA_HB:hd % MLA_HB + 1] = delta
        for ref, dx, heads in ((doa_ref, dxa, SWA_HEADS), (dob_ref, dxb, MLA_HEADS)):
            for hd in range(heads):
                slab = dx[:, LANE * (hd // 2):LANE * (hd // 2 + 1)]
                ref[:, LANE * hd:LANE * (hd + 1)] = _unpack_pair(slab, hd % 2).astype(BF16)
        _acc(dga_ref, dga, first)
        _acc(dgb_ref, dgb, first)

    return pl.pallas_call(
        body, name="mix_bwd", grid=(t // tm,),
        in_specs=[_row(tm, D_MODEL), _row(tm, SWA_Q_W), _row(tm, MLA_OUT_W), _const(ga.shape), _const(gb.shape),
                  _const(wo.shape)],
        out_specs=[_row(tm, HP), _row(tm, HP), pl.BlockSpec((MLA_HEADS // MLA_HB, tm, MLA_HB), lambda i: (0, i, 0)),
                   _const((1, SWA_Q_W)), _const((1, MLA_OUT_W))],
        out_shape=[jax.ShapeDtypeStruct((t, HP), BF16), jax.ShapeDtypeStruct((t, HP), BF16), _head_stats(t),
                   jax.ShapeDtypeStruct((1, SWA_Q_W), F32), jax.ShapeDtypeStruct((1, MLA_OUT_W), F32)],
        compiler_params=_params("arbitrary"),
    )(dh2, oa, ob, ga, gb, wo)


def _swa_bwd(sinks, q, k, v, o, do):
    t = q.shape[0]
    ts = _tile(t)

    def body(sink_ref, q_ref, kp_ref, kc_ref, vp_ref, vc_ref, o_ref, do_ref,
             dq_ref, dkc_ref, dkp_ref, dvc_ref, dvp_ref, dsink_ref):
        n = pl.program_id(0)
        chains = _swa_chains(t)
        qs, ks, probs = _swa_scores(sink_ref, q_ref, kp_ref, kc_ref, n, t)
        dos = [_swa_group(do_ref, slice(BLOCK * rb, BLOCK * (rb + 1)), j) for rb, j in chains]
        vs = [_swa_keys(vp_ref, vc_ref, rb, j) for rb, j in chains]
        dps = [_dot_nt(v2, do4) for do4, v2 in zip(dos, vs)]
        dss, dsks = [], []
        for (rb, j), (p, psink), do4, dp in zip(chains, probs, dos, dps):
            o4 = _swa_packed_group(o_ref, slice(BLOCK * rb, BLOCK * (rb + 1)), j)
            delta = jnp.sum(o4 * do4.astype(F32), axis=1, keepdims=True)
            delta = jnp.broadcast_to(delta, (SWA_GROUP * BLOCK, LANE)).T[:1, :]
            dss.append((p * (dp - delta) * SCALE_A).astype(BF16))
            dsks.append(-psink * delta)
        dqs = [_dot_tn(ds, k2) for ds, k2 in zip(dss, ks)]
        dks = [_dot(ds, q4) for ds, q4 in zip(dss, qs)]
        dvs = [_dot(p.astype(BF16), do4) for (p, _), do4 in zip(probs, dos)]
        dsink = [jnp.zeros((1, LANE), F32)] * SWA_HEADS
        ext = {}
        for (rb, j), dq4, dk2, dv2, dsk in zip(chains, dqs, dks, dvs, dsks):
            for g in range(SWA_GROUP):
                hd = SWA_GROUP * j + g
                rows = slice(BLOCK * g, BLOCK * (g + 1))
                dq_ref[BLOCK * rb:BLOCK * (rb + 1), LANE * hd:LANE * (hd + 1)] = dq4[rows].astype(BF16)
                dsink[hd] = dsink[hd] + jnp.sum(dsk[:, rows], axis=1, keepdims=True)
            for half in range(2):
                key = (j, rb + half)
                part = (dk2[BLOCK * half:BLOCK * (half + 1)], dv2[BLOCK * half:BLOCK * (half + 1)])
                ext[key] = part if key not in ext else (ext[key][0] + part[0], ext[key][1] + part[1])
        for (j, blk), (dk, dv) in ext.items():
            sl = slice(LANE * j, LANE * (j + 1))
            if blk == 0:
                dkp_ref[:, sl] = dk
                dvp_ref[:, sl] = dv
            else:
                dkc_ref[BLOCK * (blk - 1):BLOCK * blk, sl] = dk
                dvc_ref[BLOCK * (blk - 1):BLOCK * blk, sl] = dv
        for hd in range(SWA_HEADS):
            _acc(dsink_ref.at[hd:hd + 1, :], jnp.broadcast_to(dsink[hd], (1, LANE)), n == 0)

    cur = lambda n: (n, 0)
    kv = pl.BlockSpec((ts, 2 * LANE), cur)
    kvp = pl.BlockSpec((BLOCK, 2 * LANE), cur)
    hp = pl.BlockSpec((ts, HP), cur)
    kvs = jax.ShapeDtypeStruct((t, 2 * LANE), F32)
    kvps = jax.ShapeDtypeStruct((t // ts * BLOCK, 2 * LANE), F32)
    return pl.pallas_call(
        body, name="swa_bwd", grid=(t // ts,),
        in_specs=_swa_specs(t) + [pl.BlockSpec((ts, SWA_Q_W), cur), hp],
        out_specs=[hp, kv, kvp, kv, kvp, _const((SWA_HEADS, LANE))],
        out_shape=[jax.ShapeDtypeStruct((t, HP), BF16), kvs, kvps, kvs, kvps,
                   jax.ShapeDtypeStruct((SWA_HEADS, LANE), F32)],
        compiler_params=_params("arbitrary"),
    )(sinks, q, k, k, v, v, o, do)


def _mla_bwd(q, k, v, do, lse, dl, slabs=()):
    t = q.shape[0]
    tq = _tile(t)
    nq = t // tq
    n = len(slabs)
    hb = MLA_HB
    steps = (MLA_HEADS // hb) * nq

    def body(k_ref, v_ref, q_ref, do_ref, lse_ref, dl_ref, *rest):
        in_refs, (dq_ref, dk_ref, dv_ref), out_refs = rest[:n], rest[n:n + 3], rest[n + 3:2 * n + 3]
        (dq_sc, dk_sc, dv_sc), sems = rest[2 * n + 3:2 * n + 6], rest[2 * n + 6:]
        j = pl.program_id(1)
        step_id = pl.program_id(0) * nq + j
        if n:
            plan = _exchange_plan(in_refs, out_refs, *sems)
            pl.when(step_id == 0)(plan.start)

        @pl.when(j == 0)
        def _():
            dq_sc[...] = jnp.zeros(dq_sc.shape, F32)

        dk_sc[...] = jnp.zeros(dk_sc.shape, F32)
        dv_sc[...] = jnp.zeros(dv_sc.shape, F32)
        ks, vs = _heads(k_ref, hb), _heads(v_ref, hb)

        def step(i, carry, masked):
            rows = pl.ds(pl.multiple_of(i * tq, tq), tq)
            qs, dos = _heads(q_ref, hb, rows), _heads(do_ref, hb, rows)
            ss = [_dot_nt(qh, kh) for qh, kh in zip(qs, ks)]
            dps = [_dot_nt(doh, vh) for doh, vh in zip(dos, vs)]
            if masked:
                mask = _causal_mask(i * tq, j * tq, tq, tq, False)
                ss = [jnp.where(mask, s_, NEG) for s_ in ss]
            ps = [jnp.exp2(s_ - lse_ref[rows, a:a + 1]) for a, s_ in enumerate(ss)]
            dss = [(p * (dp - dl_ref[rows, a:a + 1])).astype(BF16) for a, (p, dp) in enumerate(zip(ps, dps))]
            for a, (ds, p, qh, kh, doh) in enumerate(zip(dss, ps, qs, ks, dos)):
                dq_sc[a, rows, :] += _dot(ds, kh)
                dk_sc[a] += _dot_tn(ds, qh)
                dv_sc[a] += _dot_tn(p.astype(BF16), doh)
            return carry

        split = jnp.where(j == 0, nq, j + 1)
        lax.fori_loop(j, split, lambda i, c: step(i, c, True), 0)
        lax.fori_loop(split, nq, lambda i, c: step(i, c, False), 0)
        for a in range(hb):
            dk_ref[:, LANE * a:LANE * (a + 1)] = (dk_sc[a] * (1.0 / LOG2E)).astype(BF16)
            dv_ref[:, LANE * a:LANE * (a + 1)] = dv_sc[a].astype(BF16)

        @pl.when(j == nq - 1)
        def _():
            for a in range(hb):
                dq_ref[:, LANE * a:LANE * (a + 1)] = (dq_sc[a] * SCALE_B).astype(BF16)

        if n:
            pl.when(step_id == steps - 1)(plan.finish)

    blk = pl.BlockSpec((tq, hb * LANE), lambda h, j: (j, h))
    full = pl.BlockSpec((t, hb * LANE), lambda h, j: (0, h))
    cols = pl.BlockSpec((None, t, hb), lambda h, j: (h, 0, 0))
    out = pl.pallas_call(
        body, name="mla_bwd_exchange" if n else "mla_bwd", grid=(MLA_HEADS // hb, nq),
        in_specs=[blk, blk, full, full, cols, cols] + [ANY] * n, out_specs=[full, blk, blk] + [ANY] * n,
        out_shape=[jax.ShapeDtypeStruct((t, HP), BF16)] * 3 + [jax.ShapeDtypeStruct(a.shape, a.dtype) for a in slabs],
        scratch_shapes=[pltpu.VMEM((hb, t, LANE), F32)] + [pltpu.VMEM((hb, tq, LANE), F32)] * 2
        + (_comm_sems(n) if n else []),
        compiler_params=_params("arbitrary", "arbitrary"),
    )(k, v, q, do, lse, dl, *slabs)
    return out[:3], out[3:]


def _pre_bwd(dh2, h, cq, ckv, dqa, dka, dka_next, dva, dva_next, dqb, dkf, dvb, g1, win, gq, wqu, gkv, wkv, tabs):
    t = h.shape[0]
    tm = _tile(t)

    def body(dh2_ref, h_ref, cq_ref, ckv_ref, dqa_ref, dka_ref, dkan_ref, dva_ref, dvan_ref, dqb_ref, dkf_ref, dvb_ref,
             g1_ref, win_ref, gq_ref, wqu_ref, gkv_ref, wkv_ref, tab_ref,
             dh_ref, dp_ref, dqbo_ref, dkvo_ref, dg1_ref, dgq_ref, dgkv_ref):
        first = pl.program_id(0) == 0
        ca, sa1, sa2, cb, sb1, sb2, ck = _tabs(tab_ref)
        dkr = jnp.zeros((tm, LANE), F32)
        for c in range(MLA_HEADS):
            sl = slice(LANE * c, LANE * (c + 1))
            dqbo_ref[:, sl] = _rope_t(dqb_ref[:, sl].astype(F32), cb, sb1, sb2, 16).astype(BF16)
            dkr += dkf_ref[:, sl].astype(F32)
        dkvo_ref[:, :HP] = dkf_ref[...]
        dkvo_ref[:, HP:] = dvb_ref[...]
        dcq, dgq = _rms_bwd(cq_ref[...], gq_ref[...], _dot(dqbo_ref[...], wqu_ref[...]), MLA_Q_RANK)
        dckv, dgkv = _rms_bwd(ckv_ref[...], gkv_ref[...], _dot(dkvo_ref[...], wkv_ref[...]), MLA_KV_RANK)
        for c in range(SWA_HEADS):
            sl = slice(LANE * c, LANE * (c + 1))
            dp_ref[:, PO_QA + LANE * c:PO_QA + LANE * (c + 1)] = _rope_t(dqa_ref[:, sl].astype(F32), ca, sa1, sa2,
                                                                          32).astype(BF16)
        last = slice(tm - BLOCK, tm)
        more = pl.program_id(0) < t // tm - 1
        for c in range(SWA_KV_HEADS):
            sl = slice(LANE * c, LANE * (c + 1))
            dk = dka_ref[:, sl]
            dk_last = dk[tm - BLOCK:] + jnp.where(more, dkan_ref[:, sl], 0.0)
            cols = slice(PO_KA + LANE * c, PO_KA + LANE * (c + 1))
            if tm > BLOCK:
                dp_ref[:tm - BLOCK, cols] = _rope_t(dk[:tm - BLOCK], ca[:tm - BLOCK], sa1[:tm - BLOCK], sa2[:tm - BLOCK],
                                                    32).astype(BF16)
            dp_ref[last, cols] = _rope_t(dk_last, ca[tm - BLOCK:], sa1[tm - BLOCK:], sa2[tm - BLOCK:], 32).astype(BF16)
        if tm > BLOCK:
            dp_ref[:tm - BLOCK, PO_VA:PO_CQ] = dva_ref[:tm - BLOCK, :].astype(BF16)
        dp_ref[last, PO_VA:PO_CQ] = (dva_ref[tm - BLOCK:, :] + jnp.where(more, dvan_ref[...], 0.0)).astype(BF16)
        dp_ref[:, PO_CQ:PO_CKV] = dcq.astype(BF16)
        dp_ref[:, PO_CKV:PO_KR] = dckv.astype(BF16)
        dp_ref[:, PO_KR:PW_IN] = _rope_t(dkr, ck, sb1, sb2, 16).astype(BF16)
        dx, dg1 = _rms_bwd(h_ref[...], g1_ref[...], _dot(dp_ref[...], win_ref[...]), D_MODEL)
        dh_ref[...] = dh2_ref[...] + dx
        _acc(dg1_ref, dg1, first)
        _acc(dgq_ref, dgq, first)
        _acc(dgkv_ref, dgkv, first)

    kv = _row(tm, 2 * LANE)
    nxt = pl.BlockSpec((BLOCK, 2 * LANE), lambda i: (jnp.minimum(i + 1, t // tm - 1), 0))
    return pl.pallas_call(
        body, name="pre_bwd", grid=(t // tm,),
        in_specs=[_row(tm, D_MODEL), _row(tm, D_MODEL), _row(tm, MLA_Q_RANK), _row(tm, MLA_KV_RANK), _row(tm, HP),
                  kv, nxt, kv, nxt, _row(tm, HP), _row(tm, HP), _row(tm, HP),
                  _const(g1.shape), _const(win.shape), _const(gq.shape), _const(wqu.shape), _const(gkv.shape),
                  _const(wkv.shape), _row(tm, N_TAB * LANE)],
        out_specs=[_row(tm, D_MODEL), _row(tm, PW_IN), _row(tm, HP), _row(tm, 2 * HP),
                   _const((1, D_MODEL)), _const((1, MLA_Q_RANK)), _const((1, MLA_KV_RANK))],
        out_shape=[jax.ShapeDtypeStruct((t, D_MODEL), F32), jax.ShapeDtypeStruct((t, PW_IN), BF16),
                   jax.ShapeDtypeStruct((t, HP), BF16), jax.ShapeDtypeStruct((t, 2 * HP), BF16),
                   jax.ShapeDtypeStruct((1, D_MODEL), F32), jax.ShapeDtypeStruct((1, MLA_Q_RANK), F32),
                   jax.ShapeDtypeStruct((1, MLA_KV_RANK), F32)],
        compiler_params=_params("arbitrary"),
    )(dh2, h, cq, ckv, dqa, dka, dka_next, dva, dva_next, dqb, dkf, dvb, g1, win, gq, wqu, gkv, wkv, tabs)


def _rope_tables(t):
    pos = (jnp.arange(t, dtype=jnp.int32) - FRONT).astype(F32)[:, None]
    lane = jnp.arange(LANE)[None, :]

    def table(dim, start):
        half = dim // 2
        inv = ROPE_THETA ** (-jnp.arange(0, dim, 2, dtype=F32) / dim)
        ang = pos * inv[None, :]
        cos = jnp.concatenate([jnp.cos(ang)] * 2, axis=1)
        sin = jnp.concatenate([jnp.sin(ang)] * 2, axis=1)
        pad = lambda a: jnp.pad(a, ((0, 0), (start, LANE - start - dim)))
        first = (lane >= start) & (lane < start + half)
        second = (lane >= start + half) & (lane < start + dim)
        return pad(cos), jnp.where(first, -pad(sin), 0.0), jnp.where(second, pad(sin), 0.0)

    ca, sa1, sa2 = table(SWA_HEAD_DIM, 0)
    ck, sb1, sb2 = table(MLA_ROPE_DIM, MLA_NOPE_DIM)
    cb = jnp.where(lane < MLA_NOPE_DIM, 1.0, ck)
    return jnp.concatenate([ca, sa1, sa2, cb, sb1, sb2, ck], axis=1)


def _pad_heads(w, heads, dim, axis):
    shp = w.shape
    w = w.reshape(shp[:axis] + (heads, dim) + shp[axis + 1:])
    pad = [(0, 0)] * w.ndim
    pad[axis + 1] = (0, LANE - dim)
    return jnp.pad(w, pad).reshape(shp[:axis] + (heads * LANE,) + shp[axis + 1:])


def _unpad_heads(w, heads, dim, axis):
    shp = w.shape
    w = w.reshape(shp[:axis] + (heads, LANE) + shp[axis + 1:])
    w = lax.slice_in_dim(w, 0, dim, axis=axis + 1)
    return w.reshape(shp[:axis] + (heads * dim,) + shp[axis + 1:])


def _pad_layer(w_in, w_q_up, w_kv_up):
    o1 = SWA_Q_W
    o2 = o1 + SWA_KV_W
    o3 = o2 + SWA_KV_W
    o4 = o3 + MLA_Q_RANK
    o5 = o4 + MLA_KV_RANK
    kr = jnp.pad(w_in[o5:], ((MLA_NOPE_DIM, LANE - MLA_QK_DIM), (0, 0)))
    win = jnp.concatenate([
        _pad_heads(w_in[:o1], SWA_HEADS, SWA_HEAD_DIM, 0),
        _pad_heads(w_in[o1:o2], SWA_KV_HEADS, SWA_HEAD_DIM, 0),
        _pad_heads(w_in[o2:o3], SWA_KV_HEADS, SWA_HEAD_DIM, 0),
        w_in[o3:o5], kr], axis=0)
    wqu = _pad_heads(w_q_up, MLA_HEADS, MLA_QK_DIM, 0)
    kv = w_kv_up.reshape(MLA_HEADS, MLA_NOPE_DIM + MLA_V_DIM, MLA_KV_RANK)
    wkv = jnp.concatenate([
        _pad_heads(kv[:, :MLA_NOPE_DIM].reshape(-1, MLA_KV_RANK), MLA_HEADS, MLA_NOPE_DIM, 0),
        _pad_heads(kv[:, MLA_NOPE_DIM:].reshape(-1, MLA_KV_RANK), MLA_HEADS, MLA_V_DIM, 0)], axis=0)
    return win, wqu, wkv


def _unpad_layer(dwin, dwqu, dwkv):
    d_w_in = jnp.concatenate([
        _unpad_heads(dwin[PO_QA:PO_KA], SWA_HEADS, SWA_HEAD_DIM, 0),
        _unpad_heads(dwin[PO_KA:PO_VA], SWA_KV_HEADS, SWA_HEAD_DIM, 0),
        _unpad_heads(dwin[PO_VA:PO_CQ], SWA_KV_HEADS, SWA_HEAD_DIM, 0),
        dwin[PO_CQ:PO_KR], dwin[PO_KR + MLA_NOPE_DIM:PO_KR + MLA_QK_DIM]], axis=0)
    d_w_q_up = _unpad_heads(dwqu, MLA_HEADS, MLA_QK_DIM, 0)
    dk = _unpad_heads(dwkv[:HP], MLA_HEADS, MLA_NOPE_DIM, 0).reshape(MLA_HEADS, MLA_NOPE_DIM, MLA_KV_RANK)
    dv = _unpad_heads(dwkv[HP:], MLA_HEADS, MLA_V_DIM, 0).reshape(MLA_HEADS, MLA_V_DIM, MLA_KV_RANK)
    d_w_kv_up = jnp.concatenate([dk, dv], axis=1).reshape(-1, MLA_KV_RANK)
    return d_w_in, d_w_q_up, d_w_kv_up


def _train_example(x, target, meta, vec, weights):
    s = x.shape[0]
    depth = vec["attn_norm"].shape[0]
    t = FRONT + N_META + s
    assert t % BLOCK == 0
    tabs = _rope_tables(t)
    h = jnp.concatenate([jnp.zeros((FRONT, D_MODEL), F32), meta, x], axis=0)
    tgt = jnp.concatenate([jnp.zeros((FRONT + N_META, D_MODEL), F32), target], axis=0)
    row = lambda v: v[None, :]

    saved = []
    for l in range(depth):
        win, wqu, wkv = _pad_layer(*weights.attn_in(l))
        g1, gq, gkv, g2, ga, gb = (row(vec[n][l]) for n in ("attn_norm", "q_norm", "kv_norm", "ffn_norm",
                                                            "out_norm_swa", "out_norm_mla"))
        sk = row(vec["sinks"][l])
        u, qa, ka, va, cq, ckv, qn, kvn, qb, kf, vb = _pre_fwd(h, g1, win, gq, wqu, gkv, wkv, tabs)
        oa = _swa_fwd(sk, qa, ka, va)
        ob, lse = weights.mla_fwd(l, qb, kf, vb)
        lse = jnp.moveaxis(lse.reshape(t, MLA_HEADS // MLA_HB, MLA_HB), 1, 0)
        wo = weights.w_o(l)
        h2, mix, u2 = _mix_fwd(h, oa, ob, ga, gb, wo, g2)
        wg, wu, wd = weights.ffn(l)
        h3, gt, up = _ffn_fwd(h2, u2, wg, wu, wd)
        saved.append((h, u, qa, ka, va, cq, ckv, qn, kvn, qb, kf, vb, oa, ob, lse, h2, mix, u2, gt, up,
                      win, wqu, wkv, wo, ga, gb, g1, gq, gkv, g2, sk, wg, wu, wd))
        h = h3

    dh, d_final, loss = _loss_bwd(h, row(vec["final_norm"]), tgt)

    grads = []
    for l in reversed(range(depth)):
        (h0, u, qa, ka, va, cq, ckv, qn, kvn, qb, kf, vb, oa, ob, lse, h2, mix, u2, gt, up,
         win, wqu, wkv, wo, ga, gb, g1, gq, gkv, g2, sk, wg, wu, wd) = saved[l]
        dff = wd.shape[0]
        act, dgu, dhb = _ffn_bwd_a(dh, gt, up, wd)
        weights.ffn_grads(l, _tn_matmul(dgu, u2, "dw_gate", (0, dff)), _tn_matmul(dgu, u2, "dw_up", (dff, dff)),
                          _tn_matmul(act, dhb, "dw_down"))
        dh2, dh2b, d_g2 = _ffn_bwd_b(dh, dgu, h2, g2, wg, wu)
        weights.attn_grads(l, w_o=_tn_matmul(mix, dh2b, "dw_o"))
        doa, dob, dl, d_ga, d_gb = _mix_bwd(dh2b, oa, ob, ga, gb, wo)
        dqa, dkc, dkp, dvc, dvp, dsink = _swa_bwd(sk, qa, ka, va, oa, doa)
        dqb, dkf, dvb = weights.mla_bwd(l, qb, kf, vb, dob, lse, dl)
        dh, dp, dqbo, dkvo, d_g1, d_gq, d_gkv = _pre_bwd(
            dh2, h0, cq, ckv, dqa, dkc, dkp, dvc, dvp, dqb, dkf, dvb,
            g1, win, gq, wqu, gkv, wkv, tabs)
        d_win = _tn_matmul(dp, u, "dw_in")
        d_wqu = _tn_matmul(dqbo, qn, "dw_q_up")
        d_wkv = _tn_matmul(dkvo, kvn, "dw_kv_up")
        weights.attn_grads(l, **dict(zip(ATTN_IN, _unpad_layer(d_win, d_wqu, d_wkv))))
        grads.append(dict(attn_norm=d_g1[0], q_norm=d_gq[0], kv_norm=d_gkv[0], sinks=dsink[:, 0], out_norm_swa=d_ga[0],
                          out_norm_mla=d_gb[0], ffn_norm=d_g2[0]))
    grads = grads[::-1]
    stacked = {k: jnp.stack([g[k] for g in grads]) for k in grads[0]}
    stacked["final_norm"] = d_final[0]
    return loss[0, 0], dh[FRONT + N_META:], dh[FRONT:FRONT + N_META], stacked


MESH = pl.DeviceIdType.MESH
ANY = pl.BlockSpec(memory_space=pl.ANY)


def _place():
    return lax.axis_index("x"), lax.axis_index("y"), lax.axis_index("c")


def _index(x, y, c):
    return 4 * x + 2 * y + c


def _comm_sems(n):
    return [pltpu.SemaphoreType.DMA((n, N_DEV - 1)), pltpu.SemaphoreType.DMA((n, N_DEV - 1)),
            pltpu.SemaphoreType.DMA((n,))]


class _gather_plan:
    def __init__(self, x_refs, out_refs, send_sems, recv_sems, local_sems):
        self.x_refs, self.out_refs = x_refs, out_refs
        self.send_sems, self.recv_sems, self.local_sems = send_sems, recv_sems, local_sems
        self.n = len(x_refs)

    def _where(self):
        x, y, c = _place()
        return (x, y, c), (x, y, 1 - c), [(1 - x, y), (x, 1 - y), (1 - x, 1 - y)], c

    def _copy(self, i, k, block, to, from_input=False):
        slot = self.out_refs[i].at[_index(*block)]
        return pltpu.make_async_remote_copy(
            src_ref=self.x_refs[i] if from_input else slot, dst_ref=slot,
            send_sem=self.send_sems.at[i, k], recv_sem=self.recv_sems.at[i, k], device_id=to, device_id_type=MESH)

    def _mine(self, i, me):
        return pltpu.make_async_copy(self.x_refs[i], self.out_refs[i].at[_index(*me)], self.local_sems.at[i])

    def _first(self, me, sibling, chips, c):
        out = [self._copy(i, 1 + j, me, (*chip, c), True) for j, chip in enumerate(chips) for i in range(self.n)]
        return out + [self._copy(i, 0, me, sibling, True) for i in range(self.n)]

    def start(self):
        me, sibling, chips, c = self._where()
        for i in range(self.n):
            self._mine(i, me).start()
        for cp in self._first(me, sibling, chips, c):
            cp.start()

    def forward(self):
        me, sibling, chips, c = self._where()
        for j, chip in enumerate(chips):
            for i in range(self.n):
                self._copy(i, 1 + j, (*chip, c), me).wait_recv()
                self._copy(i, 4 + j, (*chip, c), sibling).start()

    def finish(self):
        me, sibling, chips, c = self._where()
        for i in range(self.n):
            self._copy(i, 0, sibling, me).wait_recv()
            for j, chip in enumerate(chips):
                self._copy(i, 4 + j, (*chip, 1 - c), me).wait_recv()
        for cp in self._first(me, sibling, chips, c):
            cp.wait_send()
        for j, chip in enumerate(chips):
            for i in range(self.n):
                self._copy(i, 4 + j, (*chip, c), sibling).wait_send()
        for i in range(self.n):
            self._mine(i, me).wait()


class _exchange_plan:
    def __init__(self, in_refs, out_refs, send_sems, recv_sems, local_sems):
        self.in_refs, self.out_refs = in_refs, out_refs
        self.send_sems, self.recv_sems, self.local_sems = send_sems, recv_sems, local_sems
        self.n = len(in_refs)

    def _copies(self):
        x, y, c = _place()
        me = _index(x, y, c)
        mine = [pltpu.make_async_copy(self.in_refs[i].at[me], self.out_refs[i].at[me], self.local_sems.at[i])
                for i in range(self.n)]
        remote = []
        for k in range(1, N_DEV):
            peer = (1 - x if k & 4 else x, 1 - y if k & 2 else y, 1 - c if k & 1 else c)
            remote += [pltpu.make_async_remote_copy(
                src_ref=self.in_refs[i].at[_index(*peer)], dst_ref=self.out_refs[i].at[me],
                send_sem=self.send_sems.at[i, k - 1], recv_sem=self.recv_sems.at[i, k - 1],
                device_id=peer, device_id_type=MESH) for i in range(self.n)]
        return mine, remote

    def start(self):
        mine, remote = self._copies()
        for cp in mine + remote:
            cp.start()

    def finish(self):
        mine, remote = self._copies()
        for cp in remote:
            cp.wait_recv()
        for cp in remote:
            cp.wait_send()
        for cp in mine:
            cp.wait()


def _all_gather(shards, name):
    n = len(shards)

    def body(*refs):
        plan = _gather_plan(refs[:n], refs[n:2 * n], *refs[2 * n:])
        plan.start()
        plan.forward()
        plan.finish()

    return pl.pallas_call(
        body, name=name, in_specs=[ANY] * n, out_specs=[ANY] * n, scratch_shapes=_comm_sems(n),
        out_shape=[jax.ShapeDtypeStruct((N_DEV,) + a.shape, a.dtype) for a in shards],
    )(*shards)


def _exchange(slabs, name):
    n = len(slabs)

    def body(*refs):
        plan = _exchange_plan(refs[:n], refs[n:2 * n], *refs[2 * n:])
        plan.start()
        plan.finish()

    return pl.pallas_call(
        body, name=name, in_specs=[ANY] * n, out_specs=[ANY] * n, scratch_shapes=_comm_sems(n),
        out_shape=[jax.ShapeDtypeStruct(a.shape, a.dtype) for a in slabs],
    )(*slabs)


def _adamw(w, g, m, v):
    m = ADAM_B1 * m + (1.0 - ADAM_B1) * g
    v = ADAM_B2 * v + (1.0 - ADAM_B2) * (g * g)
    m_hat = m / (1.0 - ADAM_B1 ** ADAM_STEP)
    v_hat = v / (1.0 - ADAM_B2 ** ADAM_STEP)
    return -ADAM_LR * (m_hat / (jnp.sqrt(v_hat) + ADAM_EPS) + ADAM_WD * w), m, v


def _sum_slots(ref):
    g = ref[0].astype(F32)
    for s in range(1, N_DEV):
        g = g + ref[s].astype(F32)
    return g


def _reduce_adamw(parts, w, m, v, name):
    l, r, c = w.shape
    tile = max([d for d in range(16, ADAM_ROWS + 1, 16) if r % d == 0], default=r)
    last = r // tile - 1

    def body(*refs):
        p_refs, (w_ref, m_ref, v_ref), (g_ref, d_ref, nm_ref, nv_ref) = refs[:l], refs[l:l + 3], refs[l + 3:]
        for layer in range(l):
            @pl.when(pl.program_id(0) == layer)
            def _(p_ref=p_refs[layer]):
                g = _sum_slots(p_ref)
                g_ref[...] = g
                d_ref[...], nm_ref[...], nv_ref[...] = _adamw(w_ref[...], g, m_ref[...], v_ref[...])

    def part_spec(layer):
        return pl.BlockSpec((N_DEV, tile, c),
                            lambda i, j: (0, jnp.where(i == layer, j, jnp.where(i < layer, 0, last)), 0))

    blk = pl.BlockSpec((None, tile, c), lambda i, j: (i, j, 0))
    return pl.pallas_call(
        body, name=name, grid=(l, r // tile),
        in_specs=[part_spec(layer) for layer in range(l)] + [blk, blk, blk], out_specs=[blk] * 4,
        out_shape=[jax.ShapeDtypeStruct((l, r, c), F32)] * 4,
        compiler_params=_params("arbitrary", "arbitrary"),
    )(*parts, w, m, v)


def _sum_parts(parts, name):
    _, r, c = parts.shape

    def body(p_ref, g_ref):
        g_ref[...] = _sum_slots(p_ref)

    return pl.pallas_call(body, name=name, out_shape=jax.ShapeDtypeStruct((r, c), F32))(parts)


def _adamw_call(w, g, m, v, name):
    def body(w_ref, g_ref, m_ref, v_ref, d_ref, nm_ref, nv_ref):
        d_ref[...], nm_ref[...], nv_ref[...] = _adamw(w_ref[...], g_ref[...], m_ref[...], v_ref[...])

    return pl.pallas_call(body, name=name, out_shape=[jax.ShapeDtypeStruct(w.shape, F32)] * 3)(w, g, m, v)


ATTN_IN = ("w_in", "w_q_up", "w_kv_up")
ATTN = ATTN_IN + ("w_o",)
FFN = ("w_gate", "w_up", "w_down")
TRANSPOSED = ("w_in", "w_q_up", "w_kv_up", "w_gate", "w_up")
SMALL = ("attn_norm", "ffn_norm", "final_norm", "out_norm_swa", "out_norm_mla", "q_norm", "kv_norm", "sinks")
PACK_W = 1024
SMALL_ROWS = 16


def _pack(arrs, dtype):
    flat = jnp.concatenate([a.astype(dtype).reshape(-1) for a in arrs])
    return flat.reshape(-1, PACK_W)


def _unpack(packed, like):
    flat = packed.reshape(-1)
    out, off = [], 0
    for a in like:
        out.append(flat[off:off + a.size].reshape(a.shape))
        off += a.size
    return out


def _gather_to_full(gathered):
    return gathered.reshape((-1,) + gathered.shape[2:])


def _full_to_slabs(full):
    return full.reshape((N_DEV, -1) + full.shape[1:])


class _ShardedWeights:
    def __init__(self, shards, depth, meta_shard):
        self.shards, self.depth = shards, depth
        self.gathered, self.pending, self.parts = {}, {}, {}
        first = _all_gather([shards[n][0] for n in ATTN_IN] + [meta_shard], "gather_attn0")
        self.gathered.update(zip([(n, 0) for n in ATTN_IN], first))
        self.meta = jnp.moveaxis(first[-1], 0, 1).reshape(N_META, D_MODEL)

    def _gather(self, keys, run):
        self.gathered.update(zip(keys, run([self.shards[n][l] for n, l in keys])))

    def _full(self, names, l):
        return tuple(_gather_to_full(self.gathered[n, l]) for n in names)

    def attn_in(self, l):
        return self._full(ATTN_IN, l)

    def w_o(self, l):
        return self._full(("w_o",), l)[0]

    def ffn(self, l):
        return self._full(FFN, l)

    def mla_fwd(self, l, q, k, v):
        keys = [(n, l) for n in ("w_o",) + FFN] + ([(n, l + 1) for n in ATTN_IN] if l + 1 < self.depth else [])
        out = []
        self._gather(keys, lambda xs: out.extend(_mla_fwd(q, k, v, xs)) or out[2])
        return out[0], out[1]

    def _add(self, names, l, grads):
        for n, g in zip(names, grads):
            self.pending[n, l] = _full_to_slabs(g)

    def ffn_grads(self, l, *grads):
        self._add(FFN, l, grads)

    def attn_grads(self, l, **grads):
        self._add(list(grads), l, grads.values())

    def _exchange(self, run):
        keys = list(self.pending)
        self.parts.update(zip(keys, run([self.pending.pop(k) for k in keys])))

    def mla_bwd(self, l, *args):
        out = []
        self._exchange(lambda xs: out.extend(_mla_bwd(*args, xs)) or out[1])
        return out[0]

    def flush(self):
        self._exchange(lambda xs: _exchange(xs, "exchange_attn0"))


def kernel(x, meta_tokens, attn_norm, w_in, q_norm, w_q_up, kv_norm, w_kv_up, sinks, out_norm_swa, out_norm_mla, w_o, ffn_norm, w_gate, w_up, w_down, final_norm, loss_target, m_meta_tokens, m_attn_norm, m_w_in, m_q_norm, m_w_q_up, m_kv_norm, m_w_kv_up, m_sinks, m_out_norm_swa, m_out_norm_mla, m_w_o, m_ffn_norm, m_w_gate, m_w_up, m_w_down, m_final_norm, v_meta_tokens, v_attn_norm, v_w_in, v_q_norm, v_w_q_up, v_kv_norm, v_w_kv_up, v_sinks, v_out_norm_swa, v_out_norm_mla, v_w_o, v_ffn_norm, v_w_gate, v_w_up, v_w_down, v_final_norm):
    w = dict(meta_tokens=meta_tokens, attn_norm=attn_norm, w_in=w_in, q_norm=q_norm, w_q_up=w_q_up, kv_norm=kv_norm,
             w_kv_up=w_kv_up, sinks=sinks, out_norm_swa=out_norm_swa, out_norm_mla=out_norm_mla, w_o=w_o,
             ffn_norm=ffn_norm, w_gate=w_gate, w_up=w_up, w_down=w_down, final_norm=final_norm)
    m = dict(meta_tokens=m_meta_tokens, attn_norm=m_attn_norm, w_in=m_w_in, q_norm=m_q_norm, w_q_up=m_w_q_up,
             kv_norm=m_kv_norm, w_kv_up=m_w_kv_up, sinks=m_sinks, out_norm_swa=m_out_norm_swa,
             out_norm_mla=m_out_norm_mla, w_o=m_w_o, ffn_norm=m_ffn_norm, w_gate=m_w_gate, w_up=m_w_up,
             w_down=m_w_down, final_norm=m_final_norm)
    v = dict(meta_tokens=v_meta_tokens, attn_norm=v_attn_norm, w_in=v_w_in, q_norm=v_q_norm, w_q_up=v_w_q_up,
             kv_norm=v_kv_norm, w_kv_up=v_w_kv_up, sinks=v_sinks, out_norm_swa=v_out_norm_swa,
             out_norm_mla=v_out_norm_mla, w_o=v_w_o, ffn_norm=v_ffn_norm, w_gate=v_w_gate, w_up=v_w_up,
             w_down=v_w_down, final_norm=v_final_norm)
    names = list(w)
    big = ATTN + FFN
    depth = w_in.shape[0]
    me = _index(*_place())

    as_held = lambda n, a: jnp.swapaxes(a, 1, 2) if n in TRANSPOSED else a
    weights = _ShardedWeights({n: as_held(n, w[n]).astype(BF16) for n in big}, depth, meta_tokens)
    loss, grad_x, d_meta, grads = _train_example(x[0], loss_target[0], weights.meta, {n: w[n] for n in SMALL}, weights)
    weights.flush()

    g_big, d_big, m_big, v_big = {}, {}, {}, {}
    for n in big:
        held = [as_held(n, a) for a in (w[n], m[n], v[n])]
        outs = _reduce_adamw([weights.parts[n, l] for l in range(depth)], *held, "reduce_adamw_" + n)
        g_big[n], d_big[n], m_big[n], v_big[n] = [as_held(n, a) for a in outs]

    small = [grads[n] for n in SMALL] + [loss.reshape(1)]
    pad = SMALL_ROWS * PACK_W - sum(a.size for a in small)
    part = jnp.concatenate([_pack(small + [jnp.zeros((pad,), F32)], F32), d_meta], axis=0)
    total = _sum_parts(_all_gather([part], "gather_small")[0], "sum_small")
    small_w = [w[n] for n in SMALL]
    packs = [_pack([d[n] for n in SMALL] + [jnp.zeros((pad + 1,), F32)], F32) for d in (w, m, v)]
    upd = _adamw_call(packs[0], total[:SMALL_ROWS], packs[1], packs[2], "adamw_small")
    g_small, d_small, m_small, v_small = [dict(zip(SMALL, _unpack(p, small_w))) for p in (total[:SMALL_ROWS],) + tuple(upd)]
    loss_total = total[:SMALL_ROWS].reshape(-1)[SMALL_ROWS * PACK_W - pad - 1]
    g_meta = lax.dynamic_slice_in_dim(total[SMALL_ROWS:], me * LANE, LANE, axis=1)
    d_mt, m_mt, v_mt = _adamw_call(meta_tokens, g_meta, m_meta_tokens, v_meta_tokens, "adamw_meta")

    outs = []
    for got in ({**g_big, **g_small, "meta_tokens": g_meta}, {**d_big, **d_small, "meta_tokens": d_mt},
                {**m_big, **m_small, "meta_tokens": m_mt}, {**v_big, **v_small, "meta_tokens": v_mt}):
        outs += [got[n] for n in names]
    return (loss_total, grad_x[None], *outs)
```

```python
import jax
import jax.numpy as jnp
from jax import lax
from jax.experimental import pallas as pl
from jax.experimental.pallas import tpu as pltpu

F32 = jnp.float32
BF16 = jnp.bfloat16

D_MODEL = 1024
N_META = 16
BLOCK = 128
FRONT = (-N_META) % BLOCK
ROPE_THETA = 10000.0
EPS = 1e-6
NEG = -1e30
SWA_HEADS = 8
SWA_KV_HEADS = 2
SWA_GROUP = SWA_HEADS // SWA_KV_HEADS
SWA_HEAD_DIM = 64
MLA_HEADS = 8
MLA_Q_RANK = 256
MLA_KV_RANK = 128
MLA_NOPE_DIM = 64
MLA_ROPE_DIM = 32
MLA_V_DIM = 64
MLA_QK_DIM = MLA_NOPE_DIM + MLA_ROPE_DIM
SWA_Q_W = SWA_HEADS * SWA_HEAD_DIM
SWA_KV_W = SWA_KV_HEADS * SWA_HEAD_DIM
MLA_OUT_W = MLA_HEADS * MLA_V_DIM
SCALE_A = SWA_HEAD_DIM ** -0.5
SCALE_B = MLA_QK_DIM ** -0.5
LOG2E = 1.4426950408889634
Q_SCALE = SCALE_B * LOG2E
ADAM_LR = 0.001
ADAM_B1 = 0.9
ADAM_B2 = 0.999
ADAM_EPS = 1e-08
ADAM_WD = 0.01
ADAM_STEP = 10

LANE = 128
N_DEV = 8
HP = 8 * LANE
PO_QA, PO_KA, PO_VA = 0, HP, HP + 2 * LANE
PO_CQ = PO_VA + 2 * LANE
PO_CKV = PO_CQ + MLA_Q_RANK
PO_KR = PO_CKV + MLA_KV_RANK
PW_IN = PO_KR + LANE
N_TAB = 7
VMEM_LIMIT = 56 * 2 ** 20
TN_VMEM_BUDGET = 36 * 2 ** 20
MLA_HB = 4
MLA_HB_FWD = 8
ADAM_ROWS = 256
HALF = LANE // 2
assert SWA_HEAD_DIM == HALF and MLA_V_DIM == HALF

NT = (((1,), (1,)), ((), ()))
TN = (((0,), (0,)), ((), ()))


def _tile(t):
    return 384 if t % 384 == 0 else 128


def _params(*sem):
    return pltpu.CompilerParams(dimension_semantics=sem, vmem_limit_bytes=VMEM_LIMIT)


def _row(tm, n):
    return pl.BlockSpec((tm, n), lambda i: (i, 0))


def _const(shape):
    return pl.BlockSpec(shape, lambda i: (0,) * len(shape))


def _dot(a, b):
    return jnp.dot(a, b, preferred_element_type=F32)


def _dot_nt(a, b):
    return lax.dot_general(a, b, NT, preferred_element_type=F32)


def _dot_tn(a, b):
    return lax.dot_general(a, b, TN, preferred_element_type=F32)


def _rope(x, c, s1, s2, shift):
    return x * c + pltpu.roll(x, LANE - shift, 1) * s1 + pltpu.roll(x, shift, 1) * s2


def _rope_t(dy, c, s1, s2, shift):
    return dy * c + pltpu.roll(dy * s1, shift, 1) + pltpu.roll(dy * s2, LANE - shift, 1)


def _rms_r(x, n):
    return lax.rsqrt(jnp.sum(x * x, axis=-1, keepdims=True) * (1.0 / n) + EPS)


def _rms_bwd(x, g, dy, n):
    r = _rms_r(x, n)
    xh = x * r
    dxh = dy * g
    dx = r * (dxh - xh * (jnp.sum(dxh * xh, axis=-1, keepdims=True) * (1.0 / n)))
    return dx, jnp.sum(dy * xh, axis=0, keepdims=True)


def _acc(ref, val, first):
    @pl.when(first)
    def _():
        ref[...] = val

    @pl.when(jnp.logical_not(first))
    def _():
        ref[...] += val


def _pack_pair(even, odd):
    return even + pltpu.roll(odd, HALF, 1)


def _pair_half(slab, half):
    return slab if half == 0 else pltpu.roll(slab, HALF, 1)


def _unpack_pair(slab, half):
    x = _pair_half(slab, half)
    return jnp.where(lax.broadcasted_iota(jnp.int32, x.shape, 1) < HALF, x, 0.0)


def _tabs(tab_ref):
    return [tab_ref[:, LANE * i:LANE * (i + 1)] for i in range(N_TAB)]


def _pre_fwd(h, g1, win, gq, wqu, gkv, wkv, tabs):
    t = h.shape[0]
    tm = _tile(t)

    def body(h_ref, g1_ref, win_ref, gq_ref, wqu_ref, gkv_ref, wkv_ref, tab_ref,
             u_ref, qa_ref, ka_ref, va_ref, cq_ref, ckv_ref, qn_ref, kvn_ref, qb_ref, kf_ref, vb_ref):
        ca, sa1, sa2, cb, sb1, sb2, ck = _tabs(tab_ref)
        hv = h_ref[...]
        u = (hv * _rms_r(hv, D_MODEL) * g1_ref[...]).astype(BF16)
        u_ref[...] = u
        p = _dot_nt(u, win_ref[...])
        for c in range(SWA_HEADS):
            sl = slice(LANE * c, LANE * (c + 1))
            qa_ref[:, sl] = _rope(p[:, PO_QA + LANE * c:PO_QA + LANE * (c + 1)], ca, sa1, sa2, 32).astype(BF16)
        for c in range(SWA_KV_HEADS):
            sl = slice(LANE * c, LANE * (c + 1))
            ka_ref[:, sl] = _rope(p[:, PO_KA + LANE * c:PO_KA + LANE * (c + 1)], ca, sa1, sa2, 32).astype(BF16)
        va_ref[...] = p[:, PO_VA:PO_CQ].astype(BF16)
        cq = p[:, PO_CQ:PO_CKV]
        ckv = p[:, PO_CKV:PO_KR]
        cq_ref[...] = cq
        ckv_ref[...] = ckv
        qn = (cq * _rms_r(cq, MLA_Q_RANK) * gq_ref[...]).astype(BF16)
        qn_ref[...] = qn
        qb = _dot_nt(qn, wqu_ref[...])
        kvn = (ckv * _rms_r(ckv, MLA_KV_RANK) * gkv_ref[...]).astype(BF16)
        kvn_ref[...] = kvn
        kv = _dot_nt(kvn, wkv_ref[...])
        kr = _rope(p[:, PO_KR:PW_IN], ck, sb1, sb2, 16)
        for c in range(MLA_HEADS):
            sl = slice(LANE * c, LANE * (c + 1))
            qb_ref[:, sl] = (_rope(qb[:, sl], cb, sb1, sb2, 16) * Q_SCALE).astype(BF16)
            kf_ref[:, sl] = (kv[:, sl] + kr).astype(BF16)
        vb_ref[...] = kv[:, HP:].astype(BF16)

    widths = [(D_MODEL, BF16), (HP, BF16), (2 * LANE, BF16), (2 * LANE, BF16), (MLA_Q_RANK, F32),
              (MLA_KV_RANK, F32), (MLA_Q_RANK, BF16), (MLA_KV_RANK, BF16), (HP, BF16), (HP, BF16), (HP, BF16)]
    return pl.pallas_call(
        body, name="pre_fwd", grid=(t // tm,),
        in_specs=[_row(tm, D_MODEL), _const(g1.shape), _const(win.shape), _const(gq.shape), _const(wqu.shape),
                  _const(gkv.shape), _const(wkv.shape), _row(tm, N_TAB * LANE)],
        out_specs=[_row(tm, w) for w, _ in widths],
        out_shape=[jax.ShapeDtypeStruct((t, w), d) for w, d in widths],
        compiler_params=_params("parallel"),
    )(h, g1, win, gq, wqu, gkv, wkv, tabs)


def _swa_mask(nb):
    key = lax.broadcasted_iota(jnp.int32, (2 * BLOCK, SWA_GROUP * BLOCK), 0)
    qry = lax.broadcasted_iota(jnp.int32, (2 * BLOCK, SWA_GROUP * BLOCK), 1) & (BLOCK - 1)
    return (key > qry) & (key <= qry + BLOCK) & (key + (nb - 1) * BLOCK >= FRONT)


def _swa_group(ref, rows, j):
    return jnp.concatenate([ref[rows, LANE * (SWA_GROUP * j + g):LANE * (SWA_GROUP * j + g + 1)]
                            for g in range(SWA_GROUP)], axis=0)


def _swa_packed_group(ref, rows, j):
    heads = [SWA_GROUP * j + g for g in range(SWA_GROUP)]
    return jnp.concatenate([_pair_half(ref[rows, LANE * (hd // 2):LANE * (hd // 2 + 1)], hd % 2) for hd in heads], axis=0)


def _swa_sinks(sink_ref, j):
    return jnp.concatenate([jnp.full((1, BLOCK), sink_ref[0, SWA_GROUP * j + g], F32) for g in range(SWA_GROUP)], axis=1)


def _swa_keys(prev_ref, cur_ref, rb, j):
    sl = slice(LANE * j, LANE * (j + 1))
    if rb == 0:
        return jnp.concatenate([prev_ref[:, sl], cur_ref[:BLOCK, sl]], axis=0)
    return cur_ref[BLOCK * (rb - 1):BLOCK * (rb + 1), sl]


def _swa_chains(t):
    return [(rb, j) for rb in range(_tile(t) // BLOCK) for j in range(SWA_KV_HEADS)]


def _swa_scores(sink_ref, q_ref, kp_ref, kc_ref, n, t):
    r = _tile(t) // BLOCK
    chains = _swa_chains(t)
    qs = [_swa_group(q_ref, slice(BLOCK * rb, BLOCK * (rb + 1)), j) for rb, j in chains]
    ks = [_swa_keys(kp_ref, kc_ref, rb, j) for rb, j in chains]
    ss = [_dot_nt(k2, q4) for q4, k2 in zip(qs, ks)]
    masks = [_swa_mask(n * r + rb) for rb in range(r)]
    out = []
    for (rb, j), s in zip(chains, ss):
        sink = _swa_sinks(sink_ref, j)
        s = jnp.where(masks[rb], s * SCALE_A, NEG)
        m = jnp.maximum(jnp.max(s, axis=0, keepdims=True), sink)
        e = jnp.exp(s - m)
        es = jnp.exp(sink - m)
        inv = 1.0 / (jnp.sum(e, axis=0, keepdims=True) + es)
        out.append((e * inv, es * inv))
    return qs, ks, out


def _swa_specs(t):
    ts = _tile(t)
    r = ts // BLOCK
    prev = lambda n: (jnp.maximum(n * r - 1, 0), 0)
    cur = lambda n: (n, 0)
    return [pl.BlockSpec(memory_space=pltpu.SMEM), pl.BlockSpec((ts, HP), cur),
            pl.BlockSpec((BLOCK, 2 * LANE), prev), pl.BlockSpec((ts, 2 * LANE), cur),
            pl.BlockSpec((BLOCK, 2 * LANE), prev), pl.BlockSpec((ts, 2 * LANE), cur)]


def _swa_fwd(sinks, q, k, v):
    t = q.shape[0]
    ts = _tile(t)

    def body(sink_ref, q_ref, kp_ref, kc_ref, vp_ref, vc_ref, o_ref):
        chains = _swa_chains(t)
        _, _, probs = _swa_scores(sink_ref, q_ref, kp_ref, kc_ref, pl.program_id(0), t)
        os_ = [_dot_tn(p.astype(BF16), _swa_keys(vp_ref, vc_ref, rb, j)) for (rb, j), (p, _) in zip(chains, probs)]
        for (rb, j), o4 in zip(chains, os_):
            for g in range(0, SWA_GROUP, 2):
                pair = (SWA_GROUP * j + g) // 2
                o_ref[BLOCK * rb:BLOCK * (rb + 1), LANE * pair:LANE * (pair + 1)] = _pack_pair(
                    o4[BLOCK * g:BLOCK * (g + 1)], o4[BLOCK * (g + 1):BLOCK * (g + 2)])

    return pl.pallas_call(
        body, name="swa_fwd", grid=(t // ts,),
        in_specs=_swa_specs(t),
        out_specs=pl.BlockSpec((ts, SWA_Q_W), lambda n: (n, 0)),
        out_shape=jax.ShapeDtypeStruct((t, SWA_Q_W), F32),
        compiler_params=_params("parallel"),
    )(sinks, q, k, k, v, v)


def _causal_mask(q0, k0, tq, tk, transposed):
    if transposed:
        key = k0 + lax.broadcasted_iota(jnp.int32, (tk, tq), 0)
        qry = q0 + lax.broadcasted_iota(jnp.int32, (tk, tq), 1)
    else:
        qry = q0 + lax.broadcasted_iota(jnp.int32, (tq, tk), 0)
        key = k0 + lax.broadcasted_iota(jnp.int32, (tq, tk), 1)
    return (key <= qry) & (key >= FRONT)


def _heads(ref, hb, rows=slice(None)):
    return [ref[rows, LANE * a:LANE * (a + 1)] for a in range(hb)]


def _head_stats(t, hb=MLA_HB):
    return jax.ShapeDtypeStruct((MLA_HEADS // hb, t, hb), F32)


def _mla_fwd(q, k, v, shards=()):
    t = q.shape[0]
    tq = _tile(t)
    nq = t // tq
    n = len(shards)
    hb = MLA_HB_FWD
    steps = (MLA_HEADS // hb) * nq

    def body(q_ref, k_ref, v_ref, *rest):
        x_refs, (o_ref, lse_ref), out_refs = rest[:n], rest[n:n + 2], rest[n + 2:2 * n + 2]
        acc_sc, sems = rest[2 * n + 2], rest[2 * n + 3:]
        i = pl.program_id(1)
        step_id = pl.program_id(0) * nq + i
        if n:
            plan = _gather_plan(x_refs, out_refs, *sems)
            pl.when(step_id == 0)(plan.start)
            pl.when(step_id == (3 * steps) // 4)(plan.forward)
        qs = _heads(q_ref, hb)
        acc_sc[...] = jnp.zeros(acc_sc.shape, F32)

        def step(j, carry, masked):
            rows = pl.ds(pl.multiple_of(j * tq, tq), tq)
            ks = _heads(k_ref, hb, rows)
            vs = [v_ref[rows, LANE * a:LANE * a + HALF] for a in range(hb)]
            ss = [_dot_nt(kh, qh) for qh, kh in zip(qs, ks)]
            if masked:
                mask = _causal_mask(i * tq, j * tq, tq, tq, True)
                ss = [jnp.where(mask, s, NEG) for s in ss]
            mid, out = [], []
            for s, (m, l) in zip(ss, carry):
                mn = jnp.maximum(m, jnp.max(s, axis=0, keepdims=True))
                al = jnp.exp2(m - mn)
                p = jnp.exp2(s - mn)
                out.append((mn, al * l + jnp.sum(p, axis=0, keepdims=True)))
                mid.append((al, p.astype(BF16)))
            for a, ((al, p), vh) in enumerate(zip(mid, vs)):
                acc_sc[a] = al * acc_sc[a] + _dot_tn(vh, p)
            return tuple(out)

        init = ((jnp.full((1, tq), NEG, F32), jnp.zeros((1, tq), F32)),) * hb
        carry = lax.fori_loop(0, jnp.minimum(i, 1) + 1, lambda it, c: step(it * i, c, True), init)
        carry = lax.fori_loop(1, i, lambda j, c: step(j, c, False), carry)
        outs = [acc_sc[a] * (1.0 / l) for a, (_, l) in enumerate(carry)]
        for a in range(0, hb, 2):
            o_ref[:, HALF * a:HALF * (a + 2)] = jnp.concatenate(outs[a:a + 2], axis=0).T
        for a, (m, l) in enumerate(carry):
            lse_ref[:, a:a + 1] = jnp.broadcast_to(m + jnp.log2(l), (LANE, tq)).T[:, :1]
        if n:
            pl.when(step_id == steps - 1)(plan.finish)

    blk = pl.BlockSpec((tq, hb * LANE), lambda h, i: (i, h))
    full = pl.BlockSpec((t, hb * LANE), lambda h, i: (0, h))
    packed = pl.BlockSpec((tq, hb * HALF), lambda h, i: (i, h))
    out = pl.pallas_call(
        body, name="mla_fwd_gather" if n else "mla_fwd", grid=(MLA_HEADS // hb, nq),
        in_specs=[blk, full, full] + [ANY] * n,
        out_specs=[packed, pl.BlockSpec((None, tq, hb), lambda h, i: (h, i, 0))] + [ANY] * n,
        out_shape=[jax.ShapeDtypeStruct((t, MLA_OUT_W), F32), _head_stats(t, hb)]
        + [jax.ShapeDtypeStruct((N_DEV,) + a.shape, a.dtype) for a in shards],
        scratch_shapes=[pltpu.VMEM((hb, HALF, tq), F32)] + (_comm_sems(n) if n else []),
        compiler_params=_params("arbitrary", "arbitrary"),
    )(q, k, v, *shards)
    return out[0], out[1], out[2:]


def _mix_fwd(h, oa, ob, ga, gb, wo, g2):
    t = h.shape[0]
    tm = _tile(t)

    def body(h_ref, oa_ref, ob_ref, ga_ref, gb_ref, wo_ref, g2_ref, h2_ref, mix_ref, u2_ref):
        oa_v = oa_ref[...]
        ob_v = ob_ref[...]
        na = (oa_v * _rms_r(oa_v, SWA_Q_W) * ga_ref[...]).astype(BF16)
        nb = (ob_v * _rms_r(ob_v, MLA_OUT_W) * gb_ref[...]).astype(BF16)
        mix_ref[:, :SWA_Q_W] = na
        mix_ref[:, SWA_Q_W:] = nb
        h2 = h_ref[...] + _dot(na, wo_ref[:SWA_Q_W, :]) + _dot(nb, wo_ref[SWA_Q_W:, :])
        h2_ref[...] = h2
        u2_ref[...] = (h2 * _rms_r(h2, D_MODEL) * g2_ref[...]).astype(BF16)

    mix_w = SWA_Q_W + MLA_OUT_W
    return pl.pallas_call(
        body, name="mix_fwd", grid=(t // tm,),
        in_specs=[_row(tm, D_MODEL), _row(tm, SWA_Q_W), _row(tm, MLA_OUT_W), _const(ga.shape), _const(gb.shape),
                  _const(wo.shape), _const(g2.shape)],
        out_specs=[_row(tm, D_MODEL), _row(tm, mix_w), _row(tm, D_MODEL)],
        out_shape=[jax.ShapeDtypeStruct((t, D_MODEL), F32), jax.ShapeDtypeStruct((t, mix_w), BF16),
                   jax.ShapeDtypeStruct((t, D_MODEL), BF16)],
        compiler_params=_params("parallel"),
    )(h, oa, ob, ga, gb, wo, g2)


def _ffn_fwd(h2, u2, wg_t, wu_t, wd):
    t = h2.shape[0]
    tm = _tile(t)
    dff = wd.shape[0]

    def body(h2_ref, u2_ref, wg_ref, wu_ref, wd_ref, h3_ref, g_ref, up_ref):
        u2v = u2_ref[...]
        g = _dot_nt(u2v, wg_ref[...])
        up = _dot_nt(u2v, wu_ref[...])
        g_ref[...] = g.astype(BF16)
        up_ref[...] = up.astype(BF16)
        a = (g * jax.nn.sigmoid(g) * up).astype(BF16)
        h3_ref[...] = h2_ref[...] + _dot(a, wd_ref[...])

    return pl.pallas_call(
        body, name="ffn_fwd", grid=(t // tm,),
        in_specs=[_row(tm, D_MODEL), _row(tm, D_MODEL), _const(wg_t.shape), _const(wu_t.shape), _const(wd.shape)],
        out_specs=[_row(tm, D_MODEL), _row(tm, dff), _row(tm, dff)],
        out_shape=[jax.ShapeDtypeStruct((t, D_MODEL), F32), jax.ShapeDtypeStruct((t, dff), BF16),
                   jax.ShapeDtypeStruct((t, dff), BF16)],
        compiler_params=_params("parallel"),
    )(h2, u2, wg_t, wu_t, wd)


def _loss_bwd(h, gf, target):
    t = h.shape[0]
    tm = _tile(t)
    first_row = FRONT + N_META

    def body(h_ref, gf_ref, t_ref, dh_ref, dgf_ref, loss_ref):
        i = pl.program_id(0)
        hv = h_ref[...]
        y = hv * _rms_r(hv, D_MODEL) * gf_ref[...]
        row = i * tm + lax.broadcasted_iota(jnp.int32, (tm, 1), 0)
        err = jnp.where(row >= first_row, y - t_ref[...], 0.0)
        dx, dg = _rms_bwd(hv, gf_ref[...], err * (1.0 / D_MODEL), D_MODEL)
        dh_ref[...] = dx
        _acc(dgf_ref, dg, i == 0)
        part = 0.5 * jnp.sum(jnp.sum(err * err, axis=1, keepdims=True) * (1.0 / D_MODEL), axis=0, keepdims=True)
        _acc(loss_ref, jnp.broadcast_to(part, (1, LANE)), i == 0)

    return pl.pallas_call(
        body, name="loss_bwd", grid=(t // tm,),
        in_specs=[_row(tm, D_MODEL), _const(gf.shape), _row(tm, D_MODEL)],
        out_specs=[_row(tm, D_MODEL), _const((1, D_MODEL)), _const((1, LANE))],
        out_shape=[jax.ShapeDtypeStruct((t, D_MODEL), F32), jax.ShapeDtypeStruct((1, D_MODEL), F32),
                   jax.ShapeDtypeStruct((1, LANE), F32)],
        compiler_params=_params("arbitrary"),
    )(h, gf, target)


def _tn_matmul(a, b, name, cols=None):
    t, n = b.shape
    first, k = cols or (0, a.shape[1])
    tk = next(c for c in (k, 1024, 512, 256, 128) if k % c == 0 and first % c == 0 and c <= 1024)
    fits = lambda c: 2 * (t * (tk + c) * 2 + tk * c * 2) <= TN_VMEM_BUDGET
    tn = next(c for c in (n, 1024, 512, 256, 128) if n % c == 0 and fits(c))

    def body(a_ref, b_ref, o_ref):
        o_ref[...] = _dot_tn(a_ref[...], b_ref[...]).astype(BF16)

    return pl.pallas_call(
        body, name=name, grid=(k // tk, n // tn),
        in_specs=[pl.BlockSpec((t, tk), lambda i, j: (0, i + first // tk)), pl.BlockSpec((t, tn), lambda i, j: (0, j))],
        out_specs=pl.BlockSpec((tk, tn), lambda i, j: (i, j)),
        out_shape=jax.ShapeDtypeStruct((k, n), BF16),
        compiler_params=_params("parallel", "parallel"),
    )(a, b)


def _ffn_bwd_a(dh3, g, up, wd):
    t = dh3.shape[0]
    tm = _tile(t)
    dff = wd.shape[0]

    def body(dh3_ref, g_ref, up_ref, wd_ref, a_ref, dgu_ref, dh3b_ref):
        dh3b = dh3_ref[...].astype(BF16)
        dh3b_ref[...] = dh3b
        da = _dot_nt(dh3b, wd_ref[...])
        gv = g_ref[...].astype(F32)
        upv = up_ref[...].astype(F32)
        sg = jax.nn.sigmoid(gv)
        silu = gv * sg
        a_ref[...] = (silu * upv).astype(BF16)
        dgu_ref[:, :dff] = (da * upv * (sg * (1.0 + gv * (1.0 - sg)))).astype(BF16)
        dgu_ref[:, dff:] = (da * silu).astype(BF16)

    return pl.pallas_call(
        body, name="ffn_bwd_a", grid=(t // tm,),
        in_specs=[_row(tm, D_MODEL), _row(tm, dff), _row(tm, dff), _const(wd.shape)],
        out_specs=[_row(tm, dff), _row(tm, 2 * dff), _row(tm, D_MODEL)],
        out_shape=[jax.ShapeDtypeStruct((t, dff), BF16), jax.ShapeDtypeStruct((t, 2 * dff), BF16),
                   jax.ShapeDtypeStruct((t, D_MODEL), BF16)],
        compiler_params=_params("parallel"),
    )(dh3, g, up, wd)


def _ffn_bwd_b(dh3, dgu, h2, g2, wg_t, wu_t):
    t = dh3.shape[0]
    tm = _tile(t)
    dff = wg_t.shape[0]

    def body(dh3_ref, dgu_ref, h2_ref, g2_ref, wg_ref, wu_ref, dh2_ref, dh2b_ref, dg2_ref):
        du2 = _dot(dgu_ref[:, :dff], wg_ref[...]) + _dot(dgu_ref[:, dff:], wu_ref[...])
        dx, dg = _rms_bwd(h2_ref[...], g2_ref[...], du2, D_MODEL)
        dh2 = dh3_ref[...] + dx
        dh2_ref[...] = dh2
        dh2b_ref[...] = dh2.astype(BF16)
        _acc(dg2_ref, dg, pl.program_id(0) == 0)

    return pl.pallas_call(
        body, name="ffn_bwd_b", grid=(t // tm,),
        in_specs=[_row(tm, D_MODEL), _row(tm, 2 * dff), _row(tm, D_MODEL), _const(g2.shape), _const(wg_t.shape),
                  _const(wu_t.shape)],
        out_specs=[_row(tm, D_MODEL), _row(tm, D_MODEL), _const((1, D_MODEL))],
        out_shape=[jax.ShapeDtypeStruct((t, D_MODEL), F32), jax.ShapeDtypeStruct((t, D_MODEL), BF16),
                   jax.ShapeDtypeStruct((1, D_MODEL), F32)],
        compiler_params=_params("arbitrary"),
    )(dh3, dgu, h2, g2, wg_t, wu_t)


def _mix_bwd(dh2, oa, ob, ga, gb, wo):
    t = dh2.shape[0]
    tm = _tile(t)

    def body(dh2_ref, oa_ref, ob_ref, ga_ref, gb_ref, wo_ref, doa_ref, dob_ref, dl_ref, dga_ref, dgb_ref):
        first = pl.program_id(0) == 0
        d = dh2_ref[...]
        ob_v = ob_ref[...]
        dxa, dga = _rms_bwd(oa_ref[...], ga_ref[...], _dot_nt(d, wo_ref[:SWA_Q_W, :]), SWA_Q_W)
        dxb, dgb = _rms_bwd(ob_v, gb_ref[...], _dot_nt(d, wo_ref[SWA_Q_W:, :]), MLA_OUT_W)
        lower = lax.broadcasted_iota(jnp.int32, (tm, LANE), 1) < HALF
        for hd in range(MLA_HEADS):
            sl = slice(LANE * (hd // 2), LANE * (hd // 2 + 1))
            mine = lower if hd % 2 == 0 else jnp.logical_not(lower)
            delta = jnp.sum(jnp.where(mine, ob_v[:, sl] * dxb[:, sl], 0.0), axis=1, keepdims=True)
            dl_ref[hd // MLA_HB, :, hd % MLA_HB:hd % MLA_HB + 1] = delta
        for ref, dx, heads in ((doa_ref, dxa, SWA_HEADS), (dob_ref, dxb, MLA_HEADS)):
            for hd in range(heads):
                slab = dx[:, LANE * (hd // 2):LANE * (hd // 2 + 1)]
                ref[:, LANE * hd:LANE * (hd + 1)] = _unpack_pair(slab, hd % 2).astype(BF16)
        _acc(dga_ref, dga, first)
        _acc(dgb_ref, dgb, first)

    return pl.pallas_call(
        body, name="mix_bwd", grid=(t // tm,),
        in_specs=[_row(tm, D_MODEL), _row(tm, SWA_Q_W), _row(tm, MLA_OUT_W), _const(ga.shape), _const(gb.shape),
                  _const(wo.shape)],
        out_specs=[_row(tm, HP), _row(tm, HP), pl.BlockSpec((MLA_HEADS // MLA_HB, tm, MLA_HB), lambda i: (0, i, 0)),
                   _const((1, SWA_Q_W)), _const((1, MLA_OUT_W))],
        out_shape=[jax.ShapeDtypeStruct((t, HP), BF16), jax.ShapeDtypeStruct((t, HP), BF16), _head_stats(t),
                   jax.ShapeDtypeStruct((1, SWA_Q_W), F32), jax.ShapeDtypeStruct((1, MLA_OUT_W), F32)],
        compiler_params=_params("arbitrary"),
    )(dh2, oa, ob, ga, gb, wo)


def _swa_bwd(sinks, q, k, v, o, do):
    t = q.shape[0]
    ts = _tile(t)

    def body(sink_ref, q_ref, kp_ref, kc_ref, vp_ref, vc_ref, o_ref, do_ref,
             dq_ref, dkc_ref, dkp_ref, dvc_ref, dvp_ref, dsink_ref):
        n = pl.program_id(0)
        chains = _swa_chains(t)
        qs, ks, probs = _swa_scores(sink_ref, q_ref, kp_ref, kc_ref, n, t)
        dos = [_swa_group(do_ref, slice(BLOCK * rb, BLOCK * (rb + 1)), j) for rb, j in chains]
        vs = [_swa_keys(vp_ref, vc_ref, rb, j) for rb, j in chains]
        dps = [_dot_nt(v2, do4) for do4, v2 in zip(dos, vs)]
        dss, dsks = [], []
        for (rb, j), (p, psink), do4, dp in zip(chains, probs, dos, dps):
            o4 = _swa_packed_group(o_ref, slice(BLOCK * rb, BLOCK * (rb + 1)), j)
            delta = jnp.sum(o4 * do4.astype(F32), axis=1, keepdims=True)
            delta = jnp.broadcast_to(delta, (SWA_GROUP * BLOCK, LANE)).T[:1, :]
            dss.append((p * (dp - delta) * SCALE_A).astype(BF16))
            dsks.append(-psink * delta)
        dqs = [_dot_tn(ds, k2) for ds, k2 in zip(dss, ks)]
        dks = [_dot(ds, q4) for ds, q4 in zip(dss, qs)]
        dvs = [_dot(p.astype(BF16), do4) for (p, _), do4 in zip(probs, dos)]
        dsink = [jnp.zeros((1, LANE), F32)] * SWA_HEADS
        ext = {}
        for (rb, j), dq4, dk2, dv2, dsk in zip(chains, dqs, dks, dvs, dsks):
            for g in range(SWA_GROUP):
                hd = SWA_GROUP * j + g
                rows = slice(BLOCK * g, BLOCK * (g + 1))
                dq_ref[BLOCK * rb:BLOCK * (rb + 1), LANE * hd:LANE * (hd + 1)] = dq4[rows].astype(BF16)
                dsink[hd] = dsink[hd] + jnp.sum(dsk[:, rows], axis=1, keepdims=True)
            for half in range(2):
                key = (j, rb + half)
                part = (dk2[BLOCK * half:BLOCK * (half + 1)], dv2[BLOCK * half:BLOCK * (half + 1)])
                ext[key] = part if key not in ext else (ext[key][0] + part[0], ext[key][1] + part[1])
        for (j, blk), (dk, dv) in ext.items():
            sl = slice(LANE * j, LANE * (j + 1))
            if blk == 0:
                dkp_ref[:, sl] = dk
                dvp_ref[:, sl] = dv
            else:
                dkc_ref[BLOCK * (blk - 1):BLOCK * blk, sl] = dk
                dvc_ref[BLOCK * (blk - 1):BLOCK * blk, sl] = dv
        for hd in range(SWA_HEADS):
            _acc(dsink_ref.at[hd:hd + 1, :], jnp.broadcast_to(dsink[hd], (1, LANE)), n == 0)

    cur = lambda n: (n, 0)
    kv = pl.BlockSpec((ts, 2 * LANE), cur)
    kvp = pl.BlockSpec((BLOCK, 2 * LANE), cur)
    hp = pl.BlockSpec((ts, HP), cur)
    kvs = jax.ShapeDtypeStruct((t, 2 * LANE), F32)
    kvps = jax.ShapeDtypeStruct((t // ts * BLOCK, 2 * LANE), F32)
    return pl.pallas_call(
        body, name="swa_bwd", grid=(t // ts,),
        in_specs=_swa_specs(t) + [pl.BlockSpec((ts, SWA_Q_W), cur), hp],
        out_specs=[hp, kv, kvp, kv, kvp, _const((SWA_HEADS, LANE))],
        out_shape=[jax.ShapeDtypeStruct((t, HP), BF16), kvs, kvps, kvs, kvps,
                   jax.ShapeDtypeStruct((SWA_HEADS, LANE), F32)],
        compiler_params=_params("arbitrary"),
    )(sinks, q, k, k, v, v, o, do)


def _mla_bwd(q, k, v, do, lse, dl, slabs=()):
    t = q.shape[0]
    tq = _tile(t)
    nq = t // tq
    n = len(slabs)
    hb = MLA_HB
    steps = (MLA_HEADS // hb) * nq

    def body(k_ref, v_ref, q_ref, do_ref, lse_ref, dl_ref, *rest):
        in_refs, (dq_ref, dk_ref, dv_ref), out_refs = rest[:n], rest[n:n + 3], rest[n + 3:2 * n + 3]
        (dq_sc, dk_sc, dv_sc), sems = rest[2 * n + 3:2 * n + 6], rest[2 * n + 6:]
        j = pl.program_id(1)
        step_id = pl.program_id(0) * nq + j
        if n:
            plan = _exchange_plan(in_refs, out_refs, *sems)
            pl.when(step_id == 0)(plan.start)

        @pl.when(j == 0)
        def _():
            dq_sc[...] = jnp.zeros(dq_sc.shape, F32)

        dk_sc[...] = jnp.zeros(dk_sc.shape, F32)
        dv_sc[...] = jnp.zeros(dv_sc.shape, F32)
        ks, vs = _heads(k_ref, hb), _heads(v_ref, hb)

        def step(i, carry, masked):
            rows = pl.ds(pl.multiple_of(i * tq, tq), tq)
            qs, dos = _heads(q_ref, hb, rows), _heads(do_ref, hb, rows)
            ss = [_dot_nt(qh, kh) for qh, kh in zip(qs, ks)]
            dps = [_dot_nt(doh, vh) for doh, vh in zip(dos, vs)]
            if masked:
                mask = _causal_mask(i * tq, j * tq, tq, tq, False)
                ss = [jnp.where(mask, s_, NEG) for s_ in ss]
            ps = [jnp.exp2(s_ - lse_ref[rows, a:a + 1]) for a, s_ in enumerate(ss)]
            dss = [(p * (dp - dl_ref[rows, a:a + 1])).astype(BF16) for a, (p, dp) in enumerate(zip(ps, dps))]
            for a, (ds, p, qh, kh, doh) in enumerate(zip(dss, ps, qs, ks, dos)):
                dq_sc[a, rows, :] += _dot(ds, kh)
                dk_sc[a, :MLA_QK_DIM, :] += _dot_tn(qh[:, :MLA_QK_DIM], ds)
                dv_sc[a, :MLA_V_DIM, :] += _dot_tn(doh[:, :MLA_V_DIM], p.astype(BF16))
            return carry

        split = jnp.where(j == 0, nq, j + 1)
        lax.fori_loop(j, split, lambda i, c: step(i, c, True), 0)
        lax.fori_loop(split, nq, lambda i, c: step(i, c, False), 0)
        for a in range(hb):
            dk_ref[:, LANE * a:LANE * (a + 1)] = (dk_sc[a] * (1.0 / LOG2E)).T.astype(BF16)
            dv_ref[:, LANE * a:LANE * (a + 1)] = dv_sc[a].T.astype(BF16)

        @pl.when(j == nq - 1)
        def _():
            for a in range(hb):
                dq_ref[:, LANE * a:LANE * (a + 1)] = (dq_sc[a] * SCALE_B).astype(BF16)

        if n:
            pl.when(step_id == steps - 1)(plan.finish)

    blk = pl.BlockSpec((tq, hb * LANE), lambda h, j: (j, h))
    full = pl.BlockSpec((t, hb * LANE), lambda h, j: (0, h))
    cols = pl.BlockSpec((None, t, hb), lambda h, j: (h, 0, 0))
    out = pl.pallas_call(
        body, name="mla_bwd_exchange" if n else "mla_bwd", grid=(MLA_HEADS // hb, nq),
        in_specs=[blk, blk, full, full, cols, cols] + [ANY] * n, out_specs=[full, blk, blk] + [ANY] * n,
        out_shape=[jax.ShapeDtypeStruct((t, HP), BF16)] * 3 + [jax.ShapeDtypeStruct(a.shape, a.dtype) for a in slabs],
        scratch_shapes=[pltpu.VMEM((hb, t, LANE), F32)] + [pltpu.VMEM((hb, LANE, tq), F32)] * 2
        + (_comm_sems(n) if n else []),
        compiler_params=_params("arbitrary", "arbitrary"),
    )(k, v, q, do, lse, dl, *slabs)
    return out[:3], out[3:]


def _pre_bwd(dh2, h, cq, ckv, dqa, dka, dka_next, dva, dva_next, dqb, dkf, dvb, g1, win, gq, wqu, gkv, wkv, tabs):
    t = h.shape[0]
    tm = _tile(t)

    def body(dh2_ref, h_ref, cq_ref, ckv_ref, dqa_ref, dka_ref, dkan_ref, dva_ref, dvan_ref, dqb_ref, dkf_ref, dvb_ref,
             g1_ref, win_ref, gq_ref, wqu_ref, gkv_ref, wkv_ref, tab_ref,
             dh_ref, dp_ref, dqbo_ref, dkvo_ref, dg1_ref, dgq_ref, dgkv_ref):
        first = pl.program_id(0) == 0
        ca, sa1, sa2, cb, sb1, sb2, ck = _tabs(tab_ref)
        dkr = jnp.zeros((tm, LANE), F32)
        for c in range(MLA_HEADS):
            sl = slice(LANE * c, LANE * (c + 1))
            dqbo_ref[:, sl] = _rope_t(dqb_ref[:, sl].astype(F32), cb, sb1, sb2, 16).astype(BF16)
            dkr += dkf_ref[:, sl].astype(F32)
        dkvo_ref[:, :HP] = dkf_ref[...]
        dkvo_ref[:, HP:] = dvb_ref[...]
        dcq, dgq = _rms_bwd(cq_ref[...], gq_ref[...], _dot(dqbo_ref[...], wqu_ref[...]), MLA_Q_RANK)
        dckv, dgkv = _rms_bwd(ckv_ref[...], gkv_ref[...], _dot(dkvo_ref[...], wkv_ref[...]), MLA_KV_RANK)
        for c in range(SWA_HEADS):
            sl = slice(LANE * c, LANE * (c + 1))
            dp_ref[:, PO_QA + LANE * c:PO_QA + LANE * (c + 1)] = _rope_t(dqa_ref[:, sl].astype(F32), ca, sa1, sa2,
                                                                          32).astype(BF16)
        last = slice(tm - BLOCK, tm)
        more = pl.program_id(0) < t // tm - 1
        for c in range(SWA_KV_HEADS):
            sl = slice(LANE * c, LANE * (c + 1))
            dk = dka_ref[:, sl]
            dk_last = dk[tm - BLOCK:] + jnp.where(more, dkan_ref[:, sl], 0.0)
            cols = slice(PO_KA + LANE * c, PO_KA + LANE * (c + 1))
            if tm > BLOCK:
                dp_ref[:tm - BLOCK, cols] = _rope_t(dk[:tm - BLOCK], ca[:tm - BLOCK], sa1[:tm - BLOCK], sa2[:tm - BLOCK],
                                                    32).astype(BF16)
            dp_ref[last, cols] = _rope_t(dk_last, ca[tm - BLOCK:], sa1[tm - BLOCK:], sa2[tm - BLOCK:], 32).astype(BF16)
        if tm > BLOCK:
            dp_ref[:tm - BLOCK, PO_VA:PO_CQ] = dva_ref[:tm - BLOCK, :].astype(BF16)
        dp_ref[last, PO_VA:PO_CQ] = (dva_ref[tm - BLOCK:, :] + jnp.where(more, dvan_ref[...], 0.0)).astype(BF16)
        dp_ref[:, PO_CQ:PO_CKV] = dcq.astype(BF16)
        dp_ref[:, PO_CKV:PO_KR] = dckv.astype(BF16)
        dp_ref[:, PO_KR:PW_IN] = _rope_t(dkr, ck, sb1, sb2, 16).astype(BF16)
        dx, dg1 = _rms_bwd(h_ref[...], g1_ref[...], _dot(dp_ref[...], win_ref[...]), D_MODEL)
        dh_ref[...] = dh2_ref[...] + dx
        _acc(dg1_ref, dg1, first)
        _acc(dgq_ref, dgq, first)
        _acc(dgkv_ref, dgkv, first)

    kv = _row(tm, 2 * LANE)
    nxt = pl.BlockSpec((BLOCK, 2 * LANE), lambda i: (jnp.minimum(i + 1, t // tm - 1), 0))
    return pl.pallas_call(
        body, name="pre_bwd", grid=(t // tm,),
        in_specs=[_row(tm, D_MODEL), _row(tm, D_MODEL), _row(tm, MLA_Q_RANK), _row(tm, MLA_KV_RANK), _row(tm, HP),
                  kv, nxt, kv, nxt, _row(tm, HP), _row(tm, HP), _row(tm, HP),
                  _const(g1.shape), _const(win.shape), _const(gq.shape), _const(wqu.shape), _const(gkv.shape),
                  _const(wkv.shape), _row(tm, N_TAB * LANE)],
        out_specs=[_row(tm, D_MODEL), _row(tm, PW_IN), _row(tm, HP), _row(tm, 2 * HP),
                   _const((1, D_MODEL)), _const((1, MLA_Q_RANK)), _const((1, MLA_KV_RANK))],
        out_shape=[jax.ShapeDtypeStruct((t, D_MODEL), F32), jax.ShapeDtypeStruct((t, PW_IN), BF16),
                   jax.ShapeDtypeStruct((t, HP), BF16), jax.ShapeDtypeStruct((t, 2 * HP), BF16),
                   jax.ShapeDtypeStruct((1, D_MODEL), F32), jax.ShapeDtypeStruct((1, MLA_Q_RANK), F32),
                   jax.ShapeDtypeStruct((1, MLA_KV_RANK), F32)],
        compiler_params=_params("arbitrary"),
    )(dh2, h, cq, ckv, dqa, dka, dka_next, dva, dva_next, dqb, dkf, dvb, g1, win, gq, wqu, gkv, wkv, tabs)


def _rope_tables(t):
    pos = (jnp.arange(t, dtype=jnp.int32) - FRONT).astype(F32)[:, None]
    lane = jnp.arange(LANE)[None, :]

    def table(dim, start):
        half = dim // 2
        inv = ROPE_THETA ** (-jnp.arange(0, dim, 2, dtype=F32) / dim)
        ang = pos * inv[None, :]
        cos = jnp.concatenate([jnp.cos(ang)] * 2, axis=1)
        sin = jnp.concatenate([jnp.sin(ang)] * 2, axis=1)
        pad = lambda a: jnp.pad(a, ((0, 0), (start, LANE - start - dim)))
        first = (lane >= start) & (lane < start + half)
        second = (lane >= start + half) & (lane < start + dim)
        return pad(cos), jnp.where(first, -pad(sin), 0.0), jnp.where(second, pad(sin), 0.0)

    ca, sa1, sa2 = table(SWA_HEAD_DIM, 0)
    ck, sb1, sb2 = table(MLA_ROPE_DIM, MLA_NOPE_DIM)
    cb = jnp.where(lane < MLA_NOPE_DIM, 1.0, ck)
    return jnp.concatenate([ca, sa1, sa2, cb, sb1, sb2, ck], axis=1)


def _pad_heads(w, heads, dim, axis):
    shp = w.shape
    w = w.reshape(shp[:axis] + (heads, dim) + shp[axis + 1:])
    pad = [(0, 0)] * w.ndim
    pad[axis + 1] = (0, LANE - dim)
    return jnp.pad(w, pad).reshape(shp[:axis] + (heads * LANE,) + shp[axis + 1:])


def _unpad_heads(w, heads, dim, axis):
    shp = w.shape
    w = w.reshape(shp[:axis] + (heads, LANE) + shp[axis + 1:])
    w = lax.slice_in_dim(w, 0, dim, axis=axis + 1)
    return w.reshape(shp[:axis] + (heads * dim,) + shp[axis + 1:])


def _pad_layer(w_in, w_q_up, w_kv_up):
    o1 = SWA_Q_W
    o2 = o1 + SWA_KV_W
    o3 = o2 + SWA_KV_W
    o4 = o3 + MLA_Q_RANK
    o5 = o4 + MLA_KV_RANK
    kr = jnp.pad(w_in[o5:], ((MLA_NOPE_DIM, LANE - MLA_QK_DIM), (0, 0)))
    win = jnp.concatenate([
        _pad_heads(w_in[:o1], SWA_HEADS, SWA_HEAD_DIM, 0),
        _pad_heads(w_in[o1:o2], SWA_KV_HEADS, SWA_HEAD_DIM, 0),
        _pad_heads(w_in[o2:o3], SWA_KV_HEADS, SWA_HEAD_DIM, 0),
        w_in[o3:o5], kr], axis=0)
    wqu = _pad_heads(w_q_up, MLA_HEADS, MLA_QK_DIM, 0)
    kv = w_kv_up.reshape(MLA_HEADS, MLA_NOPE_DIM + MLA_V_DIM, MLA_KV_RANK)
    wkv = jnp.concatenate([
        _pad_heads(kv[:, :MLA_NOPE_DIM].reshape(-1, MLA_KV_RANK), MLA_HEADS, MLA_NOPE_DIM, 0),
        _pad_heads(kv[:, MLA_NOPE_DIM:].reshape(-1, MLA_KV_RANK), MLA_HEADS, MLA_V_DIM, 0)], axis=0)
    return win, wqu, wkv


def _unpad_layer(dwin, dwqu, dwkv):
    d_w_in = jnp.concatenate([
        _unpad_heads(dwin[PO_QA:PO_KA], SWA_HEADS, SWA_HEAD_DIM, 0),
        _unpad_heads(dwin[PO_KA:PO_VA], SWA_KV_HEADS, SWA_HEAD_DIM, 0),
        _unpad_heads(dwin[PO_VA:PO_CQ], SWA_KV_HEADS, SWA_HEAD_DIM, 0),
        dwin[PO_CQ:PO_KR], dwin[PO_KR + MLA_NOPE_DIM:PO_KR + MLA_QK_DIM]], axis=0)
    d_w_q_up = _unpad_heads(dwqu, MLA_HEADS, MLA_QK_DIM, 0)
    dk = _unpad_heads(dwkv[:HP], MLA_HEADS, MLA_NOPE_DIM, 0).reshape(MLA_HEADS, MLA_NOPE_DIM, MLA_KV_RANK)
    dv = _unpad_heads(dwkv[HP:], MLA_HEADS, MLA_V_DIM, 0).reshape(MLA_HEADS, MLA_V_DIM, MLA_KV_RANK)
    d_w_kv_up = jnp.concatenate([dk, dv], axis=1).reshape(-1, MLA_KV_RANK)
    return d_w_in, d_w_q_up, d_w_kv_up


def _train_example(x, target, meta, vec, weights):
    s = x.shape[0]
    depth = vec["attn_norm"].shape[0]
    t = FRONT + N_META + s
    assert t % BLOCK == 0
    tabs = _rope_tables(t)
    h = jnp.concatenate([jnp.zeros((FRONT, D_MODEL), F32), meta, x], axis=0)
    tgt = jnp.concatenate([jnp.zeros((FRONT + N_META, D_MODEL), F32), target], axis=0)
    row = lambda v: v[None, :]

    saved = []
    for l in range(depth):
        win, wqu, wkv = _pad_layer(*weights.attn_in(l))
        g1, gq, gkv, g2, ga, gb = (row(vec[n][l]) for n in ("attn_norm", "q_norm", "kv_norm", "ffn_norm",
                                                            "out_norm_swa", "out_norm_mla"))
        sk = row(vec["sinks"][l])
        u, qa, ka, va, cq, ckv, qn, kvn, qb, kf, vb = _pre_fwd(h, g1, win, gq, wqu, gkv, wkv, tabs)
        oa = _swa_fwd(sk, qa, ka, va)
        ob, lse = weights.mla_fwd(l, qb, kf, vb)
        lse = jnp.moveaxis(lse.reshape(t, MLA_HEADS // MLA_HB, MLA_HB), 1, 0)
        wo = weights.w_o(l)
        h2, mix, u2 = _mix_fwd(h, oa, ob, ga, gb, wo, g2)
        wg, wu, wd = weights.ffn(l)
        h3, gt, up = _ffn_fwd(h2, u2, wg, wu, wd)
        saved.append((h, u, qa, ka, va, cq, ckv, qn, kvn, qb, kf, vb, oa, ob, lse, h2, mix, u2, gt, up,
                      win, wqu, wkv, wo, ga, gb, g1, gq, gkv, g2, sk, wg, wu, wd))
        h = h3

    dh, d_final, loss = _loss_bwd(h, row(vec["final_norm"]), tgt)

    grads = []
    for l in reversed(range(depth)):
        (h0, u, qa, ka, va, cq, ckv, qn, kvn, qb, kf, vb, oa, ob, lse, h2, mix, u2, gt, up,
         win, wqu, wkv, wo, ga, gb, g1, gq, gkv, g2, sk, wg, wu, wd) = saved[l]
        dff = wd.shape[0]
        act, dgu, dhb = _ffn_bwd_a(dh, gt, up, wd)
        weights.ffn_grads(l, _tn_matmul(dgu, u2, "dw_gate", (0, dff)), _tn_matmul(dgu, u2, "dw_up", (dff, dff)),
                          _tn_matmul(act, dhb, "dw_down"))
        dh2, dh2b, d_g2 = _ffn_bwd_b(dh, dgu, h2, g2, wg, wu)
        weights.attn_grads(l, w_o=_tn_matmul(mix, dh2b, "dw_o"))
        doa, dob, dl, d_ga, d_gb = _mix_bwd(dh2b, oa, ob, ga, gb, wo)
        dqa, dkc, dkp, dvc, dvp, dsink = _swa_bwd(sk, qa, ka, va, oa, doa)
        dqb, dkf, dvb = weights.mla_bwd(l, qb, kf, vb, dob, lse, dl)
        dh, dp, dqbo, dkvo, d_g1, d_gq, d_gkv = _pre_bwd(
            dh2, h0, cq, ckv, dqa, dkc, dkp, dvc, dvp, dqb, dkf, dvb,
            g1, win, gq, wqu, gkv, wkv, tabs)
        d_win = _tn_matmul(dp, u, "dw_in")
        d_wqu = _tn_matmul(dqbo, qn, "dw_q_up")
        d_wkv = _tn_matmul(dkvo, kvn, "dw_kv_up")
        weights.attn_grads(l, **dict(zip(ATTN_IN, _unpad_layer(d_win, d_wqu, d_wkv))))
        grads.append(dict(attn_norm=d_g1[0], q_norm=d_gq[0], kv_norm=d_gkv[0], sinks=dsink[:, 0], out_norm_swa=d_ga[0],
                          out_norm_mla=d_gb[0], ffn_norm=d_g2[0]))
    grads = grads[::-1]
    stacked = {k: jnp.stack([g[k] for g in grads]) for k in grads[0]}
    stacked["final_norm"] = d_final[0]
    return loss[0, 0], dh[FRONT + N_META:], dh[FRONT:FRONT + N_META], stacked


MESH = pl.DeviceIdType.MESH
ANY = pl.BlockSpec(memory_space=pl.ANY)


def _place():
    return lax.axis_index("x"), lax.axis_index("y"), lax.axis_index("c")


def _index(x, y, c):
    return 4 * x + 2 * y + c


def _comm_sems(n):
    return [pltpu.SemaphoreType.DMA((n, N_DEV - 1)), pltpu.SemaphoreType.DMA((n, N_DEV - 1)),
            pltpu.SemaphoreType.DMA((n,))]


class _gather_plan:
    def __init__(self, x_refs, out_refs, send_sems, recv_sems, local_sems):
        self.x_refs, self.out_refs = x_refs, out_refs
        self.send_sems, self.recv_sems, self.local_sems = send_sems, recv_sems, local_sems
        self.n = len(x_refs)

    def _where(self):
        x, y, c = _place()
        return (x, y, c), (x, y, 1 - c), [(1 - x, y), (x, 1 - y), (1 - x, 1 - y)], c

    def _copy(self, i, k, block, to, from_input=False):
        slot = self.out_refs[i].at[_index(*block)]
        return pltpu.make_async_remote_copy(
            src_ref=self.x_refs[i] if from_input else slot, dst_ref=slot,
            send_sem=self.send_sems.at[i, k], recv_sem=self.recv_sems.at[i, k], device_id=to, device_id_type=MESH)

    def _mine(self, i, me):
        return pltpu.make_async_copy(self.x_refs[i], self.out_refs[i].at[_index(*me)], self.local_sems.at[i])

    def _first(self, me, sibling, chips, c):
        out = [self._copy(i, 1 + j, me, (*chip, c), True) for j, chip in enumerate(chips) for i in range(self.n)]
        return out + [self._copy(i, 0, me, sibling, True) for i in range(self.n)]

    def start(self):
        me, sibling, chips, c = self._where()
        for i in range(self.n):
            self._mine(i, me).start()
        for cp in self._first(me, sibling, chips, c):
            cp.start()

    def forward(self):
        me, sibling, chips, c = self._where()
        for j, chip in enumerate(chips):
            for i in range(self.n):
                self._copy(i, 1 + j, (*chip, c), me).wait_recv()
                self._copy(i, 4 + j, (*chip, c), sibling).start()

    def finish(self):
        me, sibling, chips, c = self._where()
        for i in range(self.n):
            self._copy(i, 0, sibling, me).wait_recv()
            for j, chip in enumerate(chips):
                self._copy(i, 4 + j, (*chip, 1 - c), me).wait_recv()
        for cp in self._first(me, sibling, chips, c):
            cp.wait_send()
        for j, chip in enumerate(chips):
            for i in range(self.n):
                self._copy(i, 4 + j, (*chip, c), sibling).wait_send()
        for i in range(self.n):
            self._mine(i, me).wait()


class _exchange_plan:
    def __init__(self, in_refs, out_refs, send_sems, recv_sems, local_sems):
        self.in_refs, self.out_refs = in_refs, out_refs
        self.send_sems, self.recv_sems, self.local_sems = send_sems, recv_sems, local_sems
        self.n = len(in_refs)

    def _copies(self):
        x, y, c = _place()
        me = _index(x, y, c)
        mine = [pltpu.make_async_copy(self.in_refs[i].at[me], self.out_refs[i].at[me], self.local_sems.at[i])
                for i in range(self.n)]
        remote = []
        for k in range(1, N_DEV):
            peer = (1 - x if k & 4 else x, 1 - y if k & 2 else y, 1 - c if k & 1 else c)
            remote += [pltpu.make_async_remote_copy(
                src_ref=self.in_refs[i].at[_index(*peer)], dst_ref=self.out_refs[i].at[me],
                send_sem=self.send_sems.at[i, k - 1], recv_sem=self.recv_sems.at[i, k - 1],
                device_id=peer, device_id_type=MESH) for i in range(self.n)]
        return mine, remote

    def start(self):
        mine, remote = self._copies()
        for cp in mine + remote:
            cp.start()

    def finish(self):
        mine, remote = self._copies()
        for cp in remote:
            cp.wait_recv()
        for cp in remote:
            cp.wait_send()
        for cp in mine:
            cp.wait()


def _all_gather(shards, name):
    n = len(shards)

    def body(*refs):
        plan = _gather_plan(refs[:n], refs[n:2 * n], *refs[2 * n:])
        plan.start()
        plan.forward()
        plan.finish()

    return pl.pallas_call(
        body, name=name, in_specs=[ANY] * n, out_specs=[ANY] * n, scratch_shapes=_comm_sems(n),
        out_shape=[jax.ShapeDtypeStruct((N_DEV,) + a.shape, a.dtype) for a in shards],
    )(*shards)


def _exchange(slabs, name):
    n = len(slabs)

    def body(*refs):
        plan = _exchange_plan(refs[:n], refs[n:2 * n], *refs[2 * n:])
        plan.start()
        plan.finish()

    return pl.pallas_call(
        body, name=name, in_specs=[ANY] * n, out_specs=[ANY] * n, scratch_shapes=_comm_sems(n),
        out_shape=[jax.ShapeDtypeStruct(a.shape, a.dtype) for a in slabs],
    )(*slabs)


def _adamw(w, g, m, v):
    m = ADAM_B1 * m + (1.0 - ADAM_B1) * g
    v = ADAM_B2 * v + (1.0 - ADAM_B2) * (g * g)
    m_hat = m / (1.0 - ADAM_B1 ** ADAM_STEP)
    v_hat = v / (1.0 - ADAM_B2 ** ADAM_STEP)
    return -ADAM_LR * (m_hat / (jnp.sqrt(v_hat) + ADAM_EPS) + ADAM_WD * w), m, v


def _sum_slots(ref):
    g = ref[0].astype(F32)
    for s in range(1, N_DEV):
        g = g + ref[s].astype(F32)
    return g


def _reduce_adamw(parts, w, m, v, name):
    l, r, c = w.shape
    tile = max([d for d in range(16, ADAM_ROWS + 1, 16) if r % d == 0], default=r)
    last = r // tile - 1

    def body(*refs):
        p_refs, (w_ref, m_ref, v_ref), (g_ref, d_ref, nm_ref, nv_ref) = refs[:l], refs[l:l + 3], refs[l + 3:]
        for layer in range(l):
            @pl.when(pl.program_id(0) == layer)
            def _(p_ref=p_refs[layer]):
                g = _sum_slots(p_ref)
                g_ref[...] = g
                d_ref[...], nm_ref[...], nv_ref[...] = _adamw(w_ref[...], g, m_ref[...], v_ref[...])

    def part_spec(layer):
        return pl.BlockSpec((N_DEV, tile, c),
                            lambda i, j: (0, jnp.where(i == layer, j, jnp.where(i < layer, 0, last)), 0))

    blk = pl.BlockSpec((None, tile, c), lambda i, j: (i, j, 0))
    return pl.pallas_call(
        body, name=name, grid=(l, r // tile),
        in_specs=[part_spec(layer) for layer in range(l)] + [blk, blk, blk], out_specs=[blk] * 4,
        out_shape=[jax.ShapeDtypeStruct((l, r, c), F32)] * 4,
        compiler_params=_params("arbitrary", "arbitrary"),
    )(*parts, w, m, v)


def _sum_parts(parts, name):
    _, r, c = parts.shape

    def body(p_ref, g_ref):
        g_ref[...] = _sum_slots(p_ref)

    return pl.pallas_call(body, name=name, out_shape=jax.ShapeDtypeStruct((r, c), F32))(parts)


def _adamw_call(w, g, m, v, name):
    def body(w_ref, g_ref, m_ref, v_ref, d_ref, nm_ref, nv_ref):
        d_ref[...], nm_ref[...], nv_ref[...] = _adamw(w_ref[...], g_ref[...], m_ref[...], v_ref[...])

    return pl.pallas_call(body, name=name, out_shape=[jax.ShapeDtypeStruct(w.shape, F32)] * 3)(w, g, m, v)


ATTN_IN = ("w_in", "w_q_up", "w_kv_up")
ATTN = ATTN_IN + ("w_o",)
FFN = ("w_gate", "w_up", "w_down")
TRANSPOSED = ("w_in", "w_q_up", "w_kv_up", "w_gate", "w_up")
SMALL = ("attn_norm", "ffn_norm", "final_norm", "out_norm_swa", "out_norm_mla", "q_norm", "kv_norm", "sinks")
PACK_W = 1024
SMALL_ROWS = 16


def _pack(arrs, dtype):
    flat = jnp.concatenate([a.astype(dtype).reshape(-1) for a in arrs])
    return flat.reshape(-1, PACK_W)


def _unpack(packed, like):
    flat = packed.reshape(-1)
    out, off = [], 0
    for a in like:
        out.append(flat[off:off + a.size].reshape(a.shape))
        off += a.size
    return out


def _gather_to_full(gathered):
    return gathered.reshape((-1,) + gathered.shape[2:])


def _full_to_slabs(full):
    return full.reshape((N_DEV, -1) + full.shape[1:])


class _ShardedWeights:
    def __init__(self, shards, depth, meta_shard):
        self.shards, self.depth = shards, depth
        self.gathered, self.pending, self.parts = {}, {}, {}
        first = _all_gather([shards[n][0] for n in ATTN_IN] + [meta_shard], "gather_attn0")
        self.gathered.update(zip([(n, 0) for n in ATTN_IN], first))
        self.meta = jnp.moveaxis(first[-1], 0, 1).reshape(N_META, D_MODEL)

    def _gather(self, keys, run):
        self.gathered.update(zip(keys, run([self.shards[n][l] for n, l in keys])))

    def _full(self, names, l):
        return tuple(_gather_to_full(self.gathered[n, l]) for n in names)

    def attn_in(self, l):
        return self._full(ATTN_IN, l)

    def w_o(self, l):
        return self._full(("w_o",), l)[0]

    def ffn(self, l):
        return self._full(FFN, l)

    def mla_fwd(self, l, q, k, v):
        keys = [(n, l) for n in ("w_o",) + FFN] + ([(n, l + 1) for n in ATTN_IN] if l + 1 < self.depth else [])
        out = []
        self._gather(keys, lambda xs: out.extend(_mla_fwd(q, k, v, xs)) or out[2])
        return out[0], out[1]

    def _add(self, names, l, grads):
        for n, g in zip(names, grads):
            self.pending[n, l] = _full_to_slabs(g)

    def ffn_grads(self, l, *grads):
        self._add(FFN, l, grads)

    def attn_grads(self, l, **grads):
        self._add(list(grads), l, grads.values())

    def _exchange(self, run):
        keys = list(self.pending)
        self.parts.update(zip(keys, run([self.pending.pop(k) for k in keys])))

    def mla_bwd(self, l, *args):
        out = []
        self._exchange(lambda xs: out.extend(_mla_bwd(*args, xs)) or out[1])
        return out[0]

    def flush(self):
        self._exchange(lambda xs: _exchange(xs, "exchange_attn0"))


def kernel(x, meta_tokens, attn_norm, w_in, q_norm, w_q_up, kv_norm, w_kv_up, sinks, out_norm_swa, out_norm_mla, w_o, ffn_norm, w_gate, w_up, w_down, final_norm, loss_target, m_meta_tokens, m_attn_norm, m_w_in, m_q_norm, m_w_q_up, m_kv_norm, m_w_kv_up, m_sinks, m_out_norm_swa, m_out_norm_mla, m_w_o, m_ffn_norm, m_w_gate, m_w_up, m_w_down, m_final_norm, v_meta_tokens, v_attn_norm, v_w_in, v_q_norm, v_w_q_up, v_kv_norm, v_w_kv_up, v_sinks, v_out_norm_swa, v_out_norm_mla, v_w_o, v_ffn_norm, v_w_gate, v_w_up, v_w_down, v_final_norm):
    w = dict(meta_tokens=meta_tokens, attn_norm=attn_norm, w_in=w_in, q_norm=q_norm, w_q_up=w_q_up, kv_norm=kv_norm,
             w_kv_up=w_kv_up, sinks=sinks, out_norm_swa=out_norm_swa, out_norm_mla=out_norm_mla, w_o=w_o,
             ffn_norm=ffn_norm, w_gate=w_gate, w_up=w_up, w_down=w_down, final_norm=final_norm)
    m = dict(meta_tokens=m_meta_tokens, attn_norm=m_attn_norm, w_in=m_w_in, q_norm=m_q_norm, w_q_up=m_w_q_up,
             kv_norm=m_kv_norm, w_kv_up=m_w_kv_up, sinks=m_sinks, out_norm_swa=m_out_norm_swa,
             out_norm_mla=m_out_norm_mla, w_o=m_w_o, ffn_norm=m_ffn_norm, w_gate=m_w_gate, w_up=m_w_up,
             w_down=m_w_down, final_norm=m_final_norm)
    v = dict(meta_tokens=v_meta_tokens, attn_norm=v_attn_norm, w_in=v_w_in, q_norm=v_q_norm, w_q_up=v_w_q_up,
             kv_norm=v_kv_norm, w_kv_up=v_w_kv_up, sinks=v_sinks, out_norm_swa=v_out_norm_swa,
             out_norm_mla=v_out_norm_mla, w_o=v_w_o, ffn_norm=v_ffn_norm, w_gate=v_w_gate, w_up=v_w_up,
             w_down=v_w_down, final_norm=v_final_norm)
    names = list(w)
    big = ATTN + FFN
    depth = w_in.shape[0]
    me = _index(*_place())

    as_held = lambda n, a: jnp.swapaxes(a, 1, 2) if n in TRANSPOSED else a
    weights = _ShardedWeights({n: as_held(n, w[n]).astype(BF16) for n in big}, depth, meta_tokens)
    loss, grad_x, d_meta, grads = _train_example(x[0], loss_target[0], weights.meta, {n: w[n] for n in SMALL}, weights)
    weights.flush()

    g_big, d_big, m_big, v_big = {}, {}, {}, {}
    for n in big:
        held = [as_held(n, a) for a in (w[n], m[n], v[n])]
        outs = _reduce_adamw([weights.parts[n, l] for l in range(depth)], *held, "reduce_adamw_" + n)
        g_big[n], d_big[n], m_big[n], v_big[n] = [as_held(n, a) for a in outs]

    small = [grads[n] for n in SMALL] + [loss.reshape(1)]
    pad = SMALL_ROWS * PACK_W - sum(a.size for a in small)
    part = jnp.concatenate([_pack(small + [jnp.zeros((pad,), F32)], F32), d_meta], axis=0)
    total = _sum_parts(_all_gather([part], "gather_small")[0], "sum_small")
    small_w = [w[n] for n in SMALL]
    packs = [_pack([d[n] for n in SMALL] + [jnp.zeros((pad + 1,), F32)], F32) for d in (w, m, v)]
    upd = _adamw_call(packs[0], total[:SMALL_ROWS], packs[1], packs[2], "adamw_small")
    g_small, d_small, m_small, v_small = [dict(zip(SMALL, _unpack(p, small_w))) for p in (total[:SMALL_ROWS],) + tuple(upd)]
    loss_total = total[:SMALL_ROWS].reshape(-1)[SMALL_ROWS * PACK_W - pad - 1]
    g_meta = lax.dynamic_slice_in_dim(total[SMALL_ROWS:], me * LANE, LANE, axis=1)
    d_mt, m_mt, v_mt = _adamw_call(meta_tokens, g_meta, m_meta_tokens, v_meta_tokens, "adamw_meta")

    outs = []
    for got in ({**g_big, **g_small, "meta_tokens": g_meta}, {**d_big, **d_small, "meta_tokens": d_mt},
                {**m_big, **m_small, "meta_tokens": m_mt}, {**v_big, **v_small, "meta_tokens": v_mt}):
        outs += [got[n] for n in names]
    return (loss_total, grad_x[None], *outs)
```

```python
import jax
import jax.numpy as jnp
from jax import lax
from jax.experimental import pallas as pl
from jax.experimental.pallas import tpu as pltpu

F32 = jnp.float32
BF16 = jnp.bfloat16

D_MODEL = 1024
N_META = 16
BLOCK = 128
FRONT = (-N_META) % BLOCK
ROPE_THETA = 10000.0
EPS = 1e-6
NEG = -1e30
SWA_HEADS = 8
SWA_KV_HEADS = 2
SWA_GROUP = SWA_HEADS // SWA_KV_HEADS
SWA_HEAD_DIM = 64
MLA_HEADS = 8
MLA_Q_RANK = 256
MLA_KV_RANK = 128
MLA_NOPE_DIM = 64
MLA_ROPE_DIM = 32
MLA_V_DIM = 64
MLA_QK_DIM = MLA_NOPE_DIM + MLA_ROPE_DIM
SWA_Q_W = SWA_HEADS * SWA_HEAD_DIM
SWA_KV_W = SWA_KV_HEADS * SWA_HEAD_DIM
MLA_OUT_W = MLA_HEADS * MLA_V_DIM
SCALE_A = SWA_HEAD_DIM ** -0.5
SCALE_B = MLA_QK_DIM ** -0.5
LOG2E = 1.4426950408889634
Q_SCALE = SCALE_B * LOG2E
ADAM_LR = 0.001
ADAM_B1 = 0.9
ADAM_B2 = 0.999
ADAM_EPS = 1e-08
ADAM_WD = 0.01
ADAM_STEP = 10

LANE = 128
N_DEV = 8
HP = 8 * LANE
PO_QA, PO_KA, PO_VA = 0, HP, HP + 2 * LANE
PO_CQ = PO_VA + 2 * LANE
PO_CKV = PO_CQ + MLA_Q_RANK
PO_KR = PO_CKV + MLA_KV_RANK
PW_IN = PO_KR + LANE
N_TAB = 7
VMEM_LIMIT = 56 * 2 ** 20
TN_VMEM_BUDGET = 36 * 2 ** 20
MLA_HB = 4
MLA_HB_FWD = 8
ADAM_ROWS = 256
HALF = LANE // 2
assert SWA_HEAD_DIM == HALF and MLA_V_DIM == HALF

NT = (((1,), (1,)), ((), ()))
TN = (((0,), (0,)), ((), ()))


def _tile(t):
    return 384 if t % 384 == 0 else 128


def _params(*sem):
    return pltpu.CompilerParams(dimension_semantics=sem, vmem_limit_bytes=VMEM_LIMIT)


def _row(tm, n):
    return pl.BlockSpec((tm, n), lambda i: (i, 0))


def _const(shape):
    return pl.BlockSpec(shape, lambda i: (0,) * len(shape))


def _dot(a, b):
    return jnp.dot(a, b, preferred_element_type=F32)


def _dot_nt(a, b):
    return lax.dot_general(a, b, NT, preferred_element_type=F32)


def _dot_tn(a, b):
    return lax.dot_general(a, b, TN, preferred_element_type=F32)


def _rope(x, c, s1, s2, shift):
    return x * c + pltpu.roll(x, LANE - shift, 1) * s1 + pltpu.roll(x, shift, 1) * s2


def _rope_t(dy, c, s1, s2, shift):
    return dy * c + pltpu.roll(dy * s1, shift, 1) + pltpu.roll(dy * s2, LANE - shift, 1)


def _rms_r(x, n):
    return lax.rsqrt(jnp.sum(x * x, axis=-1, keepdims=True) * (1.0 / n) + EPS)


def _rms_bwd(x, g, dy, n):
    r = _rms_r(x, n)
    xh = x * r
    dxh = dy * g
    dx = r * (dxh - xh * (jnp.sum(dxh * xh, axis=-1, keepdims=True) * (1.0 / n)))
    return dx, jnp.sum(dy * xh, axis=0, keepdims=True)


def _acc(ref, val, first):
    @pl.when(first)
    def _():
        ref[...] = val

    @pl.when(jnp.logical_not(first))
    def _():
        ref[...] += val


def _pair_half(slab, half):
    return slab if half == 0 else pltpu.roll(slab, HALF, 1)


def _unpack_pair(slab, half):
    x = _pair_half(slab, half)
    return jnp.where(lax.broadcasted_iota(jnp.int32, x.shape, 1) < HALF, x, 0.0)


def _tabs(tab_ref):
    return [tab_ref[:, LANE * i:LANE * (i + 1)] for i in range(N_TAB)]


def _pre_fwd(h, g1, win, gq, wqu, gkv, wkv, tabs):
    t = h.shape[0]
    tm = _tile(t)

    def body(h_ref, g1_ref, win_ref, gq_ref, wqu_ref, gkv_ref, wkv_ref, tab_ref,
             u_ref, qa_ref, ka_ref, va_ref, cq_ref, ckv_ref, qn_ref, kvn_ref, qb_ref, kf_ref, vb_ref):
        ca, sa1, sa2, cb, sb1, sb2, ck = _tabs(tab_ref)
        hv = h_ref[...]
        u = (hv * _rms_r(hv, D_MODEL) * g1_ref[...]).astype(BF16)
        u_ref[...] = u
        p = _dot_nt(u, win_ref[...])
        for c in range(SWA_HEADS):
            sl = slice(LANE * c, LANE * (c + 1))
            qa_ref[:, sl] = _rope(p[:, PO_QA + LANE * c:PO_QA + LANE * (c + 1)], ca, sa1, sa2, 32).astype(BF16)
        for c in range(SWA_KV_HEADS):
            sl = slice(LANE * c, LANE * (c + 1))
            ka_ref[:, sl] = _rope(p[:, PO_KA + LANE * c:PO_KA + LANE * (c + 1)], ca, sa1, sa2, 32).astype(BF16)
        va_ref[...] = p[:, PO_VA:PO_CQ].astype(BF16)
        cq = p[:, PO_CQ:PO_CKV]
        ckv = p[:, PO_CKV:PO_KR]
        cq_ref[...] = cq
        ckv_ref[...] = ckv
        qn = (cq * _rms_r(cq, MLA_Q_RANK) * gq_ref[...]).astype(BF16)
        qn_ref[...] = qn
        qb = _dot_nt(qn, wqu_ref[...])
        kvn = (ckv * _rms_r(ckv, MLA_KV_RANK) * gkv_ref[...]).astype(BF16)
        kvn_ref[...] = kvn
        kv = _dot_nt(kvn, wkv_ref[...])
        kr = _rope(p[:, PO_KR:PW_IN], ck, sb1, sb2, 16)
        for c in range(MLA_HEADS):
            sl = slice(LANE * c, LANE * (c + 1))
            qb_ref[:, sl] = (_rope(qb[:, sl], cb, sb1, sb2, 16) * Q_SCALE).astype(BF16)
            kf_ref[:, sl] = (kv[:, sl] + kr).astype(BF16)
        vb_ref[...] = kv[:, HP:].astype(BF16)

    widths = [(D_MODEL, BF16), (HP, BF16), (2 * LANE, BF16), (2 * LANE, BF16), (MLA_Q_RANK, F32),
              (MLA_KV_RANK, F32), (MLA_Q_RANK, BF16), (MLA_KV_RANK, BF16), (HP, BF16), (HP, BF16), (HP, BF16)]
    return pl.pallas_call(
        body, name="pre_fwd", grid=(t // tm,),
        in_specs=[_row(tm, D_MODEL), _const(g1.shape), _const(win.shape), _const(gq.shape), _const(wqu.shape),
                  _const(gkv.shape), _const(wkv.shape), _row(tm, N_TAB * LANE)],
        out_specs=[_row(tm, w) for w, _ in widths],
        out_shape=[jax.ShapeDtypeStruct((t, w), d) for w, d in widths],
        compiler_params=_params("parallel"),
    )(h, g1, win, gq, wqu, gkv, wkv, tabs)


def _swa_mask(nb):
    key = lax.broadcasted_iota(jnp.int32, (2 * BLOCK, SWA_GROUP * BLOCK), 0)
    qry = lax.broadcasted_iota(jnp.int32, (2 * BLOCK, SWA_GROUP * BLOCK), 1) & (BLOCK - 1)
    return (key > qry) & (key <= qry + BLOCK) & (key + (nb - 1) * BLOCK >= FRONT)


def _swa_group(ref, rows, j):
    return jnp.concatenate([ref[rows, LANE * (SWA_GROUP * j + g):LANE * (SWA_GROUP * j + g + 1)]
                            for g in range(SWA_GROUP)], axis=0)


def _swa_packed_group(ref, rows, j):
    heads = [SWA_GROUP * j + g for g in range(SWA_GROUP)]
    return jnp.concatenate([_pair_half(ref[rows, LANE * (hd // 2):LANE * (hd // 2 + 1)], hd % 2) for hd in heads], axis=0)


def _swa_sinks(sink_ref, j):
    return jnp.concatenate([jnp.full((1, BLOCK), sink_ref[0, SWA_GROUP * j + g], F32) for g in range(SWA_GROUP)], axis=1)


def _swa_keys(prev_ref, cur_ref, rb, j):
    sl = slice(LANE * j, LANE * (j + 1))
    if rb == 0:
        return jnp.concatenate([prev_ref[:, sl], cur_ref[:BLOCK, sl]], axis=0)
    return cur_ref[BLOCK * (rb - 1):BLOCK * (rb + 1), sl]


def _swa_chains(t):
    return [(rb, j) for rb in range(_tile(t) // BLOCK) for j in range(SWA_KV_HEADS)]


def _swa_scores(sink_ref, q_ref, kp_ref, kc_ref, n, t):
    r = _tile(t) // BLOCK
    chains = _swa_chains(t)
    qs = [_swa_group(q_ref, slice(BLOCK * rb, BLOCK * (rb + 1)), j) for rb, j in chains]
    ks = [_swa_keys(kp_ref, kc_ref, rb, j) for rb, j in chains]
    ss = [_dot_nt(k2, q4) for q4, k2 in zip(qs, ks)]
    masks = [_swa_mask(n * r + rb) for rb in range(r)]
    out = []
    for (rb, j), s in zip(chains, ss):
        sink = _swa_sinks(sink_ref, j)
        s = jnp.where(masks[rb], s * SCALE_A, NEG)
        m = jnp.maximum(jnp.max(s, axis=0, keepdims=True), sink)
        e = jnp.exp(s - m)
        es = jnp.exp(sink - m)
        inv = 1.0 / (jnp.sum(e, axis=0, keepdims=True) + es)
        out.append((e * inv, es * inv))
    return qs, ks, out


def _swa_specs(t):
    ts = _tile(t)
    r = ts // BLOCK
    prev = lambda n: (jnp.maximum(n * r - 1, 0), 0)
    cur = lambda n: (n, 0)
    return [pl.BlockSpec(memory_space=pltpu.SMEM), pl.BlockSpec((ts, HP), cur),
            pl.BlockSpec((BLOCK, 2 * LANE), prev), pl.BlockSpec((ts, 2 * LANE), cur),
            pl.BlockSpec((BLOCK, 2 * LANE), prev), pl.BlockSpec((ts, 2 * LANE), cur)]


def _swa_fwd(sinks, q, k, v):
    t = q.shape[0]
    ts = _tile(t)

    def body(sink_ref, q_ref, kp_ref, kc_ref, vp_ref, vc_ref, o_ref):
        chains = _swa_chains(t)
        _, _, probs = _swa_scores(sink_ref, q_ref, kp_ref, kc_ref, pl.program_id(0), t)
        os_ = [_dot_tn(_swa_keys(vp_ref, vc_ref, rb, j)[:, :HALF], p.astype(BF16)) for (rb, j), (p, _) in zip(chains, probs)]
        for (rb, j), o4 in zip(chains, os_):
            for g in range(0, SWA_GROUP, 2):
                pair = (SWA_GROUP * j + g) // 2
                o_ref[BLOCK * rb:BLOCK * (rb + 1), LANE * pair:LANE * (pair + 1)] = jnp.concatenate(
                    [o4[:, BLOCK * g:BLOCK * (g + 1)], o4[:, BLOCK * (g + 1):BLOCK * (g + 2)]], axis=0).T

    return pl.pallas_call(
        body, name="swa_fwd", grid=(t // ts,),
        in_specs=_swa_specs(t),
        out_specs=pl.BlockSpec((ts, SWA_Q_W), lambda n: (n, 0)),
        out_shape=jax.ShapeDtypeStruct((t, SWA_Q_W), F32),
        compiler_params=_params("parallel"),
    )(sinks, q, k, k, v, v)


def _causal_mask(q0, k0, tq, tk, transposed):
    if transposed:
        key = k0 + lax.broadcasted_iota(jnp.int32, (tk, tq), 0)
        qry = q0 + lax.broadcasted_iota(jnp.int32, (tk, tq), 1)
    else:
        qry = q0 + lax.broadcasted_iota(jnp.int32, (tq, tk), 0)
        key = k0 + lax.broadcasted_iota(jnp.int32, (tq, tk), 1)
    return (key <= qry) & (key >= FRONT)


def _heads(ref, hb, rows=slice(None)):
    return [ref[rows, LANE * a:LANE * (a + 1)] for a in range(hb)]


def _head_stats(t, hb=MLA_HB):
    return jax.ShapeDtypeStruct((MLA_HEADS // hb, t, hb), F32)


def _mla_fwd(q, k, v, shards=()):
    t = q.shape[0]
    tq = _tile(t)
    nq = t // tq
    n = len(shards)
    hb = MLA_HB_FWD
    steps = (MLA_HEADS // hb) * nq

    def body(q_ref, k_ref, v_ref, *rest):
        x_refs, (o_ref, lse_ref), out_refs = rest[:n], rest[n:n + 2], rest[n + 2:2 * n + 2]
        acc_sc, sems = rest[2 * n + 2], rest[2 * n + 3:]
        i = pl.program_id(1)
        step_id = pl.program_id(0) * nq + i
        if n:
            plan = _gather_plan(x_refs, out_refs, *sems)
            pl.when(step_id == 0)(plan.start)
            pl.when(step_id == (3 * steps) // 4)(plan.forward)
        qs = _heads(q_ref, hb)
        acc_sc[...] = jnp.zeros(acc_sc.shape, F32)

        def step(j, carry, masked):
            rows = pl.ds(pl.multiple_of(j * tq, tq), tq)
            ks = _heads(k_ref, hb, rows)
            vs = [v_ref[rows, LANE * a:LANE * a + HALF] for a in range(hb)]
            ss = [_dot_nt(kh, qh) for qh, kh in zip(qs, ks)]
            if masked:
                mask = _causal_mask(i * tq, j * tq, tq, tq, True)
                ss = [jnp.where(mask, s, NEG) for s in ss]
            mid, out = [], []
            for s, (m, l) in zip(ss, carry):
                mn = jnp.maximum(m, jnp.max(s, axis=0, keepdims=True))
                al = jnp.exp2(m - mn)
                p = jnp.exp2(s - mn)
                out.append((mn, al * l + jnp.sum(p, axis=0, keepdims=True)))
                mid.append((al, p.astype(BF16)))
            for a, ((al, p), vh) in enumerate(zip(mid, vs)):
                acc_sc[a] = al * acc_sc[a] + _dot_tn(vh, p)
            return tuple(out)

        init = ((jnp.full((1, tq), NEG, F32), jnp.zeros((1, tq), F32)),) * hb
        carry = lax.fori_loop(0, jnp.minimum(i, 1) + 1, lambda it, c: step(it * i, c, True), init)
        carry = lax.fori_loop(1, i, lambda j, c: step(j, c, False), carry)
        outs = [acc_sc[a] * (1.0 / l) for a, (_, l) in enumerate(carry)]
        for a in range(0, hb, 2):
            o_ref[:, HALF * a:HALF * (a + 2)] = jnp.concatenate(outs[a:a + 2], axis=0).T
        for a, (m, l) in enumerate(carry):
            lse_ref[:, a:a + 1] = jnp.broadcast_to(m + jnp.log2(l), (LANE, tq)).T[:, :1]
        if n:
            pl.when(step_id == steps - 1)(plan.finish)

    blk = pl.BlockSpec((tq, hb * LANE), lambda h, i: (i, h))
    full = pl.BlockSpec((t, hb * LANE), lambda h, i: (0, h))
    packed = pl.BlockSpec((tq, hb * HALF), lambda h, i: (i, h))
    out = pl.pallas_call(
        body, name="mla_fwd_gather" if n else "mla_fwd", grid=(MLA_HEADS // hb, nq),
        in_specs=[blk, full, full] + [ANY] * n,
        out_specs=[packed, pl.BlockSpec((None, tq, hb), lambda h, i: (h, i, 0))] + [ANY] * n,
        out_shape=[jax.ShapeDtypeStruct((t, MLA_OUT_W), F32), _head_stats(t, hb)]
        + [jax.ShapeDtypeStruct((N_DEV,) + a.shape, a.dtype) for a in shards],
        scratch_shapes=[pltpu.VMEM((hb, HALF, tq), F32)] + (_comm_sems(n) if n else []),
        compiler_params=_params("arbitrary", "arbitrary"),
    )(q, k, v, *shards)
    return out[0], out[1], out[2:]


def _mix_fwd(h, oa, ob, ga, gb, wo, g2):
    t = h.shape[0]
    tm = _tile(t)

    def body(h_ref, oa_ref, ob_ref, ga_ref, gb_ref, wo_ref, g2_ref, h2_ref, mix_ref, u2_ref):
        oa_v = oa_ref[...]
        ob_v = ob_ref[...]
        na = (oa_v * _rms_r(oa_v, SWA_Q_W) * ga_ref[...]).astype(BF16)
        nb = (ob_v * _rms_r(ob_v, MLA_OUT_W) * gb_ref[...]).astype(BF16)
        mix_ref[:, :SWA_Q_W] = na
        mix_ref[:, SWA_Q_W:] = nb
        h2 = h_ref[...] + _dot(na, wo_ref[:SWA_Q_W, :]) + _dot(nb, wo_ref[SWA_Q_W:, :])
        h2_ref[...] = h2
        u2_ref[...] = (h2 * _rms_r(h2, D_MODEL) * g2_ref[...]).astype(BF16)

    mix_w = SWA_Q_W + MLA_OUT_W
    return pl.pallas_call(
        body, name="mix_fwd", grid=(t // tm,),
        in_specs=[_row(tm, D_MODEL), _row(tm, SWA_Q_W), _row(tm, MLA_OUT_W), _const(ga.shape), _const(gb.shape),
                  _const(wo.shape), _const(g2.shape)],
        out_specs=[_row(tm, D_MODEL), _row(tm, mix_w), _row(tm, D_MODEL)],
        out_shape=[jax.ShapeDtypeStruct((t, D_MODEL), F32), jax.ShapeDtypeStruct((t, mix_w), BF16),
                   jax.ShapeDtypeStruct((t, D_MODEL), BF16)],
        compiler_params=_params("parallel"),
    )(h, oa, ob, ga, gb, wo, g2)


def _ffn_fwd(h2, u2, wg_t, wu_t, wd):
    t = h2.shape[0]
    tm = _tile(t)
    dff = wd.shape[0]

    def body(h2_ref, u2_ref, wg_ref, wu_ref, wd_ref, h3_ref, g_ref, up_ref):
        u2v = u2_ref[...]
        g = _dot_nt(u2v, wg_ref[...])
        up = _dot_nt(u2v, wu_ref[...])
        g_ref[...] = g.astype(BF16)
        up_ref[...] = up.astype(BF16)
        a = (g * jax.nn.sigmoid(g) * up).astype(BF16)
        h3_ref[...] = h2_ref[...] + _dot(a, wd_ref[...])

    return pl.pallas_call(
        body, name="ffn_fwd", grid=(t // tm,),
        in_specs=[_row(tm, D_MODEL), _row(tm, D_MODEL), _const(wg_t.shape), _const(wu_t.shape), _const(wd.shape)],
        out_specs=[_row(tm, D_MODEL), _row(tm, dff), _row(tm, dff)],
        out_shape=[jax.ShapeDtypeStruct((t, D_MODEL), F32), jax.ShapeDtypeStruct((t, dff), BF16),
                   jax.ShapeDtypeStruct((t, dff), BF16)],
        compiler_params=_params("parallel"),
    )(h2, u2, wg_t, wu_t, wd)


def _loss_bwd(h, gf, target):
    t = h.shape[0]
    tm = _tile(t)
    first_row = FRONT + N_META

    def body(h_ref, gf_ref, t_ref, dh_ref, dgf_ref, loss_ref):
        i = pl.program_id(0)
        hv = h_ref[...]
        y = hv * _rms_r(hv, D_MODEL) * gf_ref[...]
        row = i * tm + lax.broadcasted_iota(jnp.int32, (tm, 1), 0)
        err = jnp.where(row >= first_row, y - t_ref[...], 0.0)
        dx, dg = _rms_bwd(hv, gf_ref[...], err * (1.0 / D_MODEL), D_MODEL)
        dh_ref[...] = dx
        _acc(dgf_ref, dg, i == 0)
        part = 0.5 * jnp.sum(jnp.sum(err * err, axis=1, keepdims=True) * (1.0 / D_MODEL), axis=0, keepdims=True)
        _acc(loss_ref, jnp.broadcast_to(part, (1, LANE)), i == 0)

    return pl.pallas_call(
        body, name="loss_bwd", grid=(t // tm,),
        in_specs=[_row(tm, D_MODEL), _const(gf.shape), _row(tm, D_MODEL)],
        out_specs=[_row(tm, D_MODEL), _const((1, D_MODEL)), _const((1, LANE))],
        out_shape=[jax.ShapeDtypeStruct((t, D_MODEL), F32), jax.ShapeDtypeStruct((1, D_MODEL), F32),
                   jax.ShapeDtypeStruct((1, LANE), F32)],
        compiler_params=_params("arbitrary"),
    )(h, gf, target)


def _tn_matmul(a, b, name, cols=None):
    t, n = b.shape
    first, k = cols or (0, a.shape[1])
    tk = next(c for c in (k, 1024, 512, 256, 128) if k % c == 0 and first % c == 0 and c <= 1024)
    fits = lambda c: 2 * (t * (tk + c) * 2 + tk * c * 2) <= TN_VMEM_BUDGET
    tn = next(c for c in (n, 1024, 512, 256, 128) if n % c == 0 and fits(c))

    def body(a_ref, b_ref, o_ref):
        o_ref[...] = _dot_tn(a_ref[...], b_ref[...]).astype(BF16)

    return pl.pallas_call(
        body, name=name, grid=(k // tk, n // tn),
        in_specs=[pl.BlockSpec((t, tk), lambda i, j: (0, i + first // tk)), pl.BlockSpec((t, tn), lambda i, j: (0, j))],
        out_specs=pl.BlockSpec((tk, tn), lambda i, j: (i, j)),
        out_shape=jax.ShapeDtypeStruct((k, n), BF16),
        compiler_params=_params("parallel", "parallel"),
    )(a, b)


def _ffn_bwd_a(dh3, g, up, wd):
    t = dh3.shape[0]
    tm = _tile(t)
    dff = wd.shape[0]

    def body(dh3_ref, g_ref, up_ref, wd_ref, a_ref, dgu_ref, dh3b_ref):
        dh3b = dh3_ref[...].astype(BF16)
        dh3b_ref[...] = dh3b
        da = _dot_nt(dh3b, wd_ref[...])
        gv = g_ref[...].astype(F32)
        upv = up_ref[...].astype(F32)
        sg = jax.nn.sigmoid(gv)
        silu = gv * sg
        a_ref[...] = (silu * upv).astype(BF16)
        dgu_ref[:, :dff] = (da * upv * (sg * (1.0 + gv * (1.0 - sg)))).astype(BF16)
        dgu_ref[:, dff:] = (da * silu).astype(BF16)

    return pl.pallas_call(
        body, name="ffn_bwd_a", grid=(t // tm,),
        in_specs=[_row(tm, D_MODEL), _row(tm, dff), _row(tm, dff), _const(wd.shape)],
        out_specs=[_row(tm, dff), _row(tm, 2 * dff), _row(tm, D_MODEL)],
        out_shape=[jax.ShapeDtypeStruct((t, dff), BF16), jax.ShapeDtypeStruct((t, 2 * dff), BF16),
                   jax.ShapeDtypeStruct((t, D_MODEL), BF16)],
        compiler_params=_params("parallel"),
    )(dh3, g, up, wd)


def _ffn_bwd_b(dh3, dgu, h2, g2, wg_t, wu_t):
    t = dh3.shape[0]
    tm = _tile(t)
    dff = wg_t.shape[0]

    def body(dh3_ref, dgu_ref, h2_ref, g2_ref, wg_ref, wu_ref, dh2_ref, dh2b_ref, dg2_ref):
        du2 = _dot(dgu_ref[:, :dff], wg_ref[...]) + _dot(dgu_ref[:, dff:], wu_ref[...])
        dx, dg = _rms_bwd(h2_ref[...], g2_ref[...], du2, D_MODEL)
        dh2 = dh3_ref[...] + dx
        dh2_ref[...] = dh2
        dh2b_ref[...] = dh2.astype(BF16)
        _acc(dg2_ref, dg, pl.program_id(0) == 0)

    return pl.pallas_call(
        body, name="ffn_bwd_b", grid=(t // tm,),
        in_specs=[_row(tm, D_MODEL), _row(tm, 2 * dff), _row(tm, D_MODEL), _const(g2.shape), _const(wg_t.shape),
                  _const(wu_t.shape)],
        out_specs=[_row(tm, D_MODEL), _row(tm, D_MODEL), _const((1, D_MODEL))],
        out_shape=[jax.ShapeDtypeStruct((t, D_MODEL), F32), jax.ShapeDtypeStruct((t, D_MODEL), BF16),
                   jax.ShapeDtypeStruct((1, D_MODEL), F32)],
        compiler_params=_params("arbitrary"),
    )(dh3, dgu, h2, g2, wg_t, wu_t)


def _mix_bwd(dh2, oa, ob, ga, gb, wo):
    t = dh2.shape[0]
    tm = _tile(t)

    def body(dh2_ref, oa_ref, ob_ref, ga_ref, gb_ref, wo_ref, doa_ref, dob_ref, dl_ref, dga_ref, dgb_ref):
        first = pl.program_id(0) == 0
        d = dh2_ref[...]
        ob_v = ob_ref[...]
        dxa, dga = _rms_bwd(oa_ref[...], ga_ref[...], _dot_nt(d, wo_ref[:SWA_Q_W, :]), SWA_Q_W)
        dxb, dgb = _rms_bwd(ob_v, gb_ref[...], _dot_nt(d, wo_ref[SWA_Q_W:, :]), MLA_OUT_W)
        lower = lax.broadcasted_iota(jnp.int32, (tm, LANE), 1) < HALF
        for hd in range(MLA_HEADS):
            sl = slice(LANE * (hd // 2), LANE * (hd // 2 + 1))
            mine = lower if hd % 2 == 0 else jnp.logical_not(lower)
            delta = jnp.sum(jnp.where(mine, ob_v[:, sl] * dxb[:, sl], 0.0), axis=1, keepdims=True)
            dl_ref[hd // MLA_HB, :, hd % MLA_HB:hd % MLA_HB + 1] = delta
        for ref, dx, heads in ((doa_ref, dxa, SWA_HEADS), (dob_ref, dxb, MLA_HEADS)):
            for hd in range(heads):
                slab = dx[:, LANE * (hd // 2):LANE * (hd // 2 + 1)]
                ref[:, LANE * hd:LANE * (hd + 1)] = _unpack_pair(slab, hd % 2).astype(BF16)
        _acc(dga_ref, dga, first)
        _acc(dgb_ref, dgb, first)

    return pl.pallas_call(
        body, name="mix_bwd", grid=(t // tm,),
        in_specs=[_row(tm, D_MODEL), _row(tm, SWA_Q_W), _row(tm, MLA_OUT_W), _const(ga.shape), _const(gb.shape),
                  _const(wo.shape)],
        out_specs=[_row(tm, HP), _row(tm, HP), pl.BlockSpec((MLA_HEADS // MLA_HB, tm, MLA_HB), lambda i: (0, i, 0)),
                   _const((1, SWA_Q_W)), _const((1, MLA_OUT_W))],
        out_shape=[jax.ShapeDtypeStruct((t, HP), BF16), jax.ShapeDtypeStruct((t, HP), BF16), _head_stats(t),
                   jax.ShapeDtypeStruct((1, SWA_Q_W), F32), jax.ShapeDtypeStruct((1, MLA_OUT_W), F32)],
        compiler_params=_params("arbitrary"),
    )(dh2, oa, ob, ga, gb, wo)


def _swa_bwd(sinks, q, k, v, o, do):
    t = q.shape[0]
    ts = _tile(t)

    def body(sink_ref, q_ref, kp_ref, kc_ref, vp_ref, vc_ref, o_ref, do_ref,
             dq_ref, dkc_ref, dkp_ref, dvc_ref, dvp_ref, dsink_ref):
        n = pl.program_id(0)
        chains = _swa_chains(t)
        qs, ks, probs = _swa_scores(sink_ref, q_ref, kp_ref, kc_ref, n, t)
        dos = [_swa_group(do_ref, slice(BLOCK * rb, BLOCK * (rb + 1)), j) for rb, j in chains]
        vs = [_swa_keys(vp_ref, vc_ref, rb, j) for rb, j in chains]
        dps = [_dot_nt(v2, do4) for do4, v2 in zip(dos, vs)]
        dss, dsks = [], []
        for (rb, j), (p, psink), do4, dp in zip(chains, probs, dos, dps):
            o4 = _swa_packed_group(o_ref, slice(BLOCK * rb, BLOCK * (rb + 1)), j)
            delta = jnp.sum(o4 * do4.astype(F32), axis=1, keepdims=True)
            delta = jnp.broadcast_to(delta, (SWA_GROUP * BLOCK, LANE)).T[:1, :]
            dss.append((p * (dp - delta) * SCALE_A).astype(BF16))
            dsks.append(-psink * delta)
        dqs = [_dot_tn(k2[:, :HALF], ds) for ds, k2 in zip(dss, ks)]
        dks = [_dot(ds, q4) for ds, q4 in zip(dss, qs)]
        dvs = [_dot(p.astype(BF16), do4) for (p, _), do4 in zip(probs, dos)]
        dsink = [jnp.zeros((1, LANE), F32)] * SWA_HEADS
        ext = {}
        for (rb, j), dq4, dk2, dv2, dsk in zip(chains, dqs, dks, dvs, dsks):
            for g in range(SWA_GROUP):
                hd = SWA_GROUP * j + g
                cols = slice(BLOCK * g, BLOCK * (g + 1))
                dq_ref[BLOCK * rb:BLOCK * (rb + 1), LANE * hd:LANE * (hd + 1)] = jnp.concatenate(
                    [dq4[:, cols], jnp.zeros((HALF, BLOCK), F32)], axis=0).T.astype(BF16)
                dsink[hd] = dsink[hd] + jnp.sum(dsk[:, cols], axis=1, keepdims=True)
            for half in range(2):
                key = (j, rb + half)
                part = (dk2[BLOCK * half:BLOCK * (half + 1)], dv2[BLOCK * half:BLOCK * (half + 1)])
                ext[key] = part if key not in ext else (ext[key][0] + part[0], ext[key][1] + part[1])
        for (j, blk), (dk, dv) in ext.items():
            sl = slice(LANE * j, LANE * (j + 1))
            if blk == 0:
                dkp_ref[:, sl] = dk
                dvp_ref[:, sl] = dv
            else:
                dkc_ref[BLOCK * (blk - 1):BLOCK * blk, sl] = dk
                dvc_ref[BLOCK * (blk - 1):BLOCK * blk, sl] = dv
        for hd in range(SWA_HEADS):
            _acc(dsink_ref.at[hd:hd + 1, :], jnp.broadcast_to(dsink[hd], (1, LANE)), n == 0)

    cur = lambda n: (n, 0)
    kv = pl.BlockSpec((ts, 2 * LANE), cur)
    kvp = pl.BlockSpec((BLOCK, 2 * LANE), cur)
    hp = pl.BlockSpec((ts, HP), cur)
    kvs = jax.ShapeDtypeStruct((t, 2 * LANE), F32)
    kvps = jax.ShapeDtypeStruct((t // ts * BLOCK, 2 * LANE), F32)
    return pl.pallas_call(
        body, name="swa_bwd", grid=(t // ts,),
        in_specs=_swa_specs(t) + [pl.BlockSpec((ts, SWA_Q_W), cur), hp],
        out_specs=[hp, kv, kvp, kv, kvp, _const((SWA_HEADS, LANE))],
        out_shape=[jax.ShapeDtypeStruct((t, HP), BF16), kvs, kvps, kvs, kvps,
                   jax.ShapeDtypeStruct((SWA_HEADS, LANE), F32)],
        compiler_params=_params("arbitrary"),
    )(sinks, q, k, k, v, v, o, do)


def _mla_bwd(q, k, v, do, lse, dl, slabs=()):
    t = q.shape[0]
    tq = _tile(t)
    nq = t // tq
    n = len(slabs)
    hb = MLA_HB
    steps = (MLA_HEADS // hb) * nq

    def body(k_ref, v_ref, q_ref, do_ref, lse_ref, dl_ref, *rest):
        in_refs, (dq_ref, dk_ref, dv_ref), out_refs = rest[:n], rest[n:n + 3], rest[n + 3:2 * n + 3]
        (dq_sc, dk_sc, dv_sc), sems = rest[2 * n + 3:2 * n + 6], rest[2 * n + 6:]
        j = pl.program_id(1)
        step_id = pl.program_id(0) * nq + j
        if n:
            plan = _exchange_plan(in_refs, out_refs, *sems)
            pl.when(step_id == 0)(plan.start)

        @pl.when(j == 0)
        def _():
            dq_sc[...] = jnp.zeros(dq_sc.shape, F32)

        dk_sc[...] = jnp.zeros(dk_sc.shape, F32)
        dv_sc[...] = jnp.zeros(dv_sc.shape, F32)
        ks, vs = _heads(k_ref, hb), _heads(v_ref, hb)

        def step(i, carry, masked):
            rows = pl.ds(pl.multiple_of(i * tq, tq), tq)
            qs, dos = _heads(q_ref, hb, rows), _heads(do_ref, hb, rows)
            ss = [_dot_nt(qh, kh) for qh, kh in zip(qs, ks)]
            dps = [_dot_nt(doh, vh) for doh, vh in zip(dos, vs)]
            if masked:
                mask = _causal_mask(i * tq, j * tq, tq, tq, False)
                ss = [jnp.where(mask, s_, NEG) for s_ in ss]
            ps = [jnp.exp2(s_ - lse_ref[rows, a:a + 1]) for a, s_ in enumerate(ss)]
            dss = [(p * (dp - dl_ref[rows, a:a + 1])).astype(BF16) for a, (p, dp) in enumerate(zip(ps, dps))]
            for a, (ds, p, qh, kh, doh) in enumerate(zip(dss, ps, qs, ks, dos)):
                dq_sc[a, rows, :] += _dot(ds, kh)
                dk_sc[a, :MLA_QK_DIM, :] += _dot_tn(qh[:, :MLA_QK_DIM], ds)
                dv_sc[a, :MLA_V_DIM, :] += _dot_tn(doh[:, :MLA_V_DIM], p.astype(BF16))
            return carry

        split = jnp.where(j == 0, nq, j + 1)
        lax.fori_loop(j, split, lambda i, c: step(i, c, True), 0)
        lax.fori_loop(split, nq, lambda i, c: step(i, c, False), 0)
        for a in range(hb):
            dk_ref[:, LANE * a:LANE * (a + 1)] = (dk_sc[a] * (1.0 / LOG2E)).T.astype(BF16)
            dv_ref[:, LANE * a:LANE * (a + 1)] = dv_sc[a].T.astype(BF16)

        @pl.when(j == nq - 1)
        def _():
            for a in range(hb):
                dq_ref[:, LANE * a:LANE * (a + 1)] = (dq_sc[a] * SCALE_B).astype(BF16)

        if n:
            pl.when(step_id == steps - 1)(plan.finish)

    blk = pl.BlockSpec((tq, hb * LANE), lambda h, j: (j, h))
    full = pl.BlockSpec((t, hb * LANE), lambda h, j: (0, h))
    cols = pl.BlockSpec((None, t, hb), lambda h, j: (h, 0, 0))
    out = pl.pallas_call(
        body, name="mla_bwd_exchange" if n else "mla_bwd", grid=(MLA_HEADS // hb, nq),
        in_specs=[blk, blk, full, full, cols, cols] + [ANY] * n, out_specs=[full, blk, blk] + [ANY] * n,
        out_shape=[jax.ShapeDtypeStruct((t, HP), BF16)] * 3 + [jax.ShapeDtypeStruct(a.shape, a.dtype) for a in slabs],
        scratch_shapes=[pltpu.VMEM((hb, t, LANE), F32)] + [pltpu.VMEM((hb, LANE, tq), F32)] * 2
        + (_comm_sems(n) if n else []),
        compiler_params=_params("arbitrary", "arbitrary"),
    )(k, v, q, do, lse, dl, *slabs)
    return out[:3], out[3:]


def _pre_bwd(dh2, h, cq, ckv, dqa, dka, dka_next, dva, dva_next, dqb, dkf, dvb, g1, win, gq, wqu, gkv, wkv, tabs):
    t = h.shape[0]
    tm = _tile(t)

    def body(dh2_ref, h_ref, cq_ref, ckv_ref, dqa_ref, dka_ref, dkan_ref, dva_ref, dvan_ref, dqb_ref, dkf_ref, dvb_ref,
             g1_ref, win_ref, gq_ref, wqu_ref, gkv_ref, wkv_ref, tab_ref,
             dh_ref, dp_ref, dqbo_ref, dkvo_ref, dg1_ref, dgq_ref, dgkv_ref):
        first = pl.program_id(0) == 0
        ca, sa1, sa2, cb, sb1, sb2, ck = _tabs(tab_ref)
        dkr = jnp.zeros((tm, LANE), F32)
        for c in range(MLA_HEADS):
            sl = slice(LANE * c, LANE * (c + 1))
            dqbo_ref[:, sl] = _rope_t(dqb_ref[:, sl].astype(F32), cb, sb1, sb2, 16).astype(BF16)
            dkr += dkf_ref[:, sl].astype(F32)
        dkvo_ref[:, :HP] = dkf_ref[...]
        dkvo_ref[:, HP:] = dvb_ref[...]
        dcq, dgq = _rms_bwd(cq_ref[...], gq_ref[...], _dot(dqbo_ref[...], wqu_ref[...]), MLA_Q_RANK)
        dckv, dgkv = _rms_bwd(ckv_ref[...], gkv_ref[...], _dot(dkvo_ref[...], wkv_ref[...]), MLA_KV_RANK)
        for c in range(SWA_HEADS):
            sl = slice(LANE * c, LANE * (c + 1))
            dp_ref[:, PO_QA + LANE * c:PO_QA + LANE * (c + 1)] = _rope_t(dqa_ref[:, sl].astype(F32), ca, sa1, sa2,
                                                                          32).astype(BF16)
        last = slice(tm - BLOCK, tm)
        more = pl.program_id(0) < t // tm - 1
        for c in range(SWA_KV_HEADS):
            sl = slice(LANE * c, LANE * (c + 1))
            dk = dka_ref[:, sl]
            dk_last = dk[tm - BLOCK:] + jnp.where(more, dkan_ref[:, sl], 0.0)
            cols = slice(PO_KA + LANE * c, PO_KA + LANE * (c + 1))
            if tm > BLOCK:
                dp_ref[:tm - BLOCK, cols] = _rope_t(dk[:tm - BLOCK], ca[:tm - BLOCK], sa1[:tm - BLOCK], sa2[:tm - BLOCK],
                                                    32).astype(BF16)
            dp_ref[last, cols] = _rope_t(dk_last, ca[tm - BLOCK:], sa1[tm - BLOCK:], sa2[tm - BLOCK:], 32).astype(BF16)
        if tm > BLOCK:
            dp_ref[:tm - BLOCK, PO_VA:PO_CQ] = dva_ref[:tm - BLOCK, :].astype(BF16)
        dp_ref[last, PO_VA:PO_CQ] = (dva_ref[tm - BLOCK:, :] + jnp.where(more, dvan_ref[...], 0.0)).astype(BF16)
        dp_ref[:, PO_CQ:PO_CKV] = dcq.astype(BF16)
        dp_ref[:, PO_CKV:PO_KR] = dckv.astype(BF16)
        dp_ref[:, PO_KR:PW_IN] = _rope_t(dkr, ck, sb1, sb2, 16).astype(BF16)
        dx, dg1 = _rms_bwd(h_ref[...], g1_ref[...], _dot(dp_ref[...], win_ref[...]), D_MODEL)
        dh_ref[...] = dh2_ref[...] + dx
        _acc(dg1_ref, dg1, first)
        _acc(dgq_ref, dgq, first)
        _acc(dgkv_ref, dgkv, first)

    kv = _row(tm, 2 * LANE)
    nxt = pl.BlockSpec((BLOCK, 2 * LANE), lambda i: (jnp.minimum(i + 1, t // tm - 1), 0))
    return pl.pallas_call(
        body, name="pre_bwd", grid=(t // tm,),
        in_specs=[_row(tm, D_MODEL), _row(tm, D_MODEL), _row(tm, MLA_Q_RANK), _row(tm, MLA_KV_RANK), _row(tm, HP),
                  kv, nxt, kv, nxt, _row(tm, HP), _row(tm, HP), _row(tm, HP),
                  _const(g1.shape), _const(win.shape), _const(gq.shape), _const(wqu.shape), _const(gkv.shape),
                  _const(wkv.shape), _row(tm, N_TAB * LANE)],
        out_specs=[_row(tm, D_MODEL), _row(tm, PW_IN), _row(tm, HP), _row(tm, 2 * HP),
                   _const((1, D_MODEL)), _const((1, MLA_Q_RANK)), _const((1, MLA_KV_RANK))],
        out_shape=[jax.ShapeDtypeStruct((t, D_MODEL), F32), jax.ShapeDtypeStruct((t, PW_IN), BF16),
                   jax.ShapeDtypeStruct((t, HP), BF16), jax.ShapeDtypeStruct((t, 2 * HP), BF16),
                   jax.ShapeDtypeStruct((1, D_MODEL), F32), jax.ShapeDtypeStruct((1, MLA_Q_RANK), F32),
                   jax.ShapeDtypeStruct((1, MLA_KV_RANK), F32)],
        compiler_params=_params("arbitrary"),
    )(dh2, h, cq, ckv, dqa, dka, dka_next, dva, dva_next, dqb, dkf, dvb, g1, win, gq, wqu, gkv, wkv, tabs)


def _rope_tables(t):
    pos = (jnp.arange(t, dtype=jnp.int32) - FRONT).astype(F32)[:, None]
    lane = jnp.arange(LANE)[None, :]

    def table(dim, start):
        half = dim // 2
        inv = ROPE_THETA ** (-jnp.arange(0, dim, 2, dtype=F32) / dim)
        ang = pos * inv[None, :]
        cos = jnp.concatenate([jnp.cos(ang)] * 2, axis=1)
        sin = jnp.concatenate([jnp.sin(ang)] * 2, axis=1)
        pad = lambda a: jnp.pad(a, ((0, 0), (start, LANE - start - dim)))
        first = (lane >= start) & (lane < start + half)
        second = (lane >= start + half) & (lane < start + dim)
        return pad(cos), jnp.where(first, -pad(sin), 0.0), jnp.where(second, pad(sin), 0.0)

    ca, sa1, sa2 = table(SWA_HEAD_DIM, 0)
    ck, sb1, sb2 = table(MLA_ROPE_DIM, MLA_NOPE_DIM)
    cb = jnp.where(lane < MLA_NOPE_DIM, 1.0, ck)
    return jnp.concatenate([ca, sa1, sa2, cb, sb1, sb2, ck], axis=1)


def _pad_heads(w, heads, dim, axis):
    shp = w.shape
    w = w.reshape(shp[:axis] + (heads, dim) + shp[axis + 1:])
    pad = [(0, 0)] * w.ndim
    pad[axis + 1] = (0, LANE - dim)
    return jnp.pad(w, pad).reshape(shp[:axis] + (heads * LANE,) + shp[axis + 1:])


def _unpad_heads(w, heads, dim, axis):
    shp = w.shape
    w = w.reshape(shp[:axis] + (heads, LANE) + shp[axis + 1:])
    w = lax.slice_in_dim(w, 0, dim, axis=axis + 1)
    return w.reshape(shp[:axis] + (heads * dim,) + shp[axis + 1:])


def _pad_layer(w_in, w_q_up, w_kv_up):
    o1 = SWA_Q_W
    o2 = o1 + SWA_KV_W
    o3 = o2 + SWA_KV_W
    o4 = o3 + MLA_Q_RANK
    o5 = o4 + MLA_KV_RANK
    kr = jnp.pad(w_in[o5:], ((MLA_NOPE_DIM, LANE - MLA_QK_DIM), (0, 0)))
    win = jnp.concatenate([
        _pad_heads(w_in[:o1], SWA_HEADS, SWA_HEAD_DIM, 0),
        _pad_heads(w_in[o1:o2], SWA_KV_HEADS, SWA_HEAD_DIM, 0),
        _pad_heads(w_in[o2:o3], SWA_KV_HEADS, SWA_HEAD_DIM, 0),
        w_in[o3:o5], kr], axis=0)
    wqu = _pad_heads(w_q_up, MLA_HEADS, MLA_QK_DIM, 0)
    kv = w_kv_up.reshape(MLA_HEADS, MLA_NOPE_DIM + MLA_V_DIM, MLA_KV_RANK)
    wkv = jnp.concatenate([
        _pad_heads(kv[:, :MLA_NOPE_DIM].reshape(-1, MLA_KV_RANK), MLA_HEADS, MLA_NOPE_DIM, 0),
        _pad_heads(kv[:, MLA_NOPE_DIM:].reshape(-1, MLA_KV_RANK), MLA_HEADS, MLA_V_DIM, 0)], axis=0)
    return win, wqu, wkv


def _unpad_layer(dwin, dwqu, dwkv):
    d_w_in = jnp.concatenate([
        _unpad_heads(dwin[PO_QA:PO_KA], SWA_HEADS, SWA_HEAD_DIM, 0),
        _unpad_heads(dwin[PO_KA:PO_VA], SWA_KV_HEADS, SWA_HEAD_DIM, 0),
        _unpad_heads(dwin[PO_VA:PO_CQ], SWA_KV_HEADS, SWA_HEAD_DIM, 0),
        dwin[PO_CQ:PO_KR], dwin[PO_KR + MLA_NOPE_DIM:PO_KR + MLA_QK_DIM]], axis=0)
    d_w_q_up = _unpad_heads(dwqu, MLA_HEADS, MLA_QK_DIM, 0)
    dk = _unpad_heads(dwkv[:HP], MLA_HEADS, MLA_NOPE_DIM, 0).reshape(MLA_HEADS, MLA_NOPE_DIM, MLA_KV_RANK)
    dv = _unpad_heads(dwkv[HP:], MLA_HEADS, MLA_V_DIM, 0).reshape(MLA_HEADS, MLA_V_DIM, MLA_KV_RANK)
    d_w_kv_up = jnp.concatenate([dk, dv], axis=1).reshape(-1, MLA_KV_RANK)
    return d_w_in, d_w_q_up, d_w_kv_up


def _train_example(x, target, meta, vec, weights):
    s = x.shape[0]
    depth = vec["attn_norm"].shape[0]
    t = FRONT + N_META + s
    assert t % BLOCK == 0
    tabs = _rope_tables(t)
    h = jnp.concatenate([jnp.zeros((FRONT, D_MODEL), F32), meta, x], axis=0)
    tgt = jnp.concatenate([jnp.zeros((FRONT + N_META, D_MODEL), F32), target], axis=0)
    row = lambda v: v[None, :]

    saved = []
    for l in range(depth):
        win, wqu, wkv = _pad_layer(*weights.attn_in(l))
        g1, gq, gkv, g2, ga, gb = (row(vec[n][l]) for n in ("attn_norm", "q_norm", "kv_norm", "ffn_norm",
                                                            "out_norm_swa", "out_norm_mla"))
        sk = row(vec["sinks"][l])
        u, qa, ka, va, cq, ckv, qn, kvn, qb, kf, vb = _pre_fwd(h, g1, win, gq, wqu, gkv, wkv, tabs)
        oa = _swa_fwd(sk, qa, ka, va)
        ob, lse = weights.mla_fwd(l, qb, kf, vb)
        lse = jnp.moveaxis(lse.reshape(t, MLA_HEADS // MLA_HB, MLA_HB), 1, 0)
        wo = weights.w_o(l)
        h2, mix, u2 = _mix_fwd(h, oa, ob, ga, gb, wo, g2)
        wg, wu, wd = weights.ffn(l)
        h3, gt, up = _ffn_fwd(h2, u2, wg, wu, wd)
        saved.append((h, u, qa, ka, va, cq, ckv, qn, kvn, qb, kf, vb, oa, ob, lse, h2, mix, u2, gt, up,
                      win, wqu, wkv, wo, ga, gb, g1, gq, gkv, g2, sk, wg, wu, wd))
        h = h3

    dh, d_final, loss = _loss_bwd(h, row(vec["final_norm"]), tgt)

    grads = []
    for l in reversed(range(depth)):
        (h0, u, qa, ka, va, cq, ckv, qn, kvn, qb, kf, vb, oa, ob, lse, h2, mix, u2, gt, up,
         win, wqu, wkv, wo, ga, gb, g1, gq, gkv, g2, sk, wg, wu, wd) = saved[l]
        dff = wd.shape[0]
        act, dgu, dhb = _ffn_bwd_a(dh, gt, up, wd)
        weights.ffn_grads(l, _tn_matmul(dgu, u2, "dw_gate", (0, dff)), _tn_matmul(dgu, u2, "dw_up", (dff, dff)),
                          _tn_matmul(act, dhb, "dw_down"))
        dh2, dh2b, d_g2 = _ffn_bwd_b(dh, dgu, h2, g2, wg, wu)
        weights.attn_grads(l, w_o=_tn_matmul(mix, dh2b, "dw_o"))
        doa, dob, dl, d_ga, d_gb = _mix_bwd(dh2b, oa, ob, ga, gb, wo)
        dqa, dkc, dkp, dvc, dvp, dsink = _swa_bwd(sk, qa, ka, va, oa, doa)
        dqb, dkf, dvb = weights.mla_bwd(l, qb, kf, vb, dob, lse, dl)
        dh, dp, dqbo, dkvo, d_g1, d_gq, d_gkv = _pre_bwd(
            dh2, h0, cq, ckv, dqa, dkc, dkp, dvc, dvp, dqb, dkf, dvb,
            g1, win, gq, wqu, gkv, wkv, tabs)
        d_win = _tn_matmul(dp, u, "dw_in")
        d_wqu = _tn_matmul(dqbo, qn, "dw_q_up")
        d_wkv = _tn_matmul(dkvo, kvn, "dw_kv_up")
        weights.attn_grads(l, **dict(zip(ATTN_IN, _unpad_layer(d_win, d_wqu, d_wkv))))
        grads.append(dict(attn_norm=d_g1[0], q_norm=d_gq[0], kv_norm=d_gkv[0], sinks=dsink[:, 0], out_norm_swa=d_ga[0],
                          out_norm_mla=d_gb[0], ffn_norm=d_g2[0]))
    grads = grads[::-1]
    stacked = {k: jnp.stack([g[k] for g in grads]) for k in grads[0]}
    stacked["final_norm"] = d_final[0]
    return loss[0, 0], dh[FRONT + N_META:], dh[FRONT:FRONT + N_META], stacked


MESH = pl.DeviceIdType.MESH
ANY = pl.BlockSpec(memory_space=pl.ANY)


def _place():
    return lax.axis_index("x"), lax.axis_index("y"), lax.axis_index("c")


def _index(x, y, c):
    return 4 * x + 2 * y + c


def _comm_sems(n):
    return [pltpu.SemaphoreType.DMA((n, N_DEV - 1)), pltpu.SemaphoreType.DMA((n, N_DEV - 1)),
            pltpu.SemaphoreType.DMA((n,))]


class _gather_plan:
    def __init__(self, x_refs, out_refs, send_sems, recv_sems, local_sems):
        self.x_refs, self.out_refs = x_refs, out_refs
        self.send_sems, self.recv_sems, self.local_sems = send_sems, recv_sems, local_sems
        self.n = len(x_refs)

    def _where(self):
        x, y, c = _place()
        return (x, y, c), (x, y, 1 - c), [(1 - x, y), (x, 1 - y), (1 - x, 1 - y)], c

    def _copy(self, i, k, block, to, from_input=False):
        slot = self.out_refs[i].at[_index(*block)]
        return pltpu.make_async_remote_copy(
            src_ref=self.x_refs[i] if from_input else slot, dst_ref=slot,
            send_sem=self.send_sems.at[i, k], recv_sem=self.recv_sems.at[i, k], device_id=to, device_id_type=MESH)

    def _mine(self, i, me):
        return pltpu.make_async_copy(self.x_refs[i], self.out_refs[i].at[_index(*me)], self.local_sems.at[i])

    def _first(self, me, sibling, chips, c):
        out = [self._copy(i, 1 + j, me, (*chip, c), True) for j, chip in enumerate(chips) for i in range(self.n)]
        return out + [self._copy(i, 0, me, sibling, True) for i in range(self.n)]

    def start(self):
        me, sibling, chips, c = self._where()
        for i in range(self.n):
            self._mine(i, me).start()
        for cp in self._first(me, sibling, chips, c):
            cp.start()

    def forward(self):
        me, sibling, chips, c = self._where()
        for j, chip in enumerate(chips):
            for i in range(self.n):
                self._copy(i, 1 + j, (*chip, c), me).wait_recv()
                self._copy(i, 4 + j, (*chip, c), sibling).start()

    def finish(self):
        me, sibling, chips, c = self._where()
        for i in range(self.n):
            self._copy(i, 0, sibling, me).wait_recv()
            for j, chip in enumerate(chips):
                self._copy(i, 4 + j, (*chip, 1 - c), me).wait_recv()
        for cp in self._first(me, sibling, chips, c):
            cp.wait_send()
        for j, chip in enumerate(chips):
            for i in range(self.n):
                self._copy(i, 4 + j, (*chip, c), sibling).wait_send()
        for i in range(self.n):
            self._mine(i, me).wait()


class _exchange_plan:
    def __init__(self, in_refs, out_refs, send_sems, recv_sems, local_sems):
        self.in_refs, self.out_refs = in_refs, out_refs
        self.send_sems, self.recv_sems, self.local_sems = send_sems, recv_sems, local_sems
        self.n = len(in_refs)

    def _copies(self):
        x, y, c = _place()
        me = _index(x, y, c)
        mine = [pltpu.make_async_copy(self.in_refs[i].at[me], self.out_refs[i].at[me], self.local_sems.at[i])
                for i in range(self.n)]
        remote = []
        for k in range(1, N_DEV):
            peer = (1 - x if k & 4 else x, 1 - y if k & 2 else y, 1 - c if k & 1 else c)
            remote += [pltpu.make_async_remote_copy(
                src_ref=self.in_refs[i].at[_index(*peer)], dst_ref=self.out_refs[i].at[me],
                send_sem=self.send_sems.at[i, k - 1], recv_sem=self.recv_sems.at[i, k - 1],
                device_id=peer, device_id_type=MESH) for i in range(self.n)]
        return mine, remote

    def start(self):
        mine, remote = self._copies()
        for cp in mine + remote:
            cp.start()

    def finish(self):
        mine, remote = self._copies()
        for cp in remote:
            cp.wait_recv()
        for cp in remote:
            cp.wait_send()
        for cp in mine:
            cp.wait()


def _all_gather(shards, name):
    n = len(shards)

    def body(*refs):
        plan = _gather_plan(refs[:n], refs[n:2 * n], *refs[2 * n:])
        plan.start()
        plan.forward()
        plan.finish()

    return pl.pallas_call(
        body, name=name, in_specs=[ANY] * n, out_specs=[ANY] * n, scratch_shapes=_comm_sems(n),
        out_shape=[jax.ShapeDtypeStruct((N_DEV,) + a.shape, a.dtype) for a in shards],
    )(*shards)


def _exchange(slabs, name):
    n = len(slabs)

    def body(*refs):
        plan = _exchange_plan(refs[:n], refs[n:2 * n], *refs[2 * n:])
        plan.start()
        plan.finish()

    return pl.pallas_call(
        body, name=name, in_specs=[ANY] * n, out_specs=[ANY] * n, scratch_shapes=_comm_sems(n),
        out_shape=[jax.ShapeDtypeStruct(a.shape, a.dtype) for a in slabs],
    )(*slabs)


def _adamw(w, g, m, v):
    m = ADAM_B1 * m + (1.0 - ADAM_B1) * g
    v = ADAM_B2 * v + (1.0 - ADAM_B2) * (g * g)
    m_hat = m / (1.0 - ADAM_B1 ** ADAM_STEP)
    v_hat = v / (1.0 - ADAM_B2 ** ADAM_STEP)
    return -ADAM_LR * (m_hat / (jnp.sqrt(v_hat) + ADAM_EPS) + ADAM_WD * w), m, v


def _sum_slots(ref):
    g = ref[0].astype(F32)
    for s in range(1, N_DEV):
        g = g + ref[s].astype(F32)
    return g


def _reduce_adamw(parts, w, m, v, name):
    l, r, c = w.shape
    tile = max([d for d in range(16, ADAM_ROWS + 1, 16) if r % d == 0], default=r)
    last = r // tile - 1

    def body(*refs):
        p_refs, (w_ref, m_ref, v_ref), (g_ref, d_ref, nm_ref, nv_ref) = refs[:l], refs[l:l + 3], refs[l + 3:]
        for layer in range(l):
            @pl.when(pl.program_id(0) == layer)
            def _(p_ref=p_refs[layer]):
                g = _sum_slots(p_ref)
                g_ref[...] = g
                d_ref[...], nm_ref[...], nv_ref[...] = _adamw(w_ref[...], g, m_ref[...], v_ref[...])

    def part_spec(layer):
        return pl.BlockSpec((N_DEV, tile, c),
                            lambda i, j: (0, jnp.where(i == layer, j, jnp.where(i < layer, 0, last)), 0))

    blk = pl.BlockSpec((None, tile, c), lambda i, j: (i, j, 0))
    return pl.pallas_call(
        body, name=name, grid=(l, r // tile),
        in_specs=[part_spec(layer) for layer in range(l)] + [blk, blk, blk], out_specs=[blk] * 4,
        out_shape=[jax.ShapeDtypeStruct((l, r, c), F32)] * 4,
        compiler_params=_params("arbitrary", "arbitrary"),
    )(*parts, w, m, v)


def _sum_parts(parts, name):
    _, r, c = parts.shape

    def body(p_ref, g_ref):
        g_ref[...] = _sum_slots(p_ref)

    return pl.pallas_call(body, name=name, out_shape=jax.ShapeDtypeStruct((r, c), F32))(parts)


def _adamw_call(w, g, m, v, name):
    def body(w_ref, g_ref, m_ref, v_ref, d_ref, nm_ref, nv_ref):
        d_ref[...], nm_ref[...], nv_ref[...] = _adamw(w_ref[...], g_ref[...], m_ref[...], v_ref[...])

    return pl.pallas_call(body, name=name, out_shape=[jax.ShapeDtypeStruct(w.shape, F32)] * 3)(w, g, m, v)


ATTN_IN = ("w_in", "w_q_up", "w_kv_up")
ATTN = ATTN_IN + ("w_o",)
FFN = ("w_gate", "w_up", "w_down")
TRANSPOSED = ("w_in", "w_q_up", "w_kv_up", "w_gate", "w_up")
SMALL = ("attn_norm", "ffn_norm", "final_norm", "out_norm_swa", "out_norm_mla", "q_norm", "kv_norm", "sinks")
PACK_W = 1024
SMALL_ROWS = 16


def _pack(arrs, dtype):
    flat = jnp.concatenate([a.astype(dtype).reshape(-1) for a in arrs])
    return flat.reshape(-1, PACK_W)


def _unpack(packed, like):
    flat = packed.reshape(-1)
    out, off = [], 0
    for a in like:
        out.append(flat[off:off + a.size].reshape(a.shape))
        off += a.size
    return out


def _gather_to_full(gathered):
    return gathered.reshape((-1,) + gathered.shape[2:])


def _full_to_slabs(full):
    return full.reshape((N_DEV, -1) + full.shape[1:])


class _ShardedWeights:
    def __init__(self, shards, depth, meta_shard):
        self.shards, self.depth = shards, depth
        self.gathered, self.pending, self.parts = {}, {}, {}
        first = _all_gather([shards[n][0] for n in ATTN_IN] + [meta_shard], "gather_attn0")
        self.gathered.update(zip([(n, 0) for n in ATTN_IN], first))
        self.meta = jnp.moveaxis(first[-1], 0, 1).reshape(N_META, D_MODEL)

    def _gather(self, keys, run):
        self.gathered.update(zip(keys, run([self.shards[n][l] for n, l in keys])))

    def _full(self, names, l):
        return tuple(_gather_to_full(self.gathered[n, l]) for n in names)

    def attn_in(self, l):
        return self._full(ATTN_IN, l)

    def w_o(self, l):
        return self._full(("w_o",), l)[0]

    def ffn(self, l):
        return self._full(FFN, l)

    def mla_fwd(self, l, q, k, v):
        keys = [(n, l) for n in ("w_o",) + FFN] + ([(n, l + 1) for n in ATTN_IN] if l + 1 < self.depth else [])
        out = []
        self._gather(keys, lambda xs: out.extend(_mla_fwd(q, k, v, xs)) or out[2])
        return out[0], out[1]

    def _add(self, names, l, grads):
        for n, g in zip(names, grads):
            self.pending[n, l] = _full_to_slabs(g)

    def ffn_grads(self, l, *grads):
        self._add(FFN, l, grads)

    def attn_grads(self, l, **grads):
        self._add(list(grads), l, grads.values())

    def _exchange(self, run):
        keys = list(self.pending)
        self.parts.update(zip(keys, run([self.pending.pop(k) for k in keys])))

    def mla_bwd(self, l, *args):
        out = []
        self._exchange(lambda xs: out.extend(_mla_bwd(*args, xs)) or out[1])
        return out[0]

    def flush(self):
        self._exchange(lambda xs: _exchange(xs, "exchange_attn0"))


def kernel(x, meta_tokens, attn_norm, w_in, q_norm, w_q_up, kv_norm, w_kv_up, sinks, out_norm_swa, out_norm_mla, w_o, ffn_norm, w_gate, w_up, w_down, final_norm, loss_target, m_meta_tokens, m_attn_norm, m_w_in, m_q_norm, m_w_q_up, m_kv_norm, m_w_kv_up, m_sinks, m_out_norm_swa, m_out_norm_mla, m_w_o, m_ffn_norm, m_w_gate, m_w_up, m_w_down, m_final_norm, v_meta_tokens, v_attn_norm, v_w_in, v_q_norm, v_w_q_up, v_kv_norm, v_w_kv_up, v_sinks, v_out_norm_swa, v_out_norm_mla, v_w_o, v_ffn_norm, v_w_gate, v_w_up, v_w_down, v_final_norm):
    w = dict(meta_tokens=meta_tokens, attn_norm=attn_norm, w_in=w_in, q_norm=q_norm, w_q_up=w_q_up, kv_norm=kv_norm,
             w_kv_up=w_kv_up, sinks=sinks, out_norm_swa=out_norm_swa, out_norm_mla=out_norm_mla, w_o=w_o,
             ffn_norm=ffn_norm, w_gate=w_gate, w_up=w_up, w_down=w_down, final_norm=final_norm)
    m = dict(meta_tokens=m_meta_tokens, attn_norm=m_attn_norm, w_in=m_w_in, q_norm=m_q_norm, w_q_up=m_w_q_up,
             kv_norm=m_kv_norm, w_kv_up=m_w_kv_up, sinks=m_sinks, out_norm_swa=m_out_norm_swa,
             out_norm_mla=m_out_norm_mla, w_o=m_w_o, ffn_norm=m_ffn_norm, w_gate=m_w_gate, w_up=m_w_up,
             w_down=m_w_down, final_norm=m_final_norm)
    v = dict(meta_tokens=v_meta_tokens, attn_norm=v_attn_norm, w_in=v_w_in, q_norm=v_q_norm, w_q_up=v_w_q_up,
             kv_norm=v_kv_norm, w_kv_up=v_w_kv_up, sinks=v_sinks, out_norm_swa=v_out_norm_swa,
             out_norm_mla=v_out_norm_mla, w_o=v_w_o, ffn_norm=v_ffn_norm, w_gate=v_w_gate, w_up=v_w_up,
             w_down=v_w_down, final_norm=v_final_norm)
    names = list(w)
    big = ATTN + FFN
    depth = w_in.shape[0]
    me = _index(*_place())

    as_held = lambda n, a: jnp.swapaxes(a, 1, 2) if n in TRANSPOSED else a
    weights = _ShardedWeights({n: as_held(n, w[n]).astype(BF16) for n in big}, depth, meta_tokens)
    loss, grad_x, d_meta, grads = _train_example(x[0], loss_target[0], weights.meta, {n: w[n] for n in SMALL}, weights)
    weights.flush()

    g_big, d_big, m_big, v_big = {}, {}, {}, {}
    for n in big:
        held = [as_held(n, a) for a in (w[n], m[n], v[n])]
        outs = _reduce_adamw([weights.parts[n, l] for l in range(depth)], *held, "reduce_adamw_" + n)
        g_big[n], d_big[n], m_big[n], v_big[n] = [as_held(n, a) for a in outs]

    small = [grads[n] for n in SMALL] + [loss.reshape(1)]
    pad = SMALL_ROWS * PACK_W - sum(a.size for a in small)
    part = jnp.concatenate([_pack(small + [jnp.zeros((pad,), F32)], F32), d_meta], axis=0)
    total = _sum_parts(_all_gather([part], "gather_small")[0], "sum_small")
    small_w = [w[n] for n in SMALL]
    packs = [_pack([d[n] for n in SMALL] + [jnp.zeros((pad + 1,), F32)], F32) for d in (w, m, v)]
    upd = _adamw_call(packs[0], total[:SMALL_ROWS], packs[1], packs[2], "adamw_small")
    g_small, d_small, m_small, v_small = [dict(zip(SMALL, _unpack(p, small_w))) for p in (total[:SMALL_ROWS],) + tuple(upd)]
    loss_total = total[:SMALL_ROWS].reshape(-1)[SMALL_ROWS * PACK_W - pad - 1]
    g_meta = lax.dynamic_slice_in_dim(total[SMALL_ROWS:], me * LANE, LANE, axis=1)
    d_mt, m_mt, v_mt = _adamw_call(meta_tokens, g_meta, m_meta_tokens, v_meta_tokens, "adamw_meta")

    outs = []
    for got in ({**g_big, **g_small, "meta_tokens": g_meta}, {**d_big, **d_small, "meta_tokens": d_mt},
                {**m_big, **m_small, "meta_tokens": m_mt}, {**v_big, **v_small, "meta_tokens": v_mt}):
        outs += [got[n] for n in names]
    return (loss_total, grad_x[None], *outs)
```

```python
import jax
import jax.numpy as jnp
from jax import lax
from jax.experimental import pallas as pl
from jax.experimental.pallas import tpu as pltpu

F32 = jnp.float32
BF16 = jnp.bfloat16

D_MODEL = 1024
N_META = 16
BLOCK = 128
FRONT = (-N_META) % BLOCK
ROPE_THETA = 10000.0
EPS = 1e-6
NEG = -1e30
SWA_HEADS = 8
SWA_KV_HEADS = 2
SWA_GROUP = SWA_HEADS // SWA_KV_HEADS
SWA_HEAD_DIM = 64
MLA_HEADS = 8
MLA_Q_RANK = 256
MLA_KV_RANK = 128
MLA_NOPE_DIM = 64
MLA_ROPE_DIM = 32
MLA_V_DIM = 64
MLA_QK_DIM = MLA_NOPE_DIM + MLA_ROPE_DIM
SWA_Q_W = SWA_HEADS * SWA_HEAD_DIM
SWA_KV_W = SWA_KV_HEADS * SWA_HEAD_DIM
MLA_OUT_W = MLA_HEADS * MLA_V_DIM
SCALE_A = SWA_HEAD_DIM ** -0.5
SCALE_B = MLA_QK_DIM ** -0.5
LOG2E = 1.4426950408889634
Q_SCALE = SCALE_B * LOG2E
ADAM_LR = 0.001
ADAM_B1 = 0.9
ADAM_B2 = 0.999
ADAM_EPS = 1e-08
ADAM_WD = 0.01
ADAM_STEP = 10

LANE = 128
N_DEV = 8
HP = 8 * LANE
PO_QA, PO_KA, PO_VA = 0, HP, HP + 2 * LANE
PO_CQ = PO_VA + 2 * LANE
PO_CKV = PO_CQ + MLA_Q_RANK
PO_KR = PO_CKV + MLA_KV_RANK
PW_IN = PO_KR + LANE
N_TAB = 7
VMEM_LIMIT = 56 * 2 ** 20
TN_VMEM_BUDGET = 36 * 2 ** 20
MLA_HB = 4
MLA_HB_FWD = 8
ADAM_ROWS = 256
HALF = LANE // 2
assert SWA_HEAD_DIM == HALF and MLA_V_DIM == HALF

NT = (((1,), (1,)), ((), ()))
TN = (((0,), (0,)), ((), ()))


def _tile(t):
    return 384 if t % 384 == 0 else 128


def _params(*sem):
    return pltpu.CompilerParams(dimension_semantics=sem, vmem_limit_bytes=VMEM_LIMIT)


def _row(tm, n):
    return pl.BlockSpec((tm, n), lambda i: (i, 0))


def _const(shape):
    return pl.BlockSpec(shape, lambda i: (0,) * len(shape))


def _dot(a, b):
    return jnp.dot(a, b, preferred_element_type=F32)


def _dot_nt(a, b):
    return lax.dot_general(a, b, NT, preferred_element_type=F32)


def _dot_tn(a, b):
    return lax.dot_general(a, b, TN, preferred_element_type=F32)


def _rope(x, c, s1, s2, shift):
    return x * c + pltpu.roll(x, LANE - shift, 1) * s1 + pltpu.roll(x, shift, 1) * s2


def _rope_t(dy, c, s1, s2, shift):
    return dy * c + pltpu.roll(dy * s1, shift, 1) + pltpu.roll(dy * s2, LANE - shift, 1)


def _rms_r(x, n):
    return lax.rsqrt(jnp.sum(x * x, axis=-1, keepdims=True) * (1.0 / n) + EPS)


def _rms_bwd(x, g, dy, n):
    r = _rms_r(x, n)
    xh = x * r
    dxh = dy * g
    dx = r * (dxh - xh * (jnp.sum(dxh * xh, axis=-1, keepdims=True) * (1.0 / n)))
    return dx, jnp.sum(dy * xh, axis=0, keepdims=True)


def _acc(ref, val, first):
    @pl.when(first)
    def _():
        ref[...] = val

    @pl.when(jnp.logical_not(first))
    def _():
        ref[...] += val


def _pair_half(slab, half):
    return slab if half == 0 else pltpu.roll(slab, HALF, 1)


def _unpack_pair(slab, half):
    x = _pair_half(slab, half)
    return jnp.where(lax.broadcasted_iota(jnp.int32, x.shape, 1) < HALF, x, 0.0)


def _tabs(tab_ref):
    return [tab_ref[:, LANE * i:LANE * (i + 1)] for i in range(N_TAB)]


def _pre_fwd(h, g1, win, gq, wqu, gkv, wkv, tabs):
    t = h.shape[0]
    tm = _tile(t)

    def body(h_ref, g1_ref, win_ref, gq_ref, wqu_ref, gkv_ref, wkv_ref, tab_ref,
             u_ref, qa_ref, ka_ref, va_ref, cq_ref, ckv_ref, qn_ref, kvn_ref, qb_ref, kf_ref, vb_ref):
        ca, sa1, sa2, cb, sb1, sb2, ck = _tabs(tab_ref)
        hv = h_ref[...]
        u = (hv * _rms_r(hv, D_MODEL) * g1_ref[...]).astype(BF16)
        u_ref[...] = u
        p = _dot_nt(u, win_ref[...])
        for c in range(SWA_HEADS):
            sl = slice(LANE * c, LANE * (c + 1))
            qa_ref[:, sl] = _rope(p[:, PO_QA + LANE * c:PO_QA + LANE * (c + 1)], ca, sa1, sa2, 32).astype(BF16)
        for c in range(SWA_KV_HEADS):
            sl = slice(LANE * c, LANE * (c + 1))
            ka_ref[:, sl] = _rope(p[:, PO_KA + LANE * c:PO_KA + LANE * (c + 1)], ca, sa1, sa2, 32).astype(BF16)
        va_ref[...] = p[:, PO_VA:PO_CQ].astype(BF16)
        cq = p[:, PO_CQ:PO_CKV]
        ckv = p[:, PO_CKV:PO_KR]
        cq_ref[...] = cq
        ckv_ref[...] = ckv
        qn = (cq * _rms_r(cq, MLA_Q_RANK) * gq_ref[...]).astype(BF16)
        qn_ref[...] = qn
        qb = _dot_nt(qn, wqu_ref[...])
        kvn = (ckv * _rms_r(ckv, MLA_KV_RANK) * gkv_ref[...]).astype(BF16)
        kvn_ref[...] = kvn
        kv = _dot_nt(kvn, wkv_ref[...])
        kr = _rope(p[:, PO_KR:PW_IN], ck, sb1, sb2, 16)
        for c in range(MLA_HEADS):
            sl = slice(LANE * c, LANE * (c + 1))
            qb_ref[:, sl] = (_rope(qb[:, sl], cb, sb1, sb2, 16) * Q_SCALE).astype(BF16)
            kf_ref[:, sl] = (kv[:, sl] + kr).astype(BF16)
        vb_ref[...] = kv[:, HP:].astype(BF16)

    widths = [(D_MODEL, BF16), (HP, BF16), (2 * LANE, BF16), (2 * LANE, BF16), (MLA_Q_RANK, F32),
              (MLA_KV_RANK, F32), (MLA_Q_RANK, BF16), (MLA_KV_RANK, BF16), (HP, BF16), (HP, BF16), (HP, BF16)]
    return pl.pallas_call(
        body, name="pre_fwd", grid=(t // tm,),
        in_specs=[_row(tm, D_MODEL), _const(g1.shape), _const(win.shape), _const(gq.shape), _const(wqu.shape),
                  _const(gkv.shape), _const(wkv.shape), _row(tm, N_TAB * LANE)],
        out_specs=[_row(tm, w) for w, _ in widths],
        out_shape=[jax.ShapeDtypeStruct((t, w), d) for w, d in widths],
        compiler_params=_params("parallel"),
    )(h, g1, win, gq, wqu, gkv, wkv, tabs)


def _swa_mask(nb):
    key = lax.broadcasted_iota(jnp.int32, (2 * BLOCK, SWA_GROUP * BLOCK), 0)
    qry = lax.broadcasted_iota(jnp.int32, (2 * BLOCK, SWA_GROUP * BLOCK), 1) & (BLOCK - 1)
    return (key > qry) & (key <= qry + BLOCK) & (key + (nb - 1) * BLOCK >= FRONT)


def _swa_group(ref, rows, j):
    return jnp.concatenate([ref[rows, LANE * (SWA_GROUP * j + g):LANE * (SWA_GROUP * j + g + 1)]
                            for g in range(SWA_GROUP)], axis=0)


def _swa_packed_group(ref, rows, j):
    heads = [SWA_GROUP * j + g for g in range(SWA_GROUP)]
    return jnp.concatenate([_pair_half(ref[rows, LANE * (hd // 2):LANE * (hd // 2 + 1)], hd % 2) for hd in heads], axis=0)


def _swa_sinks(sink_ref, j):
    return jnp.concatenate([jnp.full((1, BLOCK), sink_ref[0, SWA_GROUP * j + g], F32) for g in range(SWA_GROUP)], axis=1)


def _swa_keys(prev_ref, cur_ref, rb, j):
    sl = slice(LANE * j, LANE * (j + 1))
    if rb == 0:
        return jnp.concatenate([prev_ref[:, sl], cur_ref[:BLOCK, sl]], axis=0)
    return cur_ref[BLOCK * (rb - 1):BLOCK * (rb + 1), sl]


def _swa_chains(t):
    return [(rb, j) for rb in range(_tile(t) // BLOCK) for j in range(SWA_KV_HEADS)]


def _swa_scores(sink_ref, q_ref, kp_ref, kc_ref, n, t):
    r = _tile(t) // BLOCK
    chains = _swa_chains(t)
    qs = [_swa_group(q_ref, slice(BLOCK * rb, BLOCK * (rb + 1)), j) for rb, j in chains]
    ks = [_swa_keys(kp_ref, kc_ref, rb, j) for rb, j in chains]
    ss = [_dot_nt(k2, q4) for q4, k2 in zip(qs, ks)]
    masks = [_swa_mask(n * r + rb) for rb in range(r)]
    out = []
    for (rb, j), s in zip(chains, ss):
        sink = _swa_sinks(sink_ref, j)
        s = jnp.where(masks[rb], s * SCALE_A, NEG)
        m = jnp.maximum(jnp.max(s, axis=0, keepdims=True), sink)
        e = jnp.exp(s - m)
        es = jnp.exp(sink - m)
        inv = 1.0 / (jnp.sum(e, axis=0, keepdims=True) + es)
        out.append((e * inv, es * inv))
    return qs, ks, out


def _swa_specs(t):
    ts = _tile(t)
    r = ts // BLOCK
    prev = lambda n: (jnp.maximum(n * r - 1, 0), 0)
    cur = lambda n: (n, 0)
    return [pl.BlockSpec(memory_space=pltpu.SMEM), pl.BlockSpec((ts, HP), cur),
            pl.BlockSpec((BLOCK, 2 * LANE), prev), pl.BlockSpec((ts, 2 * LANE), cur),
            pl.BlockSpec((BLOCK, 2 * LANE), prev), pl.BlockSpec((ts, 2 * LANE), cur)]


def _swa_fwd(sinks, q, k, v):
    t = q.shape[0]
    ts = _tile(t)

    def body(sink_ref, q_ref, kp_ref, kc_ref, vp_ref, vc_ref, o_ref):
        chains = _swa_chains(t)
        _, _, probs = _swa_scores(sink_ref, q_ref, kp_ref, kc_ref, pl.program_id(0), t)
        os_ = [_dot_tn(_swa_keys(vp_ref, vc_ref, rb, j)[:, :HALF], p.astype(BF16)) for (rb, j), (p, _) in zip(chains, probs)]
        for (rb, j), o4 in zip(chains, os_):
            for g in range(0, SWA_GROUP, 2):
                pair = (SWA_GROUP * j + g) // 2
                o_ref[BLOCK * rb:BLOCK * (rb + 1), LANE * pair:LANE * (pair + 1)] = jnp.concatenate(
                    [o4[:, BLOCK * g:BLOCK * (g + 1)], o4[:, BLOCK * (g + 1):BLOCK * (g + 2)]], axis=0).T

    return pl.pallas_call(
        body, name="swa_fwd", grid=(t // ts,),
        in_specs=_swa_specs(t),
        out_specs=pl.BlockSpec((ts, SWA_Q_W), lambda n: (n, 0)),
        out_shape=jax.ShapeDtypeStruct((t, SWA_Q_W), F32),
        compiler_params=_params("parallel"),
    )(sinks, q, k, k, v, v)


def _causal_mask(q0, k0, tq, tk, transposed):
    if transposed:
        key = k0 + lax.broadcasted_iota(jnp.int32, (tk, tq), 0)
        qry = q0 + lax.broadcasted_iota(jnp.int32, (tk, tq), 1)
    else:
        qry = q0 + lax.broadcasted_iota(jnp.int32, (tq, tk), 0)
        key = k0 + lax.broadcasted_iota(jnp.int32, (tq, tk), 1)
    return (key <= qry) & (key >= FRONT)


def _heads(ref, hb, rows=slice(None)):
    return [ref[rows, LANE * a:LANE * (a + 1)] for a in range(hb)]


def _head_stats(t, hb=MLA_HB):
    return jax.ShapeDtypeStruct((MLA_HEADS // hb, t, hb), F32)


def _mla_fwd(q, k, v, shards=()):
    t = q.shape[0]
    tq = _tile(t)
    nq = t // tq
    n = len(shards)
    hb = MLA_HB_FWD
    steps = (MLA_HEADS // hb) * nq

    def body(q_ref, k_ref, v_ref, *rest):
        x_refs, (o_ref, lse_ref), out_refs = rest[:n], rest[n:n + 2], rest[n + 2:2 * n + 2]
        acc_sc, sems = rest[2 * n + 2], rest[2 * n + 3:]
        i = pl.program_id(1)
        step_id = pl.program_id(0) * nq + i
        if n:
            plan = _gather_plan(x_refs, out_refs, *sems)
            pl.when(step_id == 0)(plan.start)
            pl.when(step_id == (3 * steps) // 4)(plan.forward)
        qs = _heads(q_ref, hb)
        acc_sc[...] = jnp.zeros(acc_sc.shape, F32)

        def step(j, carry, masked):
            rows = pl.ds(pl.multiple_of(j * tq, tq), tq)
            ks = _heads(k_ref, hb, rows)
            vs = [v_ref[rows, LANE * a:LANE * a + HALF] for a in range(hb)]
            ss = [_dot_nt(kh, qh) for qh, kh in zip(qs, ks)]
            if masked:
                mask = _causal_mask(i * tq, j * tq, tq, tq, True)
                ss = [jnp.where(mask, s, NEG) for s in ss]
            mid, out = [], []
            for s, (m, l) in zip(ss, carry):
                mn = jnp.maximum(m, jnp.max(s, axis=0, keepdims=True))
                al = jnp.exp2(m - mn)
                p = jnp.exp2(s - mn)
                out.append((mn, al * l + jnp.sum(p, axis=0, keepdims=True)))
                mid.append((al, p.astype(BF16)))
            for a, ((al, p), vh) in enumerate(zip(mid, vs)):
                acc_sc[a] = al * acc_sc[a] + _dot_tn(vh, p)
            return tuple(out)

        init = ((jnp.full((1, tq), NEG, F32), jnp.zeros((1, tq), F32)),) * hb
        carry = lax.fori_loop(0, jnp.minimum(i, 1) + 1, lambda it, c: step(it * i, c, True), init)
        carry = lax.fori_loop(1, i, lambda j, c: step(j, c, False), carry)
        outs = [acc_sc[a] * (1.0 / l) for a, (_, l) in enumerate(carry)]
        for a in range(0, hb, 2):
            o_ref[:, HALF * a:HALF * (a + 2)] = jnp.concatenate(outs[a:a + 2], axis=0).T
        for a, (m, l) in enumerate(carry):
            lse_ref[:, a:a + 1] = jnp.broadcast_to(m + jnp.log2(l), (LANE, tq)).T[:, :1]
        if n:
            pl.when(step_id == steps - 1)(plan.finish)

    blk = pl.BlockSpec((tq, hb * LANE), lambda h, i: (i, h))
    full = pl.BlockSpec((t, hb * LANE), lambda h, i: (0, h))
    packed = pl.BlockSpec((tq, hb * HALF), lambda h, i: (i, h))
    out = pl.pallas_call(
        body, name="mla_fwd_gather" if n else "mla_fwd", grid=(MLA_HEADS // hb, nq),
        in_specs=[blk, full, full] + [ANY] * n,
        out_specs=[packed, pl.BlockSpec((None, tq, hb), lambda h, i: (h, i, 0))] + [ANY] * n,
        out_shape=[jax.ShapeDtypeStruct((t, MLA_OUT_W), F32), _head_stats(t, hb)]
        + [jax.ShapeDtypeStruct((N_DEV,) + a.shape, a.dtype) for a in shards],
        scratch_shapes=[pltpu.VMEM((hb, HALF, tq), F32)] + (_comm_sems(n) if n else []),
        compiler_params=_params("arbitrary", "arbitrary"),
    )(q, k, v, *shards)
    return out[0], out[1], out[2:]


def _mix_fwd(h, oa, ob, ga, gb, wo, g2):
    t = h.shape[0]
    tm = _tile(t)

    def body(h_ref, oa_ref, ob_ref, ga_ref, gb_ref, wo_ref, g2_ref, h2_ref, mix_ref, u2_ref):
        oa_v = oa_ref[...]
        ob_v = ob_ref[...]
        na = (oa_v * _rms_r(oa_v, SWA_Q_W) * ga_ref[...]).astype(BF16)
        nb = (ob_v * _rms_r(ob_v, MLA_OUT_W) * gb_ref[...]).astype(BF16)
        mix_ref[:, :SWA_Q_W] = na
        mix_ref[:, SWA_Q_W:] = nb
        h2 = h_ref[...] + _dot(na, wo_ref[:SWA_Q_W, :]) + _dot(nb, wo_ref[SWA_Q_W:, :])
        h2_ref[...] = h2
        u2_ref[...] = (h2 * _rms_r(h2, D_MODEL) * g2_ref[...]).astype(BF16)

    mix_w = SWA_Q_W + MLA_OUT_W
    return pl.pallas_call(
        body, name="mix_fwd", grid=(t // tm,),
        in_specs=[_row(tm, D_MODEL), _row(tm, SWA_Q_W), _row(tm, MLA_OUT_W), _const(ga.shape), _const(gb.shape),
                  _const(wo.shape), _const(g2.shape)],
        out_specs=[_row(tm, D_MODEL), _row(tm, mix_w), _row(tm, D_MODEL)],
        out_shape=[jax.ShapeDtypeStruct((t, D_MODEL), F32), jax.ShapeDtypeStruct((t, mix_w), BF16),
                   jax.ShapeDtypeStruct((t, D_MODEL), BF16)],
        compiler_params=_params("parallel"),
    )(h, oa, ob, ga, gb, wo, g2)


def _ffn_fwd(h2, u2, wg_t, wu_t, wd):
    t = h2.shape[0]
    tm = _tile(t)
    dff = wd.shape[0]

    def body(h2_ref, u2_ref, wg_ref, wu_ref, wd_ref, h3_ref, g_ref, up_ref):
        u2v = u2_ref[...]
        g = _dot_nt(u2v, wg_ref[...])
        up = _dot_nt(u2v, wu_ref[...])
        g_ref[...] = g.astype(BF16)
        up_ref[...] = up.astype(BF16)
        a = (g * jax.nn.sigmoid(g) * up).astype(BF16)
        h3_ref[...] = h2_ref[...] + _dot(a, wd_ref[...])

    return pl.pallas_call(
        body, name="ffn_fwd", grid=(t // tm,),
        in_specs=[_row(tm, D_MODEL), _row(tm, D_MODEL), _const(wg_t.shape), _const(wu_t.shape), _const(wd.shape)],
        out_specs=[_row(tm, D_MODEL), _row(tm, dff), _row(tm, dff)],
        out_shape=[jax.ShapeDtypeStruct((t, D_MODEL), F32), jax.ShapeDtypeStruct((t, dff), BF16),
                   jax.ShapeDtypeStruct((t, dff), BF16)],
        compiler_params=_params("parallel"),
    )(h2, u2, wg_t, wu_t, wd)


def _loss_bwd(h, gf, target):
    t = h.shape[0]
    tm = _tile(t)
    first_row = FRONT + N_META

    def body(h_ref, gf_ref, t_ref, dh_ref, dgf_ref, loss_ref):
        i = pl.program_id(0)
        hv = h_ref[...]
        y = hv * _rms_r(hv, D_MODEL) * gf_ref[...]
        row = i * tm + lax.broadcasted_iota(jnp.int32, (tm, 1), 0)
        err = jnp.where(row >= first_row, y - t_ref[...], 0.0)
        dx, dg = _rms_bwd(hv, gf_ref[...], err * (1.0 / D_MODEL), D_MODEL)
        dh_ref[...] = dx
        _acc(dgf_ref, dg, i == 0)
        part = 0.5 * jnp.sum(jnp.sum(err * err, axis=1, keepdims=True) * (1.0 / D_MODEL), axis=0, keepdims=True)
        _acc(loss_ref, jnp.broadcast_to(part, (1, LANE)), i == 0)

    return pl.pallas_call(
        body, name="loss_bwd", grid=(t // tm,),
        in_specs=[_row(tm, D_MODEL), _const(gf.shape), _row(tm, D_MODEL)],
        out_specs=[_row(tm, D_MODEL), _const((1, D_MODEL)), _const((1, LANE))],
        out_shape=[jax.ShapeDtypeStruct((t, D_MODEL), F32), jax.ShapeDtypeStruct((1, D_MODEL), F32),
                   jax.ShapeDtypeStruct((1, LANE), F32)],
        compiler_params=_params("arbitrary"),
    )(h, gf, target)


def _tn_matmul(a, b, name, cols=None, keep=None):
    t, n = b.shape
    first, k = cols or (0, a.shape[1])
    tk = next(c for c in (k, 1024, 512, 256, 128) if k % c == 0 and first % c == 0 and c <= 1024)
    fits = lambda c: 2 * (t * (tk + c) * 2 + tk * c * 2) <= TN_VMEM_BUDGET
    tn = next(c for c in (n, 1024, 512, 256, 128) if n % c == 0 and fits(c))
    kept = tk if keep is None else sum(size for _, size in keep)

    def body(a_ref, b_ref, o_ref):
        if keep is None:
            o_ref[...] = _dot_tn(a_ref[...], b_ref[...]).astype(BF16)
        else:
            at = a_ref[...].T
            at = jnp.concatenate([at[start:start + size] for start, size in keep], axis=0)
            o_ref[...] = _dot(at, b_ref[...]).astype(BF16)

    return pl.pallas_call(
        body, name=name, grid=(k // tk, n // tn),
        in_specs=[pl.BlockSpec((t, tk), lambda i, j: (0, i + first // tk)), pl.BlockSpec((t, tn), lambda i, j: (0, j))],
        out_specs=pl.BlockSpec((kept, tn), lambda i, j: (i, j)),
        out_shape=jax.ShapeDtypeStruct((k // tk * kept, n), BF16),
        compiler_params=_params("parallel", "parallel"),
    )(a, b)


def _ffn_bwd_a(dh3, g, up, wd):
    t = dh3.shape[0]
    tm = _tile(t)
    dff = wd.shape[0]

    def body(dh3_ref, g_ref, up_ref, wd_ref, a_ref, dgu_ref, dh3b_ref):
        dh3b = dh3_ref[...].astype(BF16)
        dh3b_ref[...] = dh3b
        da = _dot_nt(dh3b, wd_ref[...])
        gv = g_ref[...].astype(F32)
        upv = up_ref[...].astype(F32)
        sg = jax.nn.sigmoid(gv)
        silu = gv * sg
        a_ref[...] = (silu * upv).astype(BF16)
        dgu_ref[:, :dff] = (da * upv * (sg * (1.0 + gv * (1.0 - sg)))).astype(BF16)
        dgu_ref[:, dff:] = (da * silu).astype(BF16)

    return pl.pallas_call(
        body, name="ffn_bwd_a", grid=(t // tm,),
        in_specs=[_row(tm, D_MODEL), _row(tm, dff), _row(tm, dff), _const(wd.shape)],
        out_specs=[_row(tm, dff), _row(tm, 2 * dff), _row(tm, D_MODEL)],
        out_shape=[jax.ShapeDtypeStruct((t, dff), BF16), jax.ShapeDtypeStruct((t, 2 * dff), BF16),
                   jax.ShapeDtypeStruct((t, D_MODEL), BF16)],
        compiler_params=_params("parallel"),
    )(dh3, g, up, wd)


def _ffn_bwd_b(dh3, dgu, h2, g2, wg_t, wu_t):
    t = dh3.shape[0]
    tm = _tile(t)
    dff = wg_t.shape[0]

    def body(dh3_ref, dgu_ref, h2_ref, g2_ref, wg_ref, wu_ref, dh2_ref, dh2b_ref, dg2_ref):
        du2 = _dot(dgu_ref[:, :dff], wg_ref[...]) + _dot(dgu_ref[:, dff:], wu_ref[...])
        dx, dg = _rms_bwd(h2_ref[...], g2_ref[...], du2, D_MODEL)
        dh2 = dh3_ref[...] + dx
        dh2_ref[...] = dh2
        dh2b_ref[...] = dh2.astype(BF16)
        _acc(dg2_ref, dg, pl.program_id(0) == 0)

    return pl.pallas_call(
        body, name="ffn_bwd_b", grid=(t // tm,),
        in_specs=[_row(tm, D_MODEL), _row(tm, 2 * dff), _row(tm, D_MODEL), _const(g2.shape), _const(wg_t.shape),
                  _const(wu_t.shape)],
        out_specs=[_row(tm, D_MODEL), _row(tm, D_MODEL), _const((1, D_MODEL))],
        out_shape=[jax.ShapeDtypeStruct((t, D_MODEL), F32), jax.ShapeDtypeStruct((t, D_MODEL), BF16),
                   jax.ShapeDtypeStruct((1, D_MODEL), F32)],
        compiler_params=_params("arbitrary"),
    )(dh3, dgu, h2, g2, wg_t, wu_t)


def _mix_bwd(dh2, oa, ob, ga, gb, wo):
    t = dh2.shape[0]
    tm = _tile(t)

    def body(dh2_ref, oa_ref, ob_ref, ga_ref, gb_ref, wo_ref, doa_ref, dob_ref, dl_ref, dga_ref, dgb_ref):
        first = pl.program_id(0) == 0
        d = dh2_ref[...]
        ob_v = ob_ref[...]
        dxa, dga = _rms_bwd(oa_ref[...], ga_ref[...], _dot_nt(d, wo_ref[:SWA_Q_W, :]), SWA_Q_W)
        dxb, dgb = _rms_bwd(ob_v, gb_ref[...], _dot_nt(d, wo_ref[SWA_Q_W:, :]), MLA_OUT_W)
        lower = lax.broadcasted_iota(jnp.int32, (tm, LANE), 1) < HALF
        for hd in range(MLA_HEADS):
            sl = slice(LANE * (hd // 2), LANE * (hd // 2 + 1))
            mine = lower if hd % 2 == 0 else jnp.logical_not(lower)
            delta = jnp.sum(jnp.where(mine, ob_v[:, sl] * dxb[:, sl], 0.0), axis=1, keepdims=True)
            dl_ref[hd // MLA_HB, :, hd % MLA_HB:hd % MLA_HB + 1] = delta
        for ref, dx, heads in ((doa_ref, dxa, SWA_HEADS), (dob_ref, dxb, MLA_HEADS)):
            for hd in range(heads):
                slab = dx[:, LANE * (hd // 2):LANE * (hd // 2 + 1)]
                ref[:, LANE * hd:LANE * (hd + 1)] = _unpack_pair(slab, hd % 2).astype(BF16)
        _acc(dga_ref, dga, first)
        _acc(dgb_ref, dgb, first)

    return pl.pallas_call(
        body, name="mix_bwd", grid=(t // tm,),
        in_specs=[_row(tm, D_MODEL), _row(tm, SWA_Q_W), _row(tm, MLA_OUT_W), _const(ga.shape), _const(gb.shape),
                  _const(wo.shape)],
        out_specs=[_row(tm, HP), _row(tm, HP), pl.BlockSpec((MLA_HEADS // MLA_HB, tm, MLA_HB), lambda i: (0, i, 0)),
                   _const((1, SWA_Q_W)), _const((1, MLA_OUT_W))],
        out_shape=[jax.ShapeDtypeStruct((t, HP), BF16), jax.ShapeDtypeStruct((t, HP), BF16), _head_stats(t),
                   jax.ShapeDtypeStruct((1, SWA_Q_W), F32), jax.ShapeDtypeStruct((1, MLA_OUT_W), F32)],
        compiler_params=_params("arbitrary"),
    )(dh2, oa, ob, ga, gb, wo)


def _swa_bwd(sinks, q, k, v, o, do):
    t = q.shape[0]
    ts = _tile(t)

    def body(sink_ref, q_ref, kp_ref, kc_ref, vp_ref, vc_ref, o_ref, do_ref,
             dq_ref, dkc_ref, dkp_ref, dvc_ref, dvp_ref, dsink_ref):
        n = pl.program_id(0)
        chains = _swa_chains(t)
        qs, ks, probs = _swa_scores(sink_ref, q_ref, kp_ref, kc_ref, n, t)
        dos = [_swa_group(do_ref, slice(BLOCK * rb, BLOCK * (rb + 1)), j) for rb, j in chains]
        vs = [_swa_keys(vp_ref, vc_ref, rb, j) for rb, j in chains]
        dps = [_dot_nt(v2, do4) for do4, v2 in zip(dos, vs)]
        dss, dsks = [], []
        for (rb, j), (p, psink), do4, dp in zip(chains, probs, dos, dps):
            o4 = _swa_packed_group(o_ref, slice(BLOCK * rb, BLOCK * (rb + 1)), j)
            delta = jnp.sum(o4 * do4.astype(F32), axis=1, keepdims=True)
            delta = jnp.broadcast_to(delta, (SWA_GROUP * BLOCK, LANE)).T[:1, :]
            dss.append((p * (dp - delta) * SCALE_A).astype(BF16))
            dsks.append(-psink * delta)
        dqs = [_dot_tn(k2[:, :HALF], ds) for ds, k2 in zip(dss, ks)]
        dks = [_dot(ds, q4) for ds, q4 in zip(dss, qs)]
        dvs = [_dot(p.astype(BF16), do4) for (p, _), do4 in zip(probs, dos)]
        dsink = [jnp.zeros((1, LANE), F32)] * SWA_HEADS
        ext = {}
        for (rb, j), dq4, dk2, dv2, dsk in zip(chains, dqs, dks, dvs, dsks):
            for g in range(SWA_GROUP):
                hd = SWA_GROUP * j + g
                cols = slice(BLOCK * g, BLOCK * (g + 1))
                dq_ref[BLOCK * rb:BLOCK * (rb + 1), LANE * hd:LANE * (hd + 1)] = jnp.concatenate(
                    [dq4[:, cols], jnp.zeros((HALF, BLOCK), F32)], axis=0).T.astype(BF16)
                dsink[hd] = dsink[hd] + jnp.sum(dsk[:, cols], axis=1, keepdims=True)
            for half in range(2):
                key = (j, rb + half)
                part = (dk2[BLOCK * half:BLOCK * (half + 1)], dv2[BLOCK * half:BLOCK * (half + 1)])
                ext[key] = part if key not in ext else (ext[key][0] + part[0], ext[key][1] + part[1])
        for (j, blk), (dk, dv) in ext.items():
            sl = slice(LANE * j, LANE * (j + 1))
            if blk == 0:
                dkp_ref[:, sl] = dk
                dvp_ref[:, sl] = dv
            else:
                dkc_ref[BLOCK * (blk - 1):BLOCK * blk, sl] = dk
                dvc_ref[BLOCK * (blk - 1):BLOCK * blk, sl] = dv
        for hd in range(SWA_HEADS):
            _acc(dsink_ref.at[hd:hd + 1, :], jnp.broadcast_to(dsink[hd], (1, LANE)), n == 0)

    cur = lambda n: (n, 0)
    kv = pl.BlockSpec((ts, 2 * LANE), cur)
    kvp = pl.BlockSpec((BLOCK, 2 * LANE), cur)
    hp = pl.BlockSpec((ts, HP), cur)
    kvs = jax.ShapeDtypeStruct((t, 2 * LANE), F32)
    kvps = jax.ShapeDtypeStruct((t // ts * BLOCK, 2 * LANE), F32)
    return pl.pallas_call(
        body, name="swa_bwd", grid=(t // ts,),
        in_specs=_swa_specs(t) + [pl.BlockSpec((ts, SWA_Q_W), cur), hp],
        out_specs=[hp, kv, kvp, kv, kvp, _const((SWA_HEADS, LANE))],
        out_shape=[jax.ShapeDtypeStruct((t, HP), BF16), kvs, kvps, kvs, kvps,
                   jax.ShapeDtypeStruct((SWA_HEADS, LANE), F32)],
        compiler_params=_params("arbitrary"),
    )(sinks, q, k, k, v, v, o, do)


def _mla_bwd(q, k, v, do, lse, dl, slabs=()):
    t = q.shape[0]
    tq = _tile(t)
    nq = t // tq
    n = len(slabs)
    hb = MLA_HB
    steps = (MLA_HEADS // hb) * nq

    def body(k_ref, v_ref, q_ref, do_ref, lse_ref, dl_ref, *rest):
        in_refs, (dq_ref, dk_ref, dv_ref), out_refs = rest[:n], rest[n:n + 3], rest[n + 3:2 * n + 3]
        (dq_sc, dk_sc, dv_sc), sems = rest[2 * n + 3:2 * n + 6], rest[2 * n + 6:]
        j = pl.program_id(1)
        step_id = pl.program_id(0) * nq + j
        if n:
            plan = _exchange_plan(in_refs, out_refs, *sems)
            pl.when(step_id == 0)(plan.start)

        @pl.when(j == 0)
        def _():
            dq_sc[...] = jnp.zeros(dq_sc.shape, F32)

        dk_sc[...] = jnp.zeros(dk_sc.shape, F32)
        dv_sc[...] = jnp.zeros(dv_sc.shape, F32)
        ks, vs = _heads(k_ref, hb), _heads(v_ref, hb)

        def step(i, carry, masked):
            rows = pl.ds(pl.multiple_of(i * tq, tq), tq)
            qs, dos = _heads(q_ref, hb, rows), _heads(do_ref, hb, rows)
            ss = [_dot_nt(qh, kh) for qh, kh in zip(qs, ks)]
            dps = [_dot_nt(doh, vh) for doh, vh in zip(dos, vs)]
            if masked:
                mask = _causal_mask(i * tq, j * tq, tq, tq, False)
                ss = [jnp.where(mask, s_, NEG) for s_ in ss]
            ps = [jnp.exp2(s_ - lse_ref[rows, a:a + 1]) for a, s_ in enumerate(ss)]
            dss = [(p * (dp - dl_ref[rows, a:a + 1])).astype(BF16) for a, (p, dp) in enumerate(zip(ps, dps))]
            for a, (ds, p, qh, kh, doh) in enumerate(zip(dss, ps, qs, ks, dos)):
                dq_sc[a, rows, :] += _dot(ds, kh)
                dk_sc[a, :MLA_QK_DIM, :] += _dot_tn(qh[:, :MLA_QK_DIM], ds)
                dv_sc[a, :MLA_V_DIM, :] += _dot_tn(doh[:, :MLA_V_DIM], p.astype(BF16))
            return carry

        split = jnp.where(j == 0, nq, j + 1)
        lax.fori_loop(j, split, lambda i, c: step(i, c, True), 0)
        lax.fori_loop(split, nq, lambda i, c: step(i, c, False), 0)
        for a in range(hb):
            dk_ref[:, LANE * a:LANE * (a + 1)] = (dk_sc[a] * (1.0 / LOG2E)).T.astype(BF16)
            dv_ref[:, LANE * a:LANE * (a + 1)] = dv_sc[a].T.astype(BF16)

        @pl.when(j == nq - 1)
        def _():
            for a in range(hb):
                dq_ref[:, LANE * a:LANE * (a + 1)] = (dq_sc[a] * SCALE_B).astype(BF16)

        if n:
            pl.when(step_id == steps - 1)(plan.finish)

    blk = pl.BlockSpec((tq, hb * LANE), lambda h, j: (j, h))
    full = pl.BlockSpec((t, hb * LANE), lambda h, j: (0, h))
    cols = pl.BlockSpec((None, t, hb), lambda h, j: (h, 0, 0))
    out = pl.pallas_call(
        body, name="mla_bwd_exchange" if n else "mla_bwd", grid=(MLA_HEADS // hb, nq),
        in_specs=[blk, blk, full, full, cols, cols] + [ANY] * n, out_specs=[full, blk, blk] + [ANY] * n,
        out_shape=[jax.ShapeDtypeStruct((t, HP), BF16)] * 3 + [jax.ShapeDtypeStruct(a.shape, a.dtype) for a in slabs],
        scratch_shapes=[pltpu.VMEM((hb, t, LANE), F32)] + [pltpu.VMEM((hb, LANE, tq), F32)] * 2
        + (_comm_sems(n) if n else []),
        compiler_params=_params("arbitrary", "arbitrary"),
    )(k, v, q, do, lse, dl, *slabs)
    return out[:3], out[3:]


def _pre_bwd(dh2, h, cq, ckv, dqa, dka, dka_next, dva, dva_next, dqb, dkf, dvb, g1, win, gq, wqu, gkv, wkv, tabs):
    t = h.shape[0]
    tm = _tile(t)

    def body(dh2_ref, h_ref, cq_ref, ckv_ref, dqa_ref, dka_ref, dkan_ref, dva_ref, dvan_ref, dqb_ref, dkf_ref, dvb_ref,
             g1_ref, win_ref, gq_ref, wqu_ref, gkv_ref, wkv_ref, tab_ref,
             dh_ref, dp_ref, dqbo_ref, dkvo_ref, dg1_ref, dgq_ref, dgkv_ref):
        first = pl.program_id(0) == 0
        ca, sa1, sa2, cb, sb1, sb2, ck = _tabs(tab_ref)
        dkr = jnp.zeros((tm, LANE), F32)
        for c in range(MLA_HEADS):
            sl = slice(LANE * c, LANE * (c + 1))
            dqbo_ref[:, sl] = _rope_t(dqb_ref[:, sl].astype(F32), cb, sb1, sb2, 16).astype(BF16)
            dkr += dkf_ref[:, sl].astype(F32)
        dkvo_ref[:, :HP] = dkf_ref[...]
        dkvo_ref[:, HP:] = dvb_ref[...]
        dcq, dgq = _rms_bwd(cq_ref[...], gq_ref[...], _dot(dqbo_ref[...], wqu_ref[...]), MLA_Q_RANK)
        dckv, dgkv = _rms_bwd(ckv_ref[...], gkv_ref[...], _dot(dkvo_ref[...], wkv_ref[...]), MLA_KV_RANK)
        for c in range(SWA_HEADS):
            sl = slice(LANE * c, LANE * (c + 1))
            dp_ref[:, PO_QA + LANE * c:PO_QA + LANE * (c + 1)] = _rope_t(dqa_ref[:, sl].astype(F32), ca, sa1, sa2,
                                                                          32).astype(BF16)
        last = slice(tm - BLOCK, tm)
        more = pl.program_id(0) < t // tm - 1
        for c in range(SWA_KV_HEADS):
            sl = slice(LANE * c, LANE * (c + 1))
            dk = dka_ref[:, sl]
            dk_last = dk[tm - BLOCK:] + jnp.where(more, dkan_ref[:, sl], 0.0)
            cols = slice(PO_KA + LANE * c, PO_KA + LANE * (c + 1))
            if tm > BLOCK:
                dp_ref[:tm - BLOCK, cols] = _rope_t(dk[:tm - BLOCK], ca[:tm - BLOCK], sa1[:tm - BLOCK], sa2[:tm - BLOCK],
                                                    32).astype(BF16)
            dp_ref[last, cols] = _rope_t(dk_last, ca[tm - BLOCK:], sa1[tm - BLOCK:], sa2[tm - BLOCK:], 32).astype(BF16)
        if tm > BLOCK:
            dp_ref[:tm - BLOCK, PO_VA:PO_CQ] = dva_ref[:tm - BLOCK, :].astype(BF16)
        dp_ref[last, PO_VA:PO_CQ] = (dva_ref[tm - BLOCK:, :] + jnp.where(more, dvan_ref[...], 0.0)).astype(BF16)
        dp_ref[:, PO_CQ:PO_CKV] = dcq.astype(BF16)
        dp_ref[:, PO_CKV:PO_KR] = dckv.astype(BF16)
        dp_ref[:, PO_KR:PW_IN] = _rope_t(dkr, ck, sb1, sb2, 16).astype(BF16)
        dx, dg1 = _rms_bwd(h_ref[...], g1_ref[...], _dot(dp_ref[...], win_ref[...]), D_MODEL)
        dh_ref[...] = dh2_ref[...] + dx
        _acc(dg1_ref, dg1, first)
        _acc(dgq_ref, dgq, first)
        _acc(dgkv_ref, dgkv, first)

    kv = _row(tm, 2 * LANE)
    nxt = pl.BlockSpec((BLOCK, 2 * LANE), lambda i: (jnp.minimum(i + 1, t // tm - 1), 0))
    return pl.pallas_call(
        body, name="pre_bwd", grid=(t // tm,),
        in_specs=[_row(tm, D_MODEL), _row(tm, D_MODEL), _row(tm, MLA_Q_RANK), _row(tm, MLA_KV_RANK), _row(tm, HP),
                  kv, nxt, kv, nxt, _row(tm, HP), _row(tm, HP), _row(tm, HP),
                  _const(g1.shape), _const(win.shape), _const(gq.shape), _const(wqu.shape), _const(gkv.shape),
                  _const(wkv.shape), _row(tm, N_TAB * LANE)],
        out_specs=[_row(tm, D_MODEL), _row(tm, PW_IN), _row(tm, HP), _row(tm, 2 * HP),
                   _const((1, D_MODEL)), _const((1, MLA_Q_RANK)), _const((1, MLA_KV_RANK))],
        out_shape=[jax.ShapeDtypeStruct((t, D_MODEL), F32), jax.ShapeDtypeStruct((t, PW_IN), BF16),
                   jax.ShapeDtypeStruct((t, HP), BF16), jax.ShapeDtypeStruct((t, 2 * HP), BF16),
                   jax.ShapeDtypeStruct((1, D_MODEL), F32), jax.ShapeDtypeStruct((1, MLA_Q_RANK), F32),
                   jax.ShapeDtypeStruct((1, MLA_KV_RANK), F32)],
        compiler_params=_params("arbitrary"),
    )(dh2, h, cq, ckv, dqa, dka, dka_next, dva, dva_next, dqb, dkf, dvb, g1, win, gq, wqu, gkv, wkv, tabs)


def _rope_tables(t):
    pos = (jnp.arange(t, dtype=jnp.int32) - FRONT).astype(F32)[:, None]
    lane = jnp.arange(LANE)[None, :]

    def table(dim, start):
        half = dim // 2
        inv = ROPE_THETA ** (-jnp.arange(0, dim, 2, dtype=F32) / dim)
        ang = pos * inv[None, :]
        cos = jnp.concatenate([jnp.cos(ang)] * 2, axis=1)
        sin = jnp.concatenate([jnp.sin(ang)] * 2, axis=1)
        pad = lambda a: jnp.pad(a, ((0, 0), (start, LANE - start - dim)))
        first = (lane >= start) & (lane < start + half)
        second = (lane >= start + half) & (lane < start + dim)
        return pad(cos), jnp.where(first, -pad(sin), 0.0), jnp.where(second, pad(sin), 0.0)

    ca, sa1, sa2 = table(SWA_HEAD_DIM, 0)
    ck, sb1, sb2 = table(MLA_ROPE_DIM, MLA_NOPE_DIM)
    cb = jnp.where(lane < MLA_NOPE_DIM, 1.0, ck)
    return jnp.concatenate([ca, sa1, sa2, cb, sb1, sb2, ck], axis=1)


def _pad_heads(w, heads, dim, axis):
    shp = w.shape
    w = w.reshape(shp[:axis] + (heads, dim) + shp[axis + 1:])
    pad = [(0, 0)] * w.ndim
    pad[axis + 1] = (0, LANE - dim)
    return jnp.pad(w, pad).reshape(shp[:axis] + (heads * LANE,) + shp[axis + 1:])


def _unpad_heads(w, heads, dim, axis):
    shp = w.shape
    w = w.reshape(shp[:axis] + (heads, LANE) + shp[axis + 1:])
    w = lax.slice_in_dim(w, 0, dim, axis=axis + 1)
    return w.reshape(shp[:axis] + (heads * dim,) + shp[axis + 1:])


def _pad_layer(w_in, w_q_up, w_kv_up):
    o1 = SWA_Q_W
    o2 = o1 + SWA_KV_W
    o3 = o2 + SWA_KV_W
    o4 = o3 + MLA_Q_RANK
    o5 = o4 + MLA_KV_RANK
    kr = jnp.pad(w_in[o5:], ((MLA_NOPE_DIM, LANE - MLA_QK_DIM), (0, 0)))
    win = jnp.concatenate([
        _pad_heads(w_in[:o1], SWA_HEADS, SWA_HEAD_DIM, 0),
        _pad_heads(w_in[o1:o2], SWA_KV_HEADS, SWA_HEAD_DIM, 0),
        _pad_heads(w_in[o2:o3], SWA_KV_HEADS, SWA_HEAD_DIM, 0),
        w_in[o3:o5], kr], axis=0)
    wqu = _pad_heads(w_q_up, MLA_HEADS, MLA_QK_DIM, 0)
    kv = w_kv_up.reshape(MLA_HEADS, MLA_NOPE_DIM + MLA_V_DIM, MLA_KV_RANK)
    wkv = jnp.concatenate([
        _pad_heads(kv[:, :MLA_NOPE_DIM].reshape(-1, MLA_KV_RANK), MLA_HEADS, MLA_NOPE_DIM, 0),
        _pad_heads(kv[:, MLA_NOPE_DIM:].reshape(-1, MLA_KV_RANK), MLA_HEADS, MLA_V_DIM, 0)], axis=0)
    return win, wqu, wkv


IN_KEEP_SWA = [(LANE * hd, SWA_HEAD_DIM) for hd in range(SWA_HEADS)]
IN_KEEP_REST = ([(LANE * hd, SWA_HEAD_DIM) for hd in range(2 * SWA_KV_HEADS)] + [(PO_CQ - PO_KA, PO_KR - PO_CQ)]
                + [(PO_KR - PO_KA + MLA_NOPE_DIM, MLA_ROPE_DIM)])


def _unpad_layer(d_w_in, dwqu, dwkv):
    d_w_q_up = _unpad_heads(dwqu, MLA_HEADS, MLA_QK_DIM, 0)
    dk = _unpad_heads(dwkv[:, :HP], MLA_HEADS, MLA_NOPE_DIM, 1).reshape(MLA_KV_RANK, MLA_HEADS, MLA_NOPE_DIM)
    dv = _unpad_heads(dwkv[:, HP:], MLA_HEADS, MLA_V_DIM, 1).reshape(MLA_KV_RANK, MLA_HEADS, MLA_V_DIM)
    d_w_kv_up = jnp.concatenate([dk, dv], axis=2).reshape(MLA_KV_RANK, -1).T
    return d_w_in, d_w_q_up, d_w_kv_up


def _train_example(x, target, meta, vec, weights):
    s = x.shape[0]
    depth = vec["attn_norm"].shape[0]
    t = FRONT + N_META + s
    assert t % BLOCK == 0
    tabs = _rope_tables(t)
    h = jnp.concatenate([jnp.zeros((FRONT, D_MODEL), F32), meta, x], axis=0)
    tgt = jnp.concatenate([jnp.zeros((FRONT + N_META, D_MODEL), F32), target], axis=0)
    row = lambda v: v[None, :]

    saved = []
    for l in range(depth):
        win, wqu, wkv = _pad_layer(*weights.attn_in(l))
        g1, gq, gkv, g2, ga, gb = (row(vec[n][l]) for n in ("attn_norm", "q_norm", "kv_norm", "ffn_norm",
                                                            "out_norm_swa", "out_norm_mla"))
        sk = row(vec["sinks"][l])
        u, qa, ka, va, cq, ckv, qn, kvn, qb, kf, vb = _pre_fwd(h, g1, win, gq, wqu, gkv, wkv, tabs)
        oa = _swa_fwd(sk, qa, ka, va)
        ob, lse = weights.mla_fwd(l, qb, kf, vb)
        lse = jnp.moveaxis(lse.reshape(t, MLA_HEADS // MLA_HB, MLA_HB), 1, 0)
        wo = weights.w_o(l)
        h2, mix, u2 = _mix_fwd(h, oa, ob, ga, gb, wo, g2)
        wg, wu, wd = weights.ffn(l)
        h3, gt, up = _ffn_fwd(h2, u2, wg, wu, wd)
        saved.append((h, u, qa, ka, va, cq, ckv, qn, kvn, qb, kf, vb, oa, ob, lse, h2, mix, u2, gt, up,
                      win, wqu, wkv, wo, ga, gb, g1, gq, gkv, g2, sk, wg, wu, wd))
        h = h3

    dh, d_final, loss = _loss_bwd(h, row(vec["final_norm"]), tgt)

    grads = []
    for l in reversed(range(depth)):
        (h0, u, qa, ka, va, cq, ckv, qn, kvn, qb, kf, vb, oa, ob, lse, h2, mix, u2, gt, up,
         win, wqu, wkv, wo, ga, gb, g1, gq, gkv, g2, sk, wg, wu, wd) = saved[l]
        dff = wd.shape[0]
        act, dgu, dhb = _ffn_bwd_a(dh, gt, up, wd)
        weights.ffn_grads(l, _tn_matmul(dgu, u2, "dw_gate", (0, dff)), _tn_matmul(dgu, u2, "dw_up", (dff, dff)),
                          _tn_matmul(act, dhb, "dw_down"))
        dh2, dh2b, d_g2 = _ffn_bwd_b(dh, dgu, h2, g2, wg, wu)
        weights.attn_grads(l, w_o=_tn_matmul(mix, dh2b, "dw_o"))
        doa, dob, dl, d_ga, d_gb = _mix_bwd(dh2b, oa, ob, ga, gb, wo)
        dqa, dkc, dkp, dvc, dvp, dsink = _swa_bwd(sk, qa, ka, va, oa, doa)
        dqb, dkf, dvb = weights.mla_bwd(l, qb, kf, vb, dob, lse, dl)
        dh, dp, dqbo, dkvo, d_g1, d_gq, d_gkv = _pre_bwd(
            dh2, h0, cq, ckv, dqa, dkc, dkp, dvc, dvp, dqb, dkf, dvb,
            g1, win, gq, wqu, gkv, wkv, tabs)
        d_win = jnp.concatenate([_tn_matmul(dp, u, "dw_in_swa", (PO_QA, PO_KA), IN_KEEP_SWA),
                                 _tn_matmul(dp, u, "dw_in_rest", (PO_KA, PW_IN - PO_KA), IN_KEEP_REST)], axis=0)
        d_wqu = _tn_matmul(dqbo, qn, "dw_q_up")
        d_wkv = _tn_matmul(kvn, dkvo, "dw_kv_up")
        weights.attn_grads(l, **dict(zip(ATTN_IN, _unpad_layer(d_win, d_wqu, d_wkv))))
        grads.append(dict(attn_norm=d_g1[0], q_norm=d_gq[0], kv_norm=d_gkv[0], sinks=dsink[:, 0], out_norm_swa=d_ga[0],
                          out_norm_mla=d_gb[0], ffn_norm=d_g2[0]))
    grads = grads[::-1]
    stacked = {k: jnp.stack([g[k] for g in grads]) for k in grads[0]}
    stacked["final_norm"] = d_final[0]
    return loss[0, 0], dh[FRONT + N_META:], dh[FRONT:FRONT + N_META], stacked


MESH = pl.DeviceIdType.MESH
ANY = pl.BlockSpec(memory_space=pl.ANY)


def _place():
    return lax.axis_index("x"), lax.axis_index("y"), lax.axis_index("c")


def _index(x, y, c):
    return 4 * x + 2 * y + c


def _comm_sems(n):
    return [pltpu.SemaphoreType.DMA((n, N_DEV - 1)), pltpu.SemaphoreType.DMA((n, N_DEV - 1)),
            pltpu.SemaphoreType.DMA((n,))]


class _gather_plan:
    def __init__(self, x_refs, out_refs, send_sems, recv_sems, local_sems):
        self.x_refs, self.out_refs = x_refs, out_refs
        self.send_sems, self.recv_sems, self.local_sems = send_sems, recv_sems, local_sems
        self.n = len(x_refs)

    def _where(self):
        x, y, c = _place()
        return (x, y, c), (x, y, 1 - c), [(1 - x, y), (x, 1 - y), (1 - x, 1 - y)], c

    def _copy(self, i, k, block, to, from_input=False):
        slot = self.out_refs[i].at[_index(*block)]
        return pltpu.make_async_remote_copy(
            src_ref=self.x_refs[i] if from_input else slot, dst_ref=slot,
            send_sem=self.send_sems.at[i, k], recv_sem=self.recv_sems.at[i, k], device_id=to, device_id_type=MESH)

    def _mine(self, i, me):
        return pltpu.make_async_copy(self.x_refs[i], self.out_refs[i].at[_index(*me)], self.local_sems.at[i])

    def _first(self, me, sibling, chips, c):
        out = [self._copy(i, 1 + j, me, (*chip, c), True) for j, chip in enumerate(chips) for i in range(self.n)]
        return out + [self._copy(i, 0, me, sibling, True) for i in range(self.n)]

    def start(self):
        me, sibling, chips, c = self._where()
        for i in range(self.n):
            self._mine(i, me).start()
        for cp in self._first(me, sibling, chips, c):
            cp.start()

    def forward(self):
        me, sibling, chips, c = self._where()
        for j, chip in enumerate(chips):
            for i in range(self.n):
                self._copy(i, 1 + j, (*chip, c), me).wait_recv()
                self._copy(i, 4 + j, (*chip, c), sibling).start()

    def finish(self):
        me, sibling, chips, c = self._where()
        for i in range(self.n):
            self._copy(i, 0, sibling, me).wait_recv()
            for j, chip in enumerate(chips):
                self._copy(i, 4 + j, (*chip, 1 - c), me).wait_recv()
        for cp in self._first(me, sibling, chips, c):
            cp.wait_send()
        for j, chip in enumerate(chips):
            for i in range(self.n):
                self._copy(i, 4 + j, (*chip, c), sibling).wait_send()
        for i in range(self.n):
            self._mine(i, me).wait()


class _exchange_plan:
    def __init__(self, in_refs, out_refs, send_sems, recv_sems, local_sems):
        self.in_refs, self.out_refs = in_refs, out_refs
        self.send_sems, self.recv_sems, self.local_sems = send_sems, recv_sems, local_sems
        self.n = len(in_refs)

    def _copies(self):
        x, y, c = _place()
        me = _index(x, y, c)
        mine = [pltpu.make_async_copy(self.in_refs[i].at[me], self.out_refs[i].at[me], self.local_sems.at[i])
                for i in range(self.n)]
        remote = []
        for k in range(1, N_DEV):
            peer = (1 - x if k & 4 else x, 1 - y if k & 2 else y, 1 - c if k & 1 else c)
            remote += [pltpu.make_async_remote_copy(
                src_ref=self.in_refs[i].at[_index(*peer)], dst_ref=self.out_refs[i].at[me],
                send_sem=self.send_sems.at[i, k - 1], recv_sem=self.recv_sems.at[i, k - 1],
                device_id=peer, device_id_type=MESH) for i in range(self.n)]
        return mine, remote

    def start(self):
        mine, remote = self._copies()
        for cp in mine + remote:
            cp.start()

    def finish(self):
        mine, remote = self._copies()
        for cp in remote:
            cp.wait_recv()
        for cp in remote:
            cp.wait_send()
        for cp in mine:
            cp.wait()


def _all_gather(shards, name):
    n = len(shards)

    def body(*refs):
        plan = _gather_plan(refs[:n], refs[n:2 * n], *refs[2 * n:])
        plan.start()
        plan.forward()
        plan.finish()

    return pl.pallas_call(
        body, name=name, in_specs=[ANY] * n, out_specs=[ANY] * n, scratch_shapes=_comm_sems(n),
        out_shape=[jax.ShapeDtypeStruct((N_DEV,) + a.shape, a.dtype) for a in shards],
    )(*shards)


def _exchange(slabs, name):
    n = len(slabs)

    def body(*refs):
        plan = _exchange_plan(refs[:n], refs[n:2 * n], *refs[2 * n:])
        plan.start()
        plan.finish()

    return pl.pallas_call(
        body, name=name, in_specs=[ANY] * n, out_specs=[ANY] * n, scratch_shapes=_comm_sems(n),
        out_shape=[jax.ShapeDtypeStruct(a.shape, a.dtype) for a in slabs],
    )(*slabs)


def _adamw(w, g, m, v):
    m = ADAM_B1 * m + (1.0 - ADAM_B1) * g
    v = ADAM_B2 * v + (1.0 - ADAM_B2) * (g * g)
    m_hat = m / (1.0 - ADAM_B1 ** ADAM_STEP)
    v_hat = v / (1.0 - ADAM_B2 ** ADAM_STEP)
    return -ADAM_LR * (m_hat / (jnp.sqrt(v_hat) + ADAM_EPS) + ADAM_WD * w), m, v


def _sum_slots(ref):
    g = ref[0].astype(F32)
    for s in range(1, N_DEV):
        g = g + ref[s].astype(F32)
    return g


def _reduce_adamw(parts, w, m, v, name):
    l, r, c = w.shape
    tile = max([d for d in range(16, ADAM_ROWS + 1, 16) if r % d == 0], default=r)
    last = r // tile - 1

    def body(*refs):
        p_refs, (w_ref, m_ref, v_ref), (g_ref, d_ref, nm_ref, nv_ref) = refs[:l], refs[l:l + 3], refs[l + 3:]
        for layer in range(l):
            @pl.when(pl.program_id(0) == layer)
            def _(p_ref=p_refs[layer]):
                g = _sum_slots(p_ref)
                g_ref[...] = g
                d_ref[...], nm_ref[...], nv_ref[...] = _adamw(w_ref[...], g, m_ref[...], v_ref[...])

    def part_spec(layer):
        return pl.BlockSpec((N_DEV, tile, c),
                            lambda i, j: (0, jnp.where(i == layer, j, jnp.where(i < layer, 0, last)), 0))

    blk = pl.BlockSpec((None, tile, c), lambda i, j: (i, j, 0))
    return pl.pallas_call(
        body, name=name, grid=(l, r // tile),
        in_specs=[part_spec(layer) for layer in range(l)] + [blk, blk, blk], out_specs=[blk] * 4,
        out_shape=[jax.ShapeDtypeStruct((l, r, c), F32)] * 4,
        compiler_params=_params("arbitrary", "arbitrary"),
    )(*parts, w, m, v)


def _sum_parts(parts, name):
    _, r, c = parts.shape

    def body(p_ref, g_ref):
        g_ref[...] = _sum_slots(p_ref)

    return pl.pallas_call(body, name=name, out_shape=jax.ShapeDtypeStruct((r, c), F32))(parts)


def _adamw_call(w, g, m, v, name):
    def body(w_ref, g_ref, m_ref, v_ref, d_ref, nm_ref, nv_ref):
        d_ref[...], nm_ref[...], nv_ref[...] = _adamw(w_ref[...], g_ref[...], m_ref[...], v_ref[...])

    return pl.pallas_call(body, name=name, out_shape=[jax.ShapeDtypeStruct(w.shape, F32)] * 3)(w, g, m, v)


ATTN_IN = ("w_in", "w_q_up", "w_kv_up")
ATTN = ATTN_IN + ("w_o",)
FFN = ("w_gate", "w_up", "w_down")
TRANSPOSED = ("w_in", "w_q_up", "w_kv_up", "w_gate", "w_up")
SMALL = ("attn_norm", "ffn_norm", "final_norm", "out_norm_swa", "out_norm_mla", "q_norm", "kv_norm", "sinks")
PACK_W = 1024
SMALL_ROWS = 16


def _pack(arrs, dtype):
    flat = jnp.concatenate([a.astype(dtype).reshape(-1) for a in arrs])
    return flat.reshape(-1, PACK_W)


def _unpack(packed, like):
    flat = packed.reshape(-1)
    out, off = [], 0
    for a in like:
        out.append(flat[off:off + a.size].reshape(a.shape))
        off += a.size
    return out


def _gather_to_full(gathered):
    return gathered.reshape((-1,) + gathered.shape[2:])


def _full_to_slabs(full):
    return full.reshape((N_DEV, -1) + full.shape[1:])


class _ShardedWeights:
    def __init__(self, shards, depth, meta_shard):
        self.shards, self.depth = shards, depth
        self.gathered, self.pending, self.parts = {}, {}, {}
        first = _all_gather([shards[n][0] for n in ATTN_IN] + [meta_shard], "gather_attn0")
        self.gathered.update(zip([(n, 0) for n in ATTN_IN], first))
        self.meta = jnp.moveaxis(first[-1], 0, 1).reshape(N_META, D_MODEL)

    def _gather(self, keys, run):
        self.gathered.update(zip(keys, run([self.shards[n][l] for n, l in keys])))

    def _full(self, names, l):
        return tuple(_gather_to_full(self.gathered[n, l]) for n in names)

    def attn_in(self, l):
        return self._full(ATTN_IN, l)

    def w_o(self, l):
        return self._full(("w_o",), l)[0]

    def ffn(self, l):
        return self._full(FFN, l)

    def mla_fwd(self, l, q, k, v):
        keys = [(n, l) for n in ("w_o",) + FFN] + ([(n, l + 1) for n in ATTN_IN] if l + 1 < self.depth else [])
        out = []
        self._gather(keys, lambda xs: out.extend(_mla_fwd(q, k, v, xs)) or out[2])
        return out[0], out[1]

    def _add(self, names, l, grads):
        for n, g in zip(names, grads):
            self.pending[n, l] = _full_to_slabs(g)

    def ffn_grads(self, l, *grads):
        self._add(FFN, l, grads)

    def attn_grads(self, l, **grads):
        self._add(list(grads), l, grads.values())

    def _exchange(self, run):
        keys = list(self.pending)
        self.parts.update(zip(keys, run([self.pending.pop(k) for k in keys])))

    def mla_bwd(self, l, *args):
        out = []
        self._exchange(lambda xs: out.extend(_mla_bwd(*args, xs)) or out[1])
        return out[0]

    def flush(self):
        self._exchange(lambda xs: _exchange(xs, "exchange_attn0"))


def kernel(x, meta_tokens, attn_norm, w_in, q_norm, w_q_up, kv_norm, w_kv_up, sinks, out_norm_swa, out_norm_mla, w_o, ffn_norm, w_gate, w_up, w_down, final_norm, loss_target, m_meta_tokens, m_attn_norm, m_w_in, m_q_norm, m_w_q_up, m_kv_norm, m_w_kv_up, m_sinks, m_out_norm_swa, m_out_norm_mla, m_w_o, m_ffn_norm, m_w_gate, m_w_up, m_w_down, m_final_norm, v_meta_tokens, v_attn_norm, v_w_in, v_q_norm, v_w_q_up, v_kv_norm, v_w_kv_up, v_sinks, v_out_norm_swa, v_out_norm_mla, v_w_o, v_ffn_norm, v_w_gate, v_w_up, v_w_down, v_final_norm):
    w = dict(meta_tokens=meta_tokens, attn_norm=attn_norm, w_in=w_in, q_norm=q_norm, w_q_up=w_q_up, kv_norm=kv_norm,
             w_kv_up=w_kv_up, sinks=sinks, out_norm_swa=out_norm_swa, out_norm_mla=out_norm_mla, w_o=w_o,
             ffn_norm=ffn_norm, w_gate=w_gate, w_up=w_up, w_down=w_down, final_norm=final_norm)
    m = dict(meta_tokens=m_meta_tokens, attn_norm=m_attn_norm, w_in=m_w_in, q_norm=m_q_norm, w_q_up=m_w_q_up,
             kv_norm=m_kv_norm, w_kv_up=m_w_kv_up, sinks=m_sinks, out_norm_swa=m_out_norm_swa,
             out_norm_mla=m_out_norm_mla, w_o=m_w_o, ffn_norm=m_ffn_norm, w_gate=m_w_gate, w_up=m_w_up,
             w_down=m_w_down, final_norm=m_final_norm)
    v = dict(meta_tokens=v_meta_tokens, attn_norm=v_attn_norm, w_in=v_w_in, q_norm=v_q_norm, w_q_up=v_w_q_up,
             kv_norm=v_kv_norm, w_kv_up=v_w_kv_up, sinks=v_sinks, out_norm_swa=v_out_norm_swa,
             out_norm_mla=v_out_norm_mla, w_o=v_w_o, ffn_norm=v_ffn_norm, w_gate=v_w_gate, w_up=v_w_up,
             w_down=v_w_down, final_norm=v_final_norm)
    names = list(w)
    big = ATTN + FFN
    depth = w_in.shape[0]
    me = _index(*_place())

    as_held = lambda n, a: jnp.swapaxes(a, 1, 2) if n in TRANSPOSED else a
    weights = _ShardedWeights({n: as_held(n, w[n]).astype(BF16) for n in big}, depth, meta_tokens)
    loss, grad_x, d_meta, grads = _train_example(x[0], loss_target[0], weights.meta, {n: w[n] for n in SMALL}, weights)
    weights.flush()

    g_big, d_big, m_big, v_big = {}, {}, {}, {}
    for n in big:
        held = [as_held(n, a) for a in (w[n], m[n], v[n])]
        outs = _reduce_adamw([weights.parts[n, l] for l in range(depth)], *held, "reduce_adamw_" + n)
        g_big[n], d_big[n], m_big[n], v_big[n] = [as_held(n, a) for a in outs]

    small = [grads[n] for n in SMALL] + [loss.reshape(1)]
    pad = SMALL_ROWS * PACK_W - sum(a.size for a in small)
    part = jnp.concatenate([_pack(small + [jnp.zeros((pad,), F32)], F32), d_meta], axis=0)
    total = _sum_parts(_all_gather([part], "gather_small")[0], "sum_small")
    small_w = [w[n] for n in SMALL]
    packs = [_pack([d[n] for n in SMALL] + [jnp.zeros((pad + 1,), F32)], F32) for d in (w, m, v)]
    upd = _adamw_call(packs[0], total[:SMALL_ROWS], packs[1], packs[2], "adamw_small")
    g_small, d_small, m_small, v_small = [dict(zip(SMALL, _unpack(p, small_w))) for p in (total[:SMALL_ROWS],) + tuple(upd)]
    loss_total = total[:SMALL_ROWS].reshape(-1)[SMALL_ROWS * PACK_W - pad - 1]
    g_meta = lax.dynamic_slice_in_dim(total[SMALL_ROWS:], me * LANE, LANE, axis=1)
    d_mt, m_mt, v_mt = _adamw_call(meta_tokens, g_meta, m_meta_tokens, v_meta_tokens, "adamw_meta")

    outs = []
    for got in ({**g_big, **g_small, "meta_tokens": g_meta}, {**d_big, **d_small, "meta_tokens": d_mt},
                {**m_big, **m_small, "meta_tokens": m_mt}, {**v_big, **v_small, "meta_tokens": v_mt}):
        outs += [got[n] for n in names]
    return (loss_total, grad_x[None], *outs)
```

```python
import jax
import jax.numpy as jnp
from jax import lax
from jax.experimental import pallas as pl
from jax.experimental.pallas import tpu as pltpu

F32 = jnp.float32
BF16 = jnp.bfloat16

D_MODEL = 1024
N_META = 16
BLOCK = 128
FRONT = (-N_META) % BLOCK
ROPE_THETA = 10000.0
EPS = 1e-6
NEG = -1e30
SWA_HEADS = 8
SWA_KV_HEADS = 2
SWA_GROUP = SWA_HEADS // SWA_KV_HEADS
SWA_HEAD_DIM = 64
MLA_HEADS = 8
MLA_Q_RANK = 256
MLA_KV_RANK = 128
MLA_NOPE_DIM = 64
MLA_ROPE_DIM = 32
MLA_V_DIM = 64
MLA_QK_DIM = MLA_NOPE_DIM + MLA_ROPE_DIM
SWA_Q_W = SWA_HEADS * SWA_HEAD_DIM
SWA_KV_W = SWA_KV_HEADS * SWA_HEAD_DIM
MLA_OUT_W = MLA_HEADS * MLA_V_DIM
SCALE_A = SWA_HEAD_DIM ** -0.5
SCALE_B = MLA_QK_DIM ** -0.5
LOG2E = 1.4426950408889634
Q_SCALE = SCALE_B * LOG2E
ADAM_LR = 0.001
ADAM_B1 = 0.9
ADAM_B2 = 0.999
ADAM_EPS = 1e-08
ADAM_WD = 0.01
ADAM_STEP = 10

LANE = 128
N_DEV = 8
HP = 8 * LANE
PO_QA, PO_KA, PO_VA = 0, HP, HP + 2 * LANE
PO_CQ = PO_VA + 2 * LANE
PO_CKV = PO_CQ + MLA_Q_RANK
PO_KR = PO_CKV + MLA_KV_RANK
PW_IN = PO_KR + LANE
N_TAB = 7
VMEM_LIMIT = 56 * 2 ** 20
TN_VMEM_BUDGET = 36 * 2 ** 20
MLA_HB = 4
MLA_HB_FWD = 8
ADAM_ROWS = 256
HALF = LANE // 2
assert SWA_HEAD_DIM == HALF and MLA_V_DIM == HALF

NT = (((1,), (1,)), ((), ()))
TN = (((0,), (0,)), ((), ()))


def _tile(t):
    return 384 if t % 384 == 0 else 128


def _params(*sem):
    return pltpu.CompilerParams(dimension_semantics=sem, vmem_limit_bytes=VMEM_LIMIT)


def _row(tm, n):
    return pl.BlockSpec((tm, n), lambda i: (i, 0))


def _const(shape):
    return pl.BlockSpec(shape, lambda i: (0,) * len(shape))


def _dot(a, b):
    return jnp.dot(a, b, preferred_element_type=F32)


def _dot_nt(a, b):
    return lax.dot_general(a, b, NT, preferred_element_type=F32)


def _dot_tn(a, b):
    return lax.dot_general(a, b, TN, preferred_element_type=F32)


def _rope(x, c, s1, s2, shift):
    return x * c + pltpu.roll(x, LANE - shift, 1) * s1 + pltpu.roll(x, shift, 1) * s2


def _rope_t(dy, c, s1, s2, shift):
    return dy * c + pltpu.roll(dy * s1, shift, 1) + pltpu.roll(dy * s2, LANE - shift, 1)


def _rms_r(x, n):
    return lax.rsqrt(jnp.sum(x * x, axis=-1, keepdims=True) * (1.0 / n) + EPS)


def _rms_bwd(x, g, dy, n):
    r = _rms_r(x, n)
    xh = x * r
    dxh = dy * g
    dx = r * (dxh - xh * (jnp.sum(dxh * xh, axis=-1, keepdims=True) * (1.0 / n)))
    return dx, jnp.sum(dy * xh, axis=0, keepdims=True)


def _acc(ref, val, first):
    @pl.when(first)
    def _():
        ref[...] = val

    @pl.when(jnp.logical_not(first))
    def _():
        ref[...] += val


def _pair_half(slab, half):
    return slab if half == 0 else pltpu.roll(slab, HALF, 1)


def _unpack_pair(slab, half):
    x = _pair_half(slab, half)
    return jnp.where(lax.broadcasted_iota(jnp.int32, x.shape, 1) < HALF, x, 0.0)


def _tabs(tab_ref):
    return [tab_ref[:, LANE * i:LANE * (i + 1)] for i in range(N_TAB)]


def _pre_fwd(h, g1, win, gq, wqu, gkv, wkv, tabs):
    t = h.shape[0]
    tm = _tile(t)

    def body(h_ref, g1_ref, win_ref, gq_ref, wqu_ref, gkv_ref, wkv_ref, tab_ref,
             u_ref, qa_ref, ka_ref, va_ref, cq_ref, ckv_ref, qn_ref, kvn_ref, qb_ref, kf_ref, vb_ref):
        ca, sa1, sa2, cb, sb1, sb2, ck = _tabs(tab_ref)
        hv = h_ref[...]
        u = (hv * _rms_r(hv, D_MODEL) * g1_ref[...]).astype(BF16)
        u_ref[...] = u
        p = _dot_nt(u, win_ref[...])
        for c in range(SWA_HEADS):
            sl = slice(LANE * c, LANE * (c + 1))
            qa_ref[:, sl] = _rope(p[:, PO_QA + LANE * c:PO_QA + LANE * (c + 1)], ca, sa1, sa2, 32).astype(BF16)
        for c in range(SWA_KV_HEADS):
            sl = slice(LANE * c, LANE * (c + 1))
            ka_ref[:, sl] = _rope(p[:, PO_KA + LANE * c:PO_KA + LANE * (c + 1)], ca, sa1, sa2, 32).astype(BF16)
        va_ref[...] = p[:, PO_VA:PO_CQ].astype(BF16)
        cq = p[:, PO_CQ:PO_CKV]
        ckv = p[:, PO_CKV:PO_KR]
        cq_ref[...] = cq
        ckv_ref[...] = ckv
        qn = (cq * _rms_r(cq, MLA_Q_RANK) * gq_ref[...]).astype(BF16)
        qn_ref[...] = qn
        qb = _dot_nt(qn, wqu_ref[...])
        kvn = (ckv * _rms_r(ckv, MLA_KV_RANK) * gkv_ref[...]).astype(BF16)
        kvn_ref[...] = kvn
        kv = _dot_nt(kvn, wkv_ref[...])
        kr = _rope(p[:, PO_KR:PW_IN], ck, sb1, sb2, 16)
        for c in range(MLA_HEADS):
            sl = slice(LANE * c, LANE * (c + 1))
            qb_ref[:, sl] = (_rope(qb[:, sl], cb, sb1, sb2, 16) * Q_SCALE).astype(BF16)
            kf_ref[:, sl] = (kv[:, sl] + kr).astype(BF16)
        vb_ref[...] = kv[:, HP:].astype(BF16)

    widths = [(D_MODEL, BF16), (HP, BF16), (2 * LANE, BF16), (2 * LANE, BF16), (MLA_Q_RANK, F32),
              (MLA_KV_RANK, F32), (MLA_Q_RANK, BF16), (MLA_KV_RANK, BF16), (HP, BF16), (HP, BF16), (HP, BF16)]
    return pl.pallas_call(
        body, name="pre_fwd", grid=(t // tm,),
        in_specs=[_row(tm, D_MODEL), _const(g1.shape), _const(win.shape), _const(gq.shape), _const(wqu.shape),
                  _const(gkv.shape), _const(wkv.shape), _row(tm, N_TAB * LANE)],
        out_specs=[_row(tm, w) for w, _ in widths],
        out_shape=[jax.ShapeDtypeStruct((t, w), d) for w, d in widths],
        compiler_params=_params("parallel"),
    )(h, g1, win, gq, wqu, gkv, wkv, tabs)


def _swa_mask(nb):
    key = lax.broadcasted_iota(jnp.int32, (2 * BLOCK, SWA_GROUP * BLOCK), 0)
    qry = lax.broadcasted_iota(jnp.int32, (2 * BLOCK, SWA_GROUP * BLOCK), 1) & (BLOCK - 1)
    return (key > qry) & (key <= qry + BLOCK) & (key + (nb - 1) * BLOCK >= FRONT)


def _swa_group(ref, rows, j):
    return jnp.concatenate([ref[rows, LANE * (SWA_GROUP * j + g):LANE * (SWA_GROUP * j + g + 1)]
                            for g in range(SWA_GROUP)], axis=0)


def _swa_packed_group(ref, rows, j):
    heads = [SWA_GROUP * j + g for g in range(SWA_GROUP)]
    return jnp.concatenate([_pair_half(ref[rows, LANE * (hd // 2):LANE * (hd // 2 + 1)], hd % 2) for hd in heads], axis=0)


def _swa_sinks(sink_ref, j):
    return jnp.concatenate([jnp.full((1, BLOCK), sink_ref[0, SWA_GROUP * j + g], F32) for g in range(SWA_GROUP)], axis=1)


def _swa_keys(prev_ref, cur_ref, rb, j):
    sl = slice(LANE * j, LANE * (j + 1))
    if rb == 0:
        return jnp.concatenate([prev_ref[:, sl], cur_ref[:BLOCK, sl]], axis=0)
    return cur_ref[BLOCK * (rb - 1):BLOCK * (rb + 1), sl]


def _swa_chains(t):
    return [(rb, j) for rb in range(_tile(t) // BLOCK) for j in range(SWA_KV_HEADS)]


def _swa_scores(sink_ref, q_ref, kp_ref, kc_ref, n, t):
    r = _tile(t) // BLOCK
    chains = _swa_chains(t)
    qs = [_swa_group(q_ref, slice(BLOCK * rb, BLOCK * (rb + 1)), j) for rb, j in chains]
    ks = [_swa_keys(kp_ref, kc_ref, rb, j) for rb, j in chains]
    ss = [_dot_nt(k2, q4) for q4, k2 in zip(qs, ks)]
    masks = [_swa_mask(n * r + rb) for rb in range(r)]
    out = []
    for (rb, j), s in zip(chains, ss):
        sink = _swa_sinks(sink_ref, j)
        s = jnp.where(masks[rb], s * SCALE_A, NEG)
        m = jnp.maximum(jnp.max(s, axis=0, keepdims=True), sink)
        e = jnp.exp(s - m)
        es = jnp.exp(sink - m)
        inv = 1.0 / (jnp.sum(e, axis=0, keepdims=True) + es)
        out.append((e * inv, es * inv))
    return qs, ks, out


def _swa_specs(t):
    ts = _tile(t)
    r = ts // BLOCK
    prev = lambda n: (jnp.maximum(n * r - 1, 0), 0)
    cur = lambda n: (n, 0)
    return [pl.BlockSpec(memory_space=pltpu.SMEM), pl.BlockSpec((ts, HP), cur),
            pl.BlockSpec((BLOCK, 2 * LANE), prev), pl.BlockSpec((ts, 2 * LANE), cur),
            pl.BlockSpec((BLOCK, 2 * LANE), prev), pl.BlockSpec((ts, 2 * LANE), cur)]


def _swa_fwd(sinks, q, k, v):
    t = q.shape[0]
    ts = _tile(t)

    def body(sink_ref, q_ref, kp_ref, kc_ref, vp_ref, vc_ref, o_ref):
        chains = _swa_chains(t)
        _, _, probs = _swa_scores(sink_ref, q_ref, kp_ref, kc_ref, pl.program_id(0), t)
        os_ = [_dot_tn(_swa_keys(vp_ref, vc_ref, rb, j)[:, :HALF], p.astype(BF16)) for (rb, j), (p, _) in zip(chains, probs)]
        for (rb, j), o4 in zip(chains, os_):
            for g in range(0, SWA_GROUP, 2):
                pair = (SWA_GROUP * j + g) // 2
                o_ref[BLOCK * rb:BLOCK * (rb + 1), LANE * pair:LANE * (pair + 1)] = jnp.concatenate(
                    [o4[:, BLOCK * g:BLOCK * (g + 1)], o4[:, BLOCK * (g + 1):BLOCK * (g + 2)]], axis=0).T

    return pl.pallas_call(
        body, name="swa_fwd", grid=(t // ts,),
        in_specs=_swa_specs(t),
        out_specs=pl.BlockSpec((ts, SWA_Q_W), lambda n: (n, 0)),
        out_shape=jax.ShapeDtypeStruct((t, SWA_Q_W), F32),
        compiler_params=_params("parallel"),
    )(sinks, q, k, k, v, v)


def _causal_mask(q0, k0, tq, tk, transposed):
    if transposed:
        key = k0 + lax.broadcasted_iota(jnp.int32, (tk, tq), 0)
        qry = q0 + lax.broadcasted_iota(jnp.int32, (tk, tq), 1)
    else:
        qry = q0 + lax.broadcasted_iota(jnp.int32, (tq, tk), 0)
        key = k0 + lax.broadcasted_iota(jnp.int32, (tq, tk), 1)
    return (key <= qry) & (key >= FRONT)


def _heads(ref, hb, rows=slice(None)):
    return [ref[rows, LANE * a:LANE * (a + 1)] for a in range(hb)]


def _head_stats(t, hb=MLA_HB):
    return jax.ShapeDtypeStruct((MLA_HEADS // hb, t, hb), F32)


def _mla_fwd(q, k, v, shards=()):
    t = q.shape[0]
    tq = _tile(t)
    nq = t // tq
    n = len(shards)
    hb = MLA_HB_FWD
    steps = (MLA_HEADS // hb) * nq

    def body(q_ref, k_ref, v_ref, *rest):
        x_refs, (o_ref, lse_ref), out_refs = rest[:n], rest[n:n + 2], rest[n + 2:2 * n + 2]
        acc_sc, sems = rest[2 * n + 2], rest[2 * n + 3:]
        i = pl.program_id(1)
        step_id = pl.program_id(0) * nq + i
        if n:
            plan = _gather_plan(x_refs, out_refs, *sems)
            pl.when(step_id == 0)(plan.start)
            pl.when(step_id == (3 * steps) // 4)(plan.forward)
        qs = _heads(q_ref, hb)
        acc_sc[...] = jnp.zeros(acc_sc.shape, F32)

        def step(j, carry, masked):
            rows = pl.ds(pl.multiple_of(j * tq, tq), tq)
            ks = _heads(k_ref, hb, rows)
            vs = [v_ref[rows, LANE * a:LANE * a + HALF] for a in range(hb)]
            ss = [_dot_nt(kh, qh) for qh, kh in zip(qs, ks)]
            if masked:
                mask = _causal_mask(i * tq, j * tq, tq, tq, True)
                ss = [jnp.where(mask, s, NEG) for s in ss]
            mid, out = [], []
            for s, (m, l) in zip(ss, carry):
                mn = jnp.maximum(m, jnp.max(s, axis=0, keepdims=True))
                al = jnp.exp2(m - mn)
                p = jnp.exp2(s - mn)
                out.append((mn, al * l + jnp.sum(p, axis=0, keepdims=True)))
                mid.append((al, p.astype(BF16)))
            for a, ((al, p), vh) in enumerate(zip(mid, vs)):
                acc_sc[a] = al * acc_sc[a] + _dot_tn(vh, p)
            return tuple(out)

        init = ((jnp.full((1, tq), NEG, F32), jnp.zeros((1, tq), F32)),) * hb
        carry = lax.fori_loop(0, jnp.minimum(i, 1) + 1, lambda it, c: step(it * i, c, True), init)
        carry = lax.fori_loop(1, i, lambda j, c: step(j, c, False), carry)
        outs = [acc_sc[a] * (1.0 / l) for a, (_, l) in enumerate(carry)]
        for a in range(0, hb, 2):
            o_ref[:, HALF * a:HALF * (a + 2)] = jnp.concatenate(outs[a:a + 2], axis=0).T
        for a, (m, l) in enumerate(carry):
            lse_ref[:, a:a + 1] = jnp.broadcast_to(m + jnp.log2(l), (LANE, tq)).T[:, :1]
        if n:
            pl.when(step_id == steps - 1)(plan.finish)

    blk = pl.BlockSpec((tq, hb * LANE), lambda h, i: (i, h))
    full = pl.BlockSpec((t, hb * LANE), lambda h, i: (0, h))
    packed = pl.BlockSpec((tq, hb * HALF), lambda h, i: (i, h))
    out = pl.pallas_call(
        body, name="mla_fwd_gather" if n else "mla_fwd", grid=(MLA_HEADS // hb, nq),
        in_specs=[blk, full, full] + [ANY] * n,
        out_specs=[packed, pl.BlockSpec((None, tq, hb), lambda h, i: (h, i, 0))] + [ANY] * n,
        out_shape=[jax.ShapeDtypeStruct((t, MLA_OUT_W), F32), _head_stats(t, hb)]
        + [jax.ShapeDtypeStruct((N_DEV,) + a.shape, a.dtype) for a in shards],
        scratch_shapes=[pltpu.VMEM((hb, HALF, tq), F32)] + (_comm_sems(n) if n else []),
        compiler_params=_params("arbitrary", "arbitrary"),
    )(q, k, v, *shards)
    return out[0], out[1], out[2:]


def _mix_fwd(h, oa, ob, ga, gb, wo, g2):
    t = h.shape[0]
    tm = _tile(t)

    def body(h_ref, oa_ref, ob_ref, ga_ref, gb_ref, wo_ref, g2_ref, h2_ref, mix_ref, u2_ref):
        oa_v = oa_ref[...]
        ob_v = ob_ref[...]
        na = (oa_v * _rms_r(oa_v, SWA_Q_W) * ga_ref[...]).astype(BF16)
        nb = (ob_v * _rms_r(ob_v, MLA_OUT_W) * gb_ref[...]).astype(BF16)
        mix_ref[:, :SWA_Q_W] = na
        mix_ref[:, SWA_Q_W:] = nb
        h2 = h_ref[...] + _dot(na, wo_ref[:SWA_Q_W, :]) + _dot(nb, wo_ref[SWA_Q_W:, :])
        h2_ref[...] = h2
        u2_ref[...] = (h2 * _rms_r(h2, D_MODEL) * g2_ref[...]).astype(BF16)

    mix_w = SWA_Q_W + MLA_OUT_W
    return pl.pallas_call(
        body, name="mix_fwd", grid=(t // tm,),
        in_specs=[_row(tm, D_MODEL), _row(tm, SWA_Q_W), _row(tm, MLA_OUT_W), _const(ga.shape), _const(gb.shape),
                  _const(wo.shape), _const(g2.shape)],
        out_specs=[_row(tm, D_MODEL), _row(tm, mix_w), _row(tm, D_MODEL)],
        out_shape=[jax.ShapeDtypeStruct((t, D_MODEL), F32), jax.ShapeDtypeStruct((t, mix_w), BF16),
                   jax.ShapeDtypeStruct((t, D_MODEL), BF16)],
        compiler_params=_params("parallel"),
    )(h, oa, ob, ga, gb, wo, g2)


def _ffn_fwd(h2, u2, wg_t, wu_t, wd, shards=()):
    t = h2.shape[0]
    tm = _tile(t)
    dff = wd.shape[0]
    n = len(shards)
    steps = t // tm

    def body(h2_ref, u2_ref, wg_ref, wu_ref, wd_ref, *rest):
        x_refs, (h3_ref, g_ref, up_ref), out_refs, sems = rest[:n], rest[n:n + 3], rest[n + 3:2 * n + 3], rest[2 * n + 3:]
        if n:
            plan = _gather_plan(x_refs, out_refs, *sems)
            pl.when(pl.program_id(0) == 0)(plan.start)
            pl.when(pl.program_id(0) == (3 * steps) // 4)(plan.forward)
        u2v = u2_ref[...]
        g = _dot_nt(u2v, wg_ref[...])
        up = _dot_nt(u2v, wu_ref[...])
        g_ref[...] = g.astype(BF16)
        up_ref[...] = up.astype(BF16)
        a = (g * jax.nn.sigmoid(g) * up).astype(BF16)
        h3_ref[...] = h2_ref[...] + _dot(a, wd_ref[...])
        if n:
            pl.when(pl.program_id(0) == steps - 1)(plan.finish)

    out = pl.pallas_call(
        body, name="ffn_fwd_gather" if n else "ffn_fwd", grid=(steps,),
        in_specs=[_row(tm, D_MODEL), _row(tm, D_MODEL), _const(wg_t.shape), _const(wu_t.shape), _const(wd.shape)] + [ANY] * n,
        out_specs=[_row(tm, D_MODEL), _row(tm, dff), _row(tm, dff)] + [ANY] * n,
        out_shape=[jax.ShapeDtypeStruct((t, D_MODEL), F32), jax.ShapeDtypeStruct((t, dff), BF16),
                   jax.ShapeDtypeStruct((t, dff), BF16)] + [jax.ShapeDtypeStruct((N_DEV,) + a.shape, a.dtype) for a in shards],
        scratch_shapes=_comm_sems(n) if n else [],
        compiler_params=_params("arbitrary" if n else "parallel"),
    )(h2, u2, wg_t, wu_t, wd, *shards)
    return out[:3], out[3:]


def _loss_bwd(h, gf, target):
    t = h.shape[0]
    tm = _tile(t)
    r = tm // BLOCK
    assert FRONT + N_META == BLOCK and target.shape[0] == t - BLOCK

    def body(h_ref, gf_ref, *rest):
        t_refs, (dh_ref, dgf_ref, loss_ref) = rest[:r], rest[r:]
        i = pl.program_id(0)
        hv = h_ref[...]
        y = hv * _rms_r(hv, D_MODEL) * gf_ref[...]
        row = i * tm + lax.broadcasted_iota(jnp.int32, (tm, 1), 0)
        tv = jnp.concatenate([t_ref[...] for t_ref in t_refs], axis=0)
        err = jnp.where(row >= BLOCK, y - tv, 0.0)
        dx, dg = _rms_bwd(hv, gf_ref[...], err * (1.0 / D_MODEL), D_MODEL)
        dh_ref[...] = dx
        _acc(dgf_ref, dg, i == 0)
        part = 0.5 * jnp.sum(jnp.sum(err * err, axis=1, keepdims=True) * (1.0 / D_MODEL), axis=0, keepdims=True)
        _acc(loss_ref, jnp.broadcast_to(part, (1, LANE)), i == 0)

    t_specs = [pl.BlockSpec((BLOCK, D_MODEL), lambda i, b=b: (jnp.maximum(r * i + b - 1, 0), 0)) for b in range(r)]
    return pl.pallas_call(
        body, name="loss_bwd", grid=(t // tm,),
        in_specs=[_row(tm, D_MODEL), _const(gf.shape)] + t_specs,
        out_specs=[_row(tm, D_MODEL), _const((1, D_MODEL)), _const((1, LANE))],
        out_shape=[jax.ShapeDtypeStruct((t, D_MODEL), F32), jax.ShapeDtypeStruct((1, D_MODEL), F32),
                   jax.ShapeDtypeStruct((1, LANE), F32)],
        compiler_params=_params("arbitrary"),
    )(h, gf, *[target] * r)


def _tn_matmul(a, b, name, cols=None, keep=None):
    t, n = b.shape
    first, k = cols or (0, a.shape[1])
    tk = next(c for c in (k, 1024, 512, 256, 128) if k % c == 0 and first % c == 0 and c <= 1024)
    fits = lambda c: 2 * (t * (tk + c) * 2 + tk * c * 2) <= TN_VMEM_BUDGET
    tn = next(c for c in (n, 1024, 512, 256, 128) if n % c == 0 and fits(c))
    kept = tk if keep is None else sum(size for _, size in keep)

    def body(a_ref, b_ref, o_ref):
        if keep is None:
            o_ref[...] = _dot_tn(a_ref[...], b_ref[...]).astype(BF16)
        else:
            at = a_ref[...].T
            at = jnp.concatenate([at[start:start + size] for start, size in keep], axis=0)
            o_ref[...] = _dot(at, b_ref[...]).astype(BF16)

    return pl.pallas_call(
        body, name=name, grid=(k // tk, n // tn),
        in_specs=[pl.BlockSpec((t, tk), lambda i, j: (0, i + first // tk)), pl.BlockSpec((t, tn), lambda i, j: (0, j))],
        out_specs=pl.BlockSpec((kept, tn), lambda i, j: (i, j)),
        out_shape=jax.ShapeDtypeStruct((k // tk * kept, n), BF16),
        compiler_params=_params("parallel", "parallel"),
    )(a, b)


def _ffn_bwd_a(dh3, g, up, wd):
    t = dh3.shape[0]
    tm = _tile(t)
    dff = wd.shape[0]

    def body(dh3_ref, g_ref, up_ref, wd_ref, a_ref, dgu_ref, dh3b_ref):
        dh3b = dh3_ref[...].astype(BF16)
        dh3b_ref[...] = dh3b
        da = _dot_nt(dh3b, wd_ref[...])
        gv = g_ref[...].astype(F32)
        upv = up_ref[...].astype(F32)
        sg = jax.nn.sigmoid(gv)
        silu = gv * sg
        a_ref[...] = (silu * upv).astype(BF16)
        dgu_ref[:, :dff] = (da * upv * (sg * (1.0 + gv * (1.0 - sg)))).astype(BF16)
        dgu_ref[:, dff:] = (da * silu).astype(BF16)

    return pl.pallas_call(
        body, name="ffn_bwd_a", grid=(t // tm,),
        in_specs=[_row(tm, D_MODEL), _row(tm, dff), _row(tm, dff), _const(wd.shape)],
        out_specs=[_row(tm, dff), _row(tm, 2 * dff), _row(tm, D_MODEL)],
        out_shape=[jax.ShapeDtypeStruct((t, dff), BF16), jax.ShapeDtypeStruct((t, 2 * dff), BF16),
                   jax.ShapeDtypeStruct((t, D_MODEL), BF16)],
        compiler_params=_params("parallel"),
    )(dh3, g, up, wd)


def _ffn_bwd_b(dh3, dgu, h2, g2, wg_t, wu_t):
    t = dh3.shape[0]
    tm = _tile(t)
    dff = wg_t.shape[0]

    def body(dh3_ref, dgu_ref, h2_ref, g2_ref, wg_ref, wu_ref, dh2_ref, dh2b_ref, dg2_ref):
        du2 = _dot(dgu_ref[:, :dff], wg_ref[...]) + _dot(dgu_ref[:, dff:], wu_ref[...])
        dx, dg = _rms_bwd(h2_ref[...], g2_ref[...], du2, D_MODEL)
        dh2 = dh3_ref[...] + dx
        dh2_ref[...] = dh2
        dh2b_ref[...] = dh2.astype(BF16)
        _acc(dg2_ref, dg, pl.program_id(0) == 0)

    return pl.pallas_call(
        body, name="ffn_bwd_b", grid=(t // tm,),
        in_specs=[_row(tm, D_MODEL), _row(tm, 2 * dff), _row(tm, D_MODEL), _const(g2.shape), _const(wg_t.shape),
                  _const(wu_t.shape)],
        out_specs=[_row(tm, D_MODEL), _row(tm, D_MODEL), _const((1, D_MODEL))],
        out_shape=[jax.ShapeDtypeStruct((t, D_MODEL), F32), jax.ShapeDtypeStruct((t, D_MODEL), BF16),
                   jax.ShapeDtypeStruct((1, D_MODEL), F32)],
        compiler_params=_params("arbitrary"),
    )(dh3, dgu, h2, g2, wg_t, wu_t)


def _mix_bwd(dh2, oa, ob, ga, gb, wo):
    t = dh2.shape[0]
    tm = _tile(t)

    def body(dh2_ref, oa_ref, ob_ref, ga_ref, gb_ref, wo_ref, doa_ref, dob_ref, dl_ref, dga_ref, dgb_ref):
        first = pl.program_id(0) == 0
        d = dh2_ref[...]
        ob_v = ob_ref[...]
        dxa, dga = _rms_bwd(oa_ref[...], ga_ref[...], _dot_nt(d, wo_ref[:SWA_Q_W, :]), SWA_Q_W)
        dxb, dgb = _rms_bwd(ob_v, gb_ref[...], _dot_nt(d, wo_ref[SWA_Q_W:, :]), MLA_OUT_W)
        lower = lax.broadcasted_iota(jnp.int32, (tm, LANE), 1) < HALF
        for hd in range(MLA_HEADS):
            sl = slice(LANE * (hd // 2), LANE * (hd // 2 + 1))
            mine = lower if hd % 2 == 0 else jnp.logical_not(lower)
            delta = jnp.sum(jnp.where(mine, ob_v[:, sl] * dxb[:, sl], 0.0), axis=1, keepdims=True)
            dl_ref[hd // MLA_HB, :, hd % MLA_HB:hd % MLA_HB + 1] = delta
        for ref, dx, heads in ((doa_ref, dxa, SWA_HEADS), (dob_ref, dxb, MLA_HEADS)):
            for hd in range(heads):
                slab = dx[:, LANE * (hd // 2):LANE * (hd // 2 + 1)]
                ref[:, LANE * hd:LANE * (hd + 1)] = _unpack_pair(slab, hd % 2).astype(BF16)
        _acc(dga_ref, dga, first)
        _acc(dgb_ref, dgb, first)

    return pl.pallas_call(
        body, name="mix_bwd", grid=(t // tm,),
        in_specs=[_row(tm, D_MODEL), _row(tm, SWA_Q_W), _row(tm, MLA_OUT_W), _const(ga.shape), _const(gb.shape),
                  _const(wo.shape)],
        out_specs=[_row(tm, HP), _row(tm, HP), pl.BlockSpec((MLA_HEADS // MLA_HB, tm, MLA_HB), lambda i: (0, i, 0)),
                   _const((1, SWA_Q_W)), _const((1, MLA_OUT_W))],
        out_shape=[jax.ShapeDtypeStruct((t, HP), BF16), jax.ShapeDtypeStruct((t, HP), BF16), _head_stats(t),
                   jax.ShapeDtypeStruct((1, SWA_Q_W), F32), jax.ShapeDtypeStruct((1, MLA_OUT_W), F32)],
        compiler_params=_params("arbitrary"),
    )(dh2, oa, ob, ga, gb, wo)


def _swa_bwd(sinks, q, k, v, o, do):
    t = q.shape[0]
    ts = _tile(t)

    def body(sink_ref, q_ref, kp_ref, kc_ref, vp_ref, vc_ref, o_ref, do_ref,
             dq_ref, dkc_ref, dkp_ref, dvc_ref, dvp_ref, dsink_ref):
        n = pl.program_id(0)
        chains = _swa_chains(t)
        qs, ks, probs = _swa_scores(sink_ref, q_ref, kp_ref, kc_ref, n, t)
        dos = [_swa_group(do_ref, slice(BLOCK * rb, BLOCK * (rb + 1)), j) for rb, j in chains]
        vs = [_swa_keys(vp_ref, vc_ref, rb, j) for rb, j in chains]
        dps = [_dot_nt(v2, do4) for do4, v2 in zip(dos, vs)]
        dss, dsks = [], []
        for (rb, j), (p, psink), do4, dp in zip(chains, probs, dos, dps):
            o4 = _swa_packed_group(o_ref, slice(BLOCK * rb, BLOCK * (rb + 1)), j)
            delta = jnp.sum(o4 * do4.astype(F32), axis=1, keepdims=True)
            delta = jnp.broadcast_to(delta, (SWA_GROUP * BLOCK, LANE)).T[:1, :]
            dss.append((p * (dp - delta) * SCALE_A).astype(BF16))
            dsks.append(-psink * delta)
        dqs = [_dot_tn(k2[:, :HALF], ds) for ds, k2 in zip(dss, ks)]
        dks = [_dot(ds, q4) for ds, q4 in zip(dss, qs)]
        dvs = [_dot(p.astype(BF16), do4) for (p, _), do4 in zip(probs, dos)]
        dsink = [jnp.zeros((1, LANE), F32)] * SWA_HEADS
        ext = {}
        for (rb, j), dq4, dk2, dv2, dsk in zip(chains, dqs, dks, dvs, dsks):
            for g in range(SWA_GROUP):
                hd = SWA_GROUP * j + g
                cols = slice(BLOCK * g, BLOCK * (g + 1))
                dq_ref[BLOCK * rb:BLOCK * (rb + 1), LANE * hd:LANE * (hd + 1)] = jnp.concatenate(
                    [dq4[:, cols], jnp.zeros((HALF, BLOCK), F32)], axis=0).T.astype(BF16)
                dsink[hd] = dsink[hd] + jnp.sum(dsk[:, cols], axis=1, keepdims=True)
            for half in range(2):
                key = (j, rb + half)
                part = (dk2[BLOCK * half:BLOCK * (half + 1)], dv2[BLOCK * half:BLOCK * (half + 1)])
                ext[key] = part if key not in ext else (ext[key][0] + part[0], ext[key][1] + part[1])
        for (j, blk), (dk, dv) in ext.items():
            sl = slice(LANE * j, LANE * (j + 1))
            if blk == 0:
                dkp_ref[:, sl] = dk
                dvp_ref[:, sl] = dv
            else:
                dkc_ref[BLOCK * (blk - 1):BLOCK * blk, sl] = dk
                dvc_ref[BLOCK * (blk - 1):BLOCK * blk, sl] = dv
        for hd in range(SWA_HEADS):
            _acc(dsink_ref.at[hd:hd + 1, :], jnp.broadcast_to(dsink[hd], (1, LANE)), n == 0)

    cur = lambda n: (n, 0)
    kv = pl.BlockSpec((ts, 2 * LANE), cur)
    kvp = pl.BlockSpec((BLOCK, 2 * LANE), cur)
    hp = pl.BlockSpec((ts, HP), cur)
    kvs = jax.ShapeDtypeStruct((t, 2 * LANE), F32)
    kvps = jax.ShapeDtypeStruct((t // ts * BLOCK, 2 * LANE), F32)
    return pl.pallas_call(
        body, name="swa_bwd", grid=(t // ts,),
        in_specs=_swa_specs(t) + [pl.BlockSpec((ts, SWA_Q_W), cur), hp],
        out_specs=[hp, kv, kvp, kv, kvp, _const((SWA_HEADS, LANE))],
        out_shape=[jax.ShapeDtypeStruct((t, HP), BF16), kvs, kvps, kvs, kvps,
                   jax.ShapeDtypeStruct((SWA_HEADS, LANE), F32)],
        compiler_params=_params("arbitrary"),
    )(sinks, q, k, k, v, v, o, do)


def _mla_bwd(q, k, v, do, lse, dl, slabs=()):
    t = q.shape[0]
    tq = _tile(t)
    nq = t // tq
    n = len(slabs)
    hb = MLA_HB
    steps = (MLA_HEADS // hb) * nq

    def body(k_ref, v_ref, q_ref, do_ref, lse_ref, dl_ref, *rest):
        in_refs, (dq_ref, dk_ref, dv_ref), out_refs = rest[:n], rest[n:n + 3], rest[n + 3:2 * n + 3]
        (dq_sc, dk_sc, dv_sc), sems = rest[2 * n + 3:2 * n + 6], rest[2 * n + 6:]
        j = pl.program_id(1)
        step_id = pl.program_id(0) * nq + j
        if n:
            plan = _exchange_plan(in_refs, out_refs, *sems)
            pl.when(step_id == 0)(plan.start)

        @pl.when(j == 0)
        def _():
            dq_sc[...] = jnp.zeros(dq_sc.shape, F32)

        dk_sc[...] = jnp.zeros(dk_sc.shape, F32)
        dv_sc[...] = jnp.zeros(dv_sc.shape, F32)
        ks, vs = _heads(k_ref, hb), _heads(v_ref, hb)

        def step(i, carry, masked):
            rows = pl.ds(pl.multiple_of(i * tq, tq), tq)
            qs, dos = _heads(q_ref, hb, rows), _heads(do_ref, hb, rows)
            ss = [_dot_nt(qh, kh) for qh, kh in zip(qs, ks)]
            dps = [_dot_nt(doh, vh) for doh, vh in zip(dos, vs)]
            if masked:
                mask = _causal_mask(i * tq, j * tq, tq, tq, False)
                ss = [jnp.where(mask, s_, NEG) for s_ in ss]
            ps = [jnp.exp2(s_ - lse_ref[rows, a:a + 1]) for a, s_ in enumerate(ss)]
            dss = [(p * (dp - dl_ref[rows, a:a + 1])).astype(BF16) for a, (p, dp) in enumerate(zip(ps, dps))]
            for a, (ds, p, qh, kh, doh) in enumerate(zip(dss, ps, qs, ks, dos)):
                dq_sc[a, rows, :] += _dot(ds, kh)
                dk_sc[a, :MLA_QK_DIM, :] += _dot_tn(qh[:, :MLA_QK_DIM], ds)
                dv_sc[a, :MLA_V_DIM, :] += _dot_tn(doh[:, :MLA_V_DIM], p.astype(BF16))
            return carry

        split = jnp.where(j == 0, nq, j + 1)
        lax.fori_loop(j, split, lambda i, c: step(i, c, True), 0)
        lax.fori_loop(split, nq, lambda i, c: step(i, c, False), 0)
        for a in range(hb):
            dk_ref[:, LANE * a:LANE * (a + 1)] = (dk_sc[a] * (1.0 / LOG2E)).T.astype(BF16)
            dv_ref[:, LANE * a:LANE * (a + 1)] = dv_sc[a].T.astype(BF16)

        @pl.when(j == nq - 1)
        def _():
            for a in range(hb):
                dq_ref[:, LANE * a:LANE * (a + 1)] = (dq_sc[a] * SCALE_B).astype(BF16)

        if n:
            pl.when(step_id == steps - 1)(plan.finish)

    blk = pl.BlockSpec((tq, hb * LANE), lambda h, j: (j, h))
    full = pl.BlockSpec((t, hb * LANE), lambda h, j: (0, h))
    cols = pl.BlockSpec((None, t, hb), lambda h, j: (h, 0, 0))
    out = pl.pallas_call(
        body, name="mla_bwd_exchange" if n else "mla_bwd", grid=(MLA_HEADS // hb, nq),
        in_specs=[blk, blk, full, full, cols, cols] + [ANY] * n, out_specs=[full, blk, blk] + [ANY] * n,
        out_shape=[jax.ShapeDtypeStruct((t, HP), BF16)] * 3 + [jax.ShapeDtypeStruct(a.shape, a.dtype) for a in slabs],
        scratch_shapes=[pltpu.VMEM((hb, t, LANE), F32)] + [pltpu.VMEM((hb, LANE, tq), F32)] * 2
        + (_comm_sems(n) if n else []),
        compiler_params=_params("arbitrary", "arbitrary"),
    )(k, v, q, do, lse, dl, *slabs)
    return out[:3], out[3:]


def _pre_bwd(dh2, h, cq, ckv, dqa, dka, dka_next, dva, dva_next, dqb, dkf, dvb, g1, win, gq, wqu, gkv, wkv, tabs):
    t = h.shape[0]
    tm = _tile(t)

    def body(dh2_ref, h_ref, cq_ref, ckv_ref, dqa_ref, dka_ref, dkan_ref, dva_ref, dvan_ref, dqb_ref, dkf_ref, dvb_ref,
             g1_ref, win_ref, gq_ref, wqu_ref, gkv_ref, wkv_ref, tab_ref,
             dh_ref, dp_ref, dqbo_ref, dkvo_ref, dg1_ref, dgq_ref, dgkv_ref):
        first = pl.program_id(0) == 0
        ca, sa1, sa2, cb, sb1, sb2, ck = _tabs(tab_ref)
        dkr = jnp.zeros((tm, LANE), F32)
        for c in range(MLA_HEADS):
            sl = slice(LANE * c, LANE * (c + 1))
            dqbo_ref[:, sl] = _rope_t(dqb_ref[:, sl].astype(F32), cb, sb1, sb2, 16).astype(BF16)
            dkr += dkf_ref[:, sl].astype(F32)
        dkvo_ref[:, :HP] = dkf_ref[...]
        dkvo_ref[:, HP:] = dvb_ref[...]
        dcq, dgq = _rms_bwd(cq_ref[...], gq_ref[...], _dot(dqbo_ref[...], wqu_ref[...]), MLA_Q_RANK)
        dckv, dgkv = _rms_bwd(ckv_ref[...], gkv_ref[...], _dot(dkvo_ref[...], wkv_ref[...]), MLA_KV_RANK)
        for c in range(SWA_HEADS):
            sl = slice(LANE * c, LANE * (c + 1))
            dp_ref[:, PO_QA + LANE * c:PO_QA + LANE * (c + 1)] = _rope_t(dqa_ref[:, sl].astype(F32), ca, sa1, sa2,
                                                                          32).astype(BF16)
        last = slice(tm - BLOCK, tm)
        more = pl.program_id(0) < t // tm - 1
        for c in range(SWA_KV_HEADS):
            sl = slice(LANE * c, LANE * (c + 1))
            dk = dka_ref[:, sl]
            dk_last = dk[tm - BLOCK:] + jnp.where(more, dkan_ref[:, sl], 0.0)
            cols = slice(PO_KA + LANE * c, PO_KA + LANE * (c + 1))
            if tm > BLOCK:
                dp_ref[:tm - BLOCK, cols] = _rope_t(dk[:tm - BLOCK], ca[:tm - BLOCK], sa1[:tm - BLOCK], sa2[:tm - BLOCK],
                                                    32).astype(BF16)
            dp_ref[last, cols] = _rope_t(dk_last, ca[tm - BLOCK:], sa1[tm - BLOCK:], sa2[tm - BLOCK:], 32).astype(BF16)
        if tm > BLOCK:
            dp_ref[:tm - BLOCK, PO_VA:PO_CQ] = dva_ref[:tm - BLOCK, :].astype(BF16)
        dp_ref[last, PO_VA:PO_CQ] = (dva_ref[tm - BLOCK:, :] + jnp.where(more, dvan_ref[...], 0.0)).astype(BF16)
        dp_ref[:, PO_CQ:PO_CKV] = dcq.astype(BF16)
        dp_ref[:, PO_CKV:PO_KR] = dckv.astype(BF16)
        dp_ref[:, PO_KR:PW_IN] = _rope_t(dkr, ck, sb1, sb2, 16).astype(BF16)
        dx, dg1 = _rms_bwd(h_ref[...], g1_ref[...], _dot(dp_ref[...], win_ref[...]), D_MODEL)
        dh_ref[...] = dh2_ref[...] + dx
        _acc(dg1_ref, dg1, first)
        _acc(dgq_ref, dgq, first)
        _acc(dgkv_ref, dgkv, first)

    kv = _row(tm, 2 * LANE)
    nxt = pl.BlockSpec((BLOCK, 2 * LANE), lambda i: (jnp.minimum(i + 1, t // tm - 1), 0))
    return pl.pallas_call(
        body, name="pre_bwd", grid=(t // tm,),
        in_specs=[_row(tm, D_MODEL), _row(tm, D_MODEL), _row(tm, MLA_Q_RANK), _row(tm, MLA_KV_RANK), _row(tm, HP),
                  kv, nxt, kv, nxt, _row(tm, HP), _row(tm, HP), _row(tm, HP),
                  _const(g1.shape), _const(win.shape), _const(gq.shape), _const(wqu.shape), _const(gkv.shape),
                  _const(wkv.shape), _row(tm, N_TAB * LANE)],
        out_specs=[_row(tm, D_MODEL), _row(tm, PW_IN), _row(tm, HP), _row(tm, 2 * HP),
                   _const((1, D_MODEL)), _const((1, MLA_Q_RANK)), _const((1, MLA_KV_RANK))],
        out_shape=[jax.ShapeDtypeStruct((t, D_MODEL), F32), jax.ShapeDtypeStruct((t, PW_IN), BF16),
                   jax.ShapeDtypeStruct((t, HP), BF16), jax.ShapeDtypeStruct((t, 2 * HP), BF16),
                   jax.ShapeDtypeStruct((1, D_MODEL), F32), jax.ShapeDtypeStruct((1, MLA_Q_RANK), F32),
                   jax.ShapeDtypeStruct((1, MLA_KV_RANK), F32)],
        compiler_params=_params("arbitrary"),
    )(dh2, h, cq, ckv, dqa, dka, dka_next, dva, dva_next, dqb, dkf, dvb, g1, win, gq, wqu, gkv, wkv, tabs)


def _rope_tables(t):
    pos = (jnp.arange(t, dtype=jnp.int32) - FRONT).astype(F32)[:, None]
    lane = jnp.arange(LANE)[None, :]

    def table(dim, start):
        half = dim // 2
        inv = ROPE_THETA ** (-jnp.arange(0, dim, 2, dtype=F32) / dim)
        ang = pos * inv[None, :]
        cos = jnp.concatenate([jnp.cos(ang)] * 2, axis=1)
        sin = jnp.concatenate([jnp.sin(ang)] * 2, axis=1)
        pad = lambda a: jnp.pad(a, ((0, 0), (start, LANE - start - dim)))
        first = (lane >= start) & (lane < start + half)
        second = (lane >= start + half) & (lane < start + dim)
        return pad(cos), jnp.where(first, -pad(sin), 0.0), jnp.where(second, pad(sin), 0.0)

    ca, sa1, sa2 = table(SWA_HEAD_DIM, 0)
    ck, sb1, sb2 = table(MLA_ROPE_DIM, MLA_NOPE_DIM)
    cb = jnp.where(lane < MLA_NOPE_DIM, 1.0, ck)
    return jnp.concatenate([ca, sa1, sa2, cb, sb1, sb2, ck], axis=1)


def _pad_heads(w, heads, dim, axis):
    shp = w.shape
    w = w.reshape(shp[:axis] + (heads, dim) + shp[axis + 1:])
    pad = [(0, 0)] * w.ndim
    pad[axis + 1] = (0, LANE - dim)
    return jnp.pad(w, pad).reshape(shp[:axis] + (heads * LANE,) + shp[axis + 1:])


def _unpad_heads(w, heads, dim, axis):
    shp = w.shape
    w = w.reshape(shp[:axis] + (heads, LANE) + shp[axis + 1:])
    w = lax.slice_in_dim(w, 0, dim, axis=axis + 1)
    return w.reshape(shp[:axis] + (heads * dim,) + shp[axis + 1:])


def _pad_layer(w_in, w_q_up, w_kv_up):
    o1 = SWA_Q_W
    o2 = o1 + SWA_KV_W
    o3 = o2 + SWA_KV_W
    o4 = o3 + MLA_Q_RANK
    o5 = o4 + MLA_KV_RANK
    kr = jnp.pad(w_in[o5:], ((MLA_NOPE_DIM, LANE - MLA_QK_DIM), (0, 0)))
    win = jnp.concatenate([
        _pad_heads(w_in[:o1], SWA_HEADS, SWA_HEAD_DIM, 0),
        _pad_heads(w_in[o1:o2], SWA_KV_HEADS, SWA_HEAD_DIM, 0),
        _pad_heads(w_in[o2:o3], SWA_KV_HEADS, SWA_HEAD_DIM, 0),
        w_in[o3:o5], kr], axis=0)
    wqu = _pad_heads(w_q_up, MLA_HEADS, MLA_QK_DIM, 0)
    kv = w_kv_up.reshape(MLA_HEADS, MLA_NOPE_DIM + MLA_V_DIM, MLA_KV_RANK)
    wkv = jnp.concatenate([
        _pad_heads(kv[:, :MLA_NOPE_DIM].reshape(-1, MLA_KV_RANK), MLA_HEADS, MLA_NOPE_DIM, 0),
        _pad_heads(kv[:, MLA_NOPE_DIM:].reshape(-1, MLA_KV_RANK), MLA_HEADS, MLA_V_DIM, 0)], axis=0)
    return win, wqu, wkv


IN_KEEP_SWA = [(LANE * hd, SWA_HEAD_DIM) for hd in range(SWA_HEADS)]
IN_KEEP_REST = ([(LANE * hd, SWA_HEAD_DIM) for hd in range(2 * SWA_KV_HEADS)] + [(PO_CQ - PO_KA, PO_KR - PO_CQ)]
                + [(PO_KR - PO_KA + MLA_NOPE_DIM, MLA_ROPE_DIM)])


def _unpad_layer(d_w_in, dwqu, dwkv):
    d_w_q_up = _unpad_heads(dwqu, MLA_HEADS, MLA_QK_DIM, 0)
    dk = _unpad_heads(dwkv[:, :HP], MLA_HEADS, MLA_NOPE_DIM, 1).reshape(MLA_KV_RANK, MLA_HEADS, MLA_NOPE_DIM)
    dv = _unpad_heads(dwkv[:, HP:], MLA_HEADS, MLA_V_DIM, 1).reshape(MLA_KV_RANK, MLA_HEADS, MLA_V_DIM)
    d_w_kv_up = jnp.concatenate([dk, dv], axis=2).reshape(MLA_KV_RANK, -1).T
    return d_w_in, d_w_q_up, d_w_kv_up


def _train_example(x, target, meta, vec, weights):
    s = x.shape[0]
    depth = vec["attn_norm"].shape[0]
    t = FRONT + N_META + s
    assert t % BLOCK == 0
    tabs = _rope_tables(t)
    h = jnp.concatenate([jnp.zeros((FRONT, D_MODEL), F32), meta, x], axis=0)
    row = lambda v: v[None, :]

    saved = []
    for l in range(depth):
        win, wqu, wkv = _pad_layer(*weights.attn_in(l))
        g1, gq, gkv, g2, ga, gb = (row(vec[n][l]) for n in ("attn_norm", "q_norm", "kv_norm", "ffn_norm",
                                                            "out_norm_swa", "out_norm_mla"))
        sk = row(vec["sinks"][l])
        u, qa, ka, va, cq, ckv, qn, kvn, qb, kf, vb = _pre_fwd(h, g1, win, gq, wqu, gkv, wkv, tabs)
        oa = _swa_fwd(sk, qa, ka, va)
        ob, lse = weights.mla_fwd(l, qb, kf, vb)
        lse = jnp.moveaxis(lse.reshape(t, MLA_HEADS // MLA_HB, MLA_HB), 1, 0)
        wo = weights.w_o(l)
        h2, mix, u2 = _mix_fwd(h, oa, ob, ga, gb, wo, g2)
        wg, wu, wd = weights.ffn(l)
        h3, gt, up = weights.ffn_fwd(l, h2, u2)
        saved.append((h, u, qa, ka, va, cq, ckv, qn, kvn, qb, kf, vb, oa, ob, lse, h2, mix, u2, gt, up,
                      win, wqu, wkv, wo, ga, gb, g1, gq, gkv, g2, sk, wg, wu, wd))
        h = h3

    dh, d_final, loss = _loss_bwd(h, row(vec["final_norm"]), target)

    grads = []
    for l in reversed(range(depth)):
        (h0, u, qa, ka, va, cq, ckv, qn, kvn, qb, kf, vb, oa, ob, lse, h2, mix, u2, gt, up,
         win, wqu, wkv, wo, ga, gb, g1, gq, gkv, g2, sk, wg, wu, wd) = saved[l]
        dff = wd.shape[0]
        act, dgu, dhb = _ffn_bwd_a(dh, gt, up, wd)
        weights.ffn_grads(l, _tn_matmul(dgu, u2, "dw_gate", (0, dff)), _tn_matmul(dgu, u2, "dw_up", (dff, dff)),
                          _tn_matmul(act, dhb, "dw_down"))
        dh2, dh2b, d_g2 = _ffn_bwd_b(dh, dgu, h2, g2, wg, wu)
        weights.attn_grads(l, w_o=_tn_matmul(mix, dh2b, "dw_o"))
        doa, dob, dl, d_ga, d_gb = _mix_bwd(dh2b, oa, ob, ga, gb, wo)
        dqa, dkc, dkp, dvc, dvp, dsink = _swa_bwd(sk, qa, ka, va, oa, doa)
        dqb, dkf, dvb = weights.mla_bwd(l, qb, kf, vb, dob, lse, dl)
        dh, dp, dqbo, dkvo, d_g1, d_gq, d_gkv = _pre_bwd(
            dh2, h0, cq, ckv, dqa, dkc, dkp, dvc, dvp, dqb, dkf, dvb,
            g1, win, gq, wqu, gkv, wkv, tabs)
        d_win = jnp.concatenate([_tn_matmul(dp, u, "dw_in_swa", (PO_QA, PO_KA), IN_KEEP_SWA),
                                 _tn_matmul(dp, u, "dw_in_rest", (PO_KA, PW_IN - PO_KA), IN_KEEP_REST)], axis=0)
        d_wqu = _tn_matmul(dqbo, qn, "dw_q_up")
        d_wkv = _tn_matmul(kvn, dkvo, "dw_kv_up")
        weights.attn_grads(l, **dict(zip(ATTN_IN, _unpad_layer(d_win, d_wqu, d_wkv))))
        grads.append(dict(attn_norm=d_g1[0], q_norm=d_gq[0], kv_norm=d_gkv[0], sinks=dsink[:, 0], out_norm_swa=d_ga[0],
                          out_norm_mla=d_gb[0], ffn_norm=d_g2[0]))
    grads = grads[::-1]
    stacked = {k: jnp.stack([g[k] for g in grads]) for k in grads[0]}
    stacked["final_norm"] = d_final[0]
    return loss[0, 0], dh[FRONT + N_META:], dh[FRONT:FRONT + N_META], stacked


MESH = pl.DeviceIdType.MESH
ANY = pl.BlockSpec(memory_space=pl.ANY)


def _place():
    return lax.axis_index("x"), lax.axis_index("y"), lax.axis_index("c")


def _index(x, y, c):
    return 4 * x + 2 * y + c


def _comm_sems(n):
    return [pltpu.SemaphoreType.DMA((n, N_DEV - 1)), pltpu.SemaphoreType.DMA((n, N_DEV - 1)),
            pltpu.SemaphoreType.DMA((n,))]


class _gather_plan:
    def __init__(self, x_refs, out_refs, send_sems, recv_sems, local_sems):
        self.x_refs, self.out_refs = x_refs, out_refs
        self.send_sems, self.recv_sems, self.local_sems = send_sems, recv_sems, local_sems
        self.n = len(x_refs)

    def _where(self):
        x, y, c = _place()
        return (x, y, c), (x, y, 1 - c), [(1 - x, y), (x, 1 - y), (1 - x, 1 - y)], c

    def _copy(self, i, k, block, to, from_input=False):
        slot = self.out_refs[i].at[_index(*block)]
        return pltpu.make_async_remote_copy(
            src_ref=self.x_refs[i] if from_input else slot, dst_ref=slot,
            send_sem=self.send_sems.at[i, k], recv_sem=self.recv_sems.at[i, k], device_id=to, device_id_type=MESH)

    def _mine(self, i, me):
        return pltpu.make_async_copy(self.x_refs[i], self.out_refs[i].at[_index(*me)], self.local_sems.at[i])

    def _first(self, me, sibling, chips, c):
        out = [self._copy(i, 1 + j, me, (*chip, c), True) for j, chip in enumerate(chips) for i in range(self.n)]
        return out + [self._copy(i, 0, me, sibling, True) for i in range(self.n)]

    def start(self):
        me, sibling, chips, c = self._where()
        for i in range(self.n):
            self._mine(i, me).start()
        for cp in self._first(me, sibling, chips, c):
            cp.start()

    def forward(self):
        me, sibling, chips, c = self._where()
        for j, chip in enumerate(chips):
            for i in range(self.n):
                self._copy(i, 1 + j, (*chip, c), me).wait_recv()
                self._copy(i, 4 + j, (*chip, c), sibling).start()

    def finish(self):
        me, sibling, chips, c = self._where()
        for i in range(self.n):
            self._copy(i, 0, sibling, me).wait_recv()
            for j, chip in enumerate(chips):
                self._copy(i, 4 + j, (*chip, 1 - c), me).wait_recv()
        for cp in self._first(me, sibling, chips, c):
            cp.wait_send()
        for j, chip in enumerate(chips):
            for i in range(self.n):
                self._copy(i, 4 + j, (*chip, c), sibling).wait_send()
        for i in range(self.n):
            self._mine(i, me).wait()


class _exchange_plan:
    def __init__(self, in_refs, out_refs, send_sems, recv_sems, local_sems):
        self.in_refs, self.out_refs = in_refs, out_refs
        self.send_sems, self.recv_sems, self.local_sems = send_sems, recv_sems, local_sems
        self.n = len(in_refs)

    def _copies(self):
        x, y, c = _place()
        me = _index(x, y, c)
        mine = [pltpu.make_async_copy(self.in_refs[i].at[me], self.out_refs[i].at[me], self.local_sems.at[i])
                for i in range(self.n)]
        remote = []
        for k in range(1, N_DEV):
            peer = (1 - x if k & 4 else x, 1 - y if k & 2 else y, 1 - c if k & 1 else c)
            remote += [pltpu.make_async_remote_copy(
                src_ref=self.in_refs[i].at[_index(*peer)], dst_ref=self.out_refs[i].at[me],
                send_sem=self.send_sems.at[i, k - 1], recv_sem=self.recv_sems.at[i, k - 1],
                device_id=peer, device_id_type=MESH) for i in range(self.n)]
        return mine, remote

    def start(self):
        mine, remote = self._copies()
        for cp in mine + remote:
            cp.start()

    def finish(self):
        mine, remote = self._copies()
        for cp in remote:
            cp.wait_recv()
        for cp in remote:
            cp.wait_send()
        for cp in mine:
            cp.wait()


def _all_gather(shards, name):
    n = len(shards)

    def body(*refs):
        plan = _gather_plan(refs[:n], refs[n:2 * n], *refs[2 * n:])
        plan.start()
        plan.forward()
        plan.finish()

    return pl.pallas_call(
        body, name=name, in_specs=[ANY] * n, out_specs=[ANY] * n, scratch_shapes=_comm_sems(n),
        out_shape=[jax.ShapeDtypeStruct((N_DEV,) + a.shape, a.dtype) for a in shards],
    )(*shards)


def _exchange(slabs, name):
    n = len(slabs)

    def body(*refs):
        plan = _exchange_plan(refs[:n], refs[n:2 * n], *refs[2 * n:])
        plan.start()
        plan.finish()

    return pl.pallas_call(
        body, name=name, in_specs=[ANY] * n, out_specs=[ANY] * n, scratch_shapes=_comm_sems(n),
        out_shape=[jax.ShapeDtypeStruct(a.shape, a.dtype) for a in slabs],
    )(*slabs)


def _adamw(w, g, m, v):
    m = ADAM_B1 * m + (1.0 - ADAM_B1) * g
    v = ADAM_B2 * v + (1.0 - ADAM_B2) * (g * g)
    m_hat = m / (1.0 - ADAM_B1 ** ADAM_STEP)
    v_hat = v / (1.0 - ADAM_B2 ** ADAM_STEP)
    return -ADAM_LR * (m_hat / (jnp.sqrt(v_hat) + ADAM_EPS) + ADAM_WD * w), m, v


def _sum_slots(ref):
    g = ref[0].astype(F32)
    for s in range(1, N_DEV):
        g = g + ref[s].astype(F32)
    return g


def _reduce_adamw(parts, w, m, v, name):
    l, r, c = w.shape
    tile = max([d for d in range(16, ADAM_ROWS + 1, 16) if r % d == 0], default=r)
    last = r // tile - 1

    def body(*refs):
        p_refs, (w_ref, m_ref, v_ref), (g_ref, d_ref, nm_ref, nv_ref) = refs[:l], refs[l:l + 3], refs[l + 3:]
        for layer in range(l):
            @pl.when(pl.program_id(0) == layer)
            def _(p_ref=p_refs[layer]):
                g = _sum_slots(p_ref)
                g_ref[...] = g
                d_ref[...], nm_ref[...], nv_ref[...] = _adamw(w_ref[...], g, m_ref[...], v_ref[...])

    def part_spec(layer):
        return pl.BlockSpec((N_DEV, tile, c),
                            lambda i, j: (0, jnp.where(i == layer, j, jnp.where(i < layer, 0, last)), 0))

    blk = pl.BlockSpec((None, tile, c), lambda i, j: (i, j, 0))
    return pl.pallas_call(
        body, name=name, grid=(l, r // tile),
        in_specs=[part_spec(layer) for layer in range(l)] + [blk, blk, blk], out_specs=[blk] * 4,
        out_shape=[jax.ShapeDtypeStruct((l, r, c), F32)] * 4,
        compiler_params=_params("arbitrary", "arbitrary"),
    )(*parts, w, m, v)


def _sum_parts(parts, name):
    _, r, c = parts.shape

    def body(p_ref, g_ref):
        g_ref[...] = _sum_slots(p_ref)

    return pl.pallas_call(body, name=name, out_shape=jax.ShapeDtypeStruct((r, c), F32))(parts)


def _adamw_call(w, g, m, v, name):
    def body(w_ref, g_ref, m_ref, v_ref, d_ref, nm_ref, nv_ref):
        d_ref[...], nm_ref[...], nv_ref[...] = _adamw(w_ref[...], g_ref[...], m_ref[...], v_ref[...])

    return pl.pallas_call(body, name=name, out_shape=[jax.ShapeDtypeStruct(w.shape, F32)] * 3)(w, g, m, v)


ATTN_IN = ("w_in", "w_q_up", "w_kv_up")
ATTN = ATTN_IN + ("w_o",)
FFN = ("w_gate", "w_up", "w_down")
TRANSPOSED = ("w_in", "w_q_up", "w_kv_up", "w_gate", "w_up")
SMALL = ("attn_norm", "ffn_norm", "final_norm", "out_norm_swa", "out_norm_mla", "q_norm", "kv_norm", "sinks")
PACK_W = 1024
SMALL_ROWS = 16


def _pack(arrs, dtype):
    flat = jnp.concatenate([a.astype(dtype).reshape(-1) for a in arrs])
    return flat.reshape(-1, PACK_W)


def _unpack(packed, like):
    flat = packed.reshape(-1)
    out, off = [], 0
    for a in like:
        out.append(flat[off:off + a.size].reshape(a.shape))
        off += a.size
    return out


def _gather_to_full(gathered):
    return gathered.reshape((-1,) + gathered.shape[2:])


def _full_to_slabs(full):
    return full.reshape((N_DEV, -1) + full.shape[1:])


class _ShardedWeights:
    def __init__(self, shards, depth, meta_shard):
        self.shards, self.depth = shards, depth
        self.gathered, self.pending, self.parts = {}, {}, {}
        first = _all_gather([shards[n][0] for n in ATTN_IN] + [meta_shard], "gather_attn0")
        self.gathered.update(zip([(n, 0) for n in ATTN_IN], first))
        self.meta = jnp.moveaxis(first[-1], 0, 1).reshape(N_META, D_MODEL)

    def _gather(self, keys, run):
        self.gathered.update(zip(keys, run([self.shards[n][l] for n, l in keys])))

    def _full(self, names, l):
        return tuple(_gather_to_full(self.gathered[n, l]) for n in names)

    def attn_in(self, l):
        return self._full(ATTN_IN, l)

    def w_o(self, l):
        return self._full(("w_o",), l)[0]

    def ffn(self, l):
        return self._full(FFN, l)

    def mla_fwd(self, l, q, k, v):
        out = []
        self._gather([(n, l) for n in ("w_o",) + FFN], lambda xs: out.extend(_mla_fwd(q, k, v, xs)) or out[2])
        return out[0], out[1]

    def ffn_fwd(self, l, h2, u2):
        keys = [(n, l + 1) for n in ATTN_IN] if l + 1 < self.depth else []
        out = []
        self._gather(keys, lambda xs: out.extend(_ffn_fwd(h2, u2, *self.ffn(l), xs)) or out[1])
        return out[0]

    def _add(self, names, l, grads):
        for n, g in zip(names, grads):
            self.pending[n, l] = _full_to_slabs(g)

    def ffn_grads(self, l, *grads):
        self._add(FFN, l, grads)

    def attn_grads(self, l, **grads):
        self._add(list(grads), l, grads.values())

    def _exchange(self, run):
        keys = list(self.pending)
        self.parts.update(zip(keys, run([self.pending.pop(k) for k in keys])))

    def mla_bwd(self, l, *args):
        out = []
        self._exchange(lambda xs: out.extend(_mla_bwd(*args, xs)) or out[1])
        return out[0]

    def flush(self):
        self._exchange(lambda xs: _exchange(xs, "exchange_attn0"))


def kernel(x, meta_tokens, attn_norm, w_in, q_norm, w_q_up, kv_norm, w_kv_up, sinks, out_norm_swa, out_norm_mla, w_o, ffn_norm, w_gate, w_up, w_down, final_norm, loss_target, m_meta_tokens, m_attn_norm, m_w_in, m_q_norm, m_w_q_up, m_kv_norm, m_w_kv_up, m_sinks, m_out_norm_swa, m_out_norm_mla, m_w_o, m_ffn_norm, m_w_gate, m_w_up, m_w_down, m_final_norm, v_meta_tokens, v_attn_norm, v_w_in, v_q_norm, v_w_q_up, v_kv_norm, v_w_kv_up, v_sinks, v_out_norm_swa, v_out_norm_mla, v_w_o, v_ffn_norm, v_w_gate, v_w_up, v_w_down, v_final_norm):
    w = dict(meta_tokens=meta_tokens, attn_norm=attn_norm, w_in=w_in, q_norm=q_norm, w_q_up=w_q_up, kv_norm=kv_norm,
             w_kv_up=w_kv_up, sinks=sinks, out_norm_swa=out_norm_swa, out_norm_mla=out_norm_mla, w_o=w_o,
             ffn_norm=ffn_norm, w_gate=w_gate, w_up=w_up, w_down=w_down, final_norm=final_norm)
    m = dict(meta_tokens=m_meta_tokens, attn_norm=m_attn_norm, w_in=m_w_in, q_norm=m_q_norm, w_q_up=m_w_q_up,
             kv_norm=m_kv_norm, w_kv_up=m_w_kv_up, sinks=m_sinks, out_norm_swa=m_out_norm_swa,
             out_norm_mla=m_out_norm_mla, w_o=m_w_o, ffn_norm=m_ffn_norm, w_gate=m_w_gate, w_up=m_w_up,
             w_down=m_w_down, final_norm=m_final_norm)
    v = dict(meta_tokens=v_meta_tokens, attn_norm=v_attn_norm, w_in=v_w_in, q_norm=v_q_norm, w_q_up=v_w_q_up,
             kv_norm=v_kv_norm, w_kv_up=v_w_kv_up, sinks=v_sinks, out_norm_swa=v_out_norm_swa,
             out_norm_mla=v_out_norm_mla, w_o=v_w_o, ffn_norm=v_ffn_norm, w_gate=v_w_gate, w_up=v_w_up,
             w_down=v_w_down, final_norm=v_final_norm)
    names = list(w)
    big = ATTN + FFN
    depth = w_in.shape[0]
    me = _index(*_place())

    as_held = lambda n, a: jnp.swapaxes(a, 1, 2) if n in TRANSPOSED else a
    weights = _ShardedWeights({n: as_held(n, w[n]).astype(BF16) for n in big}, depth, meta_tokens)
    loss, grad_x, d_meta, grads = _train_example(x[0], loss_target[0], weights.meta, {n: w[n] for n in SMALL}, weights)
    weights.flush()

    g_big, d_big, m_big, v_big = {}, {}, {}, {}
    for n in big:
        held = [as_held(n, a) for a in (w[n], m[n], v[n])]
        outs = _reduce_adamw([weights.parts[n, l] for l in range(depth)], *held, "reduce_adamw_" + n)
        g_big[n], d_big[n], m_big[n], v_big[n] = [as_held(n, a) for a in outs]

    small = [grads[n] for n in SMALL] + [loss.reshape(1)]
    pad = SMALL_ROWS * PACK_W - sum(a.size for a in small)
    part = jnp.concatenate([_pack(small + [jnp.zeros((pad,), F32)], F32), d_meta], axis=0)
    total = _sum_parts(_all_gather([part], "gather_small")[0], "sum_small")
    small_w = [w[n] for n in SMALL]
    packs = [_pack([d[n] for n in SMALL] + [jnp.zeros((pad + 1,), F32)], F32) for d in (w, m, v)]
    upd = _adamw_call(packs[0], total[:SMALL_ROWS], packs[1], packs[2], "adamw_small")
    g_small, d_small, m_small, v_small = [dict(zip(SMALL, _unpack(p, small_w))) for p in (total[:SMALL_ROWS],) + tuple(upd)]
    loss_total = total[:SMALL_ROWS].reshape(-1)[SMALL_ROWS * PACK_W - pad - 1]
    g_meta = lax.dynamic_slice_in_dim(total[SMALL_ROWS:], me * LANE, LANE, axis=1)
    d_mt, m_mt, v_mt = _adamw_call(meta_tokens, g_meta, m_meta_tokens, v_meta_tokens, "adamw_meta")

    outs = []
    for got in ({**g_big, **g_small, "meta_tokens": g_meta}, {**d_big, **d_small, "meta_tokens": d_mt},
                {**m_big, **m_small, "meta_tokens": m_mt}, {**v_big, **v_small, "meta_tokens": v_mt}):
        outs += [got[n] for n in names]
    return (loss_total, grad_x[None], *outs)
```

```python
import functools

import jax
import jax.numpy as jnp
from jax import lax
from jax.experimental import pallas as pl
from jax.experimental.pallas import tpu as pltpu

F32 = jnp.float32
BF16 = jnp.bfloat16

D_MODEL = 1024
N_META = 16
BLOCK = 128
FRONT = (-N_META) % BLOCK
ROPE_THETA = 10000.0
EPS = 1e-6
NEG = -1e30
SWA_HEADS = 8
SWA_KV_HEADS = 2
SWA_GROUP = SWA_HEADS // SWA_KV_HEADS
SWA_HEAD_DIM = 64
MLA_HEADS = 8
MLA_Q_RANK = 256
MLA_KV_RANK = 128
MLA_NOPE_DIM = 64
MLA_ROPE_DIM = 32
MLA_V_DIM = 64
MLA_QK_DIM = MLA_NOPE_DIM + MLA_ROPE_DIM
SWA_Q_W = SWA_HEADS * SWA_HEAD_DIM
SWA_KV_W = SWA_KV_HEADS * SWA_HEAD_DIM
MLA_OUT_W = MLA_HEADS * MLA_V_DIM
SCALE_A = SWA_HEAD_DIM ** -0.5
SCALE_B = MLA_QK_DIM ** -0.5
LOG2E = 1.4426950408889634
Q_SCALE = SCALE_B * LOG2E
ADAM_LR = 0.001
ADAM_B1 = 0.9
ADAM_B2 = 0.999
ADAM_EPS = 1e-08
ADAM_WD = 0.01
ADAM_STEP = 10

LANE = 128
N_DEV = 8
HP = 8 * LANE
PO_QA, PO_KA, PO_VA = 0, HP, HP + 2 * LANE
PO_CQ = PO_VA + 2 * LANE
PO_CKV = PO_CQ + MLA_Q_RANK
PO_KR = PO_CKV + MLA_KV_RANK
PW_IN = PO_KR + LANE
N_TAB = 7
VMEM_LIMIT = 56 * 2 ** 20
TN_VMEM_BUDGET = 36 * 2 ** 20
MLA_HB = 4
MLA_HB_FWD = 8
ADAM_ROWS = 256
HALF = LANE // 2
assert SWA_HEAD_DIM == HALF and MLA_V_DIM == HALF

NT = (((1,), (1,)), ((), ()))
TN = (((0,), (0,)), ((), ()))


def _tile(t):
    return 384 if t % 384 == 0 else 128


def _params(*sem):
    return pltpu.CompilerParams(dimension_semantics=sem, vmem_limit_bytes=VMEM_LIMIT)


def _row(tm, n):
    return pl.BlockSpec((tm, n), lambda i: (i, 0))


def _const(shape):
    return pl.BlockSpec(shape, lambda i: (0,) * len(shape))


def _dot(a, b):
    return jnp.dot(a, b, preferred_element_type=F32)


def _dot_nt(a, b):
    return lax.dot_general(a, b, NT, preferred_element_type=F32)


def _dot_tn(a, b):
    return lax.dot_general(a, b, TN, preferred_element_type=F32)


def _rope(x, c, s1, s2, shift):
    return x * c + pltpu.roll(x, LANE - shift, 1) * s1 + pltpu.roll(x, shift, 1) * s2


def _rope_t(dy, c, s1, s2, shift):
    return dy * c + pltpu.roll(dy * s1, shift, 1) + pltpu.roll(dy * s2, LANE - shift, 1)


def _rms_r(x, n):
    return lax.rsqrt(jnp.sum(x * x, axis=-1, keepdims=True) * (1.0 / n) + EPS)


def _rms_bwd(x, g, dy, n):
    r = _rms_r(x, n)
    xh = x * r
    dxh = dy * g
    dx = r * (dxh - xh * (jnp.sum(dxh * xh, axis=-1, keepdims=True) * (1.0 / n)))
    return dx, jnp.sum(dy * xh, axis=0, keepdims=True)


def _acc(ref, val, first):
    @pl.when(first)
    def _():
        ref[...] = val

    @pl.when(jnp.logical_not(first))
    def _():
        ref[...] += val


def _pair_half(slab, half):
    return slab if half == 0 else pltpu.roll(slab, HALF, 1)


def _unpack_pair(slab, half):
    x = _pair_half(slab, half)
    return jnp.where(lax.broadcasted_iota(jnp.int32, x.shape, 1) < HALF, x, 0.0)


def _tabs(tab_ref):
    return [tab_ref[:, LANE * i:LANE * (i + 1)] for i in range(N_TAB)]


def _pre_fwd(h, g1, win, gq, wqu, gkv, wkv, tabs):
    t = h.shape[0]
    tm = _tile(t)

    def body(h_ref, g1_ref, win_ref, gq_ref, wqu_ref, gkv_ref, wkv_ref, tab_ref,
             u_ref, qa_ref, ka_ref, va_ref, cq_ref, ckv_ref, qn_ref, kvn_ref, qb_ref, kf_ref, vb_ref):
        ca, sa1, sa2, cb, sb1, sb2, ck = _tabs(tab_ref)
        hv = h_ref[...]
        u = (hv * _rms_r(hv, D_MODEL) * g1_ref[...]).astype(BF16)
        u_ref[...] = u
        p = _dot_nt(u, win_ref[...])
        for c in range(SWA_HEADS):
            sl = slice(LANE * c, LANE * (c + 1))
            qa_ref[:, sl] = _rope(p[:, PO_QA + LANE * c:PO_QA + LANE * (c + 1)], ca, sa1, sa2, 32).astype(BF16)
        for c in range(SWA_KV_HEADS):
            sl = slice(LANE * c, LANE * (c + 1))
            ka_ref[:, sl] = _rope(p[:, PO_KA + LANE * c:PO_KA + LANE * (c + 1)], ca, sa1, sa2, 32).astype(BF16)
        va_ref[...] = p[:, PO_VA:PO_CQ].astype(BF16)
        cq = p[:, PO_CQ:PO_CKV]
        ckv = p[:, PO_CKV:PO_KR]
        cq_ref[...] = cq
        ckv_ref[...] = ckv
        qn = (cq * _rms_r(cq, MLA_Q_RANK) * gq_ref[...]).astype(BF16)
        qn_ref[...] = qn
        qb = _dot_nt(qn, wqu_ref[...])
        kvn = (ckv * _rms_r(ckv, MLA_KV_RANK) * gkv_ref[...]).astype(BF16)
        kvn_ref[...] = kvn
        kv = _dot_nt(kvn, wkv_ref[...])
        kr = _rope(p[:, PO_KR:PW_IN], ck, sb1, sb2, 16)
        for c in range(MLA_HEADS):
            sl = slice(LANE * c, LANE * (c + 1))
            qb_ref[:, sl] = (_rope(qb[:, sl], cb, sb1, sb2, 16) * Q_SCALE).astype(BF16)
            kf_ref[:, sl] = (kv[:, sl] + kr).astype(BF16)
        vb_ref[...] = kv[:, HP:].astype(BF16)

    widths = [(D_MODEL, BF16), (HP, BF16), (2 * LANE, BF16), (2 * LANE, BF16), (MLA_Q_RANK, F32),
              (MLA_KV_RANK, F32), (MLA_Q_RANK, BF16), (MLA_KV_RANK, BF16), (HP, BF16), (HP, BF16), (HP, BF16)]
    return pl.pallas_call(
        body, name="pre_fwd", grid=(t // tm,),
        in_specs=[_row(tm, D_MODEL), _const(g1.shape), _const(win.shape), _const(gq.shape), _const(wqu.shape),
                  _const(gkv.shape), _const(wkv.shape), _row(tm, N_TAB * LANE)],
        out_specs=[_row(tm, w) for w, _ in widths],
        out_shape=[jax.ShapeDtypeStruct((t, w), d) for w, d in widths],
        compiler_params=_params("parallel"),
    )(h, g1, win, gq, wqu, gkv, wkv, tabs)


def _swa_mask(nb):
    key = lax.broadcasted_iota(jnp.int32, (2 * BLOCK, SWA_GROUP * BLOCK), 0)
    qry = lax.broadcasted_iota(jnp.int32, (2 * BLOCK, SWA_GROUP * BLOCK), 1) & (BLOCK - 1)
    return (key > qry) & (key <= qry + BLOCK) & (key + (nb - 1) * BLOCK >= FRONT)


def _swa_group(ref, rows, j):
    return jnp.concatenate([ref[rows, LANE * (SWA_GROUP * j + g):LANE * (SWA_GROUP * j + g + 1)]
                            for g in range(SWA_GROUP)], axis=0)


def _swa_packed_group(ref, rows, j):
    heads = [SWA_GROUP * j + g for g in range(SWA_GROUP)]
    return jnp.concatenate([_pair_half(ref[rows, LANE * (hd // 2):LANE * (hd // 2 + 1)], hd % 2) for hd in heads], axis=0)


def _swa_sinks(sink_ref, j):
    return jnp.concatenate([jnp.full((1, BLOCK), sink_ref[0, SWA_GROUP * j + g], F32) for g in range(SWA_GROUP)], axis=1)


def _swa_keys(prev_ref, cur_ref, rb, j):
    sl = slice(LANE * j, LANE * (j + 1))
    if rb == 0:
        return jnp.concatenate([prev_ref[:, sl], cur_ref[:BLOCK, sl]], axis=0)
    return cur_ref[BLOCK * (rb - 1):BLOCK * (rb + 1), sl]


def _swa_chains(t):
    return [(rb, j) for rb in range(_tile(t) // BLOCK) for j in range(SWA_KV_HEADS)]


def _swa_scores(sink_ref, q_ref, kp_ref, kc_ref, n, t):
    r = _tile(t) // BLOCK
    chains = _swa_chains(t)
    qs = [_swa_group(q_ref, slice(BLOCK * rb, BLOCK * (rb + 1)), j) for rb, j in chains]
    ks = [_swa_keys(kp_ref, kc_ref, rb, j) for rb, j in chains]
    ss = [_dot_nt(k2, q4) for q4, k2 in zip(qs, ks)]
    masks = [_swa_mask(n * r + rb) for rb in range(r)]
    out = []
    for (rb, j), s in zip(chains, ss):
        sink = _swa_sinks(sink_ref, j)
        s = jnp.where(masks[rb], s * SCALE_A, NEG)
        m = jnp.maximum(jnp.max(s, axis=0, keepdims=True), sink)
        e = jnp.exp(s - m)
        es = jnp.exp(sink - m)
        inv = 1.0 / (jnp.sum(e, axis=0, keepdims=True) + es)
        out.append((e * inv, es * inv))
    return qs, ks, out


def _swa_specs(t):
    ts = _tile(t)
    r = ts // BLOCK
    prev = lambda n: (jnp.maximum(n * r - 1, 0), 0)
    cur = lambda n: (n, 0)
    return [pl.BlockSpec(memory_space=pltpu.SMEM), pl.BlockSpec((ts, HP), cur),
            pl.BlockSpec((BLOCK, 2 * LANE), prev), pl.BlockSpec((ts, 2 * LANE), cur),
            pl.BlockSpec((BLOCK, 2 * LANE), prev), pl.BlockSpec((ts, 2 * LANE), cur)]


def _swa_fwd(sinks, q, k, v):
    t = q.shape[0]
    ts = _tile(t)

    def body(sink_ref, q_ref, kp_ref, kc_ref, vp_ref, vc_ref, o_ref):
        chains = _swa_chains(t)
        _, _, probs = _swa_scores(sink_ref, q_ref, kp_ref, kc_ref, pl.program_id(0), t)
        os_ = [_dot_tn(_swa_keys(vp_ref, vc_ref, rb, j)[:, :HALF], p.astype(BF16)) for (rb, j), (p, _) in zip(chains, probs)]
        for (rb, j), o4 in zip(chains, os_):
            for g in range(0, SWA_GROUP, 2):
                pair = (SWA_GROUP * j + g) // 2
                o_ref[BLOCK * rb:BLOCK * (rb + 1), LANE * pair:LANE * (pair + 1)] = jnp.concatenate(
                    [o4[:, BLOCK * g:BLOCK * (g + 1)], o4[:, BLOCK * (g + 1):BLOCK * (g + 2)]], axis=0).T

    return pl.pallas_call(
        body, name="swa_fwd", grid=(t // ts,),
        in_specs=_swa_specs(t),
        out_specs=pl.BlockSpec((ts, SWA_Q_W), lambda n: (n, 0)),
        out_shape=jax.ShapeDtypeStruct((t, SWA_Q_W), F32),
        compiler_params=_params("parallel"),
    )(sinks, q, k, k, v, v)


def _causal_mask(q0, k0, tq, tk, transposed):
    if transposed:
        key = k0 + lax.broadcasted_iota(jnp.int32, (tk, tq), 0)
        qry = q0 + lax.broadcasted_iota(jnp.int32, (tk, tq), 1)
    else:
        qry = q0 + lax.broadcasted_iota(jnp.int32, (tq, tk), 0)
        key = k0 + lax.broadcasted_iota(jnp.int32, (tq, tk), 1)
    return (key <= qry) & (key >= FRONT)


def _heads(ref, hb, rows=slice(None)):
    return [ref[rows, LANE * a:LANE * (a + 1)] for a in range(hb)]


def _head_stats(t, hb=MLA_HB):
    return jax.ShapeDtypeStruct((MLA_HEADS // hb, t, hb), F32)


def _mla_fwd(q, k, v, shards=()):
    t = q.shape[0]
    tq = _tile(t)
    nq = t // tq
    n = len(shards)
    hb = MLA_HB_FWD
    steps = (MLA_HEADS // hb) * nq

    def body(q_ref, k_ref, v_ref, *rest):
        x_refs, (o_ref, lse_ref), out_refs = rest[:n], rest[n:n + 2], rest[n + 2:2 * n + 2]
        acc_sc, sems = rest[2 * n + 2], rest[2 * n + 3:]
        i = pl.program_id(1)
        step_id = pl.program_id(0) * nq + i
        if n:
            plan = _gather_plan(x_refs, out_refs, *sems)
            pl.when(step_id == 0)(plan.start)
            pl.when(step_id == (3 * steps) // 4)(plan.forward)
        qs = _heads(q_ref, hb)
        acc_sc[...] = jnp.zeros(acc_sc.shape, F32)

        def step(j, carry, masked):
            rows = pl.ds(pl.multiple_of(j * tq, tq), tq)
            ks = _heads(k_ref, hb, rows)
            vs = [v_ref[rows, LANE * a:LANE * a + HALF] for a in range(hb)]
            ss = [_dot_nt(kh, qh) for qh, kh in zip(qs, ks)]
            if masked:
                mask = _causal_mask(i * tq, j * tq, tq, tq, True)
                ss = [jnp.where(mask, s, NEG) for s in ss]
            mid, out = [], []
            for s, (m, l) in zip(ss, carry):
                mn = jnp.maximum(m, jnp.max(s, axis=0, keepdims=True))
                al = jnp.exp2(m - mn)
                p = jnp.exp2(s - mn)
                out.append((mn, al * l + jnp.sum(p, axis=0, keepdims=True)))
                mid.append((al, p.astype(BF16)))
            for a, ((al, p), vh) in enumerate(zip(mid, vs)):
                acc_sc[a] = al * acc_sc[a] + _dot_tn(vh, p)
            return tuple(out)

        init = ((jnp.full((1, tq), NEG, F32), jnp.zeros((1, tq), F32)),) * hb
        carry = lax.fori_loop(0, jnp.minimum(i, 1) + 1, lambda it, c: step(it * i, c, True), init)
        carry = lax.fori_loop(1, i, lambda j, c: step(j, c, False), carry)
        outs = [acc_sc[a] * (1.0 / l) for a, (_, l) in enumerate(carry)]
        for a in range(0, hb, 2):
            o_ref[:, HALF * a:HALF * (a + 2)] = jnp.concatenate(outs[a:a + 2], axis=0).T
        for a, (m, l) in enumerate(carry):
            lse_ref[:, a:a + 1] = jnp.broadcast_to(m + jnp.log2(l), (LANE, tq)).T[:, :1]
        if n:
            pl.when(step_id == steps - 1)(plan.finish)

    blk = pl.BlockSpec((tq, hb * LANE), lambda h, i: (i, h))
    full = pl.BlockSpec((t, hb * LANE), lambda h, i: (0, h))
    packed = pl.BlockSpec((tq, hb * HALF), lambda h, i: (i, h))
    out = pl.pallas_call(
        body, name="mla_fwd_gather" if n else "mla_fwd", grid=(MLA_HEADS // hb, nq),
        in_specs=[blk, full, full] + [ANY] * n,
        out_specs=[packed, pl.BlockSpec((None, tq, hb), lambda h, i: (h, i, 0))] + [ANY] * n,
        out_shape=[jax.ShapeDtypeStruct((t, MLA_OUT_W), F32), _head_stats(t, hb)]
        + [jax.ShapeDtypeStruct((N_DEV,) + a.shape, a.dtype) for a in shards],
        scratch_shapes=[pltpu.VMEM((hb, HALF, tq), F32)] + (_comm_sems(n) if n else []),
        compiler_params=_params("arbitrary", "arbitrary"),
    )(q, k, v, *shards)
    return out[0], out[1], out[2:]


def _mix_fwd(h, oa, ob, ga, gb, wo, g2):
    t = h.shape[0]
    tm = _tile(t)

    def body(h_ref, oa_ref, ob_ref, ga_ref, gb_ref, wo_ref, g2_ref, h2_ref, mix_ref, u2_ref):
        oa_v = oa_ref[...]
        ob_v = ob_ref[...]
        na = (oa_v * _rms_r(oa_v, SWA_Q_W) * ga_ref[...]).astype(BF16)
        nb = (ob_v * _rms_r(ob_v, MLA_OUT_W) * gb_ref[...]).astype(BF16)
        mix_ref[:, :SWA_Q_W] = na
        mix_ref[:, SWA_Q_W:] = nb
        h2 = h_ref[...] + _dot(na, wo_ref[:SWA_Q_W, :]) + _dot(nb, wo_ref[SWA_Q_W:, :])
        h2_ref[...] = h2
        u2_ref[...] = (h2 * _rms_r(h2, D_MODEL) * g2_ref[...]).astype(BF16)

    mix_w = SWA_Q_W + MLA_OUT_W
    return pl.pallas_call(
        body, name="mix_fwd", grid=(t // tm,),
        in_specs=[_row(tm, D_MODEL), _row(tm, SWA_Q_W), _row(tm, MLA_OUT_W), _const(ga.shape), _const(gb.shape),
                  _const(wo.shape), _const(g2.shape)],
        out_specs=[_row(tm, D_MODEL), _row(tm, mix_w), _row(tm, D_MODEL)],
        out_shape=[jax.ShapeDtypeStruct((t, D_MODEL), F32), jax.ShapeDtypeStruct((t, mix_w), BF16),
                   jax.ShapeDtypeStruct((t, D_MODEL), BF16)],
        compiler_params=_params("parallel"),
    )(h, oa, ob, ga, gb, wo, g2)


def _ffn_fwd(h2, u2, wg_t, wu_t, wd, shards=()):
    t = h2.shape[0]
    tm = _tile(t)
    dff = wd.shape[0]
    n = len(shards)
    steps = t // tm

    def body(h2_ref, u2_ref, wg_ref, wu_ref, wd_ref, *rest):
        x_refs, (h3_ref, g_ref, up_ref), out_refs, sems = rest[:n], rest[n:n + 3], rest[n + 3:2 * n + 3], rest[2 * n + 3:]
        if n:
            plan = _gather_plan(x_refs, out_refs, *sems)
            pl.when(pl.program_id(0) == 0)(plan.start)
            pl.when(pl.program_id(0) == (3 * steps) // 4)(plan.forward)
        u2v = u2_ref[...]
        g = _dot_nt(u2v, wg_ref[...])
        up = _dot_nt(u2v, wu_ref[...])
        g_ref[...] = g.astype(BF16)
        up_ref[...] = up.astype(BF16)
        a = (g * jax.nn.sigmoid(g) * up).astype(BF16)
        h3_ref[...] = h2_ref[...] + _dot(a, wd_ref[...])
        if n:
            pl.when(pl.program_id(0) == steps - 1)(plan.finish)

    out = pl.pallas_call(
        body, name="ffn_fwd_gather" if n else "ffn_fwd", grid=(steps,),
        in_specs=[_row(tm, D_MODEL), _row(tm, D_MODEL), _const(wg_t.shape), _const(wu_t.shape), _const(wd.shape)] + [ANY] * n,
        out_specs=[_row(tm, D_MODEL), _row(tm, dff), _row(tm, dff)] + [ANY] * n,
        out_shape=[jax.ShapeDtypeStruct((t, D_MODEL), F32), jax.ShapeDtypeStruct((t, dff), BF16),
                   jax.ShapeDtypeStruct((t, dff), BF16)] + [jax.ShapeDtypeStruct((N_DEV,) + a.shape, a.dtype) for a in shards],
        scratch_shapes=_comm_sems(n) if n else [],
        compiler_params=_params("arbitrary" if n else "parallel"),
    )(h2, u2, wg_t, wu_t, wd, *shards)
    return out[:3], out[3:]


def _loss_bwd(h, gf, target):
    t = h.shape[0]
    tm = _tile(t)
    r = tm // BLOCK
    assert FRONT + N_META == BLOCK and target.shape[0] == t - BLOCK

    def body(h_ref, gf_ref, *rest):
        t_refs, (dh_ref, dgf_ref, loss_ref) = rest[:r], rest[r:]
        i = pl.program_id(0)
        hv = h_ref[...]
        y = hv * _rms_r(hv, D_MODEL) * gf_ref[...]
        row = i * tm + lax.broadcasted_iota(jnp.int32, (tm, 1), 0)
        tv = jnp.concatenate([t_ref[...] for t_ref in t_refs], axis=0)
        err = jnp.where(row >= BLOCK, y - tv, 0.0)
        dx, dg = _rms_bwd(hv, gf_ref[...], err * (1.0 / D_MODEL), D_MODEL)
        dh_ref[...] = dx
        _acc(dgf_ref, dg, i == 0)
        part = 0.5 * jnp.sum(jnp.sum(err * err, axis=1, keepdims=True) * (1.0 / D_MODEL), axis=0, keepdims=True)
        _acc(loss_ref, jnp.broadcast_to(part, (1, LANE)), i == 0)

    t_specs = [pl.BlockSpec((BLOCK, D_MODEL), lambda i, b=b: (jnp.maximum(r * i + b - 1, 0), 0)) for b in range(r)]
    return pl.pallas_call(
        body, name="loss_bwd", grid=(t // tm,),
        in_specs=[_row(tm, D_MODEL), _const(gf.shape)] + t_specs,
        out_specs=[_row(tm, D_MODEL), _const((1, D_MODEL)), _const((1, LANE))],
        out_shape=[jax.ShapeDtypeStruct((t, D_MODEL), F32), jax.ShapeDtypeStruct((1, D_MODEL), F32),
                   jax.ShapeDtypeStruct((1, LANE), F32)],
        compiler_params=_params("arbitrary"),
    )(h, gf, *[target] * r)


def _tn_matmul(a, b, name, cols=None, keep=None):
    t, n = b.shape
    first, k = cols or (0, a.shape[1])
    tk = next(c for c in (k, 1024, 512, 256, 128) if k % c == 0 and first % c == 0 and c <= 1024)
    fits = lambda c: 2 * (t * (tk + c) * 2 + tk * c * 2) <= TN_VMEM_BUDGET
    tn = next(c for c in (n, 1024, 512, 256, 128) if n % c == 0 and fits(c))
    kept = tk if keep is None else sum(size for _, size in keep)

    def body(a_ref, b_ref, o_ref):
        if keep is None:
            o_ref[...] = _dot_tn(a_ref[...], b_ref[...]).astype(BF16)
        else:
            at = a_ref[...].T
            at = jnp.concatenate([at[start:start + size] for start, size in keep], axis=0)
            o_ref[...] = _dot(at, b_ref[...]).astype(BF16)

    return pl.pallas_call(
        body, name=name, grid=(k // tk, n // tn),
        in_specs=[pl.BlockSpec((t, tk), lambda i, j: (0, i + first // tk)), pl.BlockSpec((t, tn), lambda i, j: (0, j))],
        out_specs=pl.BlockSpec((kept, tn), lambda i, j: (i, j)),
        out_shape=jax.ShapeDtypeStruct((k // tk * kept, n), BF16),
        compiler_params=_params("parallel", "parallel"),
    )(a, b)


def _ffn_bwd_a(dh3, g, up, wd):
    t = dh3.shape[0]
    tm = _tile(t)
    dff = wd.shape[0]

    def body(dh3_ref, g_ref, up_ref, wd_ref, a_ref, dgu_ref, dh3b_ref):
        dh3b = dh3_ref[...].astype(BF16)
        dh3b_ref[...] = dh3b
        da = _dot_nt(dh3b, wd_ref[...])
        gv = g_ref[...].astype(F32)
        upv = up_ref[...].astype(F32)
        sg = jax.nn.sigmoid(gv)
        silu = gv * sg
        a_ref[...] = (silu * upv).astype(BF16)
        dgu_ref[:, :dff] = (da * upv * (sg * (1.0 + gv * (1.0 - sg)))).astype(BF16)
        dgu_ref[:, dff:] = (da * silu).astype(BF16)

    return pl.pallas_call(
        body, name="ffn_bwd_a", grid=(t // tm,),
        in_specs=[_row(tm, D_MODEL), _row(tm, dff), _row(tm, dff), _const(wd.shape)],
        out_specs=[_row(tm, dff), _row(tm, 2 * dff), _row(tm, D_MODEL)],
        out_shape=[jax.ShapeDtypeStruct((t, dff), BF16), jax.ShapeDtypeStruct((t, 2 * dff), BF16),
                   jax.ShapeDtypeStruct((t, D_MODEL), BF16)],
        compiler_params=_params("parallel"),
    )(dh3, g, up, wd)


def _ffn_bwd_b(dh3, dgu, h2, g2, wg_t, wu_t):
    t = dh3.shape[0]
    tm = _tile(t)
    dff = wg_t.shape[0]

    def body(dh3_ref, dgu_ref, h2_ref, g2_ref, wg_ref, wu_ref, dh2_ref, dh2b_ref, dg2_ref):
        du2 = _dot(dgu_ref[:, :dff], wg_ref[...]) + _dot(dgu_ref[:, dff:], wu_ref[...])
        dx, dg = _rms_bwd(h2_ref[...], g2_ref[...], du2, D_MODEL)
        dh2 = dh3_ref[...] + dx
        dh2_ref[...] = dh2
        dh2b_ref[...] = dh2.astype(BF16)
        _acc(dg2_ref, dg, pl.program_id(0) == 0)

    return pl.pallas_call(
        body, name="ffn_bwd_b", grid=(t // tm,),
        in_specs=[_row(tm, D_MODEL), _row(tm, 2 * dff), _row(tm, D_MODEL), _const(g2.shape), _const(wg_t.shape),
                  _const(wu_t.shape)],
        out_specs=[_row(tm, D_MODEL), _row(tm, D_MODEL), _const((1, D_MODEL))],
        out_shape=[jax.ShapeDtypeStruct((t, D_MODEL), F32), jax.ShapeDtypeStruct((t, D_MODEL), BF16),
                   jax.ShapeDtypeStruct((1, D_MODEL), F32)],
        compiler_params=_params("arbitrary"),
    )(dh3, dgu, h2, g2, wg_t, wu_t)


def _mix_bwd(dh2, oa, ob, ga, gb, wo):
    t = dh2.shape[0]
    tm = _tile(t)

    def body(dh2_ref, oa_ref, ob_ref, ga_ref, gb_ref, wo_ref, doa_ref, dob_ref, dl_ref, dga_ref, dgb_ref):
        first = pl.program_id(0) == 0
        d = dh2_ref[...]
        ob_v = ob_ref[...]
        dxa, dga = _rms_bwd(oa_ref[...], ga_ref[...], _dot_nt(d, wo_ref[:SWA_Q_W, :]), SWA_Q_W)
        dxb, dgb = _rms_bwd(ob_v, gb_ref[...], _dot_nt(d, wo_ref[SWA_Q_W:, :]), MLA_OUT_W)
        lower = lax.broadcasted_iota(jnp.int32, (tm, LANE), 1) < HALF
        for hd in range(MLA_HEADS):
            sl = slice(LANE * (hd // 2), LANE * (hd // 2 + 1))
            mine = lower if hd % 2 == 0 else jnp.logical_not(lower)
            delta = jnp.sum(jnp.where(mine, ob_v[:, sl] * dxb[:, sl], 0.0), axis=1, keepdims=True)
            dl_ref[hd // MLA_HB, :, hd % MLA_HB:hd % MLA_HB + 1] = delta
        for ref, dx, heads in ((doa_ref, dxa, SWA_HEADS), (dob_ref, dxb, MLA_HEADS)):
            for hd in range(heads):
                slab = dx[:, LANE * (hd // 2):LANE * (hd // 2 + 1)]
                ref[:, LANE * hd:LANE * (hd + 1)] = _unpack_pair(slab, hd % 2).astype(BF16)
        _acc(dga_ref, dga, first)
        _acc(dgb_ref, dgb, first)

    return pl.pallas_call(
        body, name="mix_bwd", grid=(t // tm,),
        in_specs=[_row(tm, D_MODEL), _row(tm, SWA_Q_W), _row(tm, MLA_OUT_W), _const(ga.shape), _const(gb.shape),
                  _const(wo.shape)],
        out_specs=[_row(tm, HP), _row(tm, HP), pl.BlockSpec((MLA_HEADS // MLA_HB, tm, MLA_HB), lambda i: (0, i, 0)),
                   _const((1, SWA_Q_W)), _const((1, MLA_OUT_W))],
        out_shape=[jax.ShapeDtypeStruct((t, HP), BF16), jax.ShapeDtypeStruct((t, HP), BF16), _head_stats(t),
                   jax.ShapeDtypeStruct((1, SWA_Q_W), F32), jax.ShapeDtypeStruct((1, MLA_OUT_W), F32)],
        compiler_params=_params("arbitrary"),
    )(dh2, oa, ob, ga, gb, wo)


def _swa_bwd(sinks, q, k, v, o, do):
    t = q.shape[0]
    ts = _tile(t)

    def body(sink_ref, q_ref, kp_ref, kc_ref, vp_ref, vc_ref, o_ref, do_ref,
             dq_ref, dkc_ref, dkp_ref, dvc_ref, dvp_ref, dsink_ref):
        n = pl.program_id(0)
        chains = _swa_chains(t)
        qs, ks, probs = _swa_scores(sink_ref, q_ref, kp_ref, kc_ref, n, t)
        dos = [_swa_group(do_ref, slice(BLOCK * rb, BLOCK * (rb + 1)), j) for rb, j in chains]
        vs = [_swa_keys(vp_ref, vc_ref, rb, j) for rb, j in chains]
        dps = [_dot_nt(v2, do4) for do4, v2 in zip(dos, vs)]
        dss, dsks = [], []
        for (rb, j), (p, psink), do4, dp in zip(chains, probs, dos, dps):
            o4 = _swa_packed_group(o_ref, slice(BLOCK * rb, BLOCK * (rb + 1)), j)
            delta = jnp.sum(o4 * do4.astype(F32), axis=1, keepdims=True)
            delta = jnp.broadcast_to(delta, (SWA_GROUP * BLOCK, LANE)).T[:1, :]
            dss.append((p * (dp - delta) * SCALE_A).astype(BF16))
            dsks.append(-psink * delta)
        dqs = [_dot_tn(k2[:, :HALF], ds) for ds, k2 in zip(dss, ks)]
        dks = [_dot(ds, q4) for ds, q4 in zip(dss, qs)]
        dvs = [_dot(p.astype(BF16), do4) for (p, _), do4 in zip(probs, dos)]
        dsink = [jnp.zeros((1, LANE), F32)] * SWA_HEADS
        ext = {}
        for (rb, j), dq4, dk2, dv2, dsk in zip(chains, dqs, dks, dvs, dsks):
            for g in range(SWA_GROUP):
                hd = SWA_GROUP * j + g
                cols = slice(BLOCK * g, BLOCK * (g + 1))
                dq_ref[BLOCK * rb:BLOCK * (rb + 1), LANE * hd:LANE * (hd + 1)] = jnp.concatenate(
                    [dq4[:, cols], jnp.zeros((HALF, BLOCK), F32)], axis=0).T.astype(BF16)
                dsink[hd] = dsink[hd] + jnp.sum(dsk[:, cols], axis=1, keepdims=True)
            for half in range(2):
                key = (j, rb + half)
                part = (dk2[BLOCK * half:BLOCK * (half + 1)], dv2[BLOCK * half:BLOCK * (half + 1)])
                ext[key] = part if key not in ext else (ext[key][0] + part[0], ext[key][1] + part[1])
        for (j, blk), (dk, dv) in ext.items():
            sl = slice(LANE * j, LANE * (j + 1))
            if blk == 0:
                dkp_ref[:, sl] = dk
                dvp_ref[:, sl] = dv
            else:
                dkc_ref[BLOCK * (blk - 1):BLOCK * blk, sl] = dk
                dvc_ref[BLOCK * (blk - 1):BLOCK * blk, sl] = dv
        for hd in range(SWA_HEADS):
            _acc(dsink_ref.at[hd:hd + 1, :], jnp.broadcast_to(dsink[hd], (1, LANE)), n == 0)

    cur = lambda n: (n, 0)
    kv = pl.BlockSpec((ts, 2 * LANE), cur)
    kvp = pl.BlockSpec((BLOCK, 2 * LANE), cur)
    hp = pl.BlockSpec((ts, HP), cur)
    kvs = jax.ShapeDtypeStruct((t, 2 * LANE), F32)
    kvps = jax.ShapeDtypeStruct((t // ts * BLOCK, 2 * LANE), F32)
    return pl.pallas_call(
        body, name="swa_bwd", grid=(t // ts,),
        in_specs=_swa_specs(t) + [pl.BlockSpec((ts, SWA_Q_W), cur), hp],
        out_specs=[hp, kv, kvp, kv, kvp, _const((SWA_HEADS, LANE))],
        out_shape=[jax.ShapeDtypeStruct((t, HP), BF16), kvs, kvps, kvs, kvps,
                   jax.ShapeDtypeStruct((SWA_HEADS, LANE), F32)],
        compiler_params=_params("arbitrary"),
    )(sinks, q, k, k, v, v, o, do)


def _mla_bwd(q, k, v, do, lse, dl, slabs=()):
    t = q.shape[0]
    tq = _tile(t)
    nq = t // tq
    n = len(slabs)
    hb = MLA_HB
    steps = (MLA_HEADS // hb) * nq

    def body(k_ref, v_ref, q_ref, do_ref, lse_ref, dl_ref, *rest):
        in_refs, (dq_ref, dk_ref, dv_ref), out_refs = rest[:n], rest[n:n + 3], rest[n + 3:2 * n + 3]
        (dq_sc, dk_sc, dv_sc), sems = rest[2 * n + 3:2 * n + 6], rest[2 * n + 6:]
        j = pl.program_id(1)
        step_id = pl.program_id(0) * nq + j
        if n:
            plan = _exchange_plan(in_refs, out_refs, *sems)
            pl.when(step_id == 0)(plan.start)

        @pl.when(j == 0)
        def _():
            dq_sc[...] = jnp.zeros(dq_sc.shape, F32)

        dk_sc[...] = jnp.zeros(dk_sc.shape, F32)
        dv_sc[...] = jnp.zeros(dv_sc.shape, F32)
        ks, vs = _heads(k_ref, hb), _heads(v_ref, hb)

        def step(i, carry, masked):
            rows = pl.ds(pl.multiple_of(i * tq, tq), tq)
            qs, dos = _heads(q_ref, hb, rows), _heads(do_ref, hb, rows)
            ss = [_dot_nt(qh, kh) for qh, kh in zip(qs, ks)]
            dps = [_dot_nt(doh, vh) for doh, vh in zip(dos, vs)]
            if masked:
                mask = _causal_mask(i * tq, j * tq, tq, tq, False)
                ss = [jnp.where(mask, s_, NEG) for s_ in ss]
            ps = [jnp.exp2(s_ - lse_ref[rows, a:a + 1]) for a, s_ in enumerate(ss)]
            dss = [(p * (dp - dl_ref[rows, a:a + 1])).astype(BF16) for a, (p, dp) in enumerate(zip(ps, dps))]
            for a, (ds, p, qh, kh, doh) in enumerate(zip(dss, ps, qs, ks, dos)):
                dq_sc[a, rows, :] += _dot(ds, kh)
                dk_sc[a, :MLA_QK_DIM, :] += _dot_tn(qh[:, :MLA_QK_DIM], ds)
                dv_sc[a, :MLA_V_DIM, :] += _dot_tn(doh[:, :MLA_V_DIM], p.astype(BF16))
            return carry

        split = jnp.where(j == 0, nq, j + 1)
        lax.fori_loop(j, split, lambda i, c: step(i, c, True), 0)
        lax.fori_loop(split, nq, lambda i, c: step(i, c, False), 0)
        for a in range(hb):
            dk_ref[:, LANE * a:LANE * (a + 1)] = (dk_sc[a] * (1.0 / LOG2E)).T.astype(BF16)
            dv_ref[:, LANE * a:LANE * (a + 1)] = dv_sc[a].T.astype(BF16)

        @pl.when(j == nq - 1)
        def _():
            for a in range(hb):
                dq_ref[:, LANE * a:LANE * (a + 1)] = (dq_sc[a] * SCALE_B).astype(BF16)

        if n:
            pl.when(step_id == steps - 1)(plan.finish)

    blk = pl.BlockSpec((tq, hb * LANE), lambda h, j: (j, h))
    full = pl.BlockSpec((t, hb * LANE), lambda h, j: (0, h))
    cols = pl.BlockSpec((None, t, hb), lambda h, j: (h, 0, 0))
    out = pl.pallas_call(
        body, name="mla_bwd_exchange" if n else "mla_bwd", grid=(MLA_HEADS // hb, nq),
        in_specs=[blk, blk, full, full, cols, cols] + [ANY] * n, out_specs=[full, blk, blk] + [ANY] * n,
        out_shape=[jax.ShapeDtypeStruct((t, HP), BF16)] * 3 + [jax.ShapeDtypeStruct(a.shape, a.dtype) for a in slabs],
        scratch_shapes=[pltpu.VMEM((hb, t, LANE), F32)] + [pltpu.VMEM((hb, LANE, tq), F32)] * 2
        + (_comm_sems(n) if n else []),
        compiler_params=_params("arbitrary", "arbitrary"),
    )(k, v, q, do, lse, dl, *slabs)
    return out[:3], out[3:]


def _pre_bwd(dh2, h, cq, ckv, dqa, dka, dka_next, dva, dva_next, dqb, dkf, dvb, g1, win, gq, wqu, gkv, wkv, tabs):
    t = h.shape[0]
    tm = _tile(t)

    def body(dh2_ref, h_ref, cq_ref, ckv_ref, dqa_ref, dka_ref, dkan_ref, dva_ref, dvan_ref, dqb_ref, dkf_ref, dvb_ref,
             g1_ref, win_ref, gq_ref, wqu_ref, gkv_ref, wkv_ref, tab_ref,
             dh_ref, dp_ref, dqbo_ref, dkvo_ref, dg1_ref, dgq_ref, dgkv_ref):
        first = pl.program_id(0) == 0
        ca, sa1, sa2, cb, sb1, sb2, ck = _tabs(tab_ref)
        dkr = jnp.zeros((tm, LANE), F32)
        for c in range(MLA_HEADS):
            sl = slice(LANE * c, LANE * (c + 1))
            dqbo_ref[:, sl] = _rope_t(dqb_ref[:, sl].astype(F32), cb, sb1, sb2, 16).astype(BF16)
            dkr += dkf_ref[:, sl].astype(F32)
        dkvo_ref[:, :HP] = dkf_ref[...]
        dkvo_ref[:, HP:] = dvb_ref[...]
        dcq, dgq = _rms_bwd(cq_ref[...], gq_ref[...], _dot(dqbo_ref[...], wqu_ref[...]), MLA_Q_RANK)
        dckv, dgkv = _rms_bwd(ckv_ref[...], gkv_ref[...], _dot(dkvo_ref[...], wkv_ref[...]), MLA_KV_RANK)
        for c in range(SWA_HEADS):
            sl = slice(LANE * c, LANE * (c + 1))
            dp_ref[:, PO_QA + LANE * c:PO_QA + LANE * (c + 1)] = _rope_t(dqa_ref[:, sl].astype(F32), ca, sa1, sa2,
                                                                          32).astype(BF16)
        last = slice(tm - BLOCK, tm)
        more = pl.program_id(0) < t // tm - 1
        for c in range(SWA_KV_HEADS):
            sl = slice(LANE * c, LANE * (c + 1))
            dk = dka_ref[:, sl]
            dk_last = dk[tm - BLOCK:] + jnp.where(more, dkan_ref[:, sl], 0.0)
            cols = slice(PO_KA + LANE * c, PO_KA + LANE * (c + 1))
            if tm > BLOCK:
                dp_ref[:tm - BLOCK, cols] = _rope_t(dk[:tm - BLOCK], ca[:tm - BLOCK], sa1[:tm - BLOCK], sa2[:tm - BLOCK],
                                                    32).astype(BF16)
            dp_ref[last, cols] = _rope_t(dk_last, ca[tm - BLOCK:], sa1[tm - BLOCK:], sa2[tm - BLOCK:], 32).astype(BF16)
        if tm > BLOCK:
            dp_ref[:tm - BLOCK, PO_VA:PO_CQ] = dva_ref[:tm - BLOCK, :].astype(BF16)
        dp_ref[last, PO_VA:PO_CQ] = (dva_ref[tm - BLOCK:, :] + jnp.where(more, dvan_ref[...], 0.0)).astype(BF16)
        dp_ref[:, PO_CQ:PO_CKV] = dcq.astype(BF16)
        dp_ref[:, PO_CKV:PO_KR] = dckv.astype(BF16)
        dp_ref[:, PO_KR:PW_IN] = _rope_t(dkr, ck, sb1, sb2, 16).astype(BF16)
        dx, dg1 = _rms_bwd(h_ref[...], g1_ref[...], _dot(dp_ref[...], win_ref[...]), D_MODEL)
        dh_ref[...] = dh2_ref[...] + dx
        _acc(dg1_ref, dg1, first)
        _acc(dgq_ref, dgq, first)
        _acc(dgkv_ref, dgkv, first)

    kv = _row(tm, 2 * LANE)
    nxt = pl.BlockSpec((BLOCK, 2 * LANE), lambda i: (jnp.minimum(i + 1, t // tm - 1), 0))
    return pl.pallas_call(
        body, name="pre_bwd", grid=(t // tm,),
        in_specs=[_row(tm, D_MODEL), _row(tm, D_MODEL), _row(tm, MLA_Q_RANK), _row(tm, MLA_KV_RANK), _row(tm, HP),
                  kv, nxt, kv, nxt, _row(tm, HP), _row(tm, HP), _row(tm, HP),
                  _const(g1.shape), _const(win.shape), _const(gq.shape), _const(wqu.shape), _const(gkv.shape),
                  _const(wkv.shape), _row(tm, N_TAB * LANE)],
        out_specs=[_row(tm, D_MODEL), _row(tm, PW_IN), _row(tm, HP), _row(tm, 2 * HP),
                   _const((1, D_MODEL)), _const((1, MLA_Q_RANK)), _const((1, MLA_KV_RANK))],
        out_shape=[jax.ShapeDtypeStruct((t, D_MODEL), F32), jax.ShapeDtypeStruct((t, PW_IN), BF16),
                   jax.ShapeDtypeStruct((t, HP), BF16), jax.ShapeDtypeStruct((t, 2 * HP), BF16),
                   jax.ShapeDtypeStruct((1, D_MODEL), F32), jax.ShapeDtypeStruct((1, MLA_Q_RANK), F32),
                   jax.ShapeDtypeStruct((1, MLA_KV_RANK), F32)],
        compiler_params=_params("arbitrary"),
    )(dh2, h, cq, ckv, dqa, dka, dka_next, dva, dva_next, dqb, dkf, dvb, g1, win, gq, wqu, gkv, wkv, tabs)


def _rope_tables(t):
    pos = (jnp.arange(t, dtype=jnp.int32) - FRONT).astype(F32)[:, None]
    lane = jnp.arange(LANE)[None, :]

    def table(dim, start):
        half = dim // 2
        inv = ROPE_THETA ** (-jnp.arange(0, dim, 2, dtype=F32) / dim)
        ang = pos * inv[None, :]
        cos = jnp.concatenate([jnp.cos(ang)] * 2, axis=1)
        sin = jnp.concatenate([jnp.sin(ang)] * 2, axis=1)
        pad = lambda a: jnp.pad(a, ((0, 0), (start, LANE - start - dim)))
        first = (lane >= start) & (lane < start + half)
        second = (lane >= start + half) & (lane < start + dim)
        return pad(cos), jnp.where(first, -pad(sin), 0.0), jnp.where(second, pad(sin), 0.0)

    ca, sa1, sa2 = table(SWA_HEAD_DIM, 0)
    ck, sb1, sb2 = table(MLA_ROPE_DIM, MLA_NOPE_DIM)
    cb = jnp.where(lane < MLA_NOPE_DIM, 1.0, ck)
    return jnp.concatenate([ca, sa1, sa2, cb, sb1, sb2, ck], axis=1)


def _pad_heads(w, heads, dim, axis):
    shp = w.shape
    w = w.reshape(shp[:axis] + (heads, dim) + shp[axis + 1:])
    pad = [(0, 0)] * w.ndim
    pad[axis + 1] = (0, LANE - dim)
    return jnp.pad(w, pad).reshape(shp[:axis] + (heads * LANE,) + shp[axis + 1:])


def _unpad_heads(w, heads, dim, axis):
    shp = w.shape
    w = w.reshape(shp[:axis] + (heads, LANE) + shp[axis + 1:])
    w = lax.slice_in_dim(w, 0, dim, axis=axis + 1)
    return w.reshape(shp[:axis] + (heads * dim,) + shp[axis + 1:])


def _pad_layer(w_in, w_q_up, w_kv_up):
    o1 = SWA_Q_W
    o2 = o1 + SWA_KV_W
    o3 = o2 + SWA_KV_W
    o4 = o3 + MLA_Q_RANK
    o5 = o4 + MLA_KV_RANK
    kr = jnp.pad(w_in[o5:], ((MLA_NOPE_DIM, LANE - MLA_QK_DIM), (0, 0)))
    win = jnp.concatenate([
        _pad_heads(w_in[:o1], SWA_HEADS, SWA_HEAD_DIM, 0),
        _pad_heads(w_in[o1:o2], SWA_KV_HEADS, SWA_HEAD_DIM, 0),
        _pad_heads(w_in[o2:o3], SWA_KV_HEADS, SWA_HEAD_DIM, 0),
        w_in[o3:o5], kr], axis=0)
    wqu = _pad_heads(w_q_up, MLA_HEADS, MLA_QK_DIM, 0)
    kv = w_kv_up.reshape(MLA_HEADS, MLA_NOPE_DIM + MLA_V_DIM, MLA_KV_RANK)
    wkv = jnp.concatenate([
        _pad_heads(kv[:, :MLA_NOPE_DIM].reshape(-1, MLA_KV_RANK), MLA_HEADS, MLA_NOPE_DIM, 0),
        _pad_heads(kv[:, MLA_NOPE_DIM:].reshape(-1, MLA_KV_RANK), MLA_HEADS, MLA_V_DIM, 0)], axis=0)
    return win, wqu, wkv


IN_KEEP_SWA = [(LANE * hd, SWA_HEAD_DIM) for hd in range(SWA_HEADS)]
IN_KEEP_REST = ([(LANE * hd, SWA_HEAD_DIM) for hd in range(2 * SWA_KV_HEADS)] + [(PO_CQ - PO_KA, PO_KR - PO_CQ)]
                + [(PO_KR - PO_KA + MLA_NOPE_DIM, MLA_ROPE_DIM)])


def _unpad_layer(d_w_in, dwqu, dwkv):
    d_w_q_up = _unpad_heads(dwqu, MLA_HEADS, MLA_QK_DIM, 0)
    dk = _unpad_heads(dwkv[:, :HP], MLA_HEADS, MLA_NOPE_DIM, 1).reshape(MLA_KV_RANK, MLA_HEADS, MLA_NOPE_DIM)
    dv = _unpad_heads(dwkv[:, HP:], MLA_HEADS, MLA_V_DIM, 1).reshape(MLA_KV_RANK, MLA_HEADS, MLA_V_DIM)
    d_w_kv_up = jnp.concatenate([dk, dv], axis=2).reshape(MLA_KV_RANK, -1).T
    return d_w_in, d_w_q_up, d_w_kv_up


def _train_example(x, target, meta, vec, weights):
    s = x.shape[0]
    depth = vec["attn_norm"].shape[0]
    t = FRONT + N_META + s
    assert t % BLOCK == 0
    tabs = _rope_tables(t)
    h = jnp.concatenate([jnp.zeros((FRONT, D_MODEL), F32), meta, x], axis=0)
    row = lambda v: v[None, :]

    saved = []
    for l in range(depth):
        win, wqu, wkv = _pad_layer(*weights.attn_in(l))
        g1, gq, gkv, g2, ga, gb = (row(vec[n][l]) for n in ("attn_norm", "q_norm", "kv_norm", "ffn_norm",
                                                            "out_norm_swa", "out_norm_mla"))
        sk = row(vec["sinks"][l])
        u, qa, ka, va, cq, ckv, qn, kvn, qb, kf, vb = _pre_fwd(h, g1, win, gq, wqu, gkv, wkv, tabs)
        oa = _swa_fwd(sk, qa, ka, va)
        ob, lse = weights.mla_fwd(l, qb, kf, vb)
        lse = jnp.moveaxis(lse.reshape(t, MLA_HEADS // MLA_HB, MLA_HB), 1, 0)
        wo = weights.w_o(l)
        h2, mix, u2 = _mix_fwd(h, oa, ob, ga, gb, wo, g2)
        wg, wu, wd = weights.ffn(l)
        h3, gt, up = weights.ffn_fwd(l, h2, u2)
        saved.append((h, u, qa, ka, va, cq, ckv, qn, kvn, qb, kf, vb, oa, ob, lse, h2, mix, u2, gt, up,
                      win, wqu, wkv, wo, ga, gb, g1, gq, gkv, g2, sk, wg, wu, wd))
        h = h3

    dh, d_final, loss = _loss_bwd(h, row(vec["final_norm"]), target)

    grads = []
    for l in reversed(range(depth)):
        (h0, u, qa, ka, va, cq, ckv, qn, kvn, qb, kf, vb, oa, ob, lse, h2, mix, u2, gt, up,
         win, wqu, wkv, wo, ga, gb, g1, gq, gkv, g2, sk, wg, wu, wd) = saved[l]
        dff = wd.shape[0]
        act, dgu, dhb = _ffn_bwd_a(dh, gt, up, wd)
        weights.ffn_grads(l, _tn_matmul(dgu, u2, "dw_gate", (0, dff)), _tn_matmul(dgu, u2, "dw_up", (dff, dff)),
                          _tn_matmul(act, dhb, "dw_down"))
        dh2, dh2b, d_g2 = _ffn_bwd_b(dh, dgu, h2, g2, wg, wu)
        weights.attn_grads(l, w_o=_tn_matmul(mix, dh2b, "dw_o"))
        doa, dob, dl, d_ga, d_gb = _mix_bwd(dh2b, oa, ob, ga, gb, wo)
        dqa, dkc, dkp, dvc, dvp, dsink = _swa_bwd(sk, qa, ka, va, oa, doa)
        dqb, dkf, dvb = weights.mla_bwd(l, qb, kf, vb, dob, lse, dl)
        dh, dp, dqbo, dkvo, d_g1, d_gq, d_gkv = _pre_bwd(
            dh2, h0, cq, ckv, dqa, dkc, dkp, dvc, dvp, dqb, dkf, dvb,
            g1, win, gq, wqu, gkv, wkv, tabs)
        d_win = jnp.concatenate([_tn_matmul(dp, u, "dw_in_swa", (PO_QA, PO_KA), IN_KEEP_SWA),
                                 _tn_matmul(dp, u, "dw_in_rest", (PO_KA, PW_IN - PO_KA), IN_KEEP_REST)], axis=0)
        d_wqu = _tn_matmul(dqbo, qn, "dw_q_up")
        d_wkv = _tn_matmul(kvn, dkvo, "dw_kv_up")
        weights.attn_grads(l, **dict(zip(ATTN_IN, _unpad_layer(d_win, d_wqu, d_wkv))))
        grads.append(dict(attn_norm=d_g1[0], q_norm=d_gq[0], kv_norm=d_gkv[0], sinks=dsink[:, 0], out_norm_swa=d_ga[0],
                          out_norm_mla=d_gb[0], ffn_norm=d_g2[0]))
    grads = grads[::-1]
    stacked = {k: jnp.stack([g[k] for g in grads]) for k in grads[0]}
    stacked["final_norm"] = d_final[0]
    return loss[0, 0], dh[FRONT + N_META:], dh[FRONT:FRONT + N_META], stacked


MESH = pl.DeviceIdType.MESH
ANY = pl.BlockSpec(memory_space=pl.ANY)


def _place():
    return lax.axis_index("x"), lax.axis_index("y"), lax.axis_index("c")


def _index(x, y, c):
    return 4 * x + 2 * y + c


def _comm_sems(n):
    return [pltpu.SemaphoreType.DMA((n, N_DEV - 1)), pltpu.SemaphoreType.DMA((n, N_DEV - 1)),
            pltpu.SemaphoreType.DMA((n,))]


class _gather_plan:
    def __init__(self, x_refs, out_refs, send_sems, recv_sems, local_sems):
        self.x_refs, self.out_refs = x_refs, out_refs
        self.send_sems, self.recv_sems, self.local_sems = send_sems, recv_sems, local_sems
        self.n = len(x_refs)

    def _where(self):
        x, y, c = _place()
        return (x, y, c), (x, y, 1 - c), [(1 - x, y), (x, 1 - y), (1 - x, 1 - y)], c

    def _copy(self, i, k, block, to, from_input=False):
        slot = self.out_refs[i].at[_index(*block)]
        return pltpu.make_async_remote_copy(
            src_ref=self.x_refs[i] if from_input else slot, dst_ref=slot,
            send_sem=self.send_sems.at[i, k], recv_sem=self.recv_sems.at[i, k], device_id=to, device_id_type=MESH)

    def _mine(self, i, me):
        return pltpu.make_async_copy(self.x_refs[i], self.out_refs[i].at[_index(*me)], self.local_sems.at[i])

    def _first(self, me, sibling, chips, c):
        out = [self._copy(i, 1 + j, me, (*chip, c), True) for j, chip in enumerate(chips) for i in range(self.n)]
        return out + [self._copy(i, 0, me, sibling, True) for i in range(self.n)]

    def start(self):
        me, sibling, chips, c = self._where()
        for i in range(self.n):
            self._mine(i, me).start()
        for cp in self._first(me, sibling, chips, c):
            cp.start()

    def forward(self):
        me, sibling, chips, c = self._where()
        for j, chip in enumerate(chips):
            for i in range(self.n):
                self._copy(i, 1 + j, (*chip, c), me).wait_recv()
                self._copy(i, 4 + j, (*chip, c), sibling).start()

    def finish(self):
        me, sibling, chips, c = self._where()
        for i in range(self.n):
            self._copy(i, 0, sibling, me).wait_recv()
            for j, chip in enumerate(chips):
                self._copy(i, 4 + j, (*chip, 1 - c), me).wait_recv()
        for cp in self._first(me, sibling, chips, c):
            cp.wait_send()
        for j, chip in enumerate(chips):
            for i in range(self.n):
                self._copy(i, 4 + j, (*chip, c), sibling).wait_send()
        for i in range(self.n):
            self._mine(i, me).wait()


class _exchange_plan:
    def __init__(self, in_refs, out_refs, send_sems, recv_sems, local_sems):
        self.in_refs, self.out_refs = in_refs, out_refs
        self.send_sems, self.recv_sems, self.local_sems = send_sems, recv_sems, local_sems
        self.n = len(in_refs)

    def _copies(self):
        x, y, c = _place()
        me = _index(x, y, c)
        mine = [pltpu.make_async_copy(self.in_refs[i].at[me], self.out_refs[i].at[me], self.local_sems.at[i])
                for i in range(self.n)]
        remote = []
        for k in range(1, N_DEV):
            peer = (1 - x if k & 4 else x, 1 - y if k & 2 else y, 1 - c if k & 1 else c)
            remote += [pltpu.make_async_remote_copy(
                src_ref=self.in_refs[i].at[_index(*peer)], dst_ref=self.out_refs[i].at[me],
                send_sem=self.send_sems.at[i, k - 1], recv_sem=self.recv_sems.at[i, k - 1],
                device_id=peer, device_id_type=MESH) for i in range(self.n)]
        return mine, remote

    def start(self):
        mine, remote = self._copies()
        for cp in mine + remote:
            cp.start()

    def finish(self):
        mine, remote = self._copies()
        for cp in remote:
            cp.wait_recv()
        for cp in remote:
            cp.wait_send()
        for cp in mine:
            cp.wait()


def _all_gather(shards, name):
    n = len(shards)

    def body(*refs):
        plan = _gather_plan(refs[:n], refs[n:2 * n], *refs[2 * n:])
        plan.start()
        plan.forward()
        plan.finish()

    return pl.pallas_call(
        body, name=name, in_specs=[ANY] * n, out_specs=[ANY] * n, scratch_shapes=_comm_sems(n),
        out_shape=[jax.ShapeDtypeStruct((N_DEV,) + a.shape, a.dtype) for a in shards],
    )(*shards)


def _adamw(w, g, m, v):
    m = ADAM_B1 * m + (1.0 - ADAM_B1) * g
    v = ADAM_B2 * v + (1.0 - ADAM_B2) * (g * g)
    m_hat = m / (1.0 - ADAM_B1 ** ADAM_STEP)
    v_hat = v / (1.0 - ADAM_B2 ** ADAM_STEP)
    return -ADAM_LR * (m_hat / (jnp.sqrt(v_hat) + ADAM_EPS) + ADAM_WD * w), m, v


def _sum_slots(ref):
    g = ref[0].astype(F32)
    for s in range(1, N_DEV):
        g = g + ref[s].astype(F32)
    return g


def _reduce_adamw(parts, w, m, v, name, slabs=()):
    l, r, c = w.shape
    tile = max([d for d in range(16, ADAM_ROWS + 1, 16) if r % d == 0], default=r)
    last = r // tile - 1
    n = len(slabs)

    def body(*refs):
        p_refs, (w_ref, m_ref, v_ref), in_refs = refs[:l], refs[l:l + 3], refs[l + 3:l + 3 + n]
        (g_ref, d_ref, nm_ref, nv_ref), out_refs, sems = refs[l + 3 + n:l + 7 + n], refs[l + 7 + n:l + 7 + 2 * n], refs[l + 7 + 2 * n:]
        step_id = pl.program_id(0) * (last + 1) + pl.program_id(1)
        if n:
            plan = _exchange_plan(in_refs, out_refs, *sems)
            pl.when(step_id == 0)(plan.start)
        for layer in range(l):
            @pl.when(pl.program_id(0) == layer)
            def _(p_ref=p_refs[layer]):
                g = _sum_slots(p_ref)
                g_ref[...] = g
                d_ref[...], nm_ref[...], nv_ref[...] = _adamw(w_ref[...], g, m_ref[...], v_ref[...])
        if n:
            pl.when(step_id == l * (last + 1) - 1)(plan.finish)

    def part_spec(layer):
        return pl.BlockSpec((N_DEV, tile, c),
                            lambda i, j: (0, jnp.where(i == layer, j, jnp.where(i < layer, 0, last)), 0))

    blk = pl.BlockSpec((None, tile, c), lambda i, j: (i, j, 0))
    out = pl.pallas_call(
        body, name=name, grid=(l, r // tile),
        in_specs=[part_spec(layer) for layer in range(l)] + [blk, blk, blk] + [ANY] * n, out_specs=[blk] * 4 + [ANY] * n,
        out_shape=[jax.ShapeDtypeStruct((l, r, c), F32)] * 4 + [jax.ShapeDtypeStruct(a.shape, a.dtype) for a in slabs],
        scratch_shapes=_comm_sems(n) if n else [],
        compiler_params=_params("arbitrary", "arbitrary"),
    )(*parts, w, m, v, *slabs)
    return out[:4], out[4:]


def _sum_parts(parts, name):
    _, r, c = parts.shape

    def body(p_ref, g_ref):
        g_ref[...] = _sum_slots(p_ref)

    return pl.pallas_call(body, name=name, out_shape=jax.ShapeDtypeStruct((r, c), F32))(parts)


def _adamw_call(w, g, m, v, name):
    def body(w_ref, g_ref, m_ref, v_ref, d_ref, nm_ref, nv_ref):
        d_ref[...], nm_ref[...], nv_ref[...] = _adamw(w_ref[...], g_ref[...], m_ref[...], v_ref[...])

    return pl.pallas_call(body, name=name, out_shape=[jax.ShapeDtypeStruct(w.shape, F32)] * 3)(w, g, m, v)


ATTN_IN = ("w_in", "w_q_up", "w_kv_up")
ATTN = ATTN_IN + ("w_o",)
FFN = ("w_gate", "w_up", "w_down")
TRANSPOSED = ("w_in", "w_q_up", "w_kv_up", "w_gate", "w_up")
SMALL = ("attn_norm", "ffn_norm", "final_norm", "out_norm_swa", "out_norm_mla", "q_norm", "kv_norm", "sinks")
PACK_W = 1024
SMALL_ROWS = 16


def _pack(arrs, dtype):
    flat = jnp.concatenate([a.astype(dtype).reshape(-1) for a in arrs])
    return flat.reshape(-1, PACK_W)


def _unpack(packed, like):
    flat = packed.reshape(-1)
    out, off = [], 0
    for a in like:
        out.append(flat[off:off + a.size].reshape(a.shape))
        off += a.size
    return out


def _gather_to_full(gathered):
    return gathered.reshape((-1,) + gathered.shape[2:])


def _full_to_slabs(full):
    return full.reshape((N_DEV, -1) + full.shape[1:])


class _ShardedWeights:
    def __init__(self, shards, depth, meta_shard):
        self.shards, self.depth = shards, depth
        self.gathered, self.pending, self.parts = {}, {}, {}
        first = _all_gather([shards[n][0] for n in ATTN_IN] + [meta_shard], "gather_attn0")
        self.gathered.update(zip([(n, 0) for n in ATTN_IN], first))
        self.meta = jnp.moveaxis(first[-1], 0, 1).reshape(N_META, D_MODEL)

    def _gather(self, keys, run):
        self.gathered.update(zip(keys, run([self.shards[n][l] for n, l in keys])))

    def _full(self, names, l):
        return tuple(_gather_to_full(self.gathered[n, l]) for n in names)

    def attn_in(self, l):
        return self._full(ATTN_IN, l)

    def w_o(self, l):
        return self._full(("w_o",), l)[0]

    def ffn(self, l):
        return self._full(FFN, l)

    def mla_fwd(self, l, q, k, v):
        out = []
        self._gather([(n, l) for n in ("w_o",) + FFN], lambda xs: out.extend(_mla_fwd(q, k, v, xs)) or out[2])
        return out[0], out[1]

    def ffn_fwd(self, l, h2, u2):
        keys = [(n, l + 1) for n in ATTN_IN] if l + 1 < self.depth else []
        out = []
        self._gather(keys, lambda xs: out.extend(_ffn_fwd(h2, u2, *self.ffn(l), xs)) or out[1])
        return out[0]

    def _add(self, names, l, grads):
        for n, g in zip(names, grads):
            self.pending[n, l] = _full_to_slabs(g)

    def ffn_grads(self, l, *grads):
        self._add(FFN, l, grads)

    def attn_grads(self, l, **grads):
        self._add(list(grads), l, grads.values())

    def _exchange(self, run):
        keys = list(self.pending)
        self.parts.update(zip(keys, run([self.pending.pop(k) for k in keys])))

    def mla_bwd(self, l, *args):
        out = []
        self._exchange(lambda xs: out.extend(_mla_bwd(*args, xs)) or out[1])
        return out[0]

    def flush(self, run):
        self._exchange(run)


def kernel(x, meta_tokens, attn_norm, w_in, q_norm, w_q_up, kv_norm, w_kv_up, sinks, out_norm_swa, out_norm_mla, w_o, ffn_norm, w_gate, w_up, w_down, final_norm, loss_target, m_meta_tokens, m_attn_norm, m_w_in, m_q_norm, m_w_q_up, m_kv_norm, m_w_kv_up, m_sinks, m_out_norm_swa, m_out_norm_mla, m_w_o, m_ffn_norm, m_w_gate, m_w_up, m_w_down, m_final_norm, v_meta_tokens, v_attn_norm, v_w_in, v_q_norm, v_w_q_up, v_kv_norm, v_w_kv_up, v_sinks, v_out_norm_swa, v_out_norm_mla, v_w_o, v_ffn_norm, v_w_gate, v_w_up, v_w_down, v_final_norm):
    w = dict(meta_tokens=meta_tokens, attn_norm=attn_norm, w_in=w_in, q_norm=q_norm, w_q_up=w_q_up, kv_norm=kv_norm,
             w_kv_up=w_kv_up, sinks=sinks, out_norm_swa=out_norm_swa, out_norm_mla=out_norm_mla, w_o=w_o,
             ffn_norm=ffn_norm, w_gate=w_gate, w_up=w_up, w_down=w_down, final_norm=final_norm)
    m = dict(meta_tokens=m_meta_tokens, attn_norm=m_attn_norm, w_in=m_w_in, q_norm=m_q_norm, w_q_up=m_w_q_up,
             kv_norm=m_kv_norm, w_kv_up=m_w_kv_up, sinks=m_sinks, out_norm_swa=m_out_norm_swa,
             out_norm_mla=m_out_norm_mla, w_o=m_w_o, ffn_norm=m_ffn_norm, w_gate=m_w_gate, w_up=m_w_up,
             w_down=m_w_down, final_norm=m_final_norm)
    v = dict(meta_tokens=v_meta_tokens, attn_norm=v_attn_norm, w_in=v_w_in, q_norm=v_q_norm, w_q_up=v_w_q_up,
             kv_norm=v_kv_norm, w_kv_up=v_w_kv_up, sinks=v_sinks, out_norm_swa=v_out_norm_swa,
             out_norm_mla=v_out_norm_mla, w_o=v_w_o, ffn_norm=v_ffn_norm, w_gate=v_w_gate, w_up=v_w_up,
             w_down=v_w_down, final_norm=v_final_norm)
    names = list(w)
    big = ATTN + FFN
    depth = w_in.shape[0]
    me = _index(*_place())

    as_held = lambda n, a: jnp.swapaxes(a, 1, 2) if n in TRANSPOSED else a
    weights = _ShardedWeights({n: as_held(n, w[n]).astype(BF16) for n in big}, depth, meta_tokens)
    loss, grad_x, d_meta, grads = _train_example(x[0], loss_target[0], weights.meta, {n: w[n] for n in SMALL}, weights)

    g_big, d_big, m_big, v_big = {}, {}, {}, {}
    for n in FFN + ATTN[::-1]:
        held = [as_held(n, a) for a in (w[n], m[n], v[n])]
        reduce = functools.partial(_reduce_adamw, [weights.parts[n, l] for l in range(depth)], *held, "reduce_adamw_" + n)
        if n == FFN[0]:
            outs = []
            weights.flush(lambda xs: outs.extend(reduce(xs)) or outs[1])
        else:
            outs = reduce()
        g_big[n], d_big[n], m_big[n], v_big[n] = [as_held(n, a) for a in outs[0]]

    small = [grads[n] for n in SMALL] + [loss.reshape(1)]
    pad = SMALL_ROWS * PACK_W - sum(a.size for a in small)
    part = jnp.concatenate([_pack(small + [jnp.zeros((pad,), F32)], F32), d_meta], axis=0)
    total = _sum_parts(_all_gather([part], "gather_small")[0], "sum_small")
    small_w = [w[n] for n in SMALL]
    packs = [_pack([d[n] for n in SMALL] + [jnp.zeros((pad + 1,), F32)], F32) for d in (w, m, v)]
    upd = _adamw_call(packs[0], total[:SMALL_ROWS], packs[1], packs[2], "adamw_small")
    g_small, d_small, m_small, v_small = [dict(zip(SMALL, _unpack(p, small_w))) for p in (total[:SMALL_ROWS],) + tuple(upd)]
    loss_total = total[:SMALL_ROWS].reshape(-1)[SMALL_ROWS * PACK_W - pad - 1]
    g_meta = lax.dynamic_slice_in_dim(total[SMALL_ROWS:], me * LANE, LANE, axis=1)
    d_mt, m_mt, v_mt = _adamw_call(meta_tokens, g_meta, m_meta_tokens, v_meta_tokens, "adamw_meta")

    outs = []
    for got in ({**g_big, **g_small, "meta_tokens": g_meta}, {**d_big, **d_small, "meta_tokens": d_mt},
                {**m_big, **m_small, "meta_tokens": m_mt}, {**v_big, **v_small, "meta_tokens": v_mt}):
        outs += [got[n] for n in names]
    return (loss_total, grad_x[None], *outs)
```

```python
import functools

import jax
import jax.numpy as jnp
from jax import lax
from jax.experimental import pallas as pl
from jax.experimental.pallas import tpu as pltpu

F32 = jnp.float32
BF16 = jnp.bfloat16

D_MODEL = 1024
N_META = 16
BLOCK = 128
FRONT = (-N_META) % BLOCK
ROPE_THETA = 10000.0
EPS = 1e-6
NEG = -1e30
SWA_HEADS = 8
SWA_KV_HEADS = 2
SWA_GROUP = SWA_HEADS // SWA_KV_HEADS
SWA_HEAD_DIM = 64
MLA_HEADS = 8
MLA_Q_RANK = 256
MLA_KV_RANK = 128
MLA_NOPE_DIM = 64
MLA_ROPE_DIM = 32
MLA_V_DIM = 64
MLA_QK_DIM = MLA_NOPE_DIM + MLA_ROPE_DIM
SWA_Q_W = SWA_HEADS * SWA_HEAD_DIM
SWA_KV_W = SWA_KV_HEADS * SWA_HEAD_DIM
MLA_OUT_W = MLA_HEADS * MLA_V_DIM
SCALE_A = SWA_HEAD_DIM ** -0.5
SCALE_B = MLA_QK_DIM ** -0.5
LOG2E = 1.4426950408889634
Q_SCALE = SCALE_B * LOG2E
ADAM_LR = 0.001
ADAM_B1 = 0.9
ADAM_B2 = 0.999
ADAM_EPS = 1e-08
ADAM_WD = 0.01
ADAM_STEP = 10

LANE = 128
N_DEV = 8
HP = 8 * LANE
PO_QA, PO_KA, PO_VA = 0, HP, HP + 2 * LANE
PO_CQ = PO_VA + 2 * LANE
PO_CKV = PO_CQ + MLA_Q_RANK
PO_KR = PO_CKV + MLA_KV_RANK
PW_IN = PO_KR + LANE
N_TAB = 7
VMEM_LIMIT = 56 * 2 ** 20
TN_VMEM_BUDGET = 36 * 2 ** 20
MLA_HB = 4
MLA_HB_FWD = 8
ADAM_ROWS = 256
HALF = LANE // 2
assert SWA_HEAD_DIM == HALF and MLA_V_DIM == HALF

NT = (((1,), (1,)), ((), ()))
TN = (((0,), (0,)), ((), ()))


def _tile(t):
    return 384 if t % 384 == 0 else 128


def _params(*sem):
    return pltpu.CompilerParams(dimension_semantics=sem, vmem_limit_bytes=VMEM_LIMIT)


def _row(tm, n):
    return pl.BlockSpec((tm, n), lambda i: (i, 0))


def _const(shape):
    return pl.BlockSpec(shape, lambda i: (0,) * len(shape))


def _dot(a, b):
    return jnp.dot(a, b, preferred_element_type=F32)


def _dot_nt(a, b):
    return lax.dot_general(a, b, NT, preferred_element_type=F32)


def _dot_tn(a, b):
    return lax.dot_general(a, b, TN, preferred_element_type=F32)


def _rope(x, c, s1, s2, shift):
    return x * c + pltpu.roll(x, LANE - shift, 1) * s1 + pltpu.roll(x, shift, 1) * s2


def _rope_t(dy, c, s1, s2, shift):
    return dy * c + pltpu.roll(dy * s1, shift, 1) + pltpu.roll(dy * s2, LANE - shift, 1)


def _rms_r(x, n):
    return lax.rsqrt(jnp.sum(x * x, axis=-1, keepdims=True) * (1.0 / n) + EPS)


def _rms_bwd(x, g, dy, n):
    r = _rms_r(x, n)
    xh = x * r
    dxh = dy * g
    dx = r * (dxh - xh * (jnp.sum(dxh * xh, axis=-1, keepdims=True) * (1.0 / n)))
    return dx, jnp.sum(dy * xh, axis=0, keepdims=True)


def _acc(ref, val, first):
    @pl.when(first)
    def _():
        ref[...] = val

    @pl.when(jnp.logical_not(first))
    def _():
        ref[...] += val


def _pair_half(slab, half):
    return slab if half == 0 else pltpu.roll(slab, HALF, 1)


def _unpack_pair(slab, half):
    x = _pair_half(slab, half)
    return jnp.where(lax.broadcasted_iota(jnp.int32, x.shape, 1) < HALF, x, 0.0)


def _tabs(tab_ref):
    return [tab_ref[:, LANE * i:LANE * (i + 1)] for i in range(N_TAB)]


def _pre_fwd(h, g1, win, gq, wqu, gkv, wkv, tabs):
    t = h.shape[0]
    tm = _tile(t)

    def body(h_ref, g1_ref, win_ref, gq_ref, wqu_ref, gkv_ref, wkv_ref, tab_ref,
             u_ref, qa_ref, ka_ref, va_ref, cq_ref, ckv_ref, qn_ref, kvn_ref, qb_ref, kf_ref, vb_ref):
        ca, sa1, sa2, cb, sb1, sb2, ck = _tabs(tab_ref)
        hv = h_ref[...]
        u = (hv * _rms_r(hv, D_MODEL) * g1_ref[...]).astype(BF16)
        u_ref[...] = u
        p = _dot_nt(u, win_ref[...])
        for c in range(SWA_HEADS):
            sl = slice(LANE * c, LANE * (c + 1))
            qa_ref[:, sl] = _rope(p[:, PO_QA + LANE * c:PO_QA + LANE * (c + 1)], ca, sa1, sa2, 32).astype(BF16)
        for c in range(SWA_KV_HEADS):
            sl = slice(LANE * c, LANE * (c + 1))
            ka_ref[:, sl] = _rope(p[:, PO_KA + LANE * c:PO_KA + LANE * (c + 1)], ca, sa1, sa2, 32).astype(BF16)
        va_ref[...] = p[:, PO_VA:PO_CQ].astype(BF16)
        cq = p[:, PO_CQ:PO_CKV]
        ckv = p[:, PO_CKV:PO_KR]
        cq_ref[...] = cq
        ckv_ref[...] = ckv
        qn = (cq * _rms_r(cq, MLA_Q_RANK) * gq_ref[...]).astype(BF16)
        qn_ref[...] = qn
        qb = _dot_nt(qn, wqu_ref[...])
        kvn = (ckv * _rms_r(ckv, MLA_KV_RANK) * gkv_ref[...]).astype(BF16)
        kvn_ref[...] = kvn
        kv = _dot_nt(kvn, wkv_ref[...])
        kr = _rope(p[:, PO_KR:PW_IN], ck, sb1, sb2, 16)
        for c in range(MLA_HEADS):
            sl = slice(LANE * c, LANE * (c + 1))
            qb_ref[:, sl] = (_rope(qb[:, sl], cb, sb1, sb2, 16) * Q_SCALE).astype(BF16)
            kf_ref[:, sl] = (kv[:, sl] + kr).astype(BF16)
        vb_ref[...] = kv[:, HP:].astype(BF16)

    widths = [(D_MODEL, BF16), (HP, BF16), (2 * LANE, BF16), (2 * LANE, BF16), (MLA_Q_RANK, F32),
              (MLA_KV_RANK, F32), (MLA_Q_RANK, BF16), (MLA_KV_RANK, BF16), (HP, BF16), (HP, BF16), (HP, BF16)]
    return pl.pallas_call(
        body, name="pre_fwd", grid=(t // tm,),
        in_specs=[_row(tm, D_MODEL), _const(g1.shape), _const(win.shape), _const(gq.shape), _const(wqu.shape),
                  _const(gkv.shape), _const(wkv.shape), _row(tm, N_TAB * LANE)],
        out_specs=[_row(tm, w) for w, _ in widths],
        out_shape=[jax.ShapeDtypeStruct((t, w), d) for w, d in widths],
        compiler_params=_params("parallel"),
    )(h, g1, win, gq, wqu, gkv, wkv, tabs)


def _swa_mask(nb):
    key = lax.broadcasted_iota(jnp.int32, (2 * BLOCK, SWA_GROUP * BLOCK), 0)
    qry = lax.broadcasted_iota(jnp.int32, (2 * BLOCK, SWA_GROUP * BLOCK), 1) & (BLOCK - 1)
    return (key > qry) & (key <= qry + BLOCK) & (key + (nb - 1) * BLOCK >= FRONT)


def _swa_group(ref, rows, j):
    return jnp.concatenate([ref[rows, LANE * (SWA_GROUP * j + g):LANE * (SWA_GROUP * j + g + 1)]
                            for g in range(SWA_GROUP)], axis=0)


def _swa_packed_group(ref, rows, j):
    heads = [SWA_GROUP * j + g for g in range(SWA_GROUP)]
    return jnp.concatenate([_pair_half(ref[rows, LANE * (hd // 2):LANE * (hd // 2 + 1)], hd % 2) for hd in heads], axis=0)


def _swa_sinks(sink_ref, j):
    return jnp.concatenate([jnp.full((1, BLOCK), sink_ref[0, SWA_GROUP * j + g], F32) for g in range(SWA_GROUP)], axis=1)


def _swa_keys(prev_ref, cur_ref, rb, j):
    sl = slice(LANE * j, LANE * (j + 1))
    if rb == 0:
        return jnp.concatenate([prev_ref[:, sl], cur_ref[:BLOCK, sl]], axis=0)
    return cur_ref[BLOCK * (rb - 1):BLOCK * (rb + 1), sl]


def _swa_chains(t):
    return [(rb, j) for rb in range(_tile(t) // BLOCK) for j in range(SWA_KV_HEADS)]


def _swa_scores(sink_ref, q_ref, kp_ref, kc_ref, n, t):
    r = _tile(t) // BLOCK
    chains = _swa_chains(t)
    qs = [_swa_group(q_ref, slice(BLOCK * rb, BLOCK * (rb + 1)), j) for rb, j in chains]
    ks = [_swa_keys(kp_ref, kc_ref, rb, j) for rb, j in chains]
    ss = [_dot_nt(k2, q4) for q4, k2 in zip(qs, ks)]
    masks = [_swa_mask(n * r + rb) for rb in range(r)]
    out = []
    for (rb, j), s in zip(chains, ss):
        sink = _swa_sinks(sink_ref, j)
        s = jnp.where(masks[rb], s * SCALE_A, NEG)
        m = jnp.maximum(jnp.max(s, axis=0, keepdims=True), sink)
        e = jnp.exp(s - m)
        es = jnp.exp(sink - m)
        inv = 1.0 / (jnp.sum(e, axis=0, keepdims=True) + es)
        out.append((e * inv, es * inv))
    return qs, ks, out


def _swa_specs(t):
    ts = _tile(t)
    r = ts // BLOCK
    prev = lambda n: (jnp.maximum(n * r - 1, 0), 0)
    cur = lambda n: (n, 0)
    return [pl.BlockSpec(memory_space=pltpu.SMEM), pl.BlockSpec((ts, HP), cur),
            pl.BlockSpec((BLOCK, 2 * LANE), prev), pl.BlockSpec((ts, 2 * LANE), cur),
            pl.BlockSpec((BLOCK, 2 * LANE), prev), pl.BlockSpec((ts, 2 * LANE), cur)]


def _swa_fwd(sinks, q, k, v):
    t = q.shape[0]
    ts = _tile(t)

    def body(sink_ref, q_ref, kp_ref, kc_ref, vp_ref, vc_ref, o_ref):
        chains = _swa_chains(t)
        _, _, probs = _swa_scores(sink_ref, q_ref, kp_ref, kc_ref, pl.program_id(0), t)
        os_ = [_dot_tn(_swa_keys(vp_ref, vc_ref, rb, j)[:, :HALF], p.astype(BF16)) for (rb, j), (p, _) in zip(chains, probs)]
        for (rb, j), o4 in zip(chains, os_):
            for g in range(0, SWA_GROUP, 2):
                pair = (SWA_GROUP * j + g) // 2
                o_ref[BLOCK * rb:BLOCK * (rb + 1), LANE * pair:LANE * (pair + 1)] = jnp.concatenate(
                    [o4[:, BLOCK * g:BLOCK * (g + 1)], o4[:, BLOCK * (g + 1):BLOCK * (g + 2)]], axis=0).T

    return pl.pallas_call(
        body, name="swa_fwd", grid=(t // ts,),
        in_specs=_swa_specs(t),
        out_specs=pl.BlockSpec((ts, SWA_Q_W), lambda n: (n, 0)),
        out_shape=jax.ShapeDtypeStruct((t, SWA_Q_W), F32),
        compiler_params=_params("parallel"),
    )(sinks, q, k, k, v, v)


def _causal_mask(q0, k0, tq, tk, transposed):
    if transposed:
        key = k0 + lax.broadcasted_iota(jnp.int32, (tk, tq), 0)
        qry = q0 + lax.broadcasted_iota(jnp.int32, (tk, tq), 1)
    else:
        qry = q0 + lax.broadcasted_iota(jnp.int32, (tq, tk), 0)
        key = k0 + lax.broadcasted_iota(jnp.int32, (tq, tk), 1)
    return (key <= qry) & (key >= FRONT)


def _heads(ref, hb, rows=slice(None)):
    return [ref[rows, LANE * a:LANE * (a + 1)] for a in range(hb)]


def _head_stats(t, hb=MLA_HB):
    return jax.ShapeDtypeStruct((MLA_HEADS // hb, t, hb), F32)


def _mla_fwd(q, k, v, shards=()):
    t = q.shape[0]
    tq = _tile(t)
    nq = t // tq
    n = len(shards)
    hb = MLA_HB_FWD
    steps = (MLA_HEADS // hb) * nq

    def body(q_ref, k_ref, v_ref, *rest):
        x_refs, (o_ref, lse_ref), out_refs = rest[:n], rest[n:n + 2], rest[n + 2:2 * n + 2]
        acc_sc, sems = rest[2 * n + 2], rest[2 * n + 3:]
        i = pl.program_id(1)
        step_id = pl.program_id(0) * nq + i
        if n:
            plan = _gather_plan(x_refs, out_refs, *sems)
            pl.when(step_id == 0)(plan.start)
            pl.when(step_id == (3 * steps) // 4)(plan.forward)
        qs = _heads(q_ref, hb)
        acc_sc[...] = jnp.zeros(acc_sc.shape, F32)

        def step(j, carry, masked):
            rows = pl.ds(pl.multiple_of(j * tq, tq), tq)
            ks = _heads(k_ref, hb, rows)
            vs = [v_ref[rows, LANE * a:LANE * a + HALF] for a in range(hb)]
            ss = [_dot_nt(kh, qh) for qh, kh in zip(qs, ks)]
            if masked:
                mask = _causal_mask(i * tq, j * tq, tq, tq, True)
                ss = [jnp.where(mask, s, NEG) for s in ss]
            mid, out = [], []
            for s, (m, l) in zip(ss, carry):
                mn = jnp.maximum(m, jnp.max(s, axis=0, keepdims=True))
                al = jnp.exp2(m - mn)
                p = jnp.exp2(s - mn)
                out.append((mn, al * l + jnp.sum(p, axis=0, keepdims=True)))
                mid.append((al, p.astype(BF16)))
            for a, ((al, p), vh) in enumerate(zip(mid, vs)):
                acc_sc[a] = al * acc_sc[a] + _dot_tn(vh, p)
            return tuple(out)

        init = ((jnp.full((1, tq), NEG, F32), jnp.zeros((1, tq), F32)),) * hb
        carry = lax.fori_loop(0, jnp.minimum(i, 1) + 1, lambda it, c: step(it * i, c, True), init)
        carry = lax.fori_loop(1, i, lambda j, c: step(j, c, False), carry)
        outs = [acc_sc[a] * (1.0 / l) for a, (_, l) in enumerate(carry)]
        for a in range(0, hb, 2):
            o_ref[:, HALF * a:HALF * (a + 2)] = jnp.concatenate(outs[a:a + 2], axis=0).T
        for a, (m, l) in enumerate(carry):
            lse_ref[:, a:a + 1] = jnp.broadcast_to(m + jnp.log2(l), (LANE, tq)).T[:, :1]
        if n:
            pl.when(step_id == steps - 1)(plan.finish)

    blk = pl.BlockSpec((tq, hb * LANE), lambda h, i: (i, h))
    full = pl.BlockSpec((t, hb * LANE), lambda h, i: (0, h))
    packed = pl.BlockSpec((tq, hb * HALF), lambda h, i: (i, h))
    out = pl.pallas_call(
        body, name="mla_fwd_gather" if n else "mla_fwd", grid=(MLA_HEADS // hb, nq),
        in_specs=[blk, full, full] + [ANY] * n,
        out_specs=[packed, pl.BlockSpec((None, tq, hb), lambda h, i: (h, i, 0))] + [ANY] * n,
        out_shape=[jax.ShapeDtypeStruct((t, MLA_OUT_W), F32), _head_stats(t, hb)]
        + [jax.ShapeDtypeStruct((N_DEV,) + a.shape, a.dtype) for a in shards],
        scratch_shapes=[pltpu.VMEM((hb, HALF, tq), F32)] + (_comm_sems(n) if n else []),
        compiler_params=_params("arbitrary", "arbitrary"),
    )(q, k, v, *shards)
    return out[0], out[1], out[2:]


def _mix_fwd(h, oa, ob, ga, gb, wo, g2):
    t = h.shape[0]
    tm = _tile(t)

    def body(h_ref, oa_ref, ob_ref, ga_ref, gb_ref, wo_ref, g2_ref, h2_ref, mix_ref, u2_ref):
        oa_v = oa_ref[...]
        ob_v = ob_ref[...]
        na = (oa_v * _rms_r(oa_v, SWA_Q_W) * ga_ref[...]).astype(BF16)
        nb = (ob_v * _rms_r(ob_v, MLA_OUT_W) * gb_ref[...]).astype(BF16)
        mix_ref[:, :SWA_Q_W] = na
        mix_ref[:, SWA_Q_W:] = nb
        h2 = h_ref[...] + _dot(na, wo_ref[:SWA_Q_W, :]) + _dot(nb, wo_ref[SWA_Q_W:, :])
        h2_ref[...] = h2
        u2_ref[...] = (h2 * _rms_r(h2, D_MODEL) * g2_ref[...]).astype(BF16)

    mix_w = SWA_Q_W + MLA_OUT_W
    return pl.pallas_call(
        body, name="mix_fwd", grid=(t // tm,),
        in_specs=[_row(tm, D_MODEL), _row(tm, SWA_Q_W), _row(tm, MLA_OUT_W), _const(ga.shape), _const(gb.shape),
                  _const(wo.shape), _const(g2.shape)],
        out_specs=[_row(tm, D_MODEL), _row(tm, mix_w), _row(tm, D_MODEL)],
        out_shape=[jax.ShapeDtypeStruct((t, D_MODEL), F32), jax.ShapeDtypeStruct((t, mix_w), BF16),
                   jax.ShapeDtypeStruct((t, D_MODEL), BF16)],
        compiler_params=_params("parallel"),
    )(h, oa, ob, ga, gb, wo, g2)


def _ffn_fwd(h2, u2, wg_t, wu_t, wd, shards=()):
    t = h2.shape[0]
    tm = _tile(t)
    dff = wd.shape[0]
    n = len(shards)
    steps = t // tm

    def body(h2_ref, u2_ref, wg_ref, wu_ref, wd_ref, *rest):
        x_refs, (h3_ref, g_ref, up_ref), out_refs, sems = rest[:n], rest[n:n + 3], rest[n + 3:2 * n + 3], rest[2 * n + 3:]
        if n:
            plan = _gather_plan(x_refs, out_refs, *sems)
            pl.when(pl.program_id(0) == 0)(plan.start)
            pl.when(pl.program_id(0) == (3 * steps) // 4)(plan.forward)
        u2v = u2_ref[...]
        g = _dot_nt(u2v, wg_ref[...])
        up = _dot_nt(u2v, wu_ref[...])
        g_ref[...] = g.astype(BF16)
        up_ref[...] = up.astype(BF16)
        a = (g * jax.nn.sigmoid(g) * up).astype(BF16)
        h3_ref[...] = h2_ref[...] + _dot(a, wd_ref[...])
        if n:
            pl.when(pl.program_id(0) == steps - 1)(plan.finish)

    out = pl.pallas_call(
        body, name="ffn_fwd_gather" if n else "ffn_fwd", grid=(steps,),
        in_specs=[_row(tm, D_MODEL), _row(tm, D_MODEL), _const(wg_t.shape), _const(wu_t.shape), _const(wd.shape)] + [ANY] * n,
        out_specs=[_row(tm, D_MODEL), _row(tm, dff), _row(tm, dff)] + [ANY] * n,
        out_shape=[jax.ShapeDtypeStruct((t, D_MODEL), F32), jax.ShapeDtypeStruct((t, dff), BF16),
                   jax.ShapeDtypeStruct((t, dff), BF16)] + [jax.ShapeDtypeStruct((N_DEV,) + a.shape, a.dtype) for a in shards],
        scratch_shapes=_comm_sems(n) if n else [],
        compiler_params=_params("arbitrary" if n else "parallel"),
    )(h2, u2, wg_t, wu_t, wd, *shards)
    return out[:3], out[3:]


def _loss_bwd(h, gf, target):
    t = h.shape[0]
    tm = _tile(t)
    r = tm // BLOCK
    assert FRONT + N_META == BLOCK and target.shape[0] == t - BLOCK

    def body(h_ref, gf_ref, *rest):
        t_refs, (dh_ref, dgf_ref, loss_ref) = rest[:r], rest[r:]
        i = pl.program_id(0)
        hv = h_ref[...]
        y = hv * _rms_r(hv, D_MODEL) * gf_ref[...]
        row = i * tm + lax.broadcasted_iota(jnp.int32, (tm, 1), 0)
        tv = jnp.concatenate([t_ref[...] for t_ref in t_refs], axis=0)
        err = jnp.where(row >= BLOCK, y - tv, 0.0)
        dx, dg = _rms_bwd(hv, gf_ref[...], err * (1.0 / D_MODEL), D_MODEL)
        dh_ref[...] = dx
        _acc(dgf_ref, dg, i == 0)
        part = 0.5 * jnp.sum(jnp.sum(err * err, axis=1, keepdims=True) * (1.0 / D_MODEL), axis=0, keepdims=True)
        _acc(loss_ref, jnp.broadcast_to(part, (1, LANE)), i == 0)

    t_specs = [pl.BlockSpec((BLOCK, D_MODEL), lambda i, b=b: (jnp.maximum(r * i + b - 1, 0), 0)) for b in range(r)]
    return pl.pallas_call(
        body, name="loss_bwd", grid=(t // tm,),
        in_specs=[_row(tm, D_MODEL), _const(gf.shape)] + t_specs,
        out_specs=[_row(tm, D_MODEL), _const((1, D_MODEL)), _const((1, LANE))],
        out_shape=[jax.ShapeDtypeStruct((t, D_MODEL), F32), jax.ShapeDtypeStruct((1, D_MODEL), F32),
                   jax.ShapeDtypeStruct((1, LANE), F32)],
        compiler_params=_params("arbitrary"),
    )(h, gf, *[target] * r)


def _tn_matmul(a, b, name, cols=None, keep=None):
    t, n = b.shape
    first, k = cols or (0, a.shape[1])
    tk = next(c for c in (k, 1024, 512, 256, 128) if k % c == 0 and first % c == 0 and c <= 1024)
    fits = lambda c: 2 * (t * (tk + c) * 2 + tk * c * 2) <= TN_VMEM_BUDGET
    tn = next(c for c in (n, 1024, 512, 256, 128) if n % c == 0 and fits(c))
    kept = tk if keep is None else sum(size for _, size in keep)

    def body(a_ref, b_ref, o_ref):
        if keep is None:
            o_ref[...] = _dot_tn(a_ref[...], b_ref[...]).astype(BF16)
        else:
            at = a_ref[...].T
            at = jnp.concatenate([at[start:start + size] for start, size in keep], axis=0)
            o_ref[...] = _dot(at, b_ref[...]).astype(BF16)

    return pl.pallas_call(
        body, name=name, grid=(k // tk, n // tn),
        in_specs=[pl.BlockSpec((t, tk), lambda i, j: (0, i + first // tk)), pl.BlockSpec((t, tn), lambda i, j: (0, j))],
        out_specs=pl.BlockSpec((kept, tn), lambda i, j: (i, j)),
        out_shape=jax.ShapeDtypeStruct((k // tk * kept, n), BF16),
        compiler_params=_params("parallel", "parallel"),
    )(a, b)


def _ffn_bwd_a(dh3, g, up, wd):
    t = dh3.shape[0]
    tm = _tile(t)
    dff = wd.shape[0]

    def body(dh3_ref, g_ref, up_ref, wd_ref, a_ref, dgu_ref, dh3b_ref):
        dh3b = dh3_ref[...].astype(BF16)
        dh3b_ref[...] = dh3b
        da = _dot_nt(dh3b, wd_ref[...])
        gv = g_ref[...].astype(F32)
        upv = up_ref[...].astype(F32)
        sg = jax.nn.sigmoid(gv)
        silu = gv * sg
        a_ref[...] = (silu * upv).astype(BF16)
        dgu_ref[:, :dff] = (da * upv * (sg * (1.0 + gv * (1.0 - sg)))).astype(BF16)
        dgu_ref[:, dff:] = (da * silu).astype(BF16)

    return pl.pallas_call(
        body, name="ffn_bwd_a", grid=(t // tm,),
        in_specs=[_row(tm, D_MODEL), _row(tm, dff), _row(tm, dff), _const(wd.shape)],
        out_specs=[_row(tm, dff), _row(tm, 2 * dff), _row(tm, D_MODEL)],
        out_shape=[jax.ShapeDtypeStruct((t, dff), BF16), jax.ShapeDtypeStruct((t, 2 * dff), BF16),
                   jax.ShapeDtypeStruct((t, D_MODEL), BF16)],
        compiler_params=_params("parallel"),
    )(dh3, g, up, wd)


def _ffn_bwd_b(dh3, dgu, h2, g2, wg_t, wu_t):
    t = dh3.shape[0]
    tm = _tile(t)
    dff = wg_t.shape[0]

    def body(dh3_ref, dgu_ref, h2_ref, g2_ref, wg_ref, wu_ref, dh2_ref, dh2b_ref, dg2_ref):
        du2 = _dot(dgu_ref[:, :dff], wg_ref[...]) + _dot(dgu_ref[:, dff:], wu_ref[...])
        dx, dg = _rms_bwd(h2_ref[...], g2_ref[...], du2, D_MODEL)
        dh2 = dh3_ref[...] + dx
        dh2_ref[...] = dh2
        dh2b_ref[...] = dh2.astype(BF16)
        _acc(dg2_ref, dg, pl.program_id(0) == 0)

    return pl.pallas_call(
        body, name="ffn_bwd_b", grid=(t // tm,),
        in_specs=[_row(tm, D_MODEL), _row(tm, 2 * dff), _row(tm, D_MODEL), _const(g2.shape), _const(wg_t.shape),
                  _const(wu_t.shape)],
        out_specs=[_row(tm, D_MODEL), _row(tm, D_MODEL), _const((1, D_MODEL))],
        out_shape=[jax.ShapeDtypeStruct((t, D_MODEL), F32), jax.ShapeDtypeStruct((t, D_MODEL), BF16),
                   jax.ShapeDtypeStruct((1, D_MODEL), F32)],
        compiler_params=_params("arbitrary"),
    )(dh3, dgu, h2, g2, wg_t, wu_t)


def _mix_bwd(dh2, oa, ob, ga, gb, wo):
    t = dh2.shape[0]
    tm = _tile(t)

    def body(dh2_ref, oa_ref, ob_ref, ga_ref, gb_ref, wo_ref, doa_ref, dob_ref, dl_ref, dga_ref, dgb_ref):
        first = pl.program_id(0) == 0
        d = dh2_ref[...]
        ob_v = ob_ref[...]
        dxa, dga = _rms_bwd(oa_ref[...], ga_ref[...], _dot_nt(d, wo_ref[:SWA_Q_W, :]), SWA_Q_W)
        dxb, dgb = _rms_bwd(ob_v, gb_ref[...], _dot_nt(d, wo_ref[SWA_Q_W:, :]), MLA_OUT_W)
        lower = lax.broadcasted_iota(jnp.int32, (tm, LANE), 1) < HALF
        for hd in range(MLA_HEADS):
            sl = slice(LANE * (hd // 2), LANE * (hd // 2 + 1))
            mine = lower if hd % 2 == 0 else jnp.logical_not(lower)
            delta = jnp.sum(jnp.where(mine, ob_v[:, sl] * dxb[:, sl], 0.0), axis=1, keepdims=True)
            dl_ref[hd // MLA_HB, :, hd % MLA_HB:hd % MLA_HB + 1] = delta
        for ref, dx, heads in ((doa_ref, dxa, SWA_HEADS), (dob_ref, dxb, MLA_HEADS)):
            for hd in range(heads):
                slab = dx[:, LANE * (hd // 2):LANE * (hd // 2 + 1)]
                ref[:, LANE * hd:LANE * (hd + 1)] = _unpack_pair(slab, hd % 2).astype(BF16)
        _acc(dga_ref, dga, first)
        _acc(dgb_ref, dgb, first)

    return pl.pallas_call(
        body, name="mix_bwd", grid=(t // tm,),
        in_specs=[_row(tm, D_MODEL), _row(tm, SWA_Q_W), _row(tm, MLA_OUT_W), _const(ga.shape), _const(gb.shape),
                  _const(wo.shape)],
        out_specs=[_row(tm, HP), _row(tm, HP), pl.BlockSpec((MLA_HEADS // MLA_HB, tm, MLA_HB), lambda i: (0, i, 0)),
                   _const((1, SWA_Q_W)), _const((1, MLA_OUT_W))],
        out_shape=[jax.ShapeDtypeStruct((t, HP), BF16), jax.ShapeDtypeStruct((t, HP), BF16), _head_stats(t),
                   jax.ShapeDtypeStruct((1, SWA_Q_W), F32), jax.ShapeDtypeStruct((1, MLA_OUT_W), F32)],
        compiler_params=_params("arbitrary"),
    )(dh2, oa, ob, ga, gb, wo)


def _swa_bwd(sinks, q, k, v, o, do):
    t = q.shape[0]
    ts = _tile(t)

    def body(sink_ref, q_ref, kp_ref, kc_ref, vp_ref, vc_ref, o_ref, do_ref,
             dq_ref, dkc_ref, dkp_ref, dvc_ref, dvp_ref, dsink_ref):
        n = pl.program_id(0)
        chains = _swa_chains(t)
        qs, ks, probs = _swa_scores(sink_ref, q_ref, kp_ref, kc_ref, n, t)
        dos = [_swa_group(do_ref, slice(BLOCK * rb, BLOCK * (rb + 1)), j) for rb, j in chains]
        vs = [_swa_keys(vp_ref, vc_ref, rb, j) for rb, j in chains]
        dps = [_dot_nt(v2, do4) for do4, v2 in zip(dos, vs)]
        dss, dsks = [], []
        for (rb, j), (p, psink), do4, dp in zip(chains, probs, dos, dps):
            o4 = _swa_packed_group(o_ref, slice(BLOCK * rb, BLOCK * (rb + 1)), j)
            delta = jnp.sum(o4 * do4.astype(F32), axis=1, keepdims=True)
            delta = jnp.broadcast_to(delta, (SWA_GROUP * BLOCK, LANE)).T[:1, :]
            dss.append((p * (dp - delta) * SCALE_A).astype(BF16))
            dsks.append(-psink * delta)
        dqs = [_dot_tn(k2[:, :HALF], ds) for ds, k2 in zip(dss, ks)]
        dks = [_dot(ds, q4) for ds, q4 in zip(dss, qs)]
        dvs = [_dot(p.astype(BF16), do4) for (p, _), do4 in zip(probs, dos)]
        dsink = [jnp.zeros((1, LANE), F32)] * SWA_HEADS
        ext = {}
        for (rb, j), dq4, dk2, dv2, dsk in zip(chains, dqs, dks, dvs, dsks):
            for g in range(SWA_GROUP):
                hd = SWA_GROUP * j + g
                cols = slice(BLOCK * g, BLOCK * (g + 1))
                dq_ref[BLOCK * rb:BLOCK * (rb + 1), LANE * hd:LANE * (hd + 1)] = jnp.concatenate(
                    [dq4[:, cols], jnp.zeros((HALF, BLOCK), F32)], axis=0).T.astype(BF16)
                dsink[hd] = dsink[hd] + jnp.sum(dsk[:, cols], axis=1, keepdims=True)
            for half in range(2):
                key = (j, rb + half)
                part = (dk2[BLOCK * half:BLOCK * (half + 1)], dv2[BLOCK * half:BLOCK * (half + 1)])
                ext[key] = part if key not in ext else (ext[key][0] + part[0], ext[key][1] + part[1])
        for (j, blk), (dk, dv) in ext.items():
            sl = slice(LANE * j, LANE * (j + 1))
            if blk == 0:
                dkp_ref[:, sl] = dk
                dvp_ref[:, sl] = dv
            else:
                dkc_ref[BLOCK * (blk - 1):BLOCK * blk, sl] = dk
                dvc_ref[BLOCK * (blk - 1):BLOCK * blk, sl] = dv
        for hd in range(SWA_HEADS):
            _acc(dsink_ref.at[hd:hd + 1, :], jnp.broadcast_to(dsink[hd], (1, LANE)), n == 0)

    cur = lambda n: (n, 0)
    kv = pl.BlockSpec((ts, 2 * LANE), cur)
    kvp = pl.BlockSpec((BLOCK, 2 * LANE), cur)
    hp = pl.BlockSpec((ts, HP), cur)
    kvs = jax.ShapeDtypeStruct((t, 2 * LANE), F32)
    kvps = jax.ShapeDtypeStruct((t // ts * BLOCK, 2 * LANE), F32)
    return pl.pallas_call(
        body, name="swa_bwd", grid=(t // ts,),
        in_specs=_swa_specs(t) + [pl.BlockSpec((ts, SWA_Q_W), cur), hp],
        out_specs=[hp, kv, kvp, kv, kvp, _const((SWA_HEADS, LANE))],
        out_shape=[jax.ShapeDtypeStruct((t, HP), BF16), kvs, kvps, kvs, kvps,
                   jax.ShapeDtypeStruct((SWA_HEADS, LANE), F32)],
        compiler_params=_params("arbitrary"),
    )(sinks, q, k, k, v, v, o, do)


def _mla_bwd(q, k, v, do, lse, dl, slabs=()):
    t = q.shape[0]
    tq = _tile(t)
    nq = t // tq
    n = len(slabs)
    hb = MLA_HB
    steps = (MLA_HEADS // hb) * nq

    def body(k_ref, v_ref, q_ref, do_ref, lse_ref, dl_ref, *rest):
        in_refs, (dq_ref, dk_ref, dv_ref), out_refs = rest[:n], rest[n:n + 3], rest[n + 3:2 * n + 3]
        (dq_sc, dk_sc, dv_sc), sems = rest[2 * n + 3:2 * n + 6], rest[2 * n + 6:]
        j = pl.program_id(1)
        step_id = pl.program_id(0) * nq + j
        if n:
            plan = _exchange_plan(in_refs, out_refs, *sems)
            pl.when(step_id == 0)(plan.start)

        @pl.when(j == 0)
        def _():
            dq_sc[...] = jnp.zeros(dq_sc.shape, F32)

        dk_sc[...] = jnp.zeros(dk_sc.shape, F32)
        dv_sc[...] = jnp.zeros(dv_sc.shape, F32)
        ks, vs = _heads(k_ref, hb), _heads(v_ref, hb)

        def step(i, carry, masked):
            rows = pl.ds(pl.multiple_of(i * tq, tq), tq)
            qs, dos = _heads(q_ref, hb, rows), _heads(do_ref, hb, rows)
            ss = [_dot_nt(qh, kh) for qh, kh in zip(qs, ks)]
            dps = [_dot_nt(doh, vh) for doh, vh in zip(dos, vs)]
            if masked:
                mask = _causal_mask(i * tq, j * tq, tq, tq, False)
                ss = [jnp.where(mask, s_, NEG) for s_ in ss]
            ps = [jnp.exp2(s_ - lse_ref[rows, a:a + 1]) for a, s_ in enumerate(ss)]
            dss = [(p * (dp - dl_ref[rows, a:a + 1])).astype(BF16) for a, (p, dp) in enumerate(zip(ps, dps))]
            for a, (ds, p, qh, kh, doh) in enumerate(zip(dss, ps, qs, ks, dos)):
                dq_sc[a, rows, :] += _dot(ds, kh)
                dk_sc[a, :MLA_QK_DIM, :] += _dot_tn(qh[:, :MLA_QK_DIM], ds)
                dv_sc[a, :MLA_V_DIM, :] += _dot_tn(doh[:, :MLA_V_DIM], p.astype(BF16))
            return carry

        split = jnp.where(j == 0, nq, j + 1)
        lax.fori_loop(j, split, lambda i, c: step(i, c, True), 0)
        lax.fori_loop(split, nq, lambda i, c: step(i, c, False), 0)
        for a in range(hb):
            dk_ref[:, LANE * a:LANE * (a + 1)] = (dk_sc[a] * (1.0 / LOG2E)).T.astype(BF16)
            dv_ref[:, LANE * a:LANE * (a + 1)] = dv_sc[a].T.astype(BF16)

        @pl.when(j == nq - 1)
        def _():
            for a in range(hb):
                dq_ref[:, LANE * a:LANE * (a + 1)] = (dq_sc[a] * SCALE_B).astype(BF16)

        if n:
            pl.when(step_id == steps - 1)(plan.finish)

    blk = pl.BlockSpec((tq, hb * LANE), lambda h, j: (j, h))
    full = pl.BlockSpec((t, hb * LANE), lambda h, j: (0, h))
    cols = pl.BlockSpec((None, t, hb), lambda h, j: (h, 0, 0))
    out = pl.pallas_call(
        body, name="mla_bwd_exchange" if n else "mla_bwd", grid=(MLA_HEADS // hb, nq),
        in_specs=[blk, blk, full, full, cols, cols] + [ANY] * n, out_specs=[full, blk, blk] + [ANY] * n,
        out_shape=[jax.ShapeDtypeStruct((t, HP), BF16)] * 3 + [jax.ShapeDtypeStruct(a.shape, a.dtype) for a in slabs],
        scratch_shapes=[pltpu.VMEM((hb, t, LANE), F32)] + [pltpu.VMEM((hb, LANE, tq), F32)] * 2
        + (_comm_sems(n) if n else []),
        compiler_params=_params("arbitrary", "arbitrary"),
    )(k, v, q, do, lse, dl, *slabs)
    return out[:3], out[3:]


def _pre_bwd(dh2, h, cq, ckv, dqa, dka, dka_next, dva, dva_next, dqb, dkf, dvb, g1, win, gq, wqu, gkv, wkv, tabs):
    t = h.shape[0]
    tm = _tile(t)

    def body(dh2_ref, h_ref, cq_ref, ckv_ref, dqa_ref, dka_ref, dkan_ref, dva_ref, dvan_ref, dqb_ref, dkf_ref, dvb_ref,
             g1_ref, win_ref, gq_ref, wqu_ref, gkv_ref, wkv_ref, tab_ref,
             dh_ref, dp_ref, dqbo_ref, dkvo_ref, dg1_ref, dgq_ref, dgkv_ref):
        first = pl.program_id(0) == 0
        ca, sa1, sa2, cb, sb1, sb2, ck = _tabs(tab_ref)
        dkr = jnp.zeros((tm, LANE), F32)
        for c in range(MLA_HEADS):
            sl = slice(LANE * c, LANE * (c + 1))
            dqbo_ref[:, sl] = _rope_t(dqb_ref[:, sl].astype(F32), cb, sb1, sb2, 16).astype(BF16)
            dkr += dkf_ref[:, sl].astype(F32)
        dkvo_ref[:, :HP] = dkf_ref[...]
        dkvo_ref[:, HP:] = dvb_ref[...]
        dcq, dgq = _rms_bwd(cq_ref[...], gq_ref[...], _dot(dqbo_ref[...], wqu_ref[...]), MLA_Q_RANK)
        dckv, dgkv = _rms_bwd(ckv_ref[...], gkv_ref[...], _dot(dkvo_ref[...], wkv_ref[...]), MLA_KV_RANK)
        for c in range(SWA_HEADS):
            sl = slice(LANE * c, LANE * (c + 1))
            dp_ref[:, PO_QA + LANE * c:PO_QA + LANE * (c + 1)] = _rope_t(dqa_ref[:, sl].astype(F32), ca, sa1, sa2,
                                                                          32).astype(BF16)
        last = slice(tm - BLOCK, tm)
        more = pl.program_id(0) < t // tm - 1
        for c in range(SWA_KV_HEADS):
            sl = slice(LANE * c, LANE * (c + 1))
            dk = dka_ref[:, sl]
            dk_last = dk[tm - BLOCK:] + jnp.where(more, dkan_ref[:, sl], 0.0)
            cols = slice(PO_KA + LANE * c, PO_KA + LANE * (c + 1))
            if tm > BLOCK:
                dp_ref[:tm - BLOCK, cols] = _rope_t(dk[:tm - BLOCK], ca[:tm - BLOCK], sa1[:tm - BLOCK], sa2[:tm - BLOCK],
                                                    32).astype(BF16)
            dp_ref[last, cols] = _rope_t(dk_last, ca[tm - BLOCK:], sa1[tm - BLOCK:], sa2[tm - BLOCK:], 32).astype(BF16)
        if tm > BLOCK:
            dp_ref[:tm - BLOCK, PO_VA:PO_CQ] = dva_ref[:tm - BLOCK, :].astype(BF16)
        dp_ref[last, PO_VA:PO_CQ] = (dva_ref[tm - BLOCK:, :] + jnp.where(more, dvan_ref[...], 0.0)).astype(BF16)
        dp_ref[:, PO_CQ:PO_CKV] = dcq.astype(BF16)
        dp_ref[:, PO_CKV:PO_KR] = dckv.astype(BF16)
        dp_ref[:, PO_KR:PW_IN] = _rope_t(dkr, ck, sb1, sb2, 16).astype(BF16)
        dx, dg1 = _rms_bwd(h_ref[...], g1_ref[...], _dot(dp_ref[...], win_ref[...]), D_MODEL)
        dh_ref[...] = dh2_ref[...] + dx
        _acc(dg1_ref, dg1, first)
        _acc(dgq_ref, dgq, first)
        _acc(dgkv_ref, dgkv, first)

    kv = _row(tm, 2 * LANE)
    nxt = pl.BlockSpec((BLOCK, 2 * LANE), lambda i: (jnp.minimum(i + 1, t // tm - 1), 0))
    return pl.pallas_call(
        body, name="pre_bwd", grid=(t // tm,),
        in_specs=[_row(tm, D_MODEL), _row(tm, D_MODEL), _row(tm, MLA_Q_RANK), _row(tm, MLA_KV_RANK), _row(tm, HP),
                  kv, nxt, kv, nxt, _row(tm, HP), _row(tm, HP), _row(tm, HP),
                  _const(g1.shape), _const(win.shape), _const(gq.shape), _const(wqu.shape), _const(gkv.shape),
                  _const(wkv.shape), _row(tm, N_TAB * LANE)],
        out_specs=[_row(tm, D_MODEL), _row(tm, PW_IN), _row(tm, HP), _row(tm, 2 * HP),
                   _const((1, D_MODEL)), _const((1, MLA_Q_RANK)), _const((1, MLA_KV_RANK))],
        out_shape=[jax.ShapeDtypeStruct((t, D_MODEL), F32), jax.ShapeDtypeStruct((t, PW_IN), BF16),
                   jax.ShapeDtypeStruct((t, HP), BF16), jax.ShapeDtypeStruct((t, 2 * HP), BF16),
                   jax.ShapeDtypeStruct((1, D_MODEL), F32), jax.ShapeDtypeStruct((1, MLA_Q_RANK), F32),
                   jax.ShapeDtypeStruct((1, MLA_KV_RANK), F32)],
        compiler_params=_params("arbitrary"),
    )(dh2, h, cq, ckv, dqa, dka, dka_next, dva, dva_next, dqb, dkf, dvb, g1, win, gq, wqu, gkv, wkv, tabs)


def _rope_tables(t):
    pos = (jnp.arange(t, dtype=jnp.int32) - FRONT).astype(F32)[:, None]
    lane = jnp.arange(LANE)[None, :]

    def table(dim, start):
        half = dim // 2
        inv = ROPE_THETA ** (-jnp.arange(0, dim, 2, dtype=F32) / dim)
        ang = pos * inv[None, :]
        cos = jnp.concatenate([jnp.cos(ang)] * 2, axis=1)
        sin = jnp.concatenate([jnp.sin(ang)] * 2, axis=1)
        pad = lambda a: jnp.pad(a, ((0, 0), (start, LANE - start - dim)))
        first = (lane >= start) & (lane < start + half)
        second = (lane >= start + half) & (lane < start + dim)
        return pad(cos), jnp.where(first, -pad(sin), 0.0), jnp.where(second, pad(sin), 0.0)

    ca, sa1, sa2 = table(SWA_HEAD_DIM, 0)
    ck, sb1, sb2 = table(MLA_ROPE_DIM, MLA_NOPE_DIM)
    cb = jnp.where(lane < MLA_NOPE_DIM, 1.0, ck)
    return jnp.concatenate([ca, sa1, sa2, cb, sb1, sb2, ck], axis=1)


def _pad_heads(w, heads, dim, axis):
    shp = w.shape
    w = w.reshape(shp[:axis] + (heads, dim) + shp[axis + 1:])
    pad = [(0, 0)] * w.ndim
    pad[axis + 1] = (0, LANE - dim)
    return jnp.pad(w, pad).reshape(shp[:axis] + (heads * LANE,) + shp[axis + 1:])


def _unpad_heads(w, heads, dim, axis):
    shp = w.shape
    w = w.reshape(shp[:axis] + (heads, LANE) + shp[axis + 1:])
    w = lax.slice_in_dim(w, 0, dim, axis=axis + 1)
    return w.reshape(shp[:axis] + (heads * dim,) + shp[axis + 1:])


def _pad_layer(w_in, w_q_up, w_kv_up):
    o1 = SWA_Q_W
    o2 = o1 + SWA_KV_W
    o3 = o2 + SWA_KV_W
    o4 = o3 + MLA_Q_RANK
    o5 = o4 + MLA_KV_RANK
    kr = jnp.pad(w_in[o5:], ((MLA_NOPE_DIM, LANE - MLA_QK_DIM), (0, 0)))
    win = jnp.concatenate([
        _pad_heads(w_in[:o1], SWA_HEADS, SWA_HEAD_DIM, 0),
        _pad_heads(w_in[o1:o2], SWA_KV_HEADS, SWA_HEAD_DIM, 0),
        _pad_heads(w_in[o2:o3], SWA_KV_HEADS, SWA_HEAD_DIM, 0),
        w_in[o3:o5], kr], axis=0)
    wqu = _pad_heads(w_q_up, MLA_HEADS, MLA_QK_DIM, 0)
    kv = w_kv_up.reshape(MLA_HEADS, MLA_NOPE_DIM + MLA_V_DIM, MLA_KV_RANK)
    wkv = jnp.concatenate([
        _pad_heads(kv[:, :MLA_NOPE_DIM].reshape(-1, MLA_KV_RANK), MLA_HEADS, MLA_NOPE_DIM, 0),
        _pad_heads(kv[:, MLA_NOPE_DIM:].reshape(-1, MLA_KV_RANK), MLA_HEADS, MLA_V_DIM, 0)], axis=0)
    return win, wqu, wkv


IN_KEEP_SWA = [(LANE * hd, SWA_HEAD_DIM) for hd in range(SWA_HEADS)]
IN_KEEP_REST = ([(LANE * hd, SWA_HEAD_DIM) for hd in range(2 * SWA_KV_HEADS)] + [(PO_CQ - PO_KA, PO_KR - PO_CQ)]
                + [(PO_KR - PO_KA + MLA_NOPE_DIM, MLA_ROPE_DIM)])


def _unpad_layer(d_w_in, dwqu, dwkv):
    d_w_q_up = _unpad_heads(dwqu, MLA_HEADS, MLA_QK_DIM, 0)
    dk = _unpad_heads(dwkv[:, :HP], MLA_HEADS, MLA_NOPE_DIM, 1).reshape(MLA_KV_RANK, MLA_HEADS, MLA_NOPE_DIM)
    dv = _unpad_heads(dwkv[:, HP:], MLA_HEADS, MLA_V_DIM, 1).reshape(MLA_KV_RANK, MLA_HEADS, MLA_V_DIM)
    d_w_kv_up = jnp.concatenate([dk, dv], axis=2).reshape(MLA_KV_RANK, -1).T
    return d_w_in, d_w_q_up, d_w_kv_up


def _train_example(x, target, meta, vec, weights):
    s = x.shape[0]
    depth = vec["attn_norm"].shape[0]
    t = FRONT + N_META + s
    assert t % BLOCK == 0
    tabs = _rope_tables(t)
    h = jnp.concatenate([jnp.zeros((FRONT, D_MODEL), F32), meta, x], axis=0)
    row = lambda v: v[None, :]

    saved = []
    for l in range(depth):
        win, wqu, wkv = _pad_layer(*weights.attn_in(l))
        g1, gq, gkv, g2, ga, gb = (row(vec[n][l]) for n in ("attn_norm", "q_norm", "kv_norm", "ffn_norm",
                                                            "out_norm_swa", "out_norm_mla"))
        sk = row(vec["sinks"][l])
        u, qa, ka, va, cq, ckv, qn, kvn, qb, kf, vb = _pre_fwd(h, g1, win, gq, wqu, gkv, wkv, tabs)
        oa = _swa_fwd(sk, qa, ka, va)
        ob, lse = weights.mla_fwd(l, qb, kf, vb)
        lse = jnp.moveaxis(lse.reshape(t, MLA_HEADS // MLA_HB, MLA_HB), 1, 0)
        wo = weights.w_o(l)
        h2, mix, u2 = _mix_fwd(h, oa, ob, ga, gb, wo, g2)
        wg, wu, wd = weights.ffn(l)
        h3, gt, up = weights.ffn_fwd(l, h2, u2)
        saved.append((h, u, qa, ka, va, cq, ckv, qn, kvn, qb, kf, vb, oa, ob, lse, h2, mix, u2, gt, up,
                      win, wqu, wkv, wo, ga, gb, g1, gq, gkv, g2, sk, wg, wu, wd))
        h = h3

    dh, d_final, loss = _loss_bwd(h, row(vec["final_norm"]), target)

    grads = []
    for l in reversed(range(depth)):
        (h0, u, qa, ka, va, cq, ckv, qn, kvn, qb, kf, vb, oa, ob, lse, h2, mix, u2, gt, up,
         win, wqu, wkv, wo, ga, gb, g1, gq, gkv, g2, sk, wg, wu, wd) = saved[l]
        dff = wd.shape[0]
        act, dgu, dhb = _ffn_bwd_a(dh, gt, up, wd)
        weights.ffn_grads(l, _tn_matmul(dgu, u2, "dw_gate", (0, dff)), _tn_matmul(dgu, u2, "dw_up", (dff, dff)),
                          _tn_matmul(act, dhb, "dw_down"))
        dh2, dh2b, d_g2 = _ffn_bwd_b(dh, dgu, h2, g2, wg, wu)
        weights.attn_grads(l, w_o=_tn_matmul(mix, dh2b, "dw_o"))
        doa, dob, dl, d_ga, d_gb = _mix_bwd(dh2b, oa, ob, ga, gb, wo)
        dqa, dkc, dkp, dvc, dvp, dsink = _swa_bwd(sk, qa, ka, va, oa, doa)
        dqb, dkf, dvb = weights.mla_bwd(l, qb, kf, vb, dob, lse, dl)
        dh, dp, dqbo, dkvo, d_g1, d_gq, d_gkv = _pre_bwd(
            dh2, h0, cq, ckv, dqa, dkc, dkp, dvc, dvp, dqb, dkf, dvb,
            g1, win, gq, wqu, gkv, wkv, tabs)
        d_win = jnp.concatenate([_tn_matmul(dp, u, "dw_in_swa", (PO_QA, PO_KA), IN_KEEP_SWA),
                                 _tn_matmul(dp, u, "dw_in_rest", (PO_KA, PW_IN - PO_KA), IN_KEEP_REST)], axis=0)
        d_wqu = _tn_matmul(dqbo, qn, "dw_q_up")
        d_wkv = _tn_matmul(kvn, dkvo, "dw_kv_up")
        weights.attn_grads(l, **dict(zip(ATTN_IN, _unpad_layer(d_win, d_wqu, d_wkv))))
        grads.append(dict(attn_norm=d_g1[0], q_norm=d_gq[0], kv_norm=d_gkv[0], sinks=dsink[:, 0], out_norm_swa=d_ga[0],
                          out_norm_mla=d_gb[0], ffn_norm=d_g2[0]))
    grads = grads[::-1]
    stacked = {k: jnp.stack([g[k] for g in grads]) for k in grads[0]}
    stacked["final_norm"] = d_final[0]
    return loss[0, 0], dh[FRONT + N_META:], dh[FRONT:FRONT + N_META], stacked


MESH = pl.DeviceIdType.MESH
ANY = pl.BlockSpec(memory_space=pl.ANY)


def _place():
    return lax.axis_index("x"), lax.axis_index("y"), lax.axis_index("c")


def _index(x, y, c):
    return 4 * x + 2 * y + c


def _comm_sems(n):
    return [pltpu.SemaphoreType.DMA((n, N_DEV - 1)), pltpu.SemaphoreType.DMA((n, N_DEV - 1)),
            pltpu.SemaphoreType.DMA((n,))]


class _gather_plan:
    def __init__(self, x_refs, out_refs, send_sems, recv_sems, local_sems):
        self.x_refs, self.out_refs = x_refs, out_refs
        self.send_sems, self.recv_sems, self.local_sems = send_sems, recv_sems, local_sems
        self.n = len(x_refs)

    def _where(self):
        x, y, c = _place()
        return (x, y, c), (x, y, 1 - c), [(1 - x, y), (x, 1 - y), (1 - x, 1 - y)], c

    def _copy(self, i, k, block, to, from_input=False):
        slot = self.out_refs[i].at[_index(*block)]
        return pltpu.make_async_remote_copy(
            src_ref=self.x_refs[i] if from_input else slot, dst_ref=slot,
            send_sem=self.send_sems.at[i, k], recv_sem=self.recv_sems.at[i, k], device_id=to, device_id_type=MESH)

    def _mine(self, i, me):
        return pltpu.make_async_copy(self.x_refs[i], self.out_refs[i].at[_index(*me)], self.local_sems.at[i])

    def _first(self, me, sibling, chips, c):
        out = [self._copy(i, 1 + j, me, (*chip, c), True) for j, chip in enumerate(chips) for i in range(self.n)]
        return out + [self._copy(i, 0, me, sibling, True) for i in range(self.n)]

    def start(self):
        me, sibling, chips, c = self._where()
        for i in range(self.n):
            self._mine(i, me).start()
        for cp in self._first(me, sibling, chips, c):
            cp.start()

    def forward(self):
        me, sibling, chips, c = self._where()
        for j, chip in enumerate(chips):
            for i in range(self.n):
                self._copy(i, 1 + j, (*chip, c), me).wait_recv()
                self._copy(i, 4 + j, (*chip, c), sibling).start()

    def finish(self):
        me, sibling, chips, c = self._where()
        for i in range(self.n):
            self._copy(i, 0, sibling, me).wait_recv()
            for j, chip in enumerate(chips):
                self._copy(i, 4 + j, (*chip, 1 - c), me).wait_recv()
        for cp in self._first(me, sibling, chips, c):
            cp.wait_send()
        for j, chip in enumerate(chips):
            for i in range(self.n):
                self._copy(i, 4 + j, (*chip, c), sibling).wait_send()
        for i in range(self.n):
            self._mine(i, me).wait()


class _exchange_plan:
    def __init__(self, in_refs, out_refs, send_sems, recv_sems, local_sems):
        self.in_refs, self.out_refs = in_refs, out_refs
        self.send_sems, self.recv_sems, self.local_sems = send_sems, recv_sems, local_sems
        self.n = len(in_refs)

    def _copies(self):
        x, y, c = _place()
        me = _index(x, y, c)
        mine = [pltpu.make_async_copy(self.in_refs[i].at[me], self.out_refs[i].at[me], self.local_sems.at[i])
                for i in range(self.n)]
        remote = []
        for k in range(1, N_DEV):
            peer = (1 - x if k & 4 else x, 1 - y if k & 2 else y, 1 - c if k & 1 else c)
            remote += [pltpu.make_async_remote_copy(
                src_ref=self.in_refs[i].at[_index(*peer)], dst_ref=self.out_refs[i].at[me],
                send_sem=self.send_sems.at[i, k - 1], recv_sem=self.recv_sems.at[i, k - 1],
                device_id=peer, device_id_type=MESH) for i in range(self.n)]
        return mine, remote

    def start(self):
        mine, remote = self._copies()
        for cp in mine + remote:
            cp.start()

    def finish(self):
        mine, remote = self._copies()
        for cp in remote:
            cp.wait_recv()
        for cp in remote:
            cp.wait_send()
        for cp in mine:
            cp.wait()


def _all_gather(shards, name):
    n = len(shards)

    def body(*refs):
        plan = _gather_plan(refs[:n], refs[n:2 * n], *refs[2 * n:])
        plan.start()
        plan.forward()
        plan.finish()

    return pl.pallas_call(
        body, name=name, in_specs=[ANY] * n, out_specs=[ANY] * n, scratch_shapes=_comm_sems(n),
        out_shape=[jax.ShapeDtypeStruct((N_DEV,) + a.shape, a.dtype) for a in shards],
    )(*shards)


def _adamw(w, g, m, v):
    m = ADAM_B1 * m + (1.0 - ADAM_B1) * g
    v = ADAM_B2 * v + (1.0 - ADAM_B2) * (g * g)
    m_hat = m / (1.0 - ADAM_B1 ** ADAM_STEP)
    v_hat = v / (1.0 - ADAM_B2 ** ADAM_STEP)
    return -ADAM_LR * (m_hat / (jnp.sqrt(v_hat) + ADAM_EPS) + ADAM_WD * w), m, v


def _sum_slots(ref):
    g = ref[0].astype(F32)
    for s in range(1, N_DEV):
        g = g + ref[s].astype(F32)
    return g


def _reduce_adamw(parts, w, m, v, name, slabs=()):
    l, r, c = w.shape
    tile = max([d for d in range(16, ADAM_ROWS + 1, 16) if r % d == 0], default=r)
    last = r // tile - 1
    n = len(slabs)

    def body(*refs):
        p_refs, (w_ref, m_ref, v_ref), in_refs = refs[:l], refs[l:l + 3], refs[l + 3:l + 3 + n]
        (g_ref, d_ref, nm_ref, nv_ref), out_refs, sems = refs[l + 3 + n:l + 7 + n], refs[l + 7 + n:l + 7 + 2 * n], refs[l + 7 + 2 * n:]
        step_id = pl.program_id(0) * (last + 1) + pl.program_id(1)
        if n:
            plan = _exchange_plan(in_refs, out_refs, *sems)
            pl.when(step_id == 0)(plan.start)
        for layer in range(l):
            @pl.when(pl.program_id(0) == layer)
            def _(p_ref=p_refs[layer]):
                g = _sum_slots(p_ref)
                g_ref[...] = g
                d_ref[...], nm_ref[...], nv_ref[...] = _adamw(w_ref[...], g, m_ref[...], v_ref[...])
        if n:
            pl.when(step_id == l * (last + 1) - 1)(plan.finish)

    def part_spec(layer):
        return pl.BlockSpec((N_DEV, tile, c),
                            lambda i, j: (0, jnp.where(i == layer, j, jnp.where(i < layer, 0, last)), 0))

    blk = pl.BlockSpec((None, tile, c), lambda i, j: (i, j, 0))
    out = pl.pallas_call(
        body, name=name, grid=(l, r // tile),
        in_specs=[part_spec(layer) for layer in range(l)] + [blk, blk, blk] + [ANY] * n, out_specs=[blk] * 4 + [ANY] * n,
        out_shape=[jax.ShapeDtypeStruct((l, r, c), F32)] * 4 + [jax.ShapeDtypeStruct(a.shape, a.dtype) for a in slabs],
        scratch_shapes=_comm_sems(n) if n else [],
        compiler_params=_params("arbitrary", "arbitrary"),
    )(*parts, w, m, v, *slabs)
    return out[:4], out[4:]


def _sum_parts(parts, name):
    _, r, c = parts.shape

    def body(p_ref, g_ref):
        g_ref[...] = _sum_slots(p_ref)

    return pl.pallas_call(body, name=name, out_shape=jax.ShapeDtypeStruct((r, c), F32))(parts)


def _adamw_call(w, g, m, v, name):
    def body(w_ref, g_ref, m_ref, v_ref, d_ref, nm_ref, nv_ref):
        d_ref[...], nm_ref[...], nv_ref[...] = _adamw(w_ref[...], g_ref[...], m_ref[...], v_ref[...])

    return pl.pallas_call(body, name=name, out_shape=[jax.ShapeDtypeStruct(w.shape, F32)] * 3)(w, g, m, v)


ATTN_IN = ("w_in", "w_q_up", "w_kv_up")
ATTN = ATTN_IN + ("w_o",)
FFN = ("w_gate", "w_up", "w_down")
TRANSPOSED = ("w_in", "w_q_up", "w_kv_up", "w_gate", "w_up")
SMALL = ("attn_norm", "ffn_norm", "final_norm", "out_norm_swa", "out_norm_mla", "q_norm", "kv_norm", "sinks")
PACK_W = 1024
SMALL_ROWS = 16


def _pack(arrs, dtype):
    flat = jnp.concatenate([a.astype(dtype).reshape(-1) for a in arrs])
    return flat.reshape(-1, PACK_W)


def _unpack(packed, like):
    flat = packed.reshape(-1)
    out, off = [], 0
    for a in like:
        out.append(flat[off:off + a.size].reshape(a.shape))
        off += a.size
    return out


def _gather_to_full(gathered):
    return gathered.reshape((-1,) + gathered.shape[2:])


def _full_to_slabs(full):
    return full.reshape((N_DEV, -1) + full.shape[1:])


class _ShardedWeights:
    def __init__(self, shards, depth, meta_shard):
        self.shards, self.depth = shards, depth
        self.gathered, self.pending, self.parts = {}, {}, {}
        first = _all_gather([shards[n][0] for n in ATTN_IN] + [meta_shard], "gather_attn0")
        self.gathered.update(zip([(n, 0) for n in ATTN_IN], first))
        self.meta = jnp.moveaxis(first[-1], 0, 1).reshape(N_META, D_MODEL)

    def _gather(self, keys, run):
        self.gathered.update(zip(keys, run([self.shards[n][l] for n, l in keys])))

    def _full(self, names, l):
        return tuple(_gather_to_full(self.gathered[n, l]) for n in names)

    def attn_in(self, l):
        return self._full(ATTN_IN, l)

    def w_o(self, l):
        return self._full(("w_o",), l)[0]

    def ffn(self, l):
        return self._full(FFN, l)

    def mla_fwd(self, l, q, k, v):
        out = []
        self._gather([(n, l) for n in ("w_o",) + FFN], lambda xs: out.extend(_mla_fwd(q, k, v, xs)) or out[2])
        return out[0], out[1]

    def ffn_fwd(self, l, h2, u2):
        keys = [(n, l + 1) for n in ATTN_IN] if l + 1 < self.depth else []
        out = []
        self._gather(keys, lambda xs: out.extend(_ffn_fwd(h2, u2, *self.ffn(l), xs)) or out[1])
        return out[0]

    def _add(self, names, l, grads):
        for n, g in zip(names, grads):
            self.pending[n, l] = _full_to_slabs(g)

    def ffn_grads(self, l, *grads):
        self._add(FFN, l, grads)

    def attn_grads(self, l, **grads):
        self._add(list(grads), l, grads.values())

    def _exchange(self, run):
        keys = list(self.pending)
        self.parts.update(zip(keys, run([self.pending.pop(k) for k in keys])))

    def mla_bwd(self, l, *args):
        out = []
        self._exchange(lambda xs: out.extend(_mla_bwd(*args, xs)) or out[1])
        return out[0]

    def flush(self, run):
        self._exchange(run)


def kernel(x, meta_tokens, attn_norm, w_in, q_norm, w_q_up, kv_norm, w_kv_up, sinks, out_norm_swa, out_norm_mla, w_o, ffn_norm, w_gate, w_up, w_down, final_norm, loss_target, m_meta_tokens, m_attn_norm, m_w_in, m_q_norm, m_w_q_up, m_kv_norm, m_w_kv_up, m_sinks, m_out_norm_swa, m_out_norm_mla, m_w_o, m_ffn_norm, m_w_gate, m_w_up, m_w_down, m_final_norm, v_meta_tokens, v_attn_norm, v_w_in, v_q_norm, v_w_q_up, v_kv_norm, v_w_kv_up, v_sinks, v_out_norm_swa, v_out_norm_mla, v_w_o, v_ffn_norm, v_w_gate, v_w_up, v_w_down, v_final_norm):
    w = dict(meta_tokens=meta_tokens, attn_norm=attn_norm, w_in=w_in, q_norm=q_norm, w_q_up=w_q_up, kv_norm=kv_norm,
             w_kv_up=w_kv_up, sinks=sinks, out_norm_swa=out_norm_swa, out_norm_mla=out_norm_mla, w_o=w_o,
             ffn_norm=ffn_norm, w_gate=w_gate, w_up=w_up, w_down=w_down, final_norm=final_norm)
    m = dict(meta_tokens=m_meta_tokens, attn_norm=m_attn_norm, w_in=m_w_in, q_norm=m_q_norm, w_q_up=m_w_q_up,
             kv_norm=m_kv_norm, w_kv_up=m_w_kv_up, sinks=m_sinks, out_norm_swa=m_out_norm_swa,
             out_norm_mla=m_out_norm_mla, w_o=m_w_o, ffn_norm=m_ffn_norm, w_gate=m_w_gate, w_up=m_w_up,
             w_down=m_w_down, final_norm=m_final_norm)
    v = dict(meta_tokens=v_meta_tokens, attn_norm=v_attn_norm, w_in=v_w_in, q_norm=v_q_norm, w_q_up=v_w_q_up,
             kv_norm=v_kv_norm, w_kv_up=v_w_kv_up, sinks=v_sinks, out_norm_swa=v_out_norm_swa,
             out_norm_mla=v_out_norm_mla, w_o=v_w_o, ffn_norm=v_ffn_norm, w_gate=v_w_gate, w_up=v_w_up,
             w_down=v_w_down, final_norm=v_final_norm)
    names = list(w)
    big = ATTN + FFN
    depth = w_in.shape[0]
    me = _index(*_place())

    as_held = lambda n, a: jnp.swapaxes(a, 1, 2) if n in TRANSPOSED else a
    weights = _ShardedWeights({n: as_held(n, w[n]).astype(BF16) for n in big}, depth, meta_tokens)
    loss, grad_x, d_meta, grads = _train_example(x[0], loss_target[0], weights.meta, {n: w[n] for n in SMALL}, weights)

    g_big, d_big, m_big, v_big = {}, {}, {}, {}
    for n in FFN[::-1] + ATTN[::-1]:
        held = [as_held(n, a) for a in (w[n], m[n], v[n])]
        reduce = functools.partial(_reduce_adamw, [weights.parts[n, l] for l in range(depth)], *held, "reduce_adamw_" + n)
        if n == FFN[-1]:
            outs = []
            weights.flush(lambda xs: outs.extend(reduce(xs)) or outs[1])
        else:
            outs = reduce()
        g_big[n], d_big[n], m_big[n], v_big[n] = [as_held(n, a) for a in outs[0]]

    small = [grads[n] for n in SMALL] + [loss.reshape(1)]
    pad = SMALL_ROWS * PACK_W - sum(a.size for a in small)
    part = jnp.concatenate([_pack(small + [jnp.zeros((pad,), F32)], F32), d_meta], axis=0)
    total = _sum_parts(_all_gather([part], "gather_small")[0], "sum_small")
    small_w = [w[n] for n in SMALL]
    packs = [_pack([d[n] for n in SMALL] + [jnp.zeros((pad + 1,), F32)], F32) for d in (w, m, v)]
    upd = _adamw_call(packs[0], total[:SMALL_ROWS], packs[1], packs[2], "adamw_small")
    g_small, d_small, m_small, v_small = [dict(zip(SMALL, _unpack(p, small_w))) for p in (total[:SMALL_ROWS],) + tuple(upd)]
    loss_total = total[:SMALL_ROWS].reshape(-1)[SMALL_ROWS * PACK_W - pad - 1]
    g_meta = lax.dynamic_slice_in_dim(total[SMALL_ROWS:], me * LANE, LANE, axis=1)
    d_mt, m_mt, v_mt = _adamw_call(meta_tokens, g_meta, m_meta_tokens, v_meta_tokens, "adamw_meta")

    outs = []
    for got in ({**g_big, **g_small, "meta_tokens": g_meta}, {**d_big, **d_small, "meta_tokens": d_mt},
                {**m_big, **m_small, "meta_tokens": m_mt}, {**v_big, **v_small, "meta_tokens": v_mt}):
        outs += [got[n] for n in names]
    return (loss_total, grad_x[None], *outs)
```

```python
import jax
import jax.numpy as jnp
from jax import lax
from jax.experimental import pallas as pl
from jax.experimental.pallas import tpu as pltpu

F32 = jnp.float32
BF16 = jnp.bfloat16

D_MODEL = 1024
N_META = 16
BLOCK = 128
FRONT = (-N_META) % BLOCK
ROPE_THETA = 10000.0
EPS = 1e-6
NEG = -1e30
SWA_HEADS = 8
SWA_KV_HEADS = 2
SWA_GROUP = SWA_HEADS // SWA_KV_HEADS
SWA_HEAD_DIM = 64
MLA_HEADS = 8
MLA_Q_RANK = 256
MLA_KV_RANK = 128
MLA_NOPE_DIM = 64
MLA_ROPE_DIM = 32
MLA_V_DIM = 64
MLA_QK_DIM = MLA_NOPE_DIM + MLA_ROPE_DIM
SWA_Q_W = SWA_HEADS * SWA_HEAD_DIM
SWA_KV_W = SWA_KV_HEADS * SWA_HEAD_DIM
MLA_OUT_W = MLA_HEADS * MLA_V_DIM
SCALE_A = SWA_HEAD_DIM ** -0.5
SCALE_B = MLA_QK_DIM ** -0.5
LOG2E = 1.4426950408889634
Q_SCALE = SCALE_B * LOG2E
ADAM_LR = 0.001
ADAM_B1 = 0.9
ADAM_B2 = 0.999
ADAM_EPS = 1e-08
ADAM_WD = 0.01
ADAM_STEP = 10

LANE = 128
N_DEV = 8
HP = 8 * LANE
PO_QA, PO_KA, PO_VA = 0, HP, HP + 2 * LANE
PO_CQ = PO_VA + 2 * LANE
PO_CKV = PO_CQ + MLA_Q_RANK
PO_KR = PO_CKV + MLA_KV_RANK
PW_IN = PO_KR + LANE
N_TAB = 7
VMEM_LIMIT = 56 * 2 ** 20
TN_VMEM_BUDGET = 36 * 2 ** 20
MLA_HB = 4
MLA_HB_FWD = 8
ADAM_ROWS = 256
HALF = LANE // 2
assert SWA_HEAD_DIM == HALF and MLA_V_DIM == HALF

NT = (((1,), (1,)), ((), ()))
TN = (((0,), (0,)), ((), ()))


def _tile(t):
    return 384 if t % 384 == 0 else 128


def _params(*sem):
    return pltpu.CompilerParams(dimension_semantics=sem, vmem_limit_bytes=VMEM_LIMIT)


def _row(tm, n):
    return pl.BlockSpec((tm, n), lambda i: (i, 0))


def _const(shape):
    return pl.BlockSpec(shape, lambda i: (0,) * len(shape))


def _dot(a, b):
    return jnp.dot(a, b, preferred_element_type=F32)


def _dot_nt(a, b):
    return lax.dot_general(a, b, NT, preferred_element_type=F32)


def _dot_tn(a, b):
    return lax.dot_general(a, b, TN, preferred_element_type=F32)


def _rope(x, c, s1, s2, shift):
    return x * c + pltpu.roll(x, LANE - shift, 1) * s1 + pltpu.roll(x, shift, 1) * s2


def _rope_t(dy, c, s1, s2, shift):
    return dy * c + pltpu.roll(dy * s1, shift, 1) + pltpu.roll(dy * s2, LANE - shift, 1)


def _rms_r(x, n):
    return lax.rsqrt(jnp.sum(x * x, axis=-1, keepdims=True) * (1.0 / n) + EPS)


def _rms_bwd(x, g, dy, n):
    r = _rms_r(x, n)
    xh = x * r
    dxh = dy * g
    dx = r * (dxh - xh * (jnp.sum(dxh * xh, axis=-1, keepdims=True) * (1.0 / n)))
    return dx, jnp.sum(dy * xh, axis=0, keepdims=True)


def _acc(ref, val, first):
    @pl.when(first)
    def _():
        ref[...] = val

    @pl.when(jnp.logical_not(first))
    def _():
        ref[...] += val


def _pair_half(slab, half):
    return slab if half == 0 else pltpu.roll(slab, HALF, 1)


def _unpack_pair(slab, half):
    x = _pair_half(slab, half)
    return jnp.where(lax.broadcasted_iota(jnp.int32, x.shape, 1) < HALF, x, 0.0)


def _tabs(tab_ref):
    return [tab_ref[:, LANE * i:LANE * (i + 1)] for i in range(N_TAB)]


def _pre_fwd(h, g1, win, gq, wqu, gkv, wkv, tabs):
    t = h.shape[0]
    tm = _tile(t)

    def body(h_ref, g1_ref, win_ref, gq_ref, wqu_ref, gkv_ref, wkv_ref, tab_ref,
             u_ref, qa_ref, ka_ref, va_ref, cq_ref, ckv_ref, qn_ref, kvn_ref, qb_ref, kf_ref, vb_ref):
        ca, sa1, sa2, cb, sb1, sb2, ck = _tabs(tab_ref)
        hv = h_ref[...]
        u = (hv * _rms_r(hv, D_MODEL) * g1_ref[...]).astype(BF16)
        u_ref[...] = u
        p = _dot_nt(u, win_ref[...])
        for c in range(SWA_HEADS):
            sl = slice(LANE * c, LANE * (c + 1))
            qa_ref[:, sl] = _rope(p[:, PO_QA + LANE * c:PO_QA + LANE * (c + 1)], ca, sa1, sa2, 32).astype(BF16)
        for c in range(SWA_KV_HEADS):
            sl = slice(LANE * c, LANE * (c + 1))
            ka_ref[:, sl] = _rope(p[:, PO_KA + LANE * c:PO_KA + LANE * (c + 1)], ca, sa1, sa2, 32).astype(BF16)
        va_ref[...] = p[:, PO_VA:PO_CQ].astype(BF16)
        cq = p[:, PO_CQ:PO_CKV]
        ckv = p[:, PO_CKV:PO_KR]
        cq_ref[...] = cq
        ckv_ref[...] = ckv
        qn = (cq * _rms_r(cq, MLA_Q_RANK) * gq_ref[...]).astype(BF16)
        qn_ref[...] = qn
        qb = _dot_nt(qn, wqu_ref[...])
        kvn = (ckv * _rms_r(ckv, MLA_KV_RANK) * gkv_ref[...]).astype(BF16)
        kvn_ref[...] = kvn
        kv = _dot_nt(kvn, wkv_ref[...])
        kr = _rope(p[:, PO_KR:PW_IN], ck, sb1, sb2, 16)
        for c in range(MLA_HEADS):
            sl = slice(LANE * c, LANE * (c + 1))
            qb_ref[:, sl] = (_rope(qb[:, sl], cb, sb1, sb2, 16) * Q_SCALE).astype(BF16)
            kf_ref[:, sl] = (kv[:, sl] + kr).astype(BF16)
        vb_ref[...] = kv[:, HP:].astype(BF16)

    widths = [(D_MODEL, BF16), (HP, BF16), (2 * LANE, BF16), (2 * LANE, BF16), (MLA_Q_RANK, F32),
              (MLA_KV_RANK, F32), (MLA_Q_RANK, BF16), (MLA_KV_RANK, BF16), (HP, BF16), (HP, BF16), (HP, BF16)]
    return pl.pallas_call(
        body, name="pre_fwd", grid=(t // tm,),
        in_specs=[_row(tm, D_MODEL), _const(g1.shape), _const(win.shape), _const(gq.shape), _const(wqu.shape),
                  _const(gkv.shape), _const(wkv.shape), _row(tm, N_TAB * LANE)],
        out_specs=[_row(tm, w) for w, _ in widths],
        out_shape=[jax.ShapeDtypeStruct((t, w), d) for w, d in widths],
        compiler_params=_params("parallel"),
    )(h, g1, win, gq, wqu, gkv, wkv, tabs)


def _swa_mask(nb):
    key = lax.broadcasted_iota(jnp.int32, (2 * BLOCK, SWA_GROUP * BLOCK), 0)
    qry = lax.broadcasted_iota(jnp.int32, (2 * BLOCK, SWA_GROUP * BLOCK), 1) & (BLOCK - 1)
    return (key > qry) & (key <= qry + BLOCK) & (key + (nb - 1) * BLOCK >= FRONT)


def _swa_group(ref, rows, j):
    return jnp.concatenate([ref[rows, LANE * (SWA_GROUP * j + g):LANE * (SWA_GROUP * j + g + 1)]
                            for g in range(SWA_GROUP)], axis=0)


def _swa_packed_group(ref, rows, j):
    heads = [SWA_GROUP * j + g for g in range(SWA_GROUP)]
    return jnp.concatenate([_pair_half(ref[rows, LANE * (hd // 2):LANE * (hd // 2 + 1)], hd % 2) for hd in heads], axis=0)


def _swa_sinks(sink_ref, j):
    return jnp.concatenate([jnp.full((1, BLOCK), sink_ref[0, SWA_GROUP * j + g], F32) for g in range(SWA_GROUP)], axis=1)


def _swa_keys(prev_ref, cur_ref, rb, j):
    sl = slice(LANE * j, LANE * (j + 1))
    if rb == 0:
        return jnp.concatenate([prev_ref[:, sl], cur_ref[:BLOCK, sl]], axis=0)
    return cur_ref[BLOCK * (rb - 1):BLOCK * (rb + 1), sl]


def _swa_chains(t):
    return [(rb, j) for rb in range(_tile(t) // BLOCK) for j in range(SWA_KV_HEADS)]


def _swa_scores(sink_ref, q_ref, kp_ref, kc_ref, n, t):
    r = _tile(t) // BLOCK
    chains = _swa_chains(t)
    qs = [_swa_group(q_ref, slice(BLOCK * rb, BLOCK * (rb + 1)), j) for rb, j in chains]
    ks = [_swa_keys(kp_ref, kc_ref, rb, j) for rb, j in chains]
    ss = [_dot_nt(k2, q4) for q4, k2 in zip(qs, ks)]
    masks = [_swa_mask(n * r + rb) for rb in range(r)]
    out = []
    for (rb, j), s in zip(chains, ss):
        sink = _swa_sinks(sink_ref, j)
        s = jnp.where(masks[rb], s * SCALE_A, NEG)
        m = jnp.maximum(jnp.max(s, axis=0, keepdims=True), sink)
        e = jnp.exp(s - m)
        es = jnp.exp(sink - m)
        inv = 1.0 / (jnp.sum(e, axis=0, keepdims=True) + es)
        out.append((e * inv, es * inv))
    return qs, ks, out


def _swa_specs(t):
    ts = _tile(t)
    r = ts // BLOCK
    prev = lambda n: (jnp.maximum(n * r - 1, 0), 0)
    cur = lambda n: (n, 0)
    return [pl.BlockSpec(memory_space=pltpu.SMEM), pl.BlockSpec((ts, HP), cur),
            pl.BlockSpec((BLOCK, 2 * LANE), prev), pl.BlockSpec((ts, 2 * LANE), cur),
            pl.BlockSpec((BLOCK, 2 * LANE), prev), pl.BlockSpec((ts, 2 * LANE), cur)]


def _swa_fwd(sinks, q, k, v):
    t = q.shape[0]
    ts = _tile(t)

    def body(sink_ref, q_ref, kp_ref, kc_ref, vp_ref, vc_ref, o_ref):
        chains = _swa_chains(t)
        _, _, probs = _swa_scores(sink_ref, q_ref, kp_ref, kc_ref, pl.program_id(0), t)
        os_ = [_dot_tn(_swa_keys(vp_ref, vc_ref, rb, j)[:, :HALF], p.astype(BF16)) for (rb, j), (p, _) in zip(chains, probs)]
        for (rb, j), o4 in zip(chains, os_):
            for g in range(0, SWA_GROUP, 2):
                pair = (SWA_GROUP * j + g) // 2
                o_ref[BLOCK * rb:BLOCK * (rb + 1), LANE * pair:LANE * (pair + 1)] = jnp.concatenate(
                    [o4[:, BLOCK * g:BLOCK * (g + 1)], o4[:, BLOCK * (g + 1):BLOCK * (g + 2)]], axis=0).T

    return pl.pallas_call(
        body, name="swa_fwd", grid=(t // ts,),
        in_specs=_swa_specs(t),
        out_specs=pl.BlockSpec((ts, SWA_Q_W), lambda n: (n, 0)),
        out_shape=jax.ShapeDtypeStruct((t, SWA_Q_W), F32),
        compiler_params=_params("parallel"),
    )(sinks, q, k, k, v, v)


def _causal_mask(q0, k0, tq, tk, transposed):
    if transposed:
        key = k0 + lax.broadcasted_iota(jnp.int32, (tk, tq), 0)
        qry = q0 + lax.broadcasted_iota(jnp.int32, (tk, tq), 1)
    else:
        qry = q0 + lax.broadcasted_iota(jnp.int32, (tq, tk), 0)
        key = k0 + lax.broadcasted_iota(jnp.int32, (tq, tk), 1)
    return (key <= qry) & (key >= FRONT)


def _heads(ref, hb, rows=slice(None)):
    return [ref[rows, LANE * a:LANE * (a + 1)] for a in range(hb)]


def _head_stats(t, hb=MLA_HB):
    return jax.ShapeDtypeStruct((MLA_HEADS // hb, t, hb), F32)


def _mla_fwd(q, k, v, shards=()):
    t = q.shape[0]
    tq = _tile(t)
    nq = t // tq
    n = len(shards)
    hb = MLA_HB_FWD
    steps = (MLA_HEADS // hb) * nq

    def body(q_ref, k_ref, v_ref, *rest):
        x_refs, (o_ref, lse_ref), out_refs = rest[:n], rest[n:n + 2], rest[n + 2:2 * n + 2]
        acc_sc, sems = rest[2 * n + 2], rest[2 * n + 3:]
        i = pl.program_id(1)
        step_id = pl.program_id(0) * nq + i
        if n:
            plan = _gather_plan(x_refs, out_refs, *sems)
            pl.when(step_id == 0)(plan.start)
            pl.when(step_id == (3 * steps) // 4)(plan.forward)
        qs = _heads(q_ref, hb)
        acc_sc[...] = jnp.zeros(acc_sc.shape, F32)

        def step(j, carry, masked):
            rows = pl.ds(pl.multiple_of(j * tq, tq), tq)
            ks = _heads(k_ref, hb, rows)
            vs = [v_ref[rows, LANE * a:LANE * a + HALF] for a in range(hb)]
            ss = [_dot_nt(kh, qh) for qh, kh in zip(qs, ks)]
            if masked:
                mask = _causal_mask(i * tq, j * tq, tq, tq, True)
                ss = [jnp.where(mask, s, NEG) for s in ss]
            mid, out = [], []
            for s, (m, l) in zip(ss, carry):
                mn = jnp.maximum(m, jnp.max(s, axis=0, keepdims=True))
                al = jnp.exp2(m - mn)
                p = jnp.exp2(s - mn)
                out.append((mn, al * l + jnp.sum(p, axis=0, keepdims=True)))
                mid.append((al, p.astype(BF16)))
            for a, ((al, p), vh) in enumerate(zip(mid, vs)):
                acc_sc[a] = al * acc_sc[a] + _dot_tn(vh, p)
            return tuple(out)

        init = ((jnp.full((1, tq), NEG, F32), jnp.zeros((1, tq), F32)),) * hb
        carry = lax.fori_loop(0, jnp.minimum(i, 1) + 1, lambda it, c: step(it * i, c, True), init)
        carry = lax.fori_loop(1, i, lambda j, c: step(j, c, False), carry)
        outs = [acc_sc[a] * (1.0 / l) for a, (_, l) in enumerate(carry)]
        for a in range(0, hb, 2):
            o_ref[:, HALF * a:HALF * (a + 2)] = jnp.concatenate(outs[a:a + 2], axis=0).T
        for a, (m, l) in enumerate(carry):
            lse_ref[:, a:a + 1] = jnp.broadcast_to(m + jnp.log2(l), (LANE, tq)).T[:, :1]
        if n:
            pl.when(step_id == steps - 1)(plan.finish)

    blk = pl.BlockSpec((tq, hb * LANE), lambda h, i: (i, h))
    full = pl.BlockSpec((t, hb * LANE), lambda h, i: (0, h))
    packed = pl.BlockSpec((tq, hb * HALF), lambda h, i: (i, h))
    out = pl.pallas_call(
        body, name="mla_fwd_gather" if n else "mla_fwd", grid=(MLA_HEADS // hb, nq),
        in_specs=[blk, full, full] + [ANY] * n,
        out_specs=[packed, pl.BlockSpec((None, tq, hb), lambda h, i: (h, i, 0))] + [ANY] * n,
        out_shape=[jax.ShapeDtypeStruct((t, MLA_OUT_W), F32), _head_stats(t, hb)]
        + [jax.ShapeDtypeStruct((N_DEV,) + a.shape, a.dtype) for a in shards],
        scratch_shapes=[pltpu.VMEM((hb, HALF, tq), F32)] + (_comm_sems(n) if n else []),
        compiler_params=_params("arbitrary", "arbitrary"),
    )(q, k, v, *shards)
    return out[0], out[1], out[2:]


def _mix_fwd(h, oa, ob, ga, gb, wo, g2):
    t = h.shape[0]
    tm = _tile(t)

    def body(h_ref, oa_ref, ob_ref, ga_ref, gb_ref, wo_ref, g2_ref, h2_ref, mix_ref, u2_ref):
        oa_v = oa_ref[...]
        ob_v = ob_ref[...]
        na = (oa_v * _rms_r(oa_v, SWA_Q_W) * ga_ref[...]).astype(BF16)
        nb = (ob_v * _rms_r(ob_v, MLA_OUT_W) * gb_ref[...]).astype(BF16)
        mix_ref[:, :SWA_Q_W] = na
        mix_ref[:, SWA_Q_W:] = nb
        h2 = h_ref[...] + _dot(na, wo_ref[:SWA_Q_W, :]) + _dot(nb, wo_ref[SWA_Q_W:, :])
        h2_ref[...] = h2
        u2_ref[...] = (h2 * _rms_r(h2, D_MODEL) * g2_ref[...]).astype(BF16)

    mix_w = SWA_Q_W + MLA_OUT_W
    return pl.pallas_call(
        body, name="mix_fwd", grid=(t // tm,),
        in_specs=[_row(tm, D_MODEL), _row(tm, SWA_Q_W), _row(tm, MLA_OUT_W), _const(ga.shape), _const(gb.shape),
                  _const(wo.shape), _const(g2.shape)],
        out_specs=[_row(tm, D_MODEL), _row(tm, mix_w), _row(tm, D_MODEL)],
        out_shape=[jax.ShapeDtypeStruct((t, D_MODEL), F32), jax.ShapeDtypeStruct((t, mix_w), BF16),
                   jax.ShapeDtypeStruct((t, D_MODEL), BF16)],
        compiler_params=_params("parallel"),
    )(h, oa, ob, ga, gb, wo, g2)


def _ffn_fwd(h2, u2, wg_t, wu_t, wd, shards=()):
    t = h2.shape[0]
    tm = _tile(t)
    dff = wd.shape[0]
    n = len(shards)
    steps = t // tm

    def body(h2_ref, u2_ref, wg_ref, wu_ref, wd_ref, *rest):
        x_refs, (h3_ref, g_ref, up_ref), out_refs, sems = rest[:n], rest[n:n + 3], rest[n + 3:2 * n + 3], rest[2 * n + 3:]
        if n:
            plan = _gather_plan(x_refs, out_refs, *sems)
            pl.when(pl.program_id(0) == 0)(plan.start)
            pl.when(pl.program_id(0) == (3 * steps) // 4)(plan.forward)
        u2v = u2_ref[...]
        g = _dot_nt(u2v, wg_ref[...])
        up = _dot_nt(u2v, wu_ref[...])
        g_ref[...] = g.astype(BF16)
        up_ref[...] = up.astype(BF16)
        a = (g * jax.nn.sigmoid(g) * up).astype(BF16)
        h3_ref[...] = h2_ref[...] + _dot(a, wd_ref[...])
        if n:
            pl.when(pl.program_id(0) == steps - 1)(plan.finish)

    out = pl.pallas_call(
        body, name="ffn_fwd_gather" if n else "ffn_fwd", grid=(steps,),
        in_specs=[_row(tm, D_MODEL), _row(tm, D_MODEL), _const(wg_t.shape), _const(wu_t.shape), _const(wd.shape)] + [ANY] * n,
        out_specs=[_row(tm, D_MODEL), _row(tm, dff), _row(tm, dff)] + [ANY] * n,
        out_shape=[jax.ShapeDtypeStruct((t, D_MODEL), F32), jax.ShapeDtypeStruct((t, dff), BF16),
                   jax.ShapeDtypeStruct((t, dff), BF16)] + [jax.ShapeDtypeStruct((N_DEV,) + a.shape, a.dtype) for a in shards],
        scratch_shapes=_comm_sems(n) if n else [],
        compiler_params=_params("arbitrary" if n else "parallel"),
    )(h2, u2, wg_t, wu_t, wd, *shards)
    return out[:3], out[3:]


def _loss_bwd(h, gf, target):
    t = h.shape[0]
    tm = _tile(t)
    r = tm // BLOCK
    assert FRONT + N_META == BLOCK and target.shape[0] == t - BLOCK

    def body(h_ref, gf_ref, *rest):
        t_refs, (dh_ref, dgf_ref, loss_ref) = rest[:r], rest[r:]
        i = pl.program_id(0)
        hv = h_ref[...]
        y = hv * _rms_r(hv, D_MODEL) * gf_ref[...]
        row = i * tm + lax.broadcasted_iota(jnp.int32, (tm, 1), 0)
        tv = jnp.concatenate([t_ref[...] for t_ref in t_refs], axis=0)
        err = jnp.where(row >= BLOCK, y - tv, 0.0)
        dx, dg = _rms_bwd(hv, gf_ref[...], err * (1.0 / D_MODEL), D_MODEL)
        dh_ref[...] = dx
        _acc(dgf_ref, dg, i == 0)
        part = 0.5 * jnp.sum(jnp.sum(err * err, axis=1, keepdims=True) * (1.0 / D_MODEL), axis=0, keepdims=True)
        _acc(loss_ref, jnp.broadcast_to(part, (1, LANE)), i == 0)

    t_specs = [pl.BlockSpec((BLOCK, D_MODEL), lambda i, b=b: (jnp.maximum(r * i + b - 1, 0), 0)) for b in range(r)]
    return pl.pallas_call(
        body, name="loss_bwd", grid=(t // tm,),
        in_specs=[_row(tm, D_MODEL), _const(gf.shape)] + t_specs,
        out_specs=[_row(tm, D_MODEL), _const((1, D_MODEL)), _const((1, LANE))],
        out_shape=[jax.ShapeDtypeStruct((t, D_MODEL), F32), jax.ShapeDtypeStruct((1, D_MODEL), F32),
                   jax.ShapeDtypeStruct((1, LANE), F32)],
        compiler_params=_params("arbitrary"),
    )(h, gf, *[target] * r)


def _tn_matmul(a, b, name, cols=None, keep=None):
    t, n = b.shape
    first, k = cols or (0, a.shape[1])
    tk = next(c for c in (k, 1024, 512, 256, 128) if k % c == 0 and first % c == 0 and c <= 1024)
    fits = lambda c: 2 * (t * (tk + c) * 2 + tk * c * 2) <= TN_VMEM_BUDGET
    tn = next(c for c in (n, 1024, 512, 256, 128) if n % c == 0 and fits(c))
    kept = tk if keep is None else sum(size for _, size in keep)

    def body(a_ref, b_ref, o_ref):
        if keep is None:
            o_ref[...] = _dot_tn(a_ref[...], b_ref[...]).astype(BF16)
        else:
            at = a_ref[...].T
            at = jnp.concatenate([at[start:start + size] for start, size in keep], axis=0)
            o_ref[...] = _dot(at, b_ref[...]).astype(BF16)

    return pl.pallas_call(
        body, name=name, grid=(k // tk, n // tn),
        in_specs=[pl.BlockSpec((t, tk), lambda i, j: (0, i + first // tk)), pl.BlockSpec((t, tn), lambda i, j: (0, j))],
        out_specs=pl.BlockSpec((kept, tn), lambda i, j: (i, j)),
        out_shape=jax.ShapeDtypeStruct((k // tk * kept, n), BF16),
        compiler_params=_params("parallel", "parallel"),
    )(a, b)


def _ffn_bwd_a(dh3, g, up, wd):
    t = dh3.shape[0]
    tm = _tile(t)
    dff = wd.shape[0]

    def body(dh3_ref, g_ref, up_ref, wd_ref, a_ref, dgu_ref, dh3b_ref):
        dh3b = dh3_ref[...].astype(BF16)
        dh3b_ref[...] = dh3b
        da = _dot_nt(dh3b, wd_ref[...])
        gv = g_ref[...].astype(F32)
        upv = up_ref[...].astype(F32)
        sg = jax.nn.sigmoid(gv)
        silu = gv * sg
        a_ref[...] = (silu * upv).astype(BF16)
        dgu_ref[:, :dff] = (da * upv * (sg * (1.0 + gv * (1.0 - sg)))).astype(BF16)
        dgu_ref[:, dff:] = (da * silu).astype(BF16)

    return pl.pallas_call(
        body, name="ffn_bwd_a", grid=(t // tm,),
        in_specs=[_row(tm, D_MODEL), _row(tm, dff), _row(tm, dff), _const(wd.shape)],
        out_specs=[_row(tm, dff), _row(tm, 2 * dff), _row(tm, D_MODEL)],
        out_shape=[jax.ShapeDtypeStruct((t, dff), BF16), jax.ShapeDtypeStruct((t, 2 * dff), BF16),
                   jax.ShapeDtypeStruct((t, D_MODEL), BF16)],
        compiler_params=_params("parallel"),
    )(dh3, g, up, wd)


def _ffn_bwd_b(dh3, dgu, h2, g2, wg_t, wu_t):
    t = dh3.shape[0]
    tm = _tile(t)
    dff = wg_t.shape[0]

    def body(dh3_ref, dgu_ref, h2_ref, g2_ref, wg_ref, wu_ref, dh2_ref, dh2b_ref, dg2_ref):
        du2 = _dot(dgu_ref[:, :dff], wg_ref[...]) + _dot(dgu_ref[:, dff:], wu_ref[...])
        dx, dg = _rms_bwd(h2_ref[...], g2_ref[...], du2, D_MODEL)
        dh2 = dh3_ref[...] + dx
        dh2_ref[...] = dh2
        dh2b_ref[...] = dh2.astype(BF16)
        _acc(dg2_ref, dg, pl.program_id(0) == 0)

    return pl.pallas_call(
        body, name="ffn_bwd_b", grid=(t // tm,),
        in_specs=[_row(tm, D_MODEL), _row(tm, 2 * dff), _row(tm, D_MODEL), _const(g2.shape), _const(wg_t.shape),
                  _const(wu_t.shape)],
        out_specs=[_row(tm, D_MODEL), _row(tm, D_MODEL), _const((1, D_MODEL))],
        out_shape=[jax.ShapeDtypeStruct((t, D_MODEL), F32), jax.ShapeDtypeStruct((t, D_MODEL), BF16),
                   jax.ShapeDtypeStruct((1, D_MODEL), F32)],
        compiler_params=_params("arbitrary"),
    )(dh3, dgu, h2, g2, wg_t, wu_t)


def _mix_bwd(dh2, oa, ob, ga, gb, wo):
    t = dh2.shape[0]
    tm = _tile(t)

    def body(dh2_ref, oa_ref, ob_ref, ga_ref, gb_ref, wo_ref, doa_ref, dob_ref, dl_ref, dga_ref, dgb_ref):
        first = pl.program_id(0) == 0
        d = dh2_ref[...]
        ob_v = ob_ref[...]
        dxa, dga = _rms_bwd(oa_ref[...], ga_ref[...], _dot_nt(d, wo_ref[:SWA_Q_W, :]), SWA_Q_W)
        dxb, dgb = _rms_bwd(ob_v, gb_ref[...], _dot_nt(d, wo_ref[SWA_Q_W:, :]), MLA_OUT_W)
        lower = lax.broadcasted_iota(jnp.int32, (tm, LANE), 1) < HALF
        for hd in range(MLA_HEADS):
            sl = slice(LANE * (hd // 2), LANE * (hd // 2 + 1))
            mine = lower if hd % 2 == 0 else jnp.logical_not(lower)
            delta = jnp.sum(jnp.where(mine, ob_v[:, sl] * dxb[:, sl], 0.0), axis=1, keepdims=True)
            dl_ref[hd // MLA_HB, :, hd % MLA_HB:hd % MLA_HB + 1] = delta
        for ref, dx, heads in ((doa_ref, dxa, SWA_HEADS), (dob_ref, dxb, MLA_HEADS)):
            for hd in range(heads):
                slab = dx[:, LANE * (hd // 2):LANE * (hd // 2 + 1)]
                ref[:, LANE * hd:LANE * (hd + 1)] = _unpack_pair(slab, hd % 2).astype(BF16)
        _acc(dga_ref, dga, first)
        _acc(dgb_ref, dgb, first)

    return pl.pallas_call(
        body, name="mix_bwd", grid=(t // tm,),
        in_specs=[_row(tm, D_MODEL), _row(tm, SWA_Q_W), _row(tm, MLA_OUT_W), _const(ga.shape), _const(gb.shape),
                  _const(wo.shape)],
        out_specs=[_row(tm, HP), _row(tm, HP), pl.BlockSpec((MLA_HEADS // MLA_HB, tm, MLA_HB), lambda i: (0, i, 0)),
                   _const((1, SWA_Q_W)), _const((1, MLA_OUT_W))],
        out_shape=[jax.ShapeDtypeStruct((t, HP), BF16), jax.ShapeDtypeStruct((t, HP), BF16), _head_stats(t),
                   jax.ShapeDtypeStruct((1, SWA_Q_W), F32), jax.ShapeDtypeStruct((1, MLA_OUT_W), F32)],
        compiler_params=_params("arbitrary"),
    )(dh2, oa, ob, ga, gb, wo)


def _swa_bwd(sinks, q, k, v, o, do):
    t = q.shape[0]
    ts = _tile(t)

    def body(sink_ref, q_ref, kp_ref, kc_ref, vp_ref, vc_ref, o_ref, do_ref,
             dq_ref, dkc_ref, dkp_ref, dvc_ref, dvp_ref, dsink_ref):
        n = pl.program_id(0)
        chains = _swa_chains(t)
        qs, ks, probs = _swa_scores(sink_ref, q_ref, kp_ref, kc_ref, n, t)
        dos = [_swa_group(do_ref, slice(BLOCK * rb, BLOCK * (rb + 1)), j) for rb, j in chains]
        vs = [_swa_keys(vp_ref, vc_ref, rb, j) for rb, j in chains]
        dps = [_dot_nt(v2, do4) for do4, v2 in zip(dos, vs)]
        dss, dsks = [], []
        for (rb, j), (p, psink), do4, dp in zip(chains, probs, dos, dps):
            o4 = _swa_packed_group(o_ref, slice(BLOCK * rb, BLOCK * (rb + 1)), j)
            delta = jnp.sum(o4 * do4.astype(F32), axis=1, keepdims=True)
            delta = jnp.broadcast_to(delta, (SWA_GROUP * BLOCK, LANE)).T[:1, :]
            dss.append((p * (dp - delta) * SCALE_A).astype(BF16))
            dsks.append(-psink * delta)
        dqs = [_dot_tn(k2[:, :HALF], ds) for ds, k2 in zip(dss, ks)]
        dks = [_dot(ds, q4) for ds, q4 in zip(dss, qs)]
        dvs = [_dot(p.astype(BF16), do4) for (p, _), do4 in zip(probs, dos)]
        dsink = [jnp.zeros((1, LANE), F32)] * SWA_HEADS
        ext = {}
        for (rb, j), dq4, dk2, dv2, dsk in zip(chains, dqs, dks, dvs, dsks):
            for g in range(SWA_GROUP):
                hd = SWA_GROUP * j + g
                cols = slice(BLOCK * g, BLOCK * (g + 1))
                dq_ref[BLOCK * rb:BLOCK * (rb + 1), LANE * hd:LANE * (hd + 1)] = jnp.concatenate(
                    [dq4[:, cols], jnp.zeros((HALF, BLOCK), F32)], axis=0).T.astype(BF16)
                dsink[hd] = dsink[hd] + jnp.sum(dsk[:, cols], axis=1, keepdims=True)
            for half in range(2):
                key = (j, rb + half)
                part = (dk2[BLOCK * half:BLOCK * (half + 1)], dv2[BLOCK * half:BLOCK * (half + 1)])
                ext[key] = part if key not in ext else (ext[key][0] + part[0], ext[key][1] + part[1])
        for (j, blk), (dk, dv) in ext.items():
            sl = slice(LANE * j, LANE * (j + 1))
            if blk == 0:
                dkp_ref[:, sl] = dk
                dvp_ref[:, sl] = dv
            else:
                dkc_ref[BLOCK * (blk - 1):BLOCK * blk, sl] = dk
                dvc_ref[BLOCK * (blk - 1):BLOCK * blk, sl] = dv
        for hd in range(SWA_HEADS):
            _acc(dsink_ref.at[hd:hd + 1, :], jnp.broadcast_to(dsink[hd], (1, LANE)), n == 0)

    cur = lambda n: (n, 0)
    kv = pl.BlockSpec((ts, 2 * LANE), cur)
    kvp = pl.BlockSpec((BLOCK, 2 * LANE), cur)
    hp = pl.BlockSpec((ts, HP), cur)
    kvs = jax.ShapeDtypeStruct((t, 2 * LANE), F32)
    kvps = jax.ShapeDtypeStruct((t // ts * BLOCK, 2 * LANE), F32)
    return pl.pallas_call(
        body, name="swa_bwd", grid=(t // ts,),
        in_specs=_swa_specs(t) + [pl.BlockSpec((ts, SWA_Q_W), cur), hp],
        out_specs=[hp, kv, kvp, kv, kvp, _const((SWA_HEADS, LANE))],
        out_shape=[jax.ShapeDtypeStruct((t, HP), BF16), kvs, kvps, kvs, kvps,
                   jax.ShapeDtypeStruct((SWA_HEADS, LANE), F32)],
        compiler_params=_params("arbitrary"),
    )(sinks, q, k, k, v, v, o, do)


def _mla_bwd(q, k, v, do, lse, dl, slabs=()):
    t = q.shape[0]
    tq = _tile(t)
    nq = t // tq
    n = len(slabs)
    hb = MLA_HB
    steps = (MLA_HEADS // hb) * nq

    def body(k_ref, v_ref, q_ref, do_ref, lse_ref, dl_ref, *rest):
        in_refs, (dq_ref, dk_ref, dv_ref), out_refs = rest[:n], rest[n:n + 3], rest[n + 3:2 * n + 3]
        (dq_sc, dk_sc, dv_sc), sems = rest[2 * n + 3:2 * n + 6], rest[2 * n + 6:]
        j = pl.program_id(1)
        step_id = pl.program_id(0) * nq + j
        if n:
            plan = _exchange_plan(in_refs, out_refs, *sems)
            pl.when(step_id == 0)(plan.start)

        @pl.when(j == 0)
        def _():
            dq_sc[...] = jnp.zeros(dq_sc.shape, F32)

        dk_sc[...] = jnp.zeros(dk_sc.shape, F32)
        dv_sc[...] = jnp.zeros(dv_sc.shape, F32)
        ks, vs = _heads(k_ref, hb), _heads(v_ref, hb)

        def step(i, carry, masked):
            rows = pl.ds(pl.multiple_of(i * tq, tq), tq)
            qs, dos = _heads(q_ref, hb, rows), _heads(do_ref, hb, rows)
            ss = [_dot_nt(qh, kh) for qh, kh in zip(qs, ks)]
            dps = [_dot_nt(doh, vh) for doh, vh in zip(dos, vs)]
            if masked:
                mask = _causal_mask(i * tq, j * tq, tq, tq, False)
                ss = [jnp.where(mask, s_, NEG) for s_ in ss]
            ps = [jnp.exp2(s_ - lse_ref[rows, a:a + 1]) for a, s_ in enumerate(ss)]
            dss = [(p * (dp - dl_ref[rows, a:a + 1])).astype(BF16) for a, (p, dp) in enumerate(zip(ps, dps))]
            for a, (ds, p, qh, kh, doh) in enumerate(zip(dss, ps, qs, ks, dos)):
                dq_sc[a, rows, :] += _dot(ds, kh)
                dk_sc[a, :MLA_QK_DIM, :] += _dot_tn(qh[:, :MLA_QK_DIM], ds)
                dv_sc[a, :MLA_V_DIM, :] += _dot_tn(doh[:, :MLA_V_DIM], p.astype(BF16))
            return carry

        split = jnp.where(j == 0, nq, j + 1)
        lax.fori_loop(j, split, lambda i, c: step(i, c, True), 0)
        lax.fori_loop(split, nq, lambda i, c: step(i, c, False), 0)
        for a in range(hb):
            dk_ref[:, LANE * a:LANE * (a + 1)] = (dk_sc[a] * (1.0 / LOG2E)).T.astype(BF16)
            dv_ref[:, LANE * a:LANE * (a + 1)] = dv_sc[a].T.astype(BF16)

        @pl.when(j == nq - 1)
        def _():
            for a in range(hb):
                dq_ref[:, LANE * a:LANE * (a + 1)] = (dq_sc[a] * SCALE_B).astype(BF16)

        if n:
            pl.when(step_id == steps - 1)(plan.finish)

    blk = pl.BlockSpec((tq, hb * LANE), lambda h, j: (j, h))
    full = pl.BlockSpec((t, hb * LANE), lambda h, j: (0, h))
    cols = pl.BlockSpec((None, t, hb), lambda h, j: (h, 0, 0))
    out = pl.pallas_call(
        body, name="mla_bwd_exchange" if n else "mla_bwd", grid=(MLA_HEADS // hb, nq),
        in_specs=[blk, blk, full, full, cols, cols] + [ANY] * n, out_specs=[full, blk, blk] + [ANY] * n,
        out_shape=[jax.ShapeDtypeStruct((t, HP), BF16)] * 3 + [jax.ShapeDtypeStruct(a.shape, a.dtype) for a in slabs],
        scratch_shapes=[pltpu.VMEM((hb, t, LANE), F32)] + [pltpu.VMEM((hb, LANE, tq), F32)] * 2
        + (_comm_sems(n) if n else []),
        compiler_params=_params("arbitrary", "arbitrary"),
    )(k, v, q, do, lse, dl, *slabs)
    return out[:3], out[3:]


def _pre_bwd(dh2, h, cq, ckv, dqa, dka, dka_next, dva, dva_next, dqb, dkf, dvb, g1, win, gq, wqu, gkv, wkv, tabs, u, qn, kvn):
    t = h.shape[0]
    tm = _tile(t)
    keep = IN_KEEP_SWA + [(PO_KA + start, size) for start, size in IN_KEEP_REST]
    kept = sum(size for _, size in keep)

    def body(dh2_ref, h_ref, cq_ref, ckv_ref, dqa_ref, dka_ref, dkan_ref, dva_ref, dvan_ref, dqb_ref, dkf_ref, dvb_ref,
             g1_ref, win_ref, gq_ref, wqu_ref, gkv_ref, wkv_ref, tab_ref, u_ref, qn_ref, kvn_ref,
             dh_ref, dwin_ref, dwqu_ref, dwkv_ref, dg1_ref, dgq_ref, dgkv_ref,
             dp_ref, dqbo_ref, dkvo_ref, ain_sc, aqu_sc, akv_sc):
        first = pl.program_id(0) == 0
        ca, sa1, sa2, cb, sb1, sb2, ck = _tabs(tab_ref)
        dkr = jnp.zeros((tm, LANE), F32)
        for c in range(MLA_HEADS):
            sl = slice(LANE * c, LANE * (c + 1))
            dqbo_ref[:, sl] = _rope_t(dqb_ref[:, sl].astype(F32), cb, sb1, sb2, 16).astype(BF16)
            dkr += dkf_ref[:, sl].astype(F32)
        dkvo_ref[:, :HP] = dkf_ref[...]
        dkvo_ref[:, HP:] = dvb_ref[...]
        dcq, dgq = _rms_bwd(cq_ref[...], gq_ref[...], _dot(dqbo_ref[...], wqu_ref[...]), MLA_Q_RANK)
        dckv, dgkv = _rms_bwd(ckv_ref[...], gkv_ref[...], _dot(dkvo_ref[...], wkv_ref[...]), MLA_KV_RANK)
        for c in range(SWA_HEADS):
            sl = slice(LANE * c, LANE * (c + 1))
            dp_ref[:, PO_QA + LANE * c:PO_QA + LANE * (c + 1)] = _rope_t(dqa_ref[:, sl].astype(F32), ca, sa1, sa2,
                                                                          32).astype(BF16)
        last = slice(tm - BLOCK, tm)
        more = pl.program_id(0) < t // tm - 1
        for c in range(SWA_KV_HEADS):
            sl = slice(LANE * c, LANE * (c + 1))
            dk = dka_ref[:, sl]
            dk_last = dk[tm - BLOCK:] + jnp.where(more, dkan_ref[:, sl], 0.0)
            cols = slice(PO_KA + LANE * c, PO_KA + LANE * (c + 1))
            if tm > BLOCK:
                dp_ref[:tm - BLOCK, cols] = _rope_t(dk[:tm - BLOCK], ca[:tm - BLOCK], sa1[:tm - BLOCK], sa2[:tm - BLOCK],
                                                    32).astype(BF16)
            dp_ref[last, cols] = _rope_t(dk_last, ca[tm - BLOCK:], sa1[tm - BLOCK:], sa2[tm - BLOCK:], 32).astype(BF16)
        if tm > BLOCK:
            dp_ref[:tm - BLOCK, PO_VA:PO_CQ] = dva_ref[:tm - BLOCK, :].astype(BF16)
        dp_ref[last, PO_VA:PO_CQ] = (dva_ref[tm - BLOCK:, :] + jnp.where(more, dvan_ref[...], 0.0)).astype(BF16)
        dp_ref[:, PO_CQ:PO_CKV] = dcq.astype(BF16)
        dp_ref[:, PO_CKV:PO_KR] = dckv.astype(BF16)
        dp_ref[:, PO_KR:PW_IN] = _rope_t(dkr, ck, sb1, sb2, 16).astype(BF16)
        dx, dg1 = _rms_bwd(h_ref[...], g1_ref[...], _dot(dp_ref[...], win_ref[...]), D_MODEL)
        dh_ref[...] = dh2_ref[...] + dx
        _acc(dg1_ref, dg1, first)
        _acc(dgq_ref, dgq, first)
        _acc(dgkv_ref, dgkv, first)
        dpt = dp_ref[...].T
        _acc(ain_sc, _dot(jnp.concatenate([dpt[start:start + size] for start, size in keep], axis=0), u_ref[...]), first)
        _acc(aqu_sc, _dot_tn(dqbo_ref[...], qn_ref[...]), first)
        _acc(akv_sc, _dot_tn(kvn_ref[...], dkvo_ref[...]), first)

        @pl.when(pl.program_id(0) == t // tm - 1)
        def _():
            dwin_ref[...] = ain_sc[...].astype(BF16)
            dwqu_ref[...] = aqu_sc[...].astype(BF16)
            dwkv_ref[...] = akv_sc[...].astype(BF16)

    kv = _row(tm, 2 * LANE)
    nxt = pl.BlockSpec((BLOCK, 2 * LANE), lambda i: (jnp.minimum(i + 1, t // tm - 1), 0))
    return pl.pallas_call(
        body, name="pre_bwd", grid=(t // tm,),
        in_specs=[_row(tm, D_MODEL), _row(tm, D_MODEL), _row(tm, MLA_Q_RANK), _row(tm, MLA_KV_RANK), _row(tm, HP),
                  kv, nxt, kv, nxt, _row(tm, HP), _row(tm, HP), _row(tm, HP),
                  _const(g1.shape), _const(win.shape), _const(gq.shape), _const(wqu.shape), _const(gkv.shape),
                  _const(wkv.shape), _row(tm, N_TAB * LANE), _row(tm, D_MODEL), _row(tm, MLA_Q_RANK), _row(tm, MLA_KV_RANK)],
        out_specs=[_row(tm, D_MODEL), _const((kept, D_MODEL)), _const((HP, MLA_Q_RANK)), _const((MLA_KV_RANK, 2 * HP)),
                   _const((1, D_MODEL)), _const((1, MLA_Q_RANK)), _const((1, MLA_KV_RANK))],
        out_shape=[jax.ShapeDtypeStruct((t, D_MODEL), F32), jax.ShapeDtypeStruct((kept, D_MODEL), BF16),
                   jax.ShapeDtypeStruct((HP, MLA_Q_RANK), BF16), jax.ShapeDtypeStruct((MLA_KV_RANK, 2 * HP), BF16),
                   jax.ShapeDtypeStruct((1, D_MODEL), F32), jax.ShapeDtypeStruct((1, MLA_Q_RANK), F32),
                   jax.ShapeDtypeStruct((1, MLA_KV_RANK), F32)],
        scratch_shapes=[pltpu.VMEM((tm, PW_IN), BF16), pltpu.VMEM((tm, HP), BF16), pltpu.VMEM((tm, 2 * HP), BF16),
                        pltpu.VMEM((kept, D_MODEL), F32), pltpu.VMEM((HP, MLA_Q_RANK), F32),
                        pltpu.VMEM((MLA_KV_RANK, 2 * HP), F32)],
        compiler_params=_params("arbitrary"),
    )(dh2, h, cq, ckv, dqa, dka, dka_next, dva, dva_next, dqb, dkf, dvb, g1, win, gq, wqu, gkv, wkv, tabs, u, qn, kvn)


def _rope_tables(t):
    pos = (jnp.arange(t, dtype=jnp.int32) - FRONT).astype(F32)[:, None]
    lane = jnp.arange(LANE)[None, :]

    def table(dim, start):
        half = dim // 2
        inv = ROPE_THETA ** (-jnp.arange(0, dim, 2, dtype=F32) / dim)
        ang = pos * inv[None, :]
        cos = jnp.concatenate([jnp.cos(ang)] * 2, axis=1)
        sin = jnp.concatenate([jnp.sin(ang)] * 2, axis=1)
        pad = lambda a: jnp.pad(a, ((0, 0), (start, LANE - start - dim)))
        first = (lane >= start) & (lane < start + half)
        second = (lane >= start + half) & (lane < start + dim)
        return pad(cos), jnp.where(first, -pad(sin), 0.0), jnp.where(second, pad(sin), 0.0)

    ca, sa1, sa2 = table(SWA_HEAD_DIM, 0)
    ck, sb1, sb2 = table(MLA_ROPE_DIM, MLA_NOPE_DIM)
    cb = jnp.where(lane < MLA_NOPE_DIM, 1.0, ck)
    return jnp.concatenate([ca, sa1, sa2, cb, sb1, sb2, ck], axis=1)


def _pad_heads(w, heads, dim, axis):
    shp = w.shape
    w = w.reshape(shp[:axis] + (heads, dim) + shp[axis + 1:])
    pad = [(0, 0)] * w.ndim
    pad[axis + 1] = (0, LANE - dim)
    return jnp.pad(w, pad).reshape(shp[:axis] + (heads * LANE,) + shp[axis + 1:])


def _unpad_heads(w, heads, dim, axis):
    shp = w.shape
    w = w.reshape(shp[:axis] + (heads, LANE) + shp[axis + 1:])
    w = lax.slice_in_dim(w, 0, dim, axis=axis + 1)
    return w.reshape(shp[:axis] + (heads * dim,) + shp[axis + 1:])


def _pad_layer(w_in, w_q_up, w_kv_up):
    o1 = SWA_Q_W
    o2 = o1 + SWA_KV_W
    o3 = o2 + SWA_KV_W
    o4 = o3 + MLA_Q_RANK
    o5 = o4 + MLA_KV_RANK
    kr = jnp.pad(w_in[o5:], ((MLA_NOPE_DIM, LANE - MLA_QK_DIM), (0, 0)))
    win = jnp.concatenate([
        _pad_heads(w_in[:o1], SWA_HEADS, SWA_HEAD_DIM, 0),
        _pad_heads(w_in[o1:o2], SWA_KV_HEADS, SWA_HEAD_DIM, 0),
        _pad_heads(w_in[o2:o3], SWA_KV_HEADS, SWA_HEAD_DIM, 0),
        w_in[o3:o5], kr], axis=0)
    wqu = _pad_heads(w_q_up, MLA_HEADS, MLA_QK_DIM, 0)
    kv = w_kv_up.reshape(MLA_HEADS, MLA_NOPE_DIM + MLA_V_DIM, MLA_KV_RANK)
    wkv = jnp.concatenate([
        _pad_heads(kv[:, :MLA_NOPE_DIM].reshape(-1, MLA_KV_RANK), MLA_HEADS, MLA_NOPE_DIM, 0),
        _pad_heads(kv[:, MLA_NOPE_DIM:].reshape(-1, MLA_KV_RANK), MLA_HEADS, MLA_V_DIM, 0)], axis=0)
    return win, wqu, wkv


IN_KEEP_SWA = [(LANE * hd, SWA_HEAD_DIM) for hd in range(SWA_HEADS)]
IN_KEEP_REST = ([(LANE * hd, SWA_HEAD_DIM) for hd in range(2 * SWA_KV_HEADS)] + [(PO_CQ - PO_KA, PO_KR - PO_CQ)]
                + [(PO_KR - PO_KA + MLA_NOPE_DIM, MLA_ROPE_DIM)])


def _unpad_layer(d_w_in, dwqu, dwkv):
    d_w_q_up = _unpad_heads(dwqu, MLA_HEADS, MLA_QK_DIM, 0)
    dk = _unpad_heads(dwkv[:, :HP], MLA_HEADS, MLA_NOPE_DIM, 1).reshape(MLA_KV_RANK, MLA_HEADS, MLA_NOPE_DIM)
    dv = _unpad_heads(dwkv[:, HP:], MLA_HEADS, MLA_V_DIM, 1).reshape(MLA_KV_RANK, MLA_HEADS, MLA_V_DIM)
    d_w_kv_up = jnp.concatenate([dk, dv], axis=2).reshape(MLA_KV_RANK, -1).T
    return d_w_in, d_w_q_up, d_w_kv_up


def _train_example(x, target, meta, vec, weights):
    s = x.shape[0]
    depth = vec["attn_norm"].shape[0]
    t = FRONT + N_META + s
    assert t % BLOCK == 0
    tabs = _rope_tables(t)
    h = jnp.concatenate([jnp.zeros((FRONT, D_MODEL), F32), meta, x], axis=0)
    row = lambda v: v[None, :]

    saved = []
    for l in range(depth):
        win, wqu, wkv = _pad_layer(*weights.attn_in(l))
        g1, gq, gkv, g2, ga, gb = (row(vec[n][l]) for n in ("attn_norm", "q_norm", "kv_norm", "ffn_norm",
                                                            "out_norm_swa", "out_norm_mla"))
        sk = row(vec["sinks"][l])
        u, qa, ka, va, cq, ckv, qn, kvn, qb, kf, vb = _pre_fwd(h, g1, win, gq, wqu, gkv, wkv, tabs)
        oa = _swa_fwd(sk, qa, ka, va)
        ob, lse = weights.mla_fwd(l, qb, kf, vb)
        lse = jnp.moveaxis(lse.reshape(t, MLA_HEADS // MLA_HB, MLA_HB), 1, 0)
        wo = weights.w_o(l)
        h2, mix, u2 = _mix_fwd(h, oa, ob, ga, gb, wo, g2)
        wg, wu, wd = weights.ffn(l)
        h3, gt, up = weights.ffn_fwd(l, h2, u2)
        saved.append((h, u, qa, ka, va, cq, ckv, qn, kvn, qb, kf, vb, oa, ob, lse, h2, mix, u2, gt, up,
                      win, wqu, wkv, wo, ga, gb, g1, gq, gkv, g2, sk, wg, wu, wd))
        h = h3

    dh, d_final, loss = _loss_bwd(h, row(vec["final_norm"]), target)

    grads = []
    for l in reversed(range(depth)):
        (h0, u, qa, ka, va, cq, ckv, qn, kvn, qb, kf, vb, oa, ob, lse, h2, mix, u2, gt, up,
         win, wqu, wkv, wo, ga, gb, g1, gq, gkv, g2, sk, wg, wu, wd) = saved[l]
        dff = wd.shape[0]
        act, dgu, dhb = _ffn_bwd_a(dh, gt, up, wd)
        weights.ffn_grads(l, _tn_matmul(dgu, u2, "dw_gate", (0, dff)), _tn_matmul(dgu, u2, "dw_up", (dff, dff)),
                          _tn_matmul(act, dhb, "dw_down"))
        dh2, dh2b, d_g2 = _ffn_bwd_b(dh, dgu, h2, g2, wg, wu)
        weights.attn_grads(l, w_o=_tn_matmul(mix, dh2b, "dw_o"))
        doa, dob, dl, d_ga, d_gb = _mix_bwd(dh2b, oa, ob, ga, gb, wo)
        dqa, dkc, dkp, dvc, dvp, dsink = _swa_bwd(sk, qa, ka, va, oa, doa)
        dqb, dkf, dvb = weights.mla_bwd(l, qb, kf, vb, dob, lse, dl)
        dh, d_win, d_wqu, d_wkv, d_g1, d_gq, d_gkv = _pre_bwd(
            dh2, h0, cq, ckv, dqa, dkc, dkp, dvc, dvp, dqb, dkf, dvb,
            g1, win, gq, wqu, gkv, wkv, tabs, u, qn, kvn)
        weights.attn_grads(l, **dict(zip(ATTN_IN, _unpad_layer(d_win, d_wqu, d_wkv))))
        grads.append(dict(attn_norm=d_g1[0], q_norm=d_gq[0], kv_norm=d_gkv[0], sinks=dsink[:, 0], out_norm_swa=d_ga[0],
                          out_norm_mla=d_gb[0], ffn_norm=d_g2[0]))
    grads = grads[::-1]
    stacked = {k: jnp.stack([g[k] for g in grads]) for k in grads[0]}
    stacked["final_norm"] = d_final[0]
    return loss[0, 0], dh[FRONT + N_META:], dh[FRONT:FRONT + N_META], stacked


MESH = pl.DeviceIdType.MESH
ANY = pl.BlockSpec(memory_space=pl.ANY)


def _place():
    return lax.axis_index("x"), lax.axis_index("y"), lax.axis_index("c")


def _index(x, y, c):
    return 4 * x + 2 * y + c


def _comm_sems(n):
    return [pltpu.SemaphoreType.DMA((n, N_DEV - 1)), pltpu.SemaphoreType.DMA((n, N_DEV - 1)),
            pltpu.SemaphoreType.DMA((n,))]


class _gather_plan:
    def __init__(self, x_refs, out_refs, send_sems, recv_sems, local_sems):
        self.x_refs, self.out_refs = x_refs, out_refs
        self.send_sems, self.recv_sems, self.local_sems = send_sems, recv_sems, local_sems
        self.n = len(x_refs)

    def _where(self):
        x, y, c = _place()
        return (x, y, c), (x, y, 1 - c), [(1 - x, y), (x, 1 - y), (1 - x, 1 - y)], c

    def _copy(self, i, k, block, to, from_input=False):
        slot = self.out_refs[i].at[_index(*block)]
        return pltpu.make_async_remote_copy(
            src_ref=self.x_refs[i] if from_input else slot, dst_ref=slot,
            send_sem=self.send_sems.at[i, k], recv_sem=self.recv_sems.at[i, k], device_id=to, device_id_type=MESH)

    def _mine(self, i, me):
        return pltpu.make_async_copy(self.x_refs[i], self.out_refs[i].at[_index(*me)], self.local_sems.at[i])

    def _first(self, me, sibling, chips, c):
        out = [self._copy(i, 1 + j, me, (*chip, c), True) for j, chip in enumerate(chips) for i in range(self.n)]
        return out + [self._copy(i, 0, me, sibling, True) for i in range(self.n)]

    def start(self):
        me, sibling, chips, c = self._where()
        for i in range(self.n):
            self._mine(i, me).start()
        for cp in self._first(me, sibling, chips, c):
            cp.start()

    def forward(self):
        me, sibling, chips, c = self._where()
        for j, chip in enumerate(chips):
            for i in range(self.n):
                self._copy(i, 1 + j, (*chip, c), me).wait_recv()
                self._copy(i, 4 + j, (*chip, c), sibling).start()

    def finish(self):
        me, sibling, chips, c = self._where()
        for i in range(self.n):
            self._copy(i, 0, sibling, me).wait_recv()
            for j, chip in enumerate(chips):
                self._copy(i, 4 + j, (*chip, 1 - c), me).wait_recv()
        for cp in self._first(me, sibling, chips, c):
            cp.wait_send()
        for j, chip in enumerate(chips):
            for i in range(self.n):
                self._copy(i, 4 + j, (*chip, c), sibling).wait_send()
        for i in range(self.n):
            self._mine(i, me).wait()


class _exchange_plan:
    def __init__(self, in_refs, out_refs, send_sems, recv_sems, local_sems):
        self.in_refs, self.out_refs = in_refs, out_refs
        self.send_sems, self.recv_sems, self.local_sems = send_sems, recv_sems, local_sems
        self.n = len(in_refs)

    def _copies(self):
        x, y, c = _place()
        me = _index(x, y, c)
        mine = [pltpu.make_async_copy(self.in_refs[i].at[me], self.out_refs[i].at[me], self.local_sems.at[i])
                for i in range(self.n)]
        remote = []
        for k in range(1, N_DEV):
            peer = (1 - x if k & 4 else x, 1 - y if k & 2 else y, 1 - c if k & 1 else c)
            remote += [pltpu.make_async_remote_copy(
                src_ref=self.in_refs[i].at[_index(*peer)], dst_ref=self.out_refs[i].at[me],
                send_sem=self.send_sems.at[i, k - 1], recv_sem=self.recv_sems.at[i, k - 1],
                device_id=peer, device_id_type=MESH) for i in range(self.n)]
        return mine, remote

    def start(self):
        mine, remote = self._copies()
        for cp in mine + remote:
            cp.start()

    def finish(self):
        mine, remote = self._copies()
        for cp in remote:
            cp.wait_recv()
        for cp in remote:
            cp.wait_send()
        for cp in mine:
            cp.wait()


def _all_gather(shards, name):
    n = len(shards)

    def body(*refs):
        plan = _gather_plan(refs[:n], refs[n:2 * n], *refs[2 * n:])
        plan.start()
        plan.forward()
        plan.finish()

    return pl.pallas_call(
        body, name=name, in_specs=[ANY] * n, out_specs=[ANY] * n, scratch_shapes=_comm_sems(n),
        out_shape=[jax.ShapeDtypeStruct((N_DEV,) + a.shape, a.dtype) for a in shards],
    )(*shards)


def _exchange(slabs, name):
    n = len(slabs)

    def body(*refs):
        plan = _exchange_plan(refs[:n], refs[n:2 * n], *refs[2 * n:])
        plan.start()
        plan.finish()

    return pl.pallas_call(
        body, name=name, in_specs=[ANY] * n, out_specs=[ANY] * n, scratch_shapes=_comm_sems(n),
        out_shape=[jax.ShapeDtypeStruct(a.shape, a.dtype) for a in slabs],
    )(*slabs)


def _adamw(w, g, m, v):
    m = ADAM_B1 * m + (1.0 - ADAM_B1) * g
    v = ADAM_B2 * v + (1.0 - ADAM_B2) * (g * g)
    m_hat = m / (1.0 - ADAM_B1 ** ADAM_STEP)
    v_hat = v / (1.0 - ADAM_B2 ** ADAM_STEP)
    return -ADAM_LR * (m_hat / (jnp.sqrt(v_hat) + ADAM_EPS) + ADAM_WD * w), m, v


def _sum_slots(ref):
    g = ref[0].astype(F32)
    for s in range(1, N_DEV):
        g = g + ref[s].astype(F32)
    return g


def _reduce_adamw(parts, w, m, v, name):
    l, r, c = w.shape
    tile = max([d for d in range(16, ADAM_ROWS + 1, 16) if r % d == 0], default=r)
    last = r // tile - 1

    def body(*refs):
        p_refs, (w_ref, m_ref, v_ref), (g_ref, d_ref, nm_ref, nv_ref) = refs[:l], refs[l:l + 3], refs[l + 3:]
        for layer in range(l):
            @pl.when(pl.program_id(0) == layer)
            def _(p_ref=p_refs[layer]):
                g = _sum_slots(p_ref)
                g_ref[...] = g
                d_ref[...], nm_ref[...], nv_ref[...] = _adamw(w_ref[...], g, m_ref[...], v_ref[...])

    def part_spec(layer):
        return pl.BlockSpec((N_DEV, tile, c),
                            lambda i, j: (0, jnp.where(i == layer, j, jnp.where(i < layer, 0, last)), 0))

    blk = pl.BlockSpec((None, tile, c), lambda i, j: (i, j, 0))
    return pl.pallas_call(
        body, name=name, grid=(l, r // tile),
        in_specs=[part_spec(layer) for layer in range(l)] + [blk, blk, blk], out_specs=[blk] * 4,
        out_shape=[jax.ShapeDtypeStruct((l, r, c), F32)] * 4,
        compiler_params=_params("arbitrary", "arbitrary"),
    )(*parts, w, m, v)


def _sum_parts(parts, name):
    _, r, c = parts.shape

    def body(p_ref, g_ref):
        g_ref[...] = _sum_slots(p_ref)

    return pl.pallas_call(body, name=name, out_shape=jax.ShapeDtypeStruct((r, c), F32))(parts)


def _adamw_call(w, g, m, v, name):
    def body(w_ref, g_ref, m_ref, v_ref, d_ref, nm_ref, nv_ref):
        d_ref[...], nm_ref[...], nv_ref[...] = _adamw(w_ref[...], g_ref[...], m_ref[...], v_ref[...])

    return pl.pallas_call(body, name=name, out_shape=[jax.ShapeDtypeStruct(w.shape, F32)] * 3)(w, g, m, v)


ATTN_IN = ("w_in", "w_q_up", "w_kv_up")
ATTN = ATTN_IN + ("w_o",)
FFN = ("w_gate", "w_up", "w_down")
TRANSPOSED = ("w_in", "w_q_up", "w_kv_up", "w_gate", "w_up")
SMALL = ("attn_norm", "ffn_norm", "final_norm", "out_norm_swa", "out_norm_mla", "q_norm", "kv_norm", "sinks")
PACK_W = 1024
SMALL_ROWS = 16


def _pack(arrs, dtype):
    flat = jnp.concatenate([a.astype(dtype).reshape(-1) for a in arrs])
    return flat.reshape(-1, PACK_W)


def _unpack(packed, like):
    flat = packed.reshape(-1)
    out, off = [], 0
    for a in like:
        out.append(flat[off:off + a.size].reshape(a.shape))
        off += a.size
    return out


def _gather_to_full(gathered):
    return gathered.reshape((-1,) + gathered.shape[2:])


def _full_to_slabs(full):
    return full.reshape((N_DEV, -1) + full.shape[1:])


class _ShardedWeights:
    def __init__(self, shards, depth, meta_shard):
        self.shards, self.depth = shards, depth
        self.gathered, self.pending, self.parts = {}, {}, {}
        first = _all_gather([shards[n][0] for n in ATTN_IN] + [meta_shard], "gather_attn0")
        self.gathered.update(zip([(n, 0) for n in ATTN_IN], first))
        self.meta = jnp.moveaxis(first[-1], 0, 1).reshape(N_META, D_MODEL)

    def _gather(self, keys, run):
        self.gathered.update(zip(keys, run([self.shards[n][l] for n, l in keys])))

    def _full(self, names, l):
        return tuple(_gather_to_full(self.gathered[n, l]) for n in names)

    def attn_in(self, l):
        return self._full(ATTN_IN, l)

    def w_o(self, l):
        return self._full(("w_o",), l)[0]

    def ffn(self, l):
        return self._full(FFN, l)

    def mla_fwd(self, l, q, k, v):
        out = []
        self._gather([(n, l) for n in ("w_o",) + FFN], lambda xs: out.extend(_mla_fwd(q, k, v, xs)) or out[2])
        return out[0], out[1]

    def ffn_fwd(self, l, h2, u2):
        keys = [(n, l + 1) for n in ATTN_IN] if l + 1 < self.depth else []
        out = []
        self._gather(keys, lambda xs: out.extend(_ffn_fwd(h2, u2, *self.ffn(l), xs)) or out[1])
        return out[0]

    def _add(self, names, l, grads):
        for n, g in zip(names, grads):
            self.pending[n, l] = _full_to_slabs(g)

    def ffn_grads(self, l, *grads):
        self._add(FFN, l, grads)

    def attn_grads(self, l, **grads):
        self._add(list(grads), l, grads.values())

    def _exchange(self, run):
        keys = list(self.pending)
        self.parts.update(zip(keys, run([self.pending.pop(k) for k in keys])))

    def mla_bwd(self, l, *args):
        out = []
        self._exchange(lambda xs: out.extend(_mla_bwd(*args, xs)) or out[1])
        return out[0]

    def flush(self):
        self._exchange(lambda xs: _exchange(xs, "exchange_attn0"))


def kernel(x, meta_tokens, attn_norm, w_in, q_norm, w_q_up, kv_norm, w_kv_up, sinks, out_norm_swa, out_norm_mla, w_o, ffn_norm, w_gate, w_up, w_down, final_norm, loss_target, m_meta_tokens, m_attn_norm, m_w_in, m_q_norm, m_w_q_up, m_kv_norm, m_w_kv_up, m_sinks, m_out_norm_swa, m_out_norm_mla, m_w_o, m_ffn_norm, m_w_gate, m_w_up, m_w_down, m_final_norm, v_meta_tokens, v_attn_norm, v_w_in, v_q_norm, v_w_q_up, v_kv_norm, v_w_kv_up, v_sinks, v_out_norm_swa, v_out_norm_mla, v_w_o, v_ffn_norm, v_w_gate, v_w_up, v_w_down, v_final_norm):
    w = dict(meta_tokens=meta_tokens, attn_norm=attn_norm, w_in=w_in, q_norm=q_norm, w_q_up=w_q_up, kv_norm=kv_norm,
             w_kv_up=w_kv_up, sinks=sinks, out_norm_swa=out_norm_swa, out_norm_mla=out_norm_mla, w_o=w_o,
             ffn_norm=ffn_norm, w_gate=w_gate, w_up=w_up, w_down=w_down, final_norm=final_norm)
    m = dict(meta_tokens=m_meta_tokens, attn_norm=m_attn_norm, w_in=m_w_in, q_norm=m_q_norm, w_q_up=m_w_q_up,
             kv_norm=m_kv_norm, w_kv_up=m_w_kv_up, sinks=m_sinks, out_norm_swa=m_out_norm_swa,
             out_norm_mla=m_out_norm_mla, w_o=m_w_o, ffn_norm=m_ffn_norm, w_gate=m_w_gate, w_up=m_w_up,
             w_down=m_w_down, final_norm=m_final_norm)
    v = dict(meta_tokens=v_meta_tokens, attn_norm=v_attn_norm, w_in=v_w_in, q_norm=v_q_norm, w_q_up=v_w_q_up,
             kv_norm=v_kv_norm, w_kv_up=v_w_kv_up, sinks=v_sinks, out_norm_swa=v_out_norm_swa,
             out_norm_mla=v_out_norm_mla, w_o=v_w_o, ffn_norm=v_ffn_norm, w_gate=v_w_gate, w_up=v_w_up,
             w_down=v_w_down, final_norm=v_final_norm)
    names = list(w)
    big = ATTN + FFN
    depth = w_in.shape[0]
    me = _index(*_place())

    as_held = lambda n, a: jnp.swapaxes(a, 1, 2) if n in TRANSPOSED else a
    weights = _ShardedWeights({n: as_held(n, w[n]).astype(BF16) for n in big}, depth, meta_tokens)
    loss, grad_x, d_meta, grads = _train_example(x[0], loss_target[0], weights.meta, {n: w[n] for n in SMALL}, weights)
    weights.flush()

    g_big, d_big, m_big, v_big = {}, {}, {}, {}
    for n in big:
        held = [as_held(n, a) for a in (w[n], m[n], v[n])]
        outs = _reduce_adamw([weights.parts[n, l] for l in range(depth)], *held, "reduce_adamw_" + n)
        g_big[n], d_big[n], m_big[n], v_big[n] = [as_held(n, a) for a in outs]

    small = [grads[n] for n in SMALL] + [loss.reshape(1)]
    pad = SMALL_ROWS * PACK_W - sum(a.size for a in small)
    part = jnp.concatenate([_pack(small + [jnp.zeros((pad,), F32)], F32), d_meta], axis=0)
    total = _sum_parts(_all_gather([part], "gather_small")[0], "sum_small")
    small_w = [w[n] for n in SMALL]
    packs = [_pack([d[n] for n in SMALL] + [jnp.zeros((pad + 1,), F32)], F32) for d in (w, m, v)]
    upd = _adamw_call(packs[0], total[:SMALL_ROWS], packs[1], packs[2], "adamw_small")
    g_small, d_small, m_small, v_small = [dict(zip(SMALL, _unpack(p, small_w))) for p in (total[:SMALL_ROWS],) + tuple(upd)]
    loss_total = total[:SMALL_ROWS].reshape(-1)[SMALL_ROWS * PACK_W - pad - 1]
    g_meta = lax.dynamic_slice_in_dim(total[SMALL_ROWS:], me * LANE, LANE, axis=1)
    d_mt, m_mt, v_mt = _adamw_call(meta_tokens, g_meta, m_meta_tokens, v_meta_tokens, "adamw_meta")

    outs = []
    for got in ({**g_big, **g_small, "meta_tokens": g_meta}, {**d_big, **d_small, "meta_tokens": d_mt},
                {**m_big, **m_small, "meta_tokens": m_mt}, {**v_big, **v_small, "meta_tokens": v_mt}):
        outs += [got[n] for n in names]
    return (loss_total, grad_x[None], *outs)
```

```python
import jax
import jax.numpy as jnp
from jax import lax
from jax.experimental import pallas as pl
from jax.experimental.pallas import tpu as pltpu

F32 = jnp.float32
BF16 = jnp.bfloat16

D_MODEL = 1024
N_META = 16
BLOCK = 128
FRONT = (-N_META) % BLOCK
ROPE_THETA = 10000.0
EPS = 1e-6
NEG = -1e30
SWA_HEADS = 8
SWA_KV_HEADS = 2
SWA_GROUP = SWA_HEADS // SWA_KV_HEADS
SWA_HEAD_DIM = 64
MLA_HEADS = 8
MLA_Q_RANK = 256
MLA_KV_RANK = 128
MLA_NOPE_DIM = 64
MLA_ROPE_DIM = 32
MLA_V_DIM = 64
MLA_QK_DIM = MLA_NOPE_DIM + MLA_ROPE_DIM
SWA_Q_W = SWA_HEADS * SWA_HEAD_DIM
SWA_KV_W = SWA_KV_HEADS * SWA_HEAD_DIM
MLA_OUT_W = MLA_HEADS * MLA_V_DIM
SCALE_A = SWA_HEAD_DIM ** -0.5
SCALE_B = MLA_QK_DIM ** -0.5
LOG2E = 1.4426950408889634
Q_SCALE = SCALE_B * LOG2E
ADAM_LR = 0.001
ADAM_B1 = 0.9
ADAM_B2 = 0.999
ADAM_EPS = 1e-08
ADAM_WD = 0.01
ADAM_STEP = 10

LANE = 128
N_DEV = 8
HP = 8 * LANE
PO_QA, PO_KA, PO_VA = 0, HP, HP + 2 * LANE
PO_CQ = PO_VA + 2 * LANE
PO_CKV = PO_CQ + MLA_Q_RANK
PO_KR = PO_CKV + MLA_KV_RANK
PW_IN = PO_KR + LANE
N_TAB = 7
VMEM_LIMIT = 56 * 2 ** 20
TN_VMEM_BUDGET = 36 * 2 ** 20
MLA_HB = 4
MLA_HB_FWD = 8
ADAM_ROWS = 256
HALF = LANE // 2
assert SWA_HEAD_DIM == HALF and MLA_V_DIM == HALF

NT = (((1,), (1,)), ((), ()))
TN = (((0,), (0,)), ((), ()))


def _tile(t):
    return 384 if t % 384 == 0 else 128


def _params(*sem):
    return pltpu.CompilerParams(dimension_semantics=sem, vmem_limit_bytes=VMEM_LIMIT)


def _row(tm, n):
    return pl.BlockSpec((tm, n), lambda i: (i, 0))


def _const(shape):
    return pl.BlockSpec(shape, lambda i: (0,) * len(shape))


def _dot(a, b):
    return jnp.dot(a, b, preferred_element_type=F32)


def _dot_nt(a, b):
    return lax.dot_general(a, b, NT, preferred_element_type=F32)


def _dot_tn(a, b):
    return lax.dot_general(a, b, TN, preferred_element_type=F32)


def _rope(x, c, s1, s2, shift):
    return x * c + pltpu.roll(x, LANE - shift, 1) * s1 + pltpu.roll(x, shift, 1) * s2


def _rope_t(dy, c, s1, s2, shift):
    return dy * c + pltpu.roll(dy * s1, shift, 1) + pltpu.roll(dy * s2, LANE - shift, 1)


def _rms_r(x, n):
    return lax.rsqrt(jnp.sum(x * x, axis=-1, keepdims=True) * (1.0 / n) + EPS)


def _rms_bwd(x, g, dy, n):
    r = _rms_r(x, n)
    xh = x * r
    dxh = dy * g
    dx = r * (dxh - xh * (jnp.sum(dxh * xh, axis=-1, keepdims=True) * (1.0 / n)))
    return dx, jnp.sum(dy * xh, axis=0, keepdims=True)


def _acc(ref, val, first):
    @pl.when(first)
    def _():
        ref[...] = val

    @pl.when(jnp.logical_not(first))
    def _():
        ref[...] += val


def _pair_half(slab, half):
    return slab if half == 0 else pltpu.roll(slab, HALF, 1)


def _unpack_pair(slab, half):
    x = _pair_half(slab, half)
    return jnp.where(lax.broadcasted_iota(jnp.int32, x.shape, 1) < HALF, x, 0.0)


def _tabs(tab_ref):
    return [tab_ref[:, LANE * i:LANE * (i + 1)] for i in range(N_TAB)]


def _pre_fwd(h, g1, win, gq, wqu, gkv, wkv, tabs):
    t = h.shape[0]
    tm = _tile(t)

    def body(h_ref, g1_ref, win_ref, gq_ref, wqu_ref, gkv_ref, wkv_ref, tab_ref,
             u_ref, qa_ref, ka_ref, va_ref, cq_ref, ckv_ref, qn_ref, kvn_ref, qb_ref, kf_ref, vb_ref):
        ca, sa1, sa2, cb, sb1, sb2, ck = _tabs(tab_ref)
        hv = h_ref[...]
        u = (hv * _rms_r(hv, D_MODEL) * g1_ref[...]).astype(BF16)
        u_ref[...] = u
        p = _dot_nt(u, win_ref[...])
        for c in range(SWA_HEADS):
            sl = slice(LANE * c, LANE * (c + 1))
            qa_ref[:, sl] = _rope(p[:, PO_QA + LANE * c:PO_QA + LANE * (c + 1)], ca, sa1, sa2, 32).astype(BF16)
        for c in range(SWA_KV_HEADS):
            sl = slice(LANE * c, LANE * (c + 1))
            ka_ref[:, sl] = _rope(p[:, PO_KA + LANE * c:PO_KA + LANE * (c + 1)], ca, sa1, sa2, 32).astype(BF16)
        va_ref[...] = p[:, PO_VA:PO_CQ].astype(BF16)
        cq = p[:, PO_CQ:PO_CKV]
        ckv = p[:, PO_CKV:PO_KR]
        cq_ref[...] = cq
        ckv_ref[...] = ckv
        qn = (cq * _rms_r(cq, MLA_Q_RANK) * gq_ref[...]).astype(BF16)
        qn_ref[...] = qn
        qb = _dot_nt(qn, wqu_ref[...])
        kvn = (ckv * _rms_r(ckv, MLA_KV_RANK) * gkv_ref[...]).astype(BF16)
        kvn_ref[...] = kvn
        kv = _dot_nt(kvn, wkv_ref[...])
        kr = _rope(p[:, PO_KR:PW_IN], ck, sb1, sb2, 16)
        for c in range(MLA_HEADS):
            sl = slice(LANE * c, LANE * (c + 1))
            qb_ref[:, sl] = (_rope(qb[:, sl], cb, sb1, sb2, 16) * Q_SCALE).astype(BF16)
            kf_ref[:, sl] = (kv[:, sl] + kr).astype(BF16)
        vb_ref[...] = kv[:, HP:].astype(BF16)

    widths = [(D_MODEL, BF16), (HP, BF16), (2 * LANE, BF16), (2 * LANE, BF16), (MLA_Q_RANK, F32),
              (MLA_KV_RANK, F32), (MLA_Q_RANK, BF16), (MLA_KV_RANK, BF16), (HP, BF16), (HP, BF16), (HP, BF16)]
    return pl.pallas_call(
        body, name="pre_fwd", grid=(t // tm,),
        in_specs=[_row(tm, D_MODEL), _const(g1.shape), _const(win.shape), _const(gq.shape), _const(wqu.shape),
                  _const(gkv.shape), _const(wkv.shape), _row(tm, N_TAB * LANE)],
        out_specs=[_row(tm, w) for w, _ in widths],
        out_shape=[jax.ShapeDtypeStruct((t, w), d) for w, d in widths],
        compiler_params=_params("parallel"),
    )(h, g1, win, gq, wqu, gkv, wkv, tabs)


def _swa_mask(nb):
    key = lax.broadcasted_iota(jnp.int32, (2 * BLOCK, SWA_GROUP * BLOCK), 0)
    qry = lax.broadcasted_iota(jnp.int32, (2 * BLOCK, SWA_GROUP * BLOCK), 1) & (BLOCK - 1)
    return (key > qry) & (key <= qry + BLOCK) & (key + (nb - 1) * BLOCK >= FRONT)


def _swa_group(ref, rows, j):
    return jnp.concatenate([ref[rows, LANE * (SWA_GROUP * j + g):LANE * (SWA_GROUP * j + g + 1)]
                            for g in range(SWA_GROUP)], axis=0)


def _swa_packed_group(ref, rows, j):
    heads = [SWA_GROUP * j + g for g in range(SWA_GROUP)]
    return jnp.concatenate([_pair_half(ref[rows, LANE * (hd // 2):LANE * (hd // 2 + 1)], hd % 2) for hd in heads], axis=0)


def _swa_sinks(sink_ref, j):
    return jnp.concatenate([jnp.full((1, BLOCK), sink_ref[0, SWA_GROUP * j + g], F32) for g in range(SWA_GROUP)], axis=1)


def _swa_keys(prev_ref, cur_ref, rb, j):
    sl = slice(LANE * j, LANE * (j + 1))
    if rb == 0:
        return jnp.concatenate([prev_ref[:, sl], cur_ref[:BLOCK, sl]], axis=0)
    return cur_ref[BLOCK * (rb - 1):BLOCK * (rb + 1), sl]


def _swa_chains(t):
    return [(rb, j) for rb in range(_tile(t) // BLOCK) for j in range(SWA_KV_HEADS)]


def _swa_scores(sink_ref, q_ref, kp_ref, kc_ref, n, t):
    r = _tile(t) // BLOCK
    chains = _swa_chains(t)
    qs = [_swa_group(q_ref, slice(BLOCK * rb, BLOCK * (rb + 1)), j) for rb, j in chains]
    ks = [_swa_keys(kp_ref, kc_ref, rb, j) for rb, j in chains]
    ss = [_dot_nt(k2, q4) for q4, k2 in zip(qs, ks)]
    masks = [_swa_mask(n * r + rb) for rb in range(r)]
    out = []
    for (rb, j), s in zip(chains, ss):
        sink = _swa_sinks(sink_ref, j)
        s = jnp.where(masks[rb], s * SCALE_A, NEG)
        m = jnp.maximum(jnp.max(s, axis=0, keepdims=True), sink)
        e = jnp.exp(s - m)
        es = jnp.exp(sink - m)
        inv = 1.0 / (jnp.sum(e, axis=0, keepdims=True) + es)
        out.append((e * inv, es * inv))
    return qs, ks, out


def _swa_specs(t):
    ts = _tile(t)
    r = ts // BLOCK
    prev = lambda n: (jnp.maximum(n * r - 1, 0), 0)
    cur = lambda n: (n, 0)
    return [pl.BlockSpec(memory_space=pltpu.SMEM), pl.BlockSpec((ts, HP), cur),
            pl.BlockSpec((BLOCK, 2 * LANE), prev), pl.BlockSpec((ts, 2 * LANE), cur),
            pl.BlockSpec((BLOCK, 2 * LANE), prev), pl.BlockSpec((ts, 2 * LANE), cur)]


def _swa_fwd(sinks, q, k, v):
    t = q.shape[0]
    ts = _tile(t)

    def body(sink_ref, q_ref, kp_ref, kc_ref, vp_ref, vc_ref, o_ref):
        chains = _swa_chains(t)
        _, _, probs = _swa_scores(sink_ref, q_ref, kp_ref, kc_ref, pl.program_id(0), t)
        os_ = [_dot_tn(_swa_keys(vp_ref, vc_ref, rb, j)[:, :HALF], p.astype(BF16)) for (rb, j), (p, _) in zip(chains, probs)]
        for (rb, j), o4 in zip(chains, os_):
            for g in range(0, SWA_GROUP, 2):
                pair = (SWA_GROUP * j + g) // 2
                o_ref[BLOCK * rb:BLOCK * (rb + 1), LANE * pair:LANE * (pair + 1)] = jnp.concatenate(
                    [o4[:, BLOCK * g:BLOCK * (g + 1)], o4[:, BLOCK * (g + 1):BLOCK * (g + 2)]], axis=0).T

    return pl.pallas_call(
        body, name="swa_fwd", grid=(t // ts,),
        in_specs=_swa_specs(t),
        out_specs=pl.BlockSpec((ts, SWA_Q_W), lambda n: (n, 0)),
        out_shape=jax.ShapeDtypeStruct((t, SWA_Q_W), F32),
        compiler_params=_params("parallel"),
    )(sinks, q, k, k, v, v)


def _causal_mask(q0, k0, tq, tk, transposed):
    if transposed:
        key = k0 + lax.broadcasted_iota(jnp.int32, (tk, tq), 0)
        qry = q0 + lax.broadcasted_iota(jnp.int32, (tk, tq), 1)
    else:
        qry = q0 + lax.broadcasted_iota(jnp.int32, (tq, tk), 0)
        key = k0 + lax.broadcasted_iota(jnp.int32, (tq, tk), 1)
    return (key <= qry) & (key >= FRONT)


def _heads(ref, hb, rows=slice(None)):
    return [ref[rows, LANE * a:LANE * (a + 1)] for a in range(hb)]


def _head_stats(t, hb=MLA_HB):
    return jax.ShapeDtypeStruct((MLA_HEADS // hb, t, hb), F32)


def _mla_fwd(q, k, v, shards=()):
    t = q.shape[0]
    tq = _tile(t)
    nq = t // tq
    n = len(shards)
    hb = MLA_HB_FWD
    steps = (MLA_HEADS // hb) * nq

    def body(q_ref, k_ref, v_ref, *rest):
        x_refs, (o_ref, lse_ref), out_refs = rest[:n], rest[n:n + 2], rest[n + 2:2 * n + 2]
        acc_sc, sems = rest[2 * n + 2], rest[2 * n + 3:]
        i = pl.program_id(1)
        step_id = pl.program_id(0) * nq + i
        if n:
            plan = _gather_plan(x_refs, out_refs, *sems)
            pl.when(step_id == 0)(plan.start)
            pl.when(step_id == (3 * steps) // 4)(plan.forward)
        acc_sc[...] = jnp.zeros(acc_sc.shape, F32)

        def step(j, carry, masked):
            qs = _heads(q_ref, hb)
            rows = pl.ds(pl.multiple_of(j * tq, tq), tq)
            ks = _heads(k_ref, hb, rows)
            vs = [v_ref[rows, LANE * a:LANE * a + HALF] for a in range(hb)]
            ss = [_dot_nt(kh, qh) for qh, kh in zip(qs, ks)]
            if masked:
                mask = _causal_mask(i * tq, j * tq, tq, tq, True)
                ss = [jnp.where(mask, s, NEG) for s in ss]
            mid, out = [], []
            for s, (m, l) in zip(ss, carry):
                mn = jnp.maximum(m, jnp.max(s, axis=0, keepdims=True))
                al = jnp.exp2(m - mn)
                p = jnp.exp2(s - mn)
                out.append((mn, al * l + jnp.sum(p, axis=0, keepdims=True)))
                mid.append((al, p.astype(BF16)))
            for a, ((al, p), vh) in enumerate(zip(mid, vs)):
                acc_sc[a] = al * acc_sc[a] + _dot_tn(vh, p)
            return tuple(out)

        init = ((jnp.full((1, tq), NEG, F32), jnp.zeros((1, tq), F32)),) * hb
        carry = lax.fori_loop(0, jnp.minimum(i, 1) + 1, lambda it, c: step(it * i, c, True), init)
        carry = lax.fori_loop(1, i, lambda j, c: step(j, c, False), carry)
        outs = [acc_sc[a] * (1.0 / l) for a, (_, l) in enumerate(carry)]
        for a in range(0, hb, 2):
            o_ref[:, HALF * a:HALF * (a + 2)] = jnp.concatenate(outs[a:a + 2], axis=0).T
        for a, (m, l) in enumerate(carry):
            lse_ref[:, a:a + 1] = jnp.broadcast_to(m + jnp.log2(l), (LANE, tq)).T[:, :1]
        if n:
            pl.when(step_id == steps - 1)(plan.finish)

    blk = pl.BlockSpec((tq, hb * LANE), lambda h, i: (i, h))
    full = pl.BlockSpec((t, hb * LANE), lambda h, i: (0, h))
    packed = pl.BlockSpec((tq, hb * HALF), lambda h, i: (i, h))
    out = pl.pallas_call(
        body, name="mla_fwd_gather" if n else "mla_fwd", grid=(MLA_HEADS // hb, nq),
        in_specs=[blk, full, full] + [ANY] * n,
        out_specs=[packed, pl.BlockSpec((None, tq, hb), lambda h, i: (h, i, 0))] + [ANY] * n,
        out_shape=[jax.ShapeDtypeStruct((t, MLA_OUT_W), F32), _head_stats(t, hb)]
        + [jax.ShapeDtypeStruct((N_DEV,) + a.shape, a.dtype) for a in shards],
        scratch_shapes=[pltpu.VMEM((hb, HALF, tq), F32)] + (_comm_sems(n) if n else []),
        compiler_params=_params("arbitrary", "arbitrary"),
    )(q, k, v, *shards)
    return out[0], out[1], out[2:]


def _mix_fwd(h, oa, ob, ga, gb, wo, g2):
    t = h.shape[0]
    tm = _tile(t)

    def body(h_ref, oa_ref, ob_ref, ga_ref, gb_ref, wo_ref, g2_ref, h2_ref, mix_ref, u2_ref):
        oa_v = oa_ref[...]
        ob_v = ob_ref[...]
        na = (oa_v * _rms_r(oa_v, SWA_Q_W) * ga_ref[...]).astype(BF16)
        nb = (ob_v * _rms_r(ob_v, MLA_OUT_W) * gb_ref[...]).astype(BF16)
        mix_ref[:, :SWA_Q_W] = na
        mix_ref[:, SWA_Q_W:] = nb
        h2 = h_ref[...] + _dot(na, wo_ref[:SWA_Q_W, :]) + _dot(nb, wo_ref[SWA_Q_W:, :])
        h2_ref[...] = h2
        u2_ref[...] = (h2 * _rms_r(h2, D_MODEL) * g2_ref[...]).astype(BF16)

    mix_w = SWA_Q_W + MLA_OUT_W
    return pl.pallas_call(
        body, name="mix_fwd", grid=(t // tm,),
        in_specs=[_row(tm, D_MODEL), _row(tm, SWA_Q_W), _row(tm, MLA_OUT_W), _const(ga.shape), _const(gb.shape),
                  _const(wo.shape), _const(g2.shape)],
        out_specs=[_row(tm, D_MODEL), _row(tm, mix_w), _row(tm, D_MODEL)],
        out_shape=[jax.ShapeDtypeStruct((t, D_MODEL), F32), jax.ShapeDtypeStruct((t, mix_w), BF16),
                   jax.ShapeDtypeStruct((t, D_MODEL), BF16)],
        compiler_params=_params("parallel"),
    )(h, oa, ob, ga, gb, wo, g2)


def _ffn_fwd(h2, u2, wg_t, wu_t, wd, shards=()):
    t = h2.shape[0]
    tm = _tile(t)
    dff = wd.shape[0]
    n = len(shards)
    steps = t // tm

    def body(h2_ref, u2_ref, wg_ref, wu_ref, wd_ref, *rest):
        x_refs, (h3_ref, g_ref, up_ref), out_refs, sems = rest[:n], rest[n:n + 3], rest[n + 3:2 * n + 3], rest[2 * n + 3:]
        if n:
            plan = _gather_plan(x_refs, out_refs, *sems)
            pl.when(pl.program_id(0) == 0)(plan.start)
            pl.when(pl.program_id(0) == (3 * steps) // 4)(plan.forward)
        u2v = u2_ref[...]
        g = _dot_nt(u2v, wg_ref[...])
        up = _dot_nt(u2v, wu_ref[...])
        g_ref[...] = g.astype(BF16)
        up_ref[...] = up.astype(BF16)
        a = (g * jax.nn.sigmoid(g) * up).astype(BF16)
        h3_ref[...] = h2_ref[...] + _dot(a, wd_ref[...])
        if n:
            pl.when(pl.program_id(0) == steps - 1)(plan.finish)

    out = pl.pallas_call(
        body, name="ffn_fwd_gather" if n else "ffn_fwd", grid=(steps,),
        in_specs=[_row(tm, D_MODEL), _row(tm, D_MODEL), _const(wg_t.shape), _const(wu_t.shape), _const(wd.shape)] + [ANY] * n,
        out_specs=[_row(tm, D_MODEL), _row(tm, dff), _row(tm, dff)] + [ANY] * n,
        out_shape=[jax.ShapeDtypeStruct((t, D_MODEL), F32), jax.ShapeDtypeStruct((t, dff), BF16),
                   jax.ShapeDtypeStruct((t, dff), BF16)] + [jax.ShapeDtypeStruct((N_DEV,) + a.shape, a.dtype) for a in shards],
        scratch_shapes=_comm_sems(n) if n else [],
        compiler_params=_params("arbitrary" if n else "parallel"),
    )(h2, u2, wg_t, wu_t, wd, *shards)
    return out[:3], out[3:]


def _loss_bwd(h, gf, target):
    t = h.shape[0]
    tm = _tile(t)
    r = tm // BLOCK
    assert FRONT + N_META == BLOCK and target.shape[0] == t - BLOCK

    def body(h_ref, gf_ref, *rest):
        t_refs, (dh_ref, dgf_ref, loss_ref) = rest[:r], rest[r:]
        i = pl.program_id(0)
        hv = h_ref[...]
        y = hv * _rms_r(hv, D_MODEL) * gf_ref[...]
        row = i * tm + lax.broadcasted_iota(jnp.int32, (tm, 1), 0)
        tv = jnp.concatenate([t_ref[...] for t_ref in t_refs], axis=0)
        err = jnp.where(row >= BLOCK, y - tv, 0.0)
        dx, dg = _rms_bwd(hv, gf_ref[...], err * (1.0 / D_MODEL), D_MODEL)
        dh_ref[...] = dx
        _acc(dgf_ref, dg, i == 0)
        part = 0.5 * jnp.sum(jnp.sum(err * err, axis=1, keepdims=True) * (1.0 / D_MODEL), axis=0, keepdims=True)
        _acc(loss_ref, jnp.broadcast_to(part, (1, LANE)), i == 0)

    t_specs = [pl.BlockSpec((BLOCK, D_MODEL), lambda i, b=b: (jnp.maximum(r * i + b - 1, 0), 0)) for b in range(r)]
    return pl.pallas_call(
        body, name="loss_bwd", grid=(t // tm,),
        in_specs=[_row(tm, D_MODEL), _const(gf.shape)] + t_specs,
        out_specs=[_row(tm, D_MODEL), _const((1, D_MODEL)), _const((1, LANE))],
        out_shape=[jax.ShapeDtypeStruct((t, D_MODEL), F32), jax.ShapeDtypeStruct((1, D_MODEL), F32),
                   jax.ShapeDtypeStruct((1, LANE), F32)],
        compiler_params=_params("arbitrary"),
    )(h, gf, *[target] * r)


def _tn_matmul(a, b, name, cols=None, keep=None):
    t, n = b.shape
    first, k = cols or (0, a.shape[1])
    tk = next(c for c in (k, 1024, 512, 256, 128) if k % c == 0 and first % c == 0 and c <= 1024)
    fits = lambda c: 2 * (t * (tk + c) * 2 + tk * c * 2) <= TN_VMEM_BUDGET
    tn = next(c for c in (n, 1024, 512, 256, 128) if n % c == 0 and fits(c))
    kept = tk if keep is None else sum(size for _, size in keep)

    def body(a_ref, b_ref, o_ref):
        if keep is None:
            o_ref[...] = _dot_tn(a_ref[...], b_ref[...]).astype(BF16)
        else:
            at = a_ref[...].T
            at = jnp.concatenate([at[start:start + size] for start, size in keep], axis=0)
            o_ref[...] = _dot(at, b_ref[...]).astype(BF16)

    return pl.pallas_call(
        body, name=name, grid=(k // tk, n // tn),
        in_specs=[pl.BlockSpec((t, tk), lambda i, j: (0, i + first // tk)), pl.BlockSpec((t, tn), lambda i, j: (0, j))],
        out_specs=pl.BlockSpec((kept, tn), lambda i, j: (i, j)),
        out_shape=jax.ShapeDtypeStruct((k // tk * kept, n), BF16),
        compiler_params=_params("parallel", "parallel"),
    )(a, b)


def _ffn_bwd_a(dh3, g, up, wd):
    t = dh3.shape[0]
    tm = _tile(t)
    dff = wd.shape[0]

    def body(dh3_ref, g_ref, up_ref, wd_ref, a_ref, dgu_ref, dh3b_ref):
        dh3b = dh3_ref[...].astype(BF16)
        dh3b_ref[...] = dh3b
        da = _dot_nt(dh3b, wd_ref[...])
        gv = g_ref[...].astype(F32)
        upv = up_ref[...].astype(F32)
        sg = jax.nn.sigmoid(gv)
        silu = gv * sg
        a_ref[...] = (silu * upv).astype(BF16)
        dgu_ref[:, :dff] = (da * upv * (sg * (1.0 + gv * (1.0 - sg)))).astype(BF16)
        dgu_ref[:, dff:] = (da * silu).astype(BF16)

    return pl.pallas_call(
        body, name="ffn_bwd_a", grid=(t // tm,),
        in_specs=[_row(tm, D_MODEL), _row(tm, dff), _row(tm, dff), _const(wd.shape)],
        out_specs=[_row(tm, dff), _row(tm, 2 * dff), _row(tm, D_MODEL)],
        out_shape=[jax.ShapeDtypeStruct((t, dff), BF16), jax.ShapeDtypeStruct((t, 2 * dff), BF16),
                   jax.ShapeDtypeStruct((t, D_MODEL), BF16)],
        compiler_params=_params("parallel"),
    )(dh3, g, up, wd)


def _ffn_bwd_b(dh3, dgu, h2, g2, wg_t, wu_t):
    t = dh3.shape[0]
    tm = _tile(t)
    dff = wg_t.shape[0]

    def body(dh3_ref, dgu_ref, h2_ref, g2_ref, wg_ref, wu_ref, dh2_ref, dh2b_ref, dg2_ref):
        du2 = _dot(dgu_ref[:, :dff], wg_ref[...]) + _dot(dgu_ref[:, dff:], wu_ref[...])
        dx, dg = _rms_bwd(h2_ref[...], g2_ref[...], du2, D_MODEL)
        dh2 = dh3_ref[...] + dx
        dh2_ref[...] = dh2
        dh2b_ref[...] = dh2.astype(BF16)
        _acc(dg2_ref, dg, pl.program_id(0) == 0)

    return pl.pallas_call(
        body, name="ffn_bwd_b", grid=(t // tm,),
        in_specs=[_row(tm, D_MODEL), _row(tm, 2 * dff), _row(tm, D_MODEL), _const(g2.shape), _const(wg_t.shape),
                  _const(wu_t.shape)],
        out_specs=[_row(tm, D_MODEL), _row(tm, D_MODEL), _const((1, D_MODEL))],
        out_shape=[jax.ShapeDtypeStruct((t, D_MODEL), F32), jax.ShapeDtypeStruct((t, D_MODEL), BF16),
                   jax.ShapeDtypeStruct((1, D_MODEL), F32)],
        compiler_params=_params("arbitrary"),
    )(dh3, dgu, h2, g2, wg_t, wu_t)


def _mix_bwd(dh2, oa, ob, ga, gb, wo):
    t = dh2.shape[0]
    tm = _tile(t)

    def body(dh2_ref, oa_ref, ob_ref, ga_ref, gb_ref, wo_ref, doa_ref, dob_ref, dl_ref, dga_ref, dgb_ref):
        first = pl.program_id(0) == 0
        d = dh2_ref[...]
        ob_v = ob_ref[...]
        dxa, dga = _rms_bwd(oa_ref[...], ga_ref[...], _dot_nt(d, wo_ref[:SWA_Q_W, :]), SWA_Q_W)
        dxb, dgb = _rms_bwd(ob_v, gb_ref[...], _dot_nt(d, wo_ref[SWA_Q_W:, :]), MLA_OUT_W)
        lower = lax.broadcasted_iota(jnp.int32, (tm, LANE), 1) < HALF
        for hd in range(MLA_HEADS):
            sl = slice(LANE * (hd // 2), LANE * (hd // 2 + 1))
            mine = lower if hd % 2 == 0 else jnp.logical_not(lower)
            delta = jnp.sum(jnp.where(mine, ob_v[:, sl] * dxb[:, sl], 0.0), axis=1, keepdims=True)
            dl_ref[hd // MLA_HB, :, hd % MLA_HB:hd % MLA_HB + 1] = delta
        for ref, dx, heads in ((doa_ref, dxa, SWA_HEADS), (dob_ref, dxb, MLA_HEADS)):
            for hd in range(heads):
                slab = dx[:, LANE * (hd // 2):LANE * (hd // 2 + 1)]
                ref[:, LANE * hd:LANE * (hd + 1)] = _unpack_pair(slab, hd % 2).astype(BF16)
        _acc(dga_ref, dga, first)
        _acc(dgb_ref, dgb, first)

    return pl.pallas_call(
        body, name="mix_bwd", grid=(t // tm,),
        in_specs=[_row(tm, D_MODEL), _row(tm, SWA_Q_W), _row(tm, MLA_OUT_W), _const(ga.shape), _const(gb.shape),
                  _const(wo.shape)],
        out_specs=[_row(tm, HP), _row(tm, HP), pl.BlockSpec((MLA_HEADS // MLA_HB, tm, MLA_HB), lambda i: (0, i, 0)),
                   _const((1, SWA_Q_W)), _const((1, MLA_OUT_W))],
        out_shape=[jax.ShapeDtypeStruct((t, HP), BF16), jax.ShapeDtypeStruct((t, HP), BF16), _head_stats(t),
                   jax.ShapeDtypeStruct((1, SWA_Q_W), F32), jax.ShapeDtypeStruct((1, MLA_OUT_W), F32)],
        compiler_params=_params("arbitrary"),
    )(dh2, oa, ob, ga, gb, wo)


def _swa_bwd(sinks, q, k, v, o, do):
    t = q.shape[0]
    ts = _tile(t)

    def body(sink_ref, q_ref, kp_ref, kc_ref, vp_ref, vc_ref, o_ref, do_ref,
             dq_ref, dkc_ref, dkp_ref, dvc_ref, dvp_ref, dsink_ref):
        n = pl.program_id(0)
        chains = _swa_chains(t)
        qs, ks, probs = _swa_scores(sink_ref, q_ref, kp_ref, kc_ref, n, t)
        dos = [_swa_group(do_ref, slice(BLOCK * rb, BLOCK * (rb + 1)), j) for rb, j in chains]
        vs = [_swa_keys(vp_ref, vc_ref, rb, j) for rb, j in chains]
        dps = [_dot_nt(v2, do4) for do4, v2 in zip(dos, vs)]
        dss, dsks = [], []
        for (rb, j), (p, psink), do4, dp in zip(chains, probs, dos, dps):
            o4 = _swa_packed_group(o_ref, slice(BLOCK * rb, BLOCK * (rb + 1)), j)
            delta = jnp.sum(o4 * do4.astype(F32), axis=1, keepdims=True)
            delta = jnp.broadcast_to(delta, (SWA_GROUP * BLOCK, LANE)).T[:1, :]
            dss.append((p * (dp - delta) * SCALE_A).astype(BF16))
            dsks.append(-psink * delta)
        dqs = [_dot_tn(k2[:, :HALF], ds) for ds, k2 in zip(dss, ks)]
        dks = [_dot(ds, q4) for ds, q4 in zip(dss, qs)]
        dvs = [_dot(p.astype(BF16), do4) for (p, _), do4 in zip(probs, dos)]
        dsink = [jnp.zeros((1, LANE), F32)] * SWA_HEADS
        ext = {}
        for (rb, j), dq4, dk2, dv2, dsk in zip(chains, dqs, dks, dvs, dsks):
            for g in range(SWA_GROUP):
                hd = SWA_GROUP * j + g
                cols = slice(BLOCK * g, BLOCK * (g + 1))
                dq_ref[BLOCK * rb:BLOCK * (rb + 1), LANE * hd:LANE * (hd + 1)] = jnp.concatenate(
                    [dq4[:, cols], jnp.zeros((HALF, BLOCK), F32)], axis=0).T.astype(BF16)
                dsink[hd] = dsink[hd] + jnp.sum(dsk[:, cols], axis=1, keepdims=True)
            for half in range(2):
                key = (j, rb + half)
                part = (dk2[BLOCK * half:BLOCK * (half + 1)], dv2[BLOCK * half:BLOCK * (half + 1)])
                ext[key] = part if key not in ext else (ext[key][0] + part[0], ext[key][1] + part[1])
        for (j, blk), (dk, dv) in ext.items():
            sl = slice(LANE * j, LANE * (j + 1))
            if blk == 0:
                dkp_ref[:, sl] = dk
                dvp_ref[:, sl] = dv
            else:
                dkc_ref[BLOCK * (blk - 1):BLOCK * blk, sl] = dk
                dvc_ref[BLOCK * (blk - 1):BLOCK * blk, sl] = dv
        for hd in range(SWA_HEADS):
            _acc(dsink_ref.at[hd:hd + 1, :], jnp.broadcast_to(dsink[hd], (1, LANE)), n == 0)

    cur = lambda n: (n, 0)
    kv = pl.BlockSpec((ts, 2 * LANE), cur)
    kvp = pl.BlockSpec((BLOCK, 2 * LANE), cur)
    hp = pl.BlockSpec((ts, HP), cur)
    kvs = jax.ShapeDtypeStruct((t, 2 * LANE), F32)
    kvps = jax.ShapeDtypeStruct((t // ts * BLOCK, 2 * LANE), F32)
    return pl.pallas_call(
        body, name="swa_bwd", grid=(t // ts,),
        in_specs=_swa_specs(t) + [pl.BlockSpec((ts, SWA_Q_W), cur), hp],
        out_specs=[hp, kv, kvp, kv, kvp, _const((SWA_HEADS, LANE))],
        out_shape=[jax.ShapeDtypeStruct((t, HP), BF16), kvs, kvps, kvs, kvps,
                   jax.ShapeDtypeStruct((SWA_HEADS, LANE), F32)],
        compiler_params=_params("arbitrary"),
    )(sinks, q, k, k, v, v, o, do)


def _mla_bwd(q, k, v, do, lse, dl, slabs=()):
    t = q.shape[0]
    tq = _tile(t)
    nq = t // tq
    n = len(slabs)
    hb = MLA_HB
    steps = (MLA_HEADS // hb) * nq

    def body(k_ref, v_ref, q_ref, do_ref, lse_ref, dl_ref, *rest):
        in_refs, (dq_ref, dk_ref, dv_ref), out_refs = rest[:n], rest[n:n + 3], rest[n + 3:2 * n + 3]
        (dq_sc, dk_sc, dv_sc), sems = rest[2 * n + 3:2 * n + 6], rest[2 * n + 6:]
        j = pl.program_id(1)
        step_id = pl.program_id(0) * nq + j
        if n:
            plan = _exchange_plan(in_refs, out_refs, *sems)
            pl.when(step_id == 0)(plan.start)

        @pl.when(j == 0)
        def _():
            dq_sc[...] = jnp.zeros(dq_sc.shape, F32)

        dk_sc[...] = jnp.zeros(dk_sc.shape, F32)
        dv_sc[...] = jnp.zeros(dv_sc.shape, F32)
        def step(i, carry, masked):
            rows = pl.ds(pl.multiple_of(i * tq, tq), tq)
            ks, vs = _heads(k_ref, hb), _heads(v_ref, hb)
            qs, dos = _heads(q_ref, hb, rows), _heads(do_ref, hb, rows)
            ss = [_dot_nt(qh, kh) for qh, kh in zip(qs, ks)]
            dps = [_dot_nt(doh, vh) for doh, vh in zip(dos, vs)]
            if masked:
                mask = _causal_mask(i * tq, j * tq, tq, tq, False)
                ss = [jnp.where(mask, s_, NEG) for s_ in ss]
            ps = [jnp.exp2(s_ - lse_ref[rows, a:a + 1]) for a, s_ in enumerate(ss)]
            dss = [(p * (dp - dl_ref[rows, a:a + 1])).astype(BF16) for a, (p, dp) in enumerate(zip(ps, dps))]
            for a, (ds, p, qh, kh, doh) in enumerate(zip(dss, ps, qs, ks, dos)):
                dq_sc[a, rows, :] += _dot(ds, kh)
                dk_sc[a, :MLA_QK_DIM, :] += _dot_tn(qh[:, :MLA_QK_DIM], ds)
                dv_sc[a, :MLA_V_DIM, :] += _dot_tn(doh[:, :MLA_V_DIM], p.astype(BF16))
            return carry

        split = jnp.where(j == 0, nq, j + 1)
        lax.fori_loop(j, split, lambda i, c: step(i, c, True), 0)
        lax.fori_loop(split, nq, lambda i, c: step(i, c, False), 0)
        for a in range(hb):
            dk_ref[:, LANE * a:LANE * (a + 1)] = (dk_sc[a] * (1.0 / LOG2E)).T.astype(BF16)
            dv_ref[:, LANE * a:LANE * (a + 1)] = dv_sc[a].T.astype(BF16)

        @pl.when(j == nq - 1)
        def _():
            for a in range(hb):
                dq_ref[:, LANE * a:LANE * (a + 1)] = (dq_sc[a] * SCALE_B).astype(BF16)

        if n:
            pl.when(step_id == steps - 1)(plan.finish)

    blk = pl.BlockSpec((tq, hb * LANE), lambda h, j: (j, h))
    full = pl.BlockSpec((t, hb * LANE), lambda h, j: (0, h))
    cols = pl.BlockSpec((None, t, hb), lambda h, j: (h, 0, 0))
    out = pl.pallas_call(
        body, name="mla_bwd_exchange" if n else "mla_bwd", grid=(MLA_HEADS // hb, nq),
        in_specs=[blk, blk, full, full, cols, cols] + [ANY] * n, out_specs=[full, blk, blk] + [ANY] * n,
        out_shape=[jax.ShapeDtypeStruct((t, HP), BF16)] * 3 + [jax.ShapeDtypeStruct(a.shape, a.dtype) for a in slabs],
        scratch_shapes=[pltpu.VMEM((hb, t, LANE), F32)] + [pltpu.VMEM((hb, LANE, tq), F32)] * 2
        + (_comm_sems(n) if n else []),
        compiler_params=_params("arbitrary", "arbitrary"),
    )(k, v, q, do, lse, dl, *slabs)
    return out[:3], out[3:]


def _pre_bwd(dh2, h, cq, ckv, dqa, dka, dka_next, dva, dva_next, dqb, dkf, dvb, g1, win, gq, wqu, gkv, wkv, tabs, u, qn, kvn):
    t = h.shape[0]
    tm = _tile(t)
    keep = IN_KEEP_SWA + [(PO_KA + start, size) for start, size in IN_KEEP_REST]
    kept = sum(size for _, size in keep)

    def body(dh2_ref, h_ref, cq_ref, ckv_ref, dqa_ref, dka_ref, dkan_ref, dva_ref, dvan_ref, dqb_ref, dkf_ref, dvb_ref,
             g1_ref, win_ref, gq_ref, wqu_ref, gkv_ref, wkv_ref, tab_ref, u_ref, qn_ref, kvn_ref,
             dh_ref, dwin_ref, dwqu_ref, dwkv_ref, dg1_ref, dgq_ref, dgkv_ref,
             dp_ref, dqbo_ref, dkvo_ref, ain_sc, aqu_sc, akv_sc):
        first = pl.program_id(0) == 0
        ca, sa1, sa2, cb, sb1, sb2, ck = _tabs(tab_ref)
        dkr = jnp.zeros((tm, LANE), F32)
        for c in range(MLA_HEADS):
            sl = slice(LANE * c, LANE * (c + 1))
            dqbo_ref[:, sl] = _rope_t(dqb_ref[:, sl].astype(F32), cb, sb1, sb2, 16).astype(BF16)
            dkr += dkf_ref[:, sl].astype(F32)
        dkvo_ref[:, :HP] = dkf_ref[...]
        dkvo_ref[:, HP:] = dvb_ref[...]
        dcq, dgq = _rms_bwd(cq_ref[...], gq_ref[...], _dot(dqbo_ref[...], wqu_ref[...]), MLA_Q_RANK)
        dckv, dgkv = _rms_bwd(ckv_ref[...], gkv_ref[...], _dot(dkvo_ref[...], wkv_ref[...]), MLA_KV_RANK)
        for c in range(SWA_HEADS):
            sl = slice(LANE * c, LANE * (c + 1))
            dp_ref[:, PO_QA + LANE * c:PO_QA + LANE * (c + 1)] = _rope_t(dqa_ref[:, sl].astype(F32), ca, sa1, sa2,
                                                                          32).astype(BF16)
        last = slice(tm - BLOCK, tm)
        more = pl.program_id(0) < t // tm - 1
        for c in range(SWA_KV_HEADS):
            sl = slice(LANE * c, LANE * (c + 1))
            dk = dka_ref[:, sl]
            dk_last = dk[tm - BLOCK:] + jnp.where(more, dkan_ref[:, sl], 0.0)
            cols = slice(PO_KA + LANE * c, PO_KA + LANE * (c + 1))
            if tm > BLOCK:
                dp_ref[:tm - BLOCK, cols] = _rope_t(dk[:tm - BLOCK], ca[:tm - BLOCK], sa1[:tm - BLOCK], sa2[:tm - BLOCK],
                                                    32).astype(BF16)
            dp_ref[last, cols] = _rope_t(dk_last, ca[tm - BLOCK:], sa1[tm - BLOCK:], sa2[tm - BLOCK:], 32).astype(BF16)
        if tm > BLOCK:
            dp_ref[:tm - BLOCK, PO_VA:PO_CQ] = dva_ref[:tm - BLOCK, :].astype(BF16)
        dp_ref[last, PO_VA:PO_CQ] = (dva_ref[tm - BLOCK:, :] + jnp.where(more, dvan_ref[...], 0.0)).astype(BF16)
        dp_ref[:, PO_CQ:PO_CKV] = dcq.astype(BF16)
        dp_ref[:, PO_CKV:PO_KR] = dckv.astype(BF16)
        dp_ref[:, PO_KR:PW_IN] = _rope_t(dkr, ck, sb1, sb2, 16).astype(BF16)
        dx, dg1 = _rms_bwd(h_ref[...], g1_ref[...], _dot(dp_ref[...], win_ref[...]), D_MODEL)
        dh_ref[...] = dh2_ref[...] + dx
        _acc(dg1_ref, dg1, first)
        _acc(dgq_ref, dgq, first)
        _acc(dgkv_ref, dgkv, first)
        dpt = dp_ref[...].T
        _acc(ain_sc, _dot(jnp.concatenate([dpt[start:start + size] for start, size in keep], axis=0), u_ref[...]), first)
        _acc(aqu_sc, _dot_tn(dqbo_ref[...], qn_ref[...]), first)
        _acc(akv_sc, _dot_tn(kvn_ref[...], dkvo_ref[...]), first)

        @pl.when(pl.program_id(0) == t // tm - 1)
        def _():
            dwin_ref[...] = ain_sc[...].astype(BF16)
            dwqu_ref[...] = aqu_sc[...].astype(BF16)
            dwkv_ref[...] = akv_sc[...].astype(BF16)

    kv = _row(tm, 2 * LANE)
    nxt = pl.BlockSpec((BLOCK, 2 * LANE), lambda i: (jnp.minimum(i + 1, t // tm - 1), 0))
    return pl.pallas_call(
        body, name="pre_bwd", grid=(t // tm,),
        in_specs=[_row(tm, D_MODEL), _row(tm, D_MODEL), _row(tm, MLA_Q_RANK), _row(tm, MLA_KV_RANK), _row(tm, HP),
                  kv, nxt, kv, nxt, _row(tm, HP), _row(tm, HP), _row(tm, HP),
                  _const(g1.shape), _const(win.shape), _const(gq.shape), _const(wqu.shape), _const(gkv.shape),
                  _const(wkv.shape), _row(tm, N_TAB * LANE), _row(tm, D_MODEL), _row(tm, MLA_Q_RANK), _row(tm, MLA_KV_RANK)],
        out_specs=[_row(tm, D_MODEL), _const((kept, D_MODEL)), _const((HP, MLA_Q_RANK)), _const((MLA_KV_RANK, 2 * HP)),
                   _const((1, D_MODEL)), _const((1, MLA_Q_RANK)), _const((1, MLA_KV_RANK))],
        out_shape=[jax.ShapeDtypeStruct((t, D_MODEL), F32), jax.ShapeDtypeStruct((kept, D_MODEL), BF16),
                   jax.ShapeDtypeStruct((HP, MLA_Q_RANK), BF16), jax.ShapeDtypeStruct((MLA_KV_RANK, 2 * HP), BF16),
                   jax.ShapeDtypeStruct((1, D_MODEL), F32), jax.ShapeDtypeStruct((1, MLA_Q_RANK), F32),
                   jax.ShapeDtypeStruct((1, MLA_KV_RANK), F32)],
        scratch_shapes=[pltpu.VMEM((tm, PW_IN), BF16), pltpu.VMEM((tm, HP), BF16), pltpu.VMEM((tm, 2 * HP), BF16),
                        pltpu.VMEM((kept, D_MODEL), F32), pltpu.VMEM((HP, MLA_Q_RANK), F32),
                        pltpu.VMEM((MLA_KV_RANK, 2 * HP), F32)],
        compiler_params=_params("arbitrary"),
    )(dh2, h, cq, ckv, dqa, dka, dka_next, dva, dva_next, dqb, dkf, dvb, g1, win, gq, wqu, gkv, wkv, tabs, u, qn, kvn)


def _rope_tables(t):
    pos = (jnp.arange(t, dtype=jnp.int32) - FRONT).astype(F32)[:, None]
    lane = jnp.arange(LANE)[None, :]

    def table(dim, start):
        half = dim // 2
        inv = ROPE_THETA ** (-jnp.arange(0, dim, 2, dtype=F32) / dim)
        ang = pos * inv[None, :]
        cos = jnp.concatenate([jnp.cos(ang)] * 2, axis=1)
        sin = jnp.concatenate([jnp.sin(ang)] * 2, axis=1)
        pad = lambda a: jnp.pad(a, ((0, 0), (start, LANE - start - dim)))
        first = (lane >= start) & (lane < start + half)
        second = (lane >= start + half) & (lane < start + dim)
        return pad(cos), jnp.where(first, -pad(sin), 0.0), jnp.where(second, pad(sin), 0.0)

    ca, sa1, sa2 = table(SWA_HEAD_DIM, 0)
    ck, sb1, sb2 = table(MLA_ROPE_DIM, MLA_NOPE_DIM)
    cb = jnp.where(lane < MLA_NOPE_DIM, 1.0, ck)
    return jnp.concatenate([ca, sa1, sa2, cb, sb1, sb2, ck], axis=1)


def _pad_heads(w, heads, dim, axis):
    shp = w.shape
    w = w.reshape(shp[:axis] + (heads, dim) + shp[axis + 1:])
    pad = [(0, 0)] * w.ndim
    pad[axis + 1] = (0, LANE - dim)
    return jnp.pad(w, pad).reshape(shp[:axis] + (heads * LANE,) + shp[axis + 1:])


def _unpad_heads(w, heads, dim, axis):
    shp = w.shape
    w = w.reshape(shp[:axis] + (heads, LANE) + shp[axis + 1:])
    w = lax.slice_in_dim(w, 0, dim, axis=axis + 1)
    return w.reshape(shp[:axis] + (heads * dim,) + shp[axis + 1:])


def _pad_layer(w_in, w_q_up, w_kv_up):
    o1 = SWA_Q_W
    o2 = o1 + SWA_KV_W
    o3 = o2 + SWA_KV_W
    o4 = o3 + MLA_Q_RANK
    o5 = o4 + MLA_KV_RANK
    kr = jnp.pad(w_in[o5:], ((MLA_NOPE_DIM, LANE - MLA_QK_DIM), (0, 0)))
    win = jnp.concatenate([
        _pad_heads(w_in[:o1], SWA_HEADS, SWA_HEAD_DIM, 0),
        _pad_heads(w_in[o1:o2], SWA_KV_HEADS, SWA_HEAD_DIM, 0),
        _pad_heads(w_in[o2:o3], SWA_KV_HEADS, SWA_HEAD_DIM, 0),
        w_in[o3:o5], kr], axis=0)
    wqu = _pad_heads(w_q_up, MLA_HEADS, MLA_QK_DIM, 0)
    kv = w_kv_up.reshape(MLA_HEADS, MLA_NOPE_DIM + MLA_V_DIM, MLA_KV_RANK)
    wkv = jnp.concatenate([
        _pad_heads(kv[:, :MLA_NOPE_DIM].reshape(-1, MLA_KV_RANK), MLA_HEADS, MLA_NOPE_DIM, 0),
        _pad_heads(kv[:, MLA_NOPE_DIM:].reshape(-1, MLA_KV_RANK), MLA_HEADS, MLA_V_DIM, 0)], axis=0)
    return win, wqu, wkv


IN_KEEP_SWA = [(LANE * hd, SWA_HEAD_DIM) for hd in range(SWA_HEADS)]
IN_KEEP_REST = ([(LANE * hd, SWA_HEAD_DIM) for hd in range(2 * SWA_KV_HEADS)] + [(PO_CQ - PO_KA, PO_KR - PO_CQ)]
                + [(PO_KR - PO_KA + MLA_NOPE_DIM, MLA_ROPE_DIM)])


def _unpad_layer(d_w_in, dwqu, dwkv):
    d_w_q_up = _unpad_heads(dwqu, MLA_HEADS, MLA_QK_DIM, 0)
    dk = _unpad_heads(dwkv[:, :HP], MLA_HEADS, MLA_NOPE_DIM, 1).reshape(MLA_KV_RANK, MLA_HEADS, MLA_NOPE_DIM)
    dv = _unpad_heads(dwkv[:, HP:], MLA_HEADS, MLA_V_DIM, 1).reshape(MLA_KV_RANK, MLA_HEADS, MLA_V_DIM)
    d_w_kv_up = jnp.concatenate([dk, dv], axis=2).reshape(MLA_KV_RANK, -1).T
    return d_w_in, d_w_q_up, d_w_kv_up


def _train_example(x, target, meta, vec, weights):
    s = x.shape[0]
    depth = vec["attn_norm"].shape[0]
    t = FRONT + N_META + s
    assert t % BLOCK == 0
    tabs = _rope_tables(t)
    h = jnp.concatenate([jnp.zeros((FRONT, D_MODEL), F32), meta, x], axis=0)
    row = lambda v: v[None, :]

    saved = []
    for l in range(depth):
        win, wqu, wkv = _pad_layer(*weights.attn_in(l))
        g1, gq, gkv, g2, ga, gb = (row(vec[n][l]) for n in ("attn_norm", "q_norm", "kv_norm", "ffn_norm",
                                                            "out_norm_swa", "out_norm_mla"))
        sk = row(vec["sinks"][l])
        u, qa, ka, va, cq, ckv, qn, kvn, qb, kf, vb = _pre_fwd(h, g1, win, gq, wqu, gkv, wkv, tabs)
        oa = _swa_fwd(sk, qa, ka, va)
        ob, lse = weights.mla_fwd(l, qb, kf, vb)
        lse = jnp.moveaxis(lse.reshape(t, MLA_HEADS // MLA_HB, MLA_HB), 1, 0)
        wo = weights.w_o(l)
        h2, mix, u2 = _mix_fwd(h, oa, ob, ga, gb, wo, g2)
        wg, wu, wd = weights.ffn(l)
        h3, gt, up = weights.ffn_fwd(l, h2, u2)
        saved.append((h, u, qa, ka, va, cq, ckv, qn, kvn, qb, kf, vb, oa, ob, lse, h2, mix, u2, gt, up,
                      win, wqu, wkv, wo, ga, gb, g1, gq, gkv, g2, sk, wg, wu, wd))
        h = h3

    dh, d_final, loss = _loss_bwd(h, row(vec["final_norm"]), target)

    grads = []
    for l in reversed(range(depth)):
        (h0, u, qa, ka, va, cq, ckv, qn, kvn, qb, kf, vb, oa, ob, lse, h2, mix, u2, gt, up,
         win, wqu, wkv, wo, ga, gb, g1, gq, gkv, g2, sk, wg, wu, wd) = saved[l]
        dff = wd.shape[0]
        act, dgu, dhb = _ffn_bwd_a(dh, gt, up, wd)
        weights.ffn_grads(l, _tn_matmul(dgu, u2, "dw_gate", (0, dff)), _tn_matmul(dgu, u2, "dw_up", (dff, dff)),
                          _tn_matmul(act, dhb, "dw_down"))
        dh2, dh2b, d_g2 = _ffn_bwd_b(dh, dgu, h2, g2, wg, wu)
        weights.attn_grads(l, w_o=_tn_matmul(mix, dh2b, "dw_o"))
        doa, dob, dl, d_ga, d_gb = _mix_bwd(dh2b, oa, ob, ga, gb, wo)
        dqa, dkc, dkp, dvc, dvp, dsink = _swa_bwd(sk, qa, ka, va, oa, doa)
        dqb, dkf, dvb = weights.mla_bwd(l, qb, kf, vb, dob, lse, dl)
        dh, d_win, d_wqu, d_wkv, d_g1, d_gq, d_gkv = _pre_bwd(
            dh2, h0, cq, ckv, dqa, dkc, dkp, dvc, dvp, dqb, dkf, dvb,
            g1, win, gq, wqu, gkv, wkv, tabs, u, qn, kvn)
        weights.attn_grads(l, **dict(zip(ATTN_IN, _unpad_layer(d_win, d_wqu, d_wkv))))
        grads.append(dict(attn_norm=d_g1[0], q_norm=d_gq[0], kv_norm=d_gkv[0], sinks=dsink[:, 0], out_norm_swa=d_ga[0],
                          out_norm_mla=d_gb[0], ffn_norm=d_g2[0]))
    grads = grads[::-1]
    stacked = {k: jnp.stack([g[k] for g in grads]) for k in grads[0]}
    stacked["final_norm"] = d_final[0]
    return loss[0, 0], dh[FRONT + N_META:], dh[FRONT:FRONT + N_META], stacked


MESH = pl.DeviceIdType.MESH
ANY = pl.BlockSpec(memory_space=pl.ANY)


def _place():
    return lax.axis_index("x"), lax.axis_index("y"), lax.axis_index("c")


def _index(x, y, c):
    return 4 * x + 2 * y + c


def _comm_sems(n):
    return [pltpu.SemaphoreType.DMA((n, N_DEV - 1)), pltpu.SemaphoreType.DMA((n, N_DEV - 1)),
            pltpu.SemaphoreType.DMA((n,))]


class _gather_plan:
    def __init__(self, x_refs, out_refs, send_sems, recv_sems, local_sems):
        self.x_refs, self.out_refs = x_refs, out_refs
        self.send_sems, self.recv_sems, self.local_sems = send_sems, recv_sems, local_sems
        self.n = len(x_refs)

    def _where(self):
        x, y, c = _place()
        return (x, y, c), (x, y, 1 - c), [(1 - x, y), (x, 1 - y), (1 - x, 1 - y)], c

    def _copy(self, i, k, block, to, from_input=False):
        slot = self.out_refs[i].at[_index(*block)]
        return pltpu.make_async_remote_copy(
            src_ref=self.x_refs[i] if from_input else slot, dst_ref=slot,
            send_sem=self.send_sems.at[i, k], recv_sem=self.recv_sems.at[i, k], device_id=to, device_id_type=MESH)

    def _mine(self, i, me):
        return pltpu.make_async_copy(self.x_refs[i], self.out_refs[i].at[_index(*me)], self.local_sems.at[i])

    def _first(self, me, sibling, chips, c):
        out = [self._copy(i, 1 + j, me, (*chip, c), True) for j, chip in enumerate(chips) for i in range(self.n)]
        return out + [self._copy(i, 0, me, sibling, True) for i in range(self.n)]

    def start(self):
        me, sibling, chips, c = self._where()
        for i in range(self.n):
            self._mine(i, me).start()
        for cp in self._first(me, sibling, chips, c):
            cp.start()

    def forward(self):
        me, sibling, chips, c = self._where()
        for j, chip in enumerate(chips):
            for i in range(self.n):
                self._copy(i, 1 + j, (*chip, c), me).wait_recv()
                self._copy(i, 4 + j, (*chip, c), sibling).start()

    def finish(self):
        me, sibling, chips, c = self._where()
        for i in range(self.n):
            self._copy(i, 0, sibling, me).wait_recv()
            for j, chip in enumerate(chips):
                self._copy(i, 4 + j, (*chip, 1 - c), me).wait_recv()
        for cp in self._first(me, sibling, chips, c):
            cp.wait_send()
        for j, chip in enumerate(chips):
            for i in range(self.n):
                self._copy(i, 4 + j, (*chip, c), sibling).wait_send()
        for i in range(self.n):
            self._mine(i, me).wait()


class _exchange_plan:
    def __init__(self, in_refs, out_refs, send_sems, recv_sems, local_sems):
        self.in_refs, self.out_refs = in_refs, out_refs
        self.send_sems, self.recv_sems, self.local_sems = send_sems, recv_sems, local_sems
        self.n = len(in_refs)

    def _copies(self):
        x, y, c = _place()
        me = _index(x, y, c)
        mine = [pltpu.make_async_copy(self.in_refs[i].at[me], self.out_refs[i].at[me], self.local_sems.at[i])
                for i in range(self.n)]
        remote = []
        for k in range(1, N_DEV):
            peer = (1 - x if k & 4 else x, 1 - y if k & 2 else y, 1 - c if k & 1 else c)
            remote += [pltpu.make_async_remote_copy(
                src_ref=self.in_refs[i].at[_index(*peer)], dst_ref=self.out_refs[i].at[me],
                send_sem=self.send_sems.at[i, k - 1], recv_sem=self.recv_sems.at[i, k - 1],
                device_id=peer, device_id_type=MESH) for i in range(self.n)]
        return mine, remote

    def start(self):
        mine, remote = self._copies()
        for cp in mine + remote:
            cp.start()

    def finish(self):
        mine, remote = self._copies()
        for cp in remote:
            cp.wait_recv()
        for cp in remote:
            cp.wait_send()
        for cp in mine:
            cp.wait()


def _all_gather(shards, name):
    n = len(shards)

    def body(*refs):
        plan = _gather_plan(refs[:n], refs[n:2 * n], *refs[2 * n:])
        plan.start()
        plan.forward()
        plan.finish()

    return pl.pallas_call(
        body, name=name, in_specs=[ANY] * n, out_specs=[ANY] * n, scratch_shapes=_comm_sems(n),
        out_shape=[jax.ShapeDtypeStruct((N_DEV,) + a.shape, a.dtype) for a in shards],
    )(*shards)


def _exchange(slabs, name):
    n = len(slabs)

    def body(*refs):
        plan = _exchange_plan(refs[:n], refs[n:2 * n], *refs[2 * n:])
        plan.start()
        plan.finish()

    return pl.pallas_call(
        body, name=name, in_specs=[ANY] * n, out_specs=[ANY] * n, scratch_shapes=_comm_sems(n),
        out_shape=[jax.ShapeDtypeStruct(a.shape, a.dtype) for a in slabs],
    )(*slabs)


def _adamw(w, g, m, v):
    m = ADAM_B1 * m + (1.0 - ADAM_B1) * g
    v = ADAM_B2 * v + (1.0 - ADAM_B2) * (g * g)
    m_hat = m / (1.0 - ADAM_B1 ** ADAM_STEP)
    v_hat = v / (1.0 - ADAM_B2 ** ADAM_STEP)
    return -ADAM_LR * (m_hat / (jnp.sqrt(v_hat) + ADAM_EPS) + ADAM_WD * w), m, v


def _sum_slots(ref):
    g = ref[0].astype(F32)
    for s in range(1, N_DEV):
        g = g + ref[s].astype(F32)
    return g


def _reduce_adamw(parts, w, m, v, name):
    l, r, c = w.shape
    tile = max([d for d in range(16, ADAM_ROWS + 1, 16) if r % d == 0], default=r)
    last = r // tile - 1

    def body(*refs):
        p_refs, (w_ref, m_ref, v_ref), (g_ref, d_ref, nm_ref, nv_ref) = refs[:l], refs[l:l + 3], refs[l + 3:]
        for layer in range(l):
            @pl.when(pl.program_id(0) == layer)
            def _(p_ref=p_refs[layer]):
                g = _sum_slots(p_ref)
                g_ref[...] = g
                d_ref[...], nm_ref[...], nv_ref[...] = _adamw(w_ref[...], g, m_ref[...], v_ref[...])

    def part_spec(layer):
        return pl.BlockSpec((N_DEV, tile, c),
                            lambda i, j: (0, jnp.where(i == layer, j, jnp.where(i < layer, 0, last)), 0))

    blk = pl.BlockSpec((None, tile, c), lambda i, j: (i, j, 0))
    return pl.pallas_call(
        body, name=name, grid=(l, r // tile),
        in_specs=[part_spec(layer) for layer in range(l)] + [blk, blk, blk], out_specs=[blk] * 4,
        out_shape=[jax.ShapeDtypeStruct((l, r, c), F32)] * 4,
        compiler_params=_params("arbitrary", "arbitrary"),
    )(*parts, w, m, v)


def _sum_parts(parts, name):
    _, r, c = parts.shape

    def body(p_ref, g_ref):
        g_ref[...] = _sum_slots(p_ref)

    return pl.pallas_call(body, name=name, out_shape=jax.ShapeDtypeStruct((r, c), F32))(parts)


def _adamw_call(w, g, m, v, name):
    def body(w_ref, g_ref, m_ref, v_ref, d_ref, nm_ref, nv_ref):
        d_ref[...], nm_ref[...], nv_ref[...] = _adamw(w_ref[...], g_ref[...], m_ref[...], v_ref[...])

    return pl.pallas_call(body, name=name, out_shape=[jax.ShapeDtypeStruct(w.shape, F32)] * 3)(w, g, m, v)


ATTN_IN = ("w_in", "w_q_up", "w_kv_up")
ATTN = ATTN_IN + ("w_o",)
FFN = ("w_gate", "w_up", "w_down")
TRANSPOSED = ("w_in", "w_q_up", "w_kv_up", "w_gate", "w_up")
SMALL = ("attn_norm", "ffn_norm", "final_norm", "out_norm_swa", "out_norm_mla", "q_norm", "kv_norm", "sinks")
PACK_W = 1024
SMALL_ROWS = 16


def _pack(arrs, dtype):
    flat = jnp.concatenate([a.astype(dtype).reshape(-1) for a in arrs])
    return flat.reshape(-1, PACK_W)


def _unpack(packed, like):
    flat = packed.reshape(-1)
    out, off = [], 0
    for a in like:
        out.append(flat[off:off + a.size].reshape(a.shape))
        off += a.size
    return out


def _gather_to_full(gathered):
    return gathered.reshape((-1,) + gathered.shape[2:])


def _full_to_slabs(full):
    return full.reshape((N_DEV, -1) + full.shape[1:])


class _ShardedWeights:
    def __init__(self, shards, depth, meta_shard):
        self.shards, self.depth = shards, depth
        self.gathered, self.pending, self.parts = {}, {}, {}
        first = _all_gather([shards[n][0] for n in ATTN_IN] + [meta_shard], "gather_attn0")
        self.gathered.update(zip([(n, 0) for n in ATTN_IN], first))
        self.meta = jnp.moveaxis(first[-1], 0, 1).reshape(N_META, D_MODEL)

    def _gather(self, keys, run):
        self.gathered.update(zip(keys, run([self.shards[n][l] for n, l in keys])))

    def _full(self, names, l):
        return tuple(_gather_to_full(self.gathered[n, l]) for n in names)

    def attn_in(self, l):
        return self._full(ATTN_IN, l)

    def w_o(self, l):
        return self._full(("w_o",), l)[0]

    def ffn(self, l):
        return self._full(FFN, l)

    def mla_fwd(self, l, q, k, v):
        out = []
        self._gather([(n, l) for n in ("w_o",) + FFN], lambda xs: out.extend(_mla_fwd(q, k, v, xs)) or out[2])
        return out[0], out[1]

    def ffn_fwd(self, l, h2, u2):
        keys = [(n, l + 1) for n in ATTN_IN] if l + 1 < self.depth else []
        out = []
        self._gather(keys, lambda xs: out.extend(_ffn_fwd(h2, u2, *self.ffn(l), xs)) or out[1])
        return out[0]

    def _add(self, names, l, grads):
        for n, g in zip(names, grads):
            self.pending[n, l] = _full_to_slabs(g)

    def ffn_grads(self, l, *grads):
        self._add(FFN, l, grads)

    def attn_grads(self, l, **grads):
        self._add(list(grads), l, grads.values())

    def _exchange(self, run):
        keys = list(self.pending)
        self.parts.update(zip(keys, run([self.pending.pop(k) for k in keys])))

    def mla_bwd(self, l, *args):
        out = []
        self._exchange(lambda xs: out.extend(_mla_bwd(*args, xs)) or out[1])
        return out[0]

    def flush(self):
        self._exchange(lambda xs: _exchange(xs, "exchange_attn0"))


def kernel(x, meta_tokens, attn_norm, w_in, q_norm, w_q_up, kv_norm, w_kv_up, sinks, out_norm_swa, out_norm_mla, w_o, ffn_norm, w_gate, w_up, w_down, final_norm, loss_target, m_meta_tokens, m_attn_norm, m_w_in, m_q_norm, m_w_q_up, m_kv_norm, m_w_kv_up, m_sinks, m_out_norm_swa, m_out_norm_mla, m_w_o, m_ffn_norm, m_w_gate, m_w_up, m_w_down, m_final_norm, v_meta_tokens, v_attn_norm, v_w_in, v_q_norm, v_w_q_up, v_kv_norm, v_w_kv_up, v_sinks, v_out_norm_swa, v_out_norm_mla, v_w_o, v_ffn_norm, v_w_gate, v_w_up, v_w_down, v_final_norm):
    w = dict(meta_tokens=meta_tokens, attn_norm=attn_norm, w_in=w_in, q_norm=q_norm, w_q_up=w_q_up, kv_norm=kv_norm,
             w_kv_up=w_kv_up, sinks=sinks, out_norm_swa=out_norm_swa, out_norm_mla=out_norm_mla, w_o=w_o,
             ffn_norm=ffn_norm, w_gate=w_gate, w_up=w_up, w_down=w_down, final_norm=final_norm)
    m = dict(meta_tokens=m_meta_tokens, attn_norm=m_attn_norm, w_in=m_w_in, q_norm=m_q_norm, w_q_up=m_w_q_up,
             kv_norm=m_kv_norm, w_kv_up=m_w_kv_up, sinks=m_sinks, out_norm_swa=m_out_norm_swa,
             out_norm_mla=m_out_norm_mla, w_o=m_w_o, ffn_norm=m_ffn_norm, w_gate=m_w_gate, w_up=m_w_up,
             w_down=m_w_down, final_norm=m_final_norm)
    v = dict(meta_tokens=v_meta_tokens, attn_norm=v_attn_norm, w_in=v_w_in, q_norm=v_q_norm, w_q_up=v_w_q_up,
             kv_norm=v_kv_norm, w_kv_up=v_w_kv_up, sinks=v_sinks, out_norm_swa=v_out_norm_swa,
             out_norm_mla=v_out_norm_mla, w_o=v_w_o, ffn_norm=v_ffn_norm, w_gate=v_w_gate, w_up=v_w_up,
             w_down=v_w_down, final_norm=v_final_norm)
    names = list(w)
    big = ATTN + FFN
    depth = w_in.shape[0]
    me = _index(*_place())

    as_held = lambda n, a: jnp.swapaxes(a, 1, 2) if n in TRANSPOSED else a
    weights = _ShardedWeights({n: as_held(n, w[n]).astype(BF16) for n in big}, depth, meta_tokens)
    loss, grad_x, d_meta, grads = _train_example(x[0], loss_target[0], weights.meta, {n: w[n] for n in SMALL}, weights)
    weights.flush()

    g_big, d_big, m_big, v_big = {}, {}, {}, {}
    for n in big:
        held = [as_held(n, a) for a in (w[n], m[n], v[n])]
        outs = _reduce_adamw([weights.parts[n, l] for l in range(depth)], *held, "reduce_adamw_" + n)
        g_big[n], d_big[n], m_big[n], v_big[n] = [as_held(n, a) for a in outs]

    small = [grads[n] for n in SMALL] + [loss.reshape(1)]
    pad = SMALL_ROWS * PACK_W - sum(a.size for a in small)
    part = jnp.concatenate([_pack(small + [jnp.zeros((pad,), F32)], F32), d_meta], axis=0)
    total = _sum_parts(_all_gather([part], "gather_small")[0], "sum_small")
    small_w = [w[n] for n in SMALL]
    packs = [_pack([d[n] for n in SMALL] + [jnp.zeros((pad + 1,), F32)], F32) for d in (w, m, v)]
    upd = _adamw_call(packs[0], total[:SMALL_ROWS], packs[1], packs[2], "adamw_small")
    g_small, d_small, m_small, v_small = [dict(zip(SMALL, _unpack(p, small_w))) for p in (total[:SMALL_ROWS],) + tuple(upd)]
    loss_total = total[:SMALL_ROWS].reshape(-1)[SMALL_ROWS * PACK_W - pad - 1]
    g_meta = lax.dynamic_slice_in_dim(total[SMALL_ROWS:], me * LANE, LANE, axis=1)
    d_mt, m_mt, v_mt = _adamw_call(meta_tokens, g_meta, m_meta_tokens, v_meta_tokens, "adamw_meta")

    outs = []
    for got in ({**g_big, **g_small, "meta_tokens": g_meta}, {**d_big, **d_small, "meta_tokens": d_mt},
                {**m_big, **m_small, "meta_tokens": m_mt}, {**v_big, **v_small, "meta_tokens": v_mt}):
        outs += [got[n] for n in names]
    return (loss_total, grad_x[None], *outs)
```
